```python
import math
import jax, jax.numpy as jnp
from jax import lax
import numpy as np

D_MODEL = 1024
BATCH = 8
SEQ = 4096
DEPTH = 1

ATTN_HEAD_DIM = 64
ATTN_WIDTH = D_MODEL // 2
ATTN_HEADS = ATTN_WIDTH // ATTN_HEAD_DIM
DILATED_PATTERNS = ((128, 1), (512, 4), (2048, 16))
ATTN_BLOCK = 128
HGRN_HEAD_DIM = 128
HGRN_WIDTH = D_MODEL - ATTN_WIDTH
HGRN_HEADS = HGRN_WIDTH // HGRN_HEAD_DIM
HGRN_CHUNK = 64
MIX_WIDTH = ATTN_WIDTH + HGRN_WIDTH
IN_PROJ_WIDTH = 3 * ATTN_WIDTH + 4 * HGRN_WIDTH
D_FF = 4 * D_MODEL
RMS_EPS = 1e-6

kernel_name = "hymba_dilated_attn_hgrn2_sqrelu"


def rmsnorm(x, gain):
    xf = x.astype(jnp.float32)
    y = xf * lax.rsqrt(jnp.mean(xf * xf, axis=-1, keepdims=True) + RMS_EPS)
    return (y * gain.astype(jnp.float32)).astype(x.dtype)


def alibi_slopes(n_heads):
    return jnp.exp2(-8.0 * jnp.arange(1, n_heads + 1, dtype=jnp.float32) / n_heads)


def dilated_branch(q, k, v, slopes, window, dilation):
    B, S, H, Dh = q.shape
    span = window // dilation
    L = S // dilation
    nb = -(-L // ATTN_BLOCK)
    Lp = nb * ATTN_BLOCK
    Bd = B * dilation

    def to_sub(t):
        t = t.reshape(B, L, dilation, H, Dh).transpose(0, 2, 1, 3, 4).reshape(Bd, L, H, Dh)
        t = jnp.pad(t, ((0, 0), (0, Lp - L), (0, 0), (0, 0)))
        return t.reshape(Bd, nb, ATTN_BLOCK, H, Dh)

    def with_prev(t):
        prev = jnp.pad(t[:, :-1], ((0, 0), (1, 0), (0, 0), (0, 0), (0, 0)))
        return jnp.concatenate([prev, t], axis=2)

    qb = to_sub(q)
    kk = with_prev(to_sub(k))
    vv = with_prev(to_sub(v))
    s = jnp.einsum('bnqhd,bnkhd->bnhqk', qb, kk).astype(jnp.float32) * (Dh ** -0.5)
    qi = jnp.arange(ATTN_BLOCK)[:, None] + ATTN_BLOCK
    kj = jnp.arange(2 * ATTN_BLOCK)[None, :]
    dist = qi - kj
    blk = jnp.arange(nb)[:, None, None]
    valid = (dist >= 0) & (dist <= span) & (blk * ATTN_BLOCK + kj - ATTN_BLOCK >= 0)
    bias = -slopes[:, None, None] * (dist * dilation).astype(jnp.float32)
    s = jnp.where(valid[None, :, None], s + bias[None, None], -jnp.inf)
    m = jnp.max(s, axis=-1, keepdims=True)
    p = jnp.exp(s - m)
    den = jnp.sum(p, axis=-1)
    o = jnp.einsum('bnhqk,bnkhd->bnqhd', p, vv.astype(jnp.float32))
    o = o / jnp.transpose(den, (0, 1, 3, 2))[..., None]
    lse = jnp.transpose(m[..., 0] + jnp.log(den), (0, 1, 3, 2))

    def from_sub(t):
        t = t.reshape((Bd, Lp) + t.shape[3:])[:, :L]
        t = t.reshape((B, dilation, L) + t.shape[2:])
        t = jnp.swapaxes(t, 1, 2)
        return t.reshape((B, S) + t.shape[3:])

    return from_sub(o), from_sub(lse)


def dilated_mixture_attention(q, k, v, slopes):
    outs, lses = [], []
    for window, dilation in DILATED_PATTERNS:
        o, lse = dilated_branch(q, k, v, slopes, window, dilation)
        outs.append(o)
        lses.append(lse)
    w = jax.nn.softmax(jnp.stack(lses, axis=0), axis=0)
    o = jnp.sum(w[..., None] * jnp.stack(outs, axis=0), axis=0)
    return o.astype(q.dtype)


def hgrn2_recurrence(q, f_pre, i, gate, lower_bound, out_gain):
    B, S, _ = q.shape
    H, D, C = HGRN_HEADS, HGRN_HEAD_DIM, HGRN_CHUNK
    shape = (B, S, H, D)
    lb = lower_bound.reshape(H, D)
    qf = jax.nn.silu(q.astype(jnp.float32)).reshape(shape)
    log_f = jnp.logaddexp(jnp.log(lb), jnp.log1p(-lb) + jax.nn.log_sigmoid(f_pre.astype(jnp.float32).reshape(shape)))
    kf = -jnp.expm1(log_f)
    vf = i.astype(jnp.float32).reshape(shape)
    nC = S // C

    def to_chunks(t):
        return t.reshape(B, nC, C, H, D).transpose(1, 0, 3, 2, 4)

    causal = jnp.tril(jnp.ones((C, C), dtype=bool))[:, :, None]

    def step(state, xs):
        qc, kc, vc, gc = xs
        b = jnp.cumsum(gc, axis=2)
        o_inter = jnp.einsum('bhtk,bhkv->bhtv', qc * jnp.exp(b), state)
        diff = b[:, :, :, None, :] - b[:, :, None, :, :]
        decay = jnp.exp(jnp.where(causal, diff, -jnp.inf))
        a = jnp.einsum('bhtk,bhsk,bhtsk->bhts', qc, kc, decay)
        o = o_inter + jnp.einsum('bhts,bhsv->bhtv', a, vc)
        b_last = b[:, :, -1:, :]
        new_state = jnp.exp(b_last[:, :, 0, :, None]) * state + jnp.einsum(
            'bhsk,bhsv->bhkv', kc * jnp.exp(b_last - b), vc)
        return new_state, o

    state0 = jnp.zeros((B, H, D, D), jnp.float32)
    _, o = lax.scan(step, state0, (to_chunks(qf), to_chunks(kf), to_chunks(vf), to_chunks(log_f)))
    o = o.transpose(1, 0, 3, 2, 4).reshape(shape)
    o = rmsnorm(o, out_gain.reshape(H, D)).reshape(B, S, H * D)
    return (o * jax.nn.silu(gate.astype(jnp.float32))).astype(q.dtype)


def _fwd_setup_inputs(seed: int = 0) -> dict:
    key = jax.random.key(seed)
    ks = jax.random.split(key, 12)
    f32 = jnp.float32

    def gain(k, shape):
        return 1.0 + 0.02 * jax.random.normal(k, shape, f32)

    return {
        "x": jax.random.normal(ks[0], (BATCH, SEQ, D_MODEL), f32),
        "mix_pre_norm": gain(ks[1], (DEPTH, D_MODEL)),
        "w_in": jax.random.normal(ks[2], (DEPTH, D_MODEL, IN_PROJ_WIDTH), f32) * D_MODEL ** -0.5,
        "attn_out_norm": gain(ks[3], (DEPTH, ATTN_WIDTH)),
        "hgrn_lb_logits": 0.5 * jax.random.normal(ks[4], (DEPTH + 1, HGRN_WIDTH), f32),
        "hgrn_out_norm": gain(ks[5], (DEPTH, HGRN_WIDTH)),
        "w_out": jax.random.normal(ks[6], (DEPTH, MIX_WIDTH, D_MODEL), f32) * MIX_WIDTH ** -0.5,
        "mix_post_norm": gain(ks[7], (DEPTH, D_MODEL)),
        "mlp_pre_norm": gain(ks[8], (DEPTH, D_MODEL)),
        "w_ff1": jax.random.normal(ks[9], (DEPTH, D_MODEL, D_FF), f32) * D_MODEL ** -0.5,
        "w_ff2": jax.random.normal(ks[10], (DEPTH, D_FF, D_MODEL), f32) * D_FF ** -0.5,
        "mlp_post_norm": gain(ks[11], (DEPTH, D_MODEL)),
    }


def _fwd_reference(x, mix_pre_norm, w_in, attn_out_norm, hgrn_lb_logits, hgrn_out_norm, w_out,
              mix_post_norm, mlp_pre_norm, w_ff1, w_ff2, mlp_post_norm):
    B, S, _ = x.shape
    slopes = alibi_slopes(ATTN_HEADS)
    lower_bounds = jnp.cumsum(jax.nn.softmax(hgrn_lb_logits.astype(jnp.float32), axis=0), axis=0)
    aw, hw = ATTN_WIDTH, HGRN_WIDTH
    splits = [aw, 2 * aw, 3 * aw, 3 * aw + hw, 3 * aw + 2 * hw, 3 * aw + 3 * hw]
    for layer in range(DEPTH):
        h = rmsnorm(x, mix_pre_norm[layer])
        proj = h @ w_in[layer]
        q_a, k_a, v_a, q_h, f_h, i_h, g_h = jnp.split(proj, splits, axis=-1)
        ahead = (B, S, ATTN_HEADS, ATTN_HEAD_DIM)
        attn = dilated_mixture_attention(q_a.reshape(ahead), k_a.reshape(ahead), v_a.reshape(ahead), slopes)
        attn = rmsnorm(attn.reshape(B, S, ATTN_WIDTH), attn_out_norm[layer])
        rec = hgrn2_recurrence(q_h, f_h, i_h, g_h, lower_bounds[layer], hgrn_out_norm[layer])
        mixed = jnp.concatenate([attn, rec], axis=-1) @ w_out[layer]
        x = x + rmsnorm(mixed, mix_post_norm[layer])
        h = rmsnorm(x, mlp_pre_norm[layer])
        ff = jnp.square(jax.nn.relu(h @ w_ff1[layer])) @ w_ff2[layer]
        x = x + rmsnorm(ff, mlp_post_norm[layer])
    return x


import jax as _jax
import jax.numpy as _jnp

TWIN_FORMAT = 'train_step'
FWD_PARAMS = ['x', 'mix_pre_norm', 'w_in', 'attn_out_norm', 'hgrn_lb_logits', 'hgrn_out_norm', 'w_out', 'mix_post_norm', 'mlp_pre_norm', 'w_ff1', 'w_ff2', 'mlp_post_norm']
TWIN_WEIGHTS = ['mix_pre_norm', 'w_in', 'attn_out_norm', 'hgrn_lb_logits', 'hgrn_out_norm', 'w_out', 'mix_post_norm', 'mlp_pre_norm', 'w_ff1', 'w_ff2', 'mlp_post_norm']
TWIN_DIFF_INPUT = 'x'
TWIN_INPUTS = ['x', 'mix_pre_norm', 'w_in', 'attn_out_norm', 'hgrn_lb_logits', 'hgrn_out_norm', 'w_out', 'mix_post_norm', 'mlp_pre_norm', 'w_ff1', 'w_ff2', 'mlp_post_norm', 'loss_target', 'm_mix_pre_norm', 'm_w_in', 'm_attn_out_norm', 'm_hgrn_lb_logits', 'm_hgrn_out_norm', 'm_w_out', 'm_mix_post_norm', 'm_mlp_pre_norm', 'm_w_ff1', 'm_w_ff2', 'm_mlp_post_norm', 'v_mix_pre_norm', 'v_w_in', 'v_attn_out_norm', 'v_hgrn_lb_logits', 'v_hgrn_out_norm', 'v_w_out', 'v_mix_post_norm', 'v_mlp_pre_norm', 'v_w_ff1', 'v_w_ff2', 'v_mlp_post_norm']
TWIN_OUTPUTS = ['loss', 'grad_x', 'grad_mix_pre_norm', 'grad_w_in', 'grad_attn_out_norm', 'grad_hgrn_lb_logits', 'grad_hgrn_out_norm', 'grad_w_out', 'grad_mix_post_norm', 'grad_mlp_pre_norm', 'grad_w_ff1', 'grad_w_ff2', 'grad_mlp_post_norm', 'delta_mix_pre_norm', 'delta_w_in', 'delta_attn_out_norm', 'delta_hgrn_lb_logits', 'delta_hgrn_out_norm', 'delta_w_out', 'delta_mix_post_norm', 'delta_mlp_pre_norm', 'delta_w_ff1', 'delta_w_ff2', 'delta_mlp_post_norm', 'new_m_mix_pre_norm', 'new_m_w_in', 'new_m_attn_out_norm', 'new_m_hgrn_lb_logits', 'new_m_hgrn_out_norm', 'new_m_w_out', 'new_m_mix_post_norm', 'new_m_mlp_pre_norm', 'new_m_w_ff1', 'new_m_w_ff2', 'new_m_mlp_post_norm', 'new_v_mix_pre_norm', 'new_v_w_in', 'new_v_attn_out_norm', 'new_v_hgrn_lb_logits', 'new_v_hgrn_out_norm', 'new_v_w_out', 'new_v_mix_post_norm', 'new_v_mlp_pre_norm', 'new_v_w_ff1', 'new_v_w_ff2', 'new_v_mlp_post_norm']
TWIN_LEAF_KINDS = {'loss': 'loss', 'grad_x': 'grad_x', 'grad_mix_pre_norm': 'grad_w', 'grad_w_in': 'grad_w', 'grad_attn_out_norm': 'grad_w', 'grad_hgrn_lb_logits': 'grad_w', 'grad_hgrn_out_norm': 'grad_w', 'grad_w_out': 'grad_w', 'grad_mix_post_norm': 'grad_w', 'grad_mlp_pre_norm': 'grad_w', 'grad_w_ff1': 'grad_w', 'grad_w_ff2': 'grad_w', 'grad_mlp_post_norm': 'grad_w', 'delta_mix_pre_norm': 'delta_w', 'delta_w_in': 'delta_w', 'delta_attn_out_norm': 'delta_w', 'delta_hgrn_lb_logits': 'delta_w', 'delta_hgrn_out_norm': 'delta_w', 'delta_w_out': 'delta_w', 'delta_mix_post_norm': 'delta_w', 'delta_mlp_pre_norm': 'delta_w', 'delta_w_ff1': 'delta_w', 'delta_w_ff2': 'delta_w', 'delta_mlp_post_norm': 'delta_w', 'new_m_mix_pre_norm': 'new_m', 'new_m_w_in': 'new_m', 'new_m_attn_out_norm': 'new_m', 'new_m_hgrn_lb_logits': 'new_m', 'new_m_hgrn_out_norm': 'new_m', 'new_m_w_out': 'new_m', 'new_m_mix_post_norm': 'new_m', 'new_m_mlp_pre_norm': 'new_m', 'new_m_w_ff1': 'new_m', 'new_m_w_ff2': 'new_m', 'new_m_mlp_post_norm': 'new_m', 'new_v_mix_pre_norm': 'new_v', 'new_v_w_in': 'new_v', 'new_v_attn_out_norm': 'new_v', 'new_v_hgrn_lb_logits': 'new_v', 'new_v_hgrn_out_norm': 'new_v', 'new_v_w_out': 'new_v', 'new_v_mix_post_norm': 'new_v', 'new_v_mlp_pre_norm': 'new_v', 'new_v_w_ff1': 'new_v', 'new_v_w_ff2': 'new_v', 'new_v_mlp_post_norm': 'new_v'}


def _forward(args):
    return _fwd_reference(*[args[k] for k in FWD_PARAMS])


def _output_shape():
    out = _jax.eval_shape(lambda: _forward(_fwd_setup_inputs(0)))
    return out.shape, out.dtype

N_MICROBATCH = 1
ADAM_LR = 0.001
ADAM_B1 = 0.9
ADAM_B2 = 0.999
ADAM_EPS = 1e-08
ADAM_WD = 0.01
ADAM_STEP = 10
PER_EXAMPLE_BATCH_AXIS = {'x': 0, 'loss_target': 0}
SHARED_INPUTS = []
_WEIGHT_DTYPES = {'mix_pre_norm': _jnp.float32, 'w_in': _jnp.float32, 'attn_out_norm': _jnp.float32, 'hgrn_lb_logits': _jnp.float32, 'hgrn_out_norm': _jnp.float32, 'w_out': _jnp.float32, 'mix_post_norm': _jnp.float32, 'mlp_pre_norm': _jnp.float32, 'w_ff1': _jnp.float32, 'w_ff2': _jnp.float32, 'mlp_post_norm': _jnp.float32}
MOMENT_SCALE = {'mix_pre_norm': 8.214749e-01, 'w_in': 3.903404e-01, 'attn_out_norm': 1.476125e+00, 'hgrn_lb_logits': 2.449489e-02, 'hgrn_out_norm': 4.466819e-01, 'w_out': 9.089856e-01, 'mix_post_norm': 3.198259e+01, 'mlp_pre_norm': 8.836302e-01, 'w_ff1': 4.485693e-01, 'w_ff2': 9.126706e-01, 'mlp_post_norm': 3.283096e+01}


def _to_microbatches(a, axis):
    t = _jnp.moveaxis(a, axis, 0)
    t = t.reshape((N_MICROBATCH, t.shape[0] // N_MICROBATCH) + t.shape[1:])
    return _jnp.moveaxis(t, 1, axis + 1)


def setup_inputs(seed: int = 0) -> dict:
    inp = _fwd_setup_inputs(seed)
    key = _jax.random.fold_in(_jax.random.key(seed), 7919)
    shape, _ = _output_shape()
    out = dict(inp)
    out["loss_target"] = _jax.random.normal(_jax.random.fold_in(key, 0), shape, _jnp.float32)
    for i, name in enumerate(TWIN_WEIGHTS):
        w = inp[name].astype(_jnp.float32)
        if MOMENT_SCALE is None:
            s = _jnp.sqrt(_jnp.mean(_jnp.square(w)) + 1e-30)
        else:
            s = MOMENT_SCALE[name]
        km, kv = _jax.random.split(_jax.random.fold_in(key, i + 1))
        out[name] = w
        out["m_" + name] = s * _jax.random.normal(km, w.shape, _jnp.float32)
        out["v_" + name] = (s * s) * _jax.random.uniform(kv, w.shape, _jnp.float32, 0.5, 1.5)
    if N_MICROBATCH > 1:
        for name, axis in PER_EXAMPLE_BATCH_AXIS.items():
            out[name] = _to_microbatches(out[name], axis)
    return {'x': out['x'], 'mix_pre_norm': out['mix_pre_norm'], 'w_in': out['w_in'], 'attn_out_norm': out['attn_out_norm'], 'hgrn_lb_logits': out['hgrn_lb_logits'], 'hgrn_out_norm': out['hgrn_out_norm'], 'w_out': out['w_out'], 'mix_post_norm': out['mix_post_norm'], 'mlp_pre_norm': out['mlp_pre_norm'], 'w_ff1': out['w_ff1'], 'w_ff2': out['w_ff2'], 'mlp_post_norm': out['mlp_post_norm'], 'loss_target': out['loss_target'], 'm_mix_pre_norm': out['m_mix_pre_norm'], 'm_w_in': out['m_w_in'], 'm_attn_out_norm': out['m_attn_out_norm'], 'm_hgrn_lb_logits': out['m_hgrn_lb_logits'], 'm_hgrn_out_norm': out['m_hgrn_out_norm'], 'm_w_out': out['m_w_out'], 'm_mix_post_norm': out['m_mix_post_norm'], 'm_mlp_pre_norm': out['m_mlp_pre_norm'], 'm_w_ff1': out['m_w_ff1'], 'm_w_ff2': out['m_w_ff2'], 'm_mlp_post_norm': out['m_mlp_post_norm'], 'v_mix_pre_norm': out['v_mix_pre_norm'], 'v_w_in': out['v_w_in'], 'v_attn_out_norm': out['v_attn_out_norm'], 'v_hgrn_lb_logits': out['v_hgrn_lb_logits'], 'v_hgrn_out_norm': out['v_hgrn_out_norm'], 'v_w_out': out['v_w_out'], 'v_mix_post_norm': out['v_mix_post_norm'], 'v_mlp_pre_norm': out['v_mlp_pre_norm'], 'v_w_ff1': out['v_w_ff1'], 'v_w_ff2': out['v_w_ff2'], 'v_mlp_post_norm': out['v_mlp_post_norm']}


def _loss(weights, diff, rest, loss_target):
    with _jax.named_scope("forward"):
        args = {**rest, TWIN_DIFF_INPUT: diff, **{k: w.astype(_WEIGHT_DTYPES[k]) for k, w in weights.items()}}
        y = _forward(args)
    with _jax.named_scope("loss_head"):
        err = _jnp.square(y.astype(_jnp.float32) - loss_target)
        return 0.5 * _jnp.sum(_jnp.mean(err, axis=-1)) if err.ndim else 0.5 * err


def _adamw(w, g, m, v):
    m = ADAM_B1 * m + (1.0 - ADAM_B1) * g
    v = ADAM_B2 * v + (1.0 - ADAM_B2) * _jnp.square(g)
    m_hat = m / (1.0 - ADAM_B1 ** ADAM_STEP)
    v_hat = v / (1.0 - ADAM_B2 ** ADAM_STEP)
    delta = -ADAM_LR * (m_hat / (_jnp.sqrt(v_hat) + ADAM_EPS) + ADAM_WD * w)
    return delta, m, v


def reference(x, mix_pre_norm, w_in, attn_out_norm, hgrn_lb_logits, hgrn_out_norm, w_out, mix_post_norm, mlp_pre_norm, w_ff1, w_ff2, mlp_post_norm, loss_target, m_mix_pre_norm, m_w_in, m_attn_out_norm, m_hgrn_lb_logits, m_hgrn_out_norm, m_w_out, m_mix_post_norm, m_mlp_pre_norm, m_w_ff1, m_w_ff2, m_mlp_post_norm, v_mix_pre_norm, v_w_in, v_attn_out_norm, v_hgrn_lb_logits, v_hgrn_out_norm, v_w_out, v_mix_post_norm, v_mlp_pre_norm, v_w_ff1, v_w_ff2, v_mlp_post_norm):
    given = dict(x=x, mix_pre_norm=mix_pre_norm, w_in=w_in, attn_out_norm=attn_out_norm, hgrn_lb_logits=hgrn_lb_logits, hgrn_out_norm=hgrn_out_norm, w_out=w_out, mix_post_norm=mix_post_norm, mlp_pre_norm=mlp_pre_norm, w_ff1=w_ff1, w_ff2=w_ff2, mlp_post_norm=mlp_post_norm, loss_target=loss_target, m_mix_pre_norm=m_mix_pre_norm, m_w_in=m_w_in, m_attn_out_norm=m_attn_out_norm, m_hgrn_lb_logits=m_hgrn_lb_logits, m_hgrn_out_norm=m_hgrn_out_norm, m_w_out=m_w_out, m_mix_post_norm=m_mix_post_norm, m_mlp_pre_norm=m_mlp_pre_norm, m_w_ff1=m_w_ff1, m_w_ff2=m_w_ff2, m_mlp_post_norm=m_mlp_post_norm, v_mix_pre_norm=v_mix_pre_norm, v_w_in=v_w_in, v_attn_out_norm=v_attn_out_norm, v_hgrn_lb_logits=v_hgrn_lb_logits, v_hgrn_out_norm=v_hgrn_out_norm, v_w_out=v_w_out, v_mix_post_norm=v_mix_post_norm, v_mlp_pre_norm=v_mlp_pre_norm, v_w_ff1=v_w_ff1, v_w_ff2=v_w_ff2, v_mlp_post_norm=v_mlp_post_norm)
    weights = {n: given[n] for n in TWIN_WEIGHTS}
    shared = {n: given[n] for n in SHARED_INPUTS}
    per_example = {n: given[n] for n in ['x']}
    grad_fn = _jax.value_and_grad(_loss, argnums=(0, 1))

    def one_microbatch(ex, loss_target):
        ex = dict(ex)
        diff = ex.pop(TWIN_DIFF_INPUT)
        return grad_fn(weights, diff, {**shared, **ex}, loss_target)

    if N_MICROBATCH == 1:
        loss, (grad_w, grad_x) = one_microbatch(per_example, given["loss_target"])
    else:
        def body(carry, xs):
            loss_sum, grad_sum = carry
            l_k, (gw_k, gx_k) = one_microbatch(xs[0], xs[1])
            with _jax.named_scope("update"):
                return (loss_sum + l_k, _jax.tree.map(_jnp.add, grad_sum, gw_k)), gx_k

        init = (_jnp.zeros((), _jnp.float32), _jax.tree.map(_jnp.zeros_like, weights))
        (loss, grad_w), grad_x = _jax.lax.scan(body, init, (per_example, given["loss_target"]))
    with _jax.named_scope("update"):
        delta_w, new_m, new_v = {}, {}, {}
        for n in TWIN_WEIGHTS:
            delta_w[n], new_m[n], new_v[n] = _adamw(weights[n], grad_w[n], given["m_" + n], given["v_" + n])
    return (loss, grad_x, *[grad_w[n] for n in TWIN_WEIGHTS], *[delta_w[n] for n in TWIN_WEIGHTS],
            *[new_m[n] for n in TWIN_WEIGHTS], *[new_v[n] for n in TWIN_WEIGHTS])
```

```python
import jax
import jax.numpy as jnp
from jax import lax
from jax.experimental import pallas as pl
from jax.experimental.pallas import tpu as pltpu

F32 = jnp.float32
BF16 = jnp.bfloat16
MESH = pl.DeviceIdType.MESH
ANY = pl.BlockSpec(memory_space=pl.ANY)

RMS_EPS = 1e-6
D_MODEL = 1024
ATTN_W = 512
HGRN_W = 512
PROJ_W = 3584
D_FF = 4096
N_CHIPS = 4
BLK = 128
CHUNK = 64
DILATIONS = (1, 4, 16)
ATTN_SCALE = 0.125
ROW_TILE = 512
VMEM_LIMIT = 48 * 2 ** 20
FLIPS = ((1, 0), (0, 1), (1, 1))

ADAM_LR, ADAM_B1, ADAM_B2, ADAM_EPS, ADAM_WD, ADAM_STEP = 0.001, 0.9, 0.999, 1e-08, 0.01, 10


def _cp(sem=None):
    return pltpu.CompilerParams(dimension_semantics=sem, vmem_limit_bytes=VMEM_LIMIT)


def _sigmoid(v):
    return 1.0 / (1.0 + jnp.exp(-v))


def _dot(a, b, contract, precision=None):
    return lax.dot_general(a, b, (contract, ((), ())), preferred_element_type=F32, precision=precision)


NN = ((1,), (0,))
NT = ((1,), (1,))
TN = ((0,), (0,))


def _matmul(name, a, b, *, grid, a_spec, b_spec, contract, outs, epi=None, extras=(), extra_specs=(),
            acc_shape=None):
    n_ex, n_out, nj = len(extras), len(outs), grid[-1]

    def body(a_ref, b_ref, *rest):
        ex, out_refs = rest[:n_ex], rest[n_ex:n_ex + n_out]

        def finish(acc):
            res = epi(acc, *[e[...] for e in ex]) if epi else (acc,)
            for o, r in zip(out_refs, res):
                o[...] = r.astype(o.dtype)

        p = _dot(a_ref[...], b_ref[...], contract)
        if acc_shape is None:
            finish(p)
        else:
            acc_ref = rest[-1]
            j = pl.program_id(len(grid) - 1)

            @pl.when(j == 0)
            def _():
                acc_ref[...] = p

            @pl.when(j > 0)
            def _():
                acc_ref[...] += p

            @pl.when(j == nj - 1)
            def _():
                finish(acc_ref[...])

    sem = ("parallel",) * len(grid) if acc_shape is None else ("parallel",) * (len(grid) - 1) + ("arbitrary",)
    res = pl.pallas_call(
        body, name=name, grid=grid,
        in_specs=[a_spec, b_spec, *extra_specs],
        out_specs=[s for _, s in outs],
        out_shape=[s for s, _ in outs],
        scratch_shapes=[] if acc_shape is None else [pltpu.VMEM(acc_shape, F32)],
        compiler_params=_cp(sem),
    )(a, b, *extras)
    return res


def _sds(shape, dtype):
    return jax.ShapeDtypeStruct(shape, dtype)


def _mm_cols(name, a, w, contract, out_dtypes, epi=None, extras=()):
    m, k = a.shape
    jn = w.shape[0]
    nj = w.shape[2] if contract == NN else w.shape[1]
    tm = ROW_TILE
    outs = [(_sds((m, jn * nj), dt), pl.BlockSpec((tm, nj), lambda i, j: (i, j))) for dt in out_dtypes]
    return _matmul(name, a, w, grid=(m // tm, jn),
                   a_spec=pl.BlockSpec((tm, k), lambda i, j: (i, 0)),
                   b_spec=pl.BlockSpec((None,) + w.shape[1:], lambda i, j: (j, 0, 0)),
                   contract=contract, outs=outs, epi=epi, extras=extras,
                   extra_specs=[pl.BlockSpec((tm, nj), lambda i, j: (i, j)) for _ in extras])


def _mm_acc(name, a, w, contract):
    m = a.shape[0]
    jn = w.shape[0]
    kj = w.shape[1] if contract == NN else w.shape[2]
    n = w.shape[2] if contract == NN else w.shape[1]
    tm = ROW_TILE
    outs = [(_sds((m, n), F32), pl.BlockSpec((tm, n), lambda i, j: (i, 0)))]
    return _matmul(name, a, w, grid=(m // tm, jn),
                   a_spec=pl.BlockSpec((tm, kj), lambda i, j: (i, j)),
                   b_spec=pl.BlockSpec((None,) + w.shape[1:], lambda i, j: (j, 0, 0)),
                   contract=contract, outs=outs, acc_shape=(tm, n))[0]


def _mm_wgrad(name, a, b, a_by_j):
    s = a.shape[0]
    if a_by_j:
        r, c = a.shape[1] // N_CHIPS, b.shape[1]
    else:
        r, c = a.shape[1], b.shape[1] // N_CHIPS
    tr = min(r, 512)
    nr = r // tr
    if a_by_j:
        a_spec = pl.BlockSpec((s, tr), lambda j, i: (0, j * nr + i))
        b_spec = pl.BlockSpec((s, c), lambda j, i: (0, 0))
    else:
        a_spec = pl.BlockSpec((s, tr), lambda j, i: (0, i))
        b_spec = pl.BlockSpec((s, c), lambda j, i: (0, j))
    outs = [(_sds((N_CHIPS, r, c), F32), pl.BlockSpec((None, tr, c), lambda j, i: (j, i, 0)))]
    return _matmul(name, a, b, grid=(N_CHIPS, nr), a_spec=a_spec, b_spec=b_spec, contract=TN, outs=outs)[0]


def _rows_call(name, fn, ins, outs, s):
    tm = ROW_TILE
    in_specs = []
    for arr, kind in ins:
        if kind == "full":
            in_specs.append(pl.BlockSpec(arr.shape, lambda i: (0, 0)))
        else:
            _, w, cb = kind
            in_specs.append(pl.BlockSpec((tm, w), lambda i, cb=cb: (i, cb)))
    out_specs, out_shape, is_acc = [], [], []
    for w, dt, kind in outs:
        if kind == "acc":
            out_specs.append(pl.BlockSpec((1, w), lambda i: (0, 0)))
            out_shape.append(_sds((1, w), dt))
        else:
            out_specs.append(pl.BlockSpec((tm, w), lambda i: (i, 0)))
            out_shape.append(_sds((s, w), dt))
        is_acc.append(kind == "acc")
    n_in = len(ins)

    def body(*refs):
        i = pl.program_id(0)
        res = fn(*[r[...] for r in refs[:n_in]])
        for o, r, acc in zip(refs[n_in:], res, is_acc):
            if acc:
                @pl.when(i == 0)
                def _(o=o):
                    o[...] = jnp.zeros_like(o)
                o[...] += r.astype(o.dtype)
            else:
                o[...] = r.astype(o.dtype)

    sem = ("arbitrary",) if any(is_acc) else ("parallel",)
    return pl.pallas_call(body, name=name, grid=(s // tm,), in_specs=in_specs, out_specs=out_specs,
                          out_shape=out_shape, compiler_params=_cp(sem))(*[a for a, _ in ins])


def _rstd(v):
    return lax.rsqrt(jnp.mean(v * v, axis=-1, keepdims=True) + RMS_EPS)


def _norm_bwd(v, gain, dy):
    r = _rstd(v)
    n = v * r
    dn = dy * gain
    dv = r * (dn - n * jnp.mean(dn * n, axis=-1, keepdims=True))
    return dv, dy * n


def _colsum(v):
    return jnp.sum(v, axis=0, keepdims=True)


def _row(w, cb=0):
    return ("row", w, cb)


def _head_col(v, mask):
    return jnp.max(jnp.where(mask, v, -jnp.inf), axis=1, keepdims=True)


def _attn_fwd(proj, d):
    s = proj.shape[0]
    rows, nb = s // d, s // d // BLK
    projv = proj.reshape(rows, d * PROJ_W)
    cpb = PROJ_W // ATTN_W

    def body(q_ref, kc_ref, kp_ref, vc_ref, vp_ref, o_ref, l_ref):
        n = pl.program_id(1)
        row = lax.broadcasted_iota(jnp.int32, (BLK, 2 * BLK), 0)
        col = lax.broadcasted_iota(jnp.int32, (BLK, 2 * BLK), 1)
        dist = row + BLK - col
        valid = (dist >= 0) & (dist <= BLK) & (col + n * BLK >= BLK)
        distf = dist.astype(F32) * float(d)
        lane_q = lax.broadcasted_iota(jnp.int32, (BLK, BLK), 1) < 64
        lane_k = lax.broadcasted_iota(jnp.int32, (2 * BLK, BLK), 1) < 64
        for p in range(ATTN_W // BLK):
            cs = slice(p * BLK, (p + 1) * BLK)
            q2 = q_ref[:, cs]
            kk = jnp.concatenate([kp_ref[:, cs], kc_ref[:, cs]], axis=0).astype(BF16)
            vv = jnp.concatenate([vp_ref[:, cs], vc_ref[:, cs]], axis=0)
            o2 = jnp.zeros((BLK, BLK), F32)
            lse2 = jnp.zeros((BLK, BLK), F32)
            for hh in range(2):
                slope = 2.0 ** -(2 * p + hh + 1)
                mq = lane_q if hh == 0 else ~lane_q
                mk = lane_k if hh == 0 else ~lane_k
                qm = jnp.where(mq, q2, 0.0).astype(BF16)
                sc = _dot(qm, kk, NT) * ATTN_SCALE - slope * distf
                sc = jnp.where(valid, sc, -1e30)
                m = jnp.max(sc, axis=1, keepdims=True)
                pr = jnp.exp(sc - m)
                den = jnp.sum(pr, axis=1, keepdims=True)
                vm = jnp.where(mk, vv, 0.0).astype(BF16)
                o2 = o2 + _dot(pr.astype(BF16), vm, NN) / den
                lse2 = jnp.where(mq, m + jnp.log(den), lse2)
            o_ref[:, cs] = o2
            l_ref[:, cs] = lse2

    blk = (BLK, ATTN_W)
    prev = lambda r, n: jnp.maximum(n - 1, 0)
    in_specs = [
        pl.BlockSpec(blk, lambda r, n: (n, r * cpb)),
        pl.BlockSpec(blk, lambda r, n: (n, r * cpb + 1)),
        pl.BlockSpec(blk, lambda r, n: (prev(r, n), r * cpb + 1)),
        pl.BlockSpec(blk, lambda r, n: (n, r * cpb + 2)),
        pl.BlockSpec(blk, lambda r, n: (prev(r, n), r * cpb + 2)),
    ]
    out_spec = pl.BlockSpec(blk, lambda r, n: (n, r))
    o, lse = pl.pallas_call(
        body, name=f"attn_fwd_d{d}", grid=(d, nb), in_specs=in_specs, out_specs=[out_spec, out_spec],
        out_shape=[_sds((rows, d * ATTN_W), F32)] * 2, compiler_params=_cp(("parallel", "parallel")),
    )(projv, projv, projv, projv, projv)
    return o.reshape(s, ATTN_W), lse.reshape(s, ATTN_W)


def _attn_bwd(proj, do, lse, delta, d):
    s = proj.shape[0]
    rows, nb = s // d, s // d // BLK
    projv = proj.reshape(rows, d * PROJ_W)
    dov, lsev, delv = (t.reshape(rows, d * ATTN_W) for t in (do, lse, delta))
    cpb = PROJ_W // ATTN_W

    def body(qc_ref, qn_ref, kc_ref, kp_ref, vc_ref, vp_ref, dc_ref, dn_ref, lc_ref, ln_ref, ec_ref, en_ref,
             dq_ref, dk_ref, dv_ref):
        n = pl.program_id(1)
        row1 = lax.broadcasted_iota(jnp.int32, (BLK, 2 * BLK), 0)
        col1 = lax.broadcasted_iota(jnp.int32, (BLK, 2 * BLK), 1)
        dist1 = row1 + BLK - col1
        valid1 = (dist1 >= 0) & (dist1 <= BLK) & (col1 + n * BLK >= BLK)
        dist1f = dist1.astype(F32) * float(d)
        row2 = lax.broadcasted_iota(jnp.int32, (2 * BLK, BLK), 0)
        col2 = lax.broadcasted_iota(jnp.int32, (2 * BLK, BLK), 1)
        dist2 = row2 - col2
        valid2 = (dist2 >= 0) & (dist2 <= BLK) & (row2 - (nb - 1 - n) * BLK < BLK)
        dist2f = dist2.astype(F32) * float(d)
        lane1 = lax.broadcasted_iota(jnp.int32, (BLK, BLK), 1) < 64
        lane2 = lax.broadcasted_iota(jnp.int32, (2 * BLK, BLK), 1) < 64
        for p in range(ATTN_W // BLK):
            cs = slice(p * BLK, (p + 1) * BLK)
            q1, d1, l1, e1 = qc_ref[:, cs], dc_ref[:, cs], lc_ref[:, cs], ec_ref[:, cs]
            kk = jnp.concatenate([kp_ref[:, cs], kc_ref[:, cs]], axis=0)
            vv = jnp.concatenate([vp_ref[:, cs], vc_ref[:, cs]], axis=0).astype(BF16)
            qq = jnp.concatenate([q1, qn_ref[:, cs]], axis=0)
            dd = jnp.concatenate([d1, dn_ref[:, cs]], axis=0)
            ll = jnp.concatenate([l1, ln_ref[:, cs]], axis=0)
            ee = jnp.concatenate([e1, en_ref[:, cs]], axis=0)
            k1 = kc_ref[:, cs].astype(BF16)
            v1 = vc_ref[:, cs].astype(BF16)
            kkb = kk.astype(BF16)
            dq2 = jnp.zeros((BLK, BLK), F32)
            dk2 = jnp.zeros((BLK, BLK), F32)
            dv2 = jnp.zeros((BLK, BLK), F32)
            for hh in range(2):
                slope = 2.0 ** -(2 * p + hh + 1)
                m1 = lane1 if hh == 0 else ~lane1
                m2 = lane2 if hh == 0 else ~lane2
                qm = jnp.where(m1, q1, 0.0).astype(BF16)
                dm = jnp.where(m1, d1, 0.0).astype(BF16)
                sc = _dot(qm, kkb, NT) * ATTN_SCALE - slope * dist1f
                pr = jnp.where(valid1, jnp.exp(sc - _head_col(l1, m1)), 0.0)
                dp = _dot(dm, vv, NT)
                ds = pr * (dp - _head_col(e1, m1))
                km = jnp.where(m2, kk, 0.0).astype(BF16)
                dq2 = dq2 + _dot(ds.astype(BF16), km, NN) * ATTN_SCALE
                qqm = jnp.where(m2, qq, 0.0).astype(BF16)
                ddm = jnp.where(m2, dd, 0.0).astype(BF16)
                sc = _dot(qqm, k1, NT) * ATTN_SCALE - slope * dist2f
                pr = jnp.where(valid2, jnp.exp(sc - _head_col(ll, m2)), 0.0)
                dp = _dot(ddm, v1, NT)
                ds = pr * (dp - _head_col(ee, m2))
                dk2 = dk2 + _dot(ds.astype(BF16), qqm, TN) * ATTN_SCALE
                dv2 = dv2 + _dot(pr.astype(BF16), ddm, TN)
            dq_ref[:, cs] = dq2
            dk_ref[:, cs] = dk2
            dv_ref[:, cs] = dv2

    blk = (BLK, ATTN_W)
    prev = lambda n: jnp.maximum(n - 1, 0)
    nxt = lambda n: jnp.minimum(n + 1, nb - 1)
    in_specs = [
        pl.BlockSpec(blk, lambda r, n: (n, r * cpb)),
        pl.BlockSpec(blk, lambda r, n: (nxt(n), r * cpb)),
        pl.BlockSpec(blk, lambda r, n: (n, r * cpb + 1)),
        pl.BlockSpec(blk, lambda r, n: (prev(n), r * cpb + 1)),
        pl.BlockSpec(blk, lambda r, n: (n, r * cpb + 2)),
        pl.BlockSpec(blk, lambda r, n: (prev(n), r * cpb + 2)),
        pl.BlockSpec(blk, lambda r, n: (n, r)),
        pl.BlockSpec(blk, lambda r, n: (nxt(n), r)),
        pl.BlockSpec(blk, lambda r, n: (n, r)),
        pl.BlockSpec(blk, lambda r, n: (nxt(n), r)),
        pl.BlockSpec(blk, lambda r, n: (n, r)),
        pl.BlockSpec(blk, lambda r, n: (nxt(n), r)),
    ]
    out_spec = pl.BlockSpec(blk, lambda r, n: (n, r))
    res = pl.pallas_call(
        body, name=f"attn_bwd_d{d}", grid=(d, nb), in_specs=in_specs, out_specs=[out_spec] * 3,
        out_shape=[_sds((rows, d * ATTN_W), F32)] * 3, compiler_params=_cp(("parallel", "parallel")),
    )(projv, projv, projv, projv, projv, projv, dov, dov, lsev, lsev, delv, delv)
    return [t.reshape(s, ATTN_W) for t in res]


def _lower_bound(lbl):
    return 1.0 / (1.0 + jnp.exp(lbl[1:2, :] - lbl[0:1, :]))


def _hgrn_chunk(q_ref, f_ref, i_ref, sl, lb, tril, tri):
    qp = q_ref[sl, :]
    sq = _sigmoid(qp)
    qf = qp * sq
    sg = _sigmoid(f_ref[sl, :])
    f = lb + (1.0 - lb) * sg
    kf = 1.0 - f
    v = i_ref[sl, :]
    b = _dot(tri, jnp.log(f), NN, precision=lax.Precision.HIGHEST)
    bm = b[CHUNK // 2:CHUNK // 2 + 1, :]
    bl = b[CHUNK - 1:CHUNK, :]
    qt = qf * jnp.exp(b - bm)
    kt = kf * jnp.exp(bm - b)
    a = jnp.where(tril, _dot(qt.astype(BF16), kt.astype(BF16), NT), 0.0)
    return qp, sq, qf, sg, f, kf, v, b, bm, bl, qt, kt, a


def _hgrn_specs(s):
    hb = lambda base: pl.BlockSpec((s, BLK), lambda h, base=base: (0, base + h))
    first = 3 * ATTN_W // BLK
    return [hb(first + k * (HGRN_W // BLK)) for k in range(4)]


def _hgrn_fwd(proj, lb_logits, out_gain):
    s = proj.shape[0]
    nc = s // CHUNK
    nh = HGRN_W // BLK

    def body(q_ref, f_ref, i_ref, g_ref, lbl_ref, gain_ref, o_ref, rec_ref, st_ref):
        lb = _lower_bound(lbl_ref[...])
        r64 = lax.broadcasted_iota(jnp.int32, (CHUNK, CHUNK), 0)
        c64 = lax.broadcasted_iota(jnp.int32, (CHUNK, CHUNK), 1)
        tril = r64 >= c64
        tri = tril.astype(F32)

        def step(c, st):
            sl = pl.ds(pl.multiple_of(c * CHUNK, CHUNK), CHUNK)
            _, _, qf, _, _, kf, v, b, _, bl, _, _, a = _hgrn_chunk(q_ref, f_ref, i_ref, sl, lb, tril, tri)
            qe = qf * jnp.exp(b)
            o_ref[sl, :] = _dot(qe.astype(BF16), st.astype(BF16), NT) + _dot(a.astype(BF16), v.astype(BF16), NN)
            st_ref[c] = st.astype(BF16)
            kh = kf * jnp.exp(bl - b)
            return st * jnp.exp(bl) + _dot(v.astype(BF16), kh.astype(BF16), TN)

        lax.fori_loop(0, nc, step, jnp.zeros((BLK, BLK), F32))
        o = o_ref[...]
        gate = g_ref[...]
        rec_ref[...] = (o * _rstd(o) * gain_ref[...] * (gate * _sigmoid(gate))).astype(BF16)

    hb = pl.BlockSpec((s, BLK), lambda h: (0, h))
    return pl.pallas_call(
        body, name="hgrn_fwd", grid=(nh,),
        in_specs=_hgrn_specs(s) + [pl.BlockSpec((2, BLK), lambda h: (0, h)), pl.BlockSpec((1, BLK), lambda h: (0, h))],
        out_specs=[hb, hb, pl.BlockSpec((None, nc, BLK, BLK), lambda h: (h, 0, 0, 0))],
        out_shape=[_sds((s, HGRN_W), F32), _sds((s, HGRN_W), BF16), _sds((nh, nc, BLK, BLK), BF16)],
        compiler_params=_cp(("parallel",)),
    )(proj, proj, proj, proj, lb_logits, out_gain)


def _hgrn_bwd(proj, o_pre, states, dcat, lb_logits, out_gain):
    s = proj.shape[0]
    nc = s // CHUNK
    nh = HGRN_W // BLK

    def body(q_ref, f_ref, i_ref, g_ref, o_ref, st_ref, dy_ref, lbl_ref, gain_ref,
             dq_ref, df_ref, di_ref, dg_ref, dgain_ref, dlbl_ref, do_scr):
        lbl = lbl_ref[...]
        lb = _lower_bound(lbl)
        gain = gain_ref[...]
        o = o_ref[...]
        r = _rstd(o)
        nrm = o * r
        gate = g_ref[...]
        sgt = _sigmoid(gate)
        dy = dy_ref[...]
        dg_ref[...] = (dy * nrm * gain * (sgt * (1.0 + gate * (1.0 - sgt)))).astype(BF16)
        dng = dy * (gate * sgt)
        dgain_ref[...] = _colsum(dng * nrm)
        dn = dng * gain
        do_scr[...] = r * (dn - nrm * jnp.mean(dn * nrm, axis=-1, keepdims=True))

        r64 = lax.broadcasted_iota(jnp.int32, (CHUNK, CHUNK), 0)
        c64 = lax.broadcasted_iota(jnp.int32, (CHUNK, CHUNK), 1)
        tril = r64 >= c64
        tri = tril.astype(F32)
        triu = (r64 <= c64).astype(F32)

        def step(k, carry):
            dst, dlb = carry
            c = nc - 1 - k
            st_end = st_ref[jnp.minimum(c + 1, nc - 1)].astype(F32)
            csum = jnp.sum(st_end * dst, axis=0, keepdims=True)
            sl = pl.ds(pl.multiple_of(c * CHUNK, CHUNK), CHUNK)
            qp, sq, qf, sg, f, kf, v, b, bm, bl, qt, kt, a = _hgrn_chunk(q_ref, f_ref, i_ref, sl, lb, tril, tri)
            st = st_ref[c]
            doc = do_scr[sl, :]
            dob = doc.astype(BF16)
            vb = v.astype(BF16)
            dstb = dst.astype(BF16)
            eb = jnp.exp(b)
            qe = qf * eb
            kh = kf * jnp.exp(bl - b)
            hi = lax.Precision.HIGHEST
            da = jnp.where(tril, _dot(doc, v, NT, hi), 0.0)
            dqf = _dot(da, kt, NN, hi) * jnp.exp(b - bm) + eb * _dot(doc, st.astype(F32), NN, hi)
            dkf = _dot(da, qt, TN, hi) * jnp.exp(bm - b) + jnp.exp(bl - b) * _dot(v, dst, NN, hi)
            dv = _dot(a.astype(BF16), dob, TN) + _dot(kh.astype(BF16), dstb, NT)
            dst_new = dst * jnp.exp(bl) + _dot(dob, qe.astype(BF16), TN)
            gq = qf * dqf - kf * dkf
            dlogf = csum + _dot(triu, gq, NN, precision=lax.Precision.HIGHEST)
            dfv = dlogf / f - dkf
            dq_ref[sl, :] = (dqf * (sq * (1.0 + qp * (1.0 - sq)))).astype(BF16)
            df_ref[sl, :] = (dfv * (1.0 - lb) * sg * (1.0 - sg)).astype(BF16)
            di_ref[sl, :] = dv.astype(BF16)
            return dst_new, dlb + _colsum(dfv * (1.0 - sg))

        _, dlb = lax.fori_loop(0, nc, step, (jnp.zeros((BLK, BLK), F32), jnp.zeros((1, BLK), F32)))
        t = dlb * lb * (1.0 - lb)
        dlbl_ref[...] = jnp.concatenate([t, -t], axis=0)

    hb = pl.BlockSpec((s, BLK), lambda h: (0, h))
    first = ATTN_W // BLK
    res = pl.pallas_call(
        body, name="hgrn_bwd", grid=(nh,),
        in_specs=_hgrn_specs(s) + [
            hb, pl.BlockSpec((None, nc, BLK, BLK), lambda h: (h, 0, 0, 0)),
            pl.BlockSpec((s, BLK), lambda h: (0, first + h)),
            pl.BlockSpec((2, BLK), lambda h: (0, h)), pl.BlockSpec((1, BLK), lambda h: (0, h))],
        out_specs=[hb, hb, hb, hb, pl.BlockSpec((1, BLK), lambda h: (0, h)),
                   pl.BlockSpec((2, BLK), lambda h: (0, h))],
        out_shape=[_sds((s, HGRN_W), BF16)] * 4 + [_sds((1, HGRN_W), F32), _sds((2, HGRN_W), F32)],
        scratch_shapes=[pltpu.VMEM((s, BLK), F32)],
        compiler_params=_cp(("parallel",)),
    )(proj, proj, proj, proj, o_pre, states, dcat, lb_logits, out_gain)
    return res


def _place():
    return lax.axis_index("x"), lax.axis_index("y"), lax.axis_index("c")


def _flip(x, y, ox, oy):
    return (1 - x if ox else x), (1 - y if oy else y)


def _half(rows, cc):
    return pl.ds(cc * (rows // 2), rows // 2)


def _ag_weights(shards):
    nk = len(shards)

    def body(*refs):
        ins, outs = refs[:nk], refs[nk:2 * nk]
        ssem, rsem, lsem = refs[2 * nk:]
        x, y, c = _place()
        j = 2 * x + y
        own = []
        for k in range(nk):
            cp = pltpu.make_async_copy(ins[k], outs[k].at[j], lsem.at[k])
            cp.start()
            own.append(cp)

        def remote(k, slot, src, dst, to):
            return pltpu.make_async_remote_copy(src_ref=src, dst_ref=dst, send_sem=ssem.at[k, slot],
                                                recv_sem=rsem.at[k, slot], device_id=to, device_id_type=MESH)

        sent = []
        for k in range(nk):
            rows = shards[k].shape[0]
            for idx, (ox, oy) in enumerate(FLIPS):
                px, py = _flip(x, y, ox, oy)
                cp = remote(k, idx, ins[k].at[_half(rows, c)], outs[k].at[j, _half(rows, c)], (px, py, c))
                cp.start()
                sent.append(cp)
        for k in range(nk):
            rows = shards[k].shape[0]
            for idx, (ox, oy) in enumerate(FLIPS):
                px, py = _flip(x, y, ox, oy)
                blk = outs[k].at[2 * px + py, _half(rows, c)]
                remote(k, idx, blk, blk, (px, py, c)).wait_recv()
                cp = remote(k, 3 + idx, blk, blk, (x, y, 1 - c))
                cp.start()
                sent.append(cp)
        for k in range(nk):
            rows = shards[k].shape[0]
            for idx, (ox, oy) in enumerate(FLIPS):
                px, py = _flip(x, y, ox, oy)
                blk = outs[k].at[2 * px + py, _half(rows, 1 - c)]
                remote(k, 3 + idx, blk, blk, (x, y, 1 - c)).wait_recv()
        for cp in sent:
            cp.wait_send()
        for cp in own:
            cp.wait()

    return pl.pallas_call(
        body, name="ag_weights", in_specs=[ANY] * nk, out_specs=[ANY] * nk,
        out_shape=[_sds((N_CHIPS,) + w.shape, w.dtype) for w in shards],
        scratch_shapes=[pltpu.SemaphoreType.DMA((nk, 6)), pltpu.SemaphoreType.DMA((nk, 6)),
                        pltpu.SemaphoreType.DMA((nk,))],
    )(*shards)


def _rs_pair(grads):
    nk = len(grads)

    def body(*refs):
        ins, outs = refs[:nk], refs[nk:2 * nk]
        ssem, rsem = refs[2 * nk:]
        x, y, c = _place()
        cps = []
        for k in range(nk):
            rows = grads[k].shape[1]
            cp = pltpu.make_async_remote_copy(src_ref=ins[k].at[:, _half(rows, 1 - c)], dst_ref=outs[k],
                                              send_sem=ssem.at[k], recv_sem=rsem.at[k],
                                              device_id=(x, y, 1 - c), device_id_type=MESH)
            cp.start()
            cps.append(cp)
        for cp in cps:
            cp.wait()

    return pl.pallas_call(
        body, name="rs_pair", in_specs=[ANY] * nk, out_specs=[ANY] * nk,
        out_shape=[_sds((N_CHIPS, g.shape[1] // 2, g.shape[2]), g.dtype) for g in grads],
        scratch_shapes=[pltpu.SemaphoreType.DMA((nk,)), pltpu.SemaphoreType.DMA((nk,))],
    )(*grads)


def _rs_chips(psums):
    nk = len(psums)

    def body(*refs):
        ins, outs = refs[:nk], refs[nk:2 * nk]
        ssem, rsem = refs[2 * nk:]
        x, y, c = _place()
        cps = []
        for k in range(nk):
            for idx, (ox, oy) in enumerate(FLIPS):
                px, py = _flip(x, y, ox, oy)
                cp = pltpu.make_async_remote_copy(src_ref=ins[k].at[2 * px + py], dst_ref=outs[k].at[idx],
                                                  send_sem=ssem.at[k, idx], recv_sem=rsem.at[k, idx],
                                                  device_id=(px, py, c), device_id_type=MESH)
                cp.start()
                cps.append(cp)
        for cp in cps:
            cp.wait()

    return pl.pallas_call(
        body, name="rs_chips", in_specs=[ANY] * nk, out_specs=[ANY] * nk,
        out_shape=[_sds((3,) + p.shape[1:], p.dtype) for p in psums],
        scratch_shapes=[pltpu.SemaphoreType.DMA((nk, 3)), pltpu.SemaphoreType.DMA((nk, 3))],
    )(*psums)


def _rs_share(halves):
    nk = len(halves)

    def body(*refs):
        ins, outs = refs[:nk], refs[nk:2 * nk]
        ssem, rsem, lsem = refs[2 * nk:]
        x, y, c = _place()
        cps = []
        for k in range(nk):
            rows = 2 * halves[k].shape[0]
            own = pltpu.make_async_copy(ins[k], outs[k].at[_half(rows, c)], lsem.at[k])
            own.start()
            cp = pltpu.make_async_remote_copy(src_ref=ins[k], dst_ref=outs[k].at[_half(rows, c)],
                                              send_sem=ssem.at[k], recv_sem=rsem.at[k],
                                              device_id=(x, y, 1 - c), device_id_type=MESH)
            cp.start()
            cps.append((own, cp, rows))
        for k, (own, cp, rows) in enumerate(cps):
            own.wait()
            cp.wait_send()
            theirs = outs[k].at[_half(rows, 1 - c)]
            pltpu.make_async_remote_copy(src_ref=theirs, dst_ref=theirs, send_sem=ssem.at[k], recv_sem=rsem.at[k],
                                         device_id=(x, y, 1 - c), device_id_type=MESH).wait_recv()

    return pl.pallas_call(
        body, name="rs_share", in_specs=[ANY] * nk, out_specs=[ANY] * nk,
        out_shape=[_sds((2 * h.shape[0], h.shape[1]), h.dtype) for h in halves],
        scratch_shapes=[pltpu.SemaphoreType.DMA((nk,)), pltpu.SemaphoreType.DMA((nk,)),
                        pltpu.SemaphoreType.DMA((nk,))],
    )(*halves)


def _allreduce_small(v):
    ndev = 8

    def body(in_ref, out_ref, buf, ssem, rsem):
        x, y, c = _place()
        me = 4 * x + 2 * y + c
        buf[me] = in_ref[...]
        cps = []
        for k in range(1, ndev):
            ox, oy, oc = (k >> 2) & 1, (k >> 1) & 1, k & 1
            px, py = _flip(x, y, ox, oy)
            pc = 1 - c if oc else c
            cp = pltpu.make_async_remote_copy(src_ref=in_ref, dst_ref=buf.at[me], send_sem=ssem.at[k - 1],
                                              recv_sem=rsem.at[k - 1], device_id=(px, py, pc), device_id_type=MESH)
            cp.start()
            cps.append((cp, 4 * px + 2 * py + pc, (px, py, pc)))
        for k, (cp, src, peer) in enumerate(cps):
            cp.wait_send()
            pltpu.make_async_remote_copy(src_ref=in_ref, dst_ref=buf.at[src], send_sem=ssem.at[k],
                                         recv_sem=rsem.at[k], device_id=peer, device_id_type=MESH).wait_recv()
        acc = buf[0]
        for i in range(1, ndev):
            acc = acc + buf[i]
        out_ref[...] = acc

    return pl.pallas_call(
        body, name="allreduce_small",
        in_specs=[pl.BlockSpec(memory_space=pltpu.VMEM)], out_specs=pl.BlockSpec(memory_space=pltpu.VMEM),
        out_shape=_sds(v.shape, v.dtype),
        scratch_shapes=[pltpu.VMEM((ndev,) + v.shape, v.dtype), pltpu.SemaphoreType.DMA((ndev - 1,)),
                        pltpu.SemaphoreType.DMA((ndev - 1,))],
    )(v)


def _rs_sum1(name, g, recv, c_idx):
    _, r, cdim = g.shape
    hr = r // 2
    tr = min(hr, 256)
    nr = hr // tr

    def body(c_ref, g_ref, r_ref, o32_ref, o16_ref):
        v = g_ref[...] + r_ref[...].astype(F32)
        o32_ref[...] = v
        o16_ref[...] = v.astype(BF16)

    spec = pl.BlockSpec((None, tr, cdim), lambda j, i, c_ref: (j, i, 0))
    return pl.pallas_call(
        body, name=name,
        grid_spec=pltpu.PrefetchScalarGridSpec(
            num_scalar_prefetch=1, grid=(N_CHIPS, nr),
            in_specs=[pl.BlockSpec((None, tr, cdim), lambda j, i, c_ref: (j, c_ref[0] * nr + i, 0)), spec],
            out_specs=[spec, spec]),
        out_shape=[_sds((N_CHIPS, hr, cdim), F32), _sds((N_CHIPS, hr, cdim), BF16)],
        compiler_params=_cp(("parallel", "parallel")),
    )(c_idx, g, recv)


def _rs_sum2(name, p32, recv, j_idx):
    _, hr, cdim = p32.shape
    tr = min(hr, 256)

    def body(j_ref, p_ref, r_ref, o_ref):
        o_ref[...] = ((p_ref[...] + r_ref[0].astype(F32)) + r_ref[1].astype(F32)) + r_ref[2].astype(F32)

    return pl.pallas_call(
        body, name=name,
        grid_spec=pltpu.PrefetchScalarGridSpec(
            num_scalar_prefetch=1, grid=(hr // tr,),
            in_specs=[pl.BlockSpec((None, tr, cdim), lambda i, j_ref: (j_ref[0], i, 0)),
                      pl.BlockSpec((3, tr, cdim), lambda i, j_ref: (0, i, 0))],
            out_specs=pl.BlockSpec((tr, cdim), lambda i, j_ref: (i, 0))),
        out_shape=_sds((hr, cdim), F32),
        compiler_params=_cp(("parallel",)),
    )(j_idx, p32, recv)


def _adamw(name, w, g, m, v):
    r, cdim = w.shape
    tr = min(r, 256)
    c1 = 1.0 - ADAM_B1 ** ADAM_STEP
    c2 = 1.0 - ADAM_B2 ** ADAM_STEP

    def body(w_ref, g_ref, m_ref, v_ref, d_ref, nm_ref, nv_ref):
        gv = g_ref[...]
        nm = ADAM_B1 * m_ref[...] + (1.0 - ADAM_B1) * gv
        nv = ADAM_B2 * v_ref[...] + (1.0 - ADAM_B2) * (gv * gv)
        d_ref[...] = -ADAM_LR * ((nm / c1) / (jnp.sqrt(nv / c2) + ADAM_EPS) + ADAM_WD * w_ref[...])
        nm_ref[...] = nm
        nv_ref[...] = nv

    spec = pl.BlockSpec((tr, cdim), lambda i: (i, 0))
    return pl.pallas_call(
        body, name=name, grid=(r // tr,), in_specs=[spec] * 4, out_specs=[spec] * 3,
        out_shape=[_sds((r, cdim), F32)] * 3, compiler_params=_cp(("parallel",)),
    )(w, g, m, v)


def _pack_small(mix_pre, attn_out, lb_logits, hgrn_out, mix_post, mlp_pre, mlp_post):
    rows = [mix_pre, jnp.concatenate([attn_out, hgrn_out], axis=1),
            jnp.concatenate([lb_logits[0:1], lb_logits[1:2]], axis=1), mix_post, mlp_pre, mlp_post,
            jnp.zeros((2, D_MODEL), F32)]
    return jnp.concatenate(rows, axis=0)


def _unpack_small(p):
    return (p[0:1], p[1:2, :ATTN_W], jnp.concatenate([p[2:3, :HGRN_W], p[2:3, HGRN_W:]], axis=0),
            p[1:2, ATTN_W:], p[3:4], p[4:5], p[5:6])


def kernel(x, mix_pre_norm, w_in, attn_out_norm, hgrn_lb_logits, hgrn_out_norm, w_out, mix_post_norm, mlp_pre_norm, w_ff1, w_ff2, mlp_post_norm, loss_target, m_mix_pre_norm, m_w_in, m_attn_out_norm, m_hgrn_lb_logits, m_hgrn_out_norm, m_w_out, m_mix_post_norm, m_mlp_pre_norm, m_w_ff1, m_w_ff2, m_mlp_post_norm, v_mix_pre_norm, v_w_in, v_attn_out_norm, v_hgrn_lb_logits, v_hgrn_out_norm, v_w_out, v_mix_post_norm, v_mlp_pre_norm, v_w_ff1, v_w_ff2, v_mlp_post_norm):
    s = x.shape[1]
    xs = x.reshape(s, D_MODEL)
    tgt = loss_target.reshape(s, D_MODEL)
    cx, cy, cc = _place()
    c_idx = jnp.reshape(cc, (1,)).astype(jnp.int32)
    j_idx = jnp.reshape(2 * cx + cy, (1,)).astype(jnp.int32)

    big_w = [w_in[0], w_out[0], w_ff1[0], w_ff2[0]]
    big_m = [m_w_in[0], m_w_out[0], m_w_ff1[0], m_w_ff2[0]]
    big_v = [v_w_in[0], v_w_out[0], v_w_ff1[0], v_w_ff2[0]]

    wg_in, wg_out, wg_1, wg_2 = _ag_weights([w.astype(BF16) for w in big_w])

    (h,) = _rows_call("norm_in", lambda xv, g: ((xv * _rstd(xv) * g),),
                      [(xs, _row(D_MODEL)), (mix_pre_norm, "full")], [(D_MODEL, BF16, "row")], s)
    (proj,) = _mm_cols("mm_proj", h, wg_in, NN, [F32])
    att = [_attn_fwd(proj, d) for d in DILATIONS]

    def comb(o1, o2, o3, l1, l2, l3, gain):
        mx = jnp.maximum(jnp.maximum(l1, l2), l3)
        lse = mx + jnp.log(jnp.exp(l1 - mx) + jnp.exp(l2 - mx) + jnp.exp(l3 - mx))
        o = jnp.exp(l1 - lse) * o1 + jnp.exp(l2 - lse) * o2 + jnp.exp(l3 - lse) * o3
        return o, lse, o * _rstd(o) * gain

    attn_o, attn_lse, attn_n = _rows_call(
        "attn_comb", comb, [(a[0], _row(ATTN_W)) for a in att] + [(a[1], _row(ATTN_W)) for a in att]
        + [(attn_out_norm, "full")], [(ATTN_W, F32, "row"), (ATTN_W, F32, "row"), (ATTN_W, BF16, "row")], s)
    hg_o, rec, states = _hgrn_fwd(proj, hgrn_lb_logits, hgrn_out_norm)
    cat = jnp.concatenate([attn_n, rec], axis=1)
    mixed = _mm_acc("mm_mixed", cat, wg_out, NN)

    def post1(xv, mv, g_post, g_pre2):
        x1 = xv + mv * _rstd(mv) * g_post
        return x1, x1 * _rstd(x1) * g_pre2

    x1, h2 = _rows_call("post1", post1, [(xs, _row(D_MODEL)), (mixed, _row(D_MODEL)), (mix_post_norm, "full"),
                                         (mlp_pre_norm, "full")], [(D_MODEL, F32, "row"), (D_MODEL, BF16, "row")], s)

    def sq_relu(u):
        r = jnp.maximum(u, 0.0)
        return r * r, r

    act, ru = _mm_cols("mm_ff1", h2, wg_1, NN, [BF16, BF16], epi=sq_relu)
    ff = _mm_acc("mm_ff2", act, wg_2, NN)

    def post2(x1v, fv, tv, g):
        y = x1v + fv * _rstd(fv) * g
        dy = (y - tv) * (1.0 / D_MODEL)
        err = y - tv
        loss = 0.5 * jnp.sum(jnp.mean(err * err, axis=-1, keepdims=True), axis=0, keepdims=True)
        dff, dgc = _norm_bwd(fv, g, dy)
        return dy, dff, _colsum(dgc), jnp.broadcast_to(loss, (1, BLK))

    dy, dff, g_mlp_post, loss_part = _rows_call(
        "post2", post2, [(x1, _row(D_MODEL)), (ff, _row(D_MODEL)), (tgt, _row(D_MODEL)), (mlp_post_norm, "full")],
        [(D_MODEL, F32, "row"), (D_MODEL, BF16, "row"), (D_MODEL, F32, "acc"), (BLK, F32, "acc")], s)

    (du,) = _mm_cols("mm_du", dff, wg_2, NT, [BF16], epi=lambda acc, r: (acc * (2.0 * r.astype(F32)),),
                     extras=(ru,))
    gw_2 = _mm_wgrad("mm_gw2", act, dff, True)
    gw_1 = _mm_wgrad("mm_gw1", h2, du, False)
    dh2 = _mm_acc("mm_dh2", du, wg_1, NT)

    def bwd_mid(dyv, dh2v, x1v, mv, g_pre2, g_post):
        d1, gc1 = _norm_bwd(x1v, g_pre2, dh2v)
        dx1 = dyv + d1
        dm, gc2 = _norm_bwd(mv, g_post, dx1)
        return dx1, dm, _colsum(gc1), _colsum(gc2)

    dx1, dmixed, g_mlp_pre, g_mix_post = _rows_call(
        "bwd_mid", bwd_mid, [(dy, _row(D_MODEL)), (dh2, _row(D_MODEL)), (x1, _row(D_MODEL)), (mixed, _row(D_MODEL)),
                             (mlp_pre_norm, "full"), (mix_post_norm, "full")],
        [(D_MODEL, F32, "row"), (D_MODEL, BF16, "row"), (D_MODEL, F32, "acc"), (D_MODEL, F32, "acc")], s)

    (dcat,) = _mm_cols("mm_dcat", dmixed, wg_out, NT, [F32])
    gw_out = _mm_wgrad("mm_gwout", cat, dmixed, True)

    def attn_norm_bwd(dc, o, gain):
        do, gc = _norm_bwd(o, gain, dc)
        t = do * o
        lane = lax.broadcasted_iota(jnp.int32, (t.shape[0], BLK), 1) < 64
        parts = []
        for p in range(ATTN_W // BLK):
            tp = t[:, p * BLK:(p + 1) * BLK]
            sa = jnp.sum(jnp.where(lane, tp, 0.0), axis=1, keepdims=True)
            sb = jnp.sum(jnp.where(lane, 0.0, tp), axis=1, keepdims=True)
            parts.append(jnp.where(lane, sa, sb))
        return do, jnp.concatenate(parts, axis=1), _colsum(gc)

    do_attn, delta, g_attn_out = _rows_call(
        "attn_norm_bwd", attn_norm_bwd, [(dcat, _row(ATTN_W, 0)), (attn_o, _row(ATTN_W)), (attn_out_norm, "full")],
        [(ATTN_W, F32, "row"), (ATTN_W, F32, "row"), (ATTN_W, F32, "acc")], s)
    dqkv = [_attn_bwd(proj, do_attn, attn_lse, delta, d) for d in DILATIONS]
    dhq, dhf, dhi, dhg, g_hgrn_out, g_lb = _hgrn_bwd(proj, hg_o, states, dcat, hgrn_lb_logits, hgrn_out_norm)
    dh_parts = [dhq, dhf, dhi, dhg]

    def dproj_asm(*a):
        a1, a4, a16 = a[0:3], a[3:6], a[6:9]
        return (jnp.concatenate([(a1[k] + a4[k] + a16[k]).astype(BF16) for k in range(3)] + list(a[9:]), axis=1),)

    (dproj,) = _rows_call("dproj_asm", dproj_asm,
                          [(t, _row(ATTN_W)) for grp in dqkv for t in grp] + [(t, _row(HGRN_W)) for t in dh_parts],
                          [(PROJ_W, BF16, "row")], s)
    dh = _mm_acc("mm_dh", dproj, wg_in, NT)
    gw_in = _mm_wgrad("mm_gwin", h, dproj, False)

    def bwd_in(dx1v, dhv, xv, g):
        d0, gc = _norm_bwd(xv, g, dhv)
        return dx1v + d0, _colsum(gc)

    grad_x, g_mix_pre = _rows_call("bwd_in", bwd_in, [(dx1, _row(D_MODEL)), (dh, _row(D_MODEL)), (xs, _row(D_MODEL)),
                                                      (mix_pre_norm, "full")],
                                   [(D_MODEL, F32, "row"), (D_MODEL, F32, "acc")], s)

    loss = lax.psum(loss_part[0, 0], ("x", "y", "c"))
    small_g = _allreduce_small(_pack_small(g_mix_pre, g_attn_out, g_lb, g_hgrn_out, g_mix_post, g_mlp_pre, g_mlp_post))

    grads = [gw_in, gw_out, gw_1, gw_2]
    names = ["in", "out", "ff1", "ff2"]
    from_pair = _rs_pair([g.astype(BF16) for g in grads])
    pair = [_rs_sum1(f"rs_sum1_{n}", g, r, c_idx) for n, g, r in zip(names, grads, from_pair)]
    from_chips = _rs_chips([p[1] for p in pair])
    halves = [_rs_sum2(f"rs_sum2_{n}", p[0], r, j_idx) for n, p, r in zip(names, pair, from_chips)]
    full = _rs_share(halves)

    upd = [_adamw(f"adamw_{n}", w, g, m, v) for n, w, g, m, v in zip(names, big_w, full, big_m, big_v)]
    small_w = _pack_small(mix_pre_norm, attn_out_norm, hgrn_lb_logits, hgrn_out_norm, mix_post_norm, mlp_pre_norm,
                          mlp_post_norm)
    small_m = _pack_small(m_mix_pre_norm, m_attn_out_norm, m_hgrn_lb_logits, m_hgrn_out_norm, m_mix_post_norm,
                          m_mlp_pre_norm, m_mlp_post_norm)
    small_v = _pack_small(v_mix_pre_norm, v_attn_out_norm, v_hgrn_lb_logits, v_hgrn_out_norm, v_mix_post_norm,
                          v_mlp_pre_norm, v_mlp_post_norm)
    small_upd = _adamw("adamw_small", small_w, small_g, small_m, small_v)

    def assemble(small, big):
        sm = _unpack_small(small)
        return (sm[0], big[0][None], sm[1], sm[2], sm[3], big[1][None], sm[4], sm[5], big[2][None], big[3][None], sm[6])

    g_out = assemble(small_g, full)
    d_out = assemble(small_upd[0], [u[0] for u in upd])
    m_out = assemble(small_upd[1], [u[1] for u in upd])
    v_out = assemble(small_upd[2], [u[2] for u in upd])
    return (loss, grad_x.reshape(x.shape), *g_out, *d_out, *m_out, *v_out)
```

```python
import jax
import jax.numpy as jnp
from jax import lax
from jax.experimental import pallas as pl
from jax.experimental.pallas import tpu as pltpu

F32 = jnp.float32
BF16 = jnp.bfloat16
MESH = pl.DeviceIdType.MESH
ANY = pl.BlockSpec(memory_space=pl.ANY)

RMS_EPS = 1e-6
D_MODEL = 1024
ATTN_W = 512
HGRN_W = 512
PROJ_W = 3584
D_FF = 4096
N_CHIPS = 4
BLK = 128
CHUNK = 64
DILATIONS = (1, 4, 16)
ATTN_SCALE = 0.125
ROW_TILE = 512
VMEM_LIMIT = 48 * 2 ** 20
FLIPS = ((1, 0), (0, 1), (1, 1))

ADAM_LR, ADAM_B1, ADAM_B2, ADAM_EPS, ADAM_WD, ADAM_STEP = 0.001, 0.9, 0.999, 1e-08, 0.01, 10


def _cp(sem=None):
    return pltpu.CompilerParams(dimension_semantics=sem, vmem_limit_bytes=VMEM_LIMIT)


def _sigmoid(v):
    return 1.0 / (1.0 + jnp.exp(-v))


def _dot(a, b, contract, precision=None):
    return lax.dot_general(a, b, (contract, ((), ())), preferred_element_type=F32, precision=precision)


NN = ((1,), (0,))
NT = ((1,), (1,))
TN = ((0,), (0,))


def _matmul(name, a, b, *, grid, a_spec, b_spec, contract, outs, epi=None, extras=(), extra_specs=(),
            acc_shape=None):
    n_ex, n_out, nj = len(extras), len(outs), grid[-1]

    def body(a_ref, b_ref, *rest):
        ex, out_refs = rest[:n_ex], rest[n_ex:n_ex + n_out]

        def finish(acc):
            res = epi(acc, *[e[...] for e in ex]) if epi else (acc,)
            for o, r in zip(out_refs, res):
                o[...] = r.astype(o.dtype)

        p = _dot(a_ref[...], b_ref[...], contract)
        if acc_shape is None:
            finish(p)
        else:
            acc_ref = rest[-1]
            j = pl.program_id(len(grid) - 1)

            @pl.when(j == 0)
            def _():
                acc_ref[...] = p

            @pl.when(j > 0)
            def _():
                acc_ref[...] += p

            @pl.when(j == nj - 1)
            def _():
                finish(acc_ref[...])

    sem = ("parallel",) * len(grid) if acc_shape is None else ("parallel",) * (len(grid) - 1) + ("arbitrary",)
    res = pl.pallas_call(
        body, name=name, grid=grid,
        in_specs=[a_spec, b_spec, *extra_specs],
        out_specs=[s for _, s in outs],
        out_shape=[s for s, _ in outs],
        scratch_shapes=[] if acc_shape is None else [pltpu.VMEM(acc_shape, F32)],
        compiler_params=_cp(sem),
    )(a, b, *extras)
    return res


def _sds(shape, dtype):
    return jax.ShapeDtypeStruct(shape, dtype)


def _mm_cols(name, a, w, contract, out_dtypes, epi=None, extras=()):
    m, k = a.shape
    jn = w.shape[0]
    nj = w.shape[2] if contract == NN else w.shape[1]
    tm = ROW_TILE
    outs = [(_sds((m, jn * nj), dt), pl.BlockSpec((tm, nj), lambda i, j: (i, j))) for dt in out_dtypes]
    return _matmul(name, a, w, grid=(m // tm, jn),
                   a_spec=pl.BlockSpec((tm, k), lambda i, j: (i, 0)),
                   b_spec=pl.BlockSpec((None,) + w.shape[1:], lambda i, j: (j, 0, 0)),
                   contract=contract, outs=outs, epi=epi, extras=extras,
                   extra_specs=[pl.BlockSpec((tm, nj), lambda i, j: (i, j)) for _ in extras])


def _mm_acc(name, a, w, contract):
    m = a.shape[0]
    jn = w.shape[0]
    kj = w.shape[1] if contract == NN else w.shape[2]
    n = w.shape[2] if contract == NN else w.shape[1]
    tm = ROW_TILE
    outs = [(_sds((m, n), F32), pl.BlockSpec((tm, n), lambda i, j: (i, 0)))]
    return _matmul(name, a, w, grid=(m // tm, jn),
                   a_spec=pl.BlockSpec((tm, kj), lambda i, j: (i, j)),
                   b_spec=pl.BlockSpec((None,) + w.shape[1:], lambda i, j: (j, 0, 0)),
                   contract=contract, outs=outs, acc_shape=(tm, n))[0]


def _mm_wgrad(name, a, b, a_by_j):
    s = a.shape[0]
    if a_by_j:
        r, c = a.shape[1] // N_CHIPS, b.shape[1]
    else:
        r, c = a.shape[1], b.shape[1] // N_CHIPS
    tr = min(r, 512)
    nr = r // tr
    if a_by_j:
        a_spec = pl.BlockSpec((s, tr), lambda j, i: (0, j * nr + i))
        b_spec = pl.BlockSpec((s, c), lambda j, i: (0, 0))
    else:
        a_spec = pl.BlockSpec((s, tr), lambda j, i: (0, i))
        b_spec = pl.BlockSpec((s, c), lambda j, i: (0, j))
    outs = [(_sds((N_CHIPS, r, c), F32), pl.BlockSpec((None, tr, c), lambda j, i: (j, i, 0)))]
    return _matmul(name, a, b, grid=(N_CHIPS, nr), a_spec=a_spec, b_spec=b_spec, contract=TN, outs=outs)[0]


def _rows_call(name, fn, ins, outs, s):
    tm = ROW_TILE
    in_specs = []
    for arr, kind in ins:
        if kind == "full":
            in_specs.append(pl.BlockSpec(arr.shape, lambda i: (0, 0)))
        else:
            _, w, cb = kind
            in_specs.append(pl.BlockSpec((tm, w), lambda i, cb=cb: (i, cb)))
    out_specs, out_shape, is_acc = [], [], []
    for w, dt, kind in outs:
        if kind == "acc":
            out_specs.append(pl.BlockSpec((1, w), lambda i: (0, 0)))
            out_shape.append(_sds((1, w), dt))
        else:
            out_specs.append(pl.BlockSpec((tm, w), lambda i: (i, 0)))
            out_shape.append(_sds((s, w), dt))
        is_acc.append(kind == "acc")
    n_in = len(ins)

    def body(*refs):
        i = pl.program_id(0)
        res = fn(*[r[...] for r in refs[:n_in]])
        for o, r, acc in zip(refs[n_in:], res, is_acc):
            if acc:
                @pl.when(i == 0)
                def _(o=o):
                    o[...] = jnp.zeros_like(o)
                o[...] += r.astype(o.dtype)
            else:
                o[...] = r.astype(o.dtype)

    sem = ("arbitrary",) if any(is_acc) else ("parallel",)
    return pl.pallas_call(body, name=name, grid=(s // tm,), in_specs=in_specs, out_specs=out_specs,
                          out_shape=out_shape, compiler_params=_cp(sem))(*[a for a, _ in ins])


def _rstd(v):
    return lax.rsqrt(jnp.mean(v * v, axis=-1, keepdims=True) + RMS_EPS)


def _norm_bwd(v, gain, dy):
    r = _rstd(v)
    n = v * r
    dn = dy * gain
    dv = r * (dn - n * jnp.mean(dn * n, axis=-1, keepdims=True))
    return dv, dy * n


def _colsum(v):
    return jnp.sum(v, axis=0, keepdims=True)


def _row(w, cb=0):
    return ("row", w, cb)


def _head_col(v, mask):
    return jnp.max(jnp.where(mask, v, -jnp.inf), axis=1, keepdims=True)


def _attn_fwd(proj, d):
    s = proj.shape[0]
    rows, nb = s // d, s // d // BLK
    projv = proj.reshape(rows, d * PROJ_W)
    cpb = PROJ_W // ATTN_W

    def body(q_ref, kc_ref, kp_ref, vc_ref, vp_ref, o_ref, l_ref):
        n = pl.program_id(1)
        row = lax.broadcasted_iota(jnp.int32, (BLK, 2 * BLK), 0)
        col = lax.broadcasted_iota(jnp.int32, (BLK, 2 * BLK), 1)
        dist = row + BLK - col
        valid = (dist >= 0) & (dist <= BLK) & (col + n * BLK >= BLK)
        distf = dist.astype(F32) * float(d)
        lane_q = lax.broadcasted_iota(jnp.int32, (BLK, BLK), 1) < 64
        lane_k = lax.broadcasted_iota(jnp.int32, (2 * BLK, BLK), 1) < 64
        for p in range(ATTN_W // BLK):
            cs = slice(p * BLK, (p + 1) * BLK)
            q2 = q_ref[:, cs]
            kk = jnp.concatenate([kp_ref[:, cs], kc_ref[:, cs]], axis=0).astype(BF16)
            vv = jnp.concatenate([vp_ref[:, cs], vc_ref[:, cs]], axis=0)
            o2 = jnp.zeros((BLK, BLK), F32)
            lse2 = jnp.zeros((BLK, BLK), F32)
            for hh in range(2):
                slope = 2.0 ** -(2 * p + hh + 1)
                mq = lane_q if hh == 0 else ~lane_q
                mk = lane_k if hh == 0 else ~lane_k
                qm = jnp.where(mq, q2, 0.0).astype(BF16)
                sc = _dot(qm, kk, NT) * ATTN_SCALE - slope * distf
                sc = jnp.where(valid, sc, -1e30)
                m = jnp.max(sc, axis=1, keepdims=True)
                pr = jnp.exp(sc - m)
                den = jnp.sum(pr, axis=1, keepdims=True)
                vm = jnp.where(mk, vv, 0.0).astype(BF16)
                o2 = o2 + _dot(pr.astype(BF16), vm, NN) / den
                lse2 = jnp.where(mq, m + jnp.log(den), lse2)
            o_ref[:, cs] = o2
            l_ref[:, cs] = lse2

    blk = (BLK, ATTN_W)
    prev = lambda r, n: jnp.maximum(n - 1, 0)
    in_specs = [
        pl.BlockSpec(blk, lambda r, n: (n, r * cpb)),
        pl.BlockSpec(blk, lambda r, n: (n, r * cpb + 1)),
        pl.BlockSpec(blk, lambda r, n: (prev(r, n), r * cpb + 1)),
        pl.BlockSpec(blk, lambda r, n: (n, r * cpb + 2)),
        pl.BlockSpec(blk, lambda r, n: (prev(r, n), r * cpb + 2)),
    ]
    out_spec = pl.BlockSpec(blk, lambda r, n: (n, r))
    o, lse = pl.pallas_call(
        body, name=f"attn_fwd_d{d}", grid=(d, nb), in_specs=in_specs, out_specs=[out_spec, out_spec],
        out_shape=[_sds((rows, d * ATTN_W), F32)] * 2, compiler_params=_cp(("parallel", "parallel")),
    )(projv, projv, projv, projv, projv)
    return o.reshape(s, ATTN_W), lse.reshape(s, ATTN_W)


def _attn_bwd(proj, do, lse, delta, d):
    s = proj.shape[0]
    rows, nb = s // d, s // d // BLK
    projv = proj.reshape(rows, d * PROJ_W)
    dov, lsev, delv = (t.reshape(rows, d * ATTN_W) for t in (do, lse, delta))
    cpb = PROJ_W // ATTN_W

    def body(qc_ref, qn_ref, kc_ref, kp_ref, vc_ref, vp_ref, dc_ref, dn_ref, lc_ref, ln_ref, ec_ref, en_ref,
             dq_ref, dk_ref, dv_ref):
        n = pl.program_id(1)
        row1 = lax.broadcasted_iota(jnp.int32, (BLK, 2 * BLK), 0)
        col1 = lax.broadcasted_iota(jnp.int32, (BLK, 2 * BLK), 1)
        dist1 = row1 + BLK - col1
        valid1 = (dist1 >= 0) & (dist1 <= BLK) & (col1 + n * BLK >= BLK)
        dist1f = dist1.astype(F32) * float(d)
        row2 = lax.broadcasted_iota(jnp.int32, (2 * BLK, BLK), 0)
        col2 = lax.broadcasted_iota(jnp.int32, (2 * BLK, BLK), 1)
        dist2 = row2 - col2
        valid2 = (dist2 >= 0) & (dist2 <= BLK) & (row2 - (nb - 1 - n) * BLK < BLK)
        dist2f = dist2.astype(F32) * float(d)
        lane1 = lax.broadcasted_iota(jnp.int32, (BLK, BLK), 1) < 64
        lane2 = lax.broadcasted_iota(jnp.int32, (2 * BLK, BLK), 1) < 64
        for p in range(ATTN_W // BLK):
            cs = slice(p * BLK, (p + 1) * BLK)
            q1, d1, l1, e1 = qc_ref[:, cs], dc_ref[:, cs], lc_ref[:, cs], ec_ref[:, cs]
            kk = jnp.concatenate([kp_ref[:, cs], kc_ref[:, cs]], axis=0)
            vv = jnp.concatenate([vp_ref[:, cs], vc_ref[:, cs]], axis=0).astype(BF16)
            qq = jnp.concatenate([q1, qn_ref[:, cs]], axis=0)
            dd = jnp.concatenate([d1, dn_ref[:, cs]], axis=0)
            ll = jnp.concatenate([l1, ln_ref[:, cs]], axis=0)
            ee = jnp.concatenate([e1, en_ref[:, cs]], axis=0)
            k1 = kc_ref[:, cs].astype(BF16)
            v1 = vc_ref[:, cs].astype(BF16)
            kkb = kk.astype(BF16)
            dq2 = jnp.zeros((BLK, BLK), F32)
            dk2 = jnp.zeros((BLK, BLK), F32)
            dv2 = jnp.zeros((BLK, BLK), F32)
            for hh in range(2):
                slope = 2.0 ** -(2 * p + hh + 1)
                m1 = lane1 if hh == 0 else ~lane1
                m2 = lane2 if hh == 0 else ~lane2
                qm = jnp.where(m1, q1, 0.0).astype(BF16)
                dm = jnp.where(m1, d1, 0.0).astype(BF16)
                sc = _dot(qm, kkb, NT) * ATTN_SCALE - slope * dist1f
                pr = jnp.where(valid1, jnp.exp(sc - _head_col(l1, m1)), 0.0)
                dp = _dot(dm, vv, NT)
                ds = pr * (dp - _head_col(e1, m1))
                km = jnp.where(m2, kk, 0.0).astype(BF16)
                dq2 = dq2 + _dot(ds.astype(BF16), km, NN) * ATTN_SCALE
                qqm = jnp.where(m2, qq, 0.0).astype(BF16)
                ddm = jnp.where(m2, dd, 0.0).astype(BF16)
                sc = _dot(qqm, k1, NT) * ATTN_SCALE - slope * dist2f
                pr = jnp.where(valid2, jnp.exp(sc - _head_col(ll, m2)), 0.0)
                dp = _dot(ddm, v1, NT)
                ds = pr * (dp - _head_col(ee, m2))
                dk2 = dk2 + _dot(ds.astype(BF16), qqm, TN) * ATTN_SCALE
                dv2 = dv2 + _dot(pr.astype(BF16), ddm, TN)
            dq_ref[:, cs] = dq2
            dk_ref[:, cs] = dk2
            dv_ref[:, cs] = dv2

    blk = (BLK, ATTN_W)
    prev = lambda n: jnp.maximum(n - 1, 0)
    nxt = lambda n: jnp.minimum(n + 1, nb - 1)
    in_specs = [
        pl.BlockSpec(blk, lambda r, n: (n, r * cpb)),
        pl.BlockSpec(blk, lambda r, n: (nxt(n), r * cpb)),
        pl.BlockSpec(blk, lambda r, n: (n, r * cpb + 1)),
        pl.BlockSpec(blk, lambda r, n: (prev(n), r * cpb + 1)),
        pl.BlockSpec(blk, lambda r, n: (n, r * cpb + 2)),
        pl.BlockSpec(blk, lambda r, n: (prev(n), r * cpb + 2)),
        pl.BlockSpec(blk, lambda r, n: (n, r)),
        pl.BlockSpec(blk, lambda r, n: (nxt(n), r)),
        pl.BlockSpec(blk, lambda r, n: (n, r)),
        pl.BlockSpec(blk, lambda r, n: (nxt(n), r)),
        pl.BlockSpec(blk, lambda r, n: (n, r)),
        pl.BlockSpec(blk, lambda r, n: (nxt(n), r)),
    ]
    out_spec = pl.BlockSpec(blk, lambda r, n: (n, r))
    res = pl.pallas_call(
        body, name=f"attn_bwd_d{d}", grid=(d, nb), in_specs=in_specs, out_specs=[out_spec] * 3,
        out_shape=[_sds((rows, d * ATTN_W), F32)] * 3, compiler_params=_cp(("parallel", "parallel")),
    )(projv, projv, projv, projv, projv, projv, dov, dov, lsev, lsev, delv, delv)
    return [t.reshape(s, ATTN_W) for t in res]


def _lower_bound(lbl):
    return 1.0 / (1.0 + jnp.exp(lbl[1:2, :] - lbl[0:1, :]))


def _hgrn_chunk(q_ref, f_ref, i_ref, sl, lb, tril, tri):
    qp = q_ref[sl, :]
    sq = _sigmoid(qp)
    qf = qp * sq
    sg = _sigmoid(f_ref[sl, :])
    f = lb + (1.0 - lb) * sg
    kf = 1.0 - f
    v = i_ref[sl, :]
    b = _dot(tri, jnp.log(f), NN, precision=lax.Precision.HIGHEST)
    bm = b[CHUNK // 2:CHUNK // 2 + 1, :]
    bl = b[CHUNK - 1:CHUNK, :]
    qt = qf * jnp.exp(b - bm)
    kt = kf * jnp.exp(bm - b)
    a = jnp.where(tril, _dot(qt.astype(BF16), kt.astype(BF16), NT), 0.0)
    return qp, sq, qf, sg, f, kf, v, b, bm, bl, qt, kt, a


def _hgrn_specs(s):
    hb = lambda base: pl.BlockSpec((s, BLK), lambda h, base=base: (0, base + h))
    first = 3 * ATTN_W // BLK
    return [hb(first + k * (HGRN_W // BLK)) for k in range(4)]


def _hgrn_fwd(proj, lb_logits, out_gain, shards):
    s = proj.shape[0]
    nc = s // CHUNK
    nh = HGRN_W // BLK
    nk = len(shards)

    def body(q_ref, f_ref, i_ref, g_ref, lbl_ref, gain_ref, *rest):
        w_refs, (o_ref, rec_ref, st_ref) = rest[:nk], rest[nk:nk + 3]
        wg_refs, (ssem, rsem) = rest[nk + 3:2 * nk + 3], rest[2 * nk + 3:]
        head = pl.program_id(0)

        @pl.when(head == 0)
        def _():
            _ag_start(w_refs, wg_refs, ssem, rsem)

        lb = _lower_bound(lbl_ref[...])
        r64 = lax.broadcasted_iota(jnp.int32, (CHUNK, CHUNK), 0)
        c64 = lax.broadcasted_iota(jnp.int32, (CHUNK, CHUNK), 1)
        tril = r64 >= c64
        tri = tril.astype(F32)

        def step(c, st):
            sl = pl.ds(pl.multiple_of(c * CHUNK, CHUNK), CHUNK)
            _, _, qf, _, _, kf, v, b, _, bl, _, _, a = _hgrn_chunk(q_ref, f_ref, i_ref, sl, lb, tril, tri)
            qe = qf * jnp.exp(b)
            o_ref[sl, :] = _dot(qe.astype(BF16), st.astype(BF16), NT) + _dot(a.astype(BF16), v.astype(BF16), NN)
            st_ref[c] = st.astype(BF16)
            kh = kf * jnp.exp(bl - b)
            return st * jnp.exp(bl) + _dot(v.astype(BF16), kh.astype(BF16), TN)

        lax.fori_loop(0, nc, step, jnp.zeros((BLK, BLK), F32))
        o = o_ref[...]
        gate = g_ref[...]
        rec_ref[...] = (o * _rstd(o) * gain_ref[...] * (gate * _sigmoid(gate))).astype(BF16)

        @pl.when(head == nh - 1)
        def _():
            _ag_finish(w_refs, wg_refs, ssem, rsem)

    hb = pl.BlockSpec((s, BLK), lambda h: (0, h))
    ag_shape, ag_sems = _ag_shapes(shards)
    return pl.pallas_call(
        body, name="hgrn_fwd", grid=(nh,),
        in_specs=_hgrn_specs(s) + [pl.BlockSpec((2, BLK), lambda h: (0, h)), pl.BlockSpec((1, BLK), lambda h: (0, h))]
        + [ANY] * nk,
        out_specs=[hb, hb, pl.BlockSpec((None, nc, BLK, BLK), lambda h: (h, 0, 0, 0))] + [ANY] * nk,
        out_shape=[_sds((s, HGRN_W), F32), _sds((s, HGRN_W), BF16), _sds((nh, nc, BLK, BLK), BF16)] + ag_shape,
        scratch_shapes=ag_sems,
        compiler_params=_cp(("arbitrary",)),
    )(proj, proj, proj, proj, lb_logits, out_gain, *shards)


def _hgrn_bwd(proj, o_pre, states, dcat, lb_logits, out_gain, psums):
    s = proj.shape[0]
    nc = s // CHUNK
    nh = HGRN_W // BLK
    nk = len(psums)

    def body(q_ref, f_ref, i_ref, g_ref, o_ref, st_ref, dy_ref, lbl_ref, gain_ref, *rest):
        p_refs, (dq_ref, df_ref, di_ref, dg_ref, dgain_ref, dlbl_ref) = rest[:nk], rest[nk:nk + 6]
        got_refs, (do_scr, ssem, rsem) = rest[nk + 6:2 * nk + 6], rest[2 * nk + 6:]
        head = pl.program_id(0)

        @pl.when(head == 0)
        def _():
            for cp in _rs_chip_copies(p_refs, got_refs, ssem, rsem):
                cp.start()

        lbl = lbl_ref[...]
        lb = _lower_bound(lbl)
        gain = gain_ref[...]
        o = o_ref[...]
        r = _rstd(o)
        nrm = o * r
        gate = g_ref[...]
        sgt = _sigmoid(gate)
        dy = dy_ref[...]
        dg_ref[...] = (dy * nrm * gain * (sgt * (1.0 + gate * (1.0 - sgt)))).astype(BF16)
        dng = dy * (gate * sgt)
        dgain_ref[...] = _colsum(dng * nrm)
        dn = dng * gain
        do_scr[...] = r * (dn - nrm * jnp.mean(dn * nrm, axis=-1, keepdims=True))

        r64 = lax.broadcasted_iota(jnp.int32, (CHUNK, CHUNK), 0)
        c64 = lax.broadcasted_iota(jnp.int32, (CHUNK, CHUNK), 1)
        tril = r64 >= c64
        tri = tril.astype(F32)
        triu = (r64 <= c64).astype(F32)

        def step(k, carry):
            dst, dlb = carry
            c = nc - 1 - k
            st_end = st_ref[jnp.minimum(c + 1, nc - 1)].astype(F32)
            csum = jnp.sum(st_end * dst, axis=0, keepdims=True)
            sl = pl.ds(pl.multiple_of(c * CHUNK, CHUNK), CHUNK)
            qp, sq, qf, sg, f, kf, v, b, bm, bl, qt, kt, a = _hgrn_chunk(q_ref, f_ref, i_ref, sl, lb, tril, tri)
            st = st_ref[c]
            doc = do_scr[sl, :]
            dob = doc.astype(BF16)
            vb = v.astype(BF16)
            dstb = dst.astype(BF16)
            eb = jnp.exp(b)
            qe = qf * eb
            kh = kf * jnp.exp(bl - b)
            hi = lax.Precision.HIGHEST
            da = jnp.where(tril, _dot(doc, v, NT, hi), 0.0)
            dqf = _dot(da, kt, NN, hi) * jnp.exp(b - bm) + eb * _dot(doc, st.astype(F32), NN, hi)
            dkf = _dot(da, qt, TN, hi) * jnp.exp(bm - b) + jnp.exp(bl - b) * _dot(v, dst, NN, hi)
            dv = _dot(a.astype(BF16), dob, TN) + _dot(kh.astype(BF16), dstb, NT)
            dst_new = dst * jnp.exp(bl) + _dot(dob, qe.astype(BF16), TN)
            gq = qf * dqf - kf * dkf
            dlogf = csum + _dot(triu, gq, NN, precision=lax.Precision.HIGHEST)
            dfv = dlogf / f - dkf
            dq_ref[sl, :] = (dqf * (sq * (1.0 + qp * (1.0 - sq)))).astype(BF16)
            df_ref[sl, :] = (dfv * (1.0 - lb) * sg * (1.0 - sg)).astype(BF16)
            di_ref[sl, :] = dv.astype(BF16)
            return dst_new, dlb + _colsum(dfv * (1.0 - sg))

        _, dlb = lax.fori_loop(0, nc, step, (jnp.zeros((BLK, BLK), F32), jnp.zeros((1, BLK), F32)))
        t = dlb * lb * (1.0 - lb)
        dlbl_ref[...] = jnp.concatenate([t, -t], axis=0)

        @pl.when(head == nh - 1)
        def _():
            for cp in _rs_chip_copies(p_refs, got_refs, ssem, rsem):
                cp.wait()

    hb = pl.BlockSpec((s, BLK), lambda h: (0, h))
    first = ATTN_W // BLK
    rs_shape, rs_sems = _rs_chips_shapes(psums)
    res = pl.pallas_call(
        body, name="hgrn_bwd", grid=(nh,),
        in_specs=_hgrn_specs(s) + [
            hb, pl.BlockSpec((None, nc, BLK, BLK), lambda h: (h, 0, 0, 0)),
            pl.BlockSpec((s, BLK), lambda h: (0, first + h)),
            pl.BlockSpec((2, BLK), lambda h: (0, h)), pl.BlockSpec((1, BLK), lambda h: (0, h))] + [ANY] * nk,
        out_specs=[hb, hb, hb, hb, pl.BlockSpec((1, BLK), lambda h: (0, h)),
                   pl.BlockSpec((2, BLK), lambda h: (0, h))] + [ANY] * nk,
        out_shape=[_sds((s, HGRN_W), BF16)] * 4 + [_sds((1, HGRN_W), F32), _sds((2, HGRN_W), F32)] + rs_shape,
        scratch_shapes=[pltpu.VMEM((s, BLK), F32)] + rs_sems,
        compiler_params=_cp(("arbitrary",)),
    )(proj, proj, proj, proj, o_pre, states, dcat, lb_logits, out_gain, *psums)
    return res


def _place():
    return lax.axis_index("x"), lax.axis_index("y"), lax.axis_index("c")


def _flip(x, y, ox, oy):
    return (1 - x if ox else x), (1 - y if oy else y)


def _half(rows, cc):
    return pl.ds(cc * (rows // 2), rows // 2)


def _remote(src, dst, ssem, rsem, to):
    return pltpu.make_async_remote_copy(src_ref=src, dst_ref=dst, send_sem=ssem, recv_sem=rsem,
                                        device_id=to, device_id_type=MESH)


def _ag_chip_copies(ins, outs, ssem, rsem):
    x, y, c = _place()
    j = 2 * x + y
    cps = []
    for k in range(len(ins)):
        rows = ins[k].shape[0]
        for idx, (ox, oy) in enumerate(FLIPS):
            px, py = _flip(x, y, ox, oy)
            cps.append(_remote(ins[k].at[_half(rows, c)], outs[k].at[j, _half(rows, c)],
                               ssem.at[k, idx], rsem.at[k, idx], (px, py, c)))
    return cps


def _ag_start(ins, outs, ssem, rsem):
    for cp in _ag_chip_copies(ins, outs, ssem, rsem):
        cp.start()


def _ag_finish(ins, outs, ssem, rsem):
    x, y, c = _place()
    sib = (x, y, 1 - c)
    passed = []
    for k in range(len(ins)):
        rows = ins[k].shape[0]
        for idx, (ox, oy) in enumerate(FLIPS):
            px, py = _flip(x, y, ox, oy)
            blk = outs[k].at[2 * px + py, _half(rows, c)]
            _remote(blk, blk, ssem.at[k, idx], rsem.at[k, idx], (px, py, c)).wait_recv()
            cp = _remote(blk, blk, ssem.at[k, 3 + idx], rsem.at[k, 3 + idx], sib)
            cp.start()
            passed.append(cp)
    for k in range(len(ins)):
        rows = ins[k].shape[0]
        for idx, (ox, oy) in enumerate(FLIPS):
            px, py = _flip(x, y, ox, oy)
            blk = outs[k].at[2 * px + py, _half(rows, 1 - c)]
            _remote(blk, blk, ssem.at[k, 3 + idx], rsem.at[k, 3 + idx], sib).wait_recv()
    for cp in _ag_chip_copies(ins, outs, ssem, rsem) + passed:
        cp.wait_send()


def _ag_shapes(shards):
    nk = len(shards)
    return ([_sds((N_CHIPS,) + tuple(w.shape), w.dtype) for w in shards],
            [pltpu.SemaphoreType.DMA((nk, 6)), pltpu.SemaphoreType.DMA((nk, 6))])


def _with_own(gathered, shard, j):
    return lax.dynamic_update_index_in_dim(gathered, shard, j, 0)


def _ag_weights(name, shards):
    nk = len(shards)

    def body(*refs):
        ins, outs = refs[:nk], refs[nk:2 * nk]
        ssem, rsem = refs[2 * nk:]
        _ag_start(ins, outs, ssem, rsem)
        _ag_finish(ins, outs, ssem, rsem)

    out_shape, sems = _ag_shapes(shards)
    return pl.pallas_call(body, name=name, in_specs=[ANY] * nk, out_specs=[ANY] * nk, out_shape=out_shape,
                          scratch_shapes=sems)(*shards)


def _rs_pair(name, grads):
    nk = len(grads)

    def body(*refs):
        ins, outs = refs[:nk], refs[nk:2 * nk]
        ssem, rsem = refs[2 * nk:]
        x, y, c = _place()
        cps = []
        for k in range(nk):
            rows = grads[k].shape[1]
            cp = pltpu.make_async_remote_copy(src_ref=ins[k].at[:, _half(rows, 1 - c)], dst_ref=outs[k],
                                              send_sem=ssem.at[k], recv_sem=rsem.at[k],
                                              device_id=(x, y, 1 - c), device_id_type=MESH)
            cp.start()
            cps.append(cp)
        for cp in cps:
            cp.wait()

    return pl.pallas_call(
        body, name=name, in_specs=[ANY] * nk, out_specs=[ANY] * nk,
        out_shape=[_sds((N_CHIPS, g.shape[1] // 2, g.shape[2]), g.dtype) for g in grads],
        scratch_shapes=[pltpu.SemaphoreType.DMA((nk,)), pltpu.SemaphoreType.DMA((nk,))],
    )(*grads)


def _rs_chip_copies(ins, outs, ssem, rsem):
    x, y, c = _place()
    cps = []
    for k in range(len(ins)):
        for idx, (ox, oy) in enumerate(FLIPS):
            px, py = _flip(x, y, ox, oy)
            cps.append(_remote(ins[k].at[2 * px + py], outs[k].at[idx], ssem.at[k, idx], rsem.at[k, idx], (px, py, c)))
    return cps


def _rs_chips_shapes(psums):
    nk = len(psums)
    return ([_sds((3,) + tuple(p.shape[1:]), p.dtype) for p in psums],
            [pltpu.SemaphoreType.DMA((nk, 3)), pltpu.SemaphoreType.DMA((nk, 3))])


def _rs_chips(name, psums):
    nk = len(psums)

    def body(*refs):
        ins, outs = refs[:nk], refs[nk:2 * nk]
        ssem, rsem = refs[2 * nk:]
        for cp in _rs_chip_copies(ins, outs, ssem, rsem):
            cp.start()
        for cp in _rs_chip_copies(ins, outs, ssem, rsem):
            cp.wait()

    out_shape, sems = _rs_chips_shapes(psums)
    return pl.pallas_call(body, name=name, in_specs=[ANY] * nk, out_specs=[ANY] * nk, out_shape=out_shape,
                          scratch_shapes=sems)(*psums)


def _rs_share(fulls):
    nk = len(fulls)

    def body(*refs):
        ins, outs = refs[:nk], refs[nk:2 * nk]
        ssem, rsem = refs[2 * nk:]
        x, y, c = _place()
        cps = []
        for k in range(nk):
            rows = fulls[k].shape[0]
            cp = _remote(ins[k].at[_half(rows, c)], outs[k].at[_half(rows, c)], ssem.at[k], rsem.at[k], (x, y, 1 - c))
            cp.start()
            cps.append(cp)
        for k, cp in enumerate(cps):
            rows = fulls[k].shape[0]
            cp.wait_send()
            theirs = outs[k].at[_half(rows, 1 - c)]
            _remote(theirs, theirs, ssem.at[k], rsem.at[k], (x, y, 1 - c)).wait_recv()

    return pl.pallas_call(
        body, name="rs_share", in_specs=[ANY] * nk, out_specs=[ANY] * nk,
        out_shape=[_sds(f.shape, f.dtype) for f in fulls], input_output_aliases={k: k for k in range(nk)},
        scratch_shapes=[pltpu.SemaphoreType.DMA((nk,)), pltpu.SemaphoreType.DMA((nk,))],
    )(*fulls)


def _allreduce_small(v):
    ndev = 8

    def body(in_ref, out_ref, buf, ssem, rsem):
        x, y, c = _place()
        me = 4 * x + 2 * y + c
        buf[me] = in_ref[...]
        cps = []
        for k in range(1, ndev):
            ox, oy, oc = (k >> 2) & 1, (k >> 1) & 1, k & 1
            px, py = _flip(x, y, ox, oy)
            pc = 1 - c if oc else c
            cp = pltpu.make_async_remote_copy(src_ref=in_ref, dst_ref=buf.at[me], send_sem=ssem.at[k - 1],
                                              recv_sem=rsem.at[k - 1], device_id=(px, py, pc), device_id_type=MESH)
            cp.start()
            cps.append((cp, 4 * px + 2 * py + pc, (px, py, pc)))
        for k, (cp, src, peer) in enumerate(cps):
            cp.wait_send()
            pltpu.make_async_remote_copy(src_ref=in_ref, dst_ref=buf.at[src], send_sem=ssem.at[k],
                                         recv_sem=rsem.at[k], device_id=peer, device_id_type=MESH).wait_recv()
        acc = buf[0]
        for i in range(1, ndev):
            acc = acc + buf[i]
        out_ref[...] = acc

    return pl.pallas_call(
        body, name="allreduce_small",
        in_specs=[pl.BlockSpec(memory_space=pltpu.VMEM)], out_specs=pl.BlockSpec(memory_space=pltpu.VMEM),
        out_shape=_sds(v.shape, v.dtype),
        scratch_shapes=[pltpu.VMEM((ndev,) + v.shape, v.dtype), pltpu.SemaphoreType.DMA((ndev - 1,)),
                        pltpu.SemaphoreType.DMA((ndev - 1,))],
    )(v)


def _rs_sum1(name, g, recv, c_idx):
    _, r, cdim = g.shape
    hr = r // 2
    tr = min(hr, 256)
    nr = hr // tr

    def body(c_ref, g_ref, r_ref, o32_ref, o16_ref):
        v = g_ref[...] + r_ref[...].astype(F32)
        o32_ref[...] = v
        o16_ref[...] = v.astype(BF16)

    spec = pl.BlockSpec((None, tr, cdim), lambda j, i, c_ref: (j, i, 0))
    return pl.pallas_call(
        body, name=name,
        grid_spec=pltpu.PrefetchScalarGridSpec(
            num_scalar_prefetch=1, grid=(N_CHIPS, nr),
            in_specs=[pl.BlockSpec((None, tr, cdim), lambda j, i, c_ref: (j, c_ref[0] * nr + i, 0)), spec],
            out_specs=[spec, spec]),
        out_shape=[_sds((N_CHIPS, hr, cdim), F32), _sds((N_CHIPS, hr, cdim), BF16)],
        compiler_params=_cp(("parallel", "parallel")),
    )(c_idx, g, recv)


def _rs_sum2(name, p32, recv, jc_idx):
    _, hr, cdim = p32.shape
    tr = min(hr, 256)
    nr = hr // tr

    def body(jc_ref, p_ref, r_ref, o_ref):
        o_ref[...] = ((p_ref[...] + r_ref[0].astype(F32)) + r_ref[1].astype(F32)) + r_ref[2].astype(F32)

    return pl.pallas_call(
        body, name=name,
        grid_spec=pltpu.PrefetchScalarGridSpec(
            num_scalar_prefetch=1, grid=(nr,),
            in_specs=[pl.BlockSpec((None, tr, cdim), lambda i, jc: (jc[0], i, 0)),
                      pl.BlockSpec((3, tr, cdim), lambda i, jc: (0, i, 0))],
            out_specs=pl.BlockSpec((tr, cdim), lambda i, jc: (jc[1] * nr + i, 0))),
        out_shape=_sds((2 * hr, cdim), F32),
        compiler_params=_cp(("parallel",)),
    )(jc_idx, p32, recv)


def _adamw(name, w, g, m, v):
    r, cdim = w.shape
    tr = min(r, 256)
    c1 = 1.0 - ADAM_B1 ** ADAM_STEP
    c2 = 1.0 - ADAM_B2 ** ADAM_STEP

    def body(w_ref, g_ref, m_ref, v_ref, d_ref, nm_ref, nv_ref):
        gv = g_ref[...]
        nm = ADAM_B1 * m_ref[...] + (1.0 - ADAM_B1) * gv
        nv = ADAM_B2 * v_ref[...] + (1.0 - ADAM_B2) * (gv * gv)
        d_ref[...] = -ADAM_LR * ((nm / c1) / (jnp.sqrt(nv / c2) + ADAM_EPS) + ADAM_WD * w_ref[...])
        nm_ref[...] = nm
        nv_ref[...] = nv

    spec = pl.BlockSpec((tr, cdim), lambda i: (i, 0))
    return pl.pallas_call(
        body, name=name, grid=(r // tr,), in_specs=[spec] * 4, out_specs=[spec] * 3,
        out_shape=[_sds((r, cdim), F32)] * 3, compiler_params=_cp(("parallel",)),
    )(w, g, m, v)


def _pack_small(mix_pre, attn_out, lb_logits, hgrn_out, mix_post, mlp_pre, mlp_post):
    rows = [mix_pre, jnp.concatenate([attn_out, hgrn_out], axis=1),
            jnp.concatenate([lb_logits[0:1], lb_logits[1:2]], axis=1), mix_post, mlp_pre, mlp_post,
            jnp.zeros((2, D_MODEL), F32)]
    return jnp.concatenate(rows, axis=0)


def _unpack_small(p):
    return (p[0:1], p[1:2, :ATTN_W], jnp.concatenate([p[2:3, :HGRN_W], p[2:3, HGRN_W:]], axis=0),
            p[1:2, ATTN_W:], p[3:4], p[4:5], p[5:6])


def kernel(x, mix_pre_norm, w_in, attn_out_norm, hgrn_lb_logits, hgrn_out_norm, w_out, mix_post_norm, mlp_pre_norm, w_ff1, w_ff2, mlp_post_norm, loss_target, m_mix_pre_norm, m_w_in, m_attn_out_norm, m_hgrn_lb_logits, m_hgrn_out_norm, m_w_out, m_mix_post_norm, m_mlp_pre_norm, m_w_ff1, m_w_ff2, m_mlp_post_norm, v_mix_pre_norm, v_w_in, v_attn_out_norm, v_hgrn_lb_logits, v_hgrn_out_norm, v_w_out, v_mix_post_norm, v_mlp_pre_norm, v_w_ff1, v_w_ff2, v_mlp_post_norm):
    s = x.shape[1]
    xs = x.reshape(s, D_MODEL)
    tgt = loss_target.reshape(s, D_MODEL)
    cx, cy, cc = _place()
    chip = 2 * cx + cy
    c_idx = jnp.reshape(cc, (1,)).astype(jnp.int32)
    jc_idx = jnp.stack([chip, cc]).astype(jnp.int32)

    big_w = [w_in[0], w_out[0], w_ff1[0], w_ff2[0]]
    big_m = [m_w_in[0], m_w_out[0], m_w_ff1[0], m_w_ff2[0]]
    big_v = [v_w_in[0], v_w_out[0], v_w_ff1[0], v_w_ff2[0]]
    shards = [w.astype(BF16) for w in big_w]

    (wg_in,) = _ag_weights("ag_in", shards[:1])
    wg_in = _with_own(wg_in, shards[0], chip)

    (h,) = _rows_call("norm_in", lambda xv, g: ((xv * _rstd(xv) * g),),
                      [(xs, _row(D_MODEL)), (mix_pre_norm, "full")], [(D_MODEL, BF16, "row")], s)
    (proj,) = _mm_cols("mm_proj", h, wg_in, NN, [F32])
    hg_o, rec, states, wg_out, wg_1, wg_2 = _hgrn_fwd(proj, hgrn_lb_logits, hgrn_out_norm, shards[1:])
    wg_out, wg_1, wg_2 = (_with_own(g, w, chip) for g, w in zip((wg_out, wg_1, wg_2), shards[1:]))
    att = [_attn_fwd(proj, d) for d in DILATIONS]

    def comb(o1, o2, o3, l1, l2, l3, gain):
        mx = jnp.maximum(jnp.maximum(l1, l2), l3)
        lse = mx + jnp.log(jnp.exp(l1 - mx) + jnp.exp(l2 - mx) + jnp.exp(l3 - mx))
        o = jnp.exp(l1 - lse) * o1 + jnp.exp(l2 - lse) * o2 + jnp.exp(l3 - lse) * o3
        return o, lse, o * _rstd(o) * gain

    attn_o, attn_lse, attn_n = _rows_call(
        "attn_comb", comb, [(a[0], _row(ATTN_W)) for a in att] + [(a[1], _row(ATTN_W)) for a in att]
        + [(attn_out_norm, "full")], [(ATTN_W, F32, "row"), (ATTN_W, F32, "row"), (ATTN_W, BF16, "row")], s)
    cat = jnp.concatenate([attn_n, rec], axis=1)
    mixed = _mm_acc("mm_mixed", cat, wg_out, NN)

    def post1(xv, mv, g_post, g_pre2):
        x1 = xv + mv * _rstd(mv) * g_post
        return x1, x1 * _rstd(x1) * g_pre2

    x1, h2 = _rows_call("post1", post1, [(xs, _row(D_MODEL)), (mixed, _row(D_MODEL)), (mix_post_norm, "full"),
                                         (mlp_pre_norm, "full")], [(D_MODEL, F32, "row"), (D_MODEL, BF16, "row")], s)

    def sq_relu(u):
        r = jnp.maximum(u, 0.0)
        return r * r, r

    act, ru = _mm_cols("mm_ff1", h2, wg_1, NN, [BF16, BF16], epi=sq_relu)
    ff = _mm_acc("mm_ff2", act, wg_2, NN)

    def post2(x1v, fv, tv, g):
        y = x1v + fv * _rstd(fv) * g
        dy = (y - tv) * (1.0 / D_MODEL)
        err = y - tv
        loss = 0.5 * jnp.sum(jnp.mean(err * err, axis=-1, keepdims=True), axis=0, keepdims=True)
        dff, dgc = _norm_bwd(fv, g, dy)
        return dy, dff, _colsum(dgc), jnp.broadcast_to(loss, (1, BLK))

    dy, dff, g_mlp_post, loss_part = _rows_call(
        "post2", post2, [(x1, _row(D_MODEL)), (ff, _row(D_MODEL)), (tgt, _row(D_MODEL)), (mlp_post_norm, "full")],
        [(D_MODEL, F32, "row"), (D_MODEL, BF16, "row"), (D_MODEL, F32, "acc"), (BLK, F32, "acc")], s)

    (du,) = _mm_cols("mm_du", dff, wg_2, NT, [BF16], epi=lambda acc, r: (acc * (2.0 * r.astype(F32)),),
                     extras=(ru,))
    gw_2 = _mm_wgrad("mm_gw2", act, dff, True)
    gw_1 = _mm_wgrad("mm_gw1", h2, du, False)
    dh2 = _mm_acc("mm_dh2", du, wg_1, NT)

    def bwd_mid(dyv, dh2v, x1v, mv, g_pre2, g_post):
        d1, gc1 = _norm_bwd(x1v, g_pre2, dh2v)
        dx1 = dyv + d1
        dm, gc2 = _norm_bwd(mv, g_post, dx1)
        return dx1, dm, _colsum(gc1), _colsum(gc2)

    dx1, dmixed, g_mlp_pre, g_mix_post = _rows_call(
        "bwd_mid", bwd_mid, [(dy, _row(D_MODEL)), (dh2, _row(D_MODEL)), (x1, _row(D_MODEL)), (mixed, _row(D_MODEL)),
                             (mlp_pre_norm, "full"), (mix_post_norm, "full")],
        [(D_MODEL, F32, "row"), (D_MODEL, BF16, "row"), (D_MODEL, F32, "acc"), (D_MODEL, F32, "acc")], s)

    (dcat,) = _mm_cols("mm_dcat", dmixed, wg_out, NT, [F32])
    gw_out = _mm_wgrad("mm_gwout", cat, dmixed, True)

    names = ["out", "ff1", "ff2", "in"]
    ready = [gw_out, gw_1, gw_2]
    from_pair = _rs_pair("rs_pair_3", [g.astype(BF16) for g in ready])
    pair = [_rs_sum1(f"rs_sum1_{n}", g, r, c_idx) for n, g, r in zip(names, ready, from_pair)]

    def attn_norm_bwd(dc, o, gain):
        do, gc = _norm_bwd(o, gain, dc)
        t = do * o
        lane = lax.broadcasted_iota(jnp.int32, (t.shape[0], BLK), 1) < 64
        parts = []
        for p in range(ATTN_W // BLK):
            tp = t[:, p * BLK:(p + 1) * BLK]
            sa = jnp.sum(jnp.where(lane, tp, 0.0), axis=1, keepdims=True)
            sb = jnp.sum(jnp.where(lane, 0.0, tp), axis=1, keepdims=True)
            parts.append(jnp.where(lane, sa, sb))
        return do, jnp.concatenate(parts, axis=1), _colsum(gc)

    do_attn, delta, g_attn_out = _rows_call(
        "attn_norm_bwd", attn_norm_bwd, [(dcat, _row(ATTN_W, 0)), (attn_o, _row(ATTN_W)), (attn_out_norm, "full")],
        [(ATTN_W, F32, "row"), (ATTN_W, F32, "row"), (ATTN_W, F32, "acc")], s)
    dqkv = [_attn_bwd(proj, do_attn, attn_lse, delta, d) for d in DILATIONS]
    dhq, dhf, dhi, dhg, g_hgrn_out, g_lb, *from_chips = _hgrn_bwd(
        proj, hg_o, states, dcat, hgrn_lb_logits, hgrn_out_norm, [p[1] for p in pair])
    dh_parts = [dhq, dhf, dhi, dhg]

    def dproj_asm(*a):
        a1, a4, a16 = a[0:3], a[3:6], a[6:9]
        return (jnp.concatenate([(a1[k] + a4[k] + a16[k]).astype(BF16) for k in range(3)] + list(a[9:]), axis=1),)

    (dproj,) = _rows_call("dproj_asm", dproj_asm,
                          [(t, _row(ATTN_W)) for grp in dqkv for t in grp] + [(t, _row(HGRN_W)) for t in dh_parts],
                          [(PROJ_W, BF16, "row")], s)
    dh = _mm_acc("mm_dh", dproj, wg_in, NT)
    gw_in = _mm_wgrad("mm_gwin", h, dproj, False)

    def bwd_in(dx1v, dhv, xv, g):
        d0, gc = _norm_bwd(xv, g, dhv)
        return dx1v + d0, _colsum(gc)

    grad_x, g_mix_pre = _rows_call("bwd_in", bwd_in, [(dx1, _row(D_MODEL)), (dh, _row(D_MODEL)), (xs, _row(D_MODEL)),
                                                      (mix_pre_norm, "full")],
                                   [(D_MODEL, F32, "row"), (D_MODEL, F32, "acc")], s)

    loss = lax.psum(loss_part[0, 0], ("x", "y", "c"))
    small_g = _allreduce_small(_pack_small(g_mix_pre, g_attn_out, g_lb, g_hgrn_out, g_mix_post, g_mlp_pre, g_mlp_post))

    (from_pair_in,) = _rs_pair("rs_pair_in", [gw_in.astype(BF16)])
    pair.append(_rs_sum1("rs_sum1_in", gw_in, from_pair_in, c_idx))
    from_chips += _rs_chips("rs_chips_in", [pair[3][1]])
    reduced = [_rs_sum2(f"rs_sum2_{n}", p[0], r, jc_idx) for n, p, r in zip(names, pair, from_chips)]
    g_wout, g_w1, g_w2, g_win = _rs_share(reduced)
    full = [g_win, g_wout, g_w1, g_w2]

    upd = [_adamw(f"adamw_{n}", w, g, m, v) for n, w, g, m, v in zip(("in", "out", "ff1", "ff2"), big_w, full, big_m, big_v)]
    small_w = _pack_small(mix_pre_norm, attn_out_norm, hgrn_lb_logits, hgrn_out_norm, mix_post_norm, mlp_pre_norm,
                          mlp_post_norm)
    small_m = _pack_small(m_mix_pre_norm, m_attn_out_norm, m_hgrn_lb_logits, m_hgrn_out_norm, m_mix_post_norm,
                          m_mlp_pre_norm, m_mlp_post_norm)
    small_v = _pack_small(v_mix_pre_norm, v_attn_out_norm, v_hgrn_lb_logits, v_hgrn_out_norm, v_mix_post_norm,
                          v_mlp_pre_norm, v_mlp_post_norm)
    small_upd = _adamw("adamw_small", small_w, small_g, small_m, small_v)

    def assemble(small, big):
        sm = _unpack_small(small)
        return (sm[0], big[0][None], sm[1], sm[2], sm[3], big[1][None], sm[4], sm[5], big[2][None], big[3][None], sm[6])

    g_out = assemble(small_g, full)
    d_out = assemble(small_upd[0], [u[0] for u in upd])
    m_out = assemble(small_upd[1], [u[1] for u in upd])
    v_out = assemble(small_upd[2], [u[2] for u in upd])
    return (loss, grad_x.reshape(x.shape), *g_out, *d_out, *m_out, *v_out)
```

```python
import jax
import jax.numpy as jnp
from jax import lax
from jax.experimental import pallas as pl
from jax.experimental.pallas import tpu as pltpu

F32 = jnp.float32
BF16 = jnp.bfloat16
MESH = pl.DeviceIdType.MESH
ANY = pl.BlockSpec(memory_space=pl.ANY)

RMS_EPS = 1e-6
D_MODEL = 1024
ATTN_W = 512
HGRN_W = 512
PROJ_W = 3584
D_FF = 4096
N_CHIPS = 4
BLK = 128
CHUNK = 64
HGRN_UNROLL = 2
DILATIONS = (1, 4, 16)
ATTN_SCALE = 0.125
ROW_TILE = 512
MM_TILE = 1024
VMEM_LIMIT = 48 * 2 ** 20
FLIPS = ((1, 0), (0, 1), (1, 1))

ADAM_LR, ADAM_B1, ADAM_B2, ADAM_EPS, ADAM_WD, ADAM_STEP = 0.001, 0.9, 0.999, 1e-08, 0.01, 10


def _cp(sem=None):
    return pltpu.CompilerParams(dimension_semantics=sem, vmem_limit_bytes=VMEM_LIMIT)


def _sigmoid(v):
    return 1.0 / (1.0 + jnp.exp(-v))


def _dot(a, b, contract, precision=None):
    return lax.dot_general(a, b, (contract, ((), ())), preferred_element_type=F32, precision=precision)


NN = ((1,), (0,))
NT = ((1,), (1,))
TN = ((0,), (0,))


def _matmul(name, a, b, *, grid, a_spec, b_spec, contract, outs, epi=None, extras=(), extra_specs=(),
            acc_shape=None):
    n_ex, n_out, nj = len(extras), len(outs), grid[-1]

    def body(a_ref, b_ref, *rest):
        ex, out_refs = rest[:n_ex], rest[n_ex:n_ex + n_out]

        def finish(acc):
            res = epi(acc, *[e[...] for e in ex]) if epi else (acc,)
            for o, r in zip(out_refs, res):
                o[...] = r.astype(o.dtype)

        p = _dot(a_ref[...], b_ref[...], contract)
        if acc_shape is None:
            finish(p)
        else:
            acc_ref = rest[-1]
            j = pl.program_id(len(grid) - 1)

            @pl.when(j == 0)
            def _():
                acc_ref[...] = p

            @pl.when(j > 0)
            def _():
                acc_ref[...] += p

            @pl.when(j == nj - 1)
            def _():
                finish(acc_ref[...])

    sem = ("parallel",) * len(grid) if acc_shape is None else ("parallel",) * (len(grid) - 1) + ("arbitrary",)
    res = pl.pallas_call(
        body, name=name, grid=grid,
        in_specs=[a_spec, b_spec, *extra_specs],
        out_specs=[s for _, s in outs],
        out_shape=[s for s, _ in outs],
        scratch_shapes=[] if acc_shape is None else [pltpu.VMEM(acc_shape, F32)],
        compiler_params=_cp(sem),
    )(a, b, *extras)
    return res


def _sds(shape, dtype):
    return jax.ShapeDtypeStruct(shape, dtype)


def _mm_cols(name, a, w, contract, out_dtypes, epi=None, extras=()):
    m, k = a.shape
    jn = w.shape[0]
    nj = w.shape[2] if contract == NN else w.shape[1]
    tm = min(m, MM_TILE)
    outs = [(_sds((m, jn * nj), dt), pl.BlockSpec((tm, nj), lambda j, i: (i, j))) for dt in out_dtypes]
    return _matmul(name, a, w, grid=(jn, m // tm),
                   a_spec=pl.BlockSpec((tm, k), lambda j, i: (i, 0)),
                   b_spec=pl.BlockSpec((None,) + w.shape[1:], lambda j, i: (j, 0, 0)),
                   contract=contract, outs=outs, epi=epi, extras=extras,
                   extra_specs=[pl.BlockSpec((tm, nj), lambda j, i: (i, j)) for _ in extras])


def _mm_acc(name, a, w, contract):
    m = a.shape[0]
    jn = w.shape[0]
    kj = w.shape[1] if contract == NN else w.shape[2]
    n = w.shape[2] if contract == NN else w.shape[1]
    tm = min(m, MM_TILE)
    outs = [(_sds((m, n), F32), pl.BlockSpec((tm, n), lambda i, j: (i, 0)))]
    return _matmul(name, a, w, grid=(m // tm, jn),
                   a_spec=pl.BlockSpec((tm, kj), lambda i, j: (i, j)),
                   b_spec=pl.BlockSpec((None,) + w.shape[1:], lambda i, j: (j, 0, 0)),
                   contract=contract, outs=outs, acc_shape=(tm, n))[0]


def _mm_wgrad(name, a, b, a_by_j):
    s = a.shape[0]
    if a_by_j:
        r, c = a.shape[1] // N_CHIPS, b.shape[1]
    else:
        r, c = a.shape[1], b.shape[1] // N_CHIPS
    tr = min(r, 512)
    nr = r // tr
    if a_by_j:
        a_spec = pl.BlockSpec((s, tr), lambda j, i: (0, j * nr + i))
        b_spec = pl.BlockSpec((s, c), lambda j, i: (0, 0))
    else:
        a_spec = pl.BlockSpec((s, tr), lambda j, i: (0, i))
        b_spec = pl.BlockSpec((s, c), lambda j, i: (0, j))
    outs = [(_sds((N_CHIPS, r, c), F32), pl.BlockSpec((None, tr, c), lambda j, i: (j, i, 0)))]
    return _matmul(name, a, b, grid=(N_CHIPS, nr), a_spec=a_spec, b_spec=b_spec, contract=TN, outs=outs)[0]


def _rows_call(name, fn, ins, outs, s):
    tm = ROW_TILE
    in_specs = []
    for arr, kind in ins:
        if kind == "full":
            in_specs.append(pl.BlockSpec(arr.shape, lambda i: (0, 0)))
        else:
            _, w, cb = kind
            in_specs.append(pl.BlockSpec((tm, w), lambda i, cb=cb: (i, cb)))
    out_specs, out_shape, is_acc = [], [], []
    for w, dt, kind in outs:
        if kind == "acc":
            out_specs.append(pl.BlockSpec((1, w), lambda i: (0, 0)))
            out_shape.append(_sds((1, w), dt))
        else:
            out_specs.append(pl.BlockSpec((tm, w), lambda i: (i, 0)))
            out_shape.append(_sds((s, w), dt))
        is_acc.append(kind == "acc")
    n_in = len(ins)

    def body(*refs):
        i = pl.program_id(0)
        res = fn(*[r[...] for r in refs[:n_in]])
        for o, r, acc in zip(refs[n_in:], res, is_acc):
            if acc:
                @pl.when(i == 0)
                def _(o=o):
                    o[...] = jnp.zeros_like(o)
                o[...] += r.astype(o.dtype)
            else:
                o[...] = r.astype(o.dtype)

    sem = ("arbitrary",) if any(is_acc) else ("parallel",)
    return pl.pallas_call(body, name=name, grid=(s // tm,), in_specs=in_specs, out_specs=out_specs,
                          out_shape=out_shape, compiler_params=_cp(sem))(*[a for a, _ in ins])


def _rstd(v):
    return lax.rsqrt(jnp.mean(v * v, axis=-1, keepdims=True) + RMS_EPS)


def _norm_bwd(v, gain, dy):
    r = _rstd(v)
    n = v * r
    dn = dy * gain
    dv = r * (dn - n * jnp.mean(dn * n, axis=-1, keepdims=True))
    return dv, dy * n


def _colsum(v):
    return jnp.sum(v, axis=0, keepdims=True)


def _row(w, cb=0):
    return ("row", w, cb)


def _head_col(v, mask):
    return jnp.max(jnp.where(mask, v, -jnp.inf), axis=1, keepdims=True)


def _attn_fwd(proj, d):
    s = proj.shape[0]
    rows, nb = s // d, s // d // BLK
    projv = proj.reshape(rows, d * PROJ_W)
    cpb = PROJ_W // ATTN_W

    def body(q_ref, kc_ref, kp_ref, vc_ref, vp_ref, o_ref, l_ref):
        n = pl.program_id(1)
        row = lax.broadcasted_iota(jnp.int32, (BLK, 2 * BLK), 0)
        col = lax.broadcasted_iota(jnp.int32, (BLK, 2 * BLK), 1)
        dist = row + BLK - col
        valid = (dist >= 0) & (dist <= BLK) & (col + n * BLK >= BLK)
        distf = dist.astype(F32) * float(d)
        lane_q = lax.broadcasted_iota(jnp.int32, (BLK, BLK), 1) < 64
        lane_k = lax.broadcasted_iota(jnp.int32, (2 * BLK, BLK), 1) < 64
        for p in range(ATTN_W // BLK):
            cs = slice(p * BLK, (p + 1) * BLK)
            q2 = q_ref[:, cs]
            kk = jnp.concatenate([kp_ref[:, cs], kc_ref[:, cs]], axis=0).astype(BF16)
            vv = jnp.concatenate([vp_ref[:, cs], vc_ref[:, cs]], axis=0)
            o2 = jnp.zeros((BLK, BLK), F32)
            lse2 = jnp.zeros((BLK, BLK), F32)
            for hh in range(2):
                slope = 2.0 ** -(2 * p + hh + 1)
                mq = lane_q if hh == 0 else ~lane_q
                mk = lane_k if hh == 0 else ~lane_k
                qm = jnp.where(mq, q2, 0.0).astype(BF16)
                sc = _dot(qm, kk, NT) * ATTN_SCALE - slope * distf
                sc = jnp.where(valid, sc, -1e30)
                m = jnp.max(sc, axis=1, keepdims=True)
                pr = jnp.exp(sc - m)
                den = jnp.sum(pr, axis=1, keepdims=True)
                vm = jnp.where(mk, vv, 0.0).astype(BF16)
                o2 = o2 + _dot(pr.astype(BF16), vm, NN) / den
                lse2 = jnp.where(mq, m + jnp.log(den), lse2)
            o_ref[:, cs] = o2
            l_ref[:, cs] = lse2

    blk = (BLK, ATTN_W)
    prev = lambda r, n: jnp.maximum(n - 1, 0)
    in_specs = [
        pl.BlockSpec(blk, lambda r, n: (n, r * cpb)),
        pl.BlockSpec(blk, lambda r, n: (n, r * cpb + 1)),
        pl.BlockSpec(blk, lambda r, n: (prev(r, n), r * cpb + 1)),
        pl.BlockSpec(blk, lambda r, n: (n, r * cpb + 2)),
        pl.BlockSpec(blk, lambda r, n: (prev(r, n), r * cpb + 2)),
    ]
    out_spec = pl.BlockSpec(blk, lambda r, n: (n, r))
    o, lse = pl.pallas_call(
        body, name=f"attn_fwd_d{d}", grid=(d, nb), in_specs=in_specs, out_specs=[out_spec, out_spec],
        out_shape=[_sds((rows, d * ATTN_W), F32)] * 2, compiler_params=_cp(("parallel", "parallel")),
    )(projv, projv, projv, projv, projv)
    return o.reshape(s, ATTN_W), lse.reshape(s, ATTN_W)


def _attn_bwd(proj, do, lse, delta, d):
    s = proj.shape[0]
    rows, nb = s // d, s // d // BLK
    projv = proj.reshape(rows, d * PROJ_W)
    dov, lsev, delv = (t.reshape(rows, d * ATTN_W) for t in (do, lse, delta))
    cpb = PROJ_W // ATTN_W

    def body(qc_ref, qn_ref, kc_ref, kp_ref, vc_ref, vp_ref, dc_ref, dn_ref, lc_ref, ln_ref, ec_ref, en_ref,
             dq_ref, dk_ref, dv_ref):
        n = pl.program_id(1)
        row1 = lax.broadcasted_iota(jnp.int32, (BLK, 2 * BLK), 0)
        col1 = lax.broadcasted_iota(jnp.int32, (BLK, 2 * BLK), 1)
        dist1 = row1 + BLK - col1
        valid1 = (dist1 >= 0) & (dist1 <= BLK) & (col1 + n * BLK >= BLK)
        dist1f = dist1.astype(F32) * float(d)
        row2 = lax.broadcasted_iota(jnp.int32, (2 * BLK, BLK), 0)
        col2 = lax.broadcasted_iota(jnp.int32, (2 * BLK, BLK), 1)
        dist2 = row2 - col2
        valid2 = (dist2 >= 0) & (dist2 <= BLK) & (row2 - (nb - 1 - n) * BLK < BLK)
        dist2f = dist2.astype(F32) * float(d)
        lane1 = lax.broadcasted_iota(jnp.int32, (BLK, BLK), 1) < 64
        lane2 = lax.broadcasted_iota(jnp.int32, (2 * BLK, BLK), 1) < 64
        for p in range(ATTN_W // BLK):
            cs = slice(p * BLK, (p + 1) * BLK)
            q1, d1, l1, e1 = qc_ref[:, cs], dc_ref[:, cs], lc_ref[:, cs], ec_ref[:, cs]
            kk = jnp.concatenate([kp_ref[:, cs], kc_ref[:, cs]], axis=0)
            vv = jnp.concatenate([vp_ref[:, cs], vc_ref[:, cs]], axis=0).astype(BF16)
            qq = jnp.concatenate([q1, qn_ref[:, cs]], axis=0)
            dd = jnp.concatenate([d1, dn_ref[:, cs]], axis=0)
            ll = jnp.concatenate([l1, ln_ref[:, cs]], axis=0)
            ee = jnp.concatenate([e1, en_ref[:, cs]], axis=0)
            k1 = kc_ref[:, cs].astype(BF16)
            v1 = vc_ref[:, cs].astype(BF16)
            kkb = kk.astype(BF16)
            dq2 = jnp.zeros((BLK, BLK), F32)
            dk2 = jnp.zeros((BLK, BLK), F32)
            dv2 = jnp.zeros((BLK, BLK), F32)
            for hh in range(2):
                slope = 2.0 ** -(2 * p + hh + 1)
                m1 = lane1 if hh == 0 else ~lane1
                m2 = lane2 if hh == 0 else ~lane2
                qm = jnp.where(m1, q1, 0.0).astype(BF16)
                dm = jnp.where(m1, d1, 0.0).astype(BF16)
                sc = _dot(qm, kkb, NT) * ATTN_SCALE - slope * dist1f
                pr = jnp.where(valid1, jnp.exp(sc - _head_col(l1, m1)), 0.0)
                dp = _dot(dm, vv, NT)
                ds = pr * (dp - _head_col(e1, m1))
                km = jnp.where(m2, kk, 0.0).astype(BF16)
                dq2 = dq2 + _dot(ds.astype(BF16), km, NN) * ATTN_SCALE
                qqm = jnp.where(m2, qq, 0.0).astype(BF16)
                ddm = jnp.where(m2, dd, 0.0).astype(BF16)
                sc = _dot(qqm, k1, NT) * ATTN_SCALE - slope * dist2f
                pr = jnp.where(valid2, jnp.exp(sc - _head_col(ll, m2)), 0.0)
                dp = _dot(ddm, v1, NT)
                ds = pr * (dp - _head_col(ee, m2))
                dk2 = dk2 + _dot(ds.astype(BF16), qqm, TN) * ATTN_SCALE
                dv2 = dv2 + _dot(pr.astype(BF16), ddm, TN)
            dq_ref[:, cs] = dq2
            dk_ref[:, cs] = dk2
            dv_ref[:, cs] = dv2

    blk = (BLK, ATTN_W)
    prev = lambda n: jnp.maximum(n - 1, 0)
    nxt = lambda n: jnp.minimum(n + 1, nb - 1)
    in_specs = [
        pl.BlockSpec(blk, lambda r, n: (n, r * cpb)),
        pl.BlockSpec(blk, lambda r, n: (nxt(n), r * cpb)),
        pl.BlockSpec(blk, lambda r, n: (n, r * cpb + 1)),
        pl.BlockSpec(blk, lambda r, n: (prev(n), r * cpb + 1)),
        pl.BlockSpec(blk, lambda r, n: (n, r * cpb + 2)),
        pl.BlockSpec(blk, lambda r, n: (prev(n), r * cpb + 2)),
        pl.BlockSpec(blk, lambda r, n: (n, r)),
        pl.BlockSpec(blk, lambda r, n: (nxt(n), r)),
        pl.BlockSpec(blk, lambda r, n: (n, r)),
        pl.BlockSpec(blk, lambda r, n: (nxt(n), r)),
        pl.BlockSpec(blk, lambda r, n: (n, r)),
        pl.BlockSpec(blk, lambda r, n: (nxt(n), r)),
    ]
    out_spec = pl.BlockSpec(blk, lambda r, n: (n, r))
    res = pl.pallas_call(
        body, name=f"attn_bwd_d{d}", grid=(d, nb), in_specs=in_specs, out_specs=[out_spec] * 3,
        out_shape=[_sds((rows, d * ATTN_W), F32)] * 3, compiler_params=_cp(("parallel", "parallel")),
    )(projv, projv, projv, projv, projv, projv, dov, dov, lsev, lsev, delv, delv)
    return [t.reshape(s, ATTN_W) for t in res]


def _lower_bound(lbl):
    return 1.0 / (1.0 + jnp.exp(lbl[1:2, :] - lbl[0:1, :]))


def _hgrn_chunk(q_ref, f_ref, i_ref, sl, lb, tril, tri):
    qp = q_ref[sl, :]
    sq = _sigmoid(qp)
    qf = qp * sq
    sg = _sigmoid(f_ref[sl, :])
    f = lb + (1.0 - lb) * sg
    kf = 1.0 - f
    v = i_ref[sl, :]
    b = _dot(tri, jnp.log(f), NN, precision=lax.Precision.HIGHEST)
    bm = b[CHUNK // 2:CHUNK // 2 + 1, :]
    bl = b[CHUNK - 1:CHUNK, :]
    qt = qf * jnp.exp(b - bm)
    kt = kf * jnp.exp(bm - b)
    a = jnp.where(tril, _dot(qt.astype(BF16), kt.astype(BF16), NT), 0.0)
    return qp, sq, qf, sg, f, kf, v, b, bm, bl, qt, kt, a


def _hgrn_specs(s):
    hb = lambda base: pl.BlockSpec((s, BLK), lambda h, base=base: (0, base + h))
    first = 3 * ATTN_W // BLK
    return [hb(first + k * (HGRN_W // BLK)) for k in range(4)]


def _hgrn_fwd(proj, lb_logits, out_gain, shards):
    s = proj.shape[0]
    nc = s // CHUNK
    nh = HGRN_W // BLK
    nk = len(shards)

    def body(q_ref, f_ref, i_ref, g_ref, lbl_ref, gain_ref, *rest):
        w_refs, (o_ref, rec_ref, st_ref) = rest[:nk], rest[nk:nk + 3]
        wg_refs, (ssem, rsem) = rest[nk + 3:2 * nk + 3], rest[2 * nk + 3:]
        head = pl.program_id(0)

        @pl.when(head == 0)
        def _():
            _ag_start(w_refs, wg_refs, ssem, rsem)

        lb = _lower_bound(lbl_ref[...])
        r64 = lax.broadcasted_iota(jnp.int32, (CHUNK, CHUNK), 0)
        c64 = lax.broadcasted_iota(jnp.int32, (CHUNK, CHUNK), 1)
        tril = r64 >= c64
        tri = tril.astype(F32)

        def step(c, st):
            sl = pl.ds(pl.multiple_of(c * CHUNK, CHUNK), CHUNK)
            _, _, qf, _, _, kf, v, b, _, bl, _, _, a = _hgrn_chunk(q_ref, f_ref, i_ref, sl, lb, tril, tri)
            qe = qf * jnp.exp(b)
            o_ref[sl, :] = _dot(qe.astype(BF16), st.astype(BF16), NT) + _dot(a.astype(BF16), v.astype(BF16), NN)
            st_ref[c] = st.astype(BF16)
            kh = kf * jnp.exp(bl - b)
            return st * jnp.exp(bl) + _dot(v.astype(BF16), kh.astype(BF16), TN)

        def trip(t, st):
            for u in range(HGRN_UNROLL):
                st = step(t * HGRN_UNROLL + u, st)
            return st

        lax.fori_loop(0, nc // HGRN_UNROLL, trip, jnp.zeros((BLK, BLK), F32))
        o = o_ref[...]
        gate = g_ref[...]
        rec_ref[...] = (o * _rstd(o) * gain_ref[...] * (gate * _sigmoid(gate))).astype(BF16)

        @pl.when(head == nh - 1)
        def _():
            _ag_finish(w_refs, wg_refs, ssem, rsem)

    hb = pl.BlockSpec((s, BLK), lambda h: (0, h))
    ag_shape, ag_sems = _ag_shapes(shards)
    return pl.pallas_call(
        body, name="hgrn_fwd", grid=(nh,),
        in_specs=_hgrn_specs(s) + [pl.BlockSpec((2, BLK), lambda h: (0, h)), pl.BlockSpec((1, BLK), lambda h: (0, h))]
        + [ANY] * nk,
        out_specs=[hb, hb, pl.BlockSpec((None, nc, BLK, BLK), lambda h: (h, 0, 0, 0))] + [ANY] * nk,
        out_shape=[_sds((s, HGRN_W), F32), _sds((s, HGRN_W), BF16), _sds((nh, nc, BLK, BLK), BF16)] + ag_shape,
        scratch_shapes=ag_sems,
        compiler_params=_cp(("arbitrary",)),
    )(proj, proj, proj, proj, lb_logits, out_gain, *shards)


def _hgrn_bwd(proj, o_pre, states, dcat, lb_logits, out_gain, psums):
    s = proj.shape[0]
    nc = s // CHUNK
    nh = HGRN_W // BLK
    nk = len(psums)

    def body(q_ref, f_ref, i_ref, g_ref, o_ref, st_ref, dy_ref, lbl_ref, gain_ref, *rest):
        p_refs, (dq_ref, df_ref, di_ref, dg_ref, dgain_ref, dlbl_ref) = rest[:nk], rest[nk:nk + 6]
        got_refs, (do_scr, ssem, rsem) = rest[nk + 6:2 * nk + 6], rest[2 * nk + 6:]
        head = pl.program_id(0)

        @pl.when(head == 0)
        def _():
            for cp in _rs_chip_copies(p_refs, got_refs, ssem, rsem):
                cp.start()

        lbl = lbl_ref[...]
        lb = _lower_bound(lbl)
        gain = gain_ref[...]
        o = o_ref[...]
        r = _rstd(o)
        nrm = o * r
        gate = g_ref[...]
        sgt = _sigmoid(gate)
        dy = dy_ref[...]
        dg_ref[...] = (dy * nrm * gain * (sgt * (1.0 + gate * (1.0 - sgt)))).astype(BF16)
        dng = dy * (gate * sgt)
        dgain_ref[...] = _colsum(dng * nrm)
        dn = dng * gain
        do_scr[...] = r * (dn - nrm * jnp.mean(dn * nrm, axis=-1, keepdims=True))

        r64 = lax.broadcasted_iota(jnp.int32, (CHUNK, CHUNK), 0)
        c64 = lax.broadcasted_iota(jnp.int32, (CHUNK, CHUNK), 1)
        tril = r64 >= c64
        tri = tril.astype(F32)
        triu = (r64 <= c64).astype(F32)

        def step(k, carry):
            dst, dlb = carry
            c = nc - 1 - k
            st_end = st_ref[jnp.minimum(c + 1, nc - 1)].astype(F32)
            csum = jnp.sum(st_end * dst, axis=0, keepdims=True)
            sl = pl.ds(pl.multiple_of(c * CHUNK, CHUNK), CHUNK)
            qp, sq, qf, sg, f, kf, v, b, bm, bl, qt, kt, a = _hgrn_chunk(q_ref, f_ref, i_ref, sl, lb, tril, tri)
            st = st_ref[c]
            doc = do_scr[sl, :]
            dob = doc.astype(BF16)
            vb = v.astype(BF16)
            dstb = dst.astype(BF16)
            eb = jnp.exp(b)
            qe = qf * eb
            kh = kf * jnp.exp(bl - b)
            hi = lax.Precision.HIGHEST
            da = jnp.where(tril, _dot(doc, v, NT, hi), 0.0)
            dqf = _dot(da, kt, NN, hi) * jnp.exp(b - bm) + eb * _dot(doc, st.astype(F32), NN, hi)
            dkf = _dot(da, qt, TN, hi) * jnp.exp(bm - b) + jnp.exp(bl - b) * _dot(v, dst, NN, hi)
            dv = _dot(a.astype(BF16), dob, TN) + _dot(kh.astype(BF16), dstb, NT)
            dst_new = dst * jnp.exp(bl) + _dot(dob, qe.astype(BF16), TN)
            gq = qf * dqf - kf * dkf
            dlogf = csum + _dot(triu, gq, NN, precision=lax.Precision.HIGHEST)
            dfv = dlogf / f - dkf
            dq_ref[sl, :] = (dqf * (sq * (1.0 + qp * (1.0 - sq)))).astype(BF16)
            df_ref[sl, :] = (dfv * (1.0 - lb) * sg * (1.0 - sg)).astype(BF16)
            di_ref[sl, :] = dv.astype(BF16)
            return dst_new, dlb + _colsum(dfv * (1.0 - sg))

        def trip(t, carry):
            for u in range(HGRN_UNROLL):
                carry = step(t * HGRN_UNROLL + u, carry)
            return carry

        _, dlb = lax.fori_loop(0, nc // HGRN_UNROLL, trip, (jnp.zeros((BLK, BLK), F32), jnp.zeros((1, BLK), F32)))
        t = dlb * lb * (1.0 - lb)
        dlbl_ref[...] = jnp.concatenate([t, -t], axis=0)

        @pl.when(head == nh - 1)
        def _():
            for cp in _rs_chip_copies(p_refs, got_refs, ssem, rsem):
                cp.wait()

    hb = pl.BlockSpec((s, BLK), lambda h: (0, h))
    first = ATTN_W // BLK
    rs_shape, rs_sems = _rs_chips_shapes(psums)
    res = pl.pallas_call(
        body, name="hgrn_bwd", grid=(nh,),
        in_specs=_hgrn_specs(s) + [
            hb, pl.BlockSpec((None, nc, BLK, BLK), lambda h: (h, 0, 0, 0)),
            pl.BlockSpec((s, BLK), lambda h: (0, first + h)),
            pl.BlockSpec((2, BLK), lambda h: (0, h)), pl.BlockSpec((1, BLK), lambda h: (0, h))] + [ANY] * nk,
        out_specs=[hb, hb, hb, hb, pl.BlockSpec((1, BLK), lambda h: (0, h)),
                   pl.BlockSpec((2, BLK), lambda h: (0, h))] + [ANY] * nk,
        out_shape=[_sds((s, HGRN_W), BF16)] * 4 + [_sds((1, HGRN_W), F32), _sds((2, HGRN_W), F32)] + rs_shape,
        scratch_shapes=[pltpu.VMEM((s, BLK), F32)] + rs_sems,
        compiler_params=_cp(("arbitrary",)),
    )(proj, proj, proj, proj, o_pre, states, dcat, lb_logits, out_gain, *psums)
    return res


def _place():
    return lax.axis_index("x"), lax.axis_index("y"), lax.axis_index("c")


def _flip(x, y, ox, oy):
    return (1 - x if ox else x), (1 - y if oy else y)


def _half(rows, cc):
    return pl.ds(cc * (rows // 2), rows // 2)


def _remote(src, dst, ssem, rsem, to):
    return pltpu.make_async_remote_copy(src_ref=src, dst_ref=dst, send_sem=ssem, recv_sem=rsem,
                                        device_id=to, device_id_type=MESH)


def _ag_chip_copies(ins, outs, ssem, rsem):
    x, y, c = _place()
    j = 2 * x + y
    cps = []
    for k in range(len(ins)):
        rows = ins[k].shape[0]
        for idx, (ox, oy) in enumerate(FLIPS):
            px, py = _flip(x, y, ox, oy)
            cps.append(_remote(ins[k].at[_half(rows, c)], outs[k].at[j, _half(rows, c)],
                               ssem.at[k, idx], rsem.at[k, idx], (px, py, c)))
    return cps


def _ag_start(ins, outs, ssem, rsem):
    for cp in _ag_chip_copies(ins, outs, ssem, rsem):
        cp.start()


def _ag_finish(ins, outs, ssem, rsem):
    x, y, c = _place()
    sib = (x, y, 1 - c)
    passed = []
    for k in range(len(ins)):
        rows = ins[k].shape[0]
        for idx, (ox, oy) in enumerate(FLIPS):
            px, py = _flip(x, y, ox, oy)
            blk = outs[k].at[2 * px + py, _half(rows, c)]
            _remote(blk, blk, ssem.at[k, idx], rsem.at[k, idx], (px, py, c)).wait_recv()
            cp = _remote(blk, blk, ssem.at[k, 3 + idx], rsem.at[k, 3 + idx], sib)
            cp.start()
            passed.append(cp)
    for k in range(len(ins)):
        rows = ins[k].shape[0]
        for idx, (ox, oy) in enumerate(FLIPS):
            px, py = _flip(x, y, ox, oy)
            blk = outs[k].at[2 * px + py, _half(rows, 1 - c)]
            _remote(blk, blk, ssem.at[k, 3 + idx], rsem.at[k, 3 + idx], sib).wait_recv()
    for cp in _ag_chip_copies(ins, outs, ssem, rsem) + passed:
        cp.wait_send()


def _ag_shapes(shards):
    nk = len(shards)
    return ([_sds((N_CHIPS,) + tuple(w.shape), w.dtype) for w in shards],
            [pltpu.SemaphoreType.DMA((nk, 6)), pltpu.SemaphoreType.DMA((nk, 6))])


def _with_own(gathered, shard, j):
    return lax.dynamic_update_index_in_dim(gathered, shard, j, 0)


def _ag_weights(name, shards):
    nk = len(shards)

    def body(*refs):
        ins, outs = refs[:nk], refs[nk:2 * nk]
        ssem, rsem = refs[2 * nk:]
        _ag_start(ins, outs, ssem, rsem)
        _ag_finish(ins, outs, ssem, rsem)

    out_shape, sems = _ag_shapes(shards)
    return pl.pallas_call(body, name=name, in_specs=[ANY] * nk, out_specs=[ANY] * nk, out_shape=out_shape,
                          scratch_shapes=sems)(*shards)


def _rs_pair(name, grads):
    nk = len(grads)

    def body(*refs):
        ins, outs = refs[:nk], refs[nk:2 * nk]
        ssem, rsem = refs[2 * nk:]
        x, y, c = _place()
        cps = []
        for k in range(nk):
            rows = grads[k].shape[1]
            cp = pltpu.make_async_remote_copy(src_ref=ins[k].at[:, _half(rows, 1 - c)], dst_ref=outs[k],
                                              send_sem=ssem.at[k], recv_sem=rsem.at[k],
                                              device_id=(x, y, 1 - c), device_id_type=MESH)
            cp.start()
            cps.append(cp)
        for cp in cps:
            cp.wait()

    return pl.pallas_call(
        body, name=name, in_specs=[ANY] * nk, out_specs=[ANY] * nk,
        out_shape=[_sds((N_CHIPS, g.shape[1] // 2, g.shape[2]), g.dtype) for g in grads],
        scratch_shapes=[pltpu.SemaphoreType.DMA((nk,)), pltpu.SemaphoreType.DMA((nk,))],
    )(*grads)


def _rs_chip_copies(ins, outs, ssem, rsem):
    x, y, c = _place()
    cps = []
    for k in range(len(ins)):
        for idx, (ox, oy) in enumerate(FLIPS):
            px, py = _flip(x, y, ox, oy)
            cps.append(_remote(ins[k].at[2 * px + py], outs[k].at[idx], ssem.at[k, idx], rsem.at[k, idx], (px, py, c)))
    return cps


def _rs_chips_shapes(psums):
    nk = len(psums)
    return ([_sds((3,) + tuple(p.shape[1:]), p.dtype) for p in psums],
            [pltpu.SemaphoreType.DMA((nk, 3)), pltpu.SemaphoreType.DMA((nk, 3))])


def _rs_chips(name, psums):
    nk = len(psums)

    def body(*refs):
        ins, outs = refs[:nk], refs[nk:2 * nk]
        ssem, rsem = refs[2 * nk:]
        for cp in _rs_chip_copies(ins, outs, ssem, rsem):
            cp.start()
        for cp in _rs_chip_copies(ins, outs, ssem, rsem):
            cp.wait()

    out_shape, sems = _rs_chips_shapes(psums)
    return pl.pallas_call(body, name=name, in_specs=[ANY] * nk, out_specs=[ANY] * nk, out_shape=out_shape,
                          scratch_shapes=sems)(*psums)


def _rs_share(fulls):
    nk = len(fulls)

    def body(*refs):
        ins, outs = refs[:nk], refs[nk:2 * nk]
        ssem, rsem = refs[2 * nk:]
        x, y, c = _place()
        cps = []
        for k in range(nk):
            rows = fulls[k].shape[0]
            cp = _remote(ins[k].at[_half(rows, c)], outs[k].at[_half(rows, c)], ssem.at[k], rsem.at[k], (x, y, 1 - c))
            cp.start()
            cps.append(cp)
        for k, cp in enumerate(cps):
            rows = fulls[k].shape[0]
            cp.wait_send()
            theirs = outs[k].at[_half(rows, 1 - c)]
            _remote(theirs, theirs, ssem.at[k], rsem.at[k], (x, y, 1 - c)).wait_recv()

    return pl.pallas_call(
        body, name="rs_share", in_specs=[ANY] * nk, out_specs=[ANY] * nk,
        out_shape=[_sds(f.shape, f.dtype) for f in fulls], input_output_aliases={k: k for k in range(nk)},
        scratch_shapes=[pltpu.SemaphoreType.DMA((nk,)), pltpu.SemaphoreType.DMA((nk,))],
    )(*fulls)


def _allreduce_small(v):
    ndev = 8

    def body(in_ref, out_ref, buf, ssem, rsem):
        x, y, c = _place()
        me = 4 * x + 2 * y + c
        buf[me] = in_ref[...]
        cps = []
        for k in range(1, ndev):
            ox, oy, oc = (k >> 2) & 1, (k >> 1) & 1, k & 1
            px, py = _flip(x, y, ox, oy)
            pc = 1 - c if oc else c
            cp = pltpu.make_async_remote_copy(src_ref=in_ref, dst_ref=buf.at[me], send_sem=ssem.at[k - 1],
                                              recv_sem=rsem.at[k - 1], device_id=(px, py, pc), device_id_type=MESH)
            cp.start()
            cps.append((cp, 4 * px + 2 * py + pc, (px, py, pc)))
        for k, (cp, src, peer) in enumerate(cps):
            cp.wait_send()
            pltpu.make_async_remote_copy(src_ref=in_ref, dst_ref=buf.at[src], send_sem=ssem.at[k],
                                         recv_sem=rsem.at[k], device_id=peer, device_id_type=MESH).wait_recv()
        acc = buf[0]
        for i in range(1, ndev):
            acc = acc + buf[i]
        out_ref[...] = acc

    return pl.pallas_call(
        body, name="allreduce_small",
        in_specs=[pl.BlockSpec(memory_space=pltpu.VMEM)], out_specs=pl.BlockSpec(memory_space=pltpu.VMEM),
        out_shape=_sds(v.shape, v.dtype),
        scratch_shapes=[pltpu.VMEM((ndev,) + v.shape, v.dtype), pltpu.SemaphoreType.DMA((ndev - 1,)),
                        pltpu.SemaphoreType.DMA((ndev - 1,))],
    )(v)


def _rs_sum1(name, g, recv, c_idx):
    _, r, cdim = g.shape
    hr = r // 2
    tr = min(hr, 256)
    nr = hr // tr

    def body(c_ref, g_ref, r_ref, o32_ref, o16_ref):
        v = g_ref[...] + r_ref[...].astype(F32)
        o32_ref[...] = v
        o16_ref[...] = v.astype(BF16)

    spec = pl.BlockSpec((None, tr, cdim), lambda j, i, c_ref: (j, i, 0))
    return pl.pallas_call(
        body, name=name,
        grid_spec=pltpu.PrefetchScalarGridSpec(
            num_scalar_prefetch=1, grid=(N_CHIPS, nr),
            in_specs=[pl.BlockSpec((None, tr, cdim), lambda j, i, c_ref: (j, c_ref[0] * nr + i, 0)), spec],
            out_specs=[spec, spec]),
        out_shape=[_sds((N_CHIPS, hr, cdim), F32), _sds((N_CHIPS, hr, cdim), BF16)],
        compiler_params=_cp(("parallel", "parallel")),
    )(c_idx, g, recv)


def _rs_sum2(name, p32, recv, jc_idx):
    _, hr, cdim = p32.shape
    tr = min(hr, 256)
    nr = hr // tr

    def body(jc_ref, p_ref, r_ref, o_ref):
        o_ref[...] = ((p_ref[...] + r_ref[0].astype(F32)) + r_ref[1].astype(F32)) + r_ref[2].astype(F32)

    return pl.pallas_call(
        body, name=name,
        grid_spec=pltpu.PrefetchScalarGridSpec(
            num_scalar_prefetch=1, grid=(nr,),
            in_specs=[pl.BlockSpec((None, tr, cdim), lambda i, jc: (jc[0], i, 0)),
                      pl.BlockSpec((3, tr, cdim), lambda i, jc: (0, i, 0))],
            out_specs=pl.BlockSpec((tr, cdim), lambda i, jc: (jc[1] * nr + i, 0))),
        out_shape=_sds((2 * hr, cdim), F32),
        compiler_params=_cp(("parallel",)),
    )(jc_idx, p32, recv)


def _adamw(name, w, g, m, v):
    r, cdim = w.shape
    tr = min(r, 256)
    c1 = 1.0 - ADAM_B1 ** ADAM_STEP
    c2 = 1.0 - ADAM_B2 ** ADAM_STEP

    def body(w_ref, g_ref, m_ref, v_ref, d_ref, nm_ref, nv_ref):
        gv = g_ref[...]
        nm = ADAM_B1 * m_ref[...] + (1.0 - ADAM_B1) * gv
        nv = ADAM_B2 * v_ref[...] + (1.0 - ADAM_B2) * (gv * gv)
        d_ref[...] = -ADAM_LR * ((nm / c1) / (jnp.sqrt(nv / c2) + ADAM_EPS) + ADAM_WD * w_ref[...])
        nm_ref[...] = nm
        nv_ref[...] = nv

    spec = pl.BlockSpec((tr, cdim), lambda i: (i, 0))
    return pl.pallas_call(
        body, name=name, grid=(r // tr,), in_specs=[spec] * 4, out_specs=[spec] * 3,
        out_shape=[_sds((r, cdim), F32)] * 3, compiler_params=_cp(("parallel",)),
    )(w, g, m, v)


def _pack_small(mix_pre, attn_out, lb_logits, hgrn_out, mix_post, mlp_pre, mlp_post):
    rows = [mix_pre, jnp.concatenate([attn_out, hgrn_out], axis=1),
            jnp.concatenate([lb_logits[0:1], lb_logits[1:2]], axis=1), mix_post, mlp_pre, mlp_post,
            jnp.zeros((2, D_MODEL), F32)]
    return jnp.concatenate(rows, axis=0)


def _unpack_small(p):
    return (p[0:1], p[1:2, :ATTN_W], jnp.concatenate([p[2:3, :HGRN_W], p[2:3, HGRN_W:]], axis=0),
            p[1:2, ATTN_W:], p[3:4], p[4:5], p[5:6])


def kernel(x, mix_pre_norm, w_in, attn_out_norm, hgrn_lb_logits, hgrn_out_norm, w_out, mix_post_norm, mlp_pre_norm, w_ff1, w_ff2, mlp_post_norm, loss_target, m_mix_pre_norm, m_w_in, m_attn_out_norm, m_hgrn_lb_logits, m_hgrn_out_norm, m_w_out, m_mix_post_norm, m_mlp_pre_norm, m_w_ff1, m_w_ff2, m_mlp_post_norm, v_mix_pre_norm, v_w_in, v_attn_out_norm, v_hgrn_lb_logits, v_hgrn_out_norm, v_w_out, v_mix_post_norm, v_mlp_pre_norm, v_w_ff1, v_w_ff2, v_mlp_post_norm):
    s = x.shape[1]
    xs = x.reshape(s, D_MODEL)
    tgt = loss_target.reshape(s, D_MODEL)
    cx, cy, cc = _place()
    chip = 2 * cx + cy
    c_idx = jnp.reshape(cc, (1,)).astype(jnp.int32)
    jc_idx = jnp.stack([chip, cc]).astype(jnp.int32)

    big_w = [w_in[0], w_out[0], w_ff1[0], w_ff2[0]]
    big_m = [m_w_in[0], m_w_out[0], m_w_ff1[0], m_w_ff2[0]]
    big_v = [v_w_in[0], v_w_out[0], v_w_ff1[0], v_w_ff2[0]]
    shards = [w.astype(BF16) for w in big_w]

    (wg_in,) = _ag_weights("ag_in", shards[:1])
    wg_in = _with_own(wg_in, shards[0], chip)

    (h,) = _rows_call("norm_in", lambda xv, g: ((xv * _rstd(xv) * g),),
                      [(xs, _row(D_MODEL)), (mix_pre_norm, "full")], [(D_MODEL, BF16, "row")], s)
    (proj,) = _mm_cols("mm_proj", h, wg_in, NN, [F32])
    hg_o, rec, states, wg_out, wg_1, wg_2 = _hgrn_fwd(proj, hgrn_lb_logits, hgrn_out_norm, shards[1:])
    wg_out, wg_1, wg_2 = (_with_own(g, w, chip) for g, w in zip((wg_out, wg_1, wg_2), shards[1:]))
    att = [_attn_fwd(proj, d) for d in DILATIONS]

    def comb(o1, o2, o3, l1, l2, l3, gain):
        mx = jnp.maximum(jnp.maximum(l1, l2), l3)
        lse = mx + jnp.log(jnp.exp(l1 - mx) + jnp.exp(l2 - mx) + jnp.exp(l3 - mx))
        o = jnp.exp(l1 - lse) * o1 + jnp.exp(l2 - lse) * o2 + jnp.exp(l3 - lse) * o3
        return o, lse, o * _rstd(o) * gain

    attn_o, attn_lse, attn_n = _rows_call(
        "attn_comb", comb, [(a[0], _row(ATTN_W)) for a in att] + [(a[1], _row(ATTN_W)) for a in att]
        + [(attn_out_norm, "full")], [(ATTN_W, F32, "row"), (ATTN_W, F32, "row"), (ATTN_W, BF16, "row")], s)
    cat = jnp.concatenate([attn_n, rec], axis=1)
    mixed = _mm_acc("mm_mixed", cat, wg_out, NN)

    def post1(xv, mv, g_post, g_pre2):
        x1 = xv + mv * _rstd(mv) * g_post
        return x1, x1 * _rstd(x1) * g_pre2

    x1, h2 = _rows_call("post1", post1, [(xs, _row(D_MODEL)), (mixed, _row(D_MODEL)), (mix_post_norm, "full"),
                                         (mlp_pre_norm, "full")], [(D_MODEL, F32, "row"), (D_MODEL, BF16, "row")], s)

    def sq_relu(u):
        r = jnp.maximum(u, 0.0)
        return r * r, r

    act, ru = _mm_cols("mm_ff1", h2, wg_1, NN, [BF16, BF16], epi=sq_relu)
    ff = _mm_acc("mm_ff2", act, wg_2, NN)

    def post2(x1v, fv, tv, g):
        y = x1v + fv * _rstd(fv) * g
        dy = (y - tv) * (1.0 / D_MODEL)
        err = y - tv
        loss = 0.5 * jnp.sum(jnp.mean(err * err, axis=-1, keepdims=True), axis=0, keepdims=True)
        dff, dgc = _norm_bwd(fv, g, dy)
        return dy, dff, _colsum(dgc), jnp.broadcast_to(loss, (1, BLK))

    dy, dff, g_mlp_post, loss_part = _rows_call(
        "post2", post2, [(x1, _row(D_MODEL)), (ff, _row(D_MODEL)), (tgt, _row(D_MODEL)), (mlp_post_norm, "full")],
        [(D_MODEL, F32, "row"), (D_MODEL, BF16, "row"), (D_MODEL, F32, "acc"), (BLK, F32, "acc")], s)

    (du,) = _mm_cols("mm_du", dff, wg_2, NT, [BF16], epi=lambda acc, r: (acc * (2.0 * r.astype(F32)),),
                     extras=(ru,))
    gw_2 = _mm_wgrad("mm_gw2", act, dff, True)
    gw_1 = _mm_wgrad("mm_gw1", h2, du, False)
    dh2 = _mm_acc("mm_dh2", du, wg_1, NT)

    def bwd_mid(dyv, dh2v, x1v, mv, g_pre2, g_post):
        d1, gc1 = _norm_bwd(x1v, g_pre2, dh2v)
        dx1 = dyv + d1
        dm, gc2 = _norm_bwd(mv, g_post, dx1)
        return dx1, dm, _colsum(gc1), _colsum(gc2)

    dx1, dmixed, g_mlp_pre, g_mix_post = _rows_call(
        "bwd_mid", bwd_mid, [(dy, _row(D_MODEL)), (dh2, _row(D_MODEL)), (x1, _row(D_MODEL)), (mixed, _row(D_MODEL)),
                             (mlp_pre_norm, "full"), (mix_post_norm, "full")],
        [(D_MODEL, F32, "row"), (D_MODEL, BF16, "row"), (D_MODEL, F32, "acc"), (D_MODEL, F32, "acc")], s)

    (dcat,) = _mm_cols("mm_dcat", dmixed, wg_out, NT, [F32])
    gw_out = _mm_wgrad("mm_gwout", cat, dmixed, True)

    names = ["out", "ff1", "ff2", "in"]
    ready = [gw_out, gw_1, gw_2]
    from_pair = _rs_pair("rs_pair_3", [g.astype(BF16) for g in ready])
    pair = [_rs_sum1(f"rs_sum1_{n}", g, r, c_idx) for n, g, r in zip(names, ready, from_pair)]

    def attn_norm_bwd(dc, o, gain):
        do, gc = _norm_bwd(o, gain, dc)
        t = do * o
        lane = lax.broadcasted_iota(jnp.int32, (t.shape[0], BLK), 1) < 64
        parts = []
        for p in range(ATTN_W // BLK):
            tp = t[:, p * BLK:(p + 1) * BLK]
            sa = jnp.sum(jnp.where(lane, tp, 0.0), axis=1, keepdims=True)
            sb = jnp.sum(jnp.where(lane, 0.0, tp), axis=1, keepdims=True)
            parts.append(jnp.where(lane, sa, sb))
        return do, jnp.concatenate(parts, axis=1), _colsum(gc)

    do_attn, delta, g_attn_out = _rows_call(
        "attn_norm_bwd", attn_norm_bwd, [(dcat, _row(ATTN_W, 0)), (attn_o, _row(ATTN_W)), (attn_out_norm, "full")],
        [(ATTN_W, F32, "row"), (ATTN_W, F32, "row"), (ATTN_W, F32, "acc")], s)
    dqkv = [_attn_bwd(proj, do_attn, attn_lse, delta, d) for d in DILATIONS]
    dhq, dhf, dhi, dhg, g_hgrn_out, g_lb, *from_chips = _hgrn_bwd(
        proj, hg_o, states, dcat, hgrn_lb_logits, hgrn_out_norm, [p[1] for p in pair])
    dh_parts = [dhq, dhf, dhi, dhg]

    def dproj_asm(*a):
        a1, a4, a16 = a[0:3], a[3:6], a[6:9]
        return (jnp.concatenate([(a1[k] + a4[k] + a16[k]).astype(BF16) for k in range(3)] + list(a[9:]), axis=1),)

    (dproj,) = _rows_call("dproj_asm", dproj_asm,
                          [(t, _row(ATTN_W)) for grp in dqkv for t in grp] + [(t, _row(HGRN_W)) for t in dh_parts],
                          [(PROJ_W, BF16, "row")], s)
    dh = _mm_acc("mm_dh", dproj, wg_in, NT)
    gw_in = _mm_wgrad("mm_gwin", h, dproj, False)

    def bwd_in(dx1v, dhv, xv, g):
        d0, gc = _norm_bwd(xv, g, dhv)
        return dx1v + d0, _colsum(gc)

    grad_x, g_mix_pre = _rows_call("bwd_in", bwd_in, [(dx1, _row(D_MODEL)), (dh, _row(D_MODEL)), (xs, _row(D_MODEL)),
                                                      (mix_pre_norm, "full")],
                                   [(D_MODEL, F32, "row"), (D_MODEL, F32, "acc")], s)

    loss = lax.psum(loss_part[0, 0], ("x", "y", "c"))
    small_g = _allreduce_small(_pack_small(g_mix_pre, g_attn_out, g_lb, g_hgrn_out, g_mix_post, g_mlp_pre, g_mlp_post))

    (from_pair_in,) = _rs_pair("rs_pair_in", [gw_in.astype(BF16)])
    pair.append(_rs_sum1("rs_sum1_in", gw_in, from_pair_in, c_idx))
    from_chips += _rs_chips("rs_chips_in", [pair[3][1]])
    reduced = [_rs_sum2(f"rs_sum2_{n}", p[0], r, jc_idx) for n, p, r in zip(names, pair, from_chips)]
    g_wout, g_w1, g_w2, g_win = _rs_share(reduced)
    full = [g_win, g_wout, g_w1, g_w2]

    upd = [_adamw(f"adamw_{n}", w, g, m, v) for n, w, g, m, v in zip(("in", "out", "ff1", "ff2"), big_w, full, big_m, big_v)]
    small_w = _pack_small(mix_pre_norm, attn_out_norm, hgrn_lb_logits, hgrn_out_norm, mix_post_norm, mlp_pre_norm,
                          mlp_post_norm)
    small_m = _pack_small(m_mix_pre_norm, m_attn_out_norm, m_hgrn_lb_logits, m_hgrn_out_norm, m_mix_post_norm,
                          m_mlp_pre_norm, m_mlp_post_norm)
    small_v = _pack_small(v_mix_pre_norm, v_attn_out_norm, v_hgrn_lb_logits, v_hgrn_out_norm, v_mix_post_norm,
                          v_mlp_pre_norm, v_mlp_post_norm)
    small_upd = _adamw("adamw_small", small_w, small_g, small_m, small_v)

    def assemble(small, big):
        sm = _unpack_small(small)
        return (sm[0], big[0][None], sm[1], sm[2], sm[3], big[1][None], sm[4], sm[5], big[2][None], big[3][None], sm[6])

    g_out = assemble(small_g, full)
    d_out = assemble(small_upd[0], [u[0] for u in upd])
    m_out = assemble(small_upd[1], [u[1] for u in upd])
    v_out = assemble(small_upd[2], [u[2] for u in upd])
    return (loss, grad_x.reshape(x.shape), *g_out, *d_out, *m_out, *v_out)
```

```python
import jax
import jax.numpy as jnp
from jax import lax
from jax.experimental import pallas as pl
from jax.experimental.pallas import tpu as pltpu

F32 = jnp.float32
BF16 = jnp.bfloat16
MESH = pl.DeviceIdType.MESH
ANY = pl.BlockSpec(memory_space=pl.ANY)

RMS_EPS = 1e-6
D_MODEL = 1024
ATTN_W = 512
HGRN_W = 512
PROJ_W = 3584
D_FF = 4096
N_CHIPS = 4
BLK = 128
CHUNK = 64
HGRN_TB = 512
DILATIONS = (1, 4, 16)
ATTN_SCALE = 0.125
ROW_TILE = 512
MM_TILE = 1024
VMEM_LIMIT = 48 * 2 ** 20
FLIPS = ((1, 0), (0, 1), (1, 1))

ADAM_LR, ADAM_B1, ADAM_B2, ADAM_EPS, ADAM_WD, ADAM_STEP = 0.001, 0.9, 0.999, 1e-08, 0.01, 10


def _cp(sem=None):
    return pltpu.CompilerParams(dimension_semantics=sem, vmem_limit_bytes=VMEM_LIMIT)


def _sigmoid(v):
    return 1.0 / (1.0 + jnp.exp(-v))


def _dot(a, b, contract, precision=None):
    return lax.dot_general(a, b, (contract, ((), ())), preferred_element_type=F32, precision=precision)


NN = ((1,), (0,))
NT = ((1,), (1,))
TN = ((0,), (0,))


def _matmul(name, a, b, *, grid, a_spec, b_spec, contract, outs, epi=None, extras=(), extra_specs=(),
            acc_shape=None):
    n_ex, n_out, nj = len(extras), len(outs), grid[-1]

    def body(a_ref, b_ref, *rest):
        ex, out_refs = rest[:n_ex], rest[n_ex:n_ex + n_out]

        def finish(acc):
            res = epi(acc, *[e[...] for e in ex]) if epi else (acc,)
            for o, r in zip(out_refs, res):
                o[...] = r.astype(o.dtype)

        p = _dot(a_ref[...], b_ref[...], contract)
        if acc_shape is None:
            finish(p)
        else:
            acc_ref = rest[-1]
            j = pl.program_id(len(grid) - 1)

            @pl.when(j == 0)
            def _():
                acc_ref[...] = p

            @pl.when(j > 0)
            def _():
                acc_ref[...] += p

            @pl.when(j == nj - 1)
            def _():
                finish(acc_ref[...])

    sem = ("parallel",) * len(grid) if acc_shape is None else ("parallel",) * (len(grid) - 1) + ("arbitrary",)
    res = pl.pallas_call(
        body, name=name, grid=grid,
        in_specs=[a_spec, b_spec, *extra_specs],
        out_specs=[s for _, s in outs],
        out_shape=[s for s, _ in outs],
        scratch_shapes=[] if acc_shape is None else [pltpu.VMEM(acc_shape, F32)],
        compiler_params=_cp(sem),
    )(a, b, *extras)
    return res


def _sds(shape, dtype):
    return jax.ShapeDtypeStruct(shape, dtype)


def _mm_cols(name, a, w, contract, out_dtypes, epi=None, extras=()):
    m, k = a.shape
    jn = w.shape[0]
    nj = w.shape[2] if contract == NN else w.shape[1]
    tm = min(m, MM_TILE)
    outs = [(_sds((m, jn * nj), dt), pl.BlockSpec((tm, nj), lambda j, i: (i, j))) for dt in out_dtypes]
    return _matmul(name, a, w, grid=(jn, m // tm),
                   a_spec=pl.BlockSpec((tm, k), lambda j, i: (i, 0)),
                   b_spec=pl.BlockSpec((None,) + w.shape[1:], lambda j, i: (j, 0, 0)),
                   contract=contract, outs=outs, epi=epi, extras=extras,
                   extra_specs=[pl.BlockSpec((tm, nj), lambda j, i: (i, j)) for _ in extras])


def _mm_acc(name, a, w, contract):
    m = a.shape[0]
    jn = w.shape[0]
    kj = w.shape[1] if contract == NN else w.shape[2]
    n = w.shape[2] if contract == NN else w.shape[1]
    tm = min(m, MM_TILE)
    outs = [(_sds((m, n), F32), pl.BlockSpec((tm, n), lambda i, j: (i, 0)))]
    return _matmul(name, a, w, grid=(m // tm, jn),
                   a_spec=pl.BlockSpec((tm, kj), lambda i, j: (i, j)),
                   b_spec=pl.BlockSpec((None,) + w.shape[1:], lambda i, j: (j, 0, 0)),
                   contract=contract, outs=outs, acc_shape=(tm, n))[0]


def _mm_wgrad(name, a, b, a_by_j):
    s = a.shape[0]
    if a_by_j:
        r, c = a.shape[1] // N_CHIPS, b.shape[1]
    else:
        r, c = a.shape[1], b.shape[1] // N_CHIPS
    tr = min(r, 512)
    nr = r // tr
    if a_by_j:
        a_spec = pl.BlockSpec((s, tr), lambda j, i: (0, j * nr + i))
        b_spec = pl.BlockSpec((s, c), lambda j, i: (0, 0))
    else:
        a_spec = pl.BlockSpec((s, tr), lambda j, i: (0, i))
        b_spec = pl.BlockSpec((s, c), lambda j, i: (0, j))
    outs = [(_sds((N_CHIPS, r, c), F32), pl.BlockSpec((None, tr, c), lambda j, i: (j, i, 0)))]
    return _matmul(name, a, b, grid=(N_CHIPS, nr), a_spec=a_spec, b_spec=b_spec, contract=TN, outs=outs)[0]


def _rows_call(name, fn, ins, outs, s):
    tm = ROW_TILE
    in_specs = []
    for arr, kind in ins:
        if kind == "full":
            in_specs.append(pl.BlockSpec(arr.shape, lambda i: (0, 0)))
        else:
            _, w, cb = kind
            in_specs.append(pl.BlockSpec((tm, w), lambda i, cb=cb: (i, cb)))
    out_specs, out_shape, is_acc = [], [], []
    for w, dt, kind in outs:
        if kind == "acc":
            out_specs.append(pl.BlockSpec((1, w), lambda i: (0, 0)))
            out_shape.append(_sds((1, w), dt))
        else:
            out_specs.append(pl.BlockSpec((tm, w), lambda i: (i, 0)))
            out_shape.append(_sds((s, w), dt))
        is_acc.append(kind == "acc")
    n_in = len(ins)

    def body(*refs):
        i = pl.program_id(0)
        res = fn(*[r[...] for r in refs[:n_in]])
        for o, r, acc in zip(refs[n_in:], res, is_acc):
            if acc:
                @pl.when(i == 0)
                def _(o=o):
                    o[...] = jnp.zeros_like(o)
                o[...] += r.astype(o.dtype)
            else:
                o[...] = r.astype(o.dtype)

    sem = ("arbitrary",) if any(is_acc) else ("parallel",)
    return pl.pallas_call(body, name=name, grid=(s // tm,), in_specs=in_specs, out_specs=out_specs,
                          out_shape=out_shape, compiler_params=_cp(sem))(*[a for a, _ in ins])


def _rstd(v):
    return lax.rsqrt(jnp.mean(v * v, axis=-1, keepdims=True) + RMS_EPS)


def _norm_bwd(v, gain, dy):
    r = _rstd(v)
    n = v * r
    dn = dy * gain
    dv = r * (dn - n * jnp.mean(dn * n, axis=-1, keepdims=True))
    return dv, dy * n


def _colsum(v):
    return jnp.sum(v, axis=0, keepdims=True)


def _row(w, cb=0):
    return ("row", w, cb)


def _head_col(v, mask):
    return jnp.max(jnp.where(mask, v, -jnp.inf), axis=1, keepdims=True)


def _attn_fwd(proj, d):
    s = proj.shape[0]
    rows, nb = s // d, s // d // BLK
    projv = proj.reshape(rows, d * PROJ_W)
    cpb = PROJ_W // ATTN_W

    def body(q_ref, kc_ref, kp_ref, vc_ref, vp_ref, o_ref, l_ref):
        n = pl.program_id(1)
        row = lax.broadcasted_iota(jnp.int32, (BLK, 2 * BLK), 0)
        col = lax.broadcasted_iota(jnp.int32, (BLK, 2 * BLK), 1)
        dist = row + BLK - col
        valid = (dist >= 0) & (dist <= BLK) & (col + n * BLK >= BLK)
        distf = dist.astype(F32) * float(d)
        lane_q = lax.broadcasted_iota(jnp.int32, (BLK, BLK), 1) < 64
        lane_k = lax.broadcasted_iota(jnp.int32, (2 * BLK, BLK), 1) < 64
        for p in range(ATTN_W // BLK):
            cs = slice(p * BLK, (p + 1) * BLK)
            q2 = q_ref[:, cs]
            kk = jnp.concatenate([kp_ref[:, cs], kc_ref[:, cs]], axis=0).astype(BF16)
            vv = jnp.concatenate([vp_ref[:, cs], vc_ref[:, cs]], axis=0)
            o2 = jnp.zeros((BLK, BLK), F32)
            lse2 = jnp.zeros((BLK, BLK), F32)
            for hh in range(2):
                slope = 2.0 ** -(2 * p + hh + 1)
                mq = lane_q if hh == 0 else ~lane_q
                mk = lane_k if hh == 0 else ~lane_k
                qm = jnp.where(mq, q2, 0.0).astype(BF16)
                sc = _dot(qm, kk, NT) * ATTN_SCALE - slope * distf
                sc = jnp.where(valid, sc, -1e30)
                m = jnp.max(sc, axis=1, keepdims=True)
                pr = jnp.exp(sc - m)
                den = jnp.sum(pr, axis=1, keepdims=True)
                vm = jnp.where(mk, vv, 0.0).astype(BF16)
                o2 = o2 + _dot(pr.astype(BF16), vm, NN) / den
                lse2 = jnp.where(mq, m + jnp.log(den), lse2)
            o_ref[:, cs] = o2
            l_ref[:, cs] = lse2

    blk = (BLK, ATTN_W)
    prev = lambda r, n: jnp.maximum(n - 1, 0)
    in_specs = [
        pl.BlockSpec(blk, lambda r, n: (n, r * cpb)),
        pl.BlockSpec(blk, lambda r, n: (n, r * cpb + 1)),
        pl.BlockSpec(blk, lambda r, n: (prev(r, n), r * cpb + 1)),
        pl.BlockSpec(blk, lambda r, n: (n, r * cpb + 2)),
        pl.BlockSpec(blk, lambda r, n: (prev(r, n), r * cpb + 2)),
    ]
    out_spec = pl.BlockSpec(blk, lambda r, n: (n, r))
    o, lse = pl.pallas_call(
        body, name=f"attn_fwd_d{d}", grid=(d, nb), in_specs=in_specs, out_specs=[out_spec, out_spec],
        out_shape=[_sds((rows, d * ATTN_W), F32)] * 2, compiler_params=_cp(("parallel", "parallel")),
    )(projv, projv, projv, projv, projv)
    return o.reshape(s, ATTN_W), lse.reshape(s, ATTN_W)


def _attn_bwd(proj, do, lse, delta, d):
    s = proj.shape[0]
    rows, nb = s // d, s // d // BLK
    projv = proj.reshape(rows, d * PROJ_W)
    dov, lsev, delv = (t.reshape(rows, d * ATTN_W) for t in (do, lse, delta))
    cpb = PROJ_W // ATTN_W

    def body(qc_ref, qn_ref, kc_ref, kp_ref, vc_ref, vp_ref, dc_ref, dn_ref, lc_ref, ln_ref, ec_ref, en_ref,
             dq_ref, dk_ref, dv_ref):
        n = pl.program_id(1)
        row1 = lax.broadcasted_iota(jnp.int32, (BLK, 2 * BLK), 0)
        col1 = lax.broadcasted_iota(jnp.int32, (BLK, 2 * BLK), 1)
        dist1 = row1 + BLK - col1
        valid1 = (dist1 >= 0) & (dist1 <= BLK) & (col1 + n * BLK >= BLK)
        dist1f = dist1.astype(F32) * float(d)
        row2 = lax.broadcasted_iota(jnp.int32, (2 * BLK, BLK), 0)
        col2 = lax.broadcasted_iota(jnp.int32, (2 * BLK, BLK), 1)
        dist2 = row2 - col2
        valid2 = (dist2 >= 0) & (dist2 <= BLK) & (row2 - (nb - 1 - n) * BLK < BLK)
        dist2f = dist2.astype(F32) * float(d)
        lane1 = lax.broadcasted_iota(jnp.int32, (BLK, BLK), 1) < 64
        lane2 = lax.broadcasted_iota(jnp.int32, (2 * BLK, BLK), 1) < 64
        for p in range(ATTN_W // BLK):
            cs = slice(p * BLK, (p + 1) * BLK)
            q1, d1, l1, e1 = qc_ref[:, cs], dc_ref[:, cs], lc_ref[:, cs], ec_ref[:, cs]
            kk = jnp.concatenate([kp_ref[:, cs], kc_ref[:, cs]], axis=0)
            vv = jnp.concatenate([vp_ref[:, cs], vc_ref[:, cs]], axis=0).astype(BF16)
            qq = jnp.concatenate([q1, qn_ref[:, cs]], axis=0)
            dd = jnp.concatenate([d1, dn_ref[:, cs]], axis=0)
            ll = jnp.concatenate([l1, ln_ref[:, cs]], axis=0)
            ee = jnp.concatenate([e1, en_ref[:, cs]], axis=0)
            k1 = kc_ref[:, cs].astype(BF16)
            v1 = vc_ref[:, cs].astype(BF16)
            kkb = kk.astype(BF16)
            dq2 = jnp.zeros((BLK, BLK), F32)
            dk2 = jnp.zeros((BLK, BLK), F32)
            dv2 = jnp.zeros((BLK, BLK), F32)
            for hh in range(2):
                slope = 2.0 ** -(2 * p + hh + 1)
                m1 = lane1 if hh == 0 else ~lane1
                m2 = lane2 if hh == 0 else ~lane2
                qm = jnp.where(m1, q1, 0.0).astype(BF16)
                dm = jnp.where(m1, d1, 0.0).astype(BF16)
                sc = _dot(qm, kkb, NT) * ATTN_SCALE - slope * dist1f
                pr = jnp.where(valid1, jnp.exp(sc - _head_col(l1, m1)), 0.0)
                dp = _dot(dm, vv, NT)
                ds = pr * (dp - _head_col(e1, m1))
                km = jnp.where(m2, kk, 0.0).astype(BF16)
                dq2 = dq2 + _dot(ds.astype(BF16), km, NN) * ATTN_SCALE
                qqm = jnp.where(m2, qq, 0.0).astype(BF16)
                ddm = jnp.where(m2, dd, 0.0).astype(BF16)
                sc = _dot(qqm, k1, NT) * ATTN_SCALE - slope * dist2f
                pr = jnp.where(valid2, jnp.exp(sc - _head_col(ll, m2)), 0.0)
                dp = _dot(ddm, v1, NT)
                ds = pr * (dp - _head_col(ee, m2))
                dk2 = dk2 + _dot(ds.astype(BF16), qqm, TN) * ATTN_SCALE
                dv2 = dv2 + _dot(pr.astype(BF16), ddm, TN)
            dq_ref[:, cs] = dq2
            dk_ref[:, cs] = dk2
            dv_ref[:, cs] = dv2

    blk = (BLK, ATTN_W)
    prev = lambda n: jnp.maximum(n - 1, 0)
    nxt = lambda n: jnp.minimum(n + 1, nb - 1)
    in_specs = [
        pl.BlockSpec(blk, lambda r, n: (n, r * cpb)),
        pl.BlockSpec(blk, lambda r, n: (nxt(n), r * cpb)),
        pl.BlockSpec(blk, lambda r, n: (n, r * cpb + 1)),
        pl.BlockSpec(blk, lambda r, n: (prev(n), r * cpb + 1)),
        pl.BlockSpec(blk, lambda r, n: (n, r * cpb + 2)),
        pl.BlockSpec(blk, lambda r, n: (prev(n), r * cpb + 2)),
        pl.BlockSpec(blk, lambda r, n: (n, r)),
        pl.BlockSpec(blk, lambda r, n: (nxt(n), r)),
        pl.BlockSpec(blk, lambda r, n: (n, r)),
        pl.BlockSpec(blk, lambda r, n: (nxt(n), r)),
        pl.BlockSpec(blk, lambda r, n: (n, r)),
        pl.BlockSpec(blk, lambda r, n: (nxt(n), r)),
    ]
    out_spec = pl.BlockSpec(blk, lambda r, n: (n, r))
    res = pl.pallas_call(
        body, name=f"attn_bwd_d{d}", grid=(d, nb), in_specs=in_specs, out_specs=[out_spec] * 3,
        out_shape=[_sds((rows, d * ATTN_W), F32)] * 3, compiler_params=_cp(("parallel", "parallel")),
    )(projv, projv, projv, projv, projv, projv, dov, dov, lsev, lsev, delv, delv)
    return [t.reshape(s, ATTN_W) for t in res]


def _lower_bound(lbl):
    return 1.0 / (1.0 + jnp.exp(lbl[1:2, :] - lbl[0:1, :]))


def _hi(a):
    bits = lax.bitcast_convert_type(a, jnp.uint32) & jnp.uint32(0xFFFF0000)
    return lax.bitcast_convert_type(bits, F32)


def _dot3(a, b, contract):
    ah, bh = _hi(a), _hi(b)
    al, bl = (a - ah).astype(BF16), (b - bh).astype(BF16)
    ah, bh = ah.astype(BF16), bh.astype(BF16)
    return _dot(ah, bh, contract) + (_dot(ah, bl, contract) + _dot(al, bh, contract))


def _cumsum_rows(tri, g):
    g1 = _hi(g)
    r1 = g - g1
    g2 = _hi(r1)
    g3 = r1 - g2
    return _dot(tri, g1.astype(BF16), NN) + (_dot(tri, g2.astype(BF16), NN) + _dot(tri, g3.astype(BF16), NN))


def _heads(fn):
    return jnp.concatenate([fn(slice(h * BLK, (h + 1) * BLK)) for h in range(HGRN_W // BLK)], axis=1)


def _head_mean(t):
    return _heads(lambda hs: jnp.broadcast_to(jnp.mean(t[:, hs], axis=1, keepdims=True), (t.shape[0], BLK)))


def _hgrn_chunk(q_ref, f_ref, i_ref, sl, lb, tri):
    qp = q_ref[sl, :]
    sq = _sigmoid(qp)
    qf = qp * sq
    sg = _sigmoid(f_ref[sl, :])
    f = lb + (1.0 - lb) * sg
    kf = 1.0 - f
    v = i_ref[sl, :]
    b = _cumsum_rows(tri, jnp.log(f))
    bm = b[CHUNK // 2:CHUNK // 2 + 1, :]
    bl = b[CHUNK - 1:CHUNK, :]
    qt = qf * jnp.exp(b - bm)
    kt = kf * jnp.exp(bm - b)
    return qp, sq, qf, sg, f, kf, v, b, bm, bl, qt, kt


def _hgrn_specs(tb, block):
    first = 3 * ATTN_W // HGRN_W
    return [pl.BlockSpec((tb, HGRN_W), lambda i, k=k: (block(i), first + k)) for k in range(4)]


def _hgrn_fwd(proj, lb_logits, out_gain, shards):
    s = proj.shape[0]
    tb = min(HGRN_TB, s)
    nb, cpb, nc = s // tb, tb // CHUNK, s // CHUNK
    nk = len(shards)

    def body(q_ref, f_ref, i_ref, g_ref, lbl_ref, gain_ref, *rest):
        w_refs, (o_ref, rec_ref, st_ref) = rest[:nk], rest[nk:nk + 3]
        wg_refs, (st_scr, ssem, rsem) = rest[nk + 3:2 * nk + 3], rest[2 * nk + 3:]
        step = pl.program_id(0)

        @pl.when(step == 0)
        def _():
            st_scr[...] = jnp.zeros_like(st_scr)
            _ag_start(w_refs, wg_refs, ssem, rsem)

        lb = _lower_bound(lbl_ref[...])
        r64 = lax.broadcasted_iota(jnp.int32, (CHUNK, CHUNK), 0)
        c64 = lax.broadcasted_iota(jnp.int32, (CHUNK, CHUNK), 1)
        tril = r64 >= c64
        tri = tril.astype(BF16)
        st = st_scr[...]
        for cc in range(cpb):
            sl = slice(cc * CHUNK, (cc + 1) * CHUNK)
            _, _, qf, _, _, kf, v, b, _, bl, qt, kt = _hgrn_chunk(q_ref, f_ref, i_ref, sl, lb, tri)
            qe = (qf * jnp.exp(b)).astype(BF16)
            kh = (kf * jnp.exp(bl - b)).astype(BF16)
            qtb, ktb, vb, stb = qt.astype(BF16), kt.astype(BF16), v.astype(BF16), st.astype(BF16)

            def out_h(hs):
                a = jnp.where(tril, _dot(qtb[:, hs], ktb[:, hs], NT), 0.0).astype(BF16)
                return _dot(qe[:, hs], stb[:, hs], NT) + _dot(a, vb[:, hs], NN)

            o_ref[sl, :] = _heads(out_h)
            st_ref[cc] = stb
            st = st * jnp.exp(bl) + _heads(lambda hs: _dot(vb[:, hs], kh[:, hs], TN))
        st_scr[...] = st
        o = o_ref[...]
        gate = g_ref[...]
        rec_ref[...] = (o * lax.rsqrt(_head_mean(o * o) + RMS_EPS) * gain_ref[...] * (gate * _sigmoid(gate))).astype(BF16)

        @pl.when(step == nb - 1)
        def _():
            _ag_finish(w_refs, wg_refs, ssem, rsem)

    row = pl.BlockSpec((tb, HGRN_W), lambda i: (i, 0))
    ag_shape, ag_sems = _ag_shapes(shards)
    return pl.pallas_call(
        body, name="hgrn_fwd", grid=(nb,),
        in_specs=_hgrn_specs(tb, lambda i: i) + [pl.BlockSpec((2, HGRN_W), lambda i: (0, 0)),
                                                 pl.BlockSpec((1, HGRN_W), lambda i: (0, 0))] + [ANY] * nk,
        out_specs=[row, row, pl.BlockSpec((cpb, BLK, HGRN_W), lambda i: (i, 0, 0))] + [ANY] * nk,
        out_shape=[_sds((s, HGRN_W), F32), _sds((s, HGRN_W), BF16), _sds((nc, BLK, HGRN_W), BF16)] + ag_shape,
        scratch_shapes=[pltpu.VMEM((BLK, HGRN_W), F32)] + ag_sems,
        compiler_params=_cp(("arbitrary",)),
    )(proj, proj, proj, proj, lb_logits, out_gain, *shards)


def _hgrn_bwd(proj, o_pre, states, dcat, lb_logits, out_gain, psums):
    s = proj.shape[0]
    tb = min(HGRN_TB, s)
    nb, cpb, nc = s // tb, tb // CHUNK, s // CHUNK
    nk = len(psums)

    def body(q_ref, f_ref, i_ref, g_ref, o_ref, st_ref, stn_ref, dy_ref, lbl_ref, gain_ref, *rest):
        p_refs, (dq_ref, df_ref, di_ref, dg_ref, dgain_ref, dlbl_ref) = rest[:nk], rest[nk:nk + 6]
        got_refs, (do_scr, dst_scr, dlb_scr, ssem, rsem) = rest[nk + 6:2 * nk + 6], rest[2 * nk + 6:]
        step = pl.program_id(0)

        @pl.when(step == 0)
        def _():
            dst_scr[...] = jnp.zeros_like(dst_scr)
            dlb_scr[...] = jnp.zeros_like(dlb_scr)
            dgain_ref[...] = jnp.zeros_like(dgain_ref)
            for cp in _rs_chip_copies(p_refs, got_refs, ssem, rsem):
                cp.start()

        lb = _lower_bound(lbl_ref[...])
        gain = gain_ref[...]
        o = o_ref[...]
        r = lax.rsqrt(_head_mean(o * o) + RMS_EPS)
        nrm = o * r
        gate = g_ref[...]
        sgt = _sigmoid(gate)
        dy = dy_ref[...]
        dg_ref[...] = (dy * nrm * gain * (sgt * (1.0 + gate * (1.0 - sgt)))).astype(BF16)
        dng = dy * (gate * sgt)
        dgain_ref[...] += _colsum(dng * nrm)
        dn = dng * gain
        do_scr[...] = r * (dn - nrm * _head_mean(dn * nrm))

        r64 = lax.broadcasted_iota(jnp.int32, (CHUNK, CHUNK), 0)
        c64 = lax.broadcasted_iota(jnp.int32, (CHUNK, CHUNK), 1)
        tril = r64 >= c64
        tri = tril.astype(BF16)
        triu = (r64 <= c64).astype(BF16)
        dst = dst_scr[...]
        dlb = dlb_scr[...]
        for cc in reversed(range(cpb)):
            sl = slice(cc * CHUNK, (cc + 1) * CHUNK)
            qp, sq, qf, sg, f, kf, v, b, bm, bl, qt, kt = _hgrn_chunk(q_ref, f_ref, i_ref, sl, lb, tri)
            stf = st_ref[cc].astype(F32)
            st_end = (st_ref[cc + 1] if cc + 1 < cpb else stn_ref[0]).astype(F32)
            csum = jnp.sum(st_end * dst, axis=0, keepdims=True)
            doc = do_scr[sl, :]
            dob, dstb = doc.astype(BF16), dst.astype(BF16)
            eb = jnp.exp(b)
            qe = (qf * eb).astype(BF16)
            kh = (kf * jnp.exp(bl - b)).astype(BF16)
            qtb, ktb = qt.astype(BF16), kt.astype(BF16)
            parts = []
            for h in range(HGRN_W // BLK):
                hs = slice(h * BLK, (h + 1) * BLK)
                da = jnp.where(tril, _dot3(doc[:, hs], v[:, hs], NT), 0.0)
                a = jnp.where(tril, _dot(qtb[:, hs], ktb[:, hs], NT), 0.0).astype(BF16)
                parts.append((
                    _dot3(da, kt[:, hs], NN), _dot3(doc[:, hs], stf[:, hs], NN),
                    _dot3(da, qt[:, hs], TN), _dot3(v[:, hs], dst[:, hs], NN),
                    _dot(a, dob[:, hs], TN) + _dot(kh[:, hs], dstb[:, hs], NT),
                    _dot(dob[:, hs], qe[:, hs], TN)))
            dqt, dqi, dkt, dks, dv, upd = (jnp.concatenate([p[n] for p in parts], axis=1) for n in range(6))
            dqf = dqt * jnp.exp(b - bm) + eb * dqi
            dkf = dkt * jnp.exp(bm - b) + jnp.exp(bl - b) * dks
            gq = qf * dqf - kf * dkf
            dlogf = csum + _cumsum_rows(triu, gq)
            dfv = dlogf / f - dkf
            dq_ref[sl, :] = (dqf * (sq * (1.0 + qp * (1.0 - sq)))).astype(BF16)
            df_ref[sl, :] = (dfv * (1.0 - lb) * sg * (1.0 - sg)).astype(BF16)
            di_ref[sl, :] = dv.astype(BF16)
            dst = dst * jnp.exp(bl) + upd
            dlb = dlb + _colsum(dfv * (1.0 - sg))
        dst_scr[...] = dst
        dlb_scr[...] = dlb

        @pl.when(step == nb - 1)
        def _():
            t = dlb * lb * (1.0 - lb)
            dlbl_ref[...] = jnp.concatenate([t, -t], axis=0)
            for cp in _rs_chip_copies(p_refs, got_refs, ssem, rsem):
                cp.wait()

    rev = lambda i: nb - 1 - i
    row = pl.BlockSpec((tb, HGRN_W), lambda i: (rev(i), 0))
    rs_shape, rs_sems = _rs_chips_shapes(psums)
    res = pl.pallas_call(
        body, name="hgrn_bwd", grid=(nb,),
        in_specs=_hgrn_specs(tb, rev) + [
            row, pl.BlockSpec((cpb, BLK, HGRN_W), lambda i: (rev(i), 0, 0)),
            pl.BlockSpec((1, BLK, HGRN_W), lambda i: (jnp.minimum((rev(i) + 1) * cpb, nc - 1), 0, 0)),
            pl.BlockSpec((tb, HGRN_W), lambda i: (rev(i), ATTN_W // HGRN_W)),
            pl.BlockSpec((2, HGRN_W), lambda i: (0, 0)), pl.BlockSpec((1, HGRN_W), lambda i: (0, 0))] + [ANY] * nk,
        out_specs=[row, row, row, row, pl.BlockSpec((1, HGRN_W), lambda i: (0, 0)),
                   pl.BlockSpec((2, HGRN_W), lambda i: (0, 0))] + [ANY] * nk,
        out_shape=[_sds((s, HGRN_W), BF16)] * 4 + [_sds((1, HGRN_W), F32), _sds((2, HGRN_W), F32)] + rs_shape,
        scratch_shapes=[pltpu.VMEM((tb, HGRN_W), F32), pltpu.VMEM((BLK, HGRN_W), F32), pltpu.VMEM((1, HGRN_W), F32)]
        + rs_sems,
        compiler_params=_cp(("arbitrary",)),
    )(proj, proj, proj, proj, o_pre, states, states, dcat, lb_logits, out_gain, *psums)
    return res


def _place():
    return lax.axis_index("x"), lax.axis_index("y"), lax.axis_index("c")


def _flip(x, y, ox, oy):
    return (1 - x if ox else x), (1 - y if oy else y)


def _half(rows, cc):
    return pl.ds(cc * (rows // 2), rows // 2)


def _remote(src, dst, ssem, rsem, to):
    return pltpu.make_async_remote_copy(src_ref=src, dst_ref=dst, send_sem=ssem, recv_sem=rsem,
                                        device_id=to, device_id_type=MESH)


def _ag_chip_copies(ins, outs, ssem, rsem):
    x, y, c = _place()
    j = 2 * x + y
    cps = []
    for k in range(len(ins)):
        rows = ins[k].shape[0]
        for idx, (ox, oy) in enumerate(FLIPS):
            px, py = _flip(x, y, ox, oy)
            cps.append(_remote(ins[k].at[_half(rows, c)], outs[k].at[j, _half(rows, c)],
                               ssem.at[k, idx], rsem.at[k, idx], (px, py, c)))
    return cps


def _ag_start(ins, outs, ssem, rsem):
    for cp in _ag_chip_copies(ins, outs, ssem, rsem):
        cp.start()


def _ag_finish(ins, outs, ssem, rsem):
    x, y, c = _place()
    sib = (x, y, 1 - c)
    passed = []
    for k in range(len(ins)):
        rows = ins[k].shape[0]
        for idx, (ox, oy) in enumerate(FLIPS):
            px, py = _flip(x, y, ox, oy)
            blk = outs[k].at[2 * px + py, _half(rows, c)]
            _remote(blk, blk, ssem.at[k, idx], rsem.at[k, idx], (px, py, c)).wait_recv()
            cp = _remote(blk, blk, ssem.at[k, 3 + idx], rsem.at[k, 3 + idx], sib)
            cp.start()
            passed.append(cp)
    for k in range(len(ins)):
        rows = ins[k].shape[0]
        for idx, (ox, oy) in enumerate(FLIPS):
            px, py = _flip(x, y, ox, oy)
            blk = outs[k].at[2 * px + py, _half(rows, 1 - c)]
            _remote(blk, blk, ssem.at[k, 3 + idx], rsem.at[k, 3 + idx], sib).wait_recv()
    for cp in _ag_chip_copies(ins, outs, ssem, rsem) + passed:
        cp.wait_send()


def _ag_shapes(shards):
    nk = len(shards)
    return ([_sds((N_CHIPS,) + tuple(w.shape), w.dtype) for w in shards],
            [pltpu.SemaphoreType.DMA((nk, 6)), pltpu.SemaphoreType.DMA((nk, 6))])


def _with_own(gathered, shard, j):
    return lax.dynamic_update_index_in_dim(gathered, shard, j, 0)


def _ag_weights(name, shards):
    nk = len(shards)

    def body(*refs):
        ins, outs = refs[:nk], refs[nk:2 * nk]
        ssem, rsem = refs[2 * nk:]
        _ag_start(ins, outs, ssem, rsem)
        _ag_finish(ins, outs, ssem, rsem)

    out_shape, sems = _ag_shapes(shards)
    return pl.pallas_call(body, name=name, in_specs=[ANY] * nk, out_specs=[ANY] * nk, out_shape=out_shape,
                          scratch_shapes=sems)(*shards)


def _rs_pair(name, grads):
    nk = len(grads)

    def body(*refs):
        ins, outs = refs[:nk], refs[nk:2 * nk]
        ssem, rsem = refs[2 * nk:]
        x, y, c = _place()
        cps = []
        for k in range(nk):
            rows = grads[k].shape[1]
            cp = pltpu.make_async_remote_copy(src_ref=ins[k].at[:, _half(rows, 1 - c)], dst_ref=outs[k],
                                              send_sem=ssem.at[k], recv_sem=rsem.at[k],
                                              device_id=(x, y, 1 - c), device_id_type=MESH)
            cp.start()
            cps.append(cp)
        for cp in cps:
            cp.wait()

    return pl.pallas_call(
        body, name=name, in_specs=[ANY] * nk, out_specs=[ANY] * nk,
        out_shape=[_sds((N_CHIPS, g.shape[1] // 2, g.shape[2]), g.dtype) for g in grads],
        scratch_shapes=[pltpu.SemaphoreType.DMA((nk,)), pltpu.SemaphoreType.DMA((nk,))],
    )(*grads)


def _rs_chip_copies(ins, outs, ssem, rsem):
    x, y, c = _place()
    cps = []
    for k in range(len(ins)):
        for idx, (ox, oy) in enumerate(FLIPS):
            px, py = _flip(x, y, ox, oy)
            cps.append(_remote(ins[k].at[2 * px + py], outs[k].at[idx], ssem.at[k, idx], rsem.at[k, idx], (px, py, c)))
    return cps


def _rs_chips_shapes(psums):
    nk = len(psums)
    return ([_sds((3,) + tuple(p.shape[1:]), p.dtype) for p in psums],
            [pltpu.SemaphoreType.DMA((nk, 3)), pltpu.SemaphoreType.DMA((nk, 3))])


def _rs_chips(name, psums):
    nk = len(psums)

    def body(*refs):
        ins, outs = refs[:nk], refs[nk:2 * nk]
        ssem, rsem = refs[2 * nk:]
        for cp in _rs_chip_copies(ins, outs, ssem, rsem):
            cp.start()
        for cp in _rs_chip_copies(ins, outs, ssem, rsem):
            cp.wait()

    out_shape, sems = _rs_chips_shapes(psums)
    return pl.pallas_call(body, name=name, in_specs=[ANY] * nk, out_specs=[ANY] * nk, out_shape=out_shape,
                          scratch_shapes=sems)(*psums)


def _rs_share(fulls):
    nk = len(fulls)

    def body(*refs):
        ins, outs = refs[:nk], refs[nk:2 * nk]
        ssem, rsem = refs[2 * nk:]
        x, y, c = _place()
        cps = []
        for k in range(nk):
            rows = fulls[k].shape[0]
            cp = _remote(ins[k].at[_half(rows, c)], outs[k].at[_half(rows, c)], ssem.at[k], rsem.at[k], (x, y, 1 - c))
            cp.start()
            cps.append(cp)
        for k, cp in enumerate(cps):
            rows = fulls[k].shape[0]
            cp.wait_send()
            theirs = outs[k].at[_half(rows, 1 - c)]
            _remote(theirs, theirs, ssem.at[k], rsem.at[k], (x, y, 1 - c)).wait_recv()

    return pl.pallas_call(
        body, name="rs_share", in_specs=[ANY] * nk, out_specs=[ANY] * nk,
        out_shape=[_sds(f.shape, f.dtype) for f in fulls], input_output_aliases={k: k for k in range(nk)},
        scratch_shapes=[pltpu.SemaphoreType.DMA((nk,)), pltpu.SemaphoreType.DMA((nk,))],
    )(*fulls)


def _allreduce_small(v):
    ndev = 8

    def body(in_ref, out_ref, buf, ssem, rsem):
        x, y, c = _place()
        me = 4 * x + 2 * y + c
        buf[me] = in_ref[...]
        cps = []
        for k in range(1, ndev):
            ox, oy, oc = (k >> 2) & 1, (k >> 1) & 1, k & 1
            px, py = _flip(x, y, ox, oy)
            pc = 1 - c if oc else c
            cp = pltpu.make_async_remote_copy(src_ref=in_ref, dst_ref=buf.at[me], send_sem=ssem.at[k - 1],
                                              recv_sem=rsem.at[k - 1], device_id=(px, py, pc), device_id_type=MESH)
            cp.start()
            cps.append((cp, 4 * px + 2 * py + pc, (px, py, pc)))
        for k, (cp, src, peer) in enumerate(cps):
            cp.wait_send()
            pltpu.make_async_remote_copy(src_ref=in_ref, dst_ref=buf.at[src], send_sem=ssem.at[k],
                                         recv_sem=rsem.at[k], device_id=peer, device_id_type=MESH).wait_recv()
        acc = buf[0]
        for i in range(1, ndev):
            acc = acc + buf[i]
        out_ref[...] = acc

    return pl.pallas_call(
        body, name="allreduce_small",
        in_specs=[pl.BlockSpec(memory_space=pltpu.VMEM)], out_specs=pl.BlockSpec(memory_space=pltpu.VMEM),
        out_shape=_sds(v.shape, v.dtype),
        scratch_shapes=[pltpu.VMEM((ndev,) + v.shape, v.dtype), pltpu.SemaphoreType.DMA((ndev - 1,)),
                        pltpu.SemaphoreType.DMA((ndev - 1,))],
    )(v)


def _rs_sum1(name, g, recv, c_idx):
    _, r, cdim = g.shape
    hr = r // 2
    tr = min(hr, 256)
    nr = hr // tr

    def body(c_ref, g_ref, r_ref, o32_ref, o16_ref):
        v = g_ref[...] + r_ref[...].astype(F32)
        o32_ref[...] = v
        o16_ref[...] = v.astype(BF16)

    spec = pl.BlockSpec((None, tr, cdim), lambda j, i, c_ref: (j, i, 0))
    return pl.pallas_call(
        body, name=name,
        grid_spec=pltpu.PrefetchScalarGridSpec(
            num_scalar_prefetch=1, grid=(N_CHIPS, nr),
            in_specs=[pl.BlockSpec((None, tr, cdim), lambda j, i, c_ref: (j, c_ref[0] * nr + i, 0)), spec],
            out_specs=[spec, spec]),
        out_shape=[_sds((N_CHIPS, hr, cdim), F32), _sds((N_CHIPS, hr, cdim), BF16)],
        compiler_params=_cp(("parallel", "parallel")),
    )(c_idx, g, recv)


def _rs_sum2(name, p32, recv, jc_idx):
    _, hr, cdim = p32.shape
    tr = min(hr, 256)
    nr = hr // tr

    def body(jc_ref, p_ref, r_ref, o_ref):
        o_ref[...] = ((p_ref[...] + r_ref[0].astype(F32)) + r_ref[1].astype(F32)) + r_ref[2].astype(F32)

    return pl.pallas_call(
        body, name=name,
        grid_spec=pltpu.PrefetchScalarGridSpec(
            num_scalar_prefetch=1, grid=(nr,),
            in_specs=[pl.BlockSpec((None, tr, cdim), lambda i, jc: (jc[0], i, 0)),
                      pl.BlockSpec((3, tr, cdim), lambda i, jc: (0, i, 0))],
            out_specs=pl.BlockSpec((tr, cdim), lambda i, jc: (jc[1] * nr + i, 0))),
        out_shape=_sds((2 * hr, cdim), F32),
        compiler_params=_cp(("parallel",)),
    )(jc_idx, p32, recv)


def _adamw(name, w, g, m, v):
    r, cdim = w.shape
    tr = min(r, 256)
    c1 = 1.0 - ADAM_B1 ** ADAM_STEP
    c2 = 1.0 - ADAM_B2 ** ADAM_STEP

    def body(w_ref, g_ref, m_ref, v_ref, d_ref, nm_ref, nv_ref):
        gv = g_ref[...]
        nm = ADAM_B1 * m_ref[...] + (1.0 - ADAM_B1) * gv
        nv = ADAM_B2 * v_ref[...] + (1.0 - ADAM_B2) * (gv * gv)
        d_ref[...] = -ADAM_LR * ((nm / c1) / (jnp.sqrt(nv / c2) + ADAM_EPS) + ADAM_WD * w_ref[...])
        nm_ref[...] = nm
        nv_ref[...] = nv

    spec = pl.BlockSpec((tr, cdim), lambda i: (i, 0))
    return pl.pallas_call(
        body, name=name, grid=(r // tr,), in_specs=[spec] * 4, out_specs=[spec] * 3,
        out_shape=[_sds((r, cdim), F32)] * 3, compiler_params=_cp(("parallel",)),
    )(w, g, m, v)


def _pack_small(mix_pre, attn_out, lb_logits, hgrn_out, mix_post, mlp_pre, mlp_post):
    rows = [mix_pre, jnp.concatenate([attn_out, hgrn_out], axis=1),
            jnp.concatenate([lb_logits[0:1], lb_logits[1:2]], axis=1), mix_post, mlp_pre, mlp_post,
            jnp.zeros((2, D_MODEL), F32)]
    return jnp.concatenate(rows, axis=0)


def _unpack_small(p):
    return (p[0:1], p[1:2, :ATTN_W], jnp.concatenate([p[2:3, :HGRN_W], p[2:3, HGRN_W:]], axis=0),
            p[1:2, ATTN_W:], p[3:4], p[4:5], p[5:6])


def kernel(x, mix_pre_norm, w_in, attn_out_norm, hgrn_lb_logits, hgrn_out_norm, w_out, mix_post_norm, mlp_pre_norm, w_ff1, w_ff2, mlp_post_norm, loss_target, m_mix_pre_norm, m_w_in, m_attn_out_norm, m_hgrn_lb_logits, m_hgrn_out_norm, m_w_out, m_mix_post_norm, m_mlp_pre_norm, m_w_ff1, m_w_ff2, m_mlp_post_norm, v_mix_pre_norm, v_w_in, v_attn_out_norm, v_hgrn_lb_logits, v_hgrn_out_norm, v_w_out, v_mix_post_norm, v_mlp_pre_norm, v_w_ff1, v_w_ff2, v_mlp_post_norm):
    s = x.shape[1]
    xs = x.reshape(s, D_MODEL)
    tgt = loss_target.reshape(s, D_MODEL)
    cx, cy, cc = _place()
    chip = 2 * cx + cy
    c_idx = jnp.reshape(cc, (1,)).astype(jnp.int32)
    jc_idx = jnp.stack([chip, cc]).astype(jnp.int32)

    big_w = [w_in[0], w_out[0], w_ff1[0], w_ff2[0]]
    big_m = [m_w_in[0], m_w_out[0], m_w_ff1[0], m_w_ff2[0]]
    big_v = [v_w_in[0], v_w_out[0], v_w_ff1[0], v_w_ff2[0]]
    shards = [w.astype(BF16) for w in big_w]

    (wg_in,) = _ag_weights("ag_in", shards[:1])
    wg_in = _with_own(wg_in, shards[0], chip)

    (h,) = _rows_call("norm_in", lambda xv, g: ((xv * _rstd(xv) * g),),
                      [(xs, _row(D_MODEL)), (mix_pre_norm, "full")], [(D_MODEL, BF16, "row")], s)
    (proj,) = _mm_cols("mm_proj", h, wg_in, NN, [F32])
    hg_o, rec, states, wg_out, wg_1, wg_2 = _hgrn_fwd(proj, hgrn_lb_logits, hgrn_out_norm, shards[1:])
    wg_out, wg_1, wg_2 = (_with_own(g, w, chip) for g, w in zip((wg_out, wg_1, wg_2), shards[1:]))
    att = [_attn_fwd(proj, d) for d in DILATIONS]

    def comb(o1, o2, o3, l1, l2, l3, gain):
        mx = jnp.maximum(jnp.maximum(l1, l2), l3)
        lse = mx + jnp.log(jnp.exp(l1 - mx) + jnp.exp(l2 - mx) + jnp.exp(l3 - mx))
        o = jnp.exp(l1 - lse) * o1 + jnp.exp(l2 - lse) * o2 + jnp.exp(l3 - lse) * o3
        return o, lse, o * _rstd(o) * gain

    attn_o, attn_lse, attn_n = _rows_call(
        "attn_comb", comb, [(a[0], _row(ATTN_W)) for a in att] + [(a[1], _row(ATTN_W)) for a in att]
        + [(attn_out_norm, "full")], [(ATTN_W, F32, "row"), (ATTN_W, F32, "row"), (ATTN_W, BF16, "row")], s)
    cat = jnp.concatenate([attn_n, rec], axis=1)
    mixed = _mm_acc("mm_mixed", cat, wg_out, NN)

    def post1(xv, mv, g_post, g_pre2):
        x1 = xv + mv * _rstd(mv) * g_post
        return x1, x1 * _rstd(x1) * g_pre2

    x1, h2 = _rows_call("post1", post1, [(xs, _row(D_MODEL)), (mixed, _row(D_MODEL)), (mix_post_norm, "full"),
                                         (mlp_pre_norm, "full")], [(D_MODEL, F32, "row"), (D_MODEL, BF16, "row")], s)

    def sq_relu(u):
        r = jnp.maximum(u, 0.0)
        return r * r, r

    act, ru = _mm_cols("mm_ff1", h2, wg_1, NN, [BF16, BF16], epi=sq_relu)
    ff = _mm_acc("mm_ff2", act, wg_2, NN)

    def post2(x1v, fv, tv, g):
        y = x1v + fv * _rstd(fv) * g
        dy = (y - tv) * (1.0 / D_MODEL)
        err = y - tv
        loss = 0.5 * jnp.sum(jnp.mean(err * err, axis=-1, keepdims=True), axis=0, keepdims=True)
        dff, dgc = _norm_bwd(fv, g, dy)
        return dy, dff, _colsum(dgc), jnp.broadcast_to(loss, (1, BLK))

    dy, dff, g_mlp_post, loss_part = _rows_call(
        "post2", post2, [(x1, _row(D_MODEL)), (ff, _row(D_MODEL)), (tgt, _row(D_MODEL)), (mlp_post_norm, "full")],
        [(D_MODEL, F32, "row"), (D_MODEL, BF16, "row"), (D_MODEL, F32, "acc"), (BLK, F32, "acc")], s)

    (du,) = _mm_cols("mm_du", dff, wg_2, NT, [BF16], epi=lambda acc, r: (acc * (2.0 * r.astype(F32)),),
                     extras=(ru,))
    gw_2 = _mm_wgrad("mm_gw2", act, dff, True)
    gw_1 = _mm_wgrad("mm_gw1", h2, du, False)
    dh2 = _mm_acc("mm_dh2", du, wg_1, NT)

    def bwd_mid(dyv, dh2v, x1v, mv, g_pre2, g_post):
        d1, gc1 = _norm_bwd(x1v, g_pre2, dh2v)
        dx1 = dyv + d1
        dm, gc2 = _norm_bwd(mv, g_post, dx1)
        return dx1, dm, _colsum(gc1), _colsum(gc2)

    dx1, dmixed, g_mlp_pre, g_mix_post = _rows_call(
        "bwd_mid", bwd_mid, [(dy, _row(D_MODEL)), (dh2, _row(D_MODEL)), (x1, _row(D_MODEL)), (mixed, _row(D_MODEL)),
                             (mlp_pre_norm, "full"), (mix_post_norm, "full")],
        [(D_MODEL, F32, "row"), (D_MODEL, BF16, "row"), (D_MODEL, F32, "acc"), (D_MODEL, F32, "acc")], s)

    (dcat,) = _mm_cols("mm_dcat", dmixed, wg_out, NT, [F32])
    gw_out = _mm_wgrad("mm_gwout", cat, dmixed, True)

    names = ["out", "ff1", "ff2", "in"]
    ready = [gw_out, gw_1, gw_2]
    from_pair = _rs_pair("rs_pair_3", [g.astype(BF16) for g in ready])
    pair = [_rs_sum1(f"rs_sum1_{n}", g, r, c_idx) for n, g, r in zip(names, ready, from_pair)]

    def attn_norm_bwd(dc, o, gain):
        do, gc = _norm_bwd(o, gain, dc)
        t = do * o
        lane = lax.broadcasted_iota(jnp.int32, (t.shape[0], BLK), 1) < 64
        parts = []
        for p in range(ATTN_W // BLK):
            tp = t[:, p * BLK:(p + 1) * BLK]
            sa = jnp.sum(jnp.where(lane, tp, 0.0), axis=1, keepdims=True)
            sb = jnp.sum(jnp.where(lane, 0.0, tp), axis=1, keepdims=True)
            parts.append(jnp.where(lane, sa, sb))
        return do, jnp.concatenate(parts, axis=1), _colsum(gc)

    do_attn, delta, g_attn_out = _rows_call(
        "attn_norm_bwd", attn_norm_bwd, [(dcat, _row(ATTN_W, 0)), (attn_o, _row(ATTN_W)), (attn_out_norm, "full")],
        [(ATTN_W, F32, "row"), (ATTN_W, F32, "row"), (ATTN_W, F32, "acc")], s)
    dqkv = [_attn_bwd(proj, do_attn, attn_lse, delta, d) for d in DILATIONS]
    dhq, dhf, dhi, dhg, g_hgrn_out, g_lb, *from_chips = _hgrn_bwd(
        proj, hg_o, states, dcat, hgrn_lb_logits, hgrn_out_norm, [p[1] for p in pair])
    dh_parts = [dhq, dhf, dhi, dhg]

    def dproj_asm(*a):
        a1, a4, a16 = a[0:3], a[3:6], a[6:9]
        return (jnp.concatenate([(a1[k] + a4[k] + a16[k]).astype(BF16) for k in range(3)] + list(a[9:]), axis=1),)

    (dproj,) = _rows_call("dproj_asm", dproj_asm,
                          [(t, _row(ATTN_W)) for grp in dqkv for t in grp] + [(t, _row(HGRN_W)) for t in dh_parts],
                          [(PROJ_W, BF16, "row")], s)
    dh = _mm_acc("mm_dh", dproj, wg_in, NT)
    gw_in = _mm_wgrad("mm_gwin", h, dproj, False)

    def bwd_in(dx1v, dhv, xv, g):
        d0, gc = _norm_bwd(xv, g, dhv)
        return dx1v + d0, _colsum(gc)

    grad_x, g_mix_pre = _rows_call("bwd_in", bwd_in, [(dx1, _row(D_MODEL)), (dh, _row(D_MODEL)), (xs, _row(D_MODEL)),
                                                      (mix_pre_norm, "full")],
                                   [(D_MODEL, F32, "row"), (D_MODEL, F32, "acc")], s)

    loss = lax.psum(loss_part[0, 0], ("x", "y", "c"))
    small_g = _allreduce_small(_pack_small(g_mix_pre, g_attn_out, g_lb, g_hgrn_out, g_mix_post, g_mlp_pre, g_mlp_post))

    (from_pair_in,) = _rs_pair("rs_pair_in", [gw_in.astype(BF16)])
    pair.append(_rs_sum1("rs_sum1_in", gw_in, from_pair_in, c_idx))
    from_chips += _rs_chips("rs_chips_in", [pair[3][1]])
    reduced = [_rs_sum2(f"rs_sum2_{n}", p[0], r, jc_idx) for n, p, r in zip(names, pair, from_chips)]
    g_wout, g_w1, g_w2, g_win = _rs_share(reduced)
    full = [g_win, g_wout, g_w1, g_w2]

    upd = [_adamw(f"adamw_{n}", w, g, m, v) for n, w, g, m, v in zip(("in", "out", "ff1", "ff2"), big_w, full, big_m, big_v)]
    small_w = _pack_small(mix_pre_norm, attn_out_norm, hgrn_lb_logits, hgrn_out_norm, mix_post_norm, mlp_pre_norm,
                          mlp_post_norm)
    small_m = _pack_small(m_mix_pre_norm, m_attn_out_norm, m_hgrn_lb_logits, m_hgrn_out_norm, m_mix_post_norm,
                          m_mlp_pre_norm, m_mlp_post_norm)
    small_v = _pack_small(v_mix_pre_norm, v_attn_out_norm, v_hgrn_lb_logits, v_hgrn_out_norm, v_mix_post_norm,
                          v_mlp_pre_norm, v_mlp_post_norm)
    small_upd = _adamw("adamw_small", small_w, small_g, small_m, small_v)

    def assemble(small, big):
        sm = _unpack_small(small)
        return (sm[0], big[0][None], sm[1], sm[2], sm[3], big[1][None], sm[4], sm[5], big[2][None], big[3][None], sm[6])

    g_out = assemble(small_g, full)
    d_out = assemble(small_upd[0], [u[0] for u in upd])
    m_out = assemble(small_upd[1], [u[1] for u in upd])
    v_out = assemble(small_upd[2], [u[2] for u in upd])
    return (loss, grad_x.reshape(x.shape), *g_out, *d_out, *m_out, *v_out)
```

```python
import numpy as np
import jax
import jax.numpy as jnp
from jax import lax
from jax.experimental import pallas as pl
from jax.experimental.pallas import tpu as pltpu

F32 = jnp.float32
BF16 = jnp.bfloat16
MESH = pl.DeviceIdType.MESH
ANY = pl.BlockSpec(memory_space=pl.ANY)

RMS_EPS = 1e-6
D_MODEL = 1024
ATTN_W = 512
HGRN_W = 512
PROJ_W = 3584
D_FF = 4096
N_CHIPS = 4
BLK = 128
CHUNK = 64
HGRN_TB = 512
DILATIONS = (1, 4, 16)
ATTN_SCALE = 0.125
ROW_TILE = 512
MM_TILE = 1024
VMEM_LIMIT = 48 * 2 ** 20
FLIPS = ((1, 0), (0, 1), (1, 1))

ADAM_LR, ADAM_B1, ADAM_B2, ADAM_EPS, ADAM_WD, ADAM_STEP = 0.001, 0.9, 0.999, 1e-08, 0.01, 10


def _cp(sem=None):
    return pltpu.CompilerParams(dimension_semantics=sem, vmem_limit_bytes=VMEM_LIMIT)


def _sigmoid(v):
    return 1.0 / (1.0 + jnp.exp(-v))


def _dot(a, b, contract, precision=None):
    return lax.dot_general(a, b, (contract, ((), ())), preferred_element_type=F32, precision=precision)


NN = ((1,), (0,))
NT = ((1,), (1,))
TN = ((0,), (0,))


def _matmul(name, a, b, *, grid, a_spec, b_spec, contract, outs, epi=None, extras=(), extra_specs=(),
            acc_shape=None):
    n_ex, n_out, nj = len(extras), len(outs), grid[-1]

    def body(a_ref, b_ref, *rest):
        ex, out_refs = rest[:n_ex], rest[n_ex:n_ex + n_out]

        def finish(acc):
            res = epi(acc, *[e[...] for e in ex]) if epi else (acc,)
            for o, r in zip(out_refs, res):
                o[...] = r.astype(o.dtype)

        p = _dot(a_ref[...], b_ref[...], contract)
        if acc_shape is None:
            finish(p)
        else:
            acc_ref = rest[-1]
            j = pl.program_id(len(grid) - 1)

            @pl.when(j == 0)
            def _():
                acc_ref[...] = p

            @pl.when(j > 0)
            def _():
                acc_ref[...] += p

            @pl.when(j == nj - 1)
            def _():
                finish(acc_ref[...])

    sem = ("parallel",) * len(grid) if acc_shape is None else ("parallel",) * (len(grid) - 1) + ("arbitrary",)
    res = pl.pallas_call(
        body, name=name, grid=grid,
        in_specs=[a_spec, b_spec, *extra_specs],
        out_specs=[s for _, s in outs],
        out_shape=[s for s, _ in outs],
        scratch_shapes=[] if acc_shape is None else [pltpu.VMEM(acc_shape, F32)],
        compiler_params=_cp(sem),
    )(a, b, *extras)
    return res


def _sds(shape, dtype):
    return jax.ShapeDtypeStruct(shape, dtype)


def _mm_cols(name, a, w, contract, out_dtypes, epi=None, extras=()):
    m, k = a.shape
    jn = w.shape[0]
    nj = w.shape[2] if contract == NN else w.shape[1]
    tm = min(m, MM_TILE)
    outs = [(_sds((m, jn * nj), dt), pl.BlockSpec((tm, nj), lambda j, i: (i, j))) for dt in out_dtypes]
    return _matmul(name, a, w, grid=(jn, m // tm),
                   a_spec=pl.BlockSpec((tm, k), lambda j, i: (i, 0)),
                   b_spec=pl.BlockSpec((None,) + w.shape[1:], lambda j, i: (j, 0, 0)),
                   contract=contract, outs=outs, epi=epi, extras=extras,
                   extra_specs=[pl.BlockSpec((tm, nj), lambda j, i: (i, j)) for _ in extras])


def _mm_acc(name, a, w, contract):
    m = a.shape[0]
    jn = w.shape[0]
    kj = w.shape[1] if contract == NN else w.shape[2]
    n = w.shape[2] if contract == NN else w.shape[1]
    tm = min(m, MM_TILE)
    outs = [(_sds((m, n), F32), pl.BlockSpec((tm, n), lambda i, j: (i, 0)))]
    return _matmul(name, a, w, grid=(m // tm, jn),
                   a_spec=pl.BlockSpec((tm, kj), lambda i, j: (i, j)),
                   b_spec=pl.BlockSpec((None,) + w.shape[1:], lambda i, j: (j, 0, 0)),
                   contract=contract, outs=outs, acc_shape=(tm, n))[0]


def _mm_wgrad(name, a, b, a_by_j):
    s = a.shape[0]
    if a_by_j:
        r, c = a.shape[1] // N_CHIPS, b.shape[1]
    else:
        r, c = a.shape[1], b.shape[1] // N_CHIPS
    tr = min(r, 512)
    nr = r // tr
    if a_by_j:
        a_spec = pl.BlockSpec((s, tr), lambda j, i: (0, j * nr + i))
        b_spec = pl.BlockSpec((s, c), lambda j, i: (0, 0))
    else:
        a_spec = pl.BlockSpec((s, tr), lambda j, i: (0, i))
        b_spec = pl.BlockSpec((s, c), lambda j, i: (0, j))
    outs = [(_sds((N_CHIPS, r, c), F32), pl.BlockSpec((None, tr, c), lambda j, i: (j, i, 0)))]
    return _matmul(name, a, b, grid=(N_CHIPS, nr), a_spec=a_spec, b_spec=b_spec, contract=TN, outs=outs)[0]


def _rows_call(name, fn, ins, outs, s):
    tm = ROW_TILE
    in_specs = []
    for arr, kind in ins:
        if kind == "full":
            in_specs.append(pl.BlockSpec(arr.shape, lambda i: (0, 0)))
        else:
            _, w, cb = kind
            in_specs.append(pl.BlockSpec((tm, w), lambda i, cb=cb: (i, cb)))
    out_specs, out_shape, is_acc = [], [], []
    for w, dt, kind in outs:
        if kind == "acc":
            out_specs.append(pl.BlockSpec((1, w), lambda i: (0, 0)))
            out_shape.append(_sds((1, w), dt))
        else:
            out_specs.append(pl.BlockSpec((tm, w), lambda i: (i, 0)))
            out_shape.append(_sds((s, w), dt))
        is_acc.append(kind == "acc")
    n_in = len(ins)

    def body(*refs):
        i = pl.program_id(0)
        res = fn(*[r[...] for r in refs[:n_in]])
        for o, r, acc in zip(refs[n_in:], res, is_acc):
            if acc:
                @pl.when(i == 0)
                def _(o=o):
                    o[...] = jnp.zeros_like(o)
                o[...] += r.astype(o.dtype)
            else:
                o[...] = r.astype(o.dtype)

    sem = ("arbitrary",) if any(is_acc) else ("parallel",)
    return pl.pallas_call(body, name=name, grid=(s // tm,), in_specs=in_specs, out_specs=out_specs,
                          out_shape=out_shape, compiler_params=_cp(sem))(*[a for a, _ in ins])


def _rstd(v):
    return lax.rsqrt(jnp.mean(v * v, axis=-1, keepdims=True) + RMS_EPS)


def _norm_bwd(v, gain, dy):
    r = _rstd(v)
    n = v * r
    dn = dy * gain
    dv = r * (dn - n * jnp.mean(dn * n, axis=-1, keepdims=True))
    return dv, dy * n


def _colsum(v):
    return jnp.sum(v, axis=0, keepdims=True)


def _row(w, cb=0):
    return ("row", w, cb)


N_PAIRS = ATTN_W // BLK


def _head_col(v, mask):
    return jnp.max(jnp.where(mask, v, -jnp.inf), axis=1, keepdims=True)


def _slopes():
    t = np.zeros((N_PAIRS, 8, 2 * BLK), np.float32)
    for p in range(N_PAIRS):
        for hh in range(2):
            t[p, hh, :] = 2.0 ** -(2 * p + hh + 1)
    return jnp.asarray(t)


def _rows(n, r, d):
    base = pl.multiple_of(n * (BLK * d), BLK)
    return pl.ds(base + r, BLK, stride=d) if d > 1 else pl.ds(base, BLK)


def _attn_fwd(proj, shards):
    s = proj.shape[0]
    nk = len(shards)

    def body(sl_ref, q_ref, k_ref, v_ref, *rest):
        w_refs, (o_ref, l_ref) = rest[:nk], rest[nk:nk + 2]
        wg_refs, (ssem, rsem) = rest[nk + 2:2 * nk + 2], rest[2 * nk + 2:]
        pair = pl.program_id(0)

        @pl.when(pair == 0)
        def _():
            _ag_start(w_refs, wg_refs, ssem, rsem)

        row = lax.broadcasted_iota(jnp.int32, (BLK, 2 * BLK), 0)
        col = lax.broadcasted_iota(jnp.int32, (BLK, 2 * BLK), 1)
        dist = row + BLK - col
        in_window = (dist >= 0) & (dist <= BLK)
        distf = dist.astype(F32)
        lane_q = lax.broadcasted_iota(jnp.int32, (BLK, BLK), 1) < 64
        lane_k = lax.broadcasted_iota(jnp.int32, (2 * BLK, BLK), 1) < 64
        for di, d in enumerate(DILATIONS):
            nb = s // (BLK * d)
            for r in range(d):
                def block(n, carry, d=d, r=r, first=(di == 0)):
                    rows = _rows(n, r, d)
                    prev = _rows(jnp.maximum(n - 1, 0), r, d)
                    valid = in_window & (col + n * BLK >= BLK)
                    q2 = q_ref[rows, :]
                    kk = jnp.concatenate([k_ref[prev, :], k_ref[rows, :]], axis=0).astype(BF16)
                    vv = jnp.concatenate([v_ref[prev, :], v_ref[rows, :]], axis=0)
                    o2 = jnp.zeros((BLK, BLK), F32)
                    lse2 = jnp.zeros((BLK, BLK), F32)
                    for hh in range(2):
                        mq = lane_q if hh == 0 else ~lane_q
                        mk = lane_k if hh == 0 else ~lane_k
                        qm = jnp.where(mq, q2, 0.0).astype(BF16)
                        sc = _dot(qm, kk, NT) * ATTN_SCALE - (sl_ref[hh:hh + 1, :] * float(d)) * distf
                        sc = jnp.where(valid, sc, -1e30)
                        m = jnp.max(sc, axis=1, keepdims=True)
                        pr = jnp.exp(sc - m)
                        den = jnp.sum(pr, axis=1, keepdims=True)
                        vm = jnp.where(mk, vv, 0.0).astype(BF16)
                        o2 = o2 + _dot(pr.astype(BF16), vm, NN) / den
                        lse2 = jnp.where(mq, m + jnp.log(den), lse2)
                    if first:
                        o_ref[rows, :] = o2
                        l_ref[rows, :] = lse2
                    else:
                        lo = l_ref[rows, :]
                        mx = jnp.maximum(lo, lse2)
                        ln = mx + jnp.log(jnp.exp(lo - mx) + jnp.exp(lse2 - mx))
                        o_ref[rows, :] = jnp.exp(lo - ln) * o_ref[rows, :] + jnp.exp(lse2 - ln) * o2
                        l_ref[rows, :] = ln
                    return carry

                lax.fori_loop(0, nb, block, 0)

        @pl.when(pair == N_PAIRS - 1)
        def _():
            _ag_finish(w_refs, wg_refs, ssem, rsem)

    cb = lambda base: pl.BlockSpec((s, BLK), lambda p, base=base: (0, base + p))
    out = pl.BlockSpec((s, BLK), lambda p: (0, p))
    ag_shape, ag_sems = _ag_shapes(shards)
    return pl.pallas_call(
        body, name="attn_fwd", grid=(N_PAIRS,),
        in_specs=[pl.BlockSpec((None, 8, 2 * BLK), lambda p: (p, 0, 0)), cb(0), cb(N_PAIRS), cb(2 * N_PAIRS)]
        + [ANY] * nk,
        out_specs=[out, out] + [ANY] * nk, out_shape=[_sds((s, ATTN_W), F32)] * 2 + ag_shape,
        scratch_shapes=ag_sems, compiler_params=_cp(("arbitrary",)),
    )(_slopes(), proj, proj, proj, *shards)


def _attn_bwd(proj, do, lse, delta, psums):
    s = proj.shape[0]
    nk = len(psums)

    def body(sl_ref, q_ref, k_ref, v_ref, do_ref, l_ref, e_ref, *rest):
        p_refs, (dq_ref, dk_ref, dv_ref) = rest[:nk], rest[nk:nk + 3]
        got_refs, (ssem, rsem) = rest[nk + 3:2 * nk + 3], rest[2 * nk + 3:]
        pair = pl.program_id(0)

        @pl.when(pair == 0)
        def _():
            for cp in _rs_chip_copies(p_refs, got_refs, ssem, rsem):
                cp.start()

        row1 = lax.broadcasted_iota(jnp.int32, (BLK, 2 * BLK), 0)
        col1 = lax.broadcasted_iota(jnp.int32, (BLK, 2 * BLK), 1)
        dist1 = row1 + BLK - col1
        win1 = (dist1 >= 0) & (dist1 <= BLK)
        dist1f = dist1.astype(F32)
        row2 = lax.broadcasted_iota(jnp.int32, (2 * BLK, BLK), 0)
        col2 = lax.broadcasted_iota(jnp.int32, (2 * BLK, BLK), 1)
        dist2 = row2 - col2
        win2 = (dist2 >= 0) & (dist2 <= BLK)
        dist2f = dist2.astype(F32)
        lane1 = lax.broadcasted_iota(jnp.int32, (BLK, BLK), 1) < 64
        lane2 = lax.broadcasted_iota(jnp.int32, (2 * BLK, BLK), 1) < 64
        for di, d in enumerate(DILATIONS):
            nb = s // (BLK * d)
            for r in range(d):
                def block(n, carry, d=d, r=r, nb=nb, first=(di == 0)):
                    rows = _rows(n, r, d)
                    prev = _rows(jnp.maximum(n - 1, 0), r, d)
                    nxt = _rows(jnp.minimum(n + 1, nb - 1), r, d)
                    valid1 = win1 & (col1 + n * BLK >= BLK)
                    valid2 = win2 & (row2 - (nb - 1 - n) * BLK < BLK)
                    q1, d1, l1, e1 = q_ref[rows, :], do_ref[rows, :], l_ref[rows, :], e_ref[rows, :]
                    k1f, v1f = k_ref[rows, :], v_ref[rows, :]
                    kk = jnp.concatenate([k_ref[prev, :], k1f], axis=0)
                    vv = jnp.concatenate([v_ref[prev, :], v1f], axis=0).astype(BF16)
                    qq = jnp.concatenate([q1, q_ref[nxt, :]], axis=0)
                    dd = jnp.concatenate([d1, do_ref[nxt, :]], axis=0)
                    ll = jnp.concatenate([l1, l_ref[nxt, :]], axis=0)
                    ee = jnp.concatenate([e1, e_ref[nxt, :]], axis=0)
                    k1, v1, kkb = k1f.astype(BF16), v1f.astype(BF16), kk.astype(BF16)
                    dq2 = jnp.zeros((BLK, BLK), F32)
                    dk2 = jnp.zeros((BLK, BLK), F32)
                    dv2 = jnp.zeros((BLK, BLK), F32)
                    for hh in range(2):
                        m1 = lane1 if hh == 0 else ~lane1
                        m2 = lane2 if hh == 0 else ~lane2
                        slope1 = sl_ref[hh:hh + 1, :] * float(d)
                        slope2 = sl_ref[hh:hh + 1, :BLK] * float(d)
                        qm = jnp.where(m1, q1, 0.0).astype(BF16)
                        dm = jnp.where(m1, d1, 0.0).astype(BF16)
                        sc = _dot(qm, kkb, NT) * ATTN_SCALE - slope1 * dist1f
                        pr = jnp.where(valid1, jnp.exp(sc - _head_col(l1, m1)), 0.0)
                        ds = pr * (_dot(dm, vv, NT) - _head_col(e1, m1))
                        km = jnp.where(m2, kk, 0.0).astype(BF16)
                        dq2 = dq2 + _dot(ds.astype(BF16), km, NN) * ATTN_SCALE
                        qqm = jnp.where(m2, qq, 0.0).astype(BF16)
                        ddm = jnp.where(m2, dd, 0.0).astype(BF16)
                        sc = _dot(qqm, k1, NT) * ATTN_SCALE - slope2 * dist2f
                        pr = jnp.where(valid2, jnp.exp(sc - _head_col(ll, m2)), 0.0)
                        ds = pr * (_dot(ddm, v1, NT) - _head_col(ee, m2))
                        dk2 = dk2 + _dot(ds.astype(BF16), qqm, TN) * ATTN_SCALE
                        dv2 = dv2 + _dot(pr.astype(BF16), ddm, TN)
                    if first:
                        dq_ref[rows, :] = dq2
                        dk_ref[rows, :] = dk2
                        dv_ref[rows, :] = dv2
                    else:
                        dq_ref[rows, :] = dq_ref[rows, :] + dq2
                        dk_ref[rows, :] = dk_ref[rows, :] + dk2
                        dv_ref[rows, :] = dv_ref[rows, :] + dv2
                    return carry

                lax.fori_loop(0, nb, block, 0)

        @pl.when(pair == N_PAIRS - 1)
        def _():
            for cp in _rs_chip_copies(p_refs, got_refs, ssem, rsem):
                cp.wait()

    cb = lambda base: pl.BlockSpec((s, BLK), lambda p, base=base: (0, base + p))
    out = pl.BlockSpec((s, BLK), lambda p: (0, p))
    rs_shape, rs_sems = _rs_chips_shapes(psums)
    return pl.pallas_call(
        body, name="attn_bwd", grid=(N_PAIRS,),
        in_specs=[pl.BlockSpec((None, 8, 2 * BLK), lambda p: (p, 0, 0)), cb(0), cb(N_PAIRS), cb(2 * N_PAIRS),
                  out, out, out] + [ANY] * nk,
        out_specs=[out] * 3 + [ANY] * nk, out_shape=[_sds((s, ATTN_W), F32)] * 3 + rs_shape,
        scratch_shapes=rs_sems, compiler_params=_cp(("arbitrary",)),
    )(_slopes(), proj, proj, proj, do, lse, delta, *psums)


def _lower_bound(lbl):
    return 1.0 / (1.0 + jnp.exp(lbl[1:2, :] - lbl[0:1, :]))


def _hi(a):
    bits = lax.bitcast_convert_type(a, jnp.uint32) & jnp.uint32(0xFFFF0000)
    return lax.bitcast_convert_type(bits, F32)


def _dot3(a, b, contract):
    ah, bh = _hi(a), _hi(b)
    al, bl = (a - ah).astype(BF16), (b - bh).astype(BF16)
    ah, bh = ah.astype(BF16), bh.astype(BF16)
    return _dot(ah, bh, contract) + (_dot(ah, bl, contract) + _dot(al, bh, contract))


def _cumsum_rows(tri, g):
    g1 = _hi(g)
    r1 = g - g1
    g2 = _hi(r1)
    g3 = r1 - g2
    return _dot(tri, g1.astype(BF16), NN) + (_dot(tri, g2.astype(BF16), NN) + _dot(tri, g3.astype(BF16), NN))


def _heads(fn):
    return jnp.concatenate([fn(slice(h * BLK, (h + 1) * BLK)) for h in range(HGRN_W // BLK)], axis=1)


def _head_mean(t):
    return _heads(lambda hs: jnp.broadcast_to(jnp.mean(t[:, hs], axis=1, keepdims=True), (t.shape[0], BLK)))


def _hgrn_chunk(q_ref, f_ref, i_ref, sl, lb, tri):
    qp = q_ref[sl, :]
    sq = _sigmoid(qp)
    qf = qp * sq
    sg = _sigmoid(f_ref[sl, :])
    f = lb + (1.0 - lb) * sg
    kf = 1.0 - f
    v = i_ref[sl, :]
    b = _cumsum_rows(tri, jnp.log(f))
    bm = b[CHUNK // 2:CHUNK // 2 + 1, :]
    bl = b[CHUNK - 1:CHUNK, :]
    qt = qf * jnp.exp(b - bm)
    kt = kf * jnp.exp(bm - b)
    return qp, sq, qf, sg, f, kf, v, b, bm, bl, qt, kt


def _hgrn_specs(tb, block):
    first = 3 * ATTN_W // HGRN_W
    return [pl.BlockSpec((tb, HGRN_W), lambda i, k=k: (block(i), first + k)) for k in range(4)]


def _hgrn_fwd(proj, lb_logits, out_gain):
    s = proj.shape[0]
    tb = min(HGRN_TB, s)
    nb, cpb, nc = s // tb, tb // CHUNK, s // CHUNK

    def body(q_ref, f_ref, i_ref, g_ref, lbl_ref, gain_ref, o_ref, rec_ref, st_ref, st_scr):
        step = pl.program_id(0)

        @pl.when(step == 0)
        def _():
            st_scr[...] = jnp.zeros_like(st_scr)

        lb = _lower_bound(lbl_ref[...])
        r64 = lax.broadcasted_iota(jnp.int32, (CHUNK, CHUNK), 0)
        c64 = lax.broadcasted_iota(jnp.int32, (CHUNK, CHUNK), 1)
        tril = r64 >= c64
        tri = tril.astype(BF16)
        st = st_scr[...]
        for cc in range(cpb):
            sl = slice(cc * CHUNK, (cc + 1) * CHUNK)
            _, _, qf, _, _, kf, v, b, _, bl, qt, kt = _hgrn_chunk(q_ref, f_ref, i_ref, sl, lb, tri)
            qe = (qf * jnp.exp(b)).astype(BF16)
            kh = (kf * jnp.exp(bl - b)).astype(BF16)
            qtb, ktb, vb, stb = qt.astype(BF16), kt.astype(BF16), v.astype(BF16), st.astype(BF16)

            def out_h(hs):
                a = jnp.where(tril, _dot(qtb[:, hs], ktb[:, hs], NT), 0.0).astype(BF16)
                return _dot(qe[:, hs], stb[:, hs], NT) + _dot(a, vb[:, hs], NN)

            o_ref[sl, :] = _heads(out_h)
            st_ref[cc] = stb
            st = st * jnp.exp(bl) + _heads(lambda hs: _dot(vb[:, hs], kh[:, hs], TN))
        st_scr[...] = st
        o = o_ref[...]
        gate = g_ref[...]
        rec_ref[...] = (o * lax.rsqrt(_head_mean(o * o) + RMS_EPS) * gain_ref[...] * (gate * _sigmoid(gate))).astype(BF16)

    row = pl.BlockSpec((tb, HGRN_W), lambda i: (i, 0))
    return pl.pallas_call(
        body, name="hgrn_fwd", grid=(nb,),
        in_specs=_hgrn_specs(tb, lambda i: i) + [pl.BlockSpec((2, HGRN_W), lambda i: (0, 0)),
                                                 pl.BlockSpec((1, HGRN_W), lambda i: (0, 0))],
        out_specs=[row, row, pl.BlockSpec((cpb, BLK, HGRN_W), lambda i: (i, 0, 0))],
        out_shape=[_sds((s, HGRN_W), F32), _sds((s, HGRN_W), BF16), _sds((nc, BLK, HGRN_W), BF16)],
        scratch_shapes=[pltpu.VMEM((BLK, HGRN_W), F32)],
        compiler_params=_cp(("arbitrary",)),
    )(proj, proj, proj, proj, lb_logits, out_gain)


def _hgrn_bwd(proj, o_pre, states, dcat, lb_logits, out_gain):
    s = proj.shape[0]
    tb = min(HGRN_TB, s)
    nb, cpb, nc = s // tb, tb // CHUNK, s // CHUNK

    def body(q_ref, f_ref, i_ref, g_ref, o_ref, st_ref, stn_ref, dy_ref, lbl_ref, gain_ref,
             dq_ref, df_ref, di_ref, dg_ref, dgain_ref, dlbl_ref, do_scr, dst_scr, dlb_scr):
        step = pl.program_id(0)

        @pl.when(step == 0)
        def _():
            dst_scr[...] = jnp.zeros_like(dst_scr)
            dlb_scr[...] = jnp.zeros_like(dlb_scr)
            dgain_ref[...] = jnp.zeros_like(dgain_ref)

        lb = _lower_bound(lbl_ref[...])
        gain = gain_ref[...]
        o = o_ref[...]
        r = lax.rsqrt(_head_mean(o * o) + RMS_EPS)
        nrm = o * r
        gate = g_ref[...]
        sgt = _sigmoid(gate)
        dy = dy_ref[...]
        dg_ref[...] = (dy * nrm * gain * (sgt * (1.0 + gate * (1.0 - sgt)))).astype(BF16)
        dng = dy * (gate * sgt)
        dgain_ref[...] += _colsum(dng * nrm)
        dn = dng * gain
        do_scr[...] = r * (dn - nrm * _head_mean(dn * nrm))

        r64 = lax.broadcasted_iota(jnp.int32, (CHUNK, CHUNK), 0)
        c64 = lax.broadcasted_iota(jnp.int32, (CHUNK, CHUNK), 1)
        tril = r64 >= c64
        tri = tril.astype(BF16)
        triu = (r64 <= c64).astype(BF16)
        dst = dst_scr[...]
        dlb = dlb_scr[...]
        for cc in reversed(range(cpb)):
            sl = slice(cc * CHUNK, (cc + 1) * CHUNK)
            qp, sq, qf, sg, f, kf, v, b, bm, bl, qt, kt = _hgrn_chunk(q_ref, f_ref, i_ref, sl, lb, tri)
            stf = st_ref[cc].astype(F32)
            st_end = (st_ref[cc + 1] if cc + 1 < cpb else stn_ref[0]).astype(F32)
            csum = jnp.sum(st_end * dst, axis=0, keepdims=True)
            doc = do_scr[sl, :]
            dob, dstb = doc.astype(BF16), dst.astype(BF16)
            eb = jnp.exp(b)
            qe = (qf * eb).astype(BF16)
            kh = (kf * jnp.exp(bl - b)).astype(BF16)
            qtb, ktb = qt.astype(BF16), kt.astype(BF16)
            parts = []
            for h in range(HGRN_W // BLK):
                hs = slice(h * BLK, (h + 1) * BLK)
                da = jnp.where(tril, _dot3(doc[:, hs], v[:, hs], NT), 0.0)
                a = jnp.where(tril, _dot(qtb[:, hs], ktb[:, hs], NT), 0.0).astype(BF16)
                parts.append((
                    _dot3(da, kt[:, hs], NN), _dot3(doc[:, hs], stf[:, hs], NN),
                    _dot3(da, qt[:, hs], TN), _dot3(v[:, hs], dst[:, hs], NN),
                    _dot(a, dob[:, hs], TN) + _dot(kh[:, hs], dstb[:, hs], NT),
                    _dot(dob[:, hs], qe[:, hs], TN)))
            dqt, dqi, dkt, dks, dv, upd = (jnp.concatenate([p[n] for p in parts], axis=1) for n in range(6))
            dqf = dqt * jnp.exp(b - bm) + eb * dqi
            dkf = dkt * jnp.exp(bm - b) + jnp.exp(bl - b) * dks
            gq = qf * dqf - kf * dkf
            dlogf = csum + _cumsum_rows(triu, gq)
            dfv = dlogf / f - dkf
            dq_ref[sl, :] = (dqf * (sq * (1.0 + qp * (1.0 - sq)))).astype(BF16)
            df_ref[sl, :] = (dfv * (1.0 - lb) * sg * (1.0 - sg)).astype(BF16)
            di_ref[sl, :] = dv.astype(BF16)
            dst = dst * jnp.exp(bl) + upd
            dlb = dlb + _colsum(dfv * (1.0 - sg))
        dst_scr[...] = dst
        dlb_scr[...] = dlb

        @pl.when(step == nb - 1)
        def _():
            t = dlb * lb * (1.0 - lb)
            dlbl_ref[...] = jnp.concatenate([t, -t], axis=0)

    rev = lambda i: nb - 1 - i
    row = pl.BlockSpec((tb, HGRN_W), lambda i: (rev(i), 0))
    res = pl.pallas_call(
        body, name="hgrn_bwd", grid=(nb,),
        in_specs=_hgrn_specs(tb, rev) + [
            row, pl.BlockSpec((cpb, BLK, HGRN_W), lambda i: (rev(i), 0, 0)),
            pl.BlockSpec((1, BLK, HGRN_W), lambda i: (jnp.minimum((rev(i) + 1) * cpb, nc - 1), 0, 0)),
            pl.BlockSpec((tb, HGRN_W), lambda i: (rev(i), ATTN_W // HGRN_W)),
            pl.BlockSpec((2, HGRN_W), lambda i: (0, 0)), pl.BlockSpec((1, HGRN_W), lambda i: (0, 0))],
        out_specs=[row, row, row, row, pl.BlockSpec((1, HGRN_W), lambda i: (0, 0)),
                   pl.BlockSpec((2, HGRN_W), lambda i: (0, 0))],
        out_shape=[_sds((s, HGRN_W), BF16)] * 4 + [_sds((1, HGRN_W), F32), _sds((2, HGRN_W), F32)],
        scratch_shapes=[pltpu.VMEM((tb, HGRN_W), F32), pltpu.VMEM((BLK, HGRN_W), F32), pltpu.VMEM((1, HGRN_W), F32)],
        compiler_params=_cp(("arbitrary",)),
    )(proj, proj, proj, proj, o_pre, states, states, dcat, lb_logits, out_gain)
    return res


def _place():
    return lax.axis_index("x"), lax.axis_index("y"), lax.axis_index("c")


def _flip(x, y, ox, oy):
    return (1 - x if ox else x), (1 - y if oy else y)


def _half(rows, cc):
    return pl.ds(cc * (rows // 2), rows // 2)


def _remote(src, dst, ssem, rsem, to):
    return pltpu.make_async_remote_copy(src_ref=src, dst_ref=dst, send_sem=ssem, recv_sem=rsem,
                                        device_id=to, device_id_type=MESH)


def _ag_chip_copies(ins, outs, ssem, rsem):
    x, y, c = _place()
    j = 2 * x + y
    cps = []
    for k in range(len(ins)):
        rows = ins[k].shape[0]
        for idx, (ox, oy) in enumerate(FLIPS):
            px, py = _flip(x, y, ox, oy)
            cps.append(_remote(ins[k].at[_half(rows, c)], outs[k].at[j, _half(rows, c)],
                               ssem.at[k, idx], rsem.at[k, idx], (px, py, c)))
    return cps


def _ag_start(ins, outs, ssem, rsem):
    for cp in _ag_chip_copies(ins, outs, ssem, rsem):
        cp.start()


def _ag_finish(ins, outs, ssem, rsem):
    x, y, c = _place()
    sib = (x, y, 1 - c)
    passed = []
    for k in range(len(ins)):
        rows = ins[k].shape[0]
        for idx, (ox, oy) in enumerate(FLIPS):
            px, py = _flip(x, y, ox, oy)
            blk = outs[k].at[2 * px + py, _half(rows, c)]
            _remote(blk, blk, ssem.at[k, idx], rsem.at[k, idx], (px, py, c)).wait_recv()
            cp = _remote(blk, blk, ssem.at[k, 3 + idx], rsem.at[k, 3 + idx], sib)
            cp.start()
            passed.append(cp)
    for k in range(len(ins)):
        rows = ins[k].shape[0]
        for idx, (ox, oy) in enumerate(FLIPS):
            px, py = _flip(x, y, ox, oy)
            blk = outs[k].at[2 * px + py, _half(rows, 1 - c)]
            _remote(blk, blk, ssem.at[k, 3 + idx], rsem.at[k, 3 + idx], sib).wait_recv()
    for cp in _ag_chip_copies(ins, outs, ssem, rsem) + passed:
        cp.wait_send()


def _ag_shapes(shards):
    nk = len(shards)
    return ([_sds((N_CHIPS,) + tuple(w.shape), w.dtype) for w in shards],
            [pltpu.SemaphoreType.DMA((nk, 6)), pltpu.SemaphoreType.DMA((nk, 6))])


def _with_own(gathered, shard, j):
    return lax.dynamic_update_index_in_dim(gathered, shard, j, 0)


def _ag_weights(name, shards):
    nk = len(shards)

    def body(*refs):
        ins, outs = refs[:nk], refs[nk:2 * nk]
        ssem, rsem = refs[2 * nk:]
        _ag_start(ins, outs, ssem, rsem)
        _ag_finish(ins, outs, ssem, rsem)

    out_shape, sems = _ag_shapes(shards)
    return pl.pallas_call(body, name=name, in_specs=[ANY] * nk, out_specs=[ANY] * nk, out_shape=out_shape,
                          scratch_shapes=sems)(*shards)


def _rs_pair(name, grads):
    nk = len(grads)

    def body(*refs):
        ins, outs = refs[:nk], refs[nk:2 * nk]
        ssem, rsem = refs[2 * nk:]
        x, y, c = _place()
        cps = []
        for k in range(nk):
            rows = grads[k].shape[1]
            cp = pltpu.make_async_remote_copy(src_ref=ins[k].at[:, _half(rows, 1 - c)], dst_ref=outs[k],
                                              send_sem=ssem.at[k], recv_sem=rsem.at[k],
                                              device_id=(x, y, 1 - c), device_id_type=MESH)
            cp.start()
            cps.append(cp)
        for cp in cps:
            cp.wait()

    return pl.pallas_call(
        body, name=name, in_specs=[ANY] * nk, out_specs=[ANY] * nk,
        out_shape=[_sds((N_CHIPS, g.shape[1] // 2, g.shape[2]), g.dtype) for g in grads],
        scratch_shapes=[pltpu.SemaphoreType.DMA((nk,)), pltpu.SemaphoreType.DMA((nk,))],
    )(*grads)


def _rs_chip_copies(ins, outs, ssem, rsem):
    x, y, c = _place()
    cps = []
    for k in range(len(ins)):
        for idx, (ox, oy) in enumerate(FLIPS):
            px, py = _flip(x, y, ox, oy)
            cps.append(_remote(ins[k].at[2 * px + py], outs[k].at[idx], ssem.at[k, idx], rsem.at[k, idx], (px, py, c)))
    return cps


def _rs_chips_shapes(psums):
    nk = len(psums)
    return ([_sds((3,) + tuple(p.shape[1:]), p.dtype) for p in psums],
            [pltpu.SemaphoreType.DMA((nk, 3)), pltpu.SemaphoreType.DMA((nk, 3))])


def _rs_chips(name, psums):
    nk = len(psums)

    def body(*refs):
        ins, outs = refs[:nk], refs[nk:2 * nk]
        ssem, rsem = refs[2 * nk:]
        for cp in _rs_chip_copies(ins, outs, ssem, rsem):
            cp.start()
        for cp in _rs_chip_copies(ins, outs, ssem, rsem):
            cp.wait()

    out_shape, sems = _rs_chips_shapes(psums)
    return pl.pallas_call(body, name=name, in_specs=[ANY] * nk, out_specs=[ANY] * nk, out_shape=out_shape,
                          scratch_shapes=sems)(*psums)


def _rs_share(fulls):
    nk = len(fulls)

    def body(*refs):
        ins, outs = refs[:nk], refs[nk:2 * nk]
        ssem, rsem = refs[2 * nk:]
        x, y, c = _place()
        cps = []
        for k in range(nk):
            rows = fulls[k].shape[0]
            cp = _remote(ins[k].at[_half(rows, c)], outs[k].at[_half(rows, c)], ssem.at[k], rsem.at[k], (x, y, 1 - c))
            cp.start()
            cps.append(cp)
        for k, cp in enumerate(cps):
            rows = fulls[k].shape[0]
            cp.wait_send()
            theirs = outs[k].at[_half(rows, 1 - c)]
            _remote(theirs, theirs, ssem.at[k], rsem.at[k], (x, y, 1 - c)).wait_recv()

    return pl.pallas_call(
        body, name="rs_share", in_specs=[ANY] * nk, out_specs=[ANY] * nk,
        out_shape=[_sds(f.shape, f.dtype) for f in fulls], input_output_aliases={k: k for k in range(nk)},
        scratch_shapes=[pltpu.SemaphoreType.DMA((nk,)), pltpu.SemaphoreType.DMA((nk,))],
    )(*fulls)


def _allreduce_small(v):
    ndev = 8

    def body(in_ref, out_ref, buf, ssem, rsem):
        x, y, c = _place()
        me = 4 * x + 2 * y + c
        buf[me] = in_ref[...]
        cps = []
        for k in range(1, ndev):
            ox, oy, oc = (k >> 2) & 1, (k >> 1) & 1, k & 1
            px, py = _flip(x, y, ox, oy)
            pc = 1 - c if oc else c
            cp = pltpu.make_async_remote_copy(src_ref=in_ref, dst_ref=buf.at[me], send_sem=ssem.at[k - 1],
                                              recv_sem=rsem.at[k - 1], device_id=(px, py, pc), device_id_type=MESH)
            cp.start()
            cps.append((cp, 4 * px + 2 * py + pc, (px, py, pc)))
        for k, (cp, src, peer) in enumerate(cps):
            cp.wait_send()
            pltpu.make_async_remote_copy(src_ref=in_ref, dst_ref=buf.at[src], send_sem=ssem.at[k],
                                         recv_sem=rsem.at[k], device_id=peer, device_id_type=MESH).wait_recv()
        acc = buf[0]
        for i in range(1, ndev):
            acc = acc + buf[i]
        out_ref[...] = acc

    return pl.pallas_call(
        body, name="allreduce_small",
        in_specs=[pl.BlockSpec(memory_space=pltpu.VMEM)], out_specs=pl.BlockSpec(memory_space=pltpu.VMEM),
        out_shape=_sds(v.shape, v.dtype),
        scratch_shapes=[pltpu.VMEM((ndev,) + v.shape, v.dtype), pltpu.SemaphoreType.DMA((ndev - 1,)),
                        pltpu.SemaphoreType.DMA((ndev - 1,))],
    )(v)


def _rs_sum1(name, g, recv, c_idx):
    _, r, cdim = g.shape
    hr = r // 2
    tr = min(hr, 256)
    nr = hr // tr

    def body(c_ref, g_ref, r_ref, o32_ref, o16_ref):
        v = g_ref[...] + r_ref[...].astype(F32)
        o32_ref[...] = v
        o16_ref[...] = v.astype(BF16)

    spec = pl.BlockSpec((None, tr, cdim), lambda j, i, c_ref: (j, i, 0))
    return pl.pallas_call(
        body, name=name,
        grid_spec=pltpu.PrefetchScalarGridSpec(
            num_scalar_prefetch=1, grid=(N_CHIPS, nr),
            in_specs=[pl.BlockSpec((None, tr, cdim), lambda j, i, c_ref: (j, c_ref[0] * nr + i, 0)), spec],
            out_specs=[spec, spec]),
        out_shape=[_sds((N_CHIPS, hr, cdim), F32), _sds((N_CHIPS, hr, cdim), BF16)],
        compiler_params=_cp(("parallel", "parallel")),
    )(c_idx, g, recv)


def _rs_sum2(name, p32, recv, jc_idx):
    _, hr, cdim = p32.shape
    tr = min(hr, 256)
    nr = hr // tr

    def body(jc_ref, p_ref, r_ref, o_ref):
        o_ref[...] = ((p_ref[...] + r_ref[0].astype(F32)) + r_ref[1].astype(F32)) + r_ref[2].astype(F32)

    return pl.pallas_call(
        body, name=name,
        grid_spec=pltpu.PrefetchScalarGridSpec(
            num_scalar_prefetch=1, grid=(nr,),
            in_specs=[pl.BlockSpec((None, tr, cdim), lambda i, jc: (jc[0], i, 0)),
                      pl.BlockSpec((3, tr, cdim), lambda i, jc: (0, i, 0))],
            out_specs=pl.BlockSpec((tr, cdim), lambda i, jc: (jc[1] * nr + i, 0))),
        out_shape=_sds((2 * hr, cdim), F32),
        compiler_params=_cp(("parallel",)),
    )(jc_idx, p32, recv)


def _adamw(name, w, g, m, v):
    r, cdim = w.shape
    tr = min(r, 256)
    c1 = 1.0 - ADAM_B1 ** ADAM_STEP
    c2 = 1.0 - ADAM_B2 ** ADAM_STEP

    def body(w_ref, g_ref, m_ref, v_ref, d_ref, nm_ref, nv_ref):
        gv = g_ref[...]
        nm = ADAM_B1 * m_ref[...] + (1.0 - ADAM_B1) * gv
        nv = ADAM_B2 * v_ref[...] + (1.0 - ADAM_B2) * (gv * gv)
        d_ref[...] = -ADAM_LR * ((nm / c1) / (jnp.sqrt(nv / c2) + ADAM_EPS) + ADAM_WD * w_ref[...])
        nm_ref[...] = nm
        nv_ref[...] = nv

    spec = pl.BlockSpec((tr, cdim), lambda i: (i, 0))
    return pl.pallas_call(
        body, name=name, grid=(r // tr,), in_specs=[spec] * 4, out_specs=[spec] * 3,
        out_shape=[_sds((r, cdim), F32)] * 3, compiler_params=_cp(("parallel",)),
    )(w, g, m, v)


def _pack_small(mix_pre, attn_out, lb_logits, hgrn_out, mix_post, mlp_pre, mlp_post):
    rows = [mix_pre, jnp.concatenate([attn_out, hgrn_out], axis=1),
            jnp.concatenate([lb_logits[0:1], lb_logits[1:2]], axis=1), mix_post, mlp_pre, mlp_post,
            jnp.zeros((2, D_MODEL), F32)]
    return jnp.concatenate(rows, axis=0)


def _unpack_small(p):
    return (p[0:1], p[1:2, :ATTN_W], jnp.concatenate([p[2:3, :HGRN_W], p[2:3, HGRN_W:]], axis=0),
            p[1:2, ATTN_W:], p[3:4], p[4:5], p[5:6])


def kernel(x, mix_pre_norm, w_in, attn_out_norm, hgrn_lb_logits, hgrn_out_norm, w_out, mix_post_norm, mlp_pre_norm, w_ff1, w_ff2, mlp_post_norm, loss_target, m_mix_pre_norm, m_w_in, m_attn_out_norm, m_hgrn_lb_logits, m_hgrn_out_norm, m_w_out, m_mix_post_norm, m_mlp_pre_norm, m_w_ff1, m_w_ff2, m_mlp_post_norm, v_mix_pre_norm, v_w_in, v_attn_out_norm, v_hgrn_lb_logits, v_hgrn_out_norm, v_w_out, v_mix_post_norm, v_mlp_pre_norm, v_w_ff1, v_w_ff2, v_mlp_post_norm):
    s = x.shape[1]
    xs = x.reshape(s, D_MODEL)
    tgt = loss_target.reshape(s, D_MODEL)
    cx, cy, cc = _place()
    chip = 2 * cx + cy
    c_idx = jnp.reshape(cc, (1,)).astype(jnp.int32)
    jc_idx = jnp.stack([chip, cc]).astype(jnp.int32)

    big_w = [w_in[0], w_out[0], w_ff1[0], w_ff2[0]]
    big_m = [m_w_in[0], m_w_out[0], m_w_ff1[0], m_w_ff2[0]]
    big_v = [v_w_in[0], v_w_out[0], v_w_ff1[0], v_w_ff2[0]]
    shards = [w.astype(BF16) for w in big_w]

    (wg_in,) = _ag_weights("ag_in", shards[:1])
    wg_in = _with_own(wg_in, shards[0], chip)

    (h,) = _rows_call("norm_in", lambda xv, g: ((xv * _rstd(xv) * g),),
                      [(xs, _row(D_MODEL)), (mix_pre_norm, "full")], [(D_MODEL, BF16, "row")], s)
    (proj,) = _mm_cols("mm_proj", h, wg_in, NN, [F32])
    hg_o, rec, states = _hgrn_fwd(proj, hgrn_lb_logits, hgrn_out_norm)
    attn_o, attn_lse, wg_out, wg_1, wg_2 = _attn_fwd(proj, shards[1:])
    wg_out, wg_1, wg_2 = (_with_own(g, w, chip) for g, w in zip((wg_out, wg_1, wg_2), shards[1:]))
    (attn_n,) = _rows_call("attn_norm", lambda o, gain: (o * _rstd(o) * gain,),
                           [(attn_o, _row(ATTN_W)), (attn_out_norm, "full")], [(ATTN_W, BF16, "row")], s)
    cat = jnp.concatenate([attn_n, rec], axis=1)
    mixed = _mm_acc("mm_mixed", cat, wg_out, NN)

    def post1(xv, mv, g_post, g_pre2):
        x1 = xv + mv * _rstd(mv) * g_post
        return x1, x1 * _rstd(x1) * g_pre2

    x1, h2 = _rows_call("post1", post1, [(xs, _row(D_MODEL)), (mixed, _row(D_MODEL)), (mix_post_norm, "full"),
                                         (mlp_pre_norm, "full")], [(D_MODEL, F32, "row"), (D_MODEL, BF16, "row")], s)

    def sq_relu(u):
        r = jnp.maximum(u, 0.0)
        return r * r, r

    act, ru = _mm_cols("mm_ff1", h2, wg_1, NN, [BF16, BF16], epi=sq_relu)
    ff = _mm_acc("mm_ff2", act, wg_2, NN)

    def post2(x1v, fv, tv, g):
        y = x1v + fv * _rstd(fv) * g
        dy = (y - tv) * (1.0 / D_MODEL)
        err = y - tv
        loss = 0.5 * jnp.sum(jnp.mean(err * err, axis=-1, keepdims=True), axis=0, keepdims=True)
        dff, dgc = _norm_bwd(fv, g, dy)
        return dy, dff, _colsum(dgc), jnp.broadcast_to(loss, (1, BLK))

    dy, dff, g_mlp_post, loss_part = _rows_call(
        "post2", post2, [(x1, _row(D_MODEL)), (ff, _row(D_MODEL)), (tgt, _row(D_MODEL)), (mlp_post_norm, "full")],
        [(D_MODEL, F32, "row"), (D_MODEL, BF16, "row"), (D_MODEL, F32, "acc"), (BLK, F32, "acc")], s)

    (du,) = _mm_cols("mm_du", dff, wg_2, NT, [BF16], epi=lambda acc, r: (acc * (2.0 * r.astype(F32)),),
                     extras=(ru,))
    gw_2 = _mm_wgrad("mm_gw2", act, dff, True)
    gw_1 = _mm_wgrad("mm_gw1", h2, du, False)
    dh2 = _mm_acc("mm_dh2", du, wg_1, NT)

    def bwd_mid(dyv, dh2v, x1v, mv, g_pre2, g_post):
        d1, gc1 = _norm_bwd(x1v, g_pre2, dh2v)
        dx1 = dyv + d1
        dm, gc2 = _norm_bwd(mv, g_post, dx1)
        return dx1, dm, _colsum(gc1), _colsum(gc2)

    dx1, dmixed, g_mlp_pre, g_mix_post = _rows_call(
        "bwd_mid", bwd_mid, [(dy, _row(D_MODEL)), (dh2, _row(D_MODEL)), (x1, _row(D_MODEL)), (mixed, _row(D_MODEL)),
                             (mlp_pre_norm, "full"), (mix_post_norm, "full")],
        [(D_MODEL, F32, "row"), (D_MODEL, BF16, "row"), (D_MODEL, F32, "acc"), (D_MODEL, F32, "acc")], s)

    (dcat,) = _mm_cols("mm_dcat", dmixed, wg_out, NT, [F32])
    gw_out = _mm_wgrad("mm_gwout", cat, dmixed, True)

    names = ["out", "ff1", "ff2", "in"]
    ready = [gw_out, gw_1, gw_2]
    from_pair = _rs_pair("rs_pair_3", [g.astype(BF16) for g in ready])
    pair = [_rs_sum1(f"rs_sum1_{n}", g, r, c_idx) for n, g, r in zip(names, ready, from_pair)]

    def attn_norm_bwd(dc, o, gain):
        do, gc = _norm_bwd(o, gain, dc)
        t = do * o
        lane = lax.broadcasted_iota(jnp.int32, (t.shape[0], BLK), 1) < 64
        parts = []
        for p in range(ATTN_W // BLK):
            tp = t[:, p * BLK:(p + 1) * BLK]
            sa = jnp.sum(jnp.where(lane, tp, 0.0), axis=1, keepdims=True)
            sb = jnp.sum(jnp.where(lane, 0.0, tp), axis=1, keepdims=True)
            parts.append(jnp.where(lane, sa, sb))
        return do, jnp.concatenate(parts, axis=1), _colsum(gc)

    do_attn, delta, g_attn_out = _rows_call(
        "attn_norm_bwd", attn_norm_bwd, [(dcat, _row(ATTN_W, 0)), (attn_o, _row(ATTN_W)), (attn_out_norm, "full")],
        [(ATTN_W, F32, "row"), (ATTN_W, F32, "row"), (ATTN_W, F32, "acc")], s)
    dq, dk, dv, *from_chips = _attn_bwd(proj, do_attn, attn_lse, delta, [p[1] for p in pair])
    dhq, dhf, dhi, dhg, g_hgrn_out, g_lb = _hgrn_bwd(proj, hg_o, states, dcat, hgrn_lb_logits, hgrn_out_norm)

    def dproj_asm(*a):
        return (jnp.concatenate([t.astype(BF16) for t in a], axis=1),)

    (dproj,) = _rows_call("dproj_asm", dproj_asm,
                          [(t, _row(ATTN_W)) for t in (dq, dk, dv)] + [(t, _row(HGRN_W)) for t in (dhq, dhf, dhi, dhg)],
                          [(PROJ_W, BF16, "row")], s)
    dh = _mm_acc("mm_dh", dproj, wg_in, NT)
    gw_in = _mm_wgrad("mm_gwin", h, dproj, False)

    def bwd_in(dx1v, dhv, xv, g):
        d0, gc = _norm_bwd(xv, g, dhv)
        return dx1v + d0, _colsum(gc)

    grad_x, g_mix_pre = _rows_call("bwd_in", bwd_in, [(dx1, _row(D_MODEL)), (dh, _row(D_MODEL)), (xs, _row(D_MODEL)),
                                                      (mix_pre_norm, "full")],
                                   [(D_MODEL, F32, "row"), (D_MODEL, F32, "acc")], s)

    loss = lax.psum(loss_part[0, 0], ("x", "y", "c"))
    small_g = _allreduce_small(_pack_small(g_mix_pre, g_attn_out, g_lb, g_hgrn_out, g_mix_post, g_mlp_pre, g_mlp_post))

    (from_pair_in,) = _rs_pair("rs_pair_in", [gw_in.astype(BF16)])
    pair.append(_rs_sum1("rs_sum1_in", gw_in, from_pair_in, c_idx))
    from_chips += _rs_chips("rs_chips_in", [pair[3][1]])
    reduced = [_rs_sum2(f"rs_sum2_{n}", p[0], r, jc_idx) for n, p, r in zip(names, pair, from_chips)]
    g_wout, g_w1, g_w2, g_win = _rs_share(reduced)
    full = [g_win, g_wout, g_w1, g_w2]

    upd = [_adamw(f"adamw_{n}", w, g, m, v) for n, w, g, m, v in zip(("in", "out", "ff1", "ff2"), big_w, full, big_m, big_v)]
    small_w = _pack_small(mix_pre_norm, attn_out_norm, hgrn_lb_logits, hgrn_out_norm, mix_post_norm, mlp_pre_norm,
                          mlp_post_norm)
    small_m = _pack_small(m_mix_pre_norm, m_attn_out_norm, m_hgrn_lb_logits, m_hgrn_out_norm, m_mix_post_norm,
                          m_mlp_pre_norm, m_mlp_post_norm)
    small_v = _pack_small(v_mix_pre_norm, v_attn_out_norm, v_hgrn_lb_logits, v_hgrn_out_norm, v_mix_post_norm,
                          v_mlp_pre_norm, v_mlp_post_norm)
    small_upd = _adamw("adamw_small", small_w, small_g, small_m, small_v)

    def assemble(small, big):
        sm = _unpack_small(small)
        return (sm[0], big[0][None], sm[1], sm[2], sm[3], big[1][None], sm[4], sm[5], big[2][None], big[3][None], sm[6])

    g_out = assemble(small_g, full)
    d_out = assemble(small_upd[0], [u[0] for u in upd])
    m_out = assemble(small_upd[1], [u[1] for u in upd])
    v_out = assemble(small_upd[2], [u[2] for u in upd])
    return (loss, grad_x.reshape(x.shape), *g_out, *d_out, *m_out, *v_out)
```

```python
import numpy as np
import jax
import jax.numpy as jnp
from jax import lax
from jax.experimental import pallas as pl
from jax.experimental.pallas import tpu as pltpu

F32 = jnp.float32
BF16 = jnp.bfloat16
MESH = pl.DeviceIdType.MESH
ANY = pl.BlockSpec(memory_space=pl.ANY)

RMS_EPS = 1e-6
D_MODEL = 1024
ATTN_W = 512
HGRN_W = 512
PROJ_W = 3584
D_FF = 4096
N_CHIPS = 4
BLK = 128
CHUNK = 64
HGRN_TB = 512
ATTN_GROUP = 4
DILATIONS = (1, 4, 16)
ATTN_SCALE = 0.125
ROW_TILE = 512
MM_TILE = 1024
VMEM_LIMIT = 48 * 2 ** 20
FLIPS = ((1, 0), (0, 1), (1, 1))

ADAM_LR, ADAM_B1, ADAM_B2, ADAM_EPS, ADAM_WD, ADAM_STEP = 0.001, 0.9, 0.999, 1e-08, 0.01, 10


def _cp(sem=None):
    return pltpu.CompilerParams(dimension_semantics=sem, vmem_limit_bytes=VMEM_LIMIT)


def _sigmoid(v):
    return 1.0 / (1.0 + jnp.exp(-v))


def _dot(a, b, contract, precision=None):
    return lax.dot_general(a, b, (contract, ((), ())), preferred_element_type=F32, precision=precision)


NN = ((1,), (0,))
NT = ((1,), (1,))
TN = ((0,), (0,))


def _matmul(name, a, b, *, grid, a_spec, b_spec, contract, outs, epi=None, extras=(), extra_specs=(),
            acc_shape=None):
    n_ex, n_out, nj = len(extras), len(outs), grid[-1]

    def body(a_ref, b_ref, *rest):
        ex, out_refs = rest[:n_ex], rest[n_ex:n_ex + n_out]

        def finish(acc):
            res = epi(acc, *[e[...] for e in ex]) if epi else (acc,)
            for o, r in zip(out_refs, res):
                o[...] = r.astype(o.dtype)

        p = _dot(a_ref[...], b_ref[...], contract)
        if acc_shape is None:
            finish(p)
        else:
            acc_ref = rest[-1]
            j = pl.program_id(len(grid) - 1)

            @pl.when(j == 0)
            def _():
                acc_ref[...] = p

            @pl.when(j > 0)
            def _():
                acc_ref[...] += p

            @pl.when(j == nj - 1)
            def _():
                finish(acc_ref[...])

    sem = ("parallel",) * len(grid) if acc_shape is None else ("parallel",) * (len(grid) - 1) + ("arbitrary",)
    res = pl.pallas_call(
        body, name=name, grid=grid,
        in_specs=[a_spec, b_spec, *extra_specs],
        out_specs=[s for _, s in outs],
        out_shape=[s for s, _ in outs],
        scratch_shapes=[] if acc_shape is None else [pltpu.VMEM(acc_shape, F32)],
        compiler_params=_cp(sem),
    )(a, b, *extras)
    return res


def _sds(shape, dtype):
    return jax.ShapeDtypeStruct(shape, dtype)


def _mm_cols(name, a, w, contract, out_dtypes, epi=None, extras=()):
    m, k = a.shape
    jn = w.shape[0]
    nj = w.shape[2] if contract == NN else w.shape[1]
    tm = min(m, MM_TILE)
    outs = [(_sds((m, jn * nj), dt), pl.BlockSpec((tm, nj), lambda j, i: (i, j))) for dt in out_dtypes]
    return _matmul(name, a, w, grid=(jn, m // tm),
                   a_spec=pl.BlockSpec((tm, k), lambda j, i: (i, 0)),
                   b_spec=pl.BlockSpec((None,) + w.shape[1:], lambda j, i: (j, 0, 0)),
                   contract=contract, outs=outs, epi=epi, extras=extras,
                   extra_specs=[pl.BlockSpec((tm, nj), lambda j, i: (i, j)) for _ in extras])


def _mm_acc(name, a, w, contract):
    m = a.shape[0]
    jn = w.shape[0]
    kj = w.shape[1] if contract == NN else w.shape[2]
    n = w.shape[2] if contract == NN else w.shape[1]
    tm = min(m, MM_TILE)
    outs = [(_sds((m, n), F32), pl.BlockSpec((tm, n), lambda i, j: (i, 0)))]
    return _matmul(name, a, w, grid=(m // tm, jn),
                   a_spec=pl.BlockSpec((tm, kj), lambda i, j: (i, j)),
                   b_spec=pl.BlockSpec((None,) + w.shape[1:], lambda i, j: (j, 0, 0)),
                   contract=contract, outs=outs, acc_shape=(tm, n))[0]


def _mm_wgrad(name, a, b, a_by_j):
    s = a.shape[0]
    if a_by_j:
        r, c = a.shape[1] // N_CHIPS, b.shape[1]
    else:
        r, c = a.shape[1], b.shape[1] // N_CHIPS
    tr = min(r, 512)
    nr = r // tr
    if a_by_j:
        a_spec = pl.BlockSpec((s, tr), lambda j, i: (0, j * nr + i))
        b_spec = pl.BlockSpec((s, c), lambda j, i: (0, 0))
    else:
        a_spec = pl.BlockSpec((s, tr), lambda j, i: (0, i))
        b_spec = pl.BlockSpec((s, c), lambda j, i: (0, j))
    outs = [(_sds((N_CHIPS, r, c), F32), pl.BlockSpec((None, tr, c), lambda j, i: (j, i, 0)))]
    return _matmul(name, a, b, grid=(N_CHIPS, nr), a_spec=a_spec, b_spec=b_spec, contract=TN, outs=outs)[0]


def _rows_call(name, fn, ins, outs, s):
    tm = ROW_TILE
    in_specs = []
    for arr, kind in ins:
        if kind == "full":
            in_specs.append(pl.BlockSpec(arr.shape, lambda i: (0, 0)))
        else:
            _, w, cb = kind
            in_specs.append(pl.BlockSpec((tm, w), lambda i, cb=cb: (i, cb)))
    out_specs, out_shape, is_acc = [], [], []
    for w, dt, kind in outs:
        if kind == "acc":
            out_specs.append(pl.BlockSpec((1, w), lambda i: (0, 0)))
            out_shape.append(_sds((1, w), dt))
        else:
            out_specs.append(pl.BlockSpec((tm, w), lambda i: (i, 0)))
            out_shape.append(_sds((s, w), dt))
        is_acc.append(kind == "acc")
    n_in = len(ins)

    def body(*refs):
        i = pl.program_id(0)
        res = fn(*[r[...] for r in refs[:n_in]])
        for o, r, acc in zip(refs[n_in:], res, is_acc):
            if acc:
                @pl.when(i == 0)
                def _(o=o):
                    o[...] = jnp.zeros_like(o)
                o[...] += r.astype(o.dtype)
            else:
                o[...] = r.astype(o.dtype)

    sem = ("arbitrary",) if any(is_acc) else ("parallel",)
    return pl.pallas_call(body, name=name, grid=(s // tm,), in_specs=in_specs, out_specs=out_specs,
                          out_shape=out_shape, compiler_params=_cp(sem))(*[a for a, _ in ins])


def _rstd(v):
    return lax.rsqrt(jnp.mean(v * v, axis=-1, keepdims=True) + RMS_EPS)


def _norm_bwd(v, gain, dy):
    r = _rstd(v)
    n = v * r
    dn = dy * gain
    dv = r * (dn - n * jnp.mean(dn * n, axis=-1, keepdims=True))
    return dv, dy * n


def _colsum(v):
    return jnp.sum(v, axis=0, keepdims=True)


def _row(w, cb=0):
    return ("row", w, cb)


N_PAIRS = ATTN_W // BLK


def _head_col(v, mask):
    return jnp.max(jnp.where(mask, v, -jnp.inf), axis=1, keepdims=True)


def _slopes():
    t = np.zeros((N_PAIRS, 8, 2 * BLK), np.float32)
    for p in range(N_PAIRS):
        for hh in range(2):
            t[p, hh, :] = 2.0 ** -(2 * p + hh + 1)
    return jnp.asarray(t)


def _rows(n, r, d):
    base = pl.multiple_of(n * (BLK * d), BLK)
    return pl.ds(base + r, BLK, stride=d) if d > 1 else pl.ds(base, BLK)


def _attn_groups(s, d):
    nb = s // (BLK * d)
    g = ATTN_GROUP
    if d >= g:
        return [(nb, lambda n, r0=r0: [(n, r0 + u) for u in range(g)]) for r0 in range(0, d, g)]
    per = g // d
    return [(nb // per, lambda t: [(per * t + u, r) for u in range(per) for r in range(d)])]


def _attn_fwd(proj, shards):
    s = proj.shape[0]
    nk = len(shards)

    def body(sl_ref, q_ref, k_ref, v_ref, *rest):
        w_refs, (o_ref, l_ref) = rest[:nk], rest[nk:nk + 2]
        wg_refs, (ssem, rsem) = rest[nk + 2:2 * nk + 2], rest[2 * nk + 2:]
        pair = pl.program_id(0)

        @pl.when(pair == 0)
        def _():
            _ag_start(w_refs, wg_refs, ssem, rsem)

        row = lax.broadcasted_iota(jnp.int32, (BLK, 2 * BLK), 0)
        col = lax.broadcasted_iota(jnp.int32, (BLK, 2 * BLK), 1)
        dist = row + BLK - col
        in_window = (dist >= 0) & (dist <= BLK)
        distf = dist.astype(F32)
        lane_q = lax.broadcasted_iota(jnp.int32, (BLK, BLK), 1) < 64
        lane_k = lax.broadcasted_iota(jnp.int32, (2 * BLK, BLK), 1) < 64

        def branch(n, r, d):
            rows = _rows(n, r, d)
            prev = _rows(jnp.maximum(n - 1, 0), r, d)
            valid = in_window & (col + n * BLK >= BLK)
            q2 = q_ref[rows, :]
            kk = jnp.concatenate([k_ref[prev, :], k_ref[rows, :]], axis=0).astype(BF16)
            vv = jnp.concatenate([v_ref[prev, :], v_ref[rows, :]], axis=0)
            o2 = jnp.zeros((BLK, BLK), F32)
            lse2 = jnp.zeros((BLK, BLK), F32)
            for hh in range(2):
                mq = lane_q if hh == 0 else ~lane_q
                mk = lane_k if hh == 0 else ~lane_k
                qm = jnp.where(mq, q2, 0.0).astype(BF16)
                sc = _dot(qm, kk, NT) * ATTN_SCALE - (sl_ref[hh:hh + 1, :] * float(d)) * distf
                sc = jnp.where(valid, sc, -1e30)
                m = jnp.max(sc, axis=1, keepdims=True)
                pr = jnp.exp(sc - m)
                den = jnp.sum(pr, axis=1, keepdims=True)
                vm = jnp.where(mk, vv, 0.0).astype(BF16)
                o2 = o2 + _dot(pr.astype(BF16), vm, NN) / den
                lse2 = jnp.where(mq, m + jnp.log(den), lse2)
            return rows, o2, lse2

        def merge(rows, o2, lse2, first):
            if first:
                o_ref[rows, :] = o2
                l_ref[rows, :] = lse2
            else:
                lo = l_ref[rows, :]
                mx = jnp.maximum(lo, lse2)
                ln = mx + jnp.log(jnp.exp(lo - mx) + jnp.exp(lse2 - mx))
                o_ref[rows, :] = jnp.exp(lo - ln) * o_ref[rows, :] + jnp.exp(lse2 - ln) * o2
                l_ref[rows, :] = ln

        for di, d in enumerate(DILATIONS):
            for trips, blocks in _attn_groups(s, d):
                def trip(t, carry, d=d, blocks=blocks, first=(di == 0)):
                    done = [branch(n, r, d) for n, r in blocks(t)]
                    for rows, o2, lse2 in done:
                        merge(rows, o2, lse2, first)
                    return carry

                lax.fori_loop(0, trips, trip, 0)

        @pl.when(pair == N_PAIRS - 1)
        def _():
            _ag_finish(w_refs, wg_refs, ssem, rsem)

    cb = lambda base: pl.BlockSpec((s, BLK), lambda p, base=base: (0, base + p))
    out = pl.BlockSpec((s, BLK), lambda p: (0, p))
    ag_shape, ag_sems = _ag_shapes(shards)
    return pl.pallas_call(
        body, name="attn_fwd", grid=(N_PAIRS,),
        in_specs=[pl.BlockSpec((None, 8, 2 * BLK), lambda p: (p, 0, 0)), cb(0), cb(N_PAIRS), cb(2 * N_PAIRS)]
        + [ANY] * nk,
        out_specs=[out, out] + [ANY] * nk, out_shape=[_sds((s, ATTN_W), F32)] * 2 + ag_shape,
        scratch_shapes=ag_sems, compiler_params=_cp(("arbitrary",)),
    )(_slopes(), proj, proj, proj, *shards)


def _attn_bwd(proj, do, lse, delta, psums):
    s = proj.shape[0]
    nk = len(psums)

    def body(sl_ref, q_ref, k_ref, v_ref, do_ref, l_ref, e_ref, *rest):
        p_refs, (dq_ref, dk_ref, dv_ref) = rest[:nk], rest[nk:nk + 3]
        got_refs, (ssem, rsem) = rest[nk + 3:2 * nk + 3], rest[2 * nk + 3:]
        pair = pl.program_id(0)

        @pl.when(pair == 0)
        def _():
            for cp in _rs_chip_copies(p_refs, got_refs, ssem, rsem):
                cp.start()

        row1 = lax.broadcasted_iota(jnp.int32, (BLK, 2 * BLK), 0)
        col1 = lax.broadcasted_iota(jnp.int32, (BLK, 2 * BLK), 1)
        dist1 = row1 + BLK - col1
        win1 = (dist1 >= 0) & (dist1 <= BLK)
        dist1f = dist1.astype(F32)
        row2 = lax.broadcasted_iota(jnp.int32, (2 * BLK, BLK), 0)
        col2 = lax.broadcasted_iota(jnp.int32, (2 * BLK, BLK), 1)
        dist2 = row2 - col2
        win2 = (dist2 >= 0) & (dist2 <= BLK)
        dist2f = dist2.astype(F32)
        lane1 = lax.broadcasted_iota(jnp.int32, (BLK, BLK), 1) < 64
        lane2 = lax.broadcasted_iota(jnp.int32, (2 * BLK, BLK), 1) < 64

        def branch(n, r, d):
            nb = s // (BLK * d)
            rows = _rows(n, r, d)
            prev = _rows(jnp.maximum(n - 1, 0), r, d)
            nxt = _rows(jnp.minimum(n + 1, nb - 1), r, d)
            valid1 = win1 & (col1 + n * BLK >= BLK)
            valid2 = win2 & (row2 - (nb - 1 - n) * BLK < BLK)
            q1, d1, l1, e1 = q_ref[rows, :], do_ref[rows, :], l_ref[rows, :], e_ref[rows, :]
            k1f, v1f = k_ref[rows, :], v_ref[rows, :]
            kk = jnp.concatenate([k_ref[prev, :], k1f], axis=0)
            vv = jnp.concatenate([v_ref[prev, :], v1f], axis=0).astype(BF16)
            qq = jnp.concatenate([q1, q_ref[nxt, :]], axis=0)
            dd = jnp.concatenate([d1, do_ref[nxt, :]], axis=0)
            ll = jnp.concatenate([l1, l_ref[nxt, :]], axis=0)
            ee = jnp.concatenate([e1, e_ref[nxt, :]], axis=0)
            k1, v1, kkb = k1f.astype(BF16), v1f.astype(BF16), kk.astype(BF16)
            dq2 = jnp.zeros((BLK, BLK), F32)
            dk2 = jnp.zeros((BLK, BLK), F32)
            dv2 = jnp.zeros((BLK, BLK), F32)
            for hh in range(2):
                m1 = lane1 if hh == 0 else ~lane1
                m2 = lane2 if hh == 0 else ~lane2
                slope1 = sl_ref[hh:hh + 1, :] * float(d)
                slope2 = sl_ref[hh:hh + 1, :BLK] * float(d)
                qm = jnp.where(m1, q1, 0.0).astype(BF16)
                dm = jnp.where(m1, d1, 0.0).astype(BF16)
                sc = _dot(qm, kkb, NT) * ATTN_SCALE - slope1 * dist1f
                pr = jnp.where(valid1, jnp.exp(sc - _head_col(l1, m1)), 0.0)
                ds = pr * (_dot(dm, vv, NT) - _head_col(e1, m1))
                km = jnp.where(m2, kk, 0.0).astype(BF16)
                dq2 = dq2 + _dot(ds.astype(BF16), km, NN) * ATTN_SCALE
                qqm = jnp.where(m2, qq, 0.0).astype(BF16)
                ddm = jnp.where(m2, dd, 0.0).astype(BF16)
                sc = _dot(qqm, k1, NT) * ATTN_SCALE - slope2 * dist2f
                pr = jnp.where(valid2, jnp.exp(sc - _head_col(ll, m2)), 0.0)
                ds = pr * (_dot(ddm, v1, NT) - _head_col(ee, m2))
                dk2 = dk2 + _dot(ds.astype(BF16), qqm, TN) * ATTN_SCALE
                dv2 = dv2 + _dot(pr.astype(BF16), ddm, TN)
            return rows, dq2, dk2, dv2

        for di, d in enumerate(DILATIONS):
            for trips, blocks in _attn_groups(s, d):
                def trip(t, carry, d=d, blocks=blocks, first=(di == 0)):
                    done = [branch(n, r, d) for n, r in blocks(t)]
                    for rows, dq2, dk2, dv2 in done:
                        if first:
                            dq_ref[rows, :] = dq2
                            dk_ref[rows, :] = dk2
                            dv_ref[rows, :] = dv2
                        else:
                            dq_ref[rows, :] = dq_ref[rows, :] + dq2
                            dk_ref[rows, :] = dk_ref[rows, :] + dk2
                            dv_ref[rows, :] = dv_ref[rows, :] + dv2
                    return carry

                lax.fori_loop(0, trips, trip, 0)

        @pl.when(pair == N_PAIRS - 1)
        def _():
            for cp in _rs_chip_copies(p_refs, got_refs, ssem, rsem):
                cp.wait()

    cb = lambda base: pl.BlockSpec((s, BLK), lambda p, base=base: (0, base + p))
    out = pl.BlockSpec((s, BLK), lambda p: (0, p))
    rs_shape, rs_sems = _rs_chips_shapes(psums)
    return pl.pallas_call(
        body, name="attn_bwd", grid=(N_PAIRS,),
        in_specs=[pl.BlockSpec((None, 8, 2 * BLK), lambda p: (p, 0, 0)), cb(0), cb(N_PAIRS), cb(2 * N_PAIRS),
                  out, out, out] + [ANY] * nk,
        out_specs=[out] * 3 + [ANY] * nk, out_shape=[_sds((s, ATTN_W), F32)] * 3 + rs_shape,
        scratch_shapes=rs_sems, compiler_params=_cp(("arbitrary",)),
    )(_slopes(), proj, proj, proj, do, lse, delta, *psums)


def _lower_bound(lbl):
    return 1.0 / (1.0 + jnp.exp(lbl[1:2, :] - lbl[0:1, :]))


def _hi(a):
    bits = lax.bitcast_convert_type(a, jnp.uint32) & jnp.uint32(0xFFFF0000)
    return lax.bitcast_convert_type(bits, F32)


def _dot3(a, b, contract):
    ah, bh = _hi(a), _hi(b)
    al, bl = (a - ah).astype(BF16), (b - bh).astype(BF16)
    ah, bh = ah.astype(BF16), bh.astype(BF16)
    return _dot(ah, bh, contract) + (_dot(ah, bl, contract) + _dot(al, bh, contract))


def _cumsum_rows(tri, g):
    g1 = _hi(g)
    r1 = g - g1
    g2 = _hi(r1)
    g3 = r1 - g2
    return _dot(tri, g1.astype(BF16), NN) + (_dot(tri, g2.astype(BF16), NN) + _dot(tri, g3.astype(BF16), NN))


def _heads(fn):
    return jnp.concatenate([fn(slice(h * BLK, (h + 1) * BLK)) for h in range(HGRN_W // BLK)], axis=1)


def _head_mean(t):
    return _heads(lambda hs: jnp.broadcast_to(jnp.mean(t[:, hs], axis=1, keepdims=True), (t.shape[0], BLK)))


def _hgrn_chunk(q_ref, f_ref, i_ref, sl, lb, tri):
    qp = q_ref[sl, :]
    sq = _sigmoid(qp)
    qf = qp * sq
    sg = _sigmoid(f_ref[sl, :])
    f = lb + (1.0 - lb) * sg
    kf = 1.0 - f
    v = i_ref[sl, :]
    b = _cumsum_rows(tri, jnp.log(f))
    bm = b[CHUNK // 2:CHUNK // 2 + 1, :]
    bl = b[CHUNK - 1:CHUNK, :]
    qt = qf * jnp.exp(b - bm)
    kt = kf * jnp.exp(bm - b)
    return qp, sq, qf, sg, f, kf, v, b, bm, bl, qt, kt


def _hgrn_specs(tb, block):
    first = 3 * ATTN_W // HGRN_W
    return [pl.BlockSpec((tb, HGRN_W), lambda i, k=k: (block(i), first + k)) for k in range(4)]


def _hgrn_fwd(proj, lb_logits, out_gain):
    s = proj.shape[0]
    tb = min(HGRN_TB, s)
    nb, cpb, nc = s // tb, tb // CHUNK, s // CHUNK

    def body(q_ref, f_ref, i_ref, g_ref, lbl_ref, gain_ref, o_ref, rec_ref, st_ref, st_scr):
        step = pl.program_id(0)

        @pl.when(step == 0)
        def _():
            st_scr[...] = jnp.zeros_like(st_scr)

        lb = _lower_bound(lbl_ref[...])
        r64 = lax.broadcasted_iota(jnp.int32, (CHUNK, CHUNK), 0)
        c64 = lax.broadcasted_iota(jnp.int32, (CHUNK, CHUNK), 1)
        tril = r64 >= c64
        tri = tril.astype(BF16)
        st = st_scr[...]
        for cc in range(cpb):
            sl = slice(cc * CHUNK, (cc + 1) * CHUNK)
            _, _, qf, _, _, kf, v, b, _, bl, qt, kt = _hgrn_chunk(q_ref, f_ref, i_ref, sl, lb, tri)
            qe = (qf * jnp.exp(b)).astype(BF16)
            kh = (kf * jnp.exp(bl - b)).astype(BF16)
            qtb, ktb, vb, stb = qt.astype(BF16), kt.astype(BF16), v.astype(BF16), st.astype(BF16)

            def out_h(hs):
                a = jnp.where(tril, _dot(qtb[:, hs], ktb[:, hs], NT), 0.0).astype(BF16)
                return _dot(qe[:, hs], stb[:, hs], NT) + _dot(a, vb[:, hs], NN)

            o_ref[sl, :] = _heads(out_h)
            st_ref[cc] = stb
            st = st * jnp.exp(bl) + _heads(lambda hs: _dot(vb[:, hs], kh[:, hs], TN))
        st_scr[...] = st
        o = o_ref[...]
        gate = g_ref[...]
        rec_ref[...] = (o * lax.rsqrt(_head_mean(o * o) + RMS_EPS) * gain_ref[...] * (gate * _sigmoid(gate))).astype(BF16)

    row = pl.BlockSpec((tb, HGRN_W), lambda i: (i, 0))
    return pl.pallas_call(
        body, name="hgrn_fwd", grid=(nb,),
        in_specs=_hgrn_specs(tb, lambda i: i) + [pl.BlockSpec((2, HGRN_W), lambda i: (0, 0)),
                                                 pl.BlockSpec((1, HGRN_W), lambda i: (0, 0))],
        out_specs=[row, row, pl.BlockSpec((cpb, BLK, HGRN_W), lambda i: (i, 0, 0))],
        out_shape=[_sds((s, HGRN_W), F32), _sds((s, HGRN_W), BF16), _sds((nc, BLK, HGRN_W), BF16)],
        scratch_shapes=[pltpu.VMEM((BLK, HGRN_W), F32)],
        compiler_params=_cp(("arbitrary",)),
    )(proj, proj, proj, proj, lb_logits, out_gain)


def _hgrn_bwd(proj, o_pre, states, dcat, lb_logits, out_gain):
    s = proj.shape[0]
    tb = min(HGRN_TB, s)
    nb, cpb, nc = s // tb, tb // CHUNK, s // CHUNK

    def body(q_ref, f_ref, i_ref, g_ref, o_ref, st_ref, stn_ref, dy_ref, lbl_ref, gain_ref,
             dq_ref, df_ref, di_ref, dg_ref, dgain_ref, dlbl_ref, do_scr, dst_scr, dlb_scr):
        step = pl.program_id(0)

        @pl.when(step == 0)
        def _():
            dst_scr[...] = jnp.zeros_like(dst_scr)
            dlb_scr[...] = jnp.zeros_like(dlb_scr)
            dgain_ref[...] = jnp.zeros_like(dgain_ref)

        lb = _lower_bound(lbl_ref[...])
        gain = gain_ref[...]
        o = o_ref[...]
        r = lax.rsqrt(_head_mean(o * o) + RMS_EPS)
        nrm = o * r
        gate = g_ref[...]
        sgt = _sigmoid(gate)
        dy = dy_ref[...]
        dg_ref[...] = (dy * nrm * gain * (sgt * (1.0 + gate * (1.0 - sgt)))).astype(BF16)
        dng = dy * (gate * sgt)
        dgain_ref[...] += _colsum(dng * nrm)
        dn = dng * gain
        do_scr[...] = r * (dn - nrm * _head_mean(dn * nrm))

        r64 = lax.broadcasted_iota(jnp.int32, (CHUNK, CHUNK), 0)
        c64 = lax.broadcasted_iota(jnp.int32, (CHUNK, CHUNK), 1)
        tril = r64 >= c64
        tri = tril.astype(BF16)
        triu = (r64 <= c64).astype(BF16)
        dst = dst_scr[...]
        dlb = dlb_scr[...]
        for cc in reversed(range(cpb)):
            sl = slice(cc * CHUNK, (cc + 1) * CHUNK)
            qp, sq, qf, sg, f, kf, v, b, bm, bl, qt, kt = _hgrn_chunk(q_ref, f_ref, i_ref, sl, lb, tri)
            stf = st_ref[cc].astype(F32)
            st_end = (st_ref[cc + 1] if cc + 1 < cpb else stn_ref[0]).astype(F32)
            csum = jnp.sum(st_end * dst, axis=0, keepdims=True)
            doc = do_scr[sl, :]
            dob, dstb = doc.astype(BF16), dst.astype(BF16)
            eb = jnp.exp(b)
            qe = (qf * eb).astype(BF16)
            kh = (kf * jnp.exp(bl - b)).astype(BF16)
            qtb, ktb = qt.astype(BF16), kt.astype(BF16)
            parts = []
            for h in range(HGRN_W // BLK):
                hs = slice(h * BLK, (h + 1) * BLK)
                da = jnp.where(tril, _dot3(doc[:, hs], v[:, hs], NT), 0.0)
                a = jnp.where(tril, _dot(qtb[:, hs], ktb[:, hs], NT), 0.0).astype(BF16)
                parts.append((
                    _dot3(da, kt[:, hs], NN), _dot3(doc[:, hs], stf[:, hs], NN),
                    _dot3(da, qt[:, hs], TN), _dot3(v[:, hs], dst[:, hs], NN),
                    _dot(a, dob[:, hs], TN) + _dot(kh[:, hs], dstb[:, hs], NT),
                    _dot(dob[:, hs], qe[:, hs], TN)))
            dqt, dqi, dkt, dks, dv, upd = (jnp.concatenate([p[n] for p in parts], axis=1) for n in range(6))
            dqf = dqt * jnp.exp(b - bm) + eb * dqi
            dkf = dkt * jnp.exp(bm - b) + jnp.exp(bl - b) * dks
            gq = qf * dqf - kf * dkf
            dlogf = csum + _cumsum_rows(triu, gq)
            dfv = dlogf / f - dkf
            dq_ref[sl, :] = (dqf * (sq * (1.0 + qp * (1.0 - sq)))).astype(BF16)
            df_ref[sl, :] = (dfv * (1.0 - lb) * sg * (1.0 - sg)).astype(BF16)
            di_ref[sl, :] = dv.astype(BF16)
            dst = dst * jnp.exp(bl) + upd
            dlb = dlb + _colsum(dfv * (1.0 - sg))
        dst_scr[...] = dst
        dlb_scr[...] = dlb

        @pl.when(step == nb - 1)
        def _():
            t = dlb * lb * (1.0 - lb)
            dlbl_ref[...] = jnp.concatenate([t, -t], axis=0)

    rev = lambda i: nb - 1 - i
    row = pl.BlockSpec((tb, HGRN_W), lambda i: (rev(i), 0))
    res = pl.pallas_call(
        body, name="hgrn_bwd", grid=(nb,),
        in_specs=_hgrn_specs(tb, rev) + [
            row, pl.BlockSpec((cpb, BLK, HGRN_W), lambda i: (rev(i), 0, 0)),
            pl.BlockSpec((1, BLK, HGRN_W), lambda i: (jnp.minimum((rev(i) + 1) * cpb, nc - 1), 0, 0)),
            pl.BlockSpec((tb, HGRN_W), lambda i: (rev(i), ATTN_W // HGRN_W)),
            pl.BlockSpec((2, HGRN_W), lambda i: (0, 0)), pl.BlockSpec((1, HGRN_W), lambda i: (0, 0))],
        out_specs=[row, row, row, row, pl.BlockSpec((1, HGRN_W), lambda i: (0, 0)),
                   pl.BlockSpec((2, HGRN_W), lambda i: (0, 0))],
        out_shape=[_sds((s, HGRN_W), BF16)] * 4 + [_sds((1, HGRN_W), F32), _sds((2, HGRN_W), F32)],
        scratch_shapes=[pltpu.VMEM((tb, HGRN_W), F32), pltpu.VMEM((BLK, HGRN_W), F32), pltpu.VMEM((1, HGRN_W), F32)],
        compiler_params=_cp(("arbitrary",)),
    )(proj, proj, proj, proj, o_pre, states, states, dcat, lb_logits, out_gain)
    return res


def _place():
    return lax.axis_index("x"), lax.axis_index("y"), lax.axis_index("c")


def _flip(x, y, ox, oy):
    return (1 - x if ox else x), (1 - y if oy else y)


def _half(rows, cc):
    return pl.ds(cc * (rows // 2), rows // 2)


def _remote(src, dst, ssem, rsem, to):
    return pltpu.make_async_remote_copy(src_ref=src, dst_ref=dst, send_sem=ssem, recv_sem=rsem,
                                        device_id=to, device_id_type=MESH)


def _ag_chip_copies(ins, outs, ssem, rsem):
    x, y, c = _place()
    j = 2 * x + y
    cps = []
    for k in range(len(ins)):
        rows = ins[k].shape[0]
        for idx, (ox, oy) in enumerate(FLIPS):
            px, py = _flip(x, y, ox, oy)
            cps.append(_remote(ins[k].at[_half(rows, c)], outs[k].at[j, _half(rows, c)],
                               ssem.at[k, idx], rsem.at[k, idx], (px, py, c)))
    return cps


def _ag_start(ins, outs, ssem, rsem):
    for cp in _ag_chip_copies(ins, outs, ssem, rsem):
        cp.start()


def _ag_finish(ins, outs, ssem, rsem):
    x, y, c = _place()
    sib = (x, y, 1 - c)
    passed = []
    for k in range(len(ins)):
        rows = ins[k].shape[0]
        for idx, (ox, oy) in enumerate(FLIPS):
            px, py = _flip(x, y, ox, oy)
            blk = outs[k].at[2 * px + py, _half(rows, c)]
            _remote(blk, blk, ssem.at[k, idx], rsem.at[k, idx], (px, py, c)).wait_recv()
            cp = _remote(blk, blk, ssem.at[k, 3 + idx], rsem.at[k, 3 + idx], sib)
            cp.start()
            passed.append(cp)
    for k in range(len(ins)):
        rows = ins[k].shape[0]
        for idx, (ox, oy) in enumerate(FLIPS):
            px, py = _flip(x, y, ox, oy)
            blk = outs[k].at[2 * px + py, _half(rows, 1 - c)]
            _remote(blk, blk, ssem.at[k, 3 + idx], rsem.at[k, 3 + idx], sib).wait_recv()
    for cp in _ag_chip_copies(ins, outs, ssem, rsem) + passed:
        cp.wait_send()


def _ag_shapes(shards):
    nk = len(shards)
    return ([_sds((N_CHIPS,) + tuple(w.shape), w.dtype) for w in shards],
            [pltpu.SemaphoreType.DMA((nk, 6)), pltpu.SemaphoreType.DMA((nk, 6))])


def _with_own(gathered, shard, j):
    return lax.dynamic_update_index_in_dim(gathered, shard, j, 0)


def _ag_weights(name, shards):
    nk = len(shards)

    def body(*refs):
        ins, outs = refs[:nk], refs[nk:2 * nk]
        ssem, rsem = refs[2 * nk:]
        _ag_start(ins, outs, ssem, rsem)
        _ag_finish(ins, outs, ssem, rsem)

    out_shape, sems = _ag_shapes(shards)
    return pl.pallas_call(body, name=name, in_specs=[ANY] * nk, out_specs=[ANY] * nk, out_shape=out_shape,
                          scratch_shapes=sems)(*shards)


def _rs_pair(name, grads):
    nk = len(grads)

    def body(*refs):
        ins, outs = refs[:nk], refs[nk:2 * nk]
        ssem, rsem = refs[2 * nk:]
        x, y, c = _place()
        cps = []
        for k in range(nk):
            rows = grads[k].shape[1]
            cp = pltpu.make_async_remote_copy(src_ref=ins[k].at[:, _half(rows, 1 - c)], dst_ref=outs[k],
                                              send_sem=ssem.at[k], recv_sem=rsem.at[k],
                                              device_id=(x, y, 1 - c), device_id_type=MESH)
            cp.start()
            cps.append(cp)
        for cp in cps:
            cp.wait()

    return pl.pallas_call(
        body, name=name, in_specs=[ANY] * nk, out_specs=[ANY] * nk,
        out_shape=[_sds((N_CHIPS, g.shape[1] // 2, g.shape[2]), g.dtype) for g in grads],
        scratch_shapes=[pltpu.SemaphoreType.DMA((nk,)), pltpu.SemaphoreType.DMA((nk,))],
    )(*grads)


def _rs_chip_copies(ins, outs, ssem, rsem):
    x, y, c = _place()
    cps = []
    for k in range(len(ins)):
        for idx, (ox, oy) in enumerate(FLIPS):
            px, py = _flip(x, y, ox, oy)
            cps.append(_remote(ins[k].at[2 * px + py], outs[k].at[idx], ssem.at[k, idx], rsem.at[k, idx], (px, py, c)))
    return cps


def _rs_chips_shapes(psums):
    nk = len(psums)
    return ([_sds((3,) + tuple(p.shape[1:]), p.dtype) for p in psums],
            [pltpu.SemaphoreType.DMA((nk, 3)), pltpu.SemaphoreType.DMA((nk, 3))])


def _rs_chips(name, psums):
    nk = len(psums)

    def body(*refs):
        ins, outs = refs[:nk], refs[nk:2 * nk]
        ssem, rsem = refs[2 * nk:]
        for cp in _rs_chip_copies(ins, outs, ssem, rsem):
            cp.start()
        for cp in _rs_chip_copies(ins, outs, ssem, rsem):
            cp.wait()

    out_shape, sems = _rs_chips_shapes(psums)
    return pl.pallas_call(body, name=name, in_specs=[ANY] * nk, out_specs=[ANY] * nk, out_shape=out_shape,
                          scratch_shapes=sems)(*psums)


def _rs_share(fulls):
    nk = len(fulls)

    def body(*refs):
        ins, outs = refs[:nk], refs[nk:2 * nk]
        ssem, rsem = refs[2 * nk:]
        x, y, c = _place()
        cps = []
        for k in range(nk):
            rows = fulls[k].shape[0]
            cp = _remote(ins[k].at[_half(rows, c)], outs[k].at[_half(rows, c)], ssem.at[k], rsem.at[k], (x, y, 1 - c))
            cp.start()
            cps.append(cp)
        for k, cp in enumerate(cps):
            rows = fulls[k].shape[0]
            cp.wait_send()
            theirs = outs[k].at[_half(rows, 1 - c)]
            _remote(theirs, theirs, ssem.at[k], rsem.at[k], (x, y, 1 - c)).wait_recv()

    return pl.pallas_call(
        body, name="rs_share", in_specs=[ANY] * nk, out_specs=[ANY] * nk,
        out_shape=[_sds(f.shape, f.dtype) for f in fulls], input_output_aliases={k: k for k in range(nk)},
        scratch_shapes=[pltpu.SemaphoreType.DMA((nk,)), pltpu.SemaphoreType.DMA((nk,))],
    )(*fulls)


def _allreduce_small(v):
    ndev = 8

    def body(in_ref, out_ref, buf, ssem, rsem):
        x, y, c = _place()
        me = 4 * x + 2 * y + c
        buf[me] = in_ref[...]
        cps = []
        for k in range(1, ndev):
            ox, oy, oc = (k >> 2) & 1, (k >> 1) & 1, k & 1
            px, py = _flip(x, y, ox, oy)
            pc = 1 - c if oc else c
            cp = pltpu.make_async_remote_copy(src_ref=in_ref, dst_ref=buf.at[me], send_sem=ssem.at[k - 1],
                                              recv_sem=rsem.at[k - 1], device_id=(px, py, pc), device_id_type=MESH)
            cp.start()
            cps.append((cp, 4 * px + 2 * py + pc, (px, py, pc)))
        for k, (cp, src, peer) in enumerate(cps):
            cp.wait_send()
            pltpu.make_async_remote_copy(src_ref=in_ref, dst_ref=buf.at[src], send_sem=ssem.at[k],
                                         recv_sem=rsem.at[k], device_id=peer, device_id_type=MESH).wait_recv()
        acc = buf[0]
        for i in range(1, ndev):
            acc = acc + buf[i]
        out_ref[...] = acc

    return pl.pallas_call(
        body, name="allreduce_small",
        in_specs=[pl.BlockSpec(memory_space=pltpu.VMEM)], out_specs=pl.BlockSpec(memory_space=pltpu.VMEM),
        out_shape=_sds(v.shape, v.dtype),
        scratch_shapes=[pltpu.VMEM((ndev,) + v.shape, v.dtype), pltpu.SemaphoreType.DMA((ndev - 1,)),
                        pltpu.SemaphoreType.DMA((ndev - 1,))],
    )(v)


def _rs_sum1(name, g, recv, c_idx):
    _, r, cdim = g.shape
    hr = r // 2
    tr = min(hr, 256)
    nr = hr // tr

    def body(c_ref, g_ref, r_ref, o32_ref, o16_ref):
        v = g_ref[...] + r_ref[...].astype(F32)
        o32_ref[...] = v
        o16_ref[...] = v.astype(BF16)

    spec = pl.BlockSpec((None, tr, cdim), lambda j, i, c_ref: (j, i, 0))
    return pl.pallas_call(
        body, name=name,
        grid_spec=pltpu.PrefetchScalarGridSpec(
            num_scalar_prefetch=1, grid=(N_CHIPS, nr),
            in_specs=[pl.BlockSpec((None, tr, cdim), lambda j, i, c_ref: (j, c_ref[0] * nr + i, 0)), spec],
            out_specs=[spec, spec]),
        out_shape=[_sds((N_CHIPS, hr, cdim), F32), _sds((N_CHIPS, hr, cdim), BF16)],
        compiler_params=_cp(("parallel", "parallel")),
    )(c_idx, g, recv)


def _rs_sum2(name, p32, recv, jc_idx):
    _, hr, cdim = p32.shape
    tr = min(hr, 256)
    nr = hr // tr

    def body(jc_ref, p_ref, r_ref, o_ref):
        o_ref[...] = ((p_ref[...] + r_ref[0].astype(F32)) + r_ref[1].astype(F32)) + r_ref[2].astype(F32)

    return pl.pallas_call(
        body, name=name,
        grid_spec=pltpu.PrefetchScalarGridSpec(
            num_scalar_prefetch=1, grid=(nr,),
            in_specs=[pl.BlockSpec((None, tr, cdim), lambda i, jc: (jc[0], i, 0)),
                      pl.BlockSpec((3, tr, cdim), lambda i, jc: (0, i, 0))],
            out_specs=pl.BlockSpec((tr, cdim), lambda i, jc: (jc[1] * nr + i, 0))),
        out_shape=_sds((2 * hr, cdim), F32),
        compiler_params=_cp(("parallel",)),
    )(jc_idx, p32, recv)


def _adamw(name, w, g, m, v):
    r, cdim = w.shape
    tr = min(r, 256)
    c1 = 1.0 - ADAM_B1 ** ADAM_STEP
    c2 = 1.0 - ADAM_B2 ** ADAM_STEP

    def body(w_ref, g_ref, m_ref, v_ref, d_ref, nm_ref, nv_ref):
        gv = g_ref[...]
        nm = ADAM_B1 * m_ref[...] + (1.0 - ADAM_B1) * gv
        nv = ADAM_B2 * v_ref[...] + (1.0 - ADAM_B2) * (gv * gv)
        d_ref[...] = -ADAM_LR * ((nm / c1) / (jnp.sqrt(nv / c2) + ADAM_EPS) + ADAM_WD * w_ref[...])
        nm_ref[...] = nm
        nv_ref[...] = nv

    spec = pl.BlockSpec((tr, cdim), lambda i: (i, 0))
    return pl.pallas_call(
        body, name=name, grid=(r // tr,), in_specs=[spec] * 4, out_specs=[spec] * 3,
        out_shape=[_sds((r, cdim), F32)] * 3, compiler_params=_cp(("parallel",)),
    )(w, g, m, v)


def _pack_small(mix_pre, attn_out, lb_logits, hgrn_out, mix_post, mlp_pre, mlp_post):
    rows = [mix_pre, jnp.concatenate([attn_out, hgrn_out], axis=1),
            jnp.concatenate([lb_logits[0:1], lb_logits[1:2]], axis=1), mix_post, mlp_pre, mlp_post,
            jnp.zeros((2, D_MODEL), F32)]
    return jnp.concatenate(rows, axis=0)


def _unpack_small(p):
    return (p[0:1], p[1:2, :ATTN_W], jnp.concatenate([p[2:3, :HGRN_W], p[2:3, HGRN_W:]], axis=0),
            p[1:2, ATTN_W:], p[3:4], p[4:5], p[5:6])


def kernel(x, mix_pre_norm, w_in, attn_out_norm, hgrn_lb_logits, hgrn_out_norm, w_out, mix_post_norm, mlp_pre_norm, w_ff1, w_ff2, mlp_post_norm, loss_target, m_mix_pre_norm, m_w_in, m_attn_out_norm, m_hgrn_lb_logits, m_hgrn_out_norm, m_w_out, m_mix_post_norm, m_mlp_pre_norm, m_w_ff1, m_w_ff2, m_mlp_post_norm, v_mix_pre_norm, v_w_in, v_attn_out_norm, v_hgrn_lb_logits, v_hgrn_out_norm, v_w_out, v_mix_post_norm, v_mlp_pre_norm, v_w_ff1, v_w_ff2, v_mlp_post_norm):
    s = x.shape[1]
    xs = x.reshape(s, D_MODEL)
    tgt = loss_target.reshape(s, D_MODEL)
    cx, cy, cc = _place()
    chip = 2 * cx + cy
    c_idx = jnp.reshape(cc, (1,)).astype(jnp.int32)
    jc_idx = jnp.stack([chip, cc]).astype(jnp.int32)

    big_w = [w_in[0], w_out[0], w_ff1[0], w_ff2[0]]
    big_m = [m_w_in[0], m_w_out[0], m_w_ff1[0], m_w_ff2[0]]
    big_v = [v_w_in[0], v_w_out[0], v_w_ff1[0], v_w_ff2[0]]
    shards = [w.astype(BF16) for w in big_w]

    (wg_in,) = _ag_weights("ag_in", shards[:1])
    wg_in = _with_own(wg_in, shards[0], chip)

    (h,) = _rows_call("norm_in", lambda xv, g: ((xv * _rstd(xv) * g),),
                      [(xs, _row(D_MODEL)), (mix_pre_norm, "full")], [(D_MODEL, BF16, "row")], s)
    (proj,) = _mm_cols("mm_proj", h, wg_in, NN, [F32])
    hg_o, rec, states = _hgrn_fwd(proj, hgrn_lb_logits, hgrn_out_norm)
    attn_o, attn_lse, wg_out, wg_1, wg_2 = _attn_fwd(proj, shards[1:])
    wg_out, wg_1, wg_2 = (_with_own(g, w, chip) for g, w in zip((wg_out, wg_1, wg_2), shards[1:]))
    (attn_n,) = _rows_call("attn_norm", lambda o, gain: (o * _rstd(o) * gain,),
                           [(attn_o, _row(ATTN_W)), (attn_out_norm, "full")], [(ATTN_W, BF16, "row")], s)
    cat = jnp.concatenate([attn_n, rec], axis=1)
    mixed = _mm_acc("mm_mixed", cat, wg_out, NN)

    def post1(xv, mv, g_post, g_pre2):
        x1 = xv + mv * _rstd(mv) * g_post
        return x1, x1 * _rstd(x1) * g_pre2

    x1, h2 = _rows_call("post1", post1, [(xs, _row(D_MODEL)), (mixed, _row(D_MODEL)), (mix_post_norm, "full"),
                                         (mlp_pre_norm, "full")], [(D_MODEL, F32, "row"), (D_MODEL, BF16, "row")], s)

    def sq_relu(u):
        r = jnp.maximum(u, 0.0)
        return r * r, r

    act, ru = _mm_cols("mm_ff1", h2, wg_1, NN, [BF16, BF16], epi=sq_relu)
    ff = _mm_acc("mm_ff2", act, wg_2, NN)

    def post2(x1v, fv, tv, g):
        y = x1v + fv * _rstd(fv) * g
        dy = (y - tv) * (1.0 / D_MODEL)
        err = y - tv
        loss = 0.5 * jnp.sum(jnp.mean(err * err, axis=-1, keepdims=True), axis=0, keepdims=True)
        dff, dgc = _norm_bwd(fv, g, dy)
        return dy, dff, _colsum(dgc), jnp.broadcast_to(loss, (1, BLK))

    dy, dff, g_mlp_post, loss_part = _rows_call(
        "post2", post2, [(x1, _row(D_MODEL)), (ff, _row(D_MODEL)), (tgt, _row(D_MODEL)), (mlp_post_norm, "full")],
        [(D_MODEL, F32, "row"), (D_MODEL, BF16, "row"), (D_MODEL, F32, "acc"), (BLK, F32, "acc")], s)

    (du,) = _mm_cols("mm_du", dff, wg_2, NT, [BF16], epi=lambda acc, r: (acc * (2.0 * r.astype(F32)),),
                     extras=(ru,))
    gw_2 = _mm_wgrad("mm_gw2", act, dff, True)
    gw_1 = _mm_wgrad("mm_gw1", h2, du, False)
    dh2 = _mm_acc("mm_dh2", du, wg_1, NT)

    def bwd_mid(dyv, dh2v, x1v, mv, g_pre2, g_post):
        d1, gc1 = _norm_bwd(x1v, g_pre2, dh2v)
        dx1 = dyv + d1
        dm, gc2 = _norm_bwd(mv, g_post, dx1)
        return dx1, dm, _colsum(gc1), _colsum(gc2)

    dx1, dmixed, g_mlp_pre, g_mix_post = _rows_call(
        "bwd_mid", bwd_mid, [(dy, _row(D_MODEL)), (dh2, _row(D_MODEL)), (x1, _row(D_MODEL)), (mixed, _row(D_MODEL)),
                             (mlp_pre_norm, "full"), (mix_post_norm, "full")],
        [(D_MODEL, F32, "row"), (D_MODEL, BF16, "row"), (D_MODEL, F32, "acc"), (D_MODEL, F32, "acc")], s)

    (dcat,) = _mm_cols("mm_dcat", dmixed, wg_out, NT, [F32])
    gw_out = _mm_wgrad("mm_gwout", cat, dmixed, True)

    names = ["out", "ff1", "ff2", "in"]
    ready = [gw_out, gw_1, gw_2]
    from_pair = _rs_pair("rs_pair_3", [g.astype(BF16) for g in ready])
    pair = [_rs_sum1(f"rs_sum1_{n}", g, r, c_idx) for n, g, r in zip(names, ready, from_pair)]

    def attn_norm_bwd(dc, o, gain):
        do, gc = _norm_bwd(o, gain, dc)
        t = do * o
        lane = lax.broadcasted_iota(jnp.int32, (t.shape[0], BLK), 1) < 64
        parts = []
        for p in range(ATTN_W // BLK):
            tp = t[:, p * BLK:(p + 1) * BLK]
            sa = jnp.sum(jnp.where(lane, tp, 0.0), axis=1, keepdims=True)
            sb = jnp.sum(jnp.where(lane, 0.0, tp), axis=1, keepdims=True)
            parts.append(jnp.where(lane, sa, sb))
        return do, jnp.concatenate(parts, axis=1), _colsum(gc)

    do_attn, delta, g_attn_out = _rows_call(
        "attn_norm_bwd", attn_norm_bwd, [(dcat, _row(ATTN_W, 0)), (attn_o, _row(ATTN_W)), (attn_out_norm, "full")],
        [(ATTN_W, F32, "row"), (ATTN_W, F32, "row"), (ATTN_W, F32, "acc")], s)
    dq, dk, dv, *from_chips = _attn_bwd(proj, do_attn, attn_lse, delta, [p[1] for p in pair])
    dhq, dhf, dhi, dhg, g_hgrn_out, g_lb = _hgrn_bwd(proj, hg_o, states, dcat, hgrn_lb_logits, hgrn_out_norm)

    def dproj_asm(*a):
        return (jnp.concatenate([t.astype(BF16) for t in a], axis=1),)

    (dproj,) = _rows_call("dproj_asm", dproj_asm,
                          [(t, _row(ATTN_W)) for t in (dq, dk, dv)] + [(t, _row(HGRN_W)) for t in (dhq, dhf, dhi, dhg)],
                          [(PROJ_W, BF16, "row")], s)
    dh = _mm_acc("mm_dh", dproj, wg_in, NT)
    gw_in = _mm_wgrad("mm_gwin", h, dproj, False)

    def bwd_in(dx1v, dhv, xv, g):
        d0, gc = _norm_bwd(xv, g, dhv)
        return dx1v + d0, _colsum(gc)

    grad_x, g_mix_pre = _rows_call("bwd_in", bwd_in, [(dx1, _row(D_MODEL)), (dh, _row(D_MODEL)), (xs, _row(D_MODEL)),
                                                      (mix_pre_norm, "full")],
                                   [(D_MODEL, F32, "row"), (D_MODEL, F32, "acc")], s)

    loss = lax.psum(loss_part[0, 0], ("x", "y", "c"))
    small_g = _allreduce_small(_pack_small(g_mix_pre, g_attn_out, g_lb, g_hgrn_out, g_mix_post, g_mlp_pre, g_mlp_post))

    (from_pair_in,) = _rs_pair("rs_pair_in", [gw_in.astype(BF16)])
    pair.append(_rs_sum1("rs_sum1_in", gw_in, from_pair_in, c_idx))
    from_chips += _rs_chips("rs_chips_in", [pair[3][1]])
    reduced = [_rs_sum2(f"rs_sum2_{n}", p[0], r, jc_idx) for n, p, r in zip(names, pair, from_chips)]
    g_wout, g_w1, g_w2, g_win = _rs_share(reduced)
    full = [g_win, g_wout, g_w1, g_w2]

    upd = [_adamw(f"adamw_{n}", w, g, m, v) for n, w, g, m, v in zip(("in", "out", "ff1", "ff2"), big_w, full, big_m, big_v)]
    small_w = _pack_small(mix_pre_norm, attn_out_norm, hgrn_lb_logits, hgrn_out_norm, mix_post_norm, mlp_pre_norm,
                          mlp_post_norm)
    small_m = _pack_small(m_mix_pre_norm, m_attn_out_norm, m_hgrn_lb_logits, m_hgrn_out_norm, m_mix_post_norm,
                          m_mlp_pre_norm, m_mlp_post_norm)
    small_v = _pack_small(v_mix_pre_norm, v_attn_out_norm, v_hgrn_lb_logits, v_hgrn_out_norm, v_mix_post_norm,
                          v_mlp_pre_norm, v_mlp_post_norm)
    small_upd = _adamw("adamw_small", small_w, small_g, small_m, small_v)

    def assemble(small, big):
        sm = _unpack_small(small)
        return (sm[0], big[0][None], sm[1], sm[2], sm[3], big[1][None], sm[4], sm[5], big[2][None], big[3][None], sm[6])

    g_out = assemble(small_g, full)
    d_out = assemble(small_upd[0], [u[0] for u in upd])
    m_out = assemble(small_upd[1], [u[1] for u in upd])
    v_out = assemble(small_upd[2], [u[2] for u in upd])
    return (loss, grad_x.reshape(x.shape), *g_out, *d_out, *m_out, *v_out)
```

```python
import functools

import numpy as np
import jax
import jax.numpy as jnp
from jax import lax
from jax.experimental import pallas as pl
from jax.experimental.pallas import tpu as pltpu

F32 = jnp.float32
BF16 = jnp.bfloat16
MESH = pl.DeviceIdType.MESH
ANY = pl.BlockSpec(memory_space=pl.ANY)

RMS_EPS = 1e-6
D_MODEL = 1024
ATTN_W = 512
HGRN_W = 512
PROJ_W = 3584
D_FF = 4096
N_CHIPS = 4
BLK = 128
CHUNK = 64
HGRN_TB = 512
ATTN_GROUP = 4
DILATIONS = (1, 4, 16)
ATTN_SCALE = 0.125
ROW_TILE = 512
MM_TILE = 1024
VMEM_LIMIT = 48 * 2 ** 20
FLIPS = ((1, 0), (0, 1), (1, 1))

ADAM_LR, ADAM_B1, ADAM_B2, ADAM_EPS, ADAM_WD, ADAM_STEP = 0.001, 0.9, 0.999, 1e-08, 0.01, 10


def _cp(sem=None):
    return pltpu.CompilerParams(dimension_semantics=sem, vmem_limit_bytes=VMEM_LIMIT)


def _sigmoid(v):
    return 1.0 / (1.0 + jnp.exp(-v))


def _dot(a, b, contract, precision=None):
    return lax.dot_general(a, b, (contract, ((), ())), preferred_element_type=F32, precision=precision)


NN = ((1,), (0,))
NT = ((1,), (1,))
TN = ((0,), (0,))


def _matmul(name, a, b, *, grid, a_spec, b_spec, contract, outs, epi=None, extras=(), extra_specs=(),
            acc_shape=None, exchange=None):
    n_ex, n_out, nj = len(extras), len(outs), grid[-1]
    x_ins, x_shapes, x_sems, x_copies = exchange if exchange else ((), [], [], None)
    n_x = len(x_ins)

    def body(a_ref, b_ref, *rest):
        ex, xi = rest[:n_ex], rest[n_ex:n_ex + n_x]
        out_refs, xo = rest[n_ex + n_x:n_ex + n_x + n_out], rest[n_ex + n_x + n_out:n_ex + 2 * n_x + n_out]
        scratch = rest[n_ex + 2 * n_x + n_out:]
        sems = scratch[1:] if acc_shape is not None else scratch
        ids = [pl.program_id(k) for k in range(len(grid))]

        if exchange:
            @pl.when(functools.reduce(jnp.logical_and, [i == 0 for i in ids]))
            def _():
                for cp in x_copies(xi, xo, *sems):
                    cp.start()

        def finish(acc):
            res = epi(acc, *[e[...] for e in ex]) if epi else (acc,)
            for o, r in zip(out_refs, res):
                o[...] = r.astype(o.dtype)

        p = _dot(a_ref[...], b_ref[...], contract)
        if acc_shape is None:
            finish(p)
        else:
            acc_ref = scratch[0]
            j = ids[-1]

            @pl.when(j == 0)
            def _():
                acc_ref[...] = p

            @pl.when(j > 0)
            def _():
                acc_ref[...] += p

            @pl.when(j == nj - 1)
            def _():
                finish(acc_ref[...])

        if exchange:
            @pl.when(functools.reduce(jnp.logical_and, [i == n - 1 for i, n in zip(ids, grid)]))
            def _():
                for cp in x_copies(xi, xo, *sems):
                    cp.wait()

    if exchange:
        sem = ("arbitrary",) * len(grid)
    else:
        sem = ("parallel",) * len(grid) if acc_shape is None else ("parallel",) * (len(grid) - 1) + ("arbitrary",)
    res = pl.pallas_call(
        body, name=name, grid=grid,
        in_specs=[a_spec, b_spec, *extra_specs] + [ANY] * n_x,
        out_specs=[s for _, s in outs] + [ANY] * n_x,
        out_shape=[s for s, _ in outs] + list(x_shapes),
        scratch_shapes=([] if acc_shape is None else [pltpu.VMEM(acc_shape, F32)]) + list(x_sems),
        compiler_params=_cp(sem),
    )(a, b, *extras, *x_ins)
    return res


def _sds(shape, dtype):
    return jax.ShapeDtypeStruct(shape, dtype)


def _mm_cols(name, a, w, contract, out_dtypes, epi=None, extras=()):
    m, k = a.shape
    jn = w.shape[0]
    nj = w.shape[2] if contract == NN else w.shape[1]
    tm = min(m, MM_TILE)
    outs = [(_sds((m, jn * nj), dt), pl.BlockSpec((tm, nj), lambda j, i: (i, j))) for dt in out_dtypes]
    return _matmul(name, a, w, grid=(jn, m // tm),
                   a_spec=pl.BlockSpec((tm, k), lambda j, i: (i, 0)),
                   b_spec=pl.BlockSpec((None,) + w.shape[1:], lambda j, i: (j, 0, 0)),
                   contract=contract, outs=outs, epi=epi, extras=extras,
                   extra_specs=[pl.BlockSpec((tm, nj), lambda j, i: (i, j)) for _ in extras])


def _mm_acc(name, a, w, contract, exchange=None):
    m = a.shape[0]
    jn = w.shape[0]
    kj = w.shape[1] if contract == NN else w.shape[2]
    n = w.shape[2] if contract == NN else w.shape[1]
    tm = min(m, MM_TILE)
    outs = [(_sds((m, n), F32), pl.BlockSpec((tm, n), lambda i, j: (i, 0)))]
    res = _matmul(name, a, w, grid=(m // tm, jn),
                  a_spec=pl.BlockSpec((tm, kj), lambda i, j: (i, j)),
                  b_spec=pl.BlockSpec((None,) + w.shape[1:], lambda i, j: (j, 0, 0)),
                  contract=contract, outs=outs, acc_shape=(tm, n), exchange=exchange)
    return res if exchange else res[0]


def _mm_wgrad(name, a, b, a_by_j):
    s = a.shape[0]
    if a_by_j:
        r, c = a.shape[1] // N_CHIPS, b.shape[1]
    else:
        r, c = a.shape[1], b.shape[1] // N_CHIPS
    tr = min(r, 512)
    nr = r // tr
    if a_by_j:
        a_spec = pl.BlockSpec((s, tr), lambda j, i: (0, j * nr + i))
        b_spec = pl.BlockSpec((s, c), lambda j, i: (0, 0))
    else:
        a_spec = pl.BlockSpec((s, tr), lambda j, i: (0, i))
        b_spec = pl.BlockSpec((s, c), lambda j, i: (0, j))
    outs = [(_sds((N_CHIPS, r, c), dt), pl.BlockSpec((None, tr, c), lambda j, i: (j, i, 0))) for dt in (F32, BF16)]
    return _matmul(name, a, b, grid=(N_CHIPS, nr), a_spec=a_spec, b_spec=b_spec, contract=TN, outs=outs,
                   epi=lambda acc: (acc, acc))


def _rows_call(name, fn, ins, outs, s):
    tm = ROW_TILE
    in_specs = []
    for arr, kind in ins:
        if kind == "full":
            in_specs.append(pl.BlockSpec(arr.shape, lambda i: (0, 0)))
        else:
            _, w, cb = kind
            in_specs.append(pl.BlockSpec((tm, w), lambda i, cb=cb: (i, cb)))
    out_specs, out_shape, is_acc = [], [], []
    for w, dt, kind in outs:
        if kind == "acc":
            out_specs.append(pl.BlockSpec((1, w), lambda i: (0, 0)))
            out_shape.append(_sds((1, w), dt))
        else:
            out_specs.append(pl.BlockSpec((tm, w), lambda i: (i, 0)))
            out_shape.append(_sds((s, w), dt))
        is_acc.append(kind == "acc")
    n_in = len(ins)

    def body(*refs):
        i = pl.program_id(0)
        res = fn(*[r[...] for r in refs[:n_in]])
        for o, r, acc in zip(refs[n_in:], res, is_acc):
            if acc:
                @pl.when(i == 0)
                def _(o=o):
                    o[...] = jnp.zeros_like(o)
                o[...] += r.astype(o.dtype)
            else:
                o[...] = r.astype(o.dtype)

    sem = ("arbitrary",) if any(is_acc) else ("parallel",)
    return pl.pallas_call(body, name=name, grid=(s // tm,), in_specs=in_specs, out_specs=out_specs,
                          out_shape=out_shape, compiler_params=_cp(sem))(*[a for a, _ in ins])


def _rstd(v):
    return lax.rsqrt(jnp.mean(v * v, axis=-1, keepdims=True) + RMS_EPS)


def _norm_bwd(v, gain, dy):
    r = _rstd(v)
    n = v * r
    dn = dy * gain
    dv = r * (dn - n * jnp.mean(dn * n, axis=-1, keepdims=True))
    return dv, dy * n


def _colsum(v):
    return jnp.sum(v, axis=0, keepdims=True)


def _row(w, cb=0):
    return ("row", w, cb)


N_PAIRS = ATTN_W // BLK


def _head_col(v, mask):
    return jnp.max(jnp.where(mask, v, -jnp.inf), axis=1, keepdims=True)


def _slopes():
    t = np.zeros((N_PAIRS, 8, 2 * BLK), np.float32)
    for p in range(N_PAIRS):
        for hh in range(2):
            t[p, hh, :] = 2.0 ** -(2 * p + hh + 1)
    return jnp.asarray(t)


def _rows(n, r, d):
    base = pl.multiple_of(n * (BLK * d), BLK)
    return pl.ds(base + r, BLK, stride=d) if d > 1 else pl.ds(base, BLK)


def _attn_groups(s, d):
    nb = s // (BLK * d)
    g = ATTN_GROUP
    if d >= g:
        return [(nb, lambda n, r0=r0: [(n, r0 + u) for u in range(g)]) for r0 in range(0, d, g)]
    per = g // d
    return [(nb // per, lambda t: [(per * t + u, r) for u in range(per) for r in range(d)])]


def _attn_fwd(proj, shards):
    s = proj.shape[0]
    nk = len(shards)

    def body(sl_ref, q_ref, k_ref, v_ref, *rest):
        w_refs, (o_ref, l_ref) = rest[:nk], rest[nk:nk + 2]
        wg_refs, (ssem, rsem) = rest[nk + 2:2 * nk + 2], rest[2 * nk + 2:]
        pair = pl.program_id(0)

        @pl.when(pair == 0)
        def _():
            _ag_start(w_refs, wg_refs, ssem, rsem)

        row = lax.broadcasted_iota(jnp.int32, (BLK, 2 * BLK), 0)
        col = lax.broadcasted_iota(jnp.int32, (BLK, 2 * BLK), 1)
        dist = row + BLK - col
        in_window = (dist >= 0) & (dist <= BLK)
        distf = dist.astype(F32)
        lane_q = lax.broadcasted_iota(jnp.int32, (BLK, BLK), 1) < 64
        lane_k = lax.broadcasted_iota(jnp.int32, (2 * BLK, BLK), 1) < 64

        def branch(n, r, d):
            rows = _rows(n, r, d)
            prev = _rows(jnp.maximum(n - 1, 0), r, d)
            valid = in_window & (col + n * BLK >= BLK)
            q2 = q_ref[rows, :]
            kk = jnp.concatenate([k_ref[prev, :], k_ref[rows, :]], axis=0).astype(BF16)
            vv = jnp.concatenate([v_ref[prev, :], v_ref[rows, :]], axis=0)
            o2 = jnp.zeros((BLK, BLK), F32)
            lse2 = jnp.zeros((BLK, BLK), F32)
            for hh in range(2):
                mq = lane_q if hh == 0 else ~lane_q
                mk = lane_k if hh == 0 else ~lane_k
                qm = jnp.where(mq, q2, 0.0).astype(BF16)
                sc = _dot(qm, kk, NT) * ATTN_SCALE - (sl_ref[hh:hh + 1, :] * float(d)) * distf
                sc = jnp.where(valid, sc, -1e30)
                m = jnp.max(sc, axis=1, keepdims=True)
                pr = jnp.exp(sc - m)
                den = jnp.sum(pr, axis=1, keepdims=True)
                vm = jnp.where(mk, vv, 0.0).astype(BF16)
                o2 = o2 + _dot(pr.astype(BF16), vm, NN) / den
                lse2 = jnp.where(mq, m + jnp.log(den), lse2)
            return rows, o2, lse2

        def merge(rows, o2, lse2, first):
            if first:
                o_ref[rows, :] = o2
                l_ref[rows, :] = lse2
            else:
                lo = l_ref[rows, :]
                mx = jnp.maximum(lo, lse2)
                ln = mx + jnp.log(jnp.exp(lo - mx) + jnp.exp(lse2 - mx))
                o_ref[rows, :] = jnp.exp(lo - ln) * o_ref[rows, :] + jnp.exp(lse2 - ln) * o2
                l_ref[rows, :] = ln

        for di, d in enumerate(DILATIONS):
            for trips, blocks in _attn_groups(s, d):
                def trip(t, carry, d=d, blocks=blocks, first=(di == 0)):
                    done = [branch(n, r, d) for n, r in blocks(t)]
                    for rows, o2, lse2 in done:
                        merge(rows, o2, lse2, first)
                    return carry

                lax.fori_loop(0, trips, trip, 0)

        @pl.when(pair == N_PAIRS - 1)
        def _():
            _ag_finish(w_refs, wg_refs, ssem, rsem)

    cb = lambda base: pl.BlockSpec((s, BLK), lambda p, base=base: (0, base + p))
    out = pl.BlockSpec((s, BLK), lambda p: (0, p))
    ag_shape, ag_sems = _ag_shapes(shards)
    return pl.pallas_call(
        body, name="attn_fwd", grid=(N_PAIRS,),
        in_specs=[pl.BlockSpec((None, 8, 2 * BLK), lambda p: (p, 0, 0)), cb(0), cb(N_PAIRS), cb(2 * N_PAIRS)]
        + [ANY] * nk,
        out_specs=[out, out] + [ANY] * nk, out_shape=[_sds((s, ATTN_W), F32)] * 2 + ag_shape,
        scratch_shapes=ag_sems, compiler_params=_cp(("arbitrary",)),
    )(_slopes(), proj, proj, proj, *shards)


def _attn_bwd(proj, do, lse, delta, psums):
    s = proj.shape[0]
    nk = len(psums)

    def body(sl_ref, q_ref, k_ref, v_ref, do_ref, l_ref, e_ref, *rest):
        p_refs, (dq_ref, dk_ref, dv_ref) = rest[:nk], rest[nk:nk + 3]
        got_refs, (ssem, rsem) = rest[nk + 3:2 * nk + 3], rest[2 * nk + 3:]
        pair = pl.program_id(0)

        @pl.when(pair == 0)
        def _():
            for cp in _rs_chip_copies(p_refs, got_refs, ssem, rsem):
                cp.start()

        row = lax.broadcasted_iota(jnp.int32, (BLK, 2 * BLK), 0)
        col = lax.broadcasted_iota(jnp.int32, (BLK, 2 * BLK), 1)
        dist = row + BLK - col
        in_window = (dist >= 0) & (dist <= BLK)
        distf = dist.astype(F32)
        lane_q = lax.broadcasted_iota(jnp.int32, (BLK, BLK), 1) < 64
        lane_k = lax.broadcasted_iota(jnp.int32, (2 * BLK, BLK), 1) < 64
        dk_ref[...] = jnp.zeros_like(dk_ref)
        dv_ref[...] = jnp.zeros_like(dv_ref)

        def branch(n, r, d):
            rows = _rows(n, r, d)
            prev = _rows(jnp.maximum(n - 1, 0), r, d)
            valid = in_window & (col + n * BLK >= BLK)
            q1, d1, l1, e1 = q_ref[rows, :], do_ref[rows, :], l_ref[rows, :], e_ref[rows, :]
            kk = jnp.concatenate([k_ref[prev, :], k_ref[rows, :]], axis=0)
            kkb = kk.astype(BF16)
            vvb = jnp.concatenate([v_ref[prev, :], v_ref[rows, :]], axis=0).astype(BF16)
            dq2 = jnp.zeros((BLK, BLK), F32)
            dkk = jnp.zeros((2 * BLK, BLK), F32)
            dvv = jnp.zeros((2 * BLK, BLK), F32)
            for hh in range(2):
                mq = lane_q if hh == 0 else ~lane_q
                mk = lane_k if hh == 0 else ~lane_k
                qm = jnp.where(mq, q1, 0.0).astype(BF16)
                dm = jnp.where(mq, d1, 0.0).astype(BF16)
                sc = _dot(qm, kkb, NT) * ATTN_SCALE - (sl_ref[hh:hh + 1, :] * float(d)) * distf
                pr = jnp.where(valid, jnp.exp(sc - _head_col(l1, mq)), 0.0)
                ds = (pr * (_dot(dm, vvb, NT) - _head_col(e1, mq))).astype(BF16)
                km = jnp.where(mk, kk, 0.0).astype(BF16)
                dq2 = dq2 + _dot(ds, km, NN) * ATTN_SCALE
                dkk = dkk + _dot(ds, qm, TN) * ATTN_SCALE
                dvv = dvv + _dot(pr.astype(BF16), dm, TN)
            return rows, prev, dq2, dkk, dvv

        for di, d in enumerate(DILATIONS):
            for trips, blocks in _attn_groups(s, d):
                def trip(t, carry, d=d, blocks=blocks, first=(di == 0)):
                    done = [branch(n, r, d) for n, r in blocks(t)]
                    for rows, prev, dq2, dkk, dvv in done:
                        dq_ref[rows, :] = dq2 if first else dq_ref[rows, :] + dq2
                        dk_ref[prev, :] = dk_ref[prev, :] + dkk[:BLK]
                        dk_ref[rows, :] = dk_ref[rows, :] + dkk[BLK:]
                        dv_ref[prev, :] = dv_ref[prev, :] + dvv[:BLK]
                        dv_ref[rows, :] = dv_ref[rows, :] + dvv[BLK:]
                    return carry

                lax.fori_loop(0, trips, trip, 0)

        @pl.when(pair == N_PAIRS - 1)
        def _():
            for cp in _rs_chip_copies(p_refs, got_refs, ssem, rsem):
                cp.wait()

    cb = lambda base: pl.BlockSpec((s, BLK), lambda p, base=base: (0, base + p))
    out = pl.BlockSpec((s, BLK), lambda p: (0, p))
    rs_shape, rs_sems = _rs_chips_shapes(psums)
    return pl.pallas_call(
        body, name="attn_bwd", grid=(N_PAIRS,),
        in_specs=[pl.BlockSpec((None, 8, 2 * BLK), lambda p: (p, 0, 0)), cb(0), cb(N_PAIRS), cb(2 * N_PAIRS),
                  out, out, out] + [ANY] * nk,
        out_specs=[out] * 3 + [ANY] * nk, out_shape=[_sds((s, ATTN_W), F32)] * 3 + rs_shape,
        scratch_shapes=rs_sems, compiler_params=_cp(("arbitrary",)),
    )(_slopes(), proj, proj, proj, do, lse, delta, *psums)


def _lower_bound(lbl):
    return 1.0 / (1.0 + jnp.exp(lbl[1:2, :] - lbl[0:1, :]))


def _hi(a):
    bits = lax.bitcast_convert_type(a, jnp.uint32) & jnp.uint32(0xFFFF0000)
    return lax.bitcast_convert_type(bits, F32)


def _dot3(a, b, contract):
    ah, bh = _hi(a), _hi(b)
    al, bl = (a - ah).astype(BF16), (b - bh).astype(BF16)
    ah, bh = ah.astype(BF16), bh.astype(BF16)
    return _dot(ah, bh, contract) + (_dot(ah, bl, contract) + _dot(al, bh, contract))


def _cumsum_rows(tri, g):
    g1 = _hi(g)
    r1 = g - g1
    g2 = _hi(r1)
    g3 = r1 - g2
    return _dot(tri, g1.astype(BF16), NN) + (_dot(tri, g2.astype(BF16), NN) + _dot(tri, g3.astype(BF16), NN))


def _heads(fn):
    return jnp.concatenate([fn(slice(h * BLK, (h + 1) * BLK)) for h in range(HGRN_W // BLK)], axis=1)


def _head_mean(t):
    return _heads(lambda hs: jnp.broadcast_to(jnp.mean(t[:, hs], axis=1, keepdims=True), (t.shape[0], BLK)))


def _hgrn_chunk(q_ref, f_ref, i_ref, sl, lb, tri):
    qp = q_ref[sl, :]
    sq = _sigmoid(qp)
    qf = qp * sq
    sg = _sigmoid(f_ref[sl, :])
    f = lb + (1.0 - lb) * sg
    kf = 1.0 - f
    v = i_ref[sl, :]
    b = _cumsum_rows(tri, jnp.log(f))
    bm = b[CHUNK // 2:CHUNK // 2 + 1, :]
    bl = b[CHUNK - 1:CHUNK, :]
    qt = qf * jnp.exp(b - bm)
    kt = kf * jnp.exp(bm - b)
    return qp, sq, qf, sg, f, kf, v, b, bm, bl, qt, kt


def _hgrn_specs(tb, block):
    first = 3 * ATTN_W // HGRN_W
    return [pl.BlockSpec((tb, HGRN_W), lambda i, k=k: (block(i), first + k)) for k in range(4)]


def _hgrn_fwd(proj, lb_logits, out_gain):
    s = proj.shape[0]
    tb = min(HGRN_TB, s)
    nb, cpb, nc = s // tb, tb // CHUNK, s // CHUNK

    def body(q_ref, f_ref, i_ref, g_ref, lbl_ref, gain_ref, o_ref, rec_ref, st_ref, st_scr):
        step = pl.program_id(0)

        @pl.when(step == 0)
        def _():
            st_scr[...] = jnp.zeros_like(st_scr)

        lb = _lower_bound(lbl_ref[...])
        r64 = lax.broadcasted_iota(jnp.int32, (CHUNK, CHUNK), 0)
        c64 = lax.broadcasted_iota(jnp.int32, (CHUNK, CHUNK), 1)
        tril = r64 >= c64
        tri = tril.astype(BF16)
        st = st_scr[...]
        for cc in range(cpb):
            sl = slice(cc * CHUNK, (cc + 1) * CHUNK)
            _, _, qf, _, _, kf, v, b, _, bl, qt, kt = _hgrn_chunk(q_ref, f_ref, i_ref, sl, lb, tri)
            qe = (qf * jnp.exp(b)).astype(BF16)
            kh = (kf * jnp.exp(bl - b)).astype(BF16)
            qtb, ktb, vb, stb = qt.astype(BF16), kt.astype(BF16), v.astype(BF16), st.astype(BF16)

            def out_h(hs):
                a = jnp.where(tril, _dot(qtb[:, hs], ktb[:, hs], NT), 0.0).astype(BF16)
                return _dot(qe[:, hs], stb[:, hs], NT) + _dot(a, vb[:, hs], NN)

            o_ref[sl, :] = _heads(out_h)
            st_ref[cc] = stb
            st = st * jnp.exp(bl) + _heads(lambda hs: _dot(vb[:, hs], kh[:, hs], TN))
        st_scr[...] = st
        o = o_ref[...]
        gate = g_ref[...]
        rec_ref[...] = (o * lax.rsqrt(_head_mean(o * o) + RMS_EPS) * gain_ref[...] * (gate * _sigmoid(gate))).astype(BF16)

    row = pl.BlockSpec((tb, HGRN_W), lambda i: (i, 0))
    return pl.pallas_call(
        body, name="hgrn_fwd", grid=(nb,),
        in_specs=_hgrn_specs(tb, lambda i: i) + [pl.BlockSpec((2, HGRN_W), lambda i: (0, 0)),
                                                 pl.BlockSpec((1, HGRN_W), lambda i: (0, 0))],
        out_specs=[row, row, pl.BlockSpec((cpb, BLK, HGRN_W), lambda i: (i, 0, 0))],
        out_shape=[_sds((s, HGRN_W), F32), _sds((s, HGRN_W), BF16), _sds((nc, BLK, HGRN_W), BF16)],
        scratch_shapes=[pltpu.VMEM((BLK, HGRN_W), F32)],
        compiler_params=_cp(("arbitrary",)),
    )(proj, proj, proj, proj, lb_logits, out_gain)


def _hgrn_bwd(proj, o_pre, states, dcat, lb_logits, out_gain):
    s = proj.shape[0]
    tb = min(HGRN_TB, s)
    nb, cpb, nc = s // tb, tb // CHUNK, s // CHUNK

    def body(q_ref, f_ref, i_ref, g_ref, o_ref, st_ref, stn_ref, dy_ref, lbl_ref, gain_ref,
             dq_ref, df_ref, di_ref, dg_ref, dgain_ref, dlbl_ref, do_scr, dst_scr, dlb_scr):
        step = pl.program_id(0)

        @pl.when(step == 0)
        def _():
            dst_scr[...] = jnp.zeros_like(dst_scr)
            dlb_scr[...] = jnp.zeros_like(dlb_scr)
            dgain_ref[...] = jnp.zeros_like(dgain_ref)

        lb = _lower_bound(lbl_ref[...])
        gain = gain_ref[...]
        o = o_ref[...]
        r = lax.rsqrt(_head_mean(o * o) + RMS_EPS)
        nrm = o * r
        gate = g_ref[...]
        sgt = _sigmoid(gate)
        dy = dy_ref[...]
        dg_ref[...] = (dy * nrm * gain * (sgt * (1.0 + gate * (1.0 - sgt)))).astype(BF16)
        dng = dy * (gate * sgt)
        dgain_ref[...] += _colsum(dng * nrm)
        dn = dng * gain
        do_scr[...] = r * (dn - nrm * _head_mean(dn * nrm))

        r64 = lax.broadcasted_iota(jnp.int32, (CHUNK, CHUNK), 0)
        c64 = lax.broadcasted_iota(jnp.int32, (CHUNK, CHUNK), 1)
        tril = r64 >= c64
        tri = tril.astype(BF16)
        triu = (r64 <= c64).astype(BF16)
        dst = dst_scr[...]
        dlb = dlb_scr[...]
        for cc in reversed(range(cpb)):
            sl = slice(cc * CHUNK, (cc + 1) * CHUNK)
            qp, sq, qf, sg, f, kf, v, b, bm, bl, qt, kt = _hgrn_chunk(q_ref, f_ref, i_ref, sl, lb, tri)
            stf = st_ref[cc].astype(F32)
            st_end = (st_ref[cc + 1] if cc + 1 < cpb else stn_ref[0]).astype(F32)
            csum = jnp.sum(st_end * dst, axis=0, keepdims=True)
            doc = do_scr[sl, :]
            dob, dstb = doc.astype(BF16), dst.astype(BF16)
            eb = jnp.exp(b)
            qe = (qf * eb).astype(BF16)
            kh = (kf * jnp.exp(bl - b)).astype(BF16)
            qtb, ktb = qt.astype(BF16), kt.astype(BF16)
            parts = []
            for h in range(HGRN_W // BLK):
                hs = slice(h * BLK, (h + 1) * BLK)
                da = jnp.where(tril, _dot3(doc[:, hs], v[:, hs], NT), 0.0)
                a = jnp.where(tril, _dot(qtb[:, hs], ktb[:, hs], NT), 0.0).astype(BF16)
                parts.append((
                    _dot3(da, kt[:, hs], NN), _dot3(doc[:, hs], stf[:, hs], NN),
                    _dot3(da, qt[:, hs], TN), _dot3(v[:, hs], dst[:, hs], NN),
                    _dot(a, dob[:, hs], TN) + _dot(kh[:, hs], dstb[:, hs], NT),
                    _dot(dob[:, hs], qe[:, hs], TN)))
            dqt, dqi, dkt, dks, dv, upd = (jnp.concatenate([p[n] for p in parts], axis=1) for n in range(6))
            dqf = dqt * jnp.exp(b - bm) + eb * dqi
            dkf = dkt * jnp.exp(bm - b) + jnp.exp(bl - b) * dks
            gq = qf * dqf - kf * dkf
            dlogf = csum + _cumsum_rows(triu, gq)
            dfv = dlogf / f - dkf
            dq_ref[sl, :] = (dqf * (sq * (1.0 + qp * (1.0 - sq)))).astype(BF16)
            df_ref[sl, :] = (dfv * (1.0 - lb) * sg * (1.0 - sg)).astype(BF16)
            di_ref[sl, :] = dv.astype(BF16)
            dst = dst * jnp.exp(bl) + upd
            dlb = dlb + _colsum(dfv * (1.0 - sg))
        dst_scr[...] = dst
        dlb_scr[...] = dlb

        @pl.when(step == nb - 1)
        def _():
            t = dlb * lb * (1.0 - lb)
            dlbl_ref[...] = jnp.concatenate([t, -t], axis=0)

    rev = lambda i: nb - 1 - i
    row = pl.BlockSpec((tb, HGRN_W), lambda i: (rev(i), 0))
    res = pl.pallas_call(
        body, name="hgrn_bwd", grid=(nb,),
        in_specs=_hgrn_specs(tb, rev) + [
            row, pl.BlockSpec((cpb, BLK, HGRN_W), lambda i: (rev(i), 0, 0)),
            pl.BlockSpec((1, BLK, HGRN_W), lambda i: (jnp.minimum((rev(i) + 1) * cpb, nc - 1), 0, 0)),
            pl.BlockSpec((tb, HGRN_W), lambda i: (rev(i), ATTN_W // HGRN_W)),
            pl.BlockSpec((2, HGRN_W), lambda i: (0, 0)), pl.BlockSpec((1, HGRN_W), lambda i: (0, 0))],
        out_specs=[row, row, row, row, pl.BlockSpec((1, HGRN_W), lambda i: (0, 0)),
                   pl.BlockSpec((2, HGRN_W), lambda i: (0, 0))],
        out_shape=[_sds((s, HGRN_W), BF16)] * 4 + [_sds((1, HGRN_W), F32), _sds((2, HGRN_W), F32)],
        scratch_shapes=[pltpu.VMEM((tb, HGRN_W), F32), pltpu.VMEM((BLK, HGRN_W), F32), pltpu.VMEM((1, HGRN_W), F32)],
        compiler_params=_cp(("arbitrary",)),
    )(proj, proj, proj, proj, o_pre, states, states, dcat, lb_logits, out_gain)
    return res


def _place():
    return lax.axis_index("x"), lax.axis_index("y"), lax.axis_index("c")


def _flip(x, y, ox, oy):
    return (1 - x if ox else x), (1 - y if oy else y)


def _half(rows, cc):
    return pl.ds(cc * (rows // 2), rows // 2)


def _remote(src, dst, ssem, rsem, to):
    return pltpu.make_async_remote_copy(src_ref=src, dst_ref=dst, send_sem=ssem, recv_sem=rsem,
                                        device_id=to, device_id_type=MESH)


def _ag_chip_copies(ins, outs, ssem, rsem):
    x, y, c = _place()
    j = 2 * x + y
    cps = []
    for k in range(len(ins)):
        rows = ins[k].shape[0]
        for idx, (ox, oy) in enumerate(FLIPS):
            px, py = _flip(x, y, ox, oy)
            cps.append(_remote(ins[k].at[_half(rows, c)], outs[k].at[j, _half(rows, c)],
                               ssem.at[k, idx], rsem.at[k, idx], (px, py, c)))
    return cps


def _ag_start(ins, outs, ssem, rsem):
    for cp in _ag_chip_copies(ins, outs, ssem, rsem):
        cp.start()


def _ag_finish(ins, outs, ssem, rsem):
    x, y, c = _place()
    sib = (x, y, 1 - c)
    passed = []
    for k in range(len(ins)):
        rows = ins[k].shape[0]
        for idx, (ox, oy) in enumerate(FLIPS):
            px, py = _flip(x, y, ox, oy)
            blk = outs[k].at[2 * px + py, _half(rows, c)]
            _remote(blk, blk, ssem.at[k, idx], rsem.at[k, idx], (px, py, c)).wait_recv()
            cp = _remote(blk, blk, ssem.at[k, 3 + idx], rsem.at[k, 3 + idx], sib)
            cp.start()
            passed.append(cp)
    for k in range(len(ins)):
        rows = ins[k].shape[0]
        for idx, (ox, oy) in enumerate(FLIPS):
            px, py = _flip(x, y, ox, oy)
            blk = outs[k].at[2 * px + py, _half(rows, 1 - c)]
            _remote(blk, blk, ssem.at[k, 3 + idx], rsem.at[k, 3 + idx], sib).wait_recv()
    for cp in _ag_chip_copies(ins, outs, ssem, rsem) + passed:
        cp.wait_send()


def _ag_shapes(shards):
    nk = len(shards)
    return ([_sds((N_CHIPS,) + tuple(w.shape), w.dtype) for w in shards],
            [pltpu.SemaphoreType.DMA((nk, 6)), pltpu.SemaphoreType.DMA((nk, 6))])


def _with_own(gathered, shard, j):
    return lax.dynamic_update_index_in_dim(gathered, shard, j, 0)


def _ag_weights(name, shards):
    nk = len(shards)

    def body(*refs):
        ins, outs = refs[:nk], refs[nk:2 * nk]
        ssem, rsem = refs[2 * nk:]
        _ag_start(ins, outs, ssem, rsem)
        _ag_finish(ins, outs, ssem, rsem)

    out_shape, sems = _ag_shapes(shards)
    return pl.pallas_call(body, name=name, in_specs=[ANY] * nk, out_specs=[ANY] * nk, out_shape=out_shape,
                          scratch_shapes=sems)(*shards)


def _rs_pair(name, grads):
    nk = len(grads)

    def body(*refs):
        ins, outs = refs[:nk], refs[nk:2 * nk]
        ssem, rsem = refs[2 * nk:]
        x, y, c = _place()
        cps = []
        for k in range(nk):
            rows = grads[k].shape[1]
            cp = pltpu.make_async_remote_copy(src_ref=ins[k].at[:, _half(rows, 1 - c)], dst_ref=outs[k],
                                              send_sem=ssem.at[k], recv_sem=rsem.at[k],
                                              device_id=(x, y, 1 - c), device_id_type=MESH)
            cp.start()
            cps.append(cp)
        for cp in cps:
            cp.wait()

    return pl.pallas_call(
        body, name=name, in_specs=[ANY] * nk, out_specs=[ANY] * nk,
        out_shape=[_sds((N_CHIPS, g.shape[1] // 2, g.shape[2]), g.dtype) for g in grads],
        scratch_shapes=[pltpu.SemaphoreType.DMA((nk,)), pltpu.SemaphoreType.DMA((nk,))],
    )(*grads)


def _rs_chip_copies(ins, outs, ssem, rsem):
    x, y, c = _place()
    cps = []
    for k in range(len(ins)):
        for idx, (ox, oy) in enumerate(FLIPS):
            px, py = _flip(x, y, ox, oy)
            cps.append(_remote(ins[k].at[2 * px + py], outs[k].at[idx], ssem.at[k, idx], rsem.at[k, idx], (px, py, c)))
    return cps


def _rs_chips_shapes(psums):
    nk = len(psums)
    return ([_sds((3,) + tuple(p.shape[1:]), p.dtype) for p in psums],
            [pltpu.SemaphoreType.DMA((nk, 3)), pltpu.SemaphoreType.DMA((nk, 3))])


def _rs_share(fulls):
    nk = len(fulls)

    def body(*refs):
        ins, outs = refs[:nk], refs[nk:2 * nk]
        ssem, rsem = refs[2 * nk:]
        x, y, c = _place()
        cps = []
        for k in range(nk):
            rows = fulls[k].shape[0]
            cp = _remote(ins[k].at[_half(rows, c)], outs[k].at[_half(rows, c)], ssem.at[k], rsem.at[k], (x, y, 1 - c))
            cp.start()
            cps.append(cp)
        for k, cp in enumerate(cps):
            rows = fulls[k].shape[0]
            cp.wait_send()
            theirs = outs[k].at[_half(rows, 1 - c)]
            _remote(theirs, theirs, ssem.at[k], rsem.at[k], (x, y, 1 - c)).wait_recv()

    return pl.pallas_call(
        body, name="rs_share", in_specs=[ANY] * nk, out_specs=[ANY] * nk,
        out_shape=[_sds(f.shape, f.dtype) for f in fulls], input_output_aliases={k: k for k in range(nk)},
        scratch_shapes=[pltpu.SemaphoreType.DMA((nk,)), pltpu.SemaphoreType.DMA((nk,))],
    )(*fulls)


def _allreduce_small(v):
    ndev = 8

    def body(in_ref, out_ref, buf, ssem, rsem):
        x, y, c = _place()
        me = 4 * x + 2 * y + c
        buf[me] = in_ref[...]
        cps = []
        for k in range(1, ndev):
            ox, oy, oc = (k >> 2) & 1, (k >> 1) & 1, k & 1
            px, py = _flip(x, y, ox, oy)
            pc = 1 - c if oc else c
            cp = pltpu.make_async_remote_copy(src_ref=in_ref, dst_ref=buf.at[me], send_sem=ssem.at[k - 1],
                                              recv_sem=rsem.at[k - 1], device_id=(px, py, pc), device_id_type=MESH)
            cp.start()
            cps.append((cp, 4 * px + 2 * py + pc, (px, py, pc)))
        for k, (cp, src, peer) in enumerate(cps):
            cp.wait_send()
            pltpu.make_async_remote_copy(src_ref=in_ref, dst_ref=buf.at[src], send_sem=ssem.at[k],
                                         recv_sem=rsem.at[k], device_id=peer, device_id_type=MESH).wait_recv()
        acc = buf[0]
        for i in range(1, ndev):
            acc = acc + buf[i]
        out_ref[...] = acc

    return pl.pallas_call(
        body, name="allreduce_small",
        in_specs=[pl.BlockSpec(memory_space=pltpu.VMEM)], out_specs=pl.BlockSpec(memory_space=pltpu.VMEM),
        out_shape=_sds(v.shape, v.dtype),
        scratch_shapes=[pltpu.VMEM((ndev,) + v.shape, v.dtype), pltpu.SemaphoreType.DMA((ndev - 1,)),
                        pltpu.SemaphoreType.DMA((ndev - 1,))],
    )(v)


def _rs_sum1(name, g, recv, c_idx):
    _, r, cdim = g.shape
    hr = r // 2
    tr = min(hr, 256)
    nr = hr // tr

    def body(c_ref, g_ref, r_ref, o32_ref, o16_ref):
        v = g_ref[...] + r_ref[...].astype(F32)
        o32_ref[...] = v
        o16_ref[...] = v.astype(BF16)

    spec = pl.BlockSpec((None, tr, cdim), lambda j, i, c_ref: (j, i, 0))
    return pl.pallas_call(
        body, name=name,
        grid_spec=pltpu.PrefetchScalarGridSpec(
            num_scalar_prefetch=1, grid=(N_CHIPS, nr),
            in_specs=[pl.BlockSpec((None, tr, cdim), lambda j, i, c_ref: (j, c_ref[0] * nr + i, 0)), spec],
            out_specs=[spec, spec]),
        out_shape=[_sds((N_CHIPS, hr, cdim), F32), _sds((N_CHIPS, hr, cdim), BF16)],
        compiler_params=_cp(("parallel", "parallel")),
    )(c_idx, g, recv)


def _rs_sum2(name, p32, recv, jc_idx):
    _, hr, cdim = p32.shape
    tr = min(hr, 256)
    nr = hr // tr

    def body(jc_ref, p_ref, r_ref, o_ref):
        o_ref[...] = ((p_ref[...] + r_ref[0].astype(F32)) + r_ref[1].astype(F32)) + r_ref[2].astype(F32)

    return pl.pallas_call(
        body, name=name,
        grid_spec=pltpu.PrefetchScalarGridSpec(
            num_scalar_prefetch=1, grid=(nr,),
            in_specs=[pl.BlockSpec((None, tr, cdim), lambda i, jc: (jc[0], i, 0)),
                      pl.BlockSpec((3, tr, cdim), lambda i, jc: (0, i, 0))],
            out_specs=pl.BlockSpec((tr, cdim), lambda i, jc: (jc[1] * nr + i, 0))),
        out_shape=_sds((2 * hr, cdim), F32),
        compiler_params=_cp(("parallel",)),
    )(jc_idx, p32, recv)


def _adamw(name, w, g, m, v):
    r, cdim = w.shape
    tr = min(r, 256)
    c1 = 1.0 - ADAM_B1 ** ADAM_STEP
    c2 = 1.0 - ADAM_B2 ** ADAM_STEP

    def body(w_ref, g_ref, m_ref, v_ref, d_ref, nm_ref, nv_ref):
        gv = g_ref[...]
        nm = ADAM_B1 * m_ref[...] + (1.0 - ADAM_B1) * gv
        nv = ADAM_B2 * v_ref[...] + (1.0 - ADAM_B2) * (gv * gv)
        d_ref[...] = -ADAM_LR * ((nm / c1) / (jnp.sqrt(nv / c2) + ADAM_EPS) + ADAM_WD * w_ref[...])
        nm_ref[...] = nm
        nv_ref[...] = nv

    spec = pl.BlockSpec((tr, cdim), lambda i: (i, 0))
    return pl.pallas_call(
        body, name=name, grid=(r // tr,), in_specs=[spec] * 4, out_specs=[spec] * 3,
        out_shape=[_sds((r, cdim), F32)] * 3, compiler_params=_cp(("parallel",)),
    )(w, g, m, v)


def _pack_small(mix_pre, attn_out, lb_logits, hgrn_out, mix_post, mlp_pre, mlp_post):
    rows = [mix_pre, jnp.concatenate([attn_out, hgrn_out], axis=1),
            jnp.concatenate([lb_logits[0:1], lb_logits[1:2]], axis=1), mix_post, mlp_pre, mlp_post,
            jnp.zeros((2, D_MODEL), F32)]
    return jnp.concatenate(rows, axis=0)


def _unpack_small(p):
    return (p[0:1], p[1:2, :ATTN_W], jnp.concatenate([p[2:3, :HGRN_W], p[2:3, HGRN_W:]], axis=0),
            p[1:2, ATTN_W:], p[3:4], p[4:5], p[5:6])


def kernel(x, mix_pre_norm, w_in, attn_out_norm, hgrn_lb_logits, hgrn_out_norm, w_out, mix_post_norm, mlp_pre_norm, w_ff1, w_ff2, mlp_post_norm, loss_target, m_mix_pre_norm, m_w_in, m_attn_out_norm, m_hgrn_lb_logits, m_hgrn_out_norm, m_w_out, m_mix_post_norm, m_mlp_pre_norm, m_w_ff1, m_w_ff2, m_mlp_post_norm, v_mix_pre_norm, v_w_in, v_attn_out_norm, v_hgrn_lb_logits, v_hgrn_out_norm, v_w_out, v_mix_post_norm, v_mlp_pre_norm, v_w_ff1, v_w_ff2, v_mlp_post_norm):
    s = x.shape[1]
    xs = x.reshape(s, D_MODEL)
    tgt = loss_target.reshape(s, D_MODEL)
    cx, cy, cc = _place()
    chip = 2 * cx + cy
    c_idx = jnp.reshape(cc, (1,)).astype(jnp.int32)
    jc_idx = jnp.stack([chip, cc]).astype(jnp.int32)

    big_w = [w_in[0], w_out[0], w_ff1[0], w_ff2[0]]
    big_m = [m_w_in[0], m_w_out[0], m_w_ff1[0], m_w_ff2[0]]
    big_v = [v_w_in[0], v_w_out[0], v_w_ff1[0], v_w_ff2[0]]
    shards = [w.astype(BF16) for w in big_w]

    (wg_in,) = _ag_weights("ag_in", shards[:1])
    wg_in = _with_own(wg_in, shards[0], chip)

    (h,) = _rows_call("norm_in", lambda xv, g: ((xv * _rstd(xv) * g),),
                      [(xs, _row(D_MODEL)), (mix_pre_norm, "full")], [(D_MODEL, BF16, "row")], s)
    (proj,) = _mm_cols("mm_proj", h, wg_in, NN, [F32])
    hg_o, rec, states = _hgrn_fwd(proj, hgrn_lb_logits, hgrn_out_norm)
    attn_o, attn_lse, wg_out, wg_1, wg_2 = _attn_fwd(proj, shards[1:])
    wg_out, wg_1, wg_2 = (_with_own(g, w, chip) for g, w in zip((wg_out, wg_1, wg_2), shards[1:]))
    (attn_n,) = _rows_call("attn_norm", lambda o, gain: (o * _rstd(o) * gain,),
                           [(attn_o, _row(ATTN_W)), (attn_out_norm, "full")], [(ATTN_W, BF16, "row")], s)
    cat = jnp.concatenate([attn_n, rec], axis=1)
    mixed = _mm_acc("mm_mixed", cat, wg_out, NN)

    def post1(xv, mv, g_post, g_pre2):
        x1 = xv + mv * _rstd(mv) * g_post
        return x1, x1 * _rstd(x1) * g_pre2

    x1, h2 = _rows_call("post1", post1, [(xs, _row(D_MODEL)), (mixed, _row(D_MODEL)), (mix_post_norm, "full"),
                                         (mlp_pre_norm, "full")], [(D_MODEL, F32, "row"), (D_MODEL, BF16, "row")], s)

    def sq_relu(u):
        r = jnp.maximum(u, 0.0)
        return r * r, r

    act, ru = _mm_cols("mm_ff1", h2, wg_1, NN, [BF16, BF16], epi=sq_relu)
    ff = _mm_acc("mm_ff2", act, wg_2, NN)

    def post2(x1v, fv, tv, g):
        y = x1v + fv * _rstd(fv) * g
        dy = (y - tv) * (1.0 / D_MODEL)
        err = y - tv
        loss = 0.5 * jnp.sum(jnp.mean(err * err, axis=-1, keepdims=True), axis=0, keepdims=True)
        dff, dgc = _norm_bwd(fv, g, dy)
        return dy, dff, _colsum(dgc), jnp.broadcast_to(loss, (1, BLK))

    dy, dff, g_mlp_post, loss_part = _rows_call(
        "post2", post2, [(x1, _row(D_MODEL)), (ff, _row(D_MODEL)), (tgt, _row(D_MODEL)), (mlp_post_norm, "full")],
        [(D_MODEL, F32, "row"), (D_MODEL, BF16, "row"), (D_MODEL, F32, "acc"), (BLK, F32, "acc")], s)

    (du,) = _mm_cols("mm_du", dff, wg_2, NT, [BF16], epi=lambda acc, r: (acc * (2.0 * r.astype(F32)),),
                     extras=(ru,))
    gw_2 = _mm_wgrad("mm_gw2", act, dff, True)
    gw_1 = _mm_wgrad("mm_gw1", h2, du, False)
    dh2 = _mm_acc("mm_dh2", du, wg_1, NT)

    def bwd_mid(dyv, dh2v, x1v, mv, g_pre2, g_post):
        d1, gc1 = _norm_bwd(x1v, g_pre2, dh2v)
        dx1 = dyv + d1
        dm, gc2 = _norm_bwd(mv, g_post, dx1)
        return dx1, dm, _colsum(gc1), _colsum(gc2)

    dx1, dmixed, g_mlp_pre, g_mix_post = _rows_call(
        "bwd_mid", bwd_mid, [(dy, _row(D_MODEL)), (dh2, _row(D_MODEL)), (x1, _row(D_MODEL)), (mixed, _row(D_MODEL)),
                             (mlp_pre_norm, "full"), (mix_post_norm, "full")],
        [(D_MODEL, F32, "row"), (D_MODEL, BF16, "row"), (D_MODEL, F32, "acc"), (D_MODEL, F32, "acc")], s)

    (dcat,) = _mm_cols("mm_dcat", dmixed, wg_out, NT, [F32])
    gw_out = _mm_wgrad("mm_gwout", cat, dmixed, True)

    names = ["out", "ff1", "ff2", "in"]
    ready = [gw_out, gw_1, gw_2]
    from_pair = _rs_pair("rs_pair_3", [g[1] for g in ready])
    pair = [_rs_sum1(f"rs_sum1_{n}", g[0], r, c_idx) for n, g, r in zip(names, ready, from_pair)]

    def attn_norm_bwd(dc, o, gain):
        do, gc = _norm_bwd(o, gain, dc)
        t = do * o
        lane = lax.broadcasted_iota(jnp.int32, (t.shape[0], BLK), 1) < 64
        parts = []
        for p in range(ATTN_W // BLK):
            tp = t[:, p * BLK:(p + 1) * BLK]
            sa = jnp.sum(jnp.where(lane, tp, 0.0), axis=1, keepdims=True)
            sb = jnp.sum(jnp.where(lane, 0.0, tp), axis=1, keepdims=True)
            parts.append(jnp.where(lane, sa, sb))
        return do, jnp.concatenate(parts, axis=1), _colsum(gc)

    do_attn, delta, g_attn_out = _rows_call(
        "attn_norm_bwd", attn_norm_bwd, [(dcat, _row(ATTN_W, 0)), (attn_o, _row(ATTN_W)), (attn_out_norm, "full")],
        [(ATTN_W, F32, "row"), (ATTN_W, F32, "row"), (ATTN_W, F32, "acc")], s)
    dq, dk, dv, *from_chips = _attn_bwd(proj, do_attn, attn_lse, delta, [p[1] for p in pair])
    dhq, dhf, dhi, dhg, g_hgrn_out, g_lb = _hgrn_bwd(proj, hg_o, states, dcat, hgrn_lb_logits, hgrn_out_norm)

    def dproj_asm(*a):
        return (jnp.concatenate([t.astype(BF16) for t in a], axis=1),)

    (dproj,) = _rows_call("dproj_asm", dproj_asm,
                          [(t, _row(ATTN_W)) for t in (dq, dk, dv)] + [(t, _row(HGRN_W)) for t in (dhq, dhf, dhi, dhg)],
                          [(PROJ_W, BF16, "row")], s)
    gw_in = _mm_wgrad("mm_gwin", h, dproj, False)
    (from_pair_in,) = _rs_pair("rs_pair_in", [gw_in[1]])
    pair.append(_rs_sum1("rs_sum1_in", gw_in[0], from_pair_in, c_idx))
    rs_shape, rs_sems = _rs_chips_shapes([pair[3][1]])
    dh, from_chips_in = _mm_acc("mm_dh", dproj, wg_in, NT,
                                exchange=([pair[3][1]], rs_shape, rs_sems, _rs_chip_copies))
    from_chips.append(from_chips_in)

    def bwd_in(dx1v, dhv, xv, g):
        d0, gc = _norm_bwd(xv, g, dhv)
        return dx1v + d0, _colsum(gc)

    grad_x, g_mix_pre = _rows_call("bwd_in", bwd_in, [(dx1, _row(D_MODEL)), (dh, _row(D_MODEL)), (xs, _row(D_MODEL)),
                                                      (mix_pre_norm, "full")],
                                   [(D_MODEL, F32, "row"), (D_MODEL, F32, "acc")], s)

    loss = lax.psum(loss_part[0, 0], ("x", "y", "c"))
    small_g = _allreduce_small(_pack_small(g_mix_pre, g_attn_out, g_lb, g_hgrn_out, g_mix_post, g_mlp_pre, g_mlp_post))

    reduced =[_rs_sum2(f"rs_sum2_{n}", p[0], r, jc_idx) for n, p, r in zip(names, pair, from_chips)]
    g_wout, g_w1, g_w2, g_win = _rs_share(reduced)
    full = [g_win, g_wout, g_w1, g_w2]

    upd = [_adamw(f"adamw_{n}", w, g, m, v) for n, w, g, m, v in zip(("in", "out", "ff1", "ff2"), big_w, full, big_m, big_v)]
    small_w = _pack_small(mix_pre_norm, attn_out_norm, hgrn_lb_logits, hgrn_out_norm, mix_post_norm, mlp_pre_norm,
                          mlp_post_norm)
    small_m = _pack_small(m_mix_pre_norm, m_attn_out_norm, m_hgrn_lb_logits, m_hgrn_out_norm, m_mix_post_norm,
                          m_mlp_pre_norm, m_mlp_post_norm)
    small_v = _pack_small(v_mix_pre_norm, v_attn_out_norm, v_hgrn_lb_logits, v_hgrn_out_norm, v_mix_post_norm,
                          v_mlp_pre_norm, v_mlp_post_norm)
    small_upd = _adamw("adamw_small", small_w, small_g, small_m, small_v)

    def assemble(small, big):
        sm = _unpack_small(small)
        return (sm[0], big[0][None], sm[1], sm[2], sm[3], big[1][None], sm[4], sm[5], big[2][None], big[3][None], sm[6])

    g_out = assemble(small_g, full)
    d_out = assemble(small_upd[0], [u[0] for u in upd])
    m_out = assemble(small_upd[1], [u[1] for u in upd])
    v_out = assemble(small_upd[2], [u[2] for u in upd])
    return (loss, grad_x.reshape(x.shape), *g_out, *d_out, *m_out, *v_out)
```

```python
import numpy as np
import jax
import jax.numpy as jnp
from jax import lax
from jax.experimental import pallas as pl
from jax.experimental.pallas import tpu as pltpu

F32 = jnp.float32
BF16 = jnp.bfloat16
MESH = pl.DeviceIdType.MESH
ANY = pl.BlockSpec(memory_space=pl.ANY)

RMS_EPS = 1e-6
D_MODEL = 1024
ATTN_W = 512
HGRN_W = 512
PROJ_W = 3584
D_FF = 4096
N_CHIPS = 4
BLK = 128
CHUNK = 64
HGRN_TB = 512
ATTN_GROUP = 4
DILATIONS = (1, 4, 16)
ATTN_SCALE = 0.125
ROW_TILE = 512
MM_TILE = 1024
VMEM_LIMIT = 48 * 2 ** 20
FLIPS = ((1, 0), (0, 1), (1, 1))

ADAM_LR, ADAM_B1, ADAM_B2, ADAM_EPS, ADAM_WD, ADAM_STEP = 0.001, 0.9, 0.999, 1e-08, 0.01, 10


def _cp(sem=None):
    return pltpu.CompilerParams(dimension_semantics=sem, vmem_limit_bytes=VMEM_LIMIT)


def _sigmoid(v):
    return 1.0 / (1.0 + jnp.exp(-v))


def _dot(a, b, contract, precision=None):
    return lax.dot_general(a, b, (contract, ((), ())), preferred_element_type=F32, precision=precision)


NN = ((1,), (0,))
NT = ((1,), (1,))
TN = ((0,), (0,))


def _matmul(name, a, b, *, grid, a_spec, b_spec, contract, outs, epi=None, extras=(), extra_specs=()):
    n_ex = len(extras)

    def body(a_ref, b_ref, *rest):
        ex, out_refs = rest[:n_ex], rest[n_ex:]
        acc = _dot(a_ref[...], b_ref[...], contract)
        res = epi(acc, *[e[...] for e in ex]) if epi else (acc,)
        for o, r in zip(out_refs, res):
            o[...] = r.astype(o.dtype)

    return pl.pallas_call(
        body, name=name, grid=grid,
        in_specs=[a_spec, b_spec, *extra_specs],
        out_specs=[s for _, s in outs],
        out_shape=[s for s, _ in outs],
        compiler_params=_cp(("parallel",) * len(grid)),
    )(a, b, *extras)


def _sds(shape, dtype):
    return jax.ShapeDtypeStruct(shape, dtype)


def _mm_cols(name, a, w, contract, out_dtypes, epi=None, extras=()):
    m, k = a.shape
    jn = w.shape[0]
    nj = w.shape[2] if contract == NN else w.shape[1]
    tm = min(m, MM_TILE)
    outs = [(_sds((m, jn * nj), dt), pl.BlockSpec((tm, nj), lambda j, i: (i, j))) for dt in out_dtypes]
    return _matmul(name, a, w, grid=(jn, m // tm),
                   a_spec=pl.BlockSpec((tm, k), lambda j, i: (i, 0)),
                   b_spec=pl.BlockSpec((None,) + w.shape[1:], lambda j, i: (j, 0, 0)),
                   contract=contract, outs=outs, epi=epi, extras=extras,
                   extra_specs=[pl.BlockSpec((tm, nj), lambda j, i: (i, j)) for _ in extras])


def _mm_wgrad(name, a, b, a_by_j):
    s = a.shape[0]
    if a_by_j:
        r, c = a.shape[1] // N_CHIPS, b.shape[1]
    else:
        r, c = a.shape[1], b.shape[1] // N_CHIPS
    tr = min(r, 512)
    nr = r // tr
    if a_by_j:
        a_spec = pl.BlockSpec((s, tr), lambda j, i: (0, j * nr + i))
        b_spec = pl.BlockSpec((s, c), lambda j, i: (0, 0))
    else:
        a_spec = pl.BlockSpec((s, tr), lambda j, i: (0, i))
        b_spec = pl.BlockSpec((s, c), lambda j, i: (0, j))
    outs = [(_sds((N_CHIPS, r, c), dt), pl.BlockSpec((None, tr, c), lambda j, i: (j, i, 0))) for dt in (F32, BF16)]
    return _matmul(name, a, b, grid=(N_CHIPS, nr), a_spec=a_spec, b_spec=b_spec, contract=TN, outs=outs,
                   epi=lambda acc: (acc, acc))


def _rows_call(name, fn, ins, outs, s, tm=ROW_TILE, matmul=None, exchange=None):
    in_specs = []
    if matmul:
        a, w = matmul
        in_specs += [pl.BlockSpec((tm, a.shape[1]), lambda i: (i, 0)), pl.BlockSpec(w.shape, lambda i: (0, 0))]
    for arr, kind in ins:
        if kind == "full":
            in_specs.append(pl.BlockSpec(arr.shape, lambda i: (0, 0)))
        else:
            _, w_, cb = kind
            in_specs.append(pl.BlockSpec((tm, w_), lambda i, cb=cb: (i, cb)))
    out_specs, out_shape, is_acc = [], [], []
    for w_, dt, kind in outs:
        if kind == "acc":
            out_specs.append(pl.BlockSpec((1, w_), lambda i: (0, 0)))
            out_shape.append(_sds((1, w_), dt))
        else:
            out_specs.append(pl.BlockSpec((tm, w_), lambda i: (i, 0)))
            out_shape.append(_sds((s, w_), dt))
        is_acc.append(kind == "acc")
    n_mm, n_in, n_out = (2 if matmul else 0), len(ins), len(outs)
    x_ins, x_shapes, x_sems, x_copies = exchange if exchange else ((), [], [], None)
    n_x = len(x_ins)
    steps = s // tm

    def body(*refs):
        in_refs, xi = refs[n_mm:n_mm + n_in], refs[n_mm + n_in:n_mm + n_in + n_x]
        out_refs = refs[n_mm + n_in + n_x:n_mm + n_in + n_x + n_out]
        xo, sems = refs[n_mm + n_in + n_x + n_out:n_mm + n_in + 2 * n_x + n_out], refs[n_mm + n_in + 2 * n_x + n_out:]
        i = pl.program_id(0)

        if exchange:
            @pl.when(i == 0)
            def _():
                for cp in x_copies(xi, xo, *sems):
                    cp.start()

        args = [r[...] for r in in_refs]
        if matmul:
            args.insert(0, _dot(refs[0][...], refs[1][...], NN))
        res = fn(*args)
        for o, r, acc in zip(out_refs, res, is_acc):
            if acc:
                @pl.when(i == 0)
                def _(o=o):
                    o[...] = jnp.zeros_like(o)
                o[...] += r.astype(o.dtype)
            else:
                o[...] = r.astype(o.dtype)

        if exchange:
            @pl.when(i == steps - 1)
            def _():
                for cp in x_copies(xi, xo, *sems):
                    cp.wait()

    sem = ("arbitrary",) if any(is_acc) or exchange else ("parallel",)
    return pl.pallas_call(
        body, name=name, grid=(steps,), in_specs=in_specs + [ANY] * n_x, out_specs=out_specs + [ANY] * n_x,
        out_shape=out_shape + list(x_shapes), scratch_shapes=list(x_sems), compiler_params=_cp(sem),
    )(*(matmul or ()), *[a for a, _ in ins], *x_ins)


def _rstd(v):
    return lax.rsqrt(jnp.mean(v * v, axis=-1, keepdims=True) + RMS_EPS)


def _norm_bwd(v, gain, dy):
    r = _rstd(v)
    n = v * r
    dn = dy * gain
    dv = r * (dn - n * jnp.mean(dn * n, axis=-1, keepdims=True))
    return dv, dy * n


def _colsum(v):
    return jnp.sum(v, axis=0, keepdims=True)


def _row(w, cb=0):
    return ("row", w, cb)


N_PAIRS = ATTN_W // BLK


def _head_col(v, mask):
    return jnp.max(jnp.where(mask, v, -jnp.inf), axis=1, keepdims=True)


def _slopes():
    t = np.zeros((N_PAIRS, 8, 2 * BLK), np.float32)
    for p in range(N_PAIRS):
        for hh in range(2):
            t[p, hh, :] = 2.0 ** -(2 * p + hh + 1)
    return jnp.asarray(t)


def _rows(n, r, d):
    base = pl.multiple_of(n * (BLK * d), BLK)
    return pl.ds(base + r, BLK, stride=d) if d > 1 else pl.ds(base, BLK)


def _attn_groups(s, d):
    nb = s // (BLK * d)
    g = ATTN_GROUP
    if d >= g:
        return [(nb, lambda n, r0=r0: [(n, r0 + u) for u in range(g)]) for r0 in range(0, d, g)]
    per = g // d
    return [(nb // per, lambda t: [(per * t + u, r) for u in range(per) for r in range(d)])]


def _attn_fwd(proj, shards):
    s = proj.shape[0]
    nk = len(shards)

    def body(sl_ref, q_ref, k_ref, v_ref, *rest):
        w_refs, (o_ref, l_ref) = rest[:nk], rest[nk:nk + 2]
        wg_refs, (ssem, rsem) = rest[nk + 2:2 * nk + 2], rest[2 * nk + 2:]
        pair = pl.program_id(0)

        @pl.when(pair == 0)
        def _():
            _ag_start(w_refs, wg_refs, ssem, rsem)

        row = lax.broadcasted_iota(jnp.int32, (BLK, 2 * BLK), 0)
        col = lax.broadcasted_iota(jnp.int32, (BLK, 2 * BLK), 1)
        dist = row + BLK - col
        in_window = (dist >= 0) & (dist <= BLK)
        distf = dist.astype(F32)
        lane_q = lax.broadcasted_iota(jnp.int32, (BLK, BLK), 1) < 64
        lane_k = lax.broadcasted_iota(jnp.int32, (2 * BLK, BLK), 1) < 64

        def branch(n, r, d):
            rows = _rows(n, r, d)
            prev = _rows(jnp.maximum(n - 1, 0), r, d)
            valid = in_window & (col + n * BLK >= BLK)
            q2 = q_ref[rows, :]
            kk = jnp.concatenate([k_ref[prev, :], k_ref[rows, :]], axis=0).astype(BF16)
            vv = jnp.concatenate([v_ref[prev, :], v_ref[rows, :]], axis=0)
            o2 = jnp.zeros((BLK, BLK), F32)
            lse2 = jnp.zeros((BLK, BLK), F32)
            for hh in range(2):
                mq = lane_q if hh == 0 else ~lane_q
                mk = lane_k if hh == 0 else ~lane_k
                qm = jnp.where(mq, q2, 0.0).astype(BF16)
                sc = _dot(qm, kk, NT) * ATTN_SCALE - (sl_ref[hh:hh + 1, :] * float(d)) * distf
                sc = jnp.where(valid, sc, -1e30)
                m = jnp.max(sc, axis=1, keepdims=True)
                pr = jnp.exp(sc - m)
                den = jnp.sum(pr, axis=1, keepdims=True)
                vm = jnp.where(mk, vv, 0.0).astype(BF16)
                o2 = o2 + _dot(pr.astype(BF16), vm, NN) / den
                lse2 = jnp.where(mq, m + jnp.log(den), lse2)
            return rows, o2, lse2

        def merge(rows, o2, lse2, first):
            if first:
                o_ref[rows, :] = o2
                l_ref[rows, :] = lse2
            else:
                lo = l_ref[rows, :]
                mx = jnp.maximum(lo, lse2)
                ln = mx + jnp.log(jnp.exp(lo - mx) + jnp.exp(lse2 - mx))
                o_ref[rows, :] = jnp.exp(lo - ln) * o_ref[rows, :] + jnp.exp(lse2 - ln) * o2
                l_ref[rows, :] = ln

        for di, d in enumerate(DILATIONS):
            for trips, blocks in _attn_groups(s, d):
                def trip(t, carry, d=d, blocks=blocks, first=(di == 0)):
                    done = [branch(n, r, d) for n, r in blocks(t)]
                    for rows, o2, lse2 in done:
                        merge(rows, o2, lse2, first)
                    return carry

                lax.fori_loop(0, trips, trip, 0)

        @pl.when(pair == N_PAIRS - 1)
        def _():
            _ag_finish(w_refs, wg_refs, ssem, rsem)

    cb = lambda base: pl.BlockSpec((s, BLK), lambda p, base=base: (0, base + p))
    out = pl.BlockSpec((s, BLK), lambda p: (0, p))
    ag_shape, ag_sems = _ag_shapes(shards)
    return pl.pallas_call(
        body, name="attn_fwd", grid=(N_PAIRS,),
        in_specs=[pl.BlockSpec((None, 8, 2 * BLK), lambda p: (p, 0, 0)), cb(0), cb(N_PAIRS), cb(2 * N_PAIRS)]
        + [ANY] * nk,
        out_specs=[out, out] + [ANY] * nk, out_shape=[_sds((s, ATTN_W), F32)] * 2 + ag_shape,
        scratch_shapes=ag_sems, compiler_params=_cp(("arbitrary",)),
    )(_slopes(), proj, proj, proj, *shards)


def _attn_bwd(proj, do, lse, delta, psums):
    s = proj.shape[0]
    nk = len(psums)

    def body(sl_ref, q_ref, k_ref, v_ref, do_ref, l_ref, e_ref, *rest):
        p_refs, (dq_ref, dk_ref, dv_ref) = rest[:nk], rest[nk:nk + 3]
        got_refs, (ssem, rsem) = rest[nk + 3:2 * nk + 3], rest[2 * nk + 3:]
        pair = pl.program_id(0)

        @pl.when(pair == 0)
        def _():
            for cp in _rs_chip_copies(p_refs, got_refs, ssem, rsem):
                cp.start()

        row = lax.broadcasted_iota(jnp.int32, (BLK, 2 * BLK), 0)
        col = lax.broadcasted_iota(jnp.int32, (BLK, 2 * BLK), 1)
        dist = row + BLK - col
        in_window = (dist >= 0) & (dist <= BLK)
        distf = dist.astype(F32)
        lane_q = lax.broadcasted_iota(jnp.int32, (BLK, BLK), 1) < 64
        lane_k = lax.broadcasted_iota(jnp.int32, (2 * BLK, BLK), 1) < 64
        dk_ref[...] = jnp.zeros_like(dk_ref)
        dv_ref[...] = jnp.zeros_like(dv_ref)

        def branch(n, r, d):
            rows = _rows(n, r, d)
            prev = _rows(jnp.maximum(n - 1, 0), r, d)
            valid = in_window & (col + n * BLK >= BLK)
            q1, d1, l1, e1 = q_ref[rows, :], do_ref[rows, :], l_ref[rows, :], e_ref[rows, :]
            kk = jnp.concatenate([k_ref[prev, :], k_ref[rows, :]], axis=0)
            kkb = kk.astype(BF16)
            vvb = jnp.concatenate([v_ref[prev, :], v_ref[rows, :]], axis=0).astype(BF16)
            dq2 = jnp.zeros((BLK, BLK), F32)
            dkk = jnp.zeros((2 * BLK, BLK), F32)
            dvv = jnp.zeros((2 * BLK, BLK), F32)
            for hh in range(2):
                mq = lane_q if hh == 0 else ~lane_q
                mk = lane_k if hh == 0 else ~lane_k
                qm = jnp.where(mq, q1, 0.0).astype(BF16)
                dm = jnp.where(mq, d1, 0.0).astype(BF16)
                sc = _dot(qm, kkb, NT) * ATTN_SCALE - (sl_ref[hh:hh + 1, :] * float(d)) * distf
                pr = jnp.where(valid, jnp.exp(sc - _head_col(l1, mq)), 0.0)
                ds = (pr * (_dot(dm, vvb, NT) - _head_col(e1, mq))).astype(BF16)
                km = jnp.where(mk, kk, 0.0).astype(BF16)
                dq2 = dq2 + _dot(ds, km, NN) * ATTN_SCALE
                dkk = dkk + _dot(ds, qm, TN) * ATTN_SCALE
                dvv = dvv + _dot(pr.astype(BF16), dm, TN)
            return rows, prev, dq2, dkk, dvv

        for di, d in enumerate(DILATIONS):
            for trips, blocks in _attn_groups(s, d):
                def trip(t, carry, d=d, blocks=blocks, first=(di == 0)):
                    done = [branch(n, r, d) for n, r in blocks(t)]
                    for rows, prev, dq2, dkk, dvv in done:
                        dq_ref[rows, :] = dq2 if first else dq_ref[rows, :] + dq2
                        dk_ref[prev, :] = dk_ref[prev, :] + dkk[:BLK]
                        dk_ref[rows, :] = dk_ref[rows, :] + dkk[BLK:]
                        dv_ref[prev, :] = dv_ref[prev, :] + dvv[:BLK]
                        dv_ref[rows, :] = dv_ref[rows, :] + dvv[BLK:]
                    return carry

                lax.fori_loop(0, trips, trip, 0)

        @pl.when(pair == N_PAIRS - 1)
        def _():
            for cp in _rs_chip_copies(p_refs, got_refs, ssem, rsem):
                cp.wait()

    cb = lambda base: pl.BlockSpec((s, BLK), lambda p, base=base: (0, base + p))
    out = pl.BlockSpec((s, BLK), lambda p: (0, p))
    rs_shape, rs_sems = _rs_chips_shapes(psums)
    return pl.pallas_call(
        body, name="attn_bwd", grid=(N_PAIRS,),
        in_specs=[pl.BlockSpec((None, 8, 2 * BLK), lambda p: (p, 0, 0)), cb(0), cb(N_PAIRS), cb(2 * N_PAIRS),
                  out, out, out] + [ANY] * nk,
        out_specs=[out] * 3 + [ANY] * nk, out_shape=[_sds((s, ATTN_W), F32)] * 3 + rs_shape,
        scratch_shapes=rs_sems, compiler_params=_cp(("arbitrary",)),
    )(_slopes(), proj, proj, proj, do, lse, delta, *psums)


def _lower_bound(lbl):
    return 1.0 / (1.0 + jnp.exp(lbl[1:2, :] - lbl[0:1, :]))


def _hi(a):
    bits = lax.bitcast_convert_type(a, jnp.uint32) & jnp.uint32(0xFFFF0000)
    return lax.bitcast_convert_type(bits, F32)


def _dot3(a, b, contract):
    ah, bh = _hi(a), _hi(b)
    al, bl = (a - ah).astype(BF16), (b - bh).astype(BF16)
    ah, bh = ah.astype(BF16), bh.astype(BF16)
    return _dot(ah, bh, contract) + (_dot(ah, bl, contract) + _dot(al, bh, contract))


def _cumsum_rows(tri, g):
    g1 = _hi(g)
    r1 = g - g1
    g2 = _hi(r1)
    g3 = r1 - g2
    return _dot(tri, g1.astype(BF16), NN) + (_dot(tri, g2.astype(BF16), NN) + _dot(tri, g3.astype(BF16), NN))


def _heads(fn):
    return jnp.concatenate([fn(slice(h * BLK, (h + 1) * BLK)) for h in range(HGRN_W // BLK)], axis=1)


def _head_mean(t):
    return _heads(lambda hs: jnp.broadcast_to(jnp.mean(t[:, hs], axis=1, keepdims=True), (t.shape[0], BLK)))


def _hgrn_chunk(q_ref, f_ref, i_ref, sl, lb, tri):
    qp = q_ref[sl, :]
    sq = _sigmoid(qp)
    qf = qp * sq
    sg = _sigmoid(f_ref[sl, :])
    f = lb + (1.0 - lb) * sg
    kf = 1.0 - f
    v = i_ref[sl, :]
    b = _cumsum_rows(tri, jnp.log(f))
    bm = b[CHUNK // 2:CHUNK // 2 + 1, :]
    bl = b[CHUNK - 1:CHUNK, :]
    qt = qf * jnp.exp(b - bm)
    kt = kf * jnp.exp(bm - b)
    return qp, sq, qf, sg, f, kf, v, b, bm, bl, qt, kt


def _hgrn_specs(tb, block):
    first = 3 * ATTN_W // HGRN_W
    return [pl.BlockSpec((tb, HGRN_W), lambda i, k=k: (block(i), first + k)) for k in range(4)]


def _hgrn_fwd(proj, lb_logits, out_gain):
    s = proj.shape[0]
    tb = min(HGRN_TB, s)
    nb, cpb, nc = s // tb, tb // CHUNK, s // CHUNK

    def body(q_ref, f_ref, i_ref, g_ref, lbl_ref, gain_ref, o_ref, rec_ref, st_ref, st_scr):
        step = pl.program_id(0)

        @pl.when(step == 0)
        def _():
            st_scr[...] = jnp.zeros_like(st_scr)

        lb = _lower_bound(lbl_ref[...])
        r64 = lax.broadcasted_iota(jnp.int32, (CHUNK, CHUNK), 0)
        c64 = lax.broadcasted_iota(jnp.int32, (CHUNK, CHUNK), 1)
        tril = r64 >= c64
        tri = tril.astype(BF16)
        st = st_scr[...]
        for cc in range(cpb):
            sl = slice(cc * CHUNK, (cc + 1) * CHUNK)
            _, _, qf, _, _, kf, v, b, _, bl, qt, kt = _hgrn_chunk(q_ref, f_ref, i_ref, sl, lb, tri)
            qe = (qf * jnp.exp(b)).astype(BF16)
            kh = (kf * jnp.exp(bl - b)).astype(BF16)
            qtb, ktb, vb, stb = qt.astype(BF16), kt.astype(BF16), v.astype(BF16), st.astype(BF16)

            def out_h(hs):
                a = jnp.where(tril, _dot(qtb[:, hs], ktb[:, hs], NT), 0.0).astype(BF16)
                return _dot(qe[:, hs], stb[:, hs], NT) + _dot(a, vb[:, hs], NN)

            o_ref[sl, :] = _heads(out_h)
            st_ref[cc] = stb
            st = st * jnp.exp(bl) + _heads(lambda hs: _dot(vb[:, hs], kh[:, hs], TN))
        st_scr[...] = st
        o = o_ref[...]
        gate = g_ref[...]
        rec_ref[...] = (o * lax.rsqrt(_head_mean(o * o) + RMS_EPS) * gain_ref[...] * (gate * _sigmoid(gate))).astype(BF16)

    row = pl.BlockSpec((tb, HGRN_W), lambda i: (i, 0))
    return pl.pallas_call(
        body, name="hgrn_fwd", grid=(nb,),
        in_specs=_hgrn_specs(tb, lambda i: i) + [pl.BlockSpec((2, HGRN_W), lambda i: (0, 0)),
                                                 pl.BlockSpec((1, HGRN_W), lambda i: (0, 0))],
        out_specs=[row, row, pl.BlockSpec((cpb, BLK, HGRN_W), lambda i: (i, 0, 0))],
        out_shape=[_sds((s, HGRN_W), F32), _sds((s, HGRN_W), BF16), _sds((nc, BLK, HGRN_W), BF16)],
        scratch_shapes=[pltpu.VMEM((BLK, HGRN_W), F32)],
        compiler_params=_cp(("arbitrary",)),
    )(proj, proj, proj, proj, lb_logits, out_gain)


def _hgrn_bwd(proj, o_pre, states, dcat, lb_logits, out_gain):
    s = proj.shape[0]
    tb = min(HGRN_TB, s)
    nb, cpb, nc = s // tb, tb // CHUNK, s // CHUNK

    def body(q_ref, f_ref, i_ref, g_ref, o_ref, st_ref, stn_ref, dy_ref, lbl_ref, gain_ref,
             dq_ref, df_ref, di_ref, dg_ref, dgain_ref, dlbl_ref, do_scr, dst_scr, dlb_scr):
        step = pl.program_id(0)

        @pl.when(step == 0)
        def _():
            dst_scr[...] = jnp.zeros_like(dst_scr)
            dlb_scr[...] = jnp.zeros_like(dlb_scr)
            dgain_ref[...] = jnp.zeros_like(dgain_ref)

        lb = _lower_bound(lbl_ref[...])
        gain = gain_ref[...]
        o = o_ref[...]
        r = lax.rsqrt(_head_mean(o * o) + RMS_EPS)
        nrm = o * r
        gate = g_ref[...]
        sgt = _sigmoid(gate)
        dy = dy_ref[...]
        dg_ref[...] = (dy * nrm * gain * (sgt * (1.0 + gate * (1.0 - sgt)))).astype(BF16)
        dng = dy * (gate * sgt)
        dgain_ref[...] += _colsum(dng * nrm)
        dn = dng * gain
        do_scr[...] = r * (dn - nrm * _head_mean(dn * nrm))

        r64 = lax.broadcasted_iota(jnp.int32, (CHUNK, CHUNK), 0)
        c64 = lax.broadcasted_iota(jnp.int32, (CHUNK, CHUNK), 1)
        tril = r64 >= c64
        tri = tril.astype(BF16)
        triu = (r64 <= c64).astype(BF16)
        dst = dst_scr[...]
        dlb = dlb_scr[...]
        for cc in reversed(range(cpb)):
            sl = slice(cc * CHUNK, (cc + 1) * CHUNK)
            qp, sq, qf, sg, f, kf, v, b, bm, bl, qt, kt = _hgrn_chunk(q_ref, f_ref, i_ref, sl, lb, tri)
            stf = st_ref[cc].astype(F32)
            st_end = (st_ref[cc + 1] if cc + 1 < cpb else stn_ref[0]).astype(F32)
            csum = jnp.sum(st_end * dst, axis=0, keepdims=True)
            doc = do_scr[sl, :]
            dob, dstb = doc.astype(BF16), dst.astype(BF16)
            eb = jnp.exp(b)
            qe = (qf * eb).astype(BF16)
            kh = (kf * jnp.exp(bl - b)).astype(BF16)
            qtb, ktb = qt.astype(BF16), kt.astype(BF16)
            parts = []
            for h in range(HGRN_W // BLK):
                hs = slice(h * BLK, (h + 1) * BLK)
                da = jnp.where(tril, _dot3(doc[:, hs], v[:, hs], NT), 0.0)
                a = jnp.where(tril, _dot(qtb[:, hs], ktb[:, hs], NT), 0.0).astype(BF16)
                parts.append((
                    _dot3(da, kt[:, hs], NN), _dot3(doc[:, hs], stf[:, hs], NN),
                    _dot3(da, qt[:, hs], TN), _dot3(v[:, hs], dst[:, hs], NN),
                    _dot(a, dob[:, hs], TN) + _dot(kh[:, hs], dstb[:, hs], NT),
                    _dot(dob[:, hs], qe[:, hs], TN)))
            dqt, dqi, dkt, dks, dv, upd = (jnp.concatenate([p[n] for p in parts], axis=1) for n in range(6))
            dqf = dqt * jnp.exp(b - bm) + eb * dqi
            dkf = dkt * jnp.exp(bm - b) + jnp.exp(bl - b) * dks
            gq = qf * dqf - kf * dkf
            dlogf = csum + _cumsum_rows(triu, gq)
            dfv = dlogf / f - dkf
            dq_ref[sl, :] = (dqf * (sq * (1.0 + qp * (1.0 - sq)))).astype(BF16)
            df_ref[sl, :] = (dfv * (1.0 - lb) * sg * (1.0 - sg)).astype(BF16)
            di_ref[sl, :] = dv.astype(BF16)
            dst = dst * jnp.exp(bl) + upd
            dlb = dlb + _colsum(dfv * (1.0 - sg))
        dst_scr[...] = dst
        dlb_scr[...] = dlb

        @pl.when(step == nb - 1)
        def _():
            t = dlb * lb * (1.0 - lb)
            dlbl_ref[...] = jnp.concatenate([t, -t], axis=0)

    rev = lambda i: nb - 1 - i
    row = pl.BlockSpec((tb, HGRN_W), lambda i: (rev(i), 0))
    res = pl.pallas_call(
        body, name="hgrn_bwd", grid=(nb,),
        in_specs=_hgrn_specs(tb, rev) + [
            row, pl.BlockSpec((cpb, BLK, HGRN_W), lambda i: (rev(i), 0, 0)),
            pl.BlockSpec((1, BLK, HGRN_W), lambda i: (jnp.minimum((rev(i) + 1) * cpb, nc - 1), 0, 0)),
            pl.BlockSpec((tb, HGRN_W), lambda i: (rev(i), ATTN_W // HGRN_W)),
            pl.BlockSpec((2, HGRN_W), lambda i: (0, 0)), pl.BlockSpec((1, HGRN_W), lambda i: (0, 0))],
        out_specs=[row, row, row, row, pl.BlockSpec((1, HGRN_W), lambda i: (0, 0)),
                   pl.BlockSpec((2, HGRN_W), lambda i: (0, 0))],
        out_shape=[_sds((s, HGRN_W), BF16)] * 4 + [_sds((1, HGRN_W), F32), _sds((2, HGRN_W), F32)],
        scratch_shapes=[pltpu.VMEM((tb, HGRN_W), F32), pltpu.VMEM((BLK, HGRN_W), F32), pltpu.VMEM((1, HGRN_W), F32)],
        compiler_params=_cp(("arbitrary",)),
    )(proj, proj, proj, proj, o_pre, states, states, dcat, lb_logits, out_gain)
    return res


def _place():
    return lax.axis_index("x"), lax.axis_index("y"), lax.axis_index("c")


def _flip(x, y, ox, oy):
    return (1 - x if ox else x), (1 - y if oy else y)


def _half(rows, cc):
    return pl.ds(cc * (rows // 2), rows // 2)


def _remote(src, dst, ssem, rsem, to):
    return pltpu.make_async_remote_copy(src_ref=src, dst_ref=dst, send_sem=ssem, recv_sem=rsem,
                                        device_id=to, device_id_type=MESH)


def _ag_chip_copies(ins, outs, ssem, rsem):
    x, y, c = _place()
    j = 2 * x + y
    cps = []
    for k in range(len(ins)):
        rows = ins[k].shape[0]
        for idx, (ox, oy) in enumerate(FLIPS):
            px, py = _flip(x, y, ox, oy)
            cps.append(_remote(ins[k].at[_half(rows, c)], outs[k].at[j, _half(rows, c)],
                               ssem.at[k, idx], rsem.at[k, idx], (px, py, c)))
    return cps


def _ag_start(ins, outs, ssem, rsem):
    for cp in _ag_chip_copies(ins, outs, ssem, rsem):
        cp.start()


def _ag_finish(ins, outs, ssem, rsem):
    x, y, c = _place()
    sib = (x, y, 1 - c)
    passed = []
    for k in range(len(ins)):
        rows = ins[k].shape[0]
        for idx, (ox, oy) in enumerate(FLIPS):
            px, py = _flip(x, y, ox, oy)
            blk = outs[k].at[2 * px + py, _half(rows, c)]
            _remote(blk, blk, ssem.at[k, idx], rsem.at[k, idx], (px, py, c)).wait_recv()
            cp = _remote(blk, blk, ssem.at[k, 3 + idx], rsem.at[k, 3 + idx], sib)
            cp.start()
            passed.append(cp)
    for k in range(len(ins)):
        rows = ins[k].shape[0]
        for idx, (ox, oy) in enumerate(FLIPS):
            px, py = _flip(x, y, ox, oy)
            blk = outs[k].at[2 * px + py, _half(rows, 1 - c)]
            _remote(blk, blk, ssem.at[k, 3 + idx], rsem.at[k, 3 + idx], sib).wait_recv()
    for cp in _ag_chip_copies(ins, outs, ssem, rsem) + passed:
        cp.wait_send()


def _ag_shapes(shards):
    nk = len(shards)
    return ([_sds((N_CHIPS,) + tuple(w.shape), w.dtype) for w in shards],
            [pltpu.SemaphoreType.DMA((nk, 6)), pltpu.SemaphoreType.DMA((nk, 6))])


def _with_own(gathered, shard, j):
    return lax.dynamic_update_index_in_dim(gathered, shard, j, 0)


def _ag_weights(name, shards):
    nk = len(shards)

    def body(*refs):
        ins, outs = refs[:nk], refs[nk:2 * nk]
        ssem, rsem = refs[2 * nk:]
        _ag_start(ins, outs, ssem, rsem)
        _ag_finish(ins, outs, ssem, rsem)

    out_shape, sems = _ag_shapes(shards)
    return pl.pallas_call(body, name=name, in_specs=[ANY] * nk, out_specs=[ANY] * nk, out_shape=out_shape,
                          scratch_shapes=sems)(*shards)


def _rs_pair(name, grads):
    nk = len(grads)

    def body(*refs):
        ins, outs = refs[:nk], refs[nk:2 * nk]
        ssem, rsem = refs[2 * nk:]
        x, y, c = _place()
        cps = []
        for k in range(nk):
            rows = grads[k].shape[1]
            cp = pltpu.make_async_remote_copy(src_ref=ins[k].at[:, _half(rows, 1 - c)], dst_ref=outs[k],
                                              send_sem=ssem.at[k], recv_sem=rsem.at[k],
                                              device_id=(x, y, 1 - c), device_id_type=MESH)
            cp.start()
            cps.append(cp)
        for cp in cps:
            cp.wait()

    return pl.pallas_call(
        body, name=name, in_specs=[ANY] * nk, out_specs=[ANY] * nk,
        out_shape=[_sds((N_CHIPS, g.shape[1] // 2, g.shape[2]), g.dtype) for g in grads],
        scratch_shapes=[pltpu.SemaphoreType.DMA((nk,)), pltpu.SemaphoreType.DMA((nk,))],
    )(*grads)


def _rs_chip_copies(ins, outs, ssem, rsem):
    x, y, c = _place()
    cps = []
    for k in range(len(ins)):
        for idx, (ox, oy) in enumerate(FLIPS):
            px, py = _flip(x, y, ox, oy)
            cps.append(_remote(ins[k].at[2 * px + py], outs[k].at[idx], ssem.at[k, idx], rsem.at[k, idx], (px, py, c)))
    return cps


def _rs_chips_shapes(psums):
    nk = len(psums)
    return ([_sds((3,) + tuple(p.shape[1:]), p.dtype) for p in psums],
            [pltpu.SemaphoreType.DMA((nk, 3)), pltpu.SemaphoreType.DMA((nk, 3))])


def _rs_share(fulls):
    nk = len(fulls)

    def body(*refs):
        ins, outs = refs[:nk], refs[nk:2 * nk]
        ssem, rsem = refs[2 * nk:]
        x, y, c = _place()
        cps = []
        for k in range(nk):
            rows = fulls[k].shape[0]
            cp = _remote(ins[k].at[_half(rows, c)], outs[k].at[_half(rows, c)], ssem.at[k], rsem.at[k], (x, y, 1 - c))
            cp.start()
            cps.append(cp)
        for k, cp in enumerate(cps):
            rows = fulls[k].shape[0]
            cp.wait_send()
            theirs = outs[k].at[_half(rows, 1 - c)]
            _remote(theirs, theirs, ssem.at[k], rsem.at[k], (x, y, 1 - c)).wait_recv()

    return pl.pallas_call(
        body, name="rs_share", in_specs=[ANY] * nk, out_specs=[ANY] * nk,
        out_shape=[_sds(f.shape, f.dtype) for f in fulls], input_output_aliases={k: k for k in range(nk)},
        scratch_shapes=[pltpu.SemaphoreType.DMA((nk,)), pltpu.SemaphoreType.DMA((nk,))],
    )(*fulls)


def _allreduce_small(v):
    ndev = 8

    def body(in_ref, out_ref, buf, ssem, rsem):
        x, y, c = _place()
        me = 4 * x + 2 * y + c
        buf[me] = in_ref[...]
        cps = []
        for k in range(1, ndev):
            ox, oy, oc = (k >> 2) & 1, (k >> 1) & 1, k & 1
            px, py = _flip(x, y, ox, oy)
            pc = 1 - c if oc else c
            cp = pltpu.make_async_remote_copy(src_ref=in_ref, dst_ref=buf.at[me], send_sem=ssem.at[k - 1],
                                              recv_sem=rsem.at[k - 1], device_id=(px, py, pc), device_id_type=MESH)
            cp.start()
            cps.append((cp, 4 * px + 2 * py + pc, (px, py, pc)))
        for k, (cp, src, peer) in enumerate(cps):
            cp.wait_send()
            pltpu.make_async_remote_copy(src_ref=in_ref, dst_ref=buf.at[src], send_sem=ssem.at[k],
                                         recv_sem=rsem.at[k], device_id=peer, device_id_type=MESH).wait_recv()
        acc = buf[0]
        for i in range(1, ndev):
            acc = acc + buf[i]
        out_ref[...] = acc

    return pl.pallas_call(
        body, name="allreduce_small",
        in_specs=[pl.BlockSpec(memory_space=pltpu.VMEM)], out_specs=pl.BlockSpec(memory_space=pltpu.VMEM),
        out_shape=_sds(v.shape, v.dtype),
        scratch_shapes=[pltpu.VMEM((ndev,) + v.shape, v.dtype), pltpu.SemaphoreType.DMA((ndev - 1,)),
                        pltpu.SemaphoreType.DMA((ndev - 1,))],
    )(v)


def _rs_sum1(name, g, recv, c_idx):
    _, r, cdim = g.shape
    hr = r // 2
    tr = min(hr, 256)
    nr = hr // tr

    def body(c_ref, g_ref, r_ref, o32_ref, o16_ref):
        v = g_ref[...] + r_ref[...].astype(F32)
        o32_ref[...] = v
        o16_ref[...] = v.astype(BF16)

    spec = pl.BlockSpec((None, tr, cdim), lambda j, i, c_ref: (j, i, 0))
    return pl.pallas_call(
        body, name=name,
        grid_spec=pltpu.PrefetchScalarGridSpec(
            num_scalar_prefetch=1, grid=(N_CHIPS, nr),
            in_specs=[pl.BlockSpec((None, tr, cdim), lambda j, i, c_ref: (j, c_ref[0] * nr + i, 0)), spec],
            out_specs=[spec, spec]),
        out_shape=[_sds((N_CHIPS, hr, cdim), F32), _sds((N_CHIPS, hr, cdim), BF16)],
        compiler_params=_cp(("parallel", "parallel")),
    )(c_idx, g, recv)


def _rs_sum2(name, p32, recv, jc_idx):
    _, hr, cdim = p32.shape
    tr = min(hr, 256)
    nr = hr // tr

    def body(jc_ref, p_ref, r_ref, o_ref):
        o_ref[...] = ((p_ref[...] + r_ref[0].astype(F32)) + r_ref[1].astype(F32)) + r_ref[2].astype(F32)

    return pl.pallas_call(
        body, name=name,
        grid_spec=pltpu.PrefetchScalarGridSpec(
            num_scalar_prefetch=1, grid=(nr,),
            in_specs=[pl.BlockSpec((None, tr, cdim), lambda i, jc: (jc[0], i, 0)),
                      pl.BlockSpec((3, tr, cdim), lambda i, jc: (0, i, 0))],
            out_specs=pl.BlockSpec((tr, cdim), lambda i, jc: (jc[1] * nr + i, 0))),
        out_shape=_sds((2 * hr, cdim), F32),
        compiler_params=_cp(("parallel",)),
    )(jc_idx, p32, recv)


def _adamw(name, w, g, m, v):
    r, cdim = w.shape
    tr = min(r, 256)
    c1 = 1.0 - ADAM_B1 ** ADAM_STEP
    c2 = 1.0 - ADAM_B2 ** ADAM_STEP

    def body(w_ref, g_ref, m_ref, v_ref, d_ref, nm_ref, nv_ref):
        gv = g_ref[...]
        nm = ADAM_B1 * m_ref[...] + (1.0 - ADAM_B1) * gv
        nv = ADAM_B2 * v_ref[...] + (1.0 - ADAM_B2) * (gv * gv)
        d_ref[...] = -ADAM_LR * ((nm / c1) / (jnp.sqrt(nv / c2) + ADAM_EPS) + ADAM_WD * w_ref[...])
        nm_ref[...] = nm
        nv_ref[...] = nv

    spec = pl.BlockSpec((tr, cdim), lambda i: (i, 0))
    return pl.pallas_call(
        body, name=name, grid=(r // tr,), in_specs=[spec] * 4, out_specs=[spec] * 3,
        out_shape=[_sds((r, cdim), F32)] * 3, compiler_params=_cp(("parallel",)),
    )(w, g, m, v)


def _pack_small(mix_pre, attn_out, lb_logits, hgrn_out, mix_post, mlp_pre, mlp_post):
    rows = [mix_pre, jnp.concatenate([attn_out, hgrn_out], axis=1),
            jnp.concatenate([lb_logits[0:1], lb_logits[1:2]], axis=1), mix_post, mlp_pre, mlp_post,
            jnp.zeros((2, D_MODEL), F32)]
    return jnp.concatenate(rows, axis=0)


def _unpack_small(p):
    return (p[0:1], p[1:2, :ATTN_W], jnp.concatenate([p[2:3, :HGRN_W], p[2:3, HGRN_W:]], axis=0),
            p[1:2, ATTN_W:], p[3:4], p[4:5], p[5:6])


def kernel(x, mix_pre_norm, w_in, attn_out_norm, hgrn_lb_logits, hgrn_out_norm, w_out, mix_post_norm, mlp_pre_norm, w_ff1, w_ff2, mlp_post_norm, loss_target, m_mix_pre_norm, m_w_in, m_attn_out_norm, m_hgrn_lb_logits, m_hgrn_out_norm, m_w_out, m_mix_post_norm, m_mlp_pre_norm, m_w_ff1, m_w_ff2, m_mlp_post_norm, v_mix_pre_norm, v_w_in, v_attn_out_norm, v_hgrn_lb_logits, v_hgrn_out_norm, v_w_out, v_mix_post_norm, v_mlp_pre_norm, v_w_ff1, v_w_ff2, v_mlp_post_norm):
    s = x.shape[1]
    xs = x.reshape(s, D_MODEL)
    tgt = loss_target.reshape(s, D_MODEL)
    cx, cy, cc = _place()
    chip = 2 * cx + cy
    c_idx = jnp.reshape(cc, (1,)).astype(jnp.int32)
    jc_idx = jnp.stack([chip, cc]).astype(jnp.int32)

    big_w = [w_in[0], w_out[0], w_ff1[0], w_ff2[0]]
    big_m = [m_w_in[0], m_w_out[0], m_w_ff1[0], m_w_ff2[0]]
    big_v = [v_w_in[0], v_w_out[0], v_w_ff1[0], v_w_ff2[0]]
    shards = [w.astype(BF16) for w in big_w]

    (wg_in,) = _ag_weights("ag_in", shards[:1])
    wg_in = _with_own(wg_in, shards[0], chip)

    (h,) = _rows_call("norm_in", lambda xv, g: ((xv * _rstd(xv) * g),),
                      [(xs, _row(D_MODEL)), (mix_pre_norm, "full")], [(D_MODEL, BF16, "row")], s)
    (proj,) = _mm_cols("mm_proj", h, wg_in, NN, [F32])
    hg_o, rec, states = _hgrn_fwd(proj, hgrn_lb_logits, hgrn_out_norm)
    attn_o, attn_lse, wg_out, wg_1, wg_2 = _attn_fwd(proj, shards[1:])
    wg_out, wg_1, wg_2 = (_with_own(g, w, chip) for g, w in zip((wg_out, wg_1, wg_2), shards[1:]))
    (attn_n,) = _rows_call("attn_norm", lambda o, gain: (o * _rstd(o) * gain,),
                           [(attn_o, _row(ATTN_W)), (attn_out_norm, "full")], [(ATTN_W, BF16, "row")], s)
    cat = jnp.concatenate([attn_n, rec], axis=1)

    def post1(mv, xv, g_post, g_pre2):
        x1 = xv + mv * _rstd(mv) * g_post
        return mv, x1, x1 * _rstd(x1) * g_pre2

    mixed, x1, h2 = _rows_call(
        "mm_mixed", post1, [(xs, _row(D_MODEL)), (mix_post_norm, "full"), (mlp_pre_norm, "full")],
        [(D_MODEL, F32, "row"), (D_MODEL, F32, "row"), (D_MODEL, BF16, "row")], s,
        matmul=(cat, wg_out.reshape(D_MODEL, D_MODEL)))

    def sq_relu(u):
        r = jnp.maximum(u, 0.0)
        return r * r, r

    act, ru = _mm_cols("mm_ff1", h2, wg_1, NN, [BF16, BF16], epi=sq_relu)

    def post2(fv, x1v, tv, g):
        y = x1v + fv * _rstd(fv) * g
        dy = (y - tv) * (1.0 / D_MODEL)
        err = y - tv
        loss = 0.5 * jnp.sum(jnp.mean(err * err, axis=-1, keepdims=True), axis=0, keepdims=True)
        dff, dgc = _norm_bwd(fv, g, dy)
        return dy, dff, _colsum(dgc), jnp.broadcast_to(loss, (1, BLK))

    dy, dff, g_mlp_post, loss_part = _rows_call(
        "mm_ff2", post2, [(x1, _row(D_MODEL)), (tgt, _row(D_MODEL)), (mlp_post_norm, "full")],
        [(D_MODEL, F32, "row"), (D_MODEL, BF16, "row"), (D_MODEL, F32, "acc"), (BLK, F32, "acc")], s,
        tm=ROW_TILE // 2, matmul=(act, wg_2.reshape(D_FF, D_MODEL)))

    (du,) = _mm_cols("mm_du", dff, wg_2, NT, [BF16], epi=lambda acc, r: (acc * (2.0 * r.astype(F32)),),
                     extras=(ru,))
    gw_2 = _mm_wgrad("mm_gw2", act, dff, True)
    gw_1 = _mm_wgrad("mm_gw1", h2, du, False)

    def bwd_mid(dh2v, dyv, x1v, mv, g_pre2, g_post):
        d1, gc1 = _norm_bwd(x1v, g_pre2, dh2v)
        dx1 = dyv + d1
        dm, gc2 = _norm_bwd(mv, g_post, dx1)
        return dx1, dm, _colsum(gc1), _colsum(gc2)

    w1_t = jnp.transpose(wg_1, (0, 2, 1)).reshape(D_FF, D_MODEL)
    dx1, dmixed, g_mlp_pre, g_mix_post = _rows_call(
        "mm_dh2", bwd_mid, [(dy, _row(D_MODEL)), (x1, _row(D_MODEL)), (mixed, _row(D_MODEL)),
                            (mlp_pre_norm, "full"), (mix_post_norm, "full")],
        [(D_MODEL, F32, "row"), (D_MODEL, BF16, "row"), (D_MODEL, F32, "acc"), (D_MODEL, F32, "acc")], s,
        tm=ROW_TILE // 2, matmul=(du, w1_t))

    (dcat,) = _mm_cols("mm_dcat", dmixed, wg_out, NT, [F32])
    gw_out = _mm_wgrad("mm_gwout", cat, dmixed, True)

    names = ["out", "ff1", "ff2", "in"]
    ready = [gw_out, gw_1, gw_2]
    from_pair = _rs_pair("rs_pair_3", [g[1] for g in ready])
    pair = [_rs_sum1(f"rs_sum1_{n}", g[0], r, c_idx) for n, g, r in zip(names, ready, from_pair)]

    def attn_norm_bwd(dc, o, gain):
        do, gc = _norm_bwd(o, gain, dc)
        t = do * o
        lane = lax.broadcasted_iota(jnp.int32, (t.shape[0], BLK), 1) < 64
        parts = []
        for p in range(ATTN_W // BLK):
            tp = t[:, p * BLK:(p + 1) * BLK]
            sa = jnp.sum(jnp.where(lane, tp, 0.0), axis=1, keepdims=True)
            sb = jnp.sum(jnp.where(lane, 0.0, tp), axis=1, keepdims=True)
            parts.append(jnp.where(lane, sa, sb))
        return do, jnp.concatenate(parts, axis=1), _colsum(gc)

    do_attn, delta, g_attn_out = _rows_call(
        "attn_norm_bwd", attn_norm_bwd, [(dcat, _row(ATTN_W, 0)), (attn_o, _row(ATTN_W)), (attn_out_norm, "full")],
        [(ATTN_W, F32, "row"), (ATTN_W, F32, "row"), (ATTN_W, F32, "acc")], s)
    dq, dk, dv, *from_chips = _attn_bwd(proj, do_attn, attn_lse, delta, [p[1] for p in pair])
    dhq, dhf, dhi, dhg, g_hgrn_out, g_lb = _hgrn_bwd(proj, hg_o, states, dcat, hgrn_lb_logits, hgrn_out_norm)

    def dproj_asm(*a):
        return (jnp.concatenate([t.astype(BF16) for t in a], axis=1),)

    (dproj,) = _rows_call("dproj_asm", dproj_asm,
                          [(t, _row(ATTN_W)) for t in (dq, dk, dv)] + [(t, _row(HGRN_W)) for t in (dhq, dhf, dhi, dhg)],
                          [(PROJ_W, BF16, "row")], s)
    gw_in = _mm_wgrad("mm_gwin", h, dproj, False)
    (from_pair_in,) = _rs_pair("rs_pair_in", [gw_in[1]])
    pair.append(_rs_sum1("rs_sum1_in", gw_in[0], from_pair_in, c_idx))
    rs_shape, rs_sems = _rs_chips_shapes([pair[3][1]])

    def bwd_in(dhv, dx1v, xv, g):
        d0, gc = _norm_bwd(xv, g, dhv)
        return dx1v + d0, _colsum(gc)

    w_in_t = jnp.transpose(wg_in, (0, 2, 1)).reshape(PROJ_W, D_MODEL)
    grad_x, g_mix_pre, from_chips_in = _rows_call(
        "mm_dh", bwd_in, [(dx1, _row(D_MODEL)), (xs, _row(D_MODEL)), (mix_pre_norm, "full")],
        [(D_MODEL, F32, "row"), (D_MODEL, F32, "acc")], s, tm=ROW_TILE // 2, matmul=(dproj, w_in_t),
        exchange=([pair[3][1]], rs_shape, rs_sems, _rs_chip_copies))
    from_chips.append(from_chips_in)

    loss = lax.psum(loss_part[0, 0], ("x", "y", "c"))
    small_g = _allreduce_small(_pack_small(g_mix_pre, g_attn_out, g_lb, g_hgrn_out, g_mix_post, g_mlp_pre, g_mlp_post))

    reduced = [_rs_sum2(f"rs_sum2_{n}", p[0], r, jc_idx) for n, p, r in zip(names, pair, from_chips)]
    g_wout, g_w1, g_w2, g_win = _rs_share(reduced)
    full = [g_win, g_wout, g_w1, g_w2]

    upd = [_adamw(f"adamw_{n}", w, g, m, v) for n, w, g, m, v in zip(("in", "out", "ff1", "ff2"), big_w, full, big_m, big_v)]
    small_w = _pack_small(mix_pre_norm, attn_out_norm, hgrn_lb_logits, hgrn_out_norm, mix_post_norm, mlp_pre_norm,
                          mlp_post_norm)
    small_m = _pack_small(m_mix_pre_norm, m_attn_out_norm, m_hgrn_lb_logits, m_hgrn_out_norm, m_mix_post_norm,
                          m_mlp_pre_norm, m_mlp_post_norm)
    small_v = _pack_small(v_mix_pre_norm, v_attn_out_norm, v_hgrn_lb_logits, v_hgrn_out_norm, v_mix_post_norm,
                          v_mlp_pre_norm, v_mlp_post_norm)
    small_upd = _adamw("adamw_small", small_w, small_g, small_m, small_v)

    def assemble(small, big):
        sm = _unpack_small(small)
        return (sm[0], big[0][None], sm[1], sm[2], sm[3], big[1][None], sm[4], sm[5], big[2][None], big[3][None], sm[6])

    g_out = assemble(small_g, full)
    d_out = assemble(small_upd[0], [u[0] for u in upd])
    m_out = assemble(small_upd[1], [u[1] for u in upd])
    v_out = assemble(small_upd[2], [u[2] for u in upd])
    return (loss, grad_x.reshape(x.shape), *g_out, *d_out, *m_out, *v_out)
```

```python
import numpy as np
import jax
import jax.numpy as jnp
from jax import lax
from jax.experimental import pallas as pl
from jax.experimental.pallas import tpu as pltpu

F32 = jnp.float32
BF16 = jnp.bfloat16
MESH = pl.DeviceIdType.MESH
ANY = pl.BlockSpec(memory_space=pl.ANY)

RMS_EPS = 1e-6
D_MODEL = 1024
ATTN_W = 512
HGRN_W = 512
PROJ_W = 3584
D_FF = 4096
N_CHIPS = 4
BLK = 128
CHUNK = 64
HGRN_TB = 512
ATTN_GROUP = 8
DILATIONS = (1, 4, 16)
ATTN_SCALE = 0.125
ROW_TILE = 512
MM_TILE = 1024
VMEM_LIMIT = 48 * 2 ** 20
FLIPS = ((1, 0), (0, 1), (1, 1))

ADAM_LR, ADAM_B1, ADAM_B2, ADAM_EPS, ADAM_WD, ADAM_STEP = 0.001, 0.9, 0.999, 1e-08, 0.01, 10


def _cp(sem=None):
    return pltpu.CompilerParams(dimension_semantics=sem, vmem_limit_bytes=VMEM_LIMIT)


def _sigmoid(v):
    return 1.0 / (1.0 + jnp.exp(-v))


def _dot(a, b, contract, precision=None):
    return lax.dot_general(a, b, (contract, ((), ())), preferred_element_type=F32, precision=precision)


NN = ((1,), (0,))
NT = ((1,), (1,))
TN = ((0,), (0,))


def _matmul(name, a, b, *, grid, a_spec, b_spec, contract, outs, epi=None, extras=(), extra_specs=()):
    n_ex = len(extras)

    def body(a_ref, b_ref, *rest):
        ex, out_refs = rest[:n_ex], rest[n_ex:]
        acc = _dot(a_ref[...], b_ref[...], contract)
        res = epi(acc, *[e[...] for e in ex]) if epi else (acc,)
        for o, r in zip(out_refs, res):
            o[...] = r.astype(o.dtype)

    return pl.pallas_call(
        body, name=name, grid=grid,
        in_specs=[a_spec, b_spec, *extra_specs],
        out_specs=[s for _, s in outs],
        out_shape=[s for s, _ in outs],
        compiler_params=_cp(("parallel",) * len(grid)),
    )(a, b, *extras)


def _sds(shape, dtype):
    return jax.ShapeDtypeStruct(shape, dtype)


def _mm_cols(name, a, w, contract, out_dtypes, epi=None, extras=()):
    m, k = a.shape
    jn = w.shape[0]
    nj = w.shape[2] if contract == NN else w.shape[1]
    tm = min(m, MM_TILE)
    outs = [(_sds((m, jn * nj), dt), pl.BlockSpec((tm, nj), lambda j, i: (i, j))) for dt in out_dtypes]
    return _matmul(name, a, w, grid=(jn, m // tm),
                   a_spec=pl.BlockSpec((tm, k), lambda j, i: (i, 0)),
                   b_spec=pl.BlockSpec((None,) + w.shape[1:], lambda j, i: (j, 0, 0)),
                   contract=contract, outs=outs, epi=epi, extras=extras,
                   extra_specs=[pl.BlockSpec((tm, nj), lambda j, i: (i, j)) for _ in extras])


def _mm_wgrad(name, a, b, a_by_j):
    s = a.shape[0]
    if a_by_j:
        r, c = a.shape[1] // N_CHIPS, b.shape[1]
    else:
        r, c = a.shape[1], b.shape[1] // N_CHIPS
    tr = min(r, 512)
    nr = r // tr
    if a_by_j:
        a_spec = pl.BlockSpec((s, tr), lambda j, i: (0, j * nr + i))
        b_spec = pl.BlockSpec((s, c), lambda j, i: (0, 0))
    else:
        a_spec = pl.BlockSpec((s, tr), lambda j, i: (0, i))
        b_spec = pl.BlockSpec((s, c), lambda j, i: (0, j))
    outs = [(_sds((N_CHIPS, r, c), dt), pl.BlockSpec((None, tr, c), lambda j, i: (j, i, 0))) for dt in (F32, BF16)]
    return _matmul(name, a, b, grid=(N_CHIPS, nr), a_spec=a_spec, b_spec=b_spec, contract=TN, outs=outs,
                   epi=lambda acc: (acc, acc))


def _rows_call(name, fn, ins, outs, s, tm=ROW_TILE, matmul=None, exchange=None):
    in_specs = []
    if matmul:
        a, w = matmul
        in_specs += [pl.BlockSpec((tm, a.shape[1]), lambda i: (i, 0)), pl.BlockSpec(w.shape, lambda i: (0, 0))]
    for arr, kind in ins:
        if kind == "full":
            in_specs.append(pl.BlockSpec(arr.shape, lambda i: (0, 0)))
        else:
            _, w_, cb = kind
            in_specs.append(pl.BlockSpec((tm, w_), lambda i, cb=cb: (i, cb)))
    out_specs, out_shape, is_acc = [], [], []
    for w_, dt, kind in outs:
        if kind == "acc":
            out_specs.append(pl.BlockSpec((1, w_), lambda i: (0, 0)))
            out_shape.append(_sds((1, w_), dt))
        else:
            out_specs.append(pl.BlockSpec((tm, w_), lambda i: (i, 0)))
            out_shape.append(_sds((s, w_), dt))
        is_acc.append(kind == "acc")
    n_mm, n_in, n_out = (2 if matmul else 0), len(ins), len(outs)
    x_ins, x_shapes, x_sems, x_start, x_finish = exchange if exchange else ((), [], [], None, None)
    n_x = len(x_ins)
    steps = s // tm

    def body(*refs):
        in_refs, xi = refs[n_mm:n_mm + n_in], refs[n_mm + n_in:n_mm + n_in + n_x]
        out_refs = refs[n_mm + n_in + n_x:n_mm + n_in + n_x + n_out]
        xo, sems = refs[n_mm + n_in + n_x + n_out:n_mm + n_in + 2 * n_x + n_out], refs[n_mm + n_in + 2 * n_x + n_out:]
        i = pl.program_id(0)

        if exchange:
            @pl.when(i == 0)
            def _():
                x_start(xi, xo, *sems)

        args = [r[...] for r in in_refs]
        if matmul:
            args.insert(0, _dot(refs[0][...], refs[1][...], NN))
        res = fn(*args)
        for o, r, acc in zip(out_refs, res, is_acc):
            if acc:
                @pl.when(i == 0)
                def _(o=o):
                    o[...] = jnp.zeros_like(o)
                o[...] += r.astype(o.dtype)
            else:
                o[...] = r.astype(o.dtype)

        if exchange:
            @pl.when(i == steps - 1)
            def _():
                x_finish(xi, xo, *sems)

    sem = ("arbitrary",) if any(is_acc) or exchange else ("parallel",)
    return pl.pallas_call(
        body, name=name, grid=(steps,), in_specs=in_specs + [ANY] * n_x, out_specs=out_specs + [ANY] * n_x,
        out_shape=out_shape + list(x_shapes), scratch_shapes=list(x_sems), compiler_params=_cp(sem),
    )(*(matmul or ()), *[a for a, _ in ins], *x_ins)


def _rstd(v):
    return lax.rsqrt(jnp.mean(v * v, axis=-1, keepdims=True) + RMS_EPS)


def _norm_bwd(v, gain, dy):
    r = _rstd(v)
    n = v * r
    dn = dy * gain
    dv = r * (dn - n * jnp.mean(dn * n, axis=-1, keepdims=True))
    return dv, dy * n


def _colsum(v):
    return jnp.sum(v, axis=0, keepdims=True)


def _row(w, cb=0):
    return ("row", w, cb)


N_PAIRS = ATTN_W // BLK


def _head_col(v, mask):
    return jnp.max(jnp.where(mask, v, -jnp.inf), axis=1, keepdims=True)


def _slopes():
    t = np.zeros((N_PAIRS, 8, 2 * BLK), np.float32)
    for p in range(N_PAIRS):
        for hh in range(2):
            t[p, hh, :] = 2.0 ** -(2 * p + hh + 1)
    return jnp.asarray(t)


def _rows(n, r, d):
    base = pl.multiple_of(n * (BLK * d), BLK)
    return pl.ds(base + r, BLK, stride=d) if d > 1 else pl.ds(base, BLK)


def _attn_groups(s, d):
    nb = s // (BLK * d)
    g = ATTN_GROUP
    if d >= g:
        return [(nb, lambda n, r0=r0: [(n, r0 + u) for u in range(g)]) for r0 in range(0, d, g)]
    per = g // d
    return [(nb // per, lambda t: [(per * t + u, r) for u in range(per) for r in range(d)])]


def _attn_fwd(proj, shards):
    s = proj.shape[0]
    nk = len(shards)

    def body(sl_ref, q_ref, k_ref, v_ref, *rest):
        w_refs, (o_ref, l_ref) = rest[:nk], rest[nk:nk + 2]
        wg_refs, (ssem, rsem) = rest[nk + 2:2 * nk + 2], rest[2 * nk + 2:]
        pair = pl.program_id(0)

        @pl.when(pair == 0)
        def _():
            _ag_start(w_refs, wg_refs, ssem, rsem)

        row = lax.broadcasted_iota(jnp.int32, (BLK, 2 * BLK), 0)
        col = lax.broadcasted_iota(jnp.int32, (BLK, 2 * BLK), 1)
        dist = row + BLK - col
        in_window = (dist >= 0) & (dist <= BLK)
        distf = dist.astype(F32)
        lane_q = lax.broadcasted_iota(jnp.int32, (BLK, BLK), 1) < 64
        lane_k = lax.broadcasted_iota(jnp.int32, (2 * BLK, BLK), 1) < 64

        def branch(n, r, d):
            rows = _rows(n, r, d)
            prev = _rows(jnp.maximum(n - 1, 0), r, d)
            valid = in_window & (col + n * BLK >= BLK)
            q2 = q_ref[rows, :]
            kk = jnp.concatenate([k_ref[prev, :], k_ref[rows, :]], axis=0).astype(BF16)
            vv = jnp.concatenate([v_ref[prev, :], v_ref[rows, :]], axis=0)
            o2 = jnp.zeros((BLK, BLK), F32)
            lse2 = jnp.zeros((BLK, BLK), F32)
            for hh in range(2):
                mq = lane_q if hh == 0 else ~lane_q
                mk = lane_k if hh == 0 else ~lane_k
                qm = jnp.where(mq, q2, 0.0).astype(BF16)
                sc = _dot(qm, kk, NT) * ATTN_SCALE - (sl_ref[hh:hh + 1, :] * float(d)) * distf
                sc = jnp.where(valid, sc, -1e30)
                m = jnp.max(sc, axis=1, keepdims=True)
                pr = jnp.exp(sc - m)
                den = jnp.sum(pr, axis=1, keepdims=True)
                vm = jnp.where(mk, vv, 0.0).astype(BF16)
                o2 = o2 + _dot(pr.astype(BF16), vm, NN) / den
                lse2 = jnp.where(mq, m + jnp.log(den), lse2)
            return rows, o2, lse2

        def merge(rows, o2, lse2, first):
            if first:
                o_ref[rows, :] = o2
                l_ref[rows, :] = lse2
            else:
                lo = l_ref[rows, :]
                mx = jnp.maximum(lo, lse2)
                ln = mx + jnp.log(jnp.exp(lo - mx) + jnp.exp(lse2 - mx))
                o_ref[rows, :] = jnp.exp(lo - ln) * o_ref[rows, :] + jnp.exp(lse2 - ln) * o2
                l_ref[rows, :] = ln

        for di, d in enumerate(DILATIONS):
            for trips, blocks in _attn_groups(s, d):
                def trip(t, carry, d=d, blocks=blocks, first=(di == 0)):
                    done = [branch(n, r, d) for n, r in blocks(t)]
                    for rows, o2, lse2 in done:
                        merge(rows, o2, lse2, first)
                    return carry

                lax.fori_loop(0, trips, trip, 0)

        @pl.when(pair == N_PAIRS - 1)
        def _():
            _ag_finish(w_refs, wg_refs, ssem, rsem)

    cb = lambda base: pl.BlockSpec((s, BLK), lambda p, base=base: (0, base + p))
    out = pl.BlockSpec((s, BLK), lambda p: (0, p))
    ag_shape, ag_sems = _ag_shapes(shards)
    return pl.pallas_call(
        body, name="attn_fwd", grid=(N_PAIRS,),
        in_specs=[pl.BlockSpec((None, 8, 2 * BLK), lambda p: (p, 0, 0)), cb(0), cb(N_PAIRS), cb(2 * N_PAIRS)]
        + [ANY] * nk,
        out_specs=[out, out] + [ANY] * nk, out_shape=[_sds((s, ATTN_W), F32)] * 2 + ag_shape,
        scratch_shapes=ag_sems, compiler_params=_cp(("arbitrary",)),
    )(_slopes(), proj, proj, proj, *shards)


def _attn_bwd(proj, do, lse, delta, psums):
    s = proj.shape[0]
    nk = len(psums)

    def body(sl_ref, q_ref, k_ref, v_ref, do_ref, l_ref, e_ref, *rest):
        p_refs, (dq_ref, dk_ref, dv_ref) = rest[:nk], rest[nk:nk + 3]
        got_refs, (ssem, rsem) = rest[nk + 3:2 * nk + 3], rest[2 * nk + 3:]
        pair = pl.program_id(0)

        @pl.when(pair == 0)
        def _():
            _rs_chips_start(p_refs, got_refs, ssem, rsem)

        row = lax.broadcasted_iota(jnp.int32, (BLK, 2 * BLK), 0)
        col = lax.broadcasted_iota(jnp.int32, (BLK, 2 * BLK), 1)
        dist = row + BLK - col
        in_window = (dist >= 0) & (dist <= BLK)
        distf = dist.astype(F32)
        lane_q = lax.broadcasted_iota(jnp.int32, (BLK, BLK), 1) < 64
        lane_k = lax.broadcasted_iota(jnp.int32, (2 * BLK, BLK), 1) < 64
        dk_ref[...] = jnp.zeros_like(dk_ref)
        dv_ref[...] = jnp.zeros_like(dv_ref)

        def branch(n, r, d):
            rows = _rows(n, r, d)
            prev = _rows(jnp.maximum(n - 1, 0), r, d)
            valid = in_window & (col + n * BLK >= BLK)
            q1, d1, l1, e1 = q_ref[rows, :], do_ref[rows, :], l_ref[rows, :], e_ref[rows, :]
            kk = jnp.concatenate([k_ref[prev, :], k_ref[rows, :]], axis=0)
            kkb = kk.astype(BF16)
            vvb = jnp.concatenate([v_ref[prev, :], v_ref[rows, :]], axis=0).astype(BF16)
            dq2 = jnp.zeros((BLK, BLK), F32)
            dkk = jnp.zeros((2 * BLK, BLK), F32)
            dvv = jnp.zeros((2 * BLK, BLK), F32)
            for hh in range(2):
                mq = lane_q if hh == 0 else ~lane_q
                mk = lane_k if hh == 0 else ~lane_k
                qm = jnp.where(mq, q1, 0.0).astype(BF16)
                dm = jnp.where(mq, d1, 0.0).astype(BF16)
                sc = _dot(qm, kkb, NT) * ATTN_SCALE - (sl_ref[hh:hh + 1, :] * float(d)) * distf
                pr = jnp.where(valid, jnp.exp(sc - _head_col(l1, mq)), 0.0)
                ds = (pr * (_dot(dm, vvb, NT) - _head_col(e1, mq))).astype(BF16)
                km = jnp.where(mk, kk, 0.0).astype(BF16)
                dq2 = dq2 + _dot(ds, km, NN) * ATTN_SCALE
                dkk = dkk + _dot(ds, qm, TN) * ATTN_SCALE
                dvv = dvv + _dot(pr.astype(BF16), dm, TN)
            return rows, prev, dq2, dkk, dvv

        for di, d in enumerate(DILATIONS):
            for trips, blocks in _attn_groups(s, d):
                def trip(t, carry, d=d, blocks=blocks, first=(di == 0)):
                    done = [branch(n, r, d) for n, r in blocks(t)]
                    for rows, prev, dq2, dkk, dvv in done:
                        dq_ref[rows, :] = dq2 if first else dq_ref[rows, :] + dq2
                        dk_ref[prev, :] = dk_ref[prev, :] + dkk[:BLK]
                        dk_ref[rows, :] = dk_ref[rows, :] + dkk[BLK:]
                        dv_ref[prev, :] = dv_ref[prev, :] + dvv[:BLK]
                        dv_ref[rows, :] = dv_ref[rows, :] + dvv[BLK:]
                    return carry

                lax.fori_loop(0, trips, trip, 0)

        @pl.when(pair == N_PAIRS - 1)
        def _():
            _rs_chips_finish(p_refs, got_refs, ssem, rsem)

    cb = lambda base: pl.BlockSpec((s, BLK), lambda p, base=base: (0, base + p))
    out = pl.BlockSpec((s, BLK), lambda p: (0, p))
    rs_shape, rs_sems = _rs_chips_shapes(psums)
    return pl.pallas_call(
        body, name="attn_bwd", grid=(N_PAIRS,),
        in_specs=[pl.BlockSpec((None, 8, 2 * BLK), lambda p: (p, 0, 0)), cb(0), cb(N_PAIRS), cb(2 * N_PAIRS),
                  out, out, out] + [ANY] * nk,
        out_specs=[out] * 3 + [ANY] * nk, out_shape=[_sds((s, ATTN_W), F32)] * 3 + rs_shape,
        scratch_shapes=rs_sems, compiler_params=_cp(("arbitrary",)),
    )(_slopes(), proj, proj, proj, do, lse, delta, *psums)


def _lower_bound(lbl):
    return 1.0 / (1.0 + jnp.exp(lbl[1:2, :] - lbl[0:1, :]))


def _hi(a):
    bits = lax.bitcast_convert_type(a, jnp.uint32) & jnp.uint32(0xFFFF0000)
    return lax.bitcast_convert_type(bits, F32)


def _dot3(a, b, contract):
    ah, bh = _hi(a), _hi(b)
    al, bl = (a - ah).astype(BF16), (b - bh).astype(BF16)
    ah, bh = ah.astype(BF16), bh.astype(BF16)
    return _dot(ah, bh, contract) + (_dot(ah, bl, contract) + _dot(al, bh, contract))


def _cumsum_rows(tri, g):
    g1 = _hi(g)
    r1 = g - g1
    g2 = _hi(r1)
    g3 = r1 - g2
    return _dot(tri, g1.astype(BF16), NN) + (_dot(tri, g2.astype(BF16), NN) + _dot(tri, g3.astype(BF16), NN))


def _heads(fn):
    return jnp.concatenate([fn(slice(h * BLK, (h + 1) * BLK)) for h in range(HGRN_W // BLK)], axis=1)


def _head_mean(t):
    return _heads(lambda hs: jnp.broadcast_to(jnp.mean(t[:, hs], axis=1, keepdims=True), (t.shape[0], BLK)))


def _hgrn_chunk(q_ref, f_ref, i_ref, sl, lb, tri):
    qp = q_ref[sl, :]
    sq = _sigmoid(qp)
    qf = qp * sq
    sg = _sigmoid(f_ref[sl, :])
    f = lb + (1.0 - lb) * sg
    kf = 1.0 - f
    v = i_ref[sl, :]
    b = _cumsum_rows(tri, jnp.log(f))
    bm = b[CHUNK // 2:CHUNK // 2 + 1, :]
    bl = b[CHUNK - 1:CHUNK, :]
    qt = qf * jnp.exp(b - bm)
    kt = kf * jnp.exp(bm - b)
    return qp, sq, qf, sg, f, kf, v, b, bm, bl, qt, kt


def _hgrn_specs(tb, block):
    first = 3 * ATTN_W // HGRN_W
    return [pl.BlockSpec((tb, HGRN_W), lambda i, k=k: (block(i), first + k)) for k in range(4)]


def _hgrn_fwd(proj, lb_logits, out_gain):
    s = proj.shape[0]
    tb = min(HGRN_TB, s)
    nb, cpb, nc = s // tb, tb // CHUNK, s // CHUNK

    def body(q_ref, f_ref, i_ref, g_ref, lbl_ref, gain_ref, o_ref, rec_ref, st_ref, st_scr):
        step = pl.program_id(0)

        @pl.when(step == 0)
        def _():
            st_scr[...] = jnp.zeros_like(st_scr)

        lb = _lower_bound(lbl_ref[...])
        r64 = lax.broadcasted_iota(jnp.int32, (CHUNK, CHUNK), 0)
        c64 = lax.broadcasted_iota(jnp.int32, (CHUNK, CHUNK), 1)
        tril = r64 >= c64
        tri = tril.astype(BF16)
        st = st_scr[...]
        for cc in range(cpb):
            sl = slice(cc * CHUNK, (cc + 1) * CHUNK)
            _, _, qf, _, _, kf, v, b, _, bl, qt, kt = _hgrn_chunk(q_ref, f_ref, i_ref, sl, lb, tri)
            qe = (qf * jnp.exp(b)).astype(BF16)
            kh = (kf * jnp.exp(bl - b)).astype(BF16)
            qtb, ktb, vb, stb = qt.astype(BF16), kt.astype(BF16), v.astype(BF16), st.astype(BF16)

            def out_h(hs):
                a = jnp.where(tril, _dot(qtb[:, hs], ktb[:, hs], NT), 0.0).astype(BF16)
                return _dot(qe[:, hs], stb[:, hs], NT) + _dot(a, vb[:, hs], NN)

            o_ref[sl, :] = _heads(out_h)
            st_ref[cc] = stb
            st = st * jnp.exp(bl) + _heads(lambda hs: _dot(vb[:, hs], kh[:, hs], TN))
        st_scr[...] = st
        o = o_ref[...]
        gate = g_ref[...]
        rec_ref[...] = (o * lax.rsqrt(_head_mean(o * o) + RMS_EPS) * gain_ref[...] * (gate * _sigmoid(gate))).astype(BF16)

    row = pl.BlockSpec((tb, HGRN_W), lambda i: (i, 0))
    return pl.pallas_call(
        body, name="hgrn_fwd", grid=(nb,),
        in_specs=_hgrn_specs(tb, lambda i: i) + [pl.BlockSpec((2, HGRN_W), lambda i: (0, 0)),
                                                 pl.BlockSpec((1, HGRN_W), lambda i: (0, 0))],
        out_specs=[row, row, pl.BlockSpec((cpb, BLK, HGRN_W), lambda i: (i, 0, 0))],
        out_shape=[_sds((s, HGRN_W), F32), _sds((s, HGRN_W), BF16), _sds((nc, BLK, HGRN_W), BF16)],
        scratch_shapes=[pltpu.VMEM((BLK, HGRN_W), F32)],
        compiler_params=_cp(("arbitrary",)),
    )(proj, proj, proj, proj, lb_logits, out_gain)


def _hgrn_bwd(proj, o_pre, states, dcat, lb_logits, out_gain):
    s = proj.shape[0]
    tb = min(HGRN_TB, s)
    nb, cpb, nc = s // tb, tb // CHUNK, s // CHUNK

    def body(q_ref, f_ref, i_ref, g_ref, o_ref, st_ref, stn_ref, dy_ref, lbl_ref, gain_ref,
             dq_ref, df_ref, di_ref, dg_ref, dgain_ref, dlbl_ref, do_scr, dst_scr, dlb_scr):
        step = pl.program_id(0)

        @pl.when(step == 0)
        def _():
            dst_scr[...] = jnp.zeros_like(dst_scr)
            dlb_scr[...] = jnp.zeros_like(dlb_scr)
            dgain_ref[...] = jnp.zeros_like(dgain_ref)

        lb = _lower_bound(lbl_ref[...])
        gain = gain_ref[...]
        o = o_ref[...]
        r = lax.rsqrt(_head_mean(o * o) + RMS_EPS)
        nrm = o * r
        gate = g_ref[...]
        sgt = _sigmoid(gate)
        dy = dy_ref[...]
        dg_ref[...] = (dy * nrm * gain * (sgt * (1.0 + gate * (1.0 - sgt)))).astype(BF16)
        dng = dy * (gate * sgt)
        dgain_ref[...] += _colsum(dng * nrm)
        dn = dng * gain
        do_scr[...] = r * (dn - nrm * _head_mean(dn * nrm))

        r64 = lax.broadcasted_iota(jnp.int32, (CHUNK, CHUNK), 0)
        c64 = lax.broadcasted_iota(jnp.int32, (CHUNK, CHUNK), 1)
        tril = r64 >= c64
        tri = tril.astype(BF16)
        triu = (r64 <= c64).astype(BF16)
        dst = dst_scr[...]
        dlb = dlb_scr[...]
        for cc in reversed(range(cpb)):
            sl = slice(cc * CHUNK, (cc + 1) * CHUNK)
            qp, sq, qf, sg, f, kf, v, b, bm, bl, qt, kt = _hgrn_chunk(q_ref, f_ref, i_ref, sl, lb, tri)
            stf = st_ref[cc].astype(F32)
            st_end = (st_ref[cc + 1] if cc + 1 < cpb else stn_ref[0]).astype(F32)
            csum = jnp.sum(st_end * dst, axis=0, keepdims=True)
            doc = do_scr[sl, :]
            dob, dstb = doc.astype(BF16), dst.astype(BF16)
            eb = jnp.exp(b)
            qe = (qf * eb).astype(BF16)
            kh = (kf * jnp.exp(bl - b)).astype(BF16)
            qtb, ktb = qt.astype(BF16), kt.astype(BF16)
            parts = []
            for h in range(HGRN_W // BLK):
                hs = slice(h * BLK, (h + 1) * BLK)
                da = jnp.where(tril, _dot3(doc[:, hs], v[:, hs], NT), 0.0)
                a = jnp.where(tril, _dot(qtb[:, hs], ktb[:, hs], NT), 0.0).astype(BF16)
                parts.append((
                    _dot3(da, kt[:, hs], NN), _dot3(doc[:, hs], stf[:, hs], NN),
                    _dot3(da, qt[:, hs], TN), _dot3(v[:, hs], dst[:, hs], NN),
                    _dot(a, dob[:, hs], TN) + _dot(kh[:, hs], dstb[:, hs], NT),
                    _dot(dob[:, hs], qe[:, hs], TN)))
            dqt, dqi, dkt, dks, dv, upd = (jnp.concatenate([p[n] for p in parts], axis=1) for n in range(6))
            dqf = dqt * jnp.exp(b - bm) + eb * dqi
            dkf = dkt * jnp.exp(bm - b) + jnp.exp(bl - b) * dks
            gq = qf * dqf - kf * dkf
            dlogf = csum + _cumsum_rows(triu, gq)
            dfv = dlogf / f - dkf
            dq_ref[sl, :] = (dqf * (sq * (1.0 + qp * (1.0 - sq)))).astype(BF16)
            df_ref[sl, :] = (dfv * (1.0 - lb) * sg * (1.0 - sg)).astype(BF16)
            di_ref[sl, :] = dv.astype(BF16)
            dst = dst * jnp.exp(bl) + upd
            dlb = dlb + _colsum(dfv * (1.0 - sg))
        dst_scr[...] = dst
        dlb_scr[...] = dlb

        @pl.when(step == nb - 1)
        def _():
            t = dlb * lb * (1.0 - lb)
            dlbl_ref[...] = jnp.concatenate([t, -t], axis=0)

    rev = lambda i: nb - 1 - i
    row = pl.BlockSpec((tb, HGRN_W), lambda i: (rev(i), 0))
    res = pl.pallas_call(
        body, name="hgrn_bwd", grid=(nb,),
        in_specs=_hgrn_specs(tb, rev) + [
            row, pl.BlockSpec((cpb, BLK, HGRN_W), lambda i: (rev(i), 0, 0)),
            pl.BlockSpec((1, BLK, HGRN_W), lambda i: (jnp.minimum((rev(i) + 1) * cpb, nc - 1), 0, 0)),
            pl.BlockSpec((tb, HGRN_W), lambda i: (rev(i), ATTN_W // HGRN_W)),
            pl.BlockSpec((2, HGRN_W), lambda i: (0, 0)), pl.BlockSpec((1, HGRN_W), lambda i: (0, 0))],
        out_specs=[row, row, row, row, pl.BlockSpec((1, HGRN_W), lambda i: (0, 0)),
                   pl.BlockSpec((2, HGRN_W), lambda i: (0, 0))],
        out_shape=[_sds((s, HGRN_W), BF16)] * 4 + [_sds((1, HGRN_W), F32), _sds((2, HGRN_W), F32)],
        scratch_shapes=[pltpu.VMEM((tb, HGRN_W), F32), pltpu.VMEM((BLK, HGRN_W), F32), pltpu.VMEM((1, HGRN_W), F32)],
        compiler_params=_cp(("arbitrary",)),
    )(proj, proj, proj, proj, o_pre, states, states, dcat, lb_logits, out_gain)
    return res


def _place():
    return lax.axis_index("x"), lax.axis_index("y"), lax.axis_index("c")


def _flip(x, y, ox, oy):
    return (1 - x if ox else x), (1 - y if oy else y)


def _half(rows, cc):
    return pl.ds(cc * (rows // 2), rows // 2)


def _remote(src, dst, ssem, rsem, to):
    return pltpu.make_async_remote_copy(src_ref=src, dst_ref=dst, send_sem=ssem, recv_sem=rsem,
                                        device_id=to, device_id_type=MESH)


def _ag_chip_copies(ins, outs, ssem, rsem):
    x, y, c = _place()
    j = 2 * x + y
    cps = []
    for k in range(len(ins)):
        rows = ins[k].shape[0]
        for idx, (ox, oy) in enumerate(FLIPS):
            px, py = _flip(x, y, ox, oy)
            cps.append(_remote(ins[k].at[_half(rows, c)], outs[k].at[j, _half(rows, c)],
                               ssem.at[k, idx], rsem.at[k, idx], (px, py, c)))
    return cps


def _ag_start(ins, outs, ssem, rsem):
    for cp in _ag_chip_copies(ins, outs, ssem, rsem):
        cp.start()


def _ag_finish(ins, outs, ssem, rsem):
    x, y, c = _place()
    sib = (x, y, 1 - c)
    passed = []
    for k in range(len(ins)):
        rows = ins[k].shape[0]
        for idx, (ox, oy) in enumerate(FLIPS):
            px, py = _flip(x, y, ox, oy)
            blk = outs[k].at[2 * px + py, _half(rows, c)]
            _remote(blk, blk, ssem.at[k, idx], rsem.at[k, idx], (px, py, c)).wait_recv()
            cp = _remote(blk, blk, ssem.at[k, 3 + idx], rsem.at[k, 3 + idx], sib)
            cp.start()
            passed.append(cp)
    for k in range(len(ins)):
        rows = ins[k].shape[0]
        for idx, (ox, oy) in enumerate(FLIPS):
            px, py = _flip(x, y, ox, oy)
            blk = outs[k].at[2 * px + py, _half(rows, 1 - c)]
            _remote(blk, blk, ssem.at[k, 3 + idx], rsem.at[k, 3 + idx], sib).wait_recv()
    for cp in _ag_chip_copies(ins, outs, ssem, rsem) + passed:
        cp.wait_send()


def _ag_shapes(shards):
    nk = len(shards)
    return ([_sds((N_CHIPS,) + tuple(w.shape), w.dtype) for w in shards],
            [pltpu.SemaphoreType.DMA((nk, 6)), pltpu.SemaphoreType.DMA((nk, 6))])


def _with_own(gathered, shard, j):
    return lax.dynamic_update_index_in_dim(gathered, shard, j, 0)


def _rs_pair_copies(ins, outs, ssem, rsem):
    x, y, c = _place()
    return [_remote(ins[k].at[:, _half(ins[k].shape[1], 1 - c)], outs[k], ssem.at[k], rsem.at[k], (x, y, 1 - c))
            for k in range(len(ins))]


def _rs_pair_start(ins, outs, ssem, rsem):
    for cp in _rs_pair_copies(ins, outs, ssem, rsem):
        cp.start()


def _rs_pair_finish(ins, outs, ssem, rsem):
    for cp in _rs_pair_copies(ins, outs, ssem, rsem):
        cp.wait()


def _rs_pair_exchange(grads):
    nk = len(grads)
    return (grads, [_sds((N_CHIPS, g.shape[1] // 2, g.shape[2]), g.dtype) for g in grads],
            [pltpu.SemaphoreType.DMA((nk,)), pltpu.SemaphoreType.DMA((nk,))], _rs_pair_start, _rs_pair_finish)


def _rs_pair(name, grads):
    nk = len(grads)
    ins, out_shape, sems, start, finish = _rs_pair_exchange(grads)

    def body(*refs):
        start(refs[:nk], refs[nk:2 * nk], *refs[2 * nk:])
        finish(refs[:nk], refs[nk:2 * nk], *refs[2 * nk:])

    return pl.pallas_call(body, name=name, in_specs=[ANY] * nk, out_specs=[ANY] * nk, out_shape=out_shape,
                          scratch_shapes=sems)(*ins)


def _rs_chip_copies(ins, outs, ssem, rsem):
    x, y, c = _place()
    cps = []
    for k in range(len(ins)):
        for idx, (ox, oy) in enumerate(FLIPS):
            px, py = _flip(x, y, ox, oy)
            cps.append(_remote(ins[k].at[2 * px + py], outs[k].at[idx], ssem.at[k, idx], rsem.at[k, idx], (px, py, c)))
    return cps


def _rs_chips_start(ins, outs, ssem, rsem):
    for cp in _rs_chip_copies(ins, outs, ssem, rsem):
        cp.start()


def _rs_chips_finish(ins, outs, ssem, rsem):
    for cp in _rs_chip_copies(ins, outs, ssem, rsem):
        cp.wait()


def _rs_chips_shapes(psums):
    nk = len(psums)
    return ([_sds((3,) + tuple(p.shape[1:]), p.dtype) for p in psums],
            [pltpu.SemaphoreType.DMA((nk, 3)), pltpu.SemaphoreType.DMA((nk, 3))])


def _rs_share(fulls):
    nk = len(fulls)

    def body(*refs):
        ins, outs = refs[:nk], refs[nk:2 * nk]
        ssem, rsem = refs[2 * nk:]
        x, y, c = _place()
        cps = []
        for k in range(nk):
            rows = fulls[k].shape[0]
            cp = _remote(ins[k].at[_half(rows, c)], outs[k].at[_half(rows, c)], ssem.at[k], rsem.at[k], (x, y, 1 - c))
            cp.start()
            cps.append(cp)
        for k, cp in enumerate(cps):
            rows = fulls[k].shape[0]
            cp.wait_send()
            theirs = outs[k].at[_half(rows, 1 - c)]
            _remote(theirs, theirs, ssem.at[k], rsem.at[k], (x, y, 1 - c)).wait_recv()

    return pl.pallas_call(
        body, name="rs_share", in_specs=[ANY] * nk, out_specs=[ANY] * nk,
        out_shape=[_sds(f.shape, f.dtype) for f in fulls], input_output_aliases={k: k for k in range(nk)},
        scratch_shapes=[pltpu.SemaphoreType.DMA((nk,)), pltpu.SemaphoreType.DMA((nk,))],
    )(*fulls)


def _allreduce_small(v):
    ndev = 8

    def body(in_ref, out_ref, buf, ssem, rsem):
        x, y, c = _place()
        me = 4 * x + 2 * y + c
        buf[me] = in_ref[...]
        cps = []
        for k in range(1, ndev):
            ox, oy, oc = (k >> 2) & 1, (k >> 1) & 1, k & 1
            px, py = _flip(x, y, ox, oy)
            pc = 1 - c if oc else c
            cp = pltpu.make_async_remote_copy(src_ref=in_ref, dst_ref=buf.at[me], send_sem=ssem.at[k - 1],
                                              recv_sem=rsem.at[k - 1], device_id=(px, py, pc), device_id_type=MESH)
            cp.start()
            cps.append((cp, 4 * px + 2 * py + pc, (px, py, pc)))
        for k, (cp, src, peer) in enumerate(cps):
            cp.wait_send()
            pltpu.make_async_remote_copy(src_ref=in_ref, dst_ref=buf.at[src], send_sem=ssem.at[k],
                                         recv_sem=rsem.at[k], device_id=peer, device_id_type=MESH).wait_recv()
        acc = buf[0]
        for i in range(1, ndev):
            acc = acc + buf[i]
        out_ref[...] = acc

    return pl.pallas_call(
        body, name="allreduce_small",
        in_specs=[pl.BlockSpec(memory_space=pltpu.VMEM)], out_specs=pl.BlockSpec(memory_space=pltpu.VMEM),
        out_shape=_sds(v.shape, v.dtype),
        scratch_shapes=[pltpu.VMEM((ndev,) + v.shape, v.dtype), pltpu.SemaphoreType.DMA((ndev - 1,)),
                        pltpu.SemaphoreType.DMA((ndev - 1,))],
    )(v)


def _rs_sum1(name, g, recv, c_idx):
    _, r, cdim = g.shape
    hr = r // 2
    tr = min(hr, 256)
    nr = hr // tr

    def body(c_ref, g_ref, r_ref, o32_ref, o16_ref):
        v = g_ref[...] + r_ref[...].astype(F32)
        o32_ref[...] = v
        o16_ref[...] = v.astype(BF16)

    spec = pl.BlockSpec((None, tr, cdim), lambda j, i, c_ref: (j, i, 0))
    return pl.pallas_call(
        body, name=name,
        grid_spec=pltpu.PrefetchScalarGridSpec(
            num_scalar_prefetch=1, grid=(N_CHIPS, nr),
            in_specs=[pl.BlockSpec((None, tr, cdim), lambda j, i, c_ref: (j, c_ref[0] * nr + i, 0)), spec],
            out_specs=[spec, spec]),
        out_shape=[_sds((N_CHIPS, hr, cdim), F32), _sds((N_CHIPS, hr, cdim), BF16)],
        compiler_params=_cp(("parallel", "parallel")),
    )(c_idx, g, recv)


def _rs_sum2(name, p32, recv, jc_idx):
    _, hr, cdim = p32.shape
    tr = min(hr, 256)
    nr = hr // tr

    def body(jc_ref, p_ref, r_ref, o_ref):
        o_ref[...] = ((p_ref[...] + r_ref[0].astype(F32)) + r_ref[1].astype(F32)) + r_ref[2].astype(F32)

    return pl.pallas_call(
        body, name=name,
        grid_spec=pltpu.PrefetchScalarGridSpec(
            num_scalar_prefetch=1, grid=(nr,),
            in_specs=[pl.BlockSpec((None, tr, cdim), lambda i, jc: (jc[0], i, 0)),
                      pl.BlockSpec((3, tr, cdim), lambda i, jc: (0, i, 0))],
            out_specs=pl.BlockSpec((tr, cdim), lambda i, jc: (jc[1] * nr + i, 0))),
        out_shape=_sds((2 * hr, cdim), F32),
        compiler_params=_cp(("parallel",)),
    )(jc_idx, p32, recv)


def _adamw(name, w, g, m, v):
    r, cdim = w.shape
    tr = min(r, 256)
    c1 = 1.0 - ADAM_B1 ** ADAM_STEP
    c2 = 1.0 - ADAM_B2 ** ADAM_STEP

    def body(w_ref, g_ref, m_ref, v_ref, d_ref, nm_ref, nv_ref):
        gv = g_ref[...]
        nm = ADAM_B1 * m_ref[...] + (1.0 - ADAM_B1) * gv
        nv = ADAM_B2 * v_ref[...] + (1.0 - ADAM_B2) * (gv * gv)
        d_ref[...] = -ADAM_LR * ((nm / c1) / (jnp.sqrt(nv / c2) + ADAM_EPS) + ADAM_WD * w_ref[...])
        nm_ref[...] = nm
        nv_ref[...] = nv

    spec = pl.BlockSpec((tr, cdim), lambda i: (i, 0))
    return pl.pallas_call(
        body, name=name, grid=(r // tr,), in_specs=[spec] * 4, out_specs=[spec] * 3,
        out_shape=[_sds((r, cdim), F32)] * 3, compiler_params=_cp(("parallel",)),
    )(w, g, m, v)


def _pack_small(mix_pre, attn_out, lb_logits, hgrn_out, mix_post, mlp_pre, mlp_post):
    rows = [mix_pre, jnp.concatenate([attn_out, hgrn_out], axis=1),
            jnp.concatenate([lb_logits[0:1], lb_logits[1:2]], axis=1), mix_post, mlp_pre, mlp_post,
            jnp.zeros((2, D_MODEL), F32)]
    return jnp.concatenate(rows, axis=0)


def _unpack_small(p):
    return (p[0:1], p[1:2, :ATTN_W], jnp.concatenate([p[2:3, :HGRN_W], p[2:3, HGRN_W:]], axis=0),
            p[1:2, ATTN_W:], p[3:4], p[4:5], p[5:6])


def kernel(x, mix_pre_norm, w_in, attn_out_norm, hgrn_lb_logits, hgrn_out_norm, w_out, mix_post_norm, mlp_pre_norm, w_ff1, w_ff2, mlp_post_norm, loss_target, m_mix_pre_norm, m_w_in, m_attn_out_norm, m_hgrn_lb_logits, m_hgrn_out_norm, m_w_out, m_mix_post_norm, m_mlp_pre_norm, m_w_ff1, m_w_ff2, m_mlp_post_norm, v_mix_pre_norm, v_w_in, v_attn_out_norm, v_hgrn_lb_logits, v_hgrn_out_norm, v_w_out, v_mix_post_norm, v_mlp_pre_norm, v_w_ff1, v_w_ff2, v_mlp_post_norm):
    s = x.shape[1]
    xs = x.reshape(s, D_MODEL)
    tgt = loss_target.reshape(s, D_MODEL)
    cx, cy, cc = _place()
    chip = 2 * cx + cy
    c_idx = jnp.reshape(cc, (1,)).astype(jnp.int32)
    jc_idx = jnp.stack([chip, cc]).astype(jnp.int32)

    big_w = [w_in[0], w_out[0], w_ff1[0], w_ff2[0]]
    big_m = [m_w_in[0], m_w_out[0], m_w_ff1[0], m_w_ff2[0]]
    big_v = [v_w_in[0], v_w_out[0], v_w_ff1[0], v_w_ff2[0]]
    shards = [w.astype(BF16) for w in big_w]

    h, wg_in = _rows_call("norm_in", lambda xv, g: ((xv * _rstd(xv) * g),),
                          [(xs, _row(D_MODEL)), (mix_pre_norm, "full")], [(D_MODEL, BF16, "row")], s,
                          exchange=(shards[:1], *_ag_shapes(shards[:1]), _ag_start, _ag_finish))
    wg_in = _with_own(wg_in, shards[0], chip)
    (proj,) = _mm_cols("mm_proj", h, wg_in, NN, [F32])
    hg_o, rec, states = _hgrn_fwd(proj, hgrn_lb_logits, hgrn_out_norm)
    attn_o, attn_lse, wg_out, wg_1, wg_2 = _attn_fwd(proj, shards[1:])
    wg_out, wg_1, wg_2 = (_with_own(g, w, chip) for g, w in zip((wg_out, wg_1, wg_2), shards[1:]))
    (attn_n,) = _rows_call("attn_norm", lambda o, gain: (o * _rstd(o) * gain,),
                           [(attn_o, _row(ATTN_W)), (attn_out_norm, "full")], [(ATTN_W, BF16, "row")], s)
    cat = jnp.concatenate([attn_n, rec], axis=1)

    def post1(mv, xv, g_post, g_pre2):
        x1 = xv + mv * _rstd(mv) * g_post
        return mv, x1, x1 * _rstd(x1) * g_pre2

    mixed, x1, h2 = _rows_call(
        "mm_mixed", post1, [(xs, _row(D_MODEL)), (mix_post_norm, "full"), (mlp_pre_norm, "full")],
        [(D_MODEL, F32, "row"), (D_MODEL, F32, "row"), (D_MODEL, BF16, "row")], s,
        matmul=(cat, wg_out.reshape(D_MODEL, D_MODEL)))

    def sq_relu(u):
        r = jnp.maximum(u, 0.0)
        return r * r, r

    act, ru = _mm_cols("mm_ff1", h2, wg_1, NN, [BF16, BF16], epi=sq_relu)

    def post2(fv, x1v, tv, g):
        y = x1v + fv * _rstd(fv) * g
        dy = (y - tv) * (1.0 / D_MODEL)
        err = y - tv
        loss = 0.5 * jnp.sum(jnp.mean(err * err, axis=-1, keepdims=True), axis=0, keepdims=True)
        dff, dgc = _norm_bwd(fv, g, dy)
        return dy, dff, _colsum(dgc), jnp.broadcast_to(loss, (1, BLK))

    dy, dff, g_mlp_post, loss_part = _rows_call(
        "mm_ff2", post2, [(x1, _row(D_MODEL)), (tgt, _row(D_MODEL)), (mlp_post_norm, "full")],
        [(D_MODEL, F32, "row"), (D_MODEL, BF16, "row"), (D_MODEL, F32, "acc"), (BLK, F32, "acc")], s,
        tm=ROW_TILE // 2, matmul=(act, wg_2.reshape(D_FF, D_MODEL)))

    (du,) = _mm_cols("mm_du", dff, wg_2, NT, [BF16], epi=lambda acc, r: (acc * (2.0 * r.astype(F32)),),
                     extras=(ru,))
    gw_2 = _mm_wgrad("mm_gw2", act, dff, True)
    gw_1 = _mm_wgrad("mm_gw1", h2, du, False)

    def bwd_mid(dh2v, dyv, x1v, mv, g_pre2, g_post):
        d1, gc1 = _norm_bwd(x1v, g_pre2, dh2v)
        dx1 = dyv + d1
        dm, gc2 = _norm_bwd(mv, g_post, dx1)
        return dx1, dm, _colsum(gc1), _colsum(gc2)

    w1_t = jnp.transpose(wg_1, (0, 2, 1)).reshape(D_FF, D_MODEL)
    dx1, dmixed, g_mlp_pre, g_mix_post, *from_pair = _rows_call(
        "mm_dh2", bwd_mid, [(dy, _row(D_MODEL)), (x1, _row(D_MODEL)), (mixed, _row(D_MODEL)),
                            (mlp_pre_norm, "full"), (mix_post_norm, "full")],
        [(D_MODEL, F32, "row"), (D_MODEL, BF16, "row"), (D_MODEL, F32, "acc"), (D_MODEL, F32, "acc")], s,
        tm=ROW_TILE // 2, matmul=(du, w1_t), exchange=_rs_pair_exchange([gw_1[1], gw_2[1]]))

    (dcat,) = _mm_cols("mm_dcat", dmixed, wg_out, NT, [F32])
    gw_out = _mm_wgrad("mm_gwout", cat, dmixed, True)
    names = ["out", "ff1", "ff2", "in"]
    ready = [gw_out, gw_1, gw_2]
    from_pair = list(_rs_pair("rs_pair_out", [gw_out[1]])) + from_pair
    pair = [_rs_sum1(f"rs_sum1_{n}", g[0], r, c_idx) for n, g, r in zip(names, ready, from_pair)]

    def attn_norm_bwd(dc, o, gain):
        do, gc = _norm_bwd(o, gain, dc)
        t = do * o
        lane = lax.broadcasted_iota(jnp.int32, (t.shape[0], BLK), 1) < 64
        parts = []
        for p in range(ATTN_W // BLK):
            tp = t[:, p * BLK:(p + 1) * BLK]
            sa = jnp.sum(jnp.where(lane, tp, 0.0), axis=1, keepdims=True)
            sb = jnp.sum(jnp.where(lane, 0.0, tp), axis=1, keepdims=True)
            parts.append(jnp.where(lane, sa, sb))
        return do, jnp.concatenate(parts, axis=1), _colsum(gc)

    do_attn, delta, g_attn_out = _rows_call(
        "attn_norm_bwd", attn_norm_bwd, [(dcat, _row(ATTN_W, 0)), (attn_o, _row(ATTN_W)), (attn_out_norm, "full")],
        [(ATTN_W, F32, "row"), (ATTN_W, F32, "row"), (ATTN_W, F32, "acc")], s)
    dq, dk, dv, *from_chips = _attn_bwd(proj, do_attn, attn_lse, delta, [p[1] for p in pair])
    dhq, dhf, dhi, dhg, g_hgrn_out, g_lb = _hgrn_bwd(proj, hg_o, states, dcat, hgrn_lb_logits, hgrn_out_norm)

    def dproj_asm(*a):
        return (jnp.concatenate([t.astype(BF16) for t in a], axis=1),)

    (dproj,) = _rows_call("dproj_asm", dproj_asm,
                          [(t, _row(ATTN_W)) for t in (dq, dk, dv)] + [(t, _row(HGRN_W)) for t in (dhq, dhf, dhi, dhg)],
                          [(PROJ_W, BF16, "row")], s)
    gw_in = _mm_wgrad("mm_gwin", h, dproj, False)
    (from_pair_in,) = _rs_pair("rs_pair_in", [gw_in[1]])
    pair.append(_rs_sum1("rs_sum1_in", gw_in[0], from_pair_in, c_idx))
    rs_shape, rs_sems = _rs_chips_shapes([pair[3][1]])

    def bwd_in(dhv, dx1v, xv, g):
        d0, gc = _norm_bwd(xv, g, dhv)
        return dx1v + d0, _colsum(gc)

    w_in_t = jnp.transpose(wg_in, (0, 2, 1)).reshape(PROJ_W, D_MODEL)
    grad_x, g_mix_pre, from_chips_in = _rows_call(
        "mm_dh", bwd_in, [(dx1, _row(D_MODEL)), (xs, _row(D_MODEL)), (mix_pre_norm, "full")],
        [(D_MODEL, F32, "row"), (D_MODEL, F32, "acc")], s, tm=ROW_TILE // 2, matmul=(dproj, w_in_t),
        exchange=([pair[3][1]], rs_shape, rs_sems, _rs_chips_start, _rs_chips_finish))
    from_chips.append(from_chips_in)

    loss = lax.psum(loss_part[0, 0], ("x", "y", "c"))
    small_g = _allreduce_small(_pack_small(g_mix_pre, g_attn_out, g_lb, g_hgrn_out, g_mix_post, g_mlp_pre, g_mlp_post))

    reduced = [_rs_sum2(f"rs_sum2_{n}", p[0], r, jc_idx) for n, p, r in zip(names, pair, from_chips)]
    g_wout, g_w1, g_w2, g_win = _rs_share(reduced)
    full = [g_win, g_wout, g_w1, g_w2]

    upd = [_adamw(f"adamw_{n}", w, g, m, v) for n, w, g, m, v in zip(("in", "out", "ff1", "ff2"), big_w, full, big_m, big_v)]
    small_w = _pack_small(mix_pre_norm, attn_out_norm, hgrn_lb_logits, hgrn_out_norm, mix_post_norm, mlp_pre_norm,
                          mlp_post_norm)
    small_m = _pack_small(m_mix_pre_norm, m_attn_out_norm, m_hgrn_lb_logits, m_hgrn_out_norm, m_mix_post_norm,
                          m_mlp_pre_norm, m_mlp_post_norm)
    small_v = _pack_small(v_mix_pre_norm, v_attn_out_norm, v_hgrn_lb_logits, v_hgrn_out_norm, v_mix_post_norm,
                          v_mlp_pre_norm, v_mlp_post_norm)
    small_upd = _adamw("adamw_small", small_w, small_g, small_m, small_v)

    def assemble(small, big):
        sm = _unpack_small(small)
        return (sm[0], big[0][None], sm[1], sm[2], sm[3], big[1][None], sm[4], sm[5], big[2][None], big[3][None], sm[6])

    g_out = assemble(small_g, full)
    d_out = assemble(small_upd[0], [u[0] for u in upd])
    m_out = assemble(small_upd[1], [u[1] for u in upd])
    v_out = assemble(small_upd[2], [u[2] for u in upd])
    return (loss, grad_x.reshape(x.shape), *g_out, *d_out, *m_out, *v_out)
```

```python
import numpy as np
import jax
import jax.numpy as jnp
from jax import lax
from jax.experimental import pallas as pl
from jax.experimental.pallas import tpu as pltpu

F32 = jnp.float32
BF16 = jnp.bfloat16
MESH = pl.DeviceIdType.MESH
ANY = pl.BlockSpec(memory_space=pl.ANY)

RMS_EPS = 1e-6
D_MODEL = 1024
ATTN_W = 512
HGRN_W = 512
PROJ_W = 3584
D_FF = 4096
N_CHIPS = 4
BLK = 128
CHUNK = 64
HGRN_TB = 512
ATTN_GROUP = 8
DILATIONS = (1, 4, 16)
ATTN_SCALE = 0.125
ROW_TILE = 512
MM_TILE = 1024
VMEM_LIMIT = 48 * 2 ** 20
FLIPS = ((1, 0), (0, 1), (1, 1))

ADAM_LR, ADAM_B1, ADAM_B2, ADAM_EPS, ADAM_WD, ADAM_STEP = 0.001, 0.9, 0.999, 1e-08, 0.01, 10


def _cp(sem=None):
    return pltpu.CompilerParams(dimension_semantics=sem, vmem_limit_bytes=VMEM_LIMIT)


def _sigmoid(v):
    return 1.0 / (1.0 + jnp.exp(-v))


def _dot(a, b, contract, precision=None):
    return lax.dot_general(a, b, (contract, ((), ())), preferred_element_type=F32, precision=precision)


NN = ((1,), (0,))
NT = ((1,), (1,))
TN = ((0,), (0,))


def _matmul(name, a, b, *, grid, a_spec, b_spec, contract, outs, epi=None, extras=(), extra_specs=()):
    n_ex = len(extras)

    def body(a_ref, b_ref, *rest):
        ex, out_refs = rest[:n_ex], rest[n_ex:]
        acc = _dot(a_ref[...], b_ref[...], contract)
        res = epi(acc, *[e[...] for e in ex]) if epi else (acc,)
        for o, r in zip(out_refs, res):
            o[...] = r.astype(o.dtype)

    return pl.pallas_call(
        body, name=name, grid=grid,
        in_specs=[a_spec, b_spec, *extra_specs],
        out_specs=[s for _, s in outs],
        out_shape=[s for s, _ in outs],
        compiler_params=_cp(("parallel",) * len(grid)),
    )(a, b, *extras)


def _sds(shape, dtype):
    return jax.ShapeDtypeStruct(shape, dtype)


def _mm_cols(name, a, w, contract, out_dtypes, epi=None, extras=()):
    m, k = a.shape
    jn = w.shape[0]
    nj = w.shape[2] if contract == NN else w.shape[1]
    tm = min(m, MM_TILE)
    outs = [(_sds((m, jn * nj), dt), pl.BlockSpec((tm, nj), lambda j, i: (i, j))) for dt in out_dtypes]
    return _matmul(name, a, w, grid=(jn, m // tm),
                   a_spec=pl.BlockSpec((tm, k), lambda j, i: (i, 0)),
                   b_spec=pl.BlockSpec((None,) + w.shape[1:], lambda j, i: (j, 0, 0)),
                   contract=contract, outs=outs, epi=epi, extras=extras,
                   extra_specs=[pl.BlockSpec((tm, nj), lambda j, i: (i, j)) for _ in extras])


def _mm_wgrad(name, a, b, a_by_j):
    s = a.shape[0]
    if a_by_j:
        r, c = a.shape[1] // N_CHIPS, b.shape[1]
    else:
        r, c = a.shape[1], b.shape[1] // N_CHIPS
    tr = min(r, 512)
    nr = r // tr
    if a_by_j:
        a_spec = pl.BlockSpec((s, tr), lambda j, i: (0, j * nr + i))
        b_spec = pl.BlockSpec((s, c), lambda j, i: (0, 0))
    else:
        a_spec = pl.BlockSpec((s, tr), lambda j, i: (0, i))
        b_spec = pl.BlockSpec((s, c), lambda j, i: (0, j))
    outs = [(_sds((N_CHIPS, r, c), dt), pl.BlockSpec((None, tr, c), lambda j, i: (j, i, 0))) for dt in (F32, BF16)]
    return _matmul(name, a, b, grid=(N_CHIPS, nr), a_spec=a_spec, b_spec=b_spec, contract=TN, outs=outs,
                   epi=lambda acc: (acc, acc))


def _rows_call(name, fn, ins, outs, s, tm=ROW_TILE, matmul=None, exchange=None):
    in_specs = []
    if matmul:
        a, w = matmul
        in_specs += [pl.BlockSpec((tm, a.shape[1]), lambda i: (i, 0)), pl.BlockSpec(w.shape, lambda i: (0, 0))]
    for arr, kind in ins:
        if kind == "full":
            in_specs.append(pl.BlockSpec(arr.shape, lambda i: (0, 0)))
        else:
            _, w_, cb = kind
            in_specs.append(pl.BlockSpec((tm, w_), lambda i, cb=cb: (i, cb)))
    out_specs, out_shape, is_acc = [], [], []
    for w_, dt, kind in outs:
        if kind == "acc":
            out_specs.append(pl.BlockSpec((1, w_), lambda i: (0, 0)))
            out_shape.append(_sds((1, w_), dt))
        else:
            out_specs.append(pl.BlockSpec((tm, w_), lambda i: (i, 0)))
            out_shape.append(_sds((s, w_), dt))
        is_acc.append(kind == "acc")
    n_mm, n_in, n_out = (2 if matmul else 0), len(ins), len(outs)
    x_ins, x_shapes, x_sems, x_start, x_finish = exchange if exchange else ((), [], [], None, None)
    n_x = len(x_ins)
    steps = s // tm

    def body(*refs):
        in_refs, xi = refs[n_mm:n_mm + n_in], refs[n_mm + n_in:n_mm + n_in + n_x]
        out_refs = refs[n_mm + n_in + n_x:n_mm + n_in + n_x + n_out]
        xo, sems = refs[n_mm + n_in + n_x + n_out:n_mm + n_in + 2 * n_x + n_out], refs[n_mm + n_in + 2 * n_x + n_out:]
        i = pl.program_id(0)

        if exchange:
            @pl.when(i == 0)
            def _():
                x_start(xi, xo, *sems)

        args = [r[...] for r in in_refs]
        if matmul:
            args.insert(0, _dot(refs[0][...], refs[1][...], NN))
        res = fn(*args)
        for o, r, acc in zip(out_refs, res, is_acc):
            if acc:
                @pl.when(i == 0)
                def _(o=o):
                    o[...] = jnp.zeros_like(o)
                o[...] += r.astype(o.dtype)
            else:
                o[...] = r.astype(o.dtype)

        if exchange:
            @pl.when(i == steps - 1)
            def _():
                x_finish(xi, xo, *sems)

    sem = ("arbitrary",) if any(is_acc) or exchange else ("parallel",)
    return pl.pallas_call(
        body, name=name, grid=(steps,), in_specs=in_specs + [ANY] * n_x, out_specs=out_specs + [ANY] * n_x,
        out_shape=out_shape + list(x_shapes), scratch_shapes=list(x_sems), compiler_params=_cp(sem),
    )(*(matmul or ()), *[a for a, _ in ins], *x_ins)


def _rstd(v):
    return lax.rsqrt(jnp.mean(v * v, axis=-1, keepdims=True) + RMS_EPS)


def _norm_bwd(v, gain, dy):
    r = _rstd(v)
    n = v * r
    dn = dy * gain
    dv = r * (dn - n * jnp.mean(dn * n, axis=-1, keepdims=True))
    return dv, dy * n


def _colsum(v):
    return jnp.sum(v, axis=0, keepdims=True)


def _row(w, cb=0):
    return ("row", w, cb)


N_PAIRS = ATTN_W // BLK


def _head_col(v, mask):
    return jnp.max(jnp.where(mask, v, -jnp.inf), axis=1, keepdims=True)


def _slopes():
    t = np.zeros((N_PAIRS, 8, 2 * BLK), np.float32)
    for p in range(N_PAIRS):
        for hh in range(2):
            t[p, hh, :] = 2.0 ** -(2 * p + hh + 1)
    return jnp.asarray(t)


def _rows(n, r, d):
    base = pl.multiple_of(n * (BLK * d), BLK)
    return pl.ds(base + r, BLK, stride=d) if d > 1 else pl.ds(base, BLK)


def _attn_bias(sl_ref, bias_scr):
    row = lax.broadcasted_iota(jnp.int32, (BLK, 2 * BLK), 0)
    col = lax.broadcasted_iota(jnp.int32, (BLK, 2 * BLK), 1)
    dist = row + BLK - col
    in_window = (dist >= 0) & (dist <= BLK)
    distf = dist.astype(F32)
    for di, d in enumerate(DILATIONS):
        for hh in range(2):
            bias_scr[2 * di + hh] = jnp.where(in_window, -(sl_ref[hh:hh + 1, :] * float(d)) * distf, -1e30)


def _first_block_penalty(n):
    col = lax.broadcasted_iota(jnp.int32, (1, 2 * BLK), 1)
    return jnp.where(col + n * BLK >= BLK, 0.0, -1e30)


def _attn_groups(s, d):
    nb = s // (BLK * d)
    g = ATTN_GROUP
    if d >= g:
        return [(nb, lambda n, r0=r0: [(n, r0 + u) for u in range(g)]) for r0 in range(0, d, g)]
    per = g // d
    return [(nb // per, lambda t: [(per * t + u, r) for u in range(per) for r in range(d)])]


def _attn_fwd(proj, shards):
    s = proj.shape[0]
    nk = len(shards)

    def body(sl_ref, q_ref, k_ref, v_ref, *rest):
        w_refs, (o_ref, l_ref) = rest[:nk], rest[nk:nk + 2]
        wg_refs, (bias_scr, ssem, rsem) = rest[nk + 2:2 * nk + 2], rest[2 * nk + 2:]
        pair = pl.program_id(0)

        @pl.when(pair == 0)
        def _():
            _ag_start(w_refs, wg_refs, ssem, rsem)

        _attn_bias(sl_ref, bias_scr)
        lane_q = lax.broadcasted_iota(jnp.int32, (BLK, BLK), 1) < 64
        lane_k = lax.broadcasted_iota(jnp.int32, (2 * BLK, BLK), 1) < 64

        def branch(n, r, di):
            d = DILATIONS[di]
            rows = _rows(n, r, d)
            prev = _rows(jnp.maximum(n - 1, 0), r, d)
            pen = _first_block_penalty(n)
            q2 = q_ref[rows, :] * ATTN_SCALE
            kk = jnp.concatenate([k_ref[prev, :], k_ref[rows, :]], axis=0).astype(BF16)
            vv = jnp.concatenate([v_ref[prev, :], v_ref[rows, :]], axis=0)
            o2 = jnp.zeros((BLK, BLK), F32)
            lse2 = jnp.zeros((BLK, BLK), F32)
            for hh in range(2):
                mq = lane_q if hh == 0 else ~lane_q
                mk = lane_k if hh == 0 else ~lane_k
                qm = jnp.where(mq, q2, 0.0).astype(BF16)
                sc = _dot(qm, kk, NT) + bias_scr[2 * di + hh] + pen
                m = jnp.max(sc, axis=1, keepdims=True)
                pr = jnp.exp(sc - m)
                den = jnp.sum(pr, axis=1, keepdims=True)
                vm = jnp.where(mk, vv, 0.0).astype(BF16)
                o2 = o2 + _dot(pr.astype(BF16), vm, NN) / den
                lse2 = jnp.where(mq, m + jnp.log(den), lse2)
            return rows, o2, lse2

        def merge(rows, o2, lse2, first):
            if first:
                o_ref[rows, :] = o2
                l_ref[rows, :] = lse2
            else:
                lo = l_ref[rows, :]
                mx = jnp.maximum(lo, lse2)
                ln = mx + jnp.log(jnp.exp(lo - mx) + jnp.exp(lse2 - mx))
                o_ref[rows, :] = jnp.exp(lo - ln) * o_ref[rows, :] + jnp.exp(lse2 - ln) * o2
                l_ref[rows, :] = ln

        for di, d in enumerate(DILATIONS):
            for trips, blocks in _attn_groups(s, d):
                def trip(t, carry, di=di, blocks=blocks):
                    done = [branch(n, r, di) for n, r in blocks(t)]
                    for rows, o2, lse2 in done:
                        merge(rows, o2, lse2, di == 0)
                    return carry

                lax.fori_loop(0, trips, trip, 0)

        @pl.when(pair == N_PAIRS - 1)
        def _():
            _ag_finish(w_refs, wg_refs, ssem, rsem)

    cb = lambda base: pl.BlockSpec((s, BLK), lambda p, base=base: (0, base + p))
    out = pl.BlockSpec((s, BLK), lambda p: (0, p))
    ag_shape, ag_sems = _ag_shapes(shards)
    return pl.pallas_call(
        body, name="attn_fwd", grid=(N_PAIRS,),
        in_specs=[pl.BlockSpec((None, 8, 2 * BLK), lambda p: (p, 0, 0)), cb(0), cb(N_PAIRS), cb(2 * N_PAIRS)]
        + [ANY] * nk,
        out_specs=[out, out] + [ANY] * nk, out_shape=[_sds((s, ATTN_W), F32)] * 2 + ag_shape,
        scratch_shapes=[pltpu.VMEM((2 * len(DILATIONS), BLK, 2 * BLK), F32)] + ag_sems,
        compiler_params=_cp(("arbitrary",)),
    )(_slopes(), proj, proj, proj, *shards)


def _attn_bwd(proj, do, lse, delta, psums):
    s = proj.shape[0]
    nk = len(psums)

    def body(sl_ref, q_ref, k_ref, v_ref, do_ref, l_ref, e_ref, *rest):
        p_refs, (dq_ref, dk_ref, dv_ref) = rest[:nk], rest[nk:nk + 3]
        got_refs, (bias_scr, ssem, rsem) = rest[nk + 3:2 * nk + 3], rest[2 * nk + 3:]
        pair = pl.program_id(0)

        @pl.when(pair == 0)
        def _():
            _rs_chips_start(p_refs, got_refs, ssem, rsem)

        _attn_bias(sl_ref, bias_scr)
        lane_q = lax.broadcasted_iota(jnp.int32, (BLK, BLK), 1) < 64
        lane_k = lax.broadcasted_iota(jnp.int32, (2 * BLK, BLK), 1) < 64
        dk_ref[...] = jnp.zeros_like(dk_ref)
        dv_ref[...] = jnp.zeros_like(dv_ref)

        def branch(n, r, di):
            d = DILATIONS[di]
            rows = _rows(n, r, d)
            prev = _rows(jnp.maximum(n - 1, 0), r, d)
            pen = _first_block_penalty(n)
            q1, d1, l1, e1 = q_ref[rows, :] * ATTN_SCALE, do_ref[rows, :], l_ref[rows, :], e_ref[rows, :]
            kk = jnp.concatenate([k_ref[prev, :], k_ref[rows, :]], axis=0)
            kkb = kk.astype(BF16)
            vvb = jnp.concatenate([v_ref[prev, :], v_ref[rows, :]], axis=0).astype(BF16)
            dq2 = jnp.zeros((BLK, BLK), F32)
            dkk = jnp.zeros((2 * BLK, BLK), F32)
            dvv = jnp.zeros((2 * BLK, BLK), F32)
            for hh in range(2):
                mq = lane_q if hh == 0 else ~lane_q
                mk = lane_k if hh == 0 else ~lane_k
                qm = jnp.where(mq, q1, 0.0).astype(BF16)
                dm = jnp.where(mq, d1, 0.0).astype(BF16)
                sc = _dot(qm, kkb, NT) + bias_scr[2 * di + hh] + pen
                pr = jnp.exp(sc - _head_col(l1, mq))
                ds = (pr * (_dot(dm, vvb, NT) - _head_col(e1, mq))).astype(BF16)
                km = jnp.where(mk, kk, 0.0).astype(BF16)
                dq2 = dq2 + _dot(ds, km, NN)
                dkk = dkk + _dot(ds, qm, TN)
                dvv = dvv + _dot(pr.astype(BF16), dm, TN)
            return rows, prev, dq2 * ATTN_SCALE, dkk, dvv

        for di, d in enumerate(DILATIONS):
            for trips, blocks in _attn_groups(s, d):
                def trip(t, carry, di=di, blocks=blocks, first=(di == 0)):
                    done = [branch(n, r, di) for n, r in blocks(t)]
                    for rows, prev, dq2, dkk, dvv in done:
                        dq_ref[rows, :] = dq2 if first else dq_ref[rows, :] + dq2
                        dk_ref[prev, :] = dk_ref[prev, :] + dkk[:BLK]
                        dk_ref[rows, :] = dk_ref[rows, :] + dkk[BLK:]
                        dv_ref[prev, :] = dv_ref[prev, :] + dvv[:BLK]
                        dv_ref[rows, :] = dv_ref[rows, :] + dvv[BLK:]
                    return carry

                lax.fori_loop(0, trips, trip, 0)

        @pl.when(pair == N_PAIRS - 1)
        def _():
            _rs_chips_finish(p_refs, got_refs, ssem, rsem)

    cb = lambda base: pl.BlockSpec((s, BLK), lambda p, base=base: (0, base + p))
    out = pl.BlockSpec((s, BLK), lambda p: (0, p))
    rs_shape, rs_sems = _rs_chips_shapes(psums)
    return pl.pallas_call(
        body, name="attn_bwd", grid=(N_PAIRS,),
        in_specs=[pl.BlockSpec((None, 8, 2 * BLK), lambda p: (p, 0, 0)), cb(0), cb(N_PAIRS), cb(2 * N_PAIRS),
                  out, out, out] + [ANY] * nk,
        out_specs=[out] * 3 + [ANY] * nk, out_shape=[_sds((s, ATTN_W), F32)] * 3 + rs_shape,
        scratch_shapes=[pltpu.VMEM((2 * len(DILATIONS), BLK, 2 * BLK), F32)] + rs_sems,
        compiler_params=_cp(("arbitrary",)),
    )(_slopes(), proj, proj, proj, do, lse, delta, *psums)


def _lower_bound(lbl):
    return 1.0 / (1.0 + jnp.exp(lbl[1:2, :] - lbl[0:1, :]))


def _hi(a):
    bits = lax.bitcast_convert_type(a, jnp.uint32) & jnp.uint32(0xFFFF0000)
    return lax.bitcast_convert_type(bits, F32)


def _dot3(a, b, contract):
    ah, bh = _hi(a), _hi(b)
    al, bl = (a - ah).astype(BF16), (b - bh).astype(BF16)
    ah, bh = ah.astype(BF16), bh.astype(BF16)
    return _dot(ah, bh, contract) + (_dot(ah, bl, contract) + _dot(al, bh, contract))


def _cumsum_rows(tri, g):
    g1 = _hi(g)
    r1 = g - g1
    g2 = _hi(r1)
    g3 = r1 - g2
    return _dot(tri, g1.astype(BF16), NN) + (_dot(tri, g2.astype(BF16), NN) + _dot(tri, g3.astype(BF16), NN))


def _heads(fn):
    return jnp.concatenate([fn(slice(h * BLK, (h + 1) * BLK)) for h in range(HGRN_W // BLK)], axis=1)


def _head_mean(t):
    return _heads(lambda hs: jnp.broadcast_to(jnp.mean(t[:, hs], axis=1, keepdims=True), (t.shape[0], BLK)))


def _hgrn_chunk(q_ref, f_ref, i_ref, sl, lb, tri):
    qp = q_ref[sl, :]
    sq = _sigmoid(qp)
    qf = qp * sq
    sg = _sigmoid(f_ref[sl, :])
    f = lb + (1.0 - lb) * sg
    kf = 1.0 - f
    v = i_ref[sl, :]
    b = _cumsum_rows(tri, jnp.log(f))
    bm = b[CHUNK // 2:CHUNK // 2 + 1, :]
    bl = b[CHUNK - 1:CHUNK, :]
    qt = qf * jnp.exp(b - bm)
    kt = kf * jnp.exp(bm - b)
    return qp, sq, qf, sg, f, kf, v, b, bm, bl, qt, kt


def _hgrn_specs(tb, block):
    first = 3 * ATTN_W // HGRN_W
    return [pl.BlockSpec((tb, HGRN_W), lambda i, k=k: (block(i), first + k)) for k in range(4)]


def _hgrn_fwd(proj, lb_logits, out_gain):
    s = proj.shape[0]
    tb = min(HGRN_TB, s)
    nb, cpb, nc = s // tb, tb // CHUNK, s // CHUNK

    def body(q_ref, f_ref, i_ref, g_ref, lbl_ref, gain_ref, o_ref, rec_ref, st_ref, st_scr):
        step = pl.program_id(0)

        @pl.when(step == 0)
        def _():
            st_scr[...] = jnp.zeros_like(st_scr)

        lb = _lower_bound(lbl_ref[...])
        r64 = lax.broadcasted_iota(jnp.int32, (CHUNK, CHUNK), 0)
        c64 = lax.broadcasted_iota(jnp.int32, (CHUNK, CHUNK), 1)
        tril = r64 >= c64
        tri = tril.astype(BF16)
        st = st_scr[...]
        for cc in range(cpb):
            sl = slice(cc * CHUNK, (cc + 1) * CHUNK)
            _, _, qf, _, _, kf, v, b, _, bl, qt, kt = _hgrn_chunk(q_ref, f_ref, i_ref, sl, lb, tri)
            qe = (qf * jnp.exp(b)).astype(BF16)
            kh = (kf * jnp.exp(bl - b)).astype(BF16)
            qtb, ktb, vb, stb = qt.astype(BF16), kt.astype(BF16), v.astype(BF16), st.astype(BF16)

            def out_h(hs):
                a = jnp.where(tril, _dot(qtb[:, hs], ktb[:, hs], NT), 0.0).astype(BF16)
                return _dot(qe[:, hs], stb[:, hs], NT) + _dot(a, vb[:, hs], NN)

            o_ref[sl, :] = _heads(out_h)
            st_ref[cc] = stb
            st = st * jnp.exp(bl) + _heads(lambda hs: _dot(vb[:, hs], kh[:, hs], TN))
        st_scr[...] = st
        o = o_ref[...]
        gate = g_ref[...]
        rec_ref[...] = (o * lax.rsqrt(_head_mean(o * o) + RMS_EPS) * gain_ref[...] * (gate * _sigmoid(gate))).astype(BF16)

    row = pl.BlockSpec((tb, HGRN_W), lambda i: (i, 0))
    return pl.pallas_call(
        body, name="hgrn_fwd", grid=(nb,),
        in_specs=_hgrn_specs(tb, lambda i: i) + [pl.BlockSpec((2, HGRN_W), lambda i: (0, 0)),
                                                 pl.BlockSpec((1, HGRN_W), lambda i: (0, 0))],
        out_specs=[row, row, pl.BlockSpec((cpb, BLK, HGRN_W), lambda i: (i, 0, 0))],
        out_shape=[_sds((s, HGRN_W), F32), _sds((s, HGRN_W), BF16), _sds((nc, BLK, HGRN_W), BF16)],
        scratch_shapes=[pltpu.VMEM((BLK, HGRN_W), F32)],
        compiler_params=_cp(("arbitrary",)),
    )(proj, proj, proj, proj, lb_logits, out_gain)


def _hgrn_bwd(proj, o_pre, states, dcat, lb_logits, out_gain):
    s = proj.shape[0]
    tb = min(HGRN_TB, s)
    nb, cpb, nc = s // tb, tb // CHUNK, s // CHUNK

    def body(q_ref, f_ref, i_ref, g_ref, o_ref, st_ref, stn_ref, dy_ref, lbl_ref, gain_ref,
             dq_ref, df_ref, di_ref, dg_ref, dgain_ref, dlbl_ref, do_scr, dst_scr, dlb_scr):
        step = pl.program_id(0)

        @pl.when(step == 0)
        def _():
            dst_scr[...] = jnp.zeros_like(dst_scr)
            dlb_scr[...] = jnp.zeros_like(dlb_scr)
            dgain_ref[...] = jnp.zeros_like(dgain_ref)

        lb = _lower_bound(lbl_ref[...])
        gain = gain_ref[...]
        o = o_ref[...]
        r = lax.rsqrt(_head_mean(o * o) + RMS_EPS)
        nrm = o * r
        gate = g_ref[...]
        sgt = _sigmoid(gate)
        dy = dy_ref[...]
        dg_ref[...] = (dy * nrm * gain * (sgt * (1.0 + gate * (1.0 - sgt)))).astype(BF16)
        dng = dy * (gate * sgt)
        dgain_ref[...] += _colsum(dng * nrm)
        dn = dng * gain
        do_scr[...] = r * (dn - nrm * _head_mean(dn * nrm))

        r64 = lax.broadcasted_iota(jnp.int32, (CHUNK, CHUNK), 0)
        c64 = lax.broadcasted_iota(jnp.int32, (CHUNK, CHUNK), 1)
        tril = r64 >= c64
        tri = tril.astype(BF16)
        triu = (r64 <= c64).astype(BF16)
        dst = dst_scr[...]
        dlb = dlb_scr[...]
        for cc in reversed(range(cpb)):
            sl = slice(cc * CHUNK, (cc + 1) * CHUNK)
            qp, sq, qf, sg, f, kf, v, b, bm, bl, qt, kt = _hgrn_chunk(q_ref, f_ref, i_ref, sl, lb, tri)
            stf = st_ref[cc].astype(F32)
            st_end = (st_ref[cc + 1] if cc + 1 < cpb else stn_ref[0]).astype(F32)
            csum = jnp.sum(st_end * dst, axis=0, keepdims=True)
            doc = do_scr[sl, :]
            dob, dstb = doc.astype(BF16), dst.astype(BF16)
            eb = jnp.exp(b)
            qe = (qf * eb).astype(BF16)
            kh = (kf * jnp.exp(bl - b)).astype(BF16)
            qtb, ktb = qt.astype(BF16), kt.astype(BF16)
            parts = []
            for h in range(HGRN_W // BLK):
                hs = slice(h * BLK, (h + 1) * BLK)
                da = jnp.where(tril, _dot3(doc[:, hs], v[:, hs], NT), 0.0)
                a = jnp.where(tril, _dot(qtb[:, hs], ktb[:, hs], NT), 0.0).astype(BF16)
                parts.append((
                    _dot3(da, kt[:, hs], NN), _dot3(doc[:, hs], stf[:, hs], NN),
                    _dot3(da, qt[:, hs], TN), _dot3(v[:, hs], dst[:, hs], NN),
                    _dot(a, dob[:, hs], TN) + _dot(kh[:, hs], dstb[:, hs], NT),
                    _dot(dob[:, hs], qe[:, hs], TN)))
            dqt, dqi, dkt, dks, dv, upd = (jnp.concatenate([p[n] for p in parts], axis=1) for n in range(6))
            dqf = dqt * jnp.exp(b - bm) + eb * dqi
            dkf = dkt * jnp.exp(bm - b) + jnp.exp(bl - b) * dks
            gq = qf * dqf - kf * dkf
            dlogf = csum + _cumsum_rows(triu, gq)
            dfv = dlogf / f - dkf
            dq_ref[sl, :] = (dqf * (sq * (1.0 + qp * (1.0 - sq)))).astype(BF16)
            df_ref[sl, :] = (dfv * (1.0 - lb) * sg * (1.0 - sg)).astype(BF16)
            di_ref[sl, :] = dv.astype(BF16)
            dst = dst * jnp.exp(bl) + upd
            dlb = dlb + _colsum(dfv * (1.0 - sg))
        dst_scr[...] = dst
        dlb_scr[...] = dlb

        @pl.when(step == nb - 1)
        def _():
            t = dlb * lb * (1.0 - lb)
            dlbl_ref[...] = jnp.concatenate([t, -t], axis=0)

    rev = lambda i: nb - 1 - i
    row = pl.BlockSpec((tb, HGRN_W), lambda i: (rev(i), 0))
    res = pl.pallas_call(
        body, name="hgrn_bwd", grid=(nb,),
        in_specs=_hgrn_specs(tb, rev) + [
            row, pl.BlockSpec((cpb, BLK, HGRN_W), lambda i: (rev(i), 0, 0)),
            pl.BlockSpec((1, BLK, HGRN_W), lambda i: (jnp.minimum((rev(i) + 1) * cpb, nc - 1), 0, 0)),
            pl.BlockSpec((tb, HGRN_W), lambda i: (rev(i), ATTN_W // HGRN_W)),
            pl.BlockSpec((2, HGRN_W), lambda i: (0, 0)), pl.BlockSpec((1, HGRN_W), lambda i: (0, 0))],
        out_specs=[row, row, row, row, pl.BlockSpec((1, HGRN_W), lambda i: (0, 0)),
                   pl.BlockSpec((2, HGRN_W), lambda i: (0, 0))],
        out_shape=[_sds((s, HGRN_W), BF16)] * 4 + [_sds((1, HGRN_W), F32), _sds((2, HGRN_W), F32)],
        scratch_shapes=[pltpu.VMEM((tb, HGRN_W), F32), pltpu.VMEM((BLK, HGRN_W), F32), pltpu.VMEM((1, HGRN_W), F32)],
        compiler_params=_cp(("arbitrary",)),
    )(proj, proj, proj, proj, o_pre, states, states, dcat, lb_logits, out_gain)
    return res


def _place():
    return lax.axis_index("x"), lax.axis_index("y"), lax.axis_index("c")


def _flip(x, y, ox, oy):
    return (1 - x if ox else x), (1 - y if oy else y)


def _half(rows, cc):
    return pl.ds(cc * (rows // 2), rows // 2)


def _remote(src, dst, ssem, rsem, to):
    return pltpu.make_async_remote_copy(src_ref=src, dst_ref=dst, send_sem=ssem, recv_sem=rsem,
                                        device_id=to, device_id_type=MESH)


def _ag_chip_copies(ins, outs, ssem, rsem):
    x, y, c = _place()
    j = 2 * x + y
    cps = []
    for k in range(len(ins)):
        rows = ins[k].shape[0]
        for idx, (ox, oy) in enumerate(FLIPS):
            px, py = _flip(x, y, ox, oy)
            cps.append(_remote(ins[k].at[_half(rows, c)], outs[k].at[j, _half(rows, c)],
                               ssem.at[k, idx], rsem.at[k, idx], (px, py, c)))
    return cps


def _ag_start(ins, outs, ssem, rsem):
    for cp in _ag_chip_copies(ins, outs, ssem, rsem):
        cp.start()


def _ag_finish(ins, outs, ssem, rsem):
    x, y, c = _place()
    sib = (x, y, 1 - c)
    passed = []
    for k in range(len(ins)):
        rows = ins[k].shape[0]
        for idx, (ox, oy) in enumerate(FLIPS):
            px, py = _flip(x, y, ox, oy)
            blk = outs[k].at[2 * px + py, _half(rows, c)]
            _remote(blk, blk, ssem.at[k, idx], rsem.at[k, idx], (px, py, c)).wait_recv()
            cp = _remote(blk, blk, ssem.at[k, 3 + idx], rsem.at[k, 3 + idx], sib)
            cp.start()
            passed.append(cp)
    for k in range(len(ins)):
        rows = ins[k].shape[0]
        for idx, (ox, oy) in enumerate(FLIPS):
            px, py = _flip(x, y, ox, oy)
            blk = outs[k].at[2 * px + py, _half(rows, 1 - c)]
            _remote(blk, blk, ssem.at[k, 3 + idx], rsem.at[k, 3 + idx], sib).wait_recv()
    for cp in _ag_chip_copies(ins, outs, ssem, rsem) + passed:
        cp.wait_send()


def _ag_shapes(shards):
    nk = len(shards)
    return ([_sds((N_CHIPS,) + tuple(w.shape), w.dtype) for w in shards],
            [pltpu.SemaphoreType.DMA((nk, 6)), pltpu.SemaphoreType.DMA((nk, 6))])


def _with_own(gathered, shard, j):
    return lax.dynamic_update_index_in_dim(gathered, shard, j, 0)


def _rs_pair_copies(ins, outs, ssem, rsem):
    x, y, c = _place()
    return [_remote(ins[k].at[:, _half(ins[k].shape[1], 1 - c)], outs[k], ssem.at[k], rsem.at[k], (x, y, 1 - c))
            for k in range(len(ins))]


def _rs_pair_start(ins, outs, ssem, rsem):
    for cp in _rs_pair_copies(ins, outs, ssem, rsem):
        cp.start()


def _rs_pair_finish(ins, outs, ssem, rsem):
    for cp in _rs_pair_copies(ins, outs, ssem, rsem):
        cp.wait()


def _rs_pair_exchange(grads):
    nk = len(grads)
    return (grads, [_sds((N_CHIPS, g.shape[1] // 2, g.shape[2]), g.dtype) for g in grads],
            [pltpu.SemaphoreType.DMA((nk,)), pltpu.SemaphoreType.DMA((nk,))], _rs_pair_start, _rs_pair_finish)


def _rs_pair(name, grads):
    nk = len(grads)
    ins, out_shape, sems, start, finish = _rs_pair_exchange(grads)

    def body(*refs):
        start(refs[:nk], refs[nk:2 * nk], *refs[2 * nk:])
        finish(refs[:nk], refs[nk:2 * nk], *refs[2 * nk:])

    return pl.pallas_call(body, name=name, in_specs=[ANY] * nk, out_specs=[ANY] * nk, out_shape=out_shape,
                          scratch_shapes=sems)(*ins)


def _rs_chip_copies(ins, outs, ssem, rsem):
    x, y, c = _place()
    cps = []
    for k in range(len(ins)):
        for idx, (ox, oy) in enumerate(FLIPS):
            px, py = _flip(x, y, ox, oy)
            cps.append(_remote(ins[k].at[2 * px + py], outs[k].at[idx], ssem.at[k, idx], rsem.at[k, idx], (px, py, c)))
    return cps


def _rs_chips_start(ins, outs, ssem, rsem):
    for cp in _rs_chip_copies(ins, outs, ssem, rsem):
        cp.start()


def _rs_chips_finish(ins, outs, ssem, rsem):
    for cp in _rs_chip_copies(ins, outs, ssem, rsem):
        cp.wait()


def _rs_chips_shapes(psums):
    nk = len(psums)
    return ([_sds((3,) + tuple(p.shape[1:]), p.dtype) for p in psums],
            [pltpu.SemaphoreType.DMA((nk, 3)), pltpu.SemaphoreType.DMA((nk, 3))])


def _rs_share(fulls):
    nk = len(fulls)

    def body(*refs):
        ins, outs = refs[:nk], refs[nk:2 * nk]
        ssem, rsem = refs[2 * nk:]
        x, y, c = _place()
        cps = []
        for k in range(nk):
            rows = fulls[k].shape[0]
            cp = _remote(ins[k].at[_half(rows, c)], outs[k].at[_half(rows, c)], ssem.at[k], rsem.at[k], (x, y, 1 - c))
            cp.start()
            cps.append(cp)
        for k, cp in enumerate(cps):
            rows = fulls[k].shape[0]
            cp.wait_send()
            theirs = outs[k].at[_half(rows, 1 - c)]
            _remote(theirs, theirs, ssem.at[k], rsem.at[k], (x, y, 1 - c)).wait_recv()

    return pl.pallas_call(
        body, name="rs_share", in_specs=[ANY] * nk, out_specs=[ANY] * nk,
        out_shape=[_sds(f.shape, f.dtype) for f in fulls], input_output_aliases={k: k for k in range(nk)},
        scratch_shapes=[pltpu.SemaphoreType.DMA((nk,)), pltpu.SemaphoreType.DMA((nk,))],
    )(*fulls)


def _allreduce_small(v):
    ndev = 8

    def body(in_ref, out_ref, buf, ssem, rsem):
        x, y, c = _place()
        me = 4 * x + 2 * y + c
        buf[me] = in_ref[...]
        cps = []
        for k in range(1, ndev):
            ox, oy, oc = (k >> 2) & 1, (k >> 1) & 1, k & 1
            px, py = _flip(x, y, ox, oy)
            pc = 1 - c if oc else c
            cp = pltpu.make_async_remote_copy(src_ref=in_ref, dst_ref=buf.at[me], send_sem=ssem.at[k - 1],
                                              recv_sem=rsem.at[k - 1], device_id=(px, py, pc), device_id_type=MESH)
            cp.start()
            cps.append((cp, 4 * px + 2 * py + pc, (px, py, pc)))
        for k, (cp, src, peer) in enumerate(cps):
            cp.wait_send()
            pltpu.make_async_remote_copy(src_ref=in_ref, dst_ref=buf.at[src], send_sem=ssem.at[k],
                                         recv_sem=rsem.at[k], device_id=peer, device_id_type=MESH).wait_recv()
        acc = buf[0]
        for i in range(1, ndev):
            acc = acc + buf[i]
        out_ref[...] = acc

    return pl.pallas_call(
        body, name="allreduce_small",
        in_specs=[pl.BlockSpec(memory_space=pltpu.VMEM)], out_specs=pl.BlockSpec(memory_space=pltpu.VMEM),
        out_shape=_sds(v.shape, v.dtype),
        scratch_shapes=[pltpu.VMEM((ndev,) + v.shape, v.dtype), pltpu.SemaphoreType.DMA((ndev - 1,)),
                        pltpu.SemaphoreType.DMA((ndev - 1,))],
    )(v)


def _rs_sum1(name, g, recv, c_idx):
    _, r, cdim = g.shape
    hr = r // 2
    tr = min(hr, 256)
    nr = hr // tr

    def body(c_ref, g_ref, r_ref, o32_ref, o16_ref):
        v = g_ref[...] + r_ref[...].astype(F32)
        o32_ref[...] = v
        o16_ref[...] = v.astype(BF16)

    spec = pl.BlockSpec((None, tr, cdim), lambda j, i, c_ref: (j, i, 0))
    return pl.pallas_call(
        body, name=name,
        grid_spec=pltpu.PrefetchScalarGridSpec(
            num_scalar_prefetch=1, grid=(N_CHIPS, nr),
            in_specs=[pl.BlockSpec((None, tr, cdim), lambda j, i, c_ref: (j, c_ref[0] * nr + i, 0)), spec],
            out_specs=[spec, spec]),
        out_shape=[_sds((N_CHIPS, hr, cdim), F32), _sds((N_CHIPS, hr, cdim), BF16)],
        compiler_params=_cp(("parallel", "parallel")),
    )(c_idx, g, recv)


def _rs_sum2(name, p32, recv, jc_idx):
    _, hr, cdim = p32.shape
    tr = min(hr, 256)
    nr = hr // tr

    def body(jc_ref, p_ref, r_ref, o_ref):
        o_ref[...] = ((p_ref[...] + r_ref[0].astype(F32)) + r_ref[1].astype(F32)) + r_ref[2].astype(F32)

    return pl.pallas_call(
        body, name=name,
        grid_spec=pltpu.PrefetchScalarGridSpec(
            num_scalar_prefetch=1, grid=(nr,),
            in_specs=[pl.BlockSpec((None, tr, cdim), lambda i, jc: (jc[0], i, 0)),
                      pl.BlockSpec((3, tr, cdim), lambda i, jc: (0, i, 0))],
            out_specs=pl.BlockSpec((tr, cdim), lambda i, jc: (jc[1] * nr + i, 0))),
        out_shape=_sds((2 * hr, cdim), F32),
        compiler_params=_cp(("parallel",)),
    )(jc_idx, p32, recv)


def _adamw(name, w, g, m, v):
    r, cdim = w.shape
    tr = min(r, 256)
    c1 = 1.0 - ADAM_B1 ** ADAM_STEP
    c2 = 1.0 - ADAM_B2 ** ADAM_STEP

    def body(w_ref, g_ref, m_ref, v_ref, d_ref, nm_ref, nv_ref):
        gv = g_ref[...]
        nm = ADAM_B1 * m_ref[...] + (1.0 - ADAM_B1) * gv
        nv = ADAM_B2 * v_ref[...] + (1.0 - ADAM_B2) * (gv * gv)
        d_ref[...] = -ADAM_LR * ((nm / c1) / (jnp.sqrt(nv / c2) + ADAM_EPS) + ADAM_WD * w_ref[...])
        nm_ref[...] = nm
        nv_ref[...] = nv

    spec = pl.BlockSpec((tr, cdim), lambda i: (i, 0))
    return pl.pallas_call(
        body, name=name, grid=(r // tr,), in_specs=[spec] * 4, out_specs=[spec] * 3,
        out_shape=[_sds((r, cdim), F32)] * 3, compiler_params=_cp(("parallel",)),
    )(w, g, m, v)


def _pack_small(mix_pre, attn_out, lb_logits, hgrn_out, mix_post, mlp_pre, mlp_post):
    rows = [mix_pre, jnp.concatenate([attn_out, hgrn_out], axis=1),
            jnp.concatenate([lb_logits[0:1], lb_logits[1:2]], axis=1), mix_post, mlp_pre, mlp_post,
            jnp.zeros((2, D_MODEL), F32)]
    return jnp.concatenate(rows, axis=0)


def _unpack_small(p):
    return (p[0:1], p[1:2, :ATTN_W], jnp.concatenate([p[2:3, :HGRN_W], p[2:3, HGRN_W:]], axis=0),
            p[1:2, ATTN_W:], p[3:4], p[4:5], p[5:6])


def kernel(x, mix_pre_norm, w_in, attn_out_norm, hgrn_lb_logits, hgrn_out_norm, w_out, mix_post_norm, mlp_pre_norm, w_ff1, w_ff2, mlp_post_norm, loss_target, m_mix_pre_norm, m_w_in, m_attn_out_norm, m_hgrn_lb_logits, m_hgrn_out_norm, m_w_out, m_mix_post_norm, m_mlp_pre_norm, m_w_ff1, m_w_ff2, m_mlp_post_norm, v_mix_pre_norm, v_w_in, v_attn_out_norm, v_hgrn_lb_logits, v_hgrn_out_norm, v_w_out, v_mix_post_norm, v_mlp_pre_norm, v_w_ff1, v_w_ff2, v_mlp_post_norm):
    s = x.shape[1]
    xs = x.reshape(s, D_MODEL)
    tgt = loss_target.reshape(s, D_MODEL)
    cx, cy, cc = _place()
    chip = 2 * cx + cy
    c_idx = jnp.reshape(cc, (1,)).astype(jnp.int32)
    jc_idx = jnp.stack([chip, cc]).astype(jnp.int32)

    big_w = [w_in[0], w_out[0], w_ff1[0], w_ff2[0]]
    big_m = [m_w_in[0], m_w_out[0], m_w_ff1[0], m_w_ff2[0]]
    big_v = [v_w_in[0], v_w_out[0], v_w_ff1[0], v_w_ff2[0]]
    shards = [w.astype(BF16) for w in big_w]

    h, wg_in = _rows_call("norm_in", lambda xv, g: ((xv * _rstd(xv) * g),),
                          [(xs, _row(D_MODEL)), (mix_pre_norm, "full")], [(D_MODEL, BF16, "row")], s,
                          exchange=(shards[:1], *_ag_shapes(shards[:1]), _ag_start, _ag_finish))
    wg_in = _with_own(wg_in, shards[0], chip)
    (proj,) = _mm_cols("mm_proj", h, wg_in, NN, [F32])
    hg_o, rec, states = _hgrn_fwd(proj, hgrn_lb_logits, hgrn_out_norm)
    attn_o, attn_lse, wg_out, wg_1, wg_2 = _attn_fwd(proj, shards[1:])
    wg_out, wg_1, wg_2 = (_with_own(g, w, chip) for g, w in zip((wg_out, wg_1, wg_2), shards[1:]))
    (attn_n,) = _rows_call("attn_norm", lambda o, gain: (o * _rstd(o) * gain,),
                           [(attn_o, _row(ATTN_W)), (attn_out_norm, "full")], [(ATTN_W, BF16, "row")], s)
    cat = jnp.concatenate([attn_n, rec], axis=1)

    def post1(mv, xv, g_post, g_pre2):
        x1 = xv + mv * _rstd(mv) * g_post
        return mv, x1, x1 * _rstd(x1) * g_pre2

    mixed, x1, h2 = _rows_call(
        "mm_mixed", post1, [(xs, _row(D_MODEL)), (mix_post_norm, "full"), (mlp_pre_norm, "full")],
        [(D_MODEL, F32, "row"), (D_MODEL, F32, "row"), (D_MODEL, BF16, "row")], s,
        matmul=(cat, wg_out.reshape(D_MODEL, D_MODEL)))

    def sq_relu(u):
        r = jnp.maximum(u, 0.0)
        return (r * r,)

    (act,) = _mm_cols("mm_ff1", h2, wg_1, NN, [BF16], epi=sq_relu)

    def post2(fv, x1v, tv, g):
        y = x1v + fv * _rstd(fv) * g
        dy = (y - tv) * (1.0 / D_MODEL)
        err = y - tv
        loss = 0.5 * jnp.sum(jnp.mean(err * err, axis=-1, keepdims=True), axis=0, keepdims=True)
        dff, dgc = _norm_bwd(fv, g, dy)
        return dy, dff, _colsum(dgc), jnp.broadcast_to(loss, (1, BLK))

    dy, dff, g_mlp_post, loss_part = _rows_call(
        "mm_ff2", post2, [(x1, _row(D_MODEL)), (tgt, _row(D_MODEL)), (mlp_post_norm, "full")],
        [(D_MODEL, F32, "row"), (D_MODEL, BF16, "row"), (D_MODEL, F32, "acc"), (BLK, F32, "acc")], s,
        tm=ROW_TILE // 2, matmul=(act, wg_2.reshape(D_FF, D_MODEL)))

    (du,) = _mm_cols("mm_du", dff, wg_2, NT, [BF16], epi=lambda acc, a: (acc * (2.0 * jnp.sqrt(a.astype(F32))),),
                     extras=(act,))
    gw_2 = _mm_wgrad("mm_gw2", act, dff, True)
    gw_1 = _mm_wgrad("mm_gw1", h2, du, False)

    def bwd_mid(dh2v, dyv, x1v, mv, g_pre2, g_post):
        d1, gc1 = _norm_bwd(x1v, g_pre2, dh2v)
        dx1 = dyv + d1
        dm, gc2 = _norm_bwd(mv, g_post, dx1)
        return dx1, dm, _colsum(gc1), _colsum(gc2)

    w1_t = jnp.transpose(wg_1, (0, 2, 1)).reshape(D_FF, D_MODEL)
    dx1, dmixed, g_mlp_pre, g_mix_post, *from_pair = _rows_call(
        "mm_dh2", bwd_mid, [(dy, _row(D_MODEL)), (x1, _row(D_MODEL)), (mixed, _row(D_MODEL)),
                            (mlp_pre_norm, "full"), (mix_post_norm, "full")],
        [(D_MODEL, F32, "row"), (D_MODEL, BF16, "row"), (D_MODEL, F32, "acc"), (D_MODEL, F32, "acc")], s,
        tm=ROW_TILE // 2, matmul=(du, w1_t), exchange=_rs_pair_exchange([gw_1[1], gw_2[1]]))

    (dcat,) = _mm_cols("mm_dcat", dmixed, wg_out, NT, [F32])
    gw_out = _mm_wgrad("mm_gwout", cat, dmixed, True)
    names = ["out", "ff1", "ff2", "in"]
    ready = [gw_out, gw_1, gw_2]
    from_pair = list(_rs_pair("rs_pair_out", [gw_out[1]])) + from_pair
    pair = [_rs_sum1(f"rs_sum1_{n}", g[0], r, c_idx) for n, g, r in zip(names, ready, from_pair)]

    def attn_norm_bwd(dc, o, gain):
        do, gc = _norm_bwd(o, gain, dc)
        t = do * o
        lane = lax.broadcasted_iota(jnp.int32, (t.shape[0], BLK), 1) < 64
        parts = []
        for p in range(ATTN_W // BLK):
            tp = t[:, p * BLK:(p + 1) * BLK]
            sa = jnp.sum(jnp.where(lane, tp, 0.0), axis=1, keepdims=True)
            sb = jnp.sum(jnp.where(lane, 0.0, tp), axis=1, keepdims=True)
            parts.append(jnp.where(lane, sa, sb))
        return do, jnp.concatenate(parts, axis=1), _colsum(gc)

    do_attn, delta, g_attn_out = _rows_call(
        "attn_norm_bwd", attn_norm_bwd, [(dcat, _row(ATTN_W, 0)), (attn_o, _row(ATTN_W)), (attn_out_norm, "full")],
        [(ATTN_W, F32, "row"), (ATTN_W, F32, "row"), (ATTN_W, F32, "acc")], s)
    dq, dk, dv, *from_chips = _attn_bwd(proj, do_attn, attn_lse, delta, [p[1] for p in pair])
    dhq, dhf, dhi, dhg, g_hgrn_out, g_lb = _hgrn_bwd(proj, hg_o, states, dcat, hgrn_lb_logits, hgrn_out_norm)

    def dproj_asm(*a):
        return (jnp.concatenate([t.astype(BF16) for t in a], axis=1),)

    (dproj,) = _rows_call("dproj_asm", dproj_asm,
                          [(t, _row(ATTN_W)) for t in (dq, dk, dv)] + [(t, _row(HGRN_W)) for t in (dhq, dhf, dhi, dhg)],
                          [(PROJ_W, BF16, "row")], s)
    gw_in = _mm_wgrad("mm_gwin", h, dproj, False)
    (from_pair_in,) = _rs_pair("rs_pair_in", [gw_in[1]])
    pair.append(_rs_sum1("rs_sum1_in", gw_in[0], from_pair_in, c_idx))
    rs_shape, rs_sems = _rs_chips_shapes([pair[3][1]])

    def bwd_in(dhv, dx1v, xv, g):
        d0, gc = _norm_bwd(xv, g, dhv)
        return dx1v + d0, _colsum(gc)

    w_in_t = jnp.transpose(wg_in, (0, 2, 1)).reshape(PROJ_W, D_MODEL)
    grad_x, g_mix_pre, from_chips_in = _rows_call(
        "mm_dh", bwd_in, [(dx1, _row(D_MODEL)), (xs, _row(D_MODEL)), (mix_pre_norm, "full")],
        [(D_MODEL, F32, "row"), (D_MODEL, F32, "acc")], s, tm=ROW_TILE // 2, matmul=(dproj, w_in_t),
        exchange=([pair[3][1]], rs_shape, rs_sems, _rs_chips_start, _rs_chips_finish))
    from_chips.append(from_chips_in)

    loss = lax.psum(loss_part[0, 0], ("x", "y", "c"))
    small_g = _allreduce_small(_pack_small(g_mix_pre, g_attn_out, g_lb, g_hgrn_out, g_mix_post, g_mlp_pre, g_mlp_post))

    reduced = [_rs_sum2(f"rs_sum2_{n}", p[0], r, jc_idx) for n, p, r in zip(names, pair, from_chips)]
    g_wout, g_w1, g_w2, g_win = _rs_share(reduced)
    full = [g_win, g_wout, g_w1, g_w2]

    upd = [_adamw(f"adamw_{n}", w, g, m, v) for n, w, g, m, v in zip(("in", "out", "ff1", "ff2"), big_w, full, big_m, big_v)]
    small_w = _pack_small(mix_pre_norm, attn_out_norm, hgrn_lb_logits, hgrn_out_norm, mix_post_norm, mlp_pre_norm,
                          mlp_post_norm)
    small_m = _pack_small(m_mix_pre_norm, m_attn_out_norm, m_hgrn_lb_logits, m_hgrn_out_norm, m_mix_post_norm,
                          m_mlp_pre_norm, m_mlp_post_norm)
    small_v = _pack_small(v_mix_pre_norm, v_attn_out_norm, v_hgrn_lb_logits, v_hgrn_out_norm, v_mix_post_norm,
                          v_mlp_pre_norm, v_mlp_post_norm)
    small_upd = _adamw("adamw_small", small_w, small_g, small_m, small_v)

    def assemble(small, big):
        sm = _unpack_small(small)
        return (sm[0], big[0][None], sm[1], sm[2], sm[3], big[1][None], sm[4], sm[5], big[2][None], big[3][None], sm[6])

    g_out = assemble(small_g, full)
    d_out = assemble(small_upd[0], [u[0] for u in upd])
    m_out = assemble(small_upd[1], [u[1] for u in upd])
    v_out = assemble(small_upd[2], [u[2] for u in upd])
    return (loss, grad_x.reshape(x.shape), *g_out, *d_out, *m_out, *v_out)
```

```python
import numpy as np
import jax
import jax.numpy as jnp
from jax import lax
from jax.experimental import pallas as pl
from jax.experimental.pallas import tpu as pltpu

F32 = jnp.float32
BF16 = jnp.bfloat16
MESH = pl.DeviceIdType.MESH
ANY = pl.BlockSpec(memory_space=pl.ANY)

RMS_EPS = 1e-6
D_MODEL = 1024
ATTN_W = 512
HGRN_W = 512
PROJ_W = 3584
D_FF = 4096
N_CHIPS = 4
BLK = 128
CHUNK = 64
HGRN_TB = 512
ATTN_GROUP = 8
DILATIONS = (1, 4, 16)
ATTN_SCALE = 0.125
ROW_TILE = 512
MM_TILE = 1024
VMEM_LIMIT = 48 * 2 ** 20
FLIPS = ((1, 0), (0, 1), (1, 1))

ADAM_LR, ADAM_B1, ADAM_B2, ADAM_EPS, ADAM_WD, ADAM_STEP = 0.001, 0.9, 0.999, 1e-08, 0.01, 10


def _cp(sem=None):
    return pltpu.CompilerParams(dimension_semantics=sem, vmem_limit_bytes=VMEM_LIMIT)


def _sigmoid(v):
    return 1.0 / (1.0 + jnp.exp(-v))


def _dot(a, b, contract, precision=None):
    return lax.dot_general(a, b, (contract, ((), ())), preferred_element_type=F32, precision=precision)


NN = ((1,), (0,))
NT = ((1,), (1,))
TN = ((0,), (0,))


def _matmul(name, a, b, *, grid, a_spec, b_spec, contract, outs, epi=None, extras=(), extra_specs=(), row_split=1):
    n_ex = len(extras)

    def body(a_ref, b_ref, *rest):
        ex, out_refs = rest[:n_ex], rest[n_ex:]
        part = a_ref.shape[0] // row_split
        for h in range(row_split):
            rows = slice(h * part, (h + 1) * part) if row_split > 1 else slice(None)
            acc = _dot(a_ref[rows, :], b_ref[...], contract)
            res = epi(acc, *[e[rows, :] for e in ex]) if epi else (acc,)
            for o, r in zip(out_refs, res):
                o[rows, :] = r.astype(o.dtype)

    return pl.pallas_call(
        body, name=name, grid=grid,
        in_specs=[a_spec, b_spec, *extra_specs],
        out_specs=[s for _, s in outs],
        out_shape=[s for s, _ in outs],
        compiler_params=_cp(("parallel",) * len(grid)),
    )(a, b, *extras)


def _sds(shape, dtype):
    return jax.ShapeDtypeStruct(shape, dtype)


def _mm_cols(name, a, w, contract, out_dtypes, epi=None, extras=()):
    m, k = a.shape
    jn = w.shape[0]
    nj = w.shape[2] if contract == NN else w.shape[1]
    tm = min(m, MM_TILE)
    outs = [(_sds((m, jn * nj), dt), pl.BlockSpec((tm, nj), lambda j, i: (i, j))) for dt in out_dtypes]
    return _matmul(name, a, w, grid=(jn, m // tm),
                   a_spec=pl.BlockSpec((tm, k), lambda j, i: (i, 0)),
                   b_spec=pl.BlockSpec((None,) + w.shape[1:], lambda j, i: (j, 0, 0)),
                   contract=contract, outs=outs, epi=epi, extras=extras,
                   extra_specs=[pl.BlockSpec((tm, nj), lambda j, i: (i, j)) for _ in extras], row_split=2)


def _mm_wgrad(name, a, b, a_by_j):
    s = a.shape[0]
    if a_by_j:
        r, c = a.shape[1] // N_CHIPS, b.shape[1]
    else:
        r, c = a.shape[1], b.shape[1] // N_CHIPS
    tr = min(r, 512)
    nr = r // tr
    if a_by_j:
        a_spec = pl.BlockSpec((s, tr), lambda j, i: (0, j * nr + i))
        b_spec = pl.BlockSpec((s, c), lambda j, i: (0, 0))
    else:
        a_spec = pl.BlockSpec((s, tr), lambda j, i: (0, i))
        b_spec = pl.BlockSpec((s, c), lambda j, i: (0, j))
    outs = [(_sds((N_CHIPS, r, c), dt), pl.BlockSpec((None, tr, c), lambda j, i: (j, i, 0))) for dt in (F32, BF16)]
    return _matmul(name, a, b, grid=(N_CHIPS, nr), a_spec=a_spec, b_spec=b_spec, contract=TN, outs=outs,
                   epi=lambda acc: (acc, acc))


def _rows_call(name, fn, ins, outs, s, tm=ROW_TILE, matmul=None, exchange=None):
    in_specs = []
    if matmul:
        a, w = matmul
        in_specs += [pl.BlockSpec((tm, a.shape[1]), lambda i: (i, 0)),
                     pl.BlockSpec(w.shape, lambda i: (0, 0), pipeline_mode=pl.Buffered(1))]
    for arr, kind in ins:
        if kind == "full":
            in_specs.append(pl.BlockSpec(arr.shape, lambda i: (0, 0)))
        else:
            _, w_, cb = kind
            in_specs.append(pl.BlockSpec((tm, w_), lambda i, cb=cb: (i, cb)))
    out_specs, out_shape, is_acc = [], [], []
    for w_, dt, kind in outs:
        if kind == "acc":
            out_specs.append(pl.BlockSpec((1, w_), lambda i: (0, 0)))
            out_shape.append(_sds((1, w_), dt))
        else:
            out_specs.append(pl.BlockSpec((tm, w_), lambda i: (i, 0)))
            out_shape.append(_sds((s, w_), dt))
        is_acc.append(kind == "acc")
    n_mm, n_in, n_out = (2 if matmul else 0), len(ins), len(outs)
    x_ins, x_shapes, x_sems, x_start, x_finish = exchange if exchange else ((), [], [], None, None)
    n_x = len(x_ins)
    steps = s // tm

    def body(*refs):
        in_refs, xi = refs[n_mm:n_mm + n_in], refs[n_mm + n_in:n_mm + n_in + n_x]
        out_refs = refs[n_mm + n_in + n_x:n_mm + n_in + n_x + n_out]
        xo, sems = refs[n_mm + n_in + n_x + n_out:n_mm + n_in + 2 * n_x + n_out], refs[n_mm + n_in + 2 * n_x + n_out:]
        i = pl.program_id(0)

        if exchange:
            @pl.when(i == 0)
            def _():
                x_start(xi, xo, *sems)

        for o, acc in zip(out_refs, is_acc):
            if acc:
                @pl.when(i == 0)
                def _(o=o):
                    o[...] = jnp.zeros_like(o)

        parts = 2 if matmul else 1
        for h in range(parts):
            rows = slice(h * (tm // parts), (h + 1) * (tm // parts))
            args = [r[...] if kind == "full" else r[rows, :] for r, (_, kind) in zip(in_refs, ins)]
            if matmul:
                args.insert(0, _dot(refs[0][rows, :], refs[1][...], NN))
            for o, r, acc in zip(out_refs, fn(*args), is_acc):
                if acc:
                    o[...] += r.astype(o.dtype)
                else:
                    o[rows, :] = r.astype(o.dtype)

        if exchange:
            @pl.when(i == steps - 1)
            def _():
                x_finish(xi, xo, *sems)

    sem = ("arbitrary",) if any(is_acc) or exchange else ("parallel",)
    return pl.pallas_call(
        body, name=name, grid=(steps,), in_specs=in_specs + [ANY] * n_x, out_specs=out_specs + [ANY] * n_x,
        out_shape=out_shape + list(x_shapes), scratch_shapes=list(x_sems), compiler_params=_cp(sem),
    )(*(matmul or ()), *[a for a, _ in ins], *x_ins)


def _rstd(v):
    return lax.rsqrt(jnp.mean(v * v, axis=-1, keepdims=True) + RMS_EPS)


def _norm_bwd(v, gain, dy):
    r = _rstd(v)
    n = v * r
    dn = dy * gain
    dv = r * (dn - n * jnp.mean(dn * n, axis=-1, keepdims=True))
    return dv, dy * n


def _colsum(v):
    return jnp.sum(v, axis=0, keepdims=True)


def _row(w, cb=0):
    return ("row", w, cb)


N_PAIRS = ATTN_W // BLK


def _head_col(v, mask):
    return jnp.max(jnp.where(mask, v, -jnp.inf), axis=1, keepdims=True)


def _slopes():
    t = np.zeros((N_PAIRS, 8, 2 * BLK), np.float32)
    for p in range(N_PAIRS):
        for hh in range(2):
            t[p, hh, :] = 2.0 ** -(2 * p + hh + 1)
    return jnp.asarray(t)


def _rows(n, r, d):
    base = pl.multiple_of(n * (BLK * d), BLK)
    return pl.ds(base + r, BLK, stride=d) if d > 1 else pl.ds(base, BLK)


def _attn_bias(sl_ref, bias_scr):
    row = lax.broadcasted_iota(jnp.int32, (BLK, 2 * BLK), 0)
    col = lax.broadcasted_iota(jnp.int32, (BLK, 2 * BLK), 1)
    dist = row + BLK - col
    in_window = (dist >= 0) & (dist <= BLK)
    distf = dist.astype(F32)
    for di, d in enumerate(DILATIONS):
        for hh in range(2):
            bias_scr[2 * di + hh] = jnp.where(in_window, -(sl_ref[hh:hh + 1, :] * float(d)) * distf, -1e30)


def _first_block_penalty(n):
    col = lax.broadcasted_iota(jnp.int32, (1, 2 * BLK), 1)
    return jnp.where(col + n * BLK >= BLK, 0.0, -1e30)


def _attn_groups(s, d):
    nb = s // (BLK * d)
    g = ATTN_GROUP
    if d >= g:
        return [(nb, lambda n, r0=r0: [(n, r0 + u) for u in range(g)]) for r0 in range(0, d, g)]
    per = g // d
    return [(nb // per, lambda t: [(per * t + u, r) for u in range(per) for r in range(d)])]


def _attn_fwd(proj, shards):
    s = proj.shape[0]
    nk = len(shards)

    def body(sl_ref, q_ref, k_ref, v_ref, *rest):
        w_refs, (o_ref, l_ref) = rest[:nk], rest[nk:nk + 2]
        wg_refs, (bias_scr, ssem, rsem) = rest[nk + 2:2 * nk + 2], rest[2 * nk + 2:]
        pair = pl.program_id(0)

        @pl.when(pair == 0)
        def _():
            _ag_start(w_refs, wg_refs, ssem, rsem)

        _attn_bias(sl_ref, bias_scr)
        lane_q = lax.broadcasted_iota(jnp.int32, (BLK, BLK), 1) < 64
        lane_k = lax.broadcasted_iota(jnp.int32, (2 * BLK, BLK), 1) < 64

        def branch(n, r, di):
            d = DILATIONS[di]
            rows = _rows(n, r, d)
            prev = _rows(jnp.maximum(n - 1, 0), r, d)
            pen = _first_block_penalty(n)
            q2 = q_ref[rows, :] * ATTN_SCALE
            kk = jnp.concatenate([k_ref[prev, :], k_ref[rows, :]], axis=0).astype(BF16)
            vv = jnp.concatenate([v_ref[prev, :], v_ref[rows, :]], axis=0)
            o2 = jnp.zeros((BLK, BLK), F32)
            lse2 = jnp.zeros((BLK, BLK), F32)
            for hh in range(2):
                mq = lane_q if hh == 0 else ~lane_q
                mk = lane_k if hh == 0 else ~lane_k
                qm = jnp.where(mq, q2, 0.0).astype(BF16)
                sc = _dot(qm, kk, NT) + bias_scr[2 * di + hh] + pen
                m = jnp.max(sc, axis=1, keepdims=True)
                pr = jnp.exp(sc - m)
                den = jnp.sum(pr, axis=1, keepdims=True)
                vm = jnp.where(mk, vv, 0.0).astype(BF16)
                o2 = o2 + _dot(pr.astype(BF16), vm, NN) / den
                lse2 = jnp.where(mq, m + jnp.log(den), lse2)
            return rows, o2, lse2

        def merge(rows, o2, lse2, first):
            if first:
                o_ref[rows, :] = o2
                l_ref[rows, :] = lse2
            else:
                lo = l_ref[rows, :]
                mx = jnp.maximum(lo, lse2)
                ln = mx + jnp.log(jnp.exp(lo - mx) + jnp.exp(lse2 - mx))
                o_ref[rows, :] = jnp.exp(lo - ln) * o_ref[rows, :] + jnp.exp(lse2 - ln) * o2
                l_ref[rows, :] = ln

        for di, d in enumerate(DILATIONS):
            for trips, blocks in _attn_groups(s, d):
                def trip(t, carry, di=di, blocks=blocks):
                    done = [branch(n, r, di) for n, r in blocks(t)]
                    for rows, o2, lse2 in done:
                        merge(rows, o2, lse2, di == 0)
                    return carry

                lax.fori_loop(0, trips, trip, 0)

        @pl.when(pair == N_PAIRS - 1)
        def _():
            _ag_finish(w_refs, wg_refs, ssem, rsem)

    cb = lambda base: pl.BlockSpec((s, BLK), lambda p, base=base: (0, base + p))
    out = pl.BlockSpec((s, BLK), lambda p: (0, p))
    ag_shape, ag_sems = _ag_shapes(shards)
    return pl.pallas_call(
        body, name="attn_fwd", grid=(N_PAIRS,),
        in_specs=[pl.BlockSpec((None, 8, 2 * BLK), lambda p: (p, 0, 0)), cb(0), cb(N_PAIRS), cb(2 * N_PAIRS)]
        + [ANY] * nk,
        out_specs=[out, out] + [ANY] * nk, out_shape=[_sds((s, ATTN_W), F32)] * 2 + ag_shape,
        scratch_shapes=[pltpu.VMEM((2 * len(DILATIONS), BLK, 2 * BLK), F32)] + ag_sems,
        compiler_params=_cp(("arbitrary",)),
    )(_slopes(), proj, proj, proj, *shards)


def _attn_bwd(proj, do, lse, delta, psums):
    s = proj.shape[0]
    nk = len(psums)

    def body(sl_ref, q_ref, k_ref, v_ref, do_ref, l_ref, e_ref, *rest):
        p_refs, (dq_ref, dk_ref, dv_ref) = rest[:nk], rest[nk:nk + 3]
        got_refs, (bias_scr, ssem, rsem) = rest[nk + 3:2 * nk + 3], rest[2 * nk + 3:]
        pair = pl.program_id(0)

        @pl.when(pair == 0)
        def _():
            _rs_chips_start(p_refs, got_refs, ssem, rsem)

        _attn_bias(sl_ref, bias_scr)
        lane_q = lax.broadcasted_iota(jnp.int32, (BLK, BLK), 1) < 64
        lane_k = lax.broadcasted_iota(jnp.int32, (2 * BLK, BLK), 1) < 64
        dk_ref[...] = jnp.zeros_like(dk_ref)
        dv_ref[...] = jnp.zeros_like(dv_ref)

        def branch(n, r, di):
            d = DILATIONS[di]
            rows = _rows(n, r, d)
            prev = _rows(jnp.maximum(n - 1, 0), r, d)
            pen = _first_block_penalty(n)
            q1, d1, l1, e1 = q_ref[rows, :] * ATTN_SCALE, do_ref[rows, :], l_ref[rows, :], e_ref[rows, :]
            kk = jnp.concatenate([k_ref[prev, :], k_ref[rows, :]], axis=0)
            kkb = kk.astype(BF16)
            vvb = jnp.concatenate([v_ref[prev, :], v_ref[rows, :]], axis=0).astype(BF16)
            dq2 = jnp.zeros((BLK, BLK), F32)
            dkk = jnp.zeros((2 * BLK, BLK), F32)
            dvv = jnp.zeros((2 * BLK, BLK), F32)
            for hh in range(2):
                mq = lane_q if hh == 0 else ~lane_q
                mk = lane_k if hh == 0 else ~lane_k
                qm = jnp.where(mq, q1, 0.0).astype(BF16)
                dm = jnp.where(mq, d1, 0.0).astype(BF16)
                sc = _dot(qm, kkb, NT) + bias_scr[2 * di + hh] + pen
                pr = jnp.exp(sc - _head_col(l1, mq))
                ds = (pr * (_dot(dm, vvb, NT) - _head_col(e1, mq))).astype(BF16)
                km = jnp.where(mk, kk, 0.0).astype(BF16)
                dq2 = dq2 + _dot(ds, km, NN)
                dkk = dkk + _dot(ds, qm, TN)
                dvv = dvv + _dot(pr.astype(BF16), dm, TN)
            return rows, prev, dq2 * ATTN_SCALE, dkk, dvv

        for di, d in enumerate(DILATIONS):
            for trips, blocks in _attn_groups(s, d):
                def trip(t, carry, di=di, blocks=blocks, first=(di == 0)):
                    done = [branch(n, r, di) for n, r in blocks(t)]
                    for rows, prev, dq2, dkk, dvv in done:
                        dq_ref[rows, :] = dq2 if first else dq_ref[rows, :] + dq2
                        dk_ref[prev, :] = dk_ref[prev, :] + dkk[:BLK]
                        dk_ref[rows, :] = dk_ref[rows, :] + dkk[BLK:]
                        dv_ref[prev, :] = dv_ref[prev, :] + dvv[:BLK]
                        dv_ref[rows, :] = dv_ref[rows, :] + dvv[BLK:]
                    return carry

                lax.fori_loop(0, trips, trip, 0)

        @pl.when(pair == N_PAIRS - 1)
        def _():
            _rs_chips_finish(p_refs, got_refs, ssem, rsem)

    cb = lambda base: pl.BlockSpec((s, BLK), lambda p, base=base: (0, base + p))
    out = pl.BlockSpec((s, BLK), lambda p: (0, p))
    rs_shape, rs_sems = _rs_chips_shapes(psums)
    return pl.pallas_call(
        body, name="attn_bwd", grid=(N_PAIRS,),
        in_specs=[pl.BlockSpec((None, 8, 2 * BLK), lambda p: (p, 0, 0)), cb(0), cb(N_PAIRS), cb(2 * N_PAIRS),
                  out, out, out] + [ANY] * nk,
        out_specs=[out] * 3 + [ANY] * nk, out_shape=[_sds((s, ATTN_W), F32)] * 3 + rs_shape,
        scratch_shapes=[pltpu.VMEM((2 * len(DILATIONS), BLK, 2 * BLK), F32)] + rs_sems,
        compiler_params=_cp(("arbitrary",)),
    )(_slopes(), proj, proj, proj, do, lse, delta, *psums)


def _lower_bound(lbl):
    return 1.0 / (1.0 + jnp.exp(lbl[1:2, :] - lbl[0:1, :]))


def _hi(a):
    bits = lax.bitcast_convert_type(a, jnp.uint32) & jnp.uint32(0xFFFF0000)
    return lax.bitcast_convert_type(bits, F32)


def _dot3(a, b, contract):
    ah, bh = _hi(a), _hi(b)
    al, bl = (a - ah).astype(BF16), (b - bh).astype(BF16)
    ah, bh = ah.astype(BF16), bh.astype(BF16)
    return _dot(ah, bh, contract) + (_dot(ah, bl, contract) + _dot(al, bh, contract))


def _cumsum_rows(tri, g):
    g1 = _hi(g)
    r1 = g - g1
    g2 = _hi(r1)
    g3 = r1 - g2
    return _dot(tri, g1.astype(BF16), NN) + (_dot(tri, g2.astype(BF16), NN) + _dot(tri, g3.astype(BF16), NN))


def _heads(fn):
    return jnp.concatenate([fn(slice(h * BLK, (h + 1) * BLK)) for h in range(HGRN_W // BLK)], axis=1)


def _head_mean(t):
    return _heads(lambda hs: jnp.broadcast_to(jnp.mean(t[:, hs], axis=1, keepdims=True), (t.shape[0], BLK)))


def _hgrn_chunk(q_ref, f_ref, i_ref, sl, lb, tri):
    qp = q_ref[sl, :]
    sq = _sigmoid(qp)
    qf = qp * sq
    sg = _sigmoid(f_ref[sl, :])
    f = lb + (1.0 - lb) * sg
    kf = 1.0 - f
    v = i_ref[sl, :]
    b = _cumsum_rows(tri, jnp.log(f))
    bm = b[CHUNK // 2:CHUNK // 2 + 1, :]
    bl = b[CHUNK - 1:CHUNK, :]
    qt = qf * jnp.exp(b - bm)
    kt = kf * jnp.exp(bm - b)
    return qp, sq, qf, sg, f, kf, v, b, bm, bl, qt, kt


def _hgrn_specs(tb, block):
    first = 3 * ATTN_W // HGRN_W
    return [pl.BlockSpec((tb, HGRN_W), lambda i, k=k: (block(i), first + k)) for k in range(4)]


def _hgrn_fwd(proj, lb_logits, out_gain):
    s = proj.shape[0]
    tb = min(HGRN_TB, s)
    nb, cpb, nc = s // tb, tb // CHUNK, s // CHUNK

    def body(q_ref, f_ref, i_ref, g_ref, lbl_ref, gain_ref, o_ref, rec_ref, st_ref, st_scr):
        step = pl.program_id(0)

        @pl.when(step == 0)
        def _():
            st_scr[...] = jnp.zeros_like(st_scr)

        lb = _lower_bound(lbl_ref[...])
        r64 = lax.broadcasted_iota(jnp.int32, (CHUNK, CHUNK), 0)
        c64 = lax.broadcasted_iota(jnp.int32, (CHUNK, CHUNK), 1)
        tril = r64 >= c64
        tri = tril.astype(BF16)
        st = st_scr[...]
        for cc in range(cpb):
            sl = slice(cc * CHUNK, (cc + 1) * CHUNK)
            _, _, qf, _, _, kf, v, b, _, bl, qt, kt = _hgrn_chunk(q_ref, f_ref, i_ref, sl, lb, tri)
            qe = (qf * jnp.exp(b)).astype(BF16)
            kh = (kf * jnp.exp(bl - b)).astype(BF16)
            qtb, ktb, vb, stb = qt.astype(BF16), kt.astype(BF16), v.astype(BF16), st.astype(BF16)

            def out_h(hs):
                a = jnp.where(tril, _dot(qtb[:, hs], ktb[:, hs], NT), 0.0).astype(BF16)
                return _dot(qe[:, hs], stb[:, hs], NT) + _dot(a, vb[:, hs], NN)

            o_ref[sl, :] = _heads(out_h)
            st_ref[cc] = stb
            st = st * jnp.exp(bl) + _heads(lambda hs: _dot(vb[:, hs], kh[:, hs], TN))
        st_scr[...] = st
        o = o_ref[...]
        gate = g_ref[...]
        rec_ref[...] = (o * lax.rsqrt(_head_mean(o * o) + RMS_EPS) * gain_ref[...] * (gate * _sigmoid(gate))).astype(BF16)

    row = pl.BlockSpec((tb, HGRN_W), lambda i: (i, 0))
    return pl.pallas_call(
        body, name="hgrn_fwd", grid=(nb,),
        in_specs=_hgrn_specs(tb, lambda i: i) + [pl.BlockSpec((2, HGRN_W), lambda i: (0, 0)),
                                                 pl.BlockSpec((1, HGRN_W), lambda i: (0, 0))],
        out_specs=[row, row, pl.BlockSpec((cpb, BLK, HGRN_W), lambda i: (i, 0, 0))],
        out_shape=[_sds((s, HGRN_W), F32), _sds((s, HGRN_W), BF16), _sds((nc, BLK, HGRN_W), BF16)],
        scratch_shapes=[pltpu.VMEM((BLK, HGRN_W), F32)],
        compiler_params=_cp(("arbitrary",)),
    )(proj, proj, proj, proj, lb_logits, out_gain)


def _hgrn_bwd(proj, o_pre, states, dcat, lb_logits, out_gain):
    s = proj.shape[0]
    tb = min(HGRN_TB, s)
    nb, cpb, nc = s // tb, tb // CHUNK, s // CHUNK

    def body(q_ref, f_ref, i_ref, g_ref, o_ref, st_ref, stn_ref, dy_ref, lbl_ref, gain_ref,
             dq_ref, df_ref, di_ref, dg_ref, dgain_ref, dlbl_ref, do_scr, dst_scr, dlb_scr):
        step = pl.program_id(0)

        @pl.when(step == 0)
        def _():
            dst_scr[...] = jnp.zeros_like(dst_scr)
            dlb_scr[...] = jnp.zeros_like(dlb_scr)
            dgain_ref[...] = jnp.zeros_like(dgain_ref)

        lb = _lower_bound(lbl_ref[...])
        gain = gain_ref[...]
        o = o_ref[...]
        r = lax.rsqrt(_head_mean(o * o) + RMS_EPS)
        nrm = o * r
        gate = g_ref[...]
        sgt = _sigmoid(gate)
        dy = dy_ref[...]
        dg_ref[...] = (dy * nrm * gain * (sgt * (1.0 + gate * (1.0 - sgt)))).astype(BF16)
        dng = dy * (gate * sgt)
        dgain_ref[...] += _colsum(dng * nrm)
        dn = dng * gain
        do_scr[...] = r * (dn - nrm * _head_mean(dn * nrm))

        r64 = lax.broadcasted_iota(jnp.int32, (CHUNK, CHUNK), 0)
        c64 = lax.broadcasted_iota(jnp.int32, (CHUNK, CHUNK), 1)
        tril = r64 >= c64
        tri = tril.astype(BF16)
        triu = (r64 <= c64).astype(BF16)
        dst = dst_scr[...]
        dlb = dlb_scr[...]
        for cc in reversed(range(cpb)):
            sl = slice(cc * CHUNK, (cc + 1) * CHUNK)
            qp, sq, qf, sg, f, kf, v, b, bm, bl, qt, kt = _hgrn_chunk(q_ref, f_ref, i_ref, sl, lb, tri)
            stf = st_ref[cc].astype(F32)
            st_end = (st_ref[cc + 1] if cc + 1 < cpb else stn_ref[0]).astype(F32)
            csum = jnp.sum(st_end * dst, axis=0, keepdims=True)
            doc = do_scr[sl, :]
            dob, dstb = doc.astype(BF16), dst.astype(BF16)
            eb = jnp.exp(b)
            qe = (qf * eb).astype(BF16)
            kh = (kf * jnp.exp(bl - b)).astype(BF16)
            qtb, ktb = qt.astype(BF16), kt.astype(BF16)
            parts = []
            for h in range(HGRN_W // BLK):
                hs = slice(h * BLK, (h + 1) * BLK)
                da = jnp.where(tril, _dot3(doc[:, hs], v[:, hs], NT), 0.0)
                a = jnp.where(tril, _dot(qtb[:, hs], ktb[:, hs], NT), 0.0).astype(BF16)
                parts.append((
                    _dot3(da, kt[:, hs], NN), _dot3(doc[:, hs], stf[:, hs], NN),
                    _dot3(da, qt[:, hs], TN), _dot3(v[:, hs], dst[:, hs], NN),
                    _dot(a, dob[:, hs], TN) + _dot(kh[:, hs], dstb[:, hs], NT),
                    _dot(dob[:, hs], qe[:, hs], TN)))
            dqt, dqi, dkt, dks, dv, upd = (jnp.concatenate([p[n] for p in parts], axis=1) for n in range(6))
            dqf = dqt * jnp.exp(b - bm) + eb * dqi
            dkf = dkt * jnp.exp(bm - b) + jnp.exp(bl - b) * dks
            gq = qf * dqf - kf * dkf
            dlogf = csum + _cumsum_rows(triu, gq)
            dfv = dlogf / f - dkf
            dq_ref[sl, :] = (dqf * (sq * (1.0 + qp * (1.0 - sq)))).astype(BF16)
            df_ref[sl, :] = (dfv * (1.0 - lb) * sg * (1.0 - sg)).astype(BF16)
            di_ref[sl, :] = dv.astype(BF16)
            dst = dst * jnp.exp(bl) + upd
            dlb = dlb + _colsum(dfv * (1.0 - sg))
        dst_scr[...] = dst
        dlb_scr[...] = dlb

        @pl.when(step == nb - 1)
        def _():
            t = dlb * lb * (1.0 - lb)
            dlbl_ref[...] = jnp.concatenate([t, -t], axis=0)

    rev = lambda i: nb - 1 - i
    row = pl.BlockSpec((tb, HGRN_W), lambda i: (rev(i), 0))
    res = pl.pallas_call(
        body, name="hgrn_bwd", grid=(nb,),
        in_specs=_hgrn_specs(tb, rev) + [
            row, pl.BlockSpec((cpb, BLK, HGRN_W), lambda i: (rev(i), 0, 0)),
            pl.BlockSpec((1, BLK, HGRN_W), lambda i: (jnp.minimum((rev(i) + 1) * cpb, nc - 1), 0, 0)),
            pl.BlockSpec((tb, HGRN_W), lambda i: (rev(i), ATTN_W // HGRN_W)),
            pl.BlockSpec((2, HGRN_W), lambda i: (0, 0)), pl.BlockSpec((1, HGRN_W), lambda i: (0, 0))],
        out_specs=[row, row, row, row, pl.BlockSpec((1, HGRN_W), lambda i: (0, 0)),
                   pl.BlockSpec((2, HGRN_W), lambda i: (0, 0))],
        out_shape=[_sds((s, HGRN_W), BF16)] * 4 + [_sds((1, HGRN_W), F32), _sds((2, HGRN_W), F32)],
        scratch_shapes=[pltpu.VMEM((tb, HGRN_W), F32), pltpu.VMEM((BLK, HGRN_W), F32), pltpu.VMEM((1, HGRN_W), F32)],
        compiler_params=_cp(("arbitrary",)),
    )(proj, proj, proj, proj, o_pre, states, states, dcat, lb_logits, out_gain)
    return res


def _place():
    return lax.axis_index("x"), lax.axis_index("y"), lax.axis_index("c")


def _flip(x, y, ox, oy):
    return (1 - x if ox else x), (1 - y if oy else y)


def _half(rows, cc):
    return pl.ds(cc * (rows // 2), rows // 2)


def _remote(src, dst, ssem, rsem, to):
    return pltpu.make_async_remote_copy(src_ref=src, dst_ref=dst, send_sem=ssem, recv_sem=rsem,
                                        device_id=to, device_id_type=MESH)


def _ag_chip_copies(ins, outs, ssem, rsem):
    x, y, c = _place()
    j = 2 * x + y
    cps = []
    for k in range(len(ins)):
        rows = ins[k].shape[0]
        for idx, (ox, oy) in enumerate(FLIPS):
            px, py = _flip(x, y, ox, oy)
            cps.append(_remote(ins[k].at[_half(rows, c)], outs[k].at[j, _half(rows, c)],
                               ssem.at[k, idx], rsem.at[k, idx], (px, py, c)))
    return cps


def _ag_start(ins, outs, ssem, rsem):
    for cp in _ag_chip_copies(ins, outs, ssem, rsem):
        cp.start()


def _ag_finish(ins, outs, ssem, rsem):
    x, y, c = _place()
    sib = (x, y, 1 - c)
    passed = []
    for k in range(len(ins)):
        rows = ins[k].shape[0]
        for idx, (ox, oy) in enumerate(FLIPS):
            px, py = _flip(x, y, ox, oy)
            blk = outs[k].at[2 * px + py, _half(rows, c)]
            _remote(blk, blk, ssem.at[k, idx], rsem.at[k, idx], (px, py, c)).wait_recv()
            cp = _remote(blk, blk, ssem.at[k, 3 + idx], rsem.at[k, 3 + idx], sib)
            cp.start()
            passed.append(cp)
    for k in range(len(ins)):
        rows = ins[k].shape[0]
        for idx, (ox, oy) in enumerate(FLIPS):
            px, py = _flip(x, y, ox, oy)
            blk = outs[k].at[2 * px + py, _half(rows, 1 - c)]
            _remote(blk, blk, ssem.at[k, 3 + idx], rsem.at[k, 3 + idx], sib).wait_recv()
    for cp in _ag_chip_copies(ins, outs, ssem, rsem) + passed:
        cp.wait_send()


def _ag_shapes(shards):
    nk = len(shards)
    return ([_sds((N_CHIPS,) + tuple(w.shape), w.dtype) for w in shards],
            [pltpu.SemaphoreType.DMA((nk, 6)), pltpu.SemaphoreType.DMA((nk, 6))])


def _with_own(gathered, shard, j):
    return lax.dynamic_update_index_in_dim(gathered, shard, j, 0)


def _rs_pair_copies(ins, outs, ssem, rsem):
    x, y, c = _place()
    return [_remote(ins[k].at[:, _half(ins[k].shape[1], 1 - c)], outs[k], ssem.at[k], rsem.at[k], (x, y, 1 - c))
            for k in range(len(ins))]


def _rs_pair_start(ins, outs, ssem, rsem):
    for cp in _rs_pair_copies(ins, outs, ssem, rsem):
        cp.start()


def _rs_pair_finish(ins, outs, ssem, rsem):
    for cp in _rs_pair_copies(ins, outs, ssem, rsem):
        cp.wait()


def _rs_pair_exchange(grads):
    nk = len(grads)
    return (grads, [_sds((N_CHIPS, g.shape[1] // 2, g.shape[2]), g.dtype) for g in grads],
            [pltpu.SemaphoreType.DMA((nk,)), pltpu.SemaphoreType.DMA((nk,))], _rs_pair_start, _rs_pair_finish)


def _rs_pair(name, grads):
    nk = len(grads)
    ins, out_shape, sems, start, finish = _rs_pair_exchange(grads)

    def body(*refs):
        start(refs[:nk], refs[nk:2 * nk], *refs[2 * nk:])
        finish(refs[:nk], refs[nk:2 * nk], *refs[2 * nk:])

    return pl.pallas_call(body, name=name, in_specs=[ANY] * nk, out_specs=[ANY] * nk, out_shape=out_shape,
                          scratch_shapes=sems)(*ins)


def _rs_chip_copies(ins, outs, ssem, rsem):
    x, y, c = _place()
    cps = []
    for k in range(len(ins)):
        for idx, (ox, oy) in enumerate(FLIPS):
            px, py = _flip(x, y, ox, oy)
            cps.append(_remote(ins[k].at[2 * px + py], outs[k].at[idx], ssem.at[k, idx], rsem.at[k, idx], (px, py, c)))
    return cps


def _rs_chips_start(ins, outs, ssem, rsem):
    for cp in _rs_chip_copies(ins, outs, ssem, rsem):
        cp.start()


def _rs_chips_finish(ins, outs, ssem, rsem):
    for cp in _rs_chip_copies(ins, outs, ssem, rsem):
        cp.wait()


def _rs_chips_shapes(psums):
    nk = len(psums)
    return ([_sds((3,) + tuple(p.shape[1:]), p.dtype) for p in psums],
            [pltpu.SemaphoreType.DMA((nk, 3)), pltpu.SemaphoreType.DMA((nk, 3))])


def _rs_share(fulls):
    nk = len(fulls)

    def body(*refs):
        ins, outs = refs[:nk], refs[nk:2 * nk]
        ssem, rsem = refs[2 * nk:]
        x, y, c = _place()
        cps = []
        for k in range(nk):
            rows = fulls[k].shape[0]
            cp = _remote(ins[k].at[_half(rows, c)], outs[k].at[_half(rows, c)], ssem.at[k], rsem.at[k], (x, y, 1 - c))
            cp.start()
            cps.append(cp)
        for k, cp in enumerate(cps):
            rows = fulls[k].shape[0]
            cp.wait_send()
            theirs = outs[k].at[_half(rows, 1 - c)]
            _remote(theirs, theirs, ssem.at[k], rsem.at[k], (x, y, 1 - c)).wait_recv()

    return pl.pallas_call(
        body, name="rs_share", in_specs=[ANY] * nk, out_specs=[ANY] * nk,
        out_shape=[_sds(f.shape, f.dtype) for f in fulls], input_output_aliases={k: k for k in range(nk)},
        scratch_shapes=[pltpu.SemaphoreType.DMA((nk,)), pltpu.SemaphoreType.DMA((nk,))],
    )(*fulls)


def _allreduce_small(v):
    ndev = 8

    def body(in_ref, out_ref, buf, ssem, rsem):
        x, y, c = _place()
        me = 4 * x + 2 * y + c
        buf[me] = in_ref[...]
        cps = []
        for k in range(1, ndev):
            ox, oy, oc = (k >> 2) & 1, (k >> 1) & 1, k & 1
            px, py = _flip(x, y, ox, oy)
            pc = 1 - c if oc else c
            cp = pltpu.make_async_remote_copy(src_ref=in_ref, dst_ref=buf.at[me], send_sem=ssem.at[k - 1],
                                              recv_sem=rsem.at[k - 1], device_id=(px, py, pc), device_id_type=MESH)
            cp.start()
            cps.append((cp, 4 * px + 2 * py + pc, (px, py, pc)))
        for k, (cp, src, peer) in enumerate(cps):
            cp.wait_send()
            pltpu.make_async_remote_copy(src_ref=in_ref, dst_ref=buf.at[src], send_sem=ssem.at[k],
                                         recv_sem=rsem.at[k], device_id=peer, device_id_type=MESH).wait_recv()
        acc = buf[0]
        for i in range(1, ndev):
            acc = acc + buf[i]
        out_ref[...] = acc

    return pl.pallas_call(
        body, name="allreduce_small",
        in_specs=[pl.BlockSpec(memory_space=pltpu.VMEM)], out_specs=pl.BlockSpec(memory_space=pltpu.VMEM),
        out_shape=_sds(v.shape, v.dtype),
        scratch_shapes=[pltpu.VMEM((ndev,) + v.shape, v.dtype), pltpu.SemaphoreType.DMA((ndev - 1,)),
                        pltpu.SemaphoreType.DMA((ndev - 1,))],
    )(v)


def _rs_sum1(name, g, recv, c_idx):
    _, r, cdim = g.shape
    hr = r // 2
    tr = min(hr, 256)
    nr = hr // tr

    def body(c_ref, g_ref, r_ref, o32_ref, o16_ref):
        v = g_ref[...] + r_ref[...].astype(F32)
        o32_ref[...] = v
        o16_ref[...] = v.astype(BF16)

    spec = pl.BlockSpec((None, tr, cdim), lambda j, i, c_ref: (j, i, 0))
    return pl.pallas_call(
        body, name=name,
        grid_spec=pltpu.PrefetchScalarGridSpec(
            num_scalar_prefetch=1, grid=(N_CHIPS, nr),
            in_specs=[pl.BlockSpec((None, tr, cdim), lambda j, i, c_ref: (j, c_ref[0] * nr + i, 0)), spec],
            out_specs=[spec, spec]),
        out_shape=[_sds((N_CHIPS, hr, cdim), F32), _sds((N_CHIPS, hr, cdim), BF16)],
        compiler_params=_cp(("parallel", "parallel")),
    )(c_idx, g, recv)


def _rs_sum2(name, p32, recv, jc_idx):
    _, hr, cdim = p32.shape
    tr = min(hr, 256)
    nr = hr // tr

    def body(jc_ref, p_ref, r_ref, o_ref):
        o_ref[...] = ((p_ref[...] + r_ref[0].astype(F32)) + r_ref[1].astype(F32)) + r_ref[2].astype(F32)

    return pl.pallas_call(
        body, name=name,
        grid_spec=pltpu.PrefetchScalarGridSpec(
            num_scalar_prefetch=1, grid=(nr,),
            in_specs=[pl.BlockSpec((None, tr, cdim), lambda i, jc: (jc[0], i, 0)),
                      pl.BlockSpec((3, tr, cdim), lambda i, jc: (0, i, 0))],
            out_specs=pl.BlockSpec((tr, cdim), lambda i, jc: (jc[1] * nr + i, 0))),
        out_shape=_sds((2 * hr, cdim), F32),
        compiler_params=_cp(("parallel",)),
    )(jc_idx, p32, recv)


def _adamw(name, w, g, m, v):
    r, cdim = w.shape
    tr = min(r, 256)
    c1 = 1.0 - ADAM_B1 ** ADAM_STEP
    c2 = 1.0 - ADAM_B2 ** ADAM_STEP

    def body(w_ref, g_ref, m_ref, v_ref, d_ref, nm_ref, nv_ref):
        gv = g_ref[...]
        nm = ADAM_B1 * m_ref[...] + (1.0 - ADAM_B1) * gv
        nv = ADAM_B2 * v_ref[...] + (1.0 - ADAM_B2) * (gv * gv)
        d_ref[...] = -ADAM_LR * ((nm / c1) / (jnp.sqrt(nv / c2) + ADAM_EPS) + ADAM_WD * w_ref[...])
        nm_ref[...] = nm
        nv_ref[...] = nv

    spec = pl.BlockSpec((tr, cdim), lambda i: (i, 0))
    return pl.pallas_call(
        body, name=name, grid=(r // tr,), in_specs=[spec] * 4, out_specs=[spec] * 3,
        out_shape=[_sds((r, cdim), F32)] * 3, compiler_params=_cp(("parallel",)),
    )(w, g, m, v)


def _pack_small(mix_pre, attn_out, lb_logits, hgrn_out, mix_post, mlp_pre, mlp_post):
    rows = [mix_pre, jnp.concatenate([attn_out, hgrn_out], axis=1),
            jnp.concatenate([lb_logits[0:1], lb_logits[1:2]], axis=1), mix_post, mlp_pre, mlp_post,
            jnp.zeros((2, D_MODEL), F32)]
    return jnp.concatenate(rows, axis=0)


def _unpack_small(p):
    return (p[0:1], p[1:2, :ATTN_W], jnp.concatenate([p[2:3, :HGRN_W], p[2:3, HGRN_W:]], axis=0),
            p[1:2, ATTN_W:], p[3:4], p[4:5], p[5:6])


def kernel(x, mix_pre_norm, w_in, attn_out_norm, hgrn_lb_logits, hgrn_out_norm, w_out, mix_post_norm, mlp_pre_norm, w_ff1, w_ff2, mlp_post_norm, loss_target, m_mix_pre_norm, m_w_in, m_attn_out_norm, m_hgrn_lb_logits, m_hgrn_out_norm, m_w_out, m_mix_post_norm, m_mlp_pre_norm, m_w_ff1, m_w_ff2, m_mlp_post_norm, v_mix_pre_norm, v_w_in, v_attn_out_norm, v_hgrn_lb_logits, v_hgrn_out_norm, v_w_out, v_mix_post_norm, v_mlp_pre_norm, v_w_ff1, v_w_ff2, v_mlp_post_norm):
    s = x.shape[1]
    xs = x.reshape(s, D_MODEL)
    tgt = loss_target.reshape(s, D_MODEL)
    cx, cy, cc = _place()
    chip = 2 * cx + cy
    c_idx = jnp.reshape(cc, (1,)).astype(jnp.int32)
    jc_idx = jnp.stack([chip, cc]).astype(jnp.int32)

    big_w = [w_in[0], w_out[0], w_ff1[0], w_ff2[0]]
    big_m = [m_w_in[0], m_w_out[0], m_w_ff1[0], m_w_ff2[0]]
    big_v = [v_w_in[0], v_w_out[0], v_w_ff1[0], v_w_ff2[0]]
    shards = [w.astype(BF16) for w in big_w]

    h, wg_in = _rows_call("norm_in", lambda xv, g: ((xv * _rstd(xv) * g),),
                          [(xs, _row(D_MODEL)), (mix_pre_norm, "full")], [(D_MODEL, BF16, "row")], s,
                          exchange=(shards[:1], *_ag_shapes(shards[:1]), _ag_start, _ag_finish))
    wg_in = _with_own(wg_in, shards[0], chip)
    (proj,) = _mm_cols("mm_proj", h, wg_in, NN, [F32])
    hg_o, rec, states = _hgrn_fwd(proj, hgrn_lb_logits, hgrn_out_norm)
    attn_o, attn_lse, wg_out, wg_1, wg_2 = _attn_fwd(proj, shards[1:])
    wg_out, wg_1, wg_2 = (_with_own(g, w, chip) for g, w in zip((wg_out, wg_1, wg_2), shards[1:]))
    (attn_n,) = _rows_call("attn_norm", lambda o, gain: (o * _rstd(o) * gain,),
                           [(attn_o, _row(ATTN_W)), (attn_out_norm, "full")], [(ATTN_W, BF16, "row")], s)
    cat = jnp.concatenate([attn_n, rec], axis=1)

    def post1(mv, xv, g_post, g_pre2):
        x1 = xv + mv * _rstd(mv) * g_post
        return mv, x1, x1 * _rstd(x1) * g_pre2

    mixed, x1, h2 = _rows_call(
        "mm_mixed", post1, [(xs, _row(D_MODEL)), (mix_post_norm, "full"), (mlp_pre_norm, "full")],
        [(D_MODEL, F32, "row"), (D_MODEL, F32, "row"), (D_MODEL, BF16, "row")], s,
        matmul=(cat, wg_out.reshape(D_MODEL, D_MODEL)))

    def sq_relu(u):
        r = jnp.maximum(u, 0.0)
        return (r * r,)

    (act,) = _mm_cols("mm_ff1", h2, wg_1, NN, [BF16], epi=sq_relu)

    def post2(fv, x1v, tv, g):
        y = x1v + fv * _rstd(fv) * g
        dy = (y - tv) * (1.0 / D_MODEL)
        err = y - tv
        loss = 0.5 * jnp.sum(jnp.mean(err * err, axis=-1, keepdims=True), axis=0, keepdims=True)
        dff, dgc = _norm_bwd(fv, g, dy)
        return dy, dff, _colsum(dgc), jnp.broadcast_to(loss, (1, BLK))

    dy, dff, g_mlp_post, loss_part = _rows_call(
        "mm_ff2", post2, [(x1, _row(D_MODEL)), (tgt, _row(D_MODEL)), (mlp_post_norm, "full")],
        [(D_MODEL, F32, "row"), (D_MODEL, BF16, "row"), (D_MODEL, F32, "acc"), (BLK, F32, "acc")], s,
        matmul=(act, wg_2.reshape(D_FF, D_MODEL)))

    (du,) = _mm_cols("mm_du", dff, wg_2, NT, [BF16], epi=lambda acc, a: (acc * (2.0 * jnp.sqrt(a.astype(F32))),),
                     extras=(act,))
    gw_2 = _mm_wgrad("mm_gw2", act, dff, True)
    gw_1 = _mm_wgrad("mm_gw1", h2, du, False)

    def bwd_mid(dh2v, dyv, x1v, mv, g_pre2, g_post):
        d1, gc1 = _norm_bwd(x1v, g_pre2, dh2v)
        dx1 = dyv + d1
        dm, gc2 = _norm_bwd(mv, g_post, dx1)
        return dx1, dm, _colsum(gc1), _colsum(gc2)

    w1_t = jnp.transpose(wg_1, (0, 2, 1)).reshape(D_FF, D_MODEL)
    dx1, dmixed, g_mlp_pre, g_mix_post, *from_pair = _rows_call(
        "mm_dh2", bwd_mid, [(dy, _row(D_MODEL)), (x1, _row(D_MODEL)), (mixed, _row(D_MODEL)),
                            (mlp_pre_norm, "full"), (mix_post_norm, "full")],
        [(D_MODEL, F32, "row"), (D_MODEL, BF16, "row"), (D_MODEL, F32, "acc"), (D_MODEL, F32, "acc")], s,
        matmul=(du, w1_t), exchange=_rs_pair_exchange([gw_1[1], gw_2[1]]))

    (dcat,) = _mm_cols("mm_dcat", dmixed, wg_out, NT, [F32])
    gw_out = _mm_wgrad("mm_gwout", cat, dmixed, True)
    names = ["out", "ff1", "ff2", "in"]
    ready = [gw_out, gw_1, gw_2]
    from_pair = list(_rs_pair("rs_pair_out", [gw_out[1]])) + from_pair
    pair = [_rs_sum1(f"rs_sum1_{n}", g[0], r, c_idx) for n, g, r in zip(names, ready, from_pair)]

    def attn_norm_bwd(dc, o, gain):
        do, gc = _norm_bwd(o, gain, dc)
        t = do * o
        lane = lax.broadcasted_iota(jnp.int32, (t.shape[0], BLK), 1) < 64
        parts = []
        for p in range(ATTN_W // BLK):
            tp = t[:, p * BLK:(p + 1) * BLK]
            sa = jnp.sum(jnp.where(lane, tp, 0.0), axis=1, keepdims=True)
            sb = jnp.sum(jnp.where(lane, 0.0, tp), axis=1, keepdims=True)
            parts.append(jnp.where(lane, sa, sb))
        return do, jnp.concatenate(parts, axis=1), _colsum(gc)

    do_attn, delta, g_attn_out = _rows_call(
        "attn_norm_bwd", attn_norm_bwd, [(dcat, _row(ATTN_W, 0)), (attn_o, _row(ATTN_W)), (attn_out_norm, "full")],
        [(ATTN_W, F32, "row"), (ATTN_W, F32, "row"), (ATTN_W, F32, "acc")], s)
    dq, dk, dv, *from_chips = _attn_bwd(proj, do_attn, attn_lse, delta, [p[1] for p in pair])
    dhq, dhf, dhi, dhg, g_hgrn_out, g_lb = _hgrn_bwd(proj, hg_o, states, dcat, hgrn_lb_logits, hgrn_out_norm)

    def dproj_asm(*a):
        return (jnp.concatenate([t.astype(BF16) for t in a], axis=1),)

    (dproj,) = _rows_call("dproj_asm", dproj_asm,
                          [(t, _row(ATTN_W)) for t in (dq, dk, dv)] + [(t, _row(HGRN_W)) for t in (dhq, dhf, dhi, dhg)],
                          [(PROJ_W, BF16, "row")], s)
    gw_in = _mm_wgrad("mm_gwin", h, dproj, False)
    (from_pair_in,) = _rs_pair("rs_pair_in", [gw_in[1]])
    pair.append(_rs_sum1("rs_sum1_in", gw_in[0], from_pair_in, c_idx))
    rs_shape, rs_sems = _rs_chips_shapes([pair[3][1]])

    def bwd_in(dhv, dx1v, xv, g):
        d0, gc = _norm_bwd(xv, g, dhv)
        return dx1v + d0, _colsum(gc)

    w_in_t = jnp.transpose(wg_in, (0, 2, 1)).reshape(PROJ_W, D_MODEL)
    grad_x, g_mix_pre, from_chips_in = _rows_call(
        "mm_dh", bwd_in, [(dx1, _row(D_MODEL)), (xs, _row(D_MODEL)), (mix_pre_norm, "full")],
        [(D_MODEL, F32, "row"), (D_MODEL, F32, "acc")], s, matmul=(dproj, w_in_t),
        exchange=([pair[3][1]], rs_shape, rs_sems, _rs_chips_start, _rs_chips_finish))
    from_chips.append(from_chips_in)

    loss = lax.psum(loss_part[0, 0], ("x", "y", "c"))
    small_g = _allreduce_small(_pack_small(g_mix_pre, g_attn_out, g_lb, g_hgrn_out, g_mix_post, g_mlp_pre, g_mlp_post))

    reduced = [_rs_sum2(f"rs_sum2_{n}", p[0], r, jc_idx) for n, p, r in zip(names, pair, from_chips)]
    g_wout, g_w1, g_w2, g_win = _rs_share(reduced)
    full = [g_win, g_wout, g_w1, g_w2]

    upd = [_adamw(f"adamw_{n}", w, g, m, v) for n, w, g, m, v in zip(("in", "out", "ff1", "ff2"), big_w, full, big_m, big_v)]
    small_w = _pack_small(mix_pre_norm, attn_out_norm, hgrn_lb_logits, hgrn_out_norm, mix_post_norm, mlp_pre_norm,
                          mlp_post_norm)
    small_m = _pack_small(m_mix_pre_norm, m_attn_out_norm, m_hgrn_lb_logits, m_hgrn_out_norm, m_mix_post_norm,
                          m_mlp_pre_norm, m_mlp_post_norm)
    small_v = _pack_small(v_mix_pre_norm, v_attn_out_norm, v_hgrn_lb_logits, v_hgrn_out_norm, v_mix_post_norm,
                          v_mlp_pre_norm, v_mlp_post_norm)
    small_upd = _adamw("adamw_small", small_w, small_g, small_m, small_v)

    def assemble(small, big):
        sm = _unpack_small(small)
        return (sm[0], big[0][None], sm[1], sm[2], sm[3], big[1][None], sm[4], sm[5], big[2][None], big[3][None], sm[6])

    g_out = assemble(small_g, full)
    d_out = assemble(small_upd[0], [u[0] for u in upd])
    m_out = assemble(small_upd[1], [u[1] for u in upd])
    v_out = assemble(small_upd[2], [u[2] for u in upd])
    return (loss, grad_x.reshape(x.shape), *g_out, *d_out, *m_out, *v_out)
```

```python
import numpy as np
import jax
import jax.numpy as jnp
from jax import lax
from jax.experimental import pallas as pl
from jax.experimental.pallas import tpu as pltpu

F32 = jnp.float32
BF16 = jnp.bfloat16
MESH = pl.DeviceIdType.MESH
ANY = pl.BlockSpec(memory_space=pl.ANY)

RMS_EPS = 1e-6
D_MODEL = 1024
ATTN_W = 512
HGRN_W = 512
PROJ_W = 3584
D_FF = 4096
N_CHIPS = 4
BLK = 128
CHUNK = 64
HGRN_TB = 512
ATTN_GROUP = 8
DILATIONS = (1, 4, 16)
ATTN_SCALE = 0.125
ROW_TILE = 512
MM_TILE = 1024
VMEM_LIMIT = 48 * 2 ** 20
FLIPS = ((1, 0), (0, 1), (1, 1))

ADAM_LR, ADAM_B1, ADAM_B2, ADAM_EPS, ADAM_WD, ADAM_STEP = 0.001, 0.9, 0.999, 1e-08, 0.01, 10


def _cp(sem=None):
    return pltpu.CompilerParams(dimension_semantics=sem, vmem_limit_bytes=VMEM_LIMIT)


def _sigmoid(v):
    return 1.0 / (1.0 + jnp.exp(-v))


def _dot(a, b, contract, precision=None):
    return lax.dot_general(a, b, (contract, ((), ())), preferred_element_type=F32, precision=precision)


NN = ((1,), (0,))
NT = ((1,), (1,))
TN = ((0,), (0,))


def _sds(shape, dtype):
    return jax.ShapeDtypeStruct(shape, dtype)


def _resident(shape):
    return pl.BlockSpec(shape, lambda *_: (0,) * len(shape), pipeline_mode=pl.Buffered(1))


def _mm_cols(name, a, w, contract, out_dtypes, epi=None, extras=()):
    m, k = a.shape
    jn = w.shape[0]
    nj = w.shape[2] if contract == NN else w.shape[1]
    tm = min(m, MM_TILE)
    n_ex, parts = len(extras), 2

    def body(a_ref, w_ref, *rest):
        ex, out_refs = rest[:n_ex], rest[n_ex:]
        i = pl.program_id(1)
        part = tm // parts
        for h in range(parts):
            rows = slice(h * part, (h + 1) * part)
            acc = _dot(a_ref[pl.ds(pl.multiple_of(i * tm + h * part, part), part), :], w_ref[...], contract)
            res = epi(acc, *[e[rows, :] for e in ex]) if epi else (acc,)
            for o, r in zip(out_refs, res):
                o[rows, :] = r.astype(o.dtype)

    blk = pl.BlockSpec((tm, nj), lambda j, i: (i, j))
    return pl.pallas_call(
        body, name=name, grid=(jn, m // tm),
        in_specs=[_resident((m, k)), pl.BlockSpec((None,) + w.shape[1:], lambda j, i: (j, 0, 0))] + [blk] * n_ex,
        out_specs=[blk] * len(out_dtypes), out_shape=[_sds((m, jn * nj), dt) for dt in out_dtypes],
        compiler_params=_cp(("parallel", "parallel")),
    )(a, w, *extras)


def _mm_wgrad(name, a, b, a_by_j):
    s = a.shape[0]
    if a_by_j:
        r, c = a.shape[1] // N_CHIPS, b.shape[1]
        in_specs = [pl.BlockSpec((s, r), lambda j: (0, j)), _resident((s, c))]
    else:
        r, c = a.shape[1], b.shape[1] // N_CHIPS
        in_specs = [_resident((s, r)), pl.BlockSpec((s, c), lambda j: (0, j))]
    tr = min(r, 512)

    def body(a_ref, b_ref, o32_ref, o16_ref):
        for h in range(r // tr):
            cols = slice(h * tr, (h + 1) * tr)
            acc = _dot(a_ref[:, cols], b_ref[...], TN)
            o32_ref[cols, :] = acc
            o16_ref[cols, :] = acc.astype(BF16)

    out = pl.BlockSpec((None, r, c), lambda j: (j, 0, 0))
    return pl.pallas_call(
        body, name=name, grid=(N_CHIPS,), in_specs=in_specs, out_specs=[out, out],
        out_shape=[_sds((N_CHIPS, r, c), F32), _sds((N_CHIPS, r, c), BF16)], compiler_params=_cp(("parallel",)),
    )(a, b)


def _rows_call(name, fn, ins, outs, s, tm=ROW_TILE, matmul=None, exchange=None):
    in_specs = []
    if matmul:
        a, w = matmul
        in_specs += [pl.BlockSpec((tm, a.shape[1]), lambda i: (i, 0)), _resident(w.shape)]
    for arr, kind in ins:
        if kind == "full":
            in_specs.append(pl.BlockSpec(arr.shape, lambda i: (0, 0)))
        else:
            _, w_, cb = kind
            in_specs.append(pl.BlockSpec((tm, w_), lambda i, cb=cb: (i, cb)))
    out_specs, out_shape, is_acc = [], [], []
    for w_, dt, kind in outs:
        if kind == "acc":
            out_specs.append(pl.BlockSpec((1, w_), lambda i: (0, 0)))
            out_shape.append(_sds((1, w_), dt))
        else:
            out_specs.append(pl.BlockSpec((tm, w_), lambda i: (i, 0)))
            out_shape.append(_sds((s, w_), dt))
        is_acc.append(kind == "acc")
    n_mm, n_in, n_out = (2 if matmul else 0), len(ins), len(outs)
    x_ins, x_shapes, x_sems, x_start, x_finish = exchange if exchange else ((), [], [], None, None)
    n_x = len(x_ins)
    steps = s // tm

    def body(*refs):
        in_refs, xi = refs[n_mm:n_mm + n_in], refs[n_mm + n_in:n_mm + n_in + n_x]
        out_refs = refs[n_mm + n_in + n_x:n_mm + n_in + n_x + n_out]
        xo, sems = refs[n_mm + n_in + n_x + n_out:n_mm + n_in + 2 * n_x + n_out], refs[n_mm + n_in + 2 * n_x + n_out:]
        i = pl.program_id(0)

        if exchange:
            @pl.when(i == 0)
            def _():
                x_start(xi, xo, *sems)

        for o, acc in zip(out_refs, is_acc):
            if acc:
                @pl.when(i == 0)
                def _(o=o):
                    o[...] = jnp.zeros_like(o)

        parts = 2 if matmul else 1
        for h in range(parts):
            rows = slice(h * (tm // parts), (h + 1) * (tm // parts))
            args = [r[...] if kind == "full" else r[rows, :] for r, (_, kind) in zip(in_refs, ins)]
            if matmul:
                args.insert(0, _dot(refs[0][rows, :], refs[1][...], NN))
            for o, r, acc in zip(out_refs, fn(*args), is_acc):
                if acc:
                    o[...] += r.astype(o.dtype)
                else:
                    o[rows, :] = r.astype(o.dtype)

        if exchange:
            @pl.when(i == steps - 1)
            def _():
                x_finish(xi, xo, *sems)

    sem = ("arbitrary",) if any(is_acc) or exchange else ("parallel",)
    return pl.pallas_call(
        body, name=name, grid=(steps,), in_specs=in_specs + [ANY] * n_x, out_specs=out_specs + [ANY] * n_x,
        out_shape=out_shape + list(x_shapes), scratch_shapes=list(x_sems), compiler_params=_cp(sem),
    )(*(matmul or ()), *[a for a, _ in ins], *x_ins)


def _rstd(v):
    return lax.rsqrt(jnp.mean(v * v, axis=-1, keepdims=True) + RMS_EPS)


def _norm_bwd(v, gain, dy):
    r = _rstd(v)
    n = v * r
    dn = dy * gain
    dv = r * (dn - n * jnp.mean(dn * n, axis=-1, keepdims=True))
    return dv, dy * n


def _colsum(v):
    return jnp.sum(v, axis=0, keepdims=True)


def _row(w, cb=0):
    return ("row", w, cb)


N_PAIRS = ATTN_W // BLK


def _head_col(v, mask):
    return jnp.max(jnp.where(mask, v, -jnp.inf), axis=1, keepdims=True)


def _slopes():
    t = np.zeros((N_PAIRS, 8, 2 * BLK), np.float32)
    for p in range(N_PAIRS):
        for hh in range(2):
            t[p, hh, :] = 2.0 ** -(2 * p + hh + 1)
    return jnp.asarray(t)


def _rows(n, r, d):
    base = pl.multiple_of(n * (BLK * d), BLK)
    return pl.ds(base + r, BLK, stride=d) if d > 1 else pl.ds(base, BLK)


def _attn_bias(sl_ref, bias_scr):
    row = lax.broadcasted_iota(jnp.int32, (BLK, 2 * BLK), 0)
    col = lax.broadcasted_iota(jnp.int32, (BLK, 2 * BLK), 1)
    dist = row + BLK - col
    in_window = (dist >= 0) & (dist <= BLK)
    distf = dist.astype(F32)
    for di, d in enumerate(DILATIONS):
        for hh in range(2):
            bias_scr[2 * di + hh] = jnp.where(in_window, -(sl_ref[hh:hh + 1, :] * float(d)) * distf, -1e30)


def _first_block_penalty(n):
    col = lax.broadcasted_iota(jnp.int32, (1, 2 * BLK), 1)
    return jnp.where(col + n * BLK >= BLK, 0.0, -1e30)


def _attn_groups(s, d):
    nb = s // (BLK * d)
    g = ATTN_GROUP
    if d >= g:
        return [(nb, lambda n, r0=r0: [(n, r0 + u) for u in range(g)]) for r0 in range(0, d, g)]
    per = g // d
    return [(nb // per, lambda t: [(per * t + u, r) for u in range(per) for r in range(d)])]


def _attn_fwd(proj, shards):
    s = proj.shape[0]
    nk = len(shards)

    def body(sl_ref, q_ref, k_ref, v_ref, *rest):
        w_refs, (o_ref, l_ref) = rest[:nk], rest[nk:nk + 2]
        wg_refs, (bias_scr, ssem, rsem) = rest[nk + 2:2 * nk + 2], rest[2 * nk + 2:]
        pair = pl.program_id(0)

        @pl.when(pair == 0)
        def _():
            _ag_start(w_refs, wg_refs, ssem, rsem)

        _attn_bias(sl_ref, bias_scr)
        lane_q = lax.broadcasted_iota(jnp.int32, (BLK, BLK), 1) < 64
        lane_k = lax.broadcasted_iota(jnp.int32, (2 * BLK, BLK), 1) < 64

        def branch(n, r, di):
            d = DILATIONS[di]
            rows = _rows(n, r, d)
            prev = _rows(jnp.maximum(n - 1, 0), r, d)
            pen = _first_block_penalty(n)
            q2 = q_ref[rows, :] * ATTN_SCALE
            kk = jnp.concatenate([k_ref[prev, :], k_ref[rows, :]], axis=0).astype(BF16)
            vv = jnp.concatenate([v_ref[prev, :], v_ref[rows, :]], axis=0)
            o2 = jnp.zeros((BLK, BLK), F32)
            lse2 = jnp.zeros((BLK, BLK), F32)
            for hh in range(2):
                mq = lane_q if hh == 0 else ~lane_q
                mk = lane_k if hh == 0 else ~lane_k
                qm = jnp.where(mq, q2, 0.0).astype(BF16)
                sc = _dot(qm, kk, NT) + bias_scr[2 * di + hh] + pen
                m = jnp.max(sc, axis=1, keepdims=True)
                pr = jnp.exp(sc - m)
                den = jnp.sum(pr, axis=1, keepdims=True)
                vm = jnp.where(mk, vv, 0.0).astype(BF16)
                o2 = o2 + _dot(pr.astype(BF16), vm, NN) / den
                lse2 = jnp.where(mq, m + jnp.log(den), lse2)
            return rows, o2, lse2

        def merge(rows, o2, lse2, first):
            if first:
                o_ref[rows, :] = o2
                l_ref[rows, :] = lse2
            else:
                lo = l_ref[rows, :]
                mx = jnp.maximum(lo, lse2)
                ln = mx + jnp.log(jnp.exp(lo - mx) + jnp.exp(lse2 - mx))
                o_ref[rows, :] = jnp.exp(lo - ln) * o_ref[rows, :] + jnp.exp(lse2 - ln) * o2
                l_ref[rows, :] = ln

        for di, d in enumerate(DILATIONS):
            for trips, blocks in _attn_groups(s, d):
                def trip(t, carry, di=di, blocks=blocks):
                    done = [branch(n, r, di) for n, r in blocks(t)]
                    for rows, o2, lse2 in done:
                        merge(rows, o2, lse2, di == 0)
                    return carry

                lax.fori_loop(0, trips, trip, 0)

        @pl.when(pair == N_PAIRS - 1)
        def _():
            _ag_finish(w_refs, wg_refs, ssem, rsem)

    cb = lambda base: pl.BlockSpec((s, BLK), lambda p, base=base: (0, base + p))
    out = pl.BlockSpec((s, BLK), lambda p: (0, p))
    ag_shape, ag_sems = _ag_shapes(shards)
    return pl.pallas_call(
        body, name="attn_fwd", grid=(N_PAIRS,),
        in_specs=[pl.BlockSpec((None, 8, 2 * BLK), lambda p: (p, 0, 0)), cb(0), cb(N_PAIRS), cb(2 * N_PAIRS)]
        + [ANY] * nk,
        out_specs=[out, out] + [ANY] * nk, out_shape=[_sds((s, ATTN_W), F32)] * 2 + ag_shape,
        scratch_shapes=[pltpu.VMEM((2 * len(DILATIONS), BLK, 2 * BLK), F32)] + ag_sems,
        compiler_params=_cp(("arbitrary",)),
    )(_slopes(), proj, proj, proj, *shards)


def _attn_bwd(proj, do, lse, delta, psums):
    s = proj.shape[0]
    nk = len(psums)

    def body(sl_ref, q_ref, k_ref, v_ref, do_ref, l_ref, e_ref, *rest):
        p_refs, (dq_ref, dk_ref, dv_ref) = rest[:nk], rest[nk:nk + 3]
        got_refs, (bias_scr, ssem, rsem) = rest[nk + 3:2 * nk + 3], rest[2 * nk + 3:]
        pair = pl.program_id(0)

        @pl.when(pair == 0)
        def _():
            _rs_chips_start(p_refs, got_refs, ssem, rsem)

        _attn_bias(sl_ref, bias_scr)
        lane_q = lax.broadcasted_iota(jnp.int32, (BLK, BLK), 1) < 64
        lane_k = lax.broadcasted_iota(jnp.int32, (2 * BLK, BLK), 1) < 64
        dk_ref[...] = jnp.zeros_like(dk_ref)
        dv_ref[...] = jnp.zeros_like(dv_ref)

        def branch(n, r, di):
            d = DILATIONS[di]
            rows = _rows(n, r, d)
            prev = _rows(jnp.maximum(n - 1, 0), r, d)
            pen = _first_block_penalty(n)
            q1, d1, l1, e1 = q_ref[rows, :] * ATTN_SCALE, do_ref[rows, :], l_ref[rows, :], e_ref[rows, :]
            kk = jnp.concatenate([k_ref[prev, :], k_ref[rows, :]], axis=0)
            kkb = kk.astype(BF16)
            vvb = jnp.concatenate([v_ref[prev, :], v_ref[rows, :]], axis=0).astype(BF16)
            dq2 = jnp.zeros((BLK, BLK), F32)
            dkk = jnp.zeros((2 * BLK, BLK), F32)
            dvv = jnp.zeros((2 * BLK, BLK), F32)
            for hh in range(2):
                mq = lane_q if hh == 0 else ~lane_q
                mk = lane_k if hh == 0 else ~lane_k
                qm = jnp.where(mq, q1, 0.0).astype(BF16)
                dm = jnp.where(mq, d1, 0.0).astype(BF16)
                sc = _dot(qm, kkb, NT) + bias_scr[2 * di + hh] + pen
                pr = jnp.exp(sc - _head_col(l1, mq))
                ds = (pr * (_dot(dm, vvb, NT) - _head_col(e1, mq))).astype(BF16)
                km = jnp.where(mk, kk, 0.0).astype(BF16)
                dq2 = dq2 + _dot(ds, km, NN)
                dkk = dkk + _dot(ds, qm, TN)
                dvv = dvv + _dot(pr.astype(BF16), dm, TN)
            return rows, prev, dq2 * ATTN_SCALE, dkk, dvv

        for di, d in enumerate(DILATIONS):
            for trips, blocks in _attn_groups(s, d):
                def trip(t, carry, di=di, blocks=blocks, first=(di == 0)):
                    done = [branch(n, r, di) for n, r in blocks(t)]
                    for rows, prev, dq2, dkk, dvv in done:
                        dq_ref[rows, :] = dq2 if first else dq_ref[rows, :] + dq2
                        dk_ref[prev, :] = dk_ref[prev, :] + dkk[:BLK]
                        dk_ref[rows, :] = dk_ref[rows, :] + dkk[BLK:]
                        dv_ref[prev, :] = dv_ref[prev, :] + dvv[:BLK]
                        dv_ref[rows, :] = dv_ref[rows, :] + dvv[BLK:]
                    return carry

                lax.fori_loop(0, trips, trip, 0)

        @pl.when(pair == N_PAIRS - 1)
        def _():
            _rs_chips_finish(p_refs, got_refs, ssem, rsem)

    cb = lambda base: pl.BlockSpec((s, BLK), lambda p, base=base: (0, base + p))
    out = pl.BlockSpec((s, BLK), lambda p: (0, p))
    rs_shape, rs_sems = _rs_chips_shapes(psums)
    return pl.pallas_call(
        body, name="attn_bwd", grid=(N_PAIRS,),
        in_specs=[pl.BlockSpec((None, 8, 2 * BLK), lambda p: (p, 0, 0)), cb(0), cb(N_PAIRS), cb(2 * N_PAIRS),
                  out, out, out] + [ANY] * nk,
        out_specs=[out] * 3 + [ANY] * nk, out_shape=[_sds((s, ATTN_W), F32)] * 3 + rs_shape,
        scratch_shapes=[pltpu.VMEM((2 * len(DILATIONS), BLK, 2 * BLK), F32)] + rs_sems,
        compiler_params=_cp(("arbitrary",)),
    )(_slopes(), proj, proj, proj, do, lse, delta, *psums)


def _lower_bound(lbl):
    return 1.0 / (1.0 + jnp.exp(lbl[1:2, :] - lbl[0:1, :]))


def _hi(a):
    bits = lax.bitcast_convert_type(a, jnp.uint32) & jnp.uint32(0xFFFF0000)
    return lax.bitcast_convert_type(bits, F32)


def _dot3(a, b, contract):
    ah, bh = _hi(a), _hi(b)
    al, bl = (a - ah).astype(BF16), (b - bh).astype(BF16)
    ah, bh = ah.astype(BF16), bh.astype(BF16)
    return _dot(ah, bh, contract) + (_dot(ah, bl, contract) + _dot(al, bh, contract))


def _cumsum_rows(tri, g):
    g1 = _hi(g)
    r1 = g - g1
    g2 = _hi(r1)
    g3 = r1 - g2
    return _dot(tri, g1.astype(BF16), NN) + (_dot(tri, g2.astype(BF16), NN) + _dot(tri, g3.astype(BF16), NN))


def _heads(fn):
    return jnp.concatenate([fn(slice(h * BLK, (h + 1) * BLK)) for h in range(HGRN_W // BLK)], axis=1)


def _head_mean(t):
    return _heads(lambda hs: jnp.broadcast_to(jnp.mean(t[:, hs], axis=1, keepdims=True), (t.shape[0], BLK)))


def _hgrn_chunk(q_ref, f_ref, i_ref, sl, lb, tri):
    qp = q_ref[sl, :]
    sq = _sigmoid(qp)
    qf = qp * sq
    sg = _sigmoid(f_ref[sl, :])
    f = lb + (1.0 - lb) * sg
    kf = 1.0 - f
    v = i_ref[sl, :]
    b = _cumsum_rows(tri, jnp.log(f))
    bm = b[CHUNK // 2:CHUNK // 2 + 1, :]
    bl = b[CHUNK - 1:CHUNK, :]
    qt = qf * jnp.exp(b - bm)
    kt = kf * jnp.exp(bm - b)
    return qp, sq, qf, sg, f, kf, v, b, bm, bl, qt, kt


def _hgrn_specs(tb, block):
    first = 3 * ATTN_W // HGRN_W
    return [pl.BlockSpec((tb, HGRN_W), lambda i, k=k: (block(i), first + k)) for k in range(4)]


def _hgrn_fwd(proj, lb_logits, out_gain):
    s = proj.shape[0]
    tb = min(HGRN_TB, s)
    nb, cpb, nc = s // tb, tb // CHUNK, s // CHUNK

    def body(q_ref, f_ref, i_ref, g_ref, lbl_ref, gain_ref, o_ref, rec_ref, st_ref, st_scr):
        step = pl.program_id(0)

        @pl.when(step == 0)
        def _():
            st_scr[...] = jnp.zeros_like(st_scr)

        lb = _lower_bound(lbl_ref[...])
        r64 = lax.broadcasted_iota(jnp.int32, (CHUNK, CHUNK), 0)
        c64 = lax.broadcasted_iota(jnp.int32, (CHUNK, CHUNK), 1)
        tril = r64 >= c64
        tri = tril.astype(BF16)
        st = st_scr[...]
        for cc in range(cpb):
            sl = slice(cc * CHUNK, (cc + 1) * CHUNK)
            _, _, qf, _, _, kf, v, b, _, bl, qt, kt = _hgrn_chunk(q_ref, f_ref, i_ref, sl, lb, tri)
            qe = (qf * jnp.exp(b)).astype(BF16)
            kh = (kf * jnp.exp(bl - b)).astype(BF16)
            qtb, ktb, vb, stb = qt.astype(BF16), kt.astype(BF16), v.astype(BF16), st.astype(BF16)

            def out_h(hs):
                a = jnp.where(tril, _dot(qtb[:, hs], ktb[:, hs], NT), 0.0).astype(BF16)
                return _dot(qe[:, hs], stb[:, hs], NT) + _dot(a, vb[:, hs], NN)

            o_ref[sl, :] = _heads(out_h)
            st_ref[cc] = stb
            st = st * jnp.exp(bl) + _heads(lambda hs: _dot(vb[:, hs], kh[:, hs], TN))
        st_scr[...] = st
        o = o_ref[...]
        gate = g_ref[...]
        rec_ref[...] = (o * lax.rsqrt(_head_mean(o * o) + RMS_EPS) * gain_ref[...] * (gate * _sigmoid(gate))).astype(BF16)

    row = pl.BlockSpec((tb, HGRN_W), lambda i: (i, 0))
    return pl.pallas_call(
        body, name="hgrn_fwd", grid=(nb,),
        in_specs=_hgrn_specs(tb, lambda i: i) + [pl.BlockSpec((2, HGRN_W), lambda i: (0, 0)),
                                                 pl.BlockSpec((1, HGRN_W), lambda i: (0, 0))],
        out_specs=[row, row, pl.BlockSpec((cpb, BLK, HGRN_W), lambda i: (i, 0, 0))],
        out_shape=[_sds((s, HGRN_W), F32), _sds((s, HGRN_W), BF16), _sds((nc, BLK, HGRN_W), BF16)],
        scratch_shapes=[pltpu.VMEM((BLK, HGRN_W), F32)],
        compiler_params=_cp(("arbitrary",)),
    )(proj, proj, proj, proj, lb_logits, out_gain)


def _hgrn_bwd(proj, o_pre, states, dcat, lb_logits, out_gain):
    s = proj.shape[0]
    tb = min(HGRN_TB, s)
    nb, cpb, nc = s // tb, tb // CHUNK, s // CHUNK

    def body(q_ref, f_ref, i_ref, g_ref, o_ref, st_ref, stn_ref, dy_ref, lbl_ref, gain_ref,
             dq_ref, df_ref, di_ref, dg_ref, dgain_ref, dlbl_ref, do_scr, dst_scr, dlb_scr):
        step = pl.program_id(0)

        @pl.when(step == 0)
        def _():
            dst_scr[...] = jnp.zeros_like(dst_scr)
            dlb_scr[...] = jnp.zeros_like(dlb_scr)
            dgain_ref[...] = jnp.zeros_like(dgain_ref)

        lb = _lower_bound(lbl_ref[...])
        gain = gain_ref[...]
        o = o_ref[...]
        r = lax.rsqrt(_head_mean(o * o) + RMS_EPS)
        nrm = o * r
        gate = g_ref[...]
        sgt = _sigmoid(gate)
        dy = dy_ref[...]
        dg_ref[...] = (dy * nrm * gain * (sgt * (1.0 + gate * (1.0 - sgt)))).astype(BF16)
        dng = dy * (gate * sgt)
        dgain_ref[...] += _colsum(dng * nrm)
        dn = dng * gain
        do_scr[...] = r * (dn - nrm * _head_mean(dn * nrm))

        r64 = lax.broadcasted_iota(jnp.int32, (CHUNK, CHUNK), 0)
        c64 = lax.broadcasted_iota(jnp.int32, (CHUNK, CHUNK), 1)
        tril = r64 >= c64
        tri = tril.astype(BF16)
        triu = (r64 <= c64).astype(BF16)
        dst = dst_scr[...]
        dlb = dlb_scr[...]
        for cc in reversed(range(cpb)):
            sl = slice(cc * CHUNK, (cc + 1) * CHUNK)
            qp, sq, qf, sg, f, kf, v, b, bm, bl, qt, kt = _hgrn_chunk(q_ref, f_ref, i_ref, sl, lb, tri)
            stf = st_ref[cc].astype(F32)
            st_end = (st_ref[cc + 1] if cc + 1 < cpb else stn_ref[0]).astype(F32)
            csum = jnp.sum(st_end * dst, axis=0, keepdims=True)
            doc = do_scr[sl, :]
            dob, dstb = doc.astype(BF16), dst.astype(BF16)
            eb = jnp.exp(b)
            qe = (qf * eb).astype(BF16)
            kh = (kf * jnp.exp(bl - b)).astype(BF16)
            qtb, ktb = qt.astype(BF16), kt.astype(BF16)
            parts = []
            for h in range(HGRN_W // BLK):
                hs = slice(h * BLK, (h + 1) * BLK)
                da = jnp.where(tril, _dot3(doc[:, hs], v[:, hs], NT), 0.0)
                a = jnp.where(tril, _dot(qtb[:, hs], ktb[:, hs], NT), 0.0).astype(BF16)
                parts.append((
                    _dot3(da, kt[:, hs], NN), _dot3(doc[:, hs], stf[:, hs], NN),
                    _dot3(da, qt[:, hs], TN), _dot3(v[:, hs], dst[:, hs], NN),
                    _dot(a, dob[:, hs], TN) + _dot(kh[:, hs], dstb[:, hs], NT),
                    _dot(dob[:, hs], qe[:, hs], TN)))
            dqt, dqi, dkt, dks, dv, upd = (jnp.concatenate([p[n] for p in parts], axis=1) for n in range(6))
            dqf = dqt * jnp.exp(b - bm) + eb * dqi
            dkf = dkt * jnp.exp(bm - b) + jnp.exp(bl - b) * dks
            gq = qf * dqf - kf * dkf
            dlogf = csum + _cumsum_rows(triu, gq)
            dfv = dlogf / f - dkf
            dq_ref[sl, :] = (dqf * (sq * (1.0 + qp * (1.0 - sq)))).astype(BF16)
            df_ref[sl, :] = (dfv * (1.0 - lb) * sg * (1.0 - sg)).astype(BF16)
            di_ref[sl, :] = dv.astype(BF16)
            dst = dst * jnp.exp(bl) + upd
            dlb = dlb + _colsum(dfv * (1.0 - sg))
        dst_scr[...] = dst
        dlb_scr[...] = dlb

        @pl.when(step == nb - 1)
        def _():
            t = dlb * lb * (1.0 - lb)
            dlbl_ref[...] = jnp.concatenate([t, -t], axis=0)

    rev = lambda i: nb - 1 - i
    row = pl.BlockSpec((tb, HGRN_W), lambda i: (rev(i), 0))
    res = pl.pallas_call(
        body, name="hgrn_bwd", grid=(nb,),
        in_specs=_hgrn_specs(tb, rev) + [
            row, pl.BlockSpec((cpb, BLK, HGRN_W), lambda i: (rev(i), 0, 0)),
            pl.BlockSpec((1, BLK, HGRN_W), lambda i: (jnp.minimum((rev(i) + 1) * cpb, nc - 1), 0, 0)),
            pl.BlockSpec((tb, HGRN_W), lambda i: (rev(i), ATTN_W // HGRN_W)),
            pl.BlockSpec((2, HGRN_W), lambda i: (0, 0)), pl.BlockSpec((1, HGRN_W), lambda i: (0, 0))],
        out_specs=[row, row, row, row, pl.BlockSpec((1, HGRN_W), lambda i: (0, 0)),
                   pl.BlockSpec((2, HGRN_W), lambda i: (0, 0))],
        out_shape=[_sds((s, HGRN_W), BF16)] * 4 + [_sds((1, HGRN_W), F32), _sds((2, HGRN_W), F32)],
        scratch_shapes=[pltpu.VMEM((tb, HGRN_W), F32), pltpu.VMEM((BLK, HGRN_W), F32), pltpu.VMEM((1, HGRN_W), F32)],
        compiler_params=_cp(("arbitrary",)),
    )(proj, proj, proj, proj, o_pre, states, states, dcat, lb_logits, out_gain)
    return res


def _place():
    return lax.axis_index("x"), lax.axis_index("y"), lax.axis_index("c")


def _flip(x, y, ox, oy):
    return (1 - x if ox else x), (1 - y if oy else y)


def _half(rows, cc):
    return pl.ds(cc * (rows // 2), rows // 2)


def _remote(src, dst, ssem, rsem, to):
    return pltpu.make_async_remote_copy(src_ref=src, dst_ref=dst, send_sem=ssem, recv_sem=rsem,
                                        device_id=to, device_id_type=MESH)


def _ag_chip_copies(ins, outs, ssem, rsem):
    x, y, c = _place()
    j = 2 * x + y
    cps = []
    for k in range(len(ins)):
        rows = ins[k].shape[0]
        for idx, (ox, oy) in enumerate(FLIPS):
            px, py = _flip(x, y, ox, oy)
            cps.append(_remote(ins[k].at[_half(rows, c)], outs[k].at[j, _half(rows, c)],
                               ssem.at[k, idx], rsem.at[k, idx], (px, py, c)))
    return cps


def _ag_start(ins, outs, ssem, rsem):
    for cp in _ag_chip_copies(ins, outs, ssem, rsem):
        cp.start()


def _ag_finish(ins, outs, ssem, rsem):
    x, y, c = _place()
    sib = (x, y, 1 - c)
    passed = []
    for k in range(len(ins)):
        rows = ins[k].shape[0]
        for idx, (ox, oy) in enumerate(FLIPS):
            px, py = _flip(x, y, ox, oy)
            blk = outs[k].at[2 * px + py, _half(rows, c)]
            _remote(blk, blk, ssem.at[k, idx], rsem.at[k, idx], (px, py, c)).wait_recv()
            cp = _remote(blk, blk, ssem.at[k, 3 + idx], rsem.at[k, 3 + idx], sib)
            cp.start()
            passed.append(cp)
    for k in range(len(ins)):
        rows = ins[k].shape[0]
        for idx, (ox, oy) in enumerate(FLIPS):
            px, py = _flip(x, y, ox, oy)
            blk = outs[k].at[2 * px + py, _half(rows, 1 - c)]
            _remote(blk, blk, ssem.at[k, 3 + idx], rsem.at[k, 3 + idx], sib).wait_recv()
    for cp in _ag_chip_copies(ins, outs, ssem, rsem) + passed:
        cp.wait_send()


def _ag_shapes(shards):
    nk = len(shards)
    return ([_sds((N_CHIPS,) + tuple(w.shape), w.dtype) for w in shards],
            [pltpu.SemaphoreType.DMA((nk, 6)), pltpu.SemaphoreType.DMA((nk, 6))])


def _with_own(gathered, shard, j):
    return lax.dynamic_update_index_in_dim(gathered, shard, j, 0)


def _rs_pair_copies(ins, outs, ssem, rsem):
    x, y, c = _place()
    return [_remote(ins[k].at[:, _half(ins[k].shape[1], 1 - c)], outs[k], ssem.at[k], rsem.at[k], (x, y, 1 - c))
            for k in range(len(ins))]


def _rs_pair_start(ins, outs, ssem, rsem):
    for cp in _rs_pair_copies(ins, outs, ssem, rsem):
        cp.start()


def _rs_pair_finish(ins, outs, ssem, rsem):
    for cp in _rs_pair_copies(ins, outs, ssem, rsem):
        cp.wait()


def _rs_pair_exchange(grads):
    nk = len(grads)
    return (grads, [_sds((N_CHIPS, g.shape[1] // 2, g.shape[2]), g.dtype) for g in grads],
            [pltpu.SemaphoreType.DMA((nk,)), pltpu.SemaphoreType.DMA((nk,))], _rs_pair_start, _rs_pair_finish)


def _rs_pair(name, grads):
    nk = len(grads)
    ins, out_shape, sems, start, finish = _rs_pair_exchange(grads)

    def body(*refs):
        start(refs[:nk], refs[nk:2 * nk], *refs[2 * nk:])
        finish(refs[:nk], refs[nk:2 * nk], *refs[2 * nk:])

    return pl.pallas_call(body, name=name, in_specs=[ANY] * nk, out_specs=[ANY] * nk, out_shape=out_shape,
                          scratch_shapes=sems)(*ins)


def _rs_chip_copies(ins, outs, ssem, rsem):
    x, y, c = _place()
    cps = []
    for k in range(len(ins)):
        for idx, (ox, oy) in enumerate(FLIPS):
            px, py = _flip(x, y, ox, oy)
            cps.append(_remote(ins[k].at[2 * px + py], outs[k].at[idx], ssem.at[k, idx], rsem.at[k, idx], (px, py, c)))
    return cps


def _rs_chips_start(ins, outs, ssem, rsem):
    for cp in _rs_chip_copies(ins, outs, ssem, rsem):
        cp.start()


def _rs_chips_finish(ins, outs, ssem, rsem):
    for cp in _rs_chip_copies(ins, outs, ssem, rsem):
        cp.wait()


def _rs_chips_shapes(psums):
    nk = len(psums)
    return ([_sds((3,) + tuple(p.shape[1:]), p.dtype) for p in psums],
            [pltpu.SemaphoreType.DMA((nk, 3)), pltpu.SemaphoreType.DMA((nk, 3))])


def _rs_share(fulls):
    nk = len(fulls)

    def body(*refs):
        ins, outs = refs[:nk], refs[nk:2 * nk]
        ssem, rsem = refs[2 * nk:]
        x, y, c = _place()
        cps = []
        for k in range(nk):
            rows = fulls[k].shape[0]
            cp = _remote(ins[k].at[_half(rows, c)], outs[k].at[_half(rows, c)], ssem.at[k], rsem.at[k], (x, y, 1 - c))
            cp.start()
            cps.append(cp)
        for k, cp in enumerate(cps):
            rows = fulls[k].shape[0]
            cp.wait_send()
            theirs = outs[k].at[_half(rows, 1 - c)]
            _remote(theirs, theirs, ssem.at[k], rsem.at[k], (x, y, 1 - c)).wait_recv()

    return pl.pallas_call(
        body, name="rs_share", in_specs=[ANY] * nk, out_specs=[ANY] * nk,
        out_shape=[_sds(f.shape, f.dtype) for f in fulls], input_output_aliases={k: k for k in range(nk)},
        scratch_shapes=[pltpu.SemaphoreType.DMA((nk,)), pltpu.SemaphoreType.DMA((nk,))],
    )(*fulls)


def _allreduce_small(v):
    ndev = 8

    def body(in_ref, out_ref, buf, ssem, rsem):
        x, y, c = _place()
        me = 4 * x + 2 * y + c
        buf[me] = in_ref[...]
        cps = []
        for k in range(1, ndev):
            ox, oy, oc = (k >> 2) & 1, (k >> 1) & 1, k & 1
            px, py = _flip(x, y, ox, oy)
            pc = 1 - c if oc else c
            cp = pltpu.make_async_remote_copy(src_ref=in_ref, dst_ref=buf.at[me], send_sem=ssem.at[k - 1],
                                              recv_sem=rsem.at[k - 1], device_id=(px, py, pc), device_id_type=MESH)
            cp.start()
            cps.append((cp, 4 * px + 2 * py + pc, (px, py, pc)))
        for k, (cp, src, peer) in enumerate(cps):
            cp.wait_send()
            pltpu.make_async_remote_copy(src_ref=in_ref, dst_ref=buf.at[src], send_sem=ssem.at[k],
                                         recv_sem=rsem.at[k], device_id=peer, device_id_type=MESH).wait_recv()
        acc = buf[0]
        for i in range(1, ndev):
            acc = acc + buf[i]
        out_ref[...] = acc

    return pl.pallas_call(
        body, name="allreduce_small",
        in_specs=[pl.BlockSpec(memory_space=pltpu.VMEM)], out_specs=pl.BlockSpec(memory_space=pltpu.VMEM),
        out_shape=_sds(v.shape, v.dtype),
        scratch_shapes=[pltpu.VMEM((ndev,) + v.shape, v.dtype), pltpu.SemaphoreType.DMA((ndev - 1,)),
                        pltpu.SemaphoreType.DMA((ndev - 1,))],
    )(v)


def _rs_sum1(name, g, recv, c_idx):
    _, r, cdim = g.shape
    hr = r // 2
    tr = min(hr, 256)
    nr = hr // tr

    def body(c_ref, g_ref, r_ref, o32_ref, o16_ref):
        v = g_ref[...] + r_ref[...].astype(F32)
        o32_ref[...] = v
        o16_ref[...] = v.astype(BF16)

    spec = pl.BlockSpec((None, tr, cdim), lambda j, i, c_ref: (j, i, 0))
    return pl.pallas_call(
        body, name=name,
        grid_spec=pltpu.PrefetchScalarGridSpec(
            num_scalar_prefetch=1, grid=(N_CHIPS, nr),
            in_specs=[pl.BlockSpec((None, tr, cdim), lambda j, i, c_ref: (j, c_ref[0] * nr + i, 0)), spec],
            out_specs=[spec, spec]),
        out_shape=[_sds((N_CHIPS, hr, cdim), F32), _sds((N_CHIPS, hr, cdim), BF16)],
        compiler_params=_cp(("parallel", "parallel")),
    )(c_idx, g, recv)


def _rs_sum2(name, p32, recv, jc_idx):
    _, hr, cdim = p32.shape
    tr = min(hr, 256)
    nr = hr // tr

    def body(jc_ref, p_ref, r_ref, o_ref):
        o_ref[...] = ((p_ref[...] + r_ref[0].astype(F32)) + r_ref[1].astype(F32)) + r_ref[2].astype(F32)

    return pl.pallas_call(
        body, name=name,
        grid_spec=pltpu.PrefetchScalarGridSpec(
            num_scalar_prefetch=1, grid=(nr,),
            in_specs=[pl.BlockSpec((None, tr, cdim), lambda i, jc: (jc[0], i, 0)),
                      pl.BlockSpec((3, tr, cdim), lambda i, jc: (0, i, 0))],
            out_specs=pl.BlockSpec((tr, cdim), lambda i, jc: (jc[1] * nr + i, 0))),
        out_shape=_sds((2 * hr, cdim), F32),
        compiler_params=_cp(("parallel",)),
    )(jc_idx, p32, recv)


def _adamw(name, w, g, m, v):
    r, cdim = w.shape
    tr = min(r, 256)
    c1 = 1.0 - ADAM_B1 ** ADAM_STEP
    c2 = 1.0 - ADAM_B2 ** ADAM_STEP

    def body(w_ref, g_ref, m_ref, v_ref, d_ref, nm_ref, nv_ref):
        gv = g_ref[...]
        nm = ADAM_B1 * m_ref[...] + (1.0 - ADAM_B1) * gv
        nv = ADAM_B2 * v_ref[...] + (1.0 - ADAM_B2) * (gv * gv)
        d_ref[...] = -ADAM_LR * ((nm / c1) / (jnp.sqrt(nv / c2) + ADAM_EPS) + ADAM_WD * w_ref[...])
        nm_ref[...] = nm
        nv_ref[...] = nv

    spec = pl.BlockSpec((tr, cdim), lambda i: (i, 0))
    return pl.pallas_call(
        body, name=name, grid=(r // tr,), in_specs=[spec] * 4, out_specs=[spec] * 3,
        out_shape=[_sds((r, cdim), F32)] * 3, compiler_params=_cp(("parallel",)),
    )(w, g, m, v)


def _pack_small(mix_pre, attn_out, lb_logits, hgrn_out, mix_post, mlp_pre, mlp_post):
    rows = [mix_pre, jnp.concatenate([attn_out, hgrn_out], axis=1),
            jnp.concatenate([lb_logits[0:1], lb_logits[1:2]], axis=1), mix_post, mlp_pre, mlp_post,
            jnp.zeros((2, D_MODEL), F32)]
    return jnp.concatenate(rows, axis=0)


def _unpack_small(p):
    return (p[0:1], p[1:2, :ATTN_W], jnp.concatenate([p[2:3, :HGRN_W], p[2:3, HGRN_W:]], axis=0),
            p[1:2, ATTN_W:], p[3:4], p[4:5], p[5:6])


def kernel(x, mix_pre_norm, w_in, attn_out_norm, hgrn_lb_logits, hgrn_out_norm, w_out, mix_post_norm, mlp_pre_norm, w_ff1, w_ff2, mlp_post_norm, loss_target, m_mix_pre_norm, m_w_in, m_attn_out_norm, m_hgrn_lb_logits, m_hgrn_out_norm, m_w_out, m_mix_post_norm, m_mlp_pre_norm, m_w_ff1, m_w_ff2, m_mlp_post_norm, v_mix_pre_norm, v_w_in, v_attn_out_norm, v_hgrn_lb_logits, v_hgrn_out_norm, v_w_out, v_mix_post_norm, v_mlp_pre_norm, v_w_ff1, v_w_ff2, v_mlp_post_norm):
    s = x.shape[1]
    xs = x.reshape(s, D_MODEL)
    tgt = loss_target.reshape(s, D_MODEL)
    cx, cy, cc = _place()
    chip = 2 * cx + cy
    c_idx = jnp.reshape(cc, (1,)).astype(jnp.int32)
    jc_idx = jnp.stack([chip, cc]).astype(jnp.int32)

    big_w = [w_in[0], w_out[0], w_ff1[0], w_ff2[0]]
    big_m = [m_w_in[0], m_w_out[0], m_w_ff1[0], m_w_ff2[0]]
    big_v = [v_w_in[0], v_w_out[0], v_w_ff1[0], v_w_ff2[0]]
    shards = [w.astype(BF16) for w in big_w]

    h, wg_in = _rows_call("norm_in", lambda xv, g: ((xv * _rstd(xv) * g),),
                          [(xs, _row(D_MODEL)), (mix_pre_norm, "full")], [(D_MODEL, BF16, "row")], s,
                          exchange=(shards[:1], *_ag_shapes(shards[:1]), _ag_start, _ag_finish))
    wg_in = _with_own(wg_in, shards[0], chip)
    (proj,) = _mm_cols("mm_proj", h, wg_in, NN, [F32])
    hg_o, rec, states = _hgrn_fwd(proj, hgrn_lb_logits, hgrn_out_norm)
    attn_o, attn_lse, wg_out, wg_1, wg_2 = _attn_fwd(proj, shards[1:])
    wg_out, wg_1, wg_2 = (_with_own(g, w, chip) for g, w in zip((wg_out, wg_1, wg_2), shards[1:]))
    (attn_n,) = _rows_call("attn_norm", lambda o, gain: (o * _rstd(o) * gain,),
                           [(attn_o, _row(ATTN_W)), (attn_out_norm, "full")], [(ATTN_W, BF16, "row")], s)
    cat = jnp.concatenate([attn_n, rec], axis=1)

    def post1(mv, xv, g_post, g_pre2):
        x1 = xv + mv * _rstd(mv) * g_post
        return mv, x1, x1 * _rstd(x1) * g_pre2

    mixed, x1, h2 = _rows_call(
        "mm_mixed", post1, [(xs, _row(D_MODEL)), (mix_post_norm, "full"), (mlp_pre_norm, "full")],
        [(D_MODEL, F32, "row"), (D_MODEL, F32, "row"), (D_MODEL, BF16, "row")], s,
        matmul=(cat, wg_out.reshape(D_MODEL, D_MODEL)))

    def sq_relu(u):
        r = jnp.maximum(u, 0.0)
        return (r * r,)

    (act,) = _mm_cols("mm_ff1", h2, wg_1, NN, [BF16], epi=sq_relu)

    def post2(fv, x1v, tv, g):
        y = x1v + fv * _rstd(fv) * g
        dy = (y - tv) * (1.0 / D_MODEL)
        err = y - tv
        loss = 0.5 * jnp.sum(jnp.mean(err * err, axis=-1, keepdims=True), axis=0, keepdims=True)
        dff, dgc = _norm_bwd(fv, g, dy)
        return dy, dff, _colsum(dgc), jnp.broadcast_to(loss, (1, BLK))

    dy, dff, g_mlp_post, loss_part = _rows_call(
        "mm_ff2", post2, [(x1, _row(D_MODEL)), (tgt, _row(D_MODEL)), (mlp_post_norm, "full")],
        [(D_MODEL, F32, "row"), (D_MODEL, BF16, "row"), (D_MODEL, F32, "acc"), (BLK, F32, "acc")], s,
        matmul=(act, wg_2.reshape(D_FF, D_MODEL)))

    (du,) = _mm_cols("mm_du", dff, wg_2, NT, [BF16], epi=lambda acc, a: (acc * (2.0 * jnp.sqrt(a.astype(F32))),),
                     extras=(act,))
    gw_2 = _mm_wgrad("mm_gw2", act, dff, True)
    gw_1 = _mm_wgrad("mm_gw1", h2, du, False)

    def bwd_mid(dh2v, dyv, x1v, mv, g_pre2, g_post):
        d1, gc1 = _norm_bwd(x1v, g_pre2, dh2v)
        dx1 = dyv + d1
        dm, gc2 = _norm_bwd(mv, g_post, dx1)
        return dx1, dm, _colsum(gc1), _colsum(gc2)

    w1_t = jnp.transpose(wg_1, (0, 2, 1)).reshape(D_FF, D_MODEL)
    dx1, dmixed, g_mlp_pre, g_mix_post, *from_pair = _rows_call(
        "mm_dh2", bwd_mid, [(dy, _row(D_MODEL)), (x1, _row(D_MODEL)), (mixed, _row(D_MODEL)),
                            (mlp_pre_norm, "full"), (mix_post_norm, "full")],
        [(D_MODEL, F32, "row"), (D_MODEL, BF16, "row"), (D_MODEL, F32, "acc"), (D_MODEL, F32, "acc")], s,
        matmul=(du, w1_t), exchange=_rs_pair_exchange([gw_1[1], gw_2[1]]))

    (dcat,) = _mm_cols("mm_dcat", dmixed, wg_out, NT, [F32])
    gw_out = _mm_wgrad("mm_gwout", cat, dmixed, True)
    names = ["out", "ff1", "ff2", "in"]
    ready = [gw_out, gw_1, gw_2]
    from_pair = list(_rs_pair("rs_pair_out", [gw_out[1]])) + from_pair
    pair = [_rs_sum1(f"rs_sum1_{n}", g[0], r, c_idx) for n, g, r in zip(names, ready, from_pair)]

    def attn_norm_bwd(dc, o, gain):
        do, gc = _norm_bwd(o, gain, dc)
        t = do * o
        lane = lax.broadcasted_iota(jnp.int32, (t.shape[0], BLK), 1) < 64
        parts = []
        for p in range(ATTN_W // BLK):
            tp = t[:, p * BLK:(p + 1) * BLK]
            sa = jnp.sum(jnp.where(lane, tp, 0.0), axis=1, keepdims=True)
            sb = jnp.sum(jnp.where(lane, 0.0, tp), axis=1, keepdims=True)
            parts.append(jnp.where(lane, sa, sb))
        return do, jnp.concatenate(parts, axis=1), _colsum(gc)

    do_attn, delta, g_attn_out = _rows_call(
        "attn_norm_bwd", attn_norm_bwd, [(dcat, _row(ATTN_W, 0)), (attn_o, _row(ATTN_W)), (attn_out_norm, "full")],
        [(ATTN_W, F32, "row"), (ATTN_W, F32, "row"), (ATTN_W, F32, "acc")], s)
    dq, dk, dv, *from_chips = _attn_bwd(proj, do_attn, attn_lse, delta, [p[1] for p in pair])
    dhq, dhf, dhi, dhg, g_hgrn_out, g_lb = _hgrn_bwd(proj, hg_o, states, dcat, hgrn_lb_logits, hgrn_out_norm)

    def dproj_asm(*a):
        return (jnp.concatenate([t.astype(BF16) for t in a], axis=1),)

    (dproj,) = _rows_call("dproj_asm", dproj_asm,
                          [(t, _row(ATTN_W)) for t in (dq, dk, dv)] + [(t, _row(HGRN_W)) for t in (dhq, dhf, dhi, dhg)],
                          [(PROJ_W, BF16, "row")], s)
    gw_in = _mm_wgrad("mm_gwin", h, dproj, False)
    (from_pair_in,) = _rs_pair("rs_pair_in", [gw_in[1]])
    pair.append(_rs_sum1("rs_sum1_in", gw_in[0], from_pair_in, c_idx))
    rs_shape, rs_sems = _rs_chips_shapes([pair[3][1]])

    def bwd_in(dhv, dx1v, xv, g):
        d0, gc = _norm_bwd(xv, g, dhv)
        return dx1v + d0, _colsum(gc)

    w_in_t = jnp.transpose(wg_in, (0, 2, 1)).reshape(PROJ_W, D_MODEL)
    grad_x, g_mix_pre, from_chips_in = _rows_call(
        "mm_dh", bwd_in, [(dx1, _row(D_MODEL)), (xs, _row(D_MODEL)), (mix_pre_norm, "full")],
        [(D_MODEL, F32, "row"), (D_MODEL, F32, "acc")], s, matmul=(dproj, w_in_t),
        exchange=([pair[3][1]], rs_shape, rs_sems, _rs_chips_start, _rs_chips_finish))
    from_chips.append(from_chips_in)

    loss = lax.psum(loss_part[0, 0], ("x", "y", "c"))
    small_g = _allreduce_small(_pack_small(g_mix_pre, g_attn_out, g_lb, g_hgrn_out, g_mix_post, g_mlp_pre, g_mlp_post))

    reduced = [_rs_sum2(f"rs_sum2_{n}", p[0], r, jc_idx) for n, p, r in zip(names, pair, from_chips)]
    g_wout, g_w1, g_w2, g_win = _rs_share(reduced)
    full = [g_win, g_wout, g_w1, g_w2]

    upd = [_adamw(f"adamw_{n}", w, g, m, v) for n, w, g, m, v in zip(("in", "out", "ff1", "ff2"), big_w, full, big_m, big_v)]
    small_w = _pack_small(mix_pre_norm, attn_out_norm, hgrn_lb_logits, hgrn_out_norm, mix_post_norm, mlp_pre_norm,
                          mlp_post_norm)
    small_m = _pack_small(m_mix_pre_norm, m_attn_out_norm, m_hgrn_lb_logits, m_hgrn_out_norm, m_mix_post_norm,
                          m_mlp_pre_norm, m_mlp_post_norm)
    small_v = _pack_small(v_mix_pre_norm, v_attn_out_norm, v_hgrn_lb_logits, v_hgrn_out_norm, v_mix_post_norm,
                          v_mlp_pre_norm, v_mlp_post_norm)
    small_upd = _adamw("adamw_small", small_w, small_g, small_m, small_v)

    def assemble(small, big):
        sm = _unpack_small(small)
        return (sm[0], big[0][None], sm[1], sm[2], sm[3], big[1][None], sm[4], sm[5], big[2][None], big[3][None], sm[6])

    g_out = assemble(small_g, full)
    d_out = assemble(small_upd[0], [u[0] for u in upd])
    m_out = assemble(small_upd[1], [u[1] for u in upd])
    v_out = assemble(small_upd[2], [u[2] for u in upd])
    return (loss, grad_x.reshape(x.shape), *g_out, *d_out, *m_out, *v_out)
```

```python
import numpy as np
import jax
import jax.numpy as jnp
from jax import lax
from jax.experimental import pallas as pl
from jax.experimental.pallas import tpu as pltpu

F32 = jnp.float32
BF16 = jnp.bfloat16
MESH = pl.DeviceIdType.MESH
ANY = pl.BlockSpec(memory_space=pl.ANY)

RMS_EPS = 1e-6
D_MODEL = 1024
ATTN_W = 512
HGRN_W = 512
PROJ_W = 3584
D_FF = 4096
N_CHIPS = 4
BLK = 128
CHUNK = 64
HGRN_TB = 512
ATTN_GROUP = 8
DILATIONS = (1, 4, 16)
ATTN_SCALE = 0.125
ROW_TILE = 512
MM_TILE = 1024
VMEM_LIMIT = 48 * 2 ** 20
FLIPS = ((1, 0), (0, 1), (1, 1))

ADAM_LR, ADAM_B1, ADAM_B2, ADAM_EPS, ADAM_WD, ADAM_STEP = 0.001, 0.9, 0.999, 1e-08, 0.01, 10


def _cp(sem=None):
    return pltpu.CompilerParams(dimension_semantics=sem, vmem_limit_bytes=VMEM_LIMIT)


def _sigmoid(v):
    return 1.0 / (1.0 + jnp.exp(-v))


def _dot(a, b, contract, precision=None):
    return lax.dot_general(a, b, (contract, ((), ())), preferred_element_type=F32, precision=precision)


NN = ((1,), (0,))
NT = ((1,), (1,))
TN = ((0,), (0,))


def _sds(shape, dtype):
    return jax.ShapeDtypeStruct(shape, dtype)


def _resident(shape):
    return pl.BlockSpec(shape, lambda *_: (0,) * len(shape), pipeline_mode=pl.Buffered(1))


def _mm_cols(name, a, w, contract, out_dtypes, epi=None, extras=()):
    m, k = a.shape
    jn = w.shape[0]
    nj = w.shape[2] if contract == NN else w.shape[1]
    tm = min(m, MM_TILE)
    n_ex, parts = len(extras), 2

    def body(a_ref, w_ref, *rest):
        ex, out_refs = rest[:n_ex], rest[n_ex:]
        i = pl.program_id(1)
        part = tm // parts
        for h in range(parts):
            rows = slice(h * part, (h + 1) * part)
            acc = _dot(a_ref[pl.ds(pl.multiple_of(i * tm + h * part, part), part), :], w_ref[...], contract)
            res = epi(acc, *[e[rows, :] for e in ex]) if epi else (acc,)
            for o, r in zip(out_refs, res):
                o[rows, :] = r.astype(o.dtype)

    blk = pl.BlockSpec((tm, nj), lambda j, i: (i, j))
    return pl.pallas_call(
        body, name=name, grid=(jn, m // tm),
        in_specs=[_resident((m, k)), pl.BlockSpec((None,) + w.shape[1:], lambda j, i: (j, 0, 0))] + [blk] * n_ex,
        out_specs=[blk] * len(out_dtypes), out_shape=[_sds((m, jn * nj), dt) for dt in out_dtypes],
        compiler_params=_cp(("parallel", "parallel")),
    )(a, w, *extras)


def _mm_wgrad(name, a, b, a_by_j):
    s = a.shape[0]
    if a_by_j:
        r, c = a.shape[1] // N_CHIPS, b.shape[1]
        in_specs = [pl.BlockSpec((s, r), lambda j: (0, j)), _resident((s, c))]
    else:
        r, c = a.shape[1], b.shape[1] // N_CHIPS
        in_specs = [_resident((s, r)), pl.BlockSpec((s, c), lambda j: (0, j))]
    tr = min(r, 512)

    def body(a_ref, b_ref, o32_ref, o16_ref):
        for h in range(r // tr):
            cols = slice(h * tr, (h + 1) * tr)
            acc = _dot(a_ref[:, cols], b_ref[...], TN)
            o32_ref[cols, :] = acc
            o16_ref[cols, :] = acc.astype(BF16)

    out = pl.BlockSpec((None, r, c), lambda j: (j, 0, 0))
    return pl.pallas_call(
        body, name=name, grid=(N_CHIPS,), in_specs=in_specs, out_specs=[out, out],
        out_shape=[_sds((N_CHIPS, r, c), F32), _sds((N_CHIPS, r, c), BF16)], compiler_params=_cp(("parallel",)),
    )(a, b)


def _rows_call(name, fn, ins, outs, s, tm=ROW_TILE, matmul=None, exchange=None):
    in_specs = []
    if matmul:
        a, w = matmul
        in_specs += [pl.BlockSpec((tm, a.shape[1]), lambda i: (i, 0)), _resident(w.shape)]
    for arr, kind in ins:
        if kind == "full":
            in_specs.append(pl.BlockSpec(arr.shape, lambda i: (0, 0)))
        else:
            _, w_, cb = kind
            in_specs.append(pl.BlockSpec((tm, w_), lambda i, cb=cb: (i, cb)))
    out_specs, out_shape, is_acc = [], [], []
    for w_, dt, kind in outs:
        if kind == "acc":
            out_specs.append(pl.BlockSpec((1, w_), lambda i: (0, 0)))
            out_shape.append(_sds((1, w_), dt))
        else:
            out_specs.append(pl.BlockSpec((tm, w_), lambda i: (i, 0)))
            out_shape.append(_sds((s, w_), dt))
        is_acc.append(kind == "acc")
    n_mm, n_in, n_out = (2 if matmul else 0), len(ins), len(outs)
    x_ins, x_shapes, x_sems, x_start, x_finish = exchange if exchange else ((), [], [], None, None)
    n_x = len(x_ins)
    steps = s // tm

    def body(*refs):
        in_refs, xi = refs[n_mm:n_mm + n_in], refs[n_mm + n_in:n_mm + n_in + n_x]
        out_refs = refs[n_mm + n_in + n_x:n_mm + n_in + n_x + n_out]
        xo, sems = refs[n_mm + n_in + n_x + n_out:n_mm + n_in + 2 * n_x + n_out], refs[n_mm + n_in + 2 * n_x + n_out:]
        i = pl.program_id(0)

        if exchange:
            @pl.when(i == 0)
            def _():
                x_start(xi, xo, *sems)

        for o, acc in zip(out_refs, is_acc):
            if acc:
                @pl.when(i == 0)
                def _(o=o):
                    o[...] = jnp.zeros_like(o)

        parts = 2 if matmul else 1
        for h in range(parts):
            rows = slice(h * (tm // parts), (h + 1) * (tm // parts))
            args = [r[...] if kind == "full" else r[rows, :] for r, (_, kind) in zip(in_refs, ins)]
            if matmul:
                a_ref, w_ref = refs[:2]
                if len(w_ref.shape) == 2:
                    acc = _dot(a_ref[rows, :], w_ref[...], NN)
                else:
                    kj = w_ref.shape[2]
                    acc = _dot(a_ref[rows, 0:kj], w_ref[0], NT)
                    for j in range(1, w_ref.shape[0]):
                        acc = acc + _dot(a_ref[rows, j * kj:(j + 1) * kj], w_ref[j], NT)
                args.insert(0, acc)
            for o, r, acc in zip(out_refs, fn(*args), is_acc):
                if acc:
                    o[...] += r.astype(o.dtype)
                else:
                    o[rows, :] = r.astype(o.dtype)

        if exchange:
            @pl.when(i == steps - 1)
            def _():
                x_finish(xi, xo, *sems)

    sem = ("arbitrary",) if any(is_acc) or exchange else ("parallel",)
    return pl.pallas_call(
        body, name=name, grid=(steps,), in_specs=in_specs + [ANY] * n_x, out_specs=out_specs + [ANY] * n_x,
        out_shape=out_shape + list(x_shapes), scratch_shapes=list(x_sems), compiler_params=_cp(sem),
    )(*(matmul or ()), *[a for a, _ in ins], *x_ins)


def _rstd(v):
    return lax.rsqrt(jnp.mean(v * v, axis=-1, keepdims=True) + RMS_EPS)


def _norm_bwd(v, gain, dy):
    r = _rstd(v)
    n = v * r
    dn = dy * gain
    dv = r * (dn - n * jnp.mean(dn * n, axis=-1, keepdims=True))
    return dv, dy * n


def _colsum(v):
    return jnp.sum(v, axis=0, keepdims=True)


def _row(w, cb=0):
    return ("row", w, cb)


N_PAIRS = ATTN_W // BLK


def _head_col(v, mask):
    return jnp.max(jnp.where(mask, v, -jnp.inf), axis=1, keepdims=True)


def _slopes():
    t = np.zeros((N_PAIRS, 8, 2 * BLK), np.float32)
    for p in range(N_PAIRS):
        for hh in range(2):
            t[p, hh, :] = 2.0 ** -(2 * p + hh + 1)
    return jnp.asarray(t)


def _rows(n, r, d):
    base = pl.multiple_of(n * (BLK * d), BLK)
    return pl.ds(base + r, BLK, stride=d) if d > 1 else pl.ds(base, BLK)


def _attn_bias(sl_ref, bias_scr):
    row = lax.broadcasted_iota(jnp.int32, (BLK, 2 * BLK), 0)
    col = lax.broadcasted_iota(jnp.int32, (BLK, 2 * BLK), 1)
    dist = row + BLK - col
    in_window = (dist >= 0) & (dist <= BLK)
    distf = dist.astype(F32)
    for di, d in enumerate(DILATIONS):
        for hh in range(2):
            bias_scr[2 * di + hh] = jnp.where(in_window, -(sl_ref[hh:hh + 1, :] * float(d)) * distf, -1e30)


def _first_block_penalty(n):
    col = lax.broadcasted_iota(jnp.int32, (1, 2 * BLK), 1)
    return jnp.where(col + n * BLK >= BLK, 0.0, -1e30)


def _attn_groups(s, d):
    nb = s // (BLK * d)
    g = ATTN_GROUP
    if d >= g:
        return [(nb, lambda n, r0=r0: [(n, r0 + u) for u in range(g)]) for r0 in range(0, d, g)]
    per = g // d
    return [(nb // per, lambda t: [(per * t + u, r) for u in range(per) for r in range(d)])]


def _attn_fwd(proj, shards):
    s = proj.shape[0]
    nk = len(shards)

    def body(sl_ref, q_ref, k_ref, v_ref, *rest):
        w_refs, (o_ref, l_ref) = rest[:nk], rest[nk:nk + 2]
        wg_refs, (bias_scr, ssem, rsem) = rest[nk + 2:2 * nk + 2], rest[2 * nk + 2:]
        pair = pl.program_id(0)

        @pl.when(pair == 0)
        def _():
            _ag_start(w_refs, wg_refs, ssem, rsem)

        _attn_bias(sl_ref, bias_scr)
        lane_q = lax.broadcasted_iota(jnp.int32, (BLK, BLK), 1) < 64
        lane_k = lax.broadcasted_iota(jnp.int32, (2 * BLK, BLK), 1) < 64

        def branch(n, r, di):
            d = DILATIONS[di]
            rows = _rows(n, r, d)
            prev = _rows(jnp.maximum(n - 1, 0), r, d)
            pen = _first_block_penalty(n)
            q2 = q_ref[rows, :] * ATTN_SCALE
            kk = jnp.concatenate([k_ref[prev, :], k_ref[rows, :]], axis=0).astype(BF16)
            vv = jnp.concatenate([v_ref[prev, :], v_ref[rows, :]], axis=0)
            o2 = jnp.zeros((BLK, BLK), F32)
            lse2 = jnp.zeros((BLK, BLK), F32)
            for hh in range(2):
                mq = lane_q if hh == 0 else ~lane_q
                mk = lane_k if hh == 0 else ~lane_k
                qm = jnp.where(mq, q2, 0.0).astype(BF16)
                sc = _dot(qm, kk, NT) + bias_scr[2 * di + hh] + pen
                m = jnp.max(sc, axis=1, keepdims=True)
                pr = jnp.exp(sc - m)
                den = jnp.sum(pr, axis=1, keepdims=True)
                vm = jnp.where(mk, vv, 0.0).astype(BF16)
                o2 = o2 + _dot(pr.astype(BF16), vm, NN) / den
                lse2 = jnp.where(mq, m + jnp.log(den), lse2)
            return rows, o2, lse2

        def merge(rows, o2, lse2, first):
            if first:
                o_ref[rows, :] = o2
                l_ref[rows, :] = lse2
            else:
                lo = l_ref[rows, :]
                mx = jnp.maximum(lo, lse2)
                ln = mx + jnp.log(jnp.exp(lo - mx) + jnp.exp(lse2 - mx))
                o_ref[rows, :] = jnp.exp(lo - ln) * o_ref[rows, :] + jnp.exp(lse2 - ln) * o2
                l_ref[rows, :] = ln

        for di, d in enumerate(DILATIONS):
            for trips, blocks in _attn_groups(s, d):
                def trip(t, carry, di=di, blocks=blocks):
                    done = [branch(n, r, di) for n, r in blocks(t)]
                    for rows, o2, lse2 in done:
                        merge(rows, o2, lse2, di == 0)
                    return carry

                lax.fori_loop(0, trips, trip, 0)

        @pl.when(pair == N_PAIRS - 1)
        def _():
            _ag_finish(w_refs, wg_refs, ssem, rsem)

    cb = lambda base: pl.BlockSpec((s, BLK), lambda p, base=base: (0, base + p))
    out = pl.BlockSpec((s, BLK), lambda p: (0, p))
    ag_shape, ag_sems = _ag_shapes(shards)
    return pl.pallas_call(
        body, name="attn_fwd", grid=(N_PAIRS,),
        in_specs=[pl.BlockSpec((None, 8, 2 * BLK), lambda p: (p, 0, 0)), cb(0), cb(N_PAIRS), cb(2 * N_PAIRS)]
        + [ANY] * nk,
        out_specs=[out, out] + [ANY] * nk, out_shape=[_sds((s, ATTN_W), F32)] * 2 + ag_shape,
        scratch_shapes=[pltpu.VMEM((2 * len(DILATIONS), BLK, 2 * BLK), F32)] + ag_sems,
        compiler_params=_cp(("arbitrary",)),
    )(_slopes(), proj, proj, proj, *shards)


def _attn_bwd(proj, do, lse, delta, psums):
    s = proj.shape[0]
    nk = len(psums)

    def body(sl_ref, q_ref, k_ref, v_ref, do_ref, l_ref, e_ref, *rest):
        p_refs, (dq_ref, dk_ref, dv_ref) = rest[:nk], rest[nk:nk + 3]
        got_refs, (bias_scr, ssem, rsem) = rest[nk + 3:2 * nk + 3], rest[2 * nk + 3:]
        pair = pl.program_id(0)

        @pl.when(pair == 0)
        def _():
            _rs_chips_start(p_refs, got_refs, ssem, rsem)

        _attn_bias(sl_ref, bias_scr)
        lane_q = lax.broadcasted_iota(jnp.int32, (BLK, BLK), 1) < 64
        lane_k = lax.broadcasted_iota(jnp.int32, (2 * BLK, BLK), 1) < 64
        dk_ref[...] = jnp.zeros_like(dk_ref)
        dv_ref[...] = jnp.zeros_like(dv_ref)

        def branch(n, r, di):
            d = DILATIONS[di]
            rows = _rows(n, r, d)
            prev = _rows(jnp.maximum(n - 1, 0), r, d)
            pen = _first_block_penalty(n)
            q1, d1, l1, e1 = q_ref[rows, :] * ATTN_SCALE, do_ref[rows, :], l_ref[rows, :], e_ref[rows, :]
            kk = jnp.concatenate([k_ref[prev, :], k_ref[rows, :]], axis=0)
            kkb = kk.astype(BF16)
            vvb = jnp.concatenate([v_ref[prev, :], v_ref[rows, :]], axis=0).astype(BF16)
            dq2 = jnp.zeros((BLK, BLK), F32)
            dkk = jnp.zeros((2 * BLK, BLK), F32)
            dvv = jnp.zeros((2 * BLK, BLK), F32)
            for hh in range(2):
                mq = lane_q if hh == 0 else ~lane_q
                mk = lane_k if hh == 0 else ~lane_k
                qm = jnp.where(mq, q1, 0.0).astype(BF16)
                dm = jnp.where(mq, d1, 0.0).astype(BF16)
                sc = _dot(qm, kkb, NT) + bias_scr[2 * di + hh] + pen
                pr = jnp.exp(sc - _head_col(l1, mq))
                ds = (pr * (_dot(dm, vvb, NT) - _head_col(e1, mq))).astype(BF16)
                km = jnp.where(mk, kk, 0.0).astype(BF16)
                dq2 = dq2 + _dot(ds, km, NN)
                dkk = dkk + _dot(ds, qm, TN)
                dvv = dvv + _dot(pr.astype(BF16), dm, TN)
            return rows, prev, dq2 * ATTN_SCALE, dkk, dvv

        for di, d in enumerate(DILATIONS):
            for trips, blocks in _attn_groups(s, d):
                def trip(t, carry, di=di, blocks=blocks, first=(di == 0)):
                    done = [branch(n, r, di) for n, r in blocks(t)]
                    for rows, prev, dq2, dkk, dvv in done:
                        dq_ref[rows, :] = dq2 if first else dq_ref[rows, :] + dq2
                        dk_ref[prev, :] = dk_ref[prev, :] + dkk[:BLK]
                        dk_ref[rows, :] = dk_ref[rows, :] + dkk[BLK:]
                        dv_ref[prev, :] = dv_ref[prev, :] + dvv[:BLK]
                        dv_ref[rows, :] = dv_ref[rows, :] + dvv[BLK:]
                    return carry

                lax.fori_loop(0, trips, trip, 0)

        @pl.when(pair == N_PAIRS - 1)
        def _():
            _rs_chips_finish(p_refs, got_refs, ssem, rsem)

    cb = lambda base: pl.BlockSpec((s, BLK), lambda p, base=base: (0, base + p))
    out = pl.BlockSpec((s, BLK), lambda p: (0, p))
    rs_shape, rs_sems = _rs_chips_shapes(psums)
    return pl.pallas_call(
        body, name="attn_bwd", grid=(N_PAIRS,),
        in_specs=[pl.BlockSpec((None, 8, 2 * BLK), lambda p: (p, 0, 0)), cb(0), cb(N_PAIRS), cb(2 * N_PAIRS),
                  out, out, out] + [ANY] * nk,
        out_specs=[out] * 3 + [ANY] * nk, out_shape=[_sds((s, ATTN_W), F32)] * 3 + rs_shape,
        scratch_shapes=[pltpu.VMEM((2 * len(DILATIONS), BLK, 2 * BLK), F32)] + rs_sems,
        compiler_params=_cp(("arbitrary",)),
    )(_slopes(), proj, proj, proj, do, lse, delta, *psums)


def _lower_bound(lbl):
    return 1.0 / (1.0 + jnp.exp(lbl[1:2, :] - lbl[0:1, :]))


def _hi(a):
    bits = lax.bitcast_convert_type(a, jnp.uint32) & jnp.uint32(0xFFFF0000)
    return lax.bitcast_convert_type(bits, F32)


def _dot3(a, b, contract):
    ah, bh = _hi(a), _hi(b)
    al, bl = (a - ah).astype(BF16), (b - bh).astype(BF16)
    ah, bh = ah.astype(BF16), bh.astype(BF16)
    return _dot(ah, bh, contract) + (_dot(ah, bl, contract) + _dot(al, bh, contract))


def _cumsum_rows(tri, g):
    g1 = _hi(g)
    r1 = g - g1
    g2 = _hi(r1)
    g3 = r1 - g2
    return _dot(tri, g1.astype(BF16), NN) + (_dot(tri, g2.astype(BF16), NN) + _dot(tri, g3.astype(BF16), NN))


def _heads(fn):
    return jnp.concatenate([fn(slice(h * BLK, (h + 1) * BLK)) for h in range(HGRN_W // BLK)], axis=1)


def _head_mean(t):
    return _heads(lambda hs: jnp.broadcast_to(jnp.mean(t[:, hs], axis=1, keepdims=True), (t.shape[0], BLK)))


def _hgrn_chunk(q_ref, f_ref, i_ref, sl, lb, tri):
    qp = q_ref[sl, :]
    sq = _sigmoid(qp)
    qf = qp * sq
    sg = _sigmoid(f_ref[sl, :])
    f = lb + (1.0 - lb) * sg
    kf = 1.0 - f
    v = i_ref[sl, :]
    b = _cumsum_rows(tri, jnp.log(f))
    bm = b[CHUNK // 2:CHUNK // 2 + 1, :]
    bl = b[CHUNK - 1:CHUNK, :]
    qt = qf * jnp.exp(b - bm)
    kt = kf * jnp.exp(bm - b)
    return qp, sq, qf, sg, f, kf, v, b, bm, bl, qt, kt


def _hgrn_specs(tb, block):
    first = 3 * ATTN_W // HGRN_W
    return [pl.BlockSpec((tb, HGRN_W), lambda i, k=k: (block(i), first + k)) for k in range(4)]


def _hgrn_fwd(proj, lb_logits, out_gain):
    s = proj.shape[0]
    tb = min(HGRN_TB, s)
    nb, cpb, nc = s // tb, tb // CHUNK, s // CHUNK

    def body(q_ref, f_ref, i_ref, g_ref, lbl_ref, gain_ref, o_ref, rec_ref, st_ref, st_scr):
        step = pl.program_id(0)

        @pl.when(step == 0)
        def _():
            st_scr[...] = jnp.zeros_like(st_scr)

        lb = _lower_bound(lbl_ref[...])
        r64 = lax.broadcasted_iota(jnp.int32, (CHUNK, CHUNK), 0)
        c64 = lax.broadcasted_iota(jnp.int32, (CHUNK, CHUNK), 1)
        tril = r64 >= c64
        tri = tril.astype(BF16)
        st = st_scr[...]
        for cc in range(cpb):
            sl = slice(cc * CHUNK, (cc + 1) * CHUNK)
            _, _, qf, _, _, kf, v, b, _, bl, qt, kt = _hgrn_chunk(q_ref, f_ref, i_ref, sl, lb, tri)
            qe = (qf * jnp.exp(b)).astype(BF16)
            kh = (kf * jnp.exp(bl - b)).astype(BF16)
            qtb, ktb, vb, stb = qt.astype(BF16), kt.astype(BF16), v.astype(BF16), st.astype(BF16)

            def out_h(hs):
                a = jnp.where(tril, _dot(qtb[:, hs], ktb[:, hs], NT), 0.0).astype(BF16)
                return _dot(qe[:, hs], stb[:, hs], NT) + _dot(a, vb[:, hs], NN)

            o_ref[sl, :] = _heads(out_h)
            st_ref[cc] = stb
            st = st * jnp.exp(bl) + _heads(lambda hs: _dot(vb[:, hs], kh[:, hs], TN))
        st_scr[...] = st
        o = o_ref[...]
        gate = g_ref[...]
        rec_ref[...] = (o * lax.rsqrt(_head_mean(o * o) + RMS_EPS) * gain_ref[...] * (gate * _sigmoid(gate))).astype(BF16)

    row = pl.BlockSpec((tb, HGRN_W), lambda i: (i, 0))
    return pl.pallas_call(
        body, name="hgrn_fwd", grid=(nb,),
        in_specs=_hgrn_specs(tb, lambda i: i) + [pl.BlockSpec((2, HGRN_W), lambda i: (0, 0)),
                                                 pl.BlockSpec((1, HGRN_W), lambda i: (0, 0))],
        out_specs=[row, row, pl.BlockSpec((cpb, BLK, HGRN_W), lambda i: (i, 0, 0))],
        out_shape=[_sds((s, HGRN_W), F32), _sds((s, HGRN_W), BF16), _sds((nc, BLK, HGRN_W), BF16)],
        scratch_shapes=[pltpu.VMEM((BLK, HGRN_W), F32)],
        compiler_params=_cp(("arbitrary",)),
    )(proj, proj, proj, proj, lb_logits, out_gain)


def _hgrn_bwd(proj, o_pre, states, dcat, lb_logits, out_gain):
    s = proj.shape[0]
    tb = min(HGRN_TB, s)
    nb, cpb, nc = s // tb, tb // CHUNK, s // CHUNK

    def body(q_ref, f_ref, i_ref, g_ref, o_ref, st_ref, stn_ref, dy_ref, lbl_ref, gain_ref,
             dq_ref, df_ref, di_ref, dg_ref, dgain_ref, dlbl_ref, do_scr, dst_scr, dlb_scr):
        step = pl.program_id(0)

        @pl.when(step == 0)
        def _():
            dst_scr[...] = jnp.zeros_like(dst_scr)
            dlb_scr[...] = jnp.zeros_like(dlb_scr)
            dgain_ref[...] = jnp.zeros_like(dgain_ref)

        lb = _lower_bound(lbl_ref[...])
        gain = gain_ref[...]
        o = o_ref[...]
        r = lax.rsqrt(_head_mean(o * o) + RMS_EPS)
        nrm = o * r
        gate = g_ref[...]
        sgt = _sigmoid(gate)
        dy = dy_ref[...]
        dg_ref[...] = (dy * nrm * gain * (sgt * (1.0 + gate * (1.0 - sgt)))).astype(BF16)
        dng = dy * (gate * sgt)
        dgain_ref[...] += _colsum(dng * nrm)
        dn = dng * gain
        do_scr[...] = r * (dn - nrm * _head_mean(dn * nrm))

        r64 = lax.broadcasted_iota(jnp.int32, (CHUNK, CHUNK), 0)
        c64 = lax.broadcasted_iota(jnp.int32, (CHUNK, CHUNK), 1)
        tril = r64 >= c64
        tri = tril.astype(BF16)
        triu = (r64 <= c64).astype(BF16)
        dst = dst_scr[...]
        dlb = dlb_scr[...]
        for cc in reversed(range(cpb)):
            sl = slice(cc * CHUNK, (cc + 1) * CHUNK)
            qp, sq, qf, sg, f, kf, v, b, bm, bl, qt, kt = _hgrn_chunk(q_ref, f_ref, i_ref, sl, lb, tri)
            stf = st_ref[cc].astype(F32)
            st_end = (st_ref[cc + 1] if cc + 1 < cpb else stn_ref[0]).astype(F32)
            csum = jnp.sum(st_end * dst, axis=0, keepdims=True)
            doc = do_scr[sl, :]
            dob, dstb = doc.astype(BF16), dst.astype(BF16)
            eb = jnp.exp(b)
            qe = (qf * eb).astype(BF16)
            kh = (kf * jnp.exp(bl - b)).astype(BF16)
            qtb, ktb = qt.astype(BF16), kt.astype(BF16)
            parts = []
            for h in range(HGRN_W // BLK):
                hs = slice(h * BLK, (h + 1) * BLK)
                da = jnp.where(tril, _dot3(doc[:, hs], v[:, hs], NT), 0.0)
                a = jnp.where(tril, _dot(qtb[:, hs], ktb[:, hs], NT), 0.0).astype(BF16)
                parts.append((
                    _dot3(da, kt[:, hs], NN), _dot3(doc[:, hs], stf[:, hs], NN),
                    _dot3(da, qt[:, hs], TN), _dot3(v[:, hs], dst[:, hs], NN),
                    _dot(a, dob[:, hs], TN) + _dot(kh[:, hs], dstb[:, hs], NT),
                    _dot(dob[:, hs], qe[:, hs], TN)))
            dqt, dqi, dkt, dks, dv, upd = (jnp.concatenate([p[n] for p in parts], axis=1) for n in range(6))
            dqf = dqt * jnp.exp(b - bm) + eb * dqi
            dkf = dkt * jnp.exp(bm - b) + jnp.exp(bl - b) * dks
            gq = qf * dqf - kf * dkf
            dlogf = csum + _cumsum_rows(triu, gq)
            dfv = dlogf / f - dkf
            dq_ref[sl, :] = (dqf * (sq * (1.0 + qp * (1.0 - sq)))).astype(BF16)
            df_ref[sl, :] = (dfv * (1.0 - lb) * sg * (1.0 - sg)).astype(BF16)
            di_ref[sl, :] = dv.astype(BF16)
            dst = dst * jnp.exp(bl) + upd
            dlb = dlb + _colsum(dfv * (1.0 - sg))
        dst_scr[...] = dst
        dlb_scr[...] = dlb

        @pl.when(step == nb - 1)
        def _():
            t = dlb * lb * (1.0 - lb)
            dlbl_ref[...] = jnp.concatenate([t, -t], axis=0)

    rev = lambda i: nb - 1 - i
    row = pl.BlockSpec((tb, HGRN_W), lambda i: (rev(i), 0))
    res = pl.pallas_call(
        body, name="hgrn_bwd", grid=(nb,),
        in_specs=_hgrn_specs(tb, rev) + [
            row, pl.BlockSpec((cpb, BLK, HGRN_W), lambda i: (rev(i), 0, 0)),
            pl.BlockSpec((1, BLK, HGRN_W), lambda i: (jnp.minimum((rev(i) + 1) * cpb, nc - 1), 0, 0)),
            pl.BlockSpec((tb, HGRN_W), lambda i: (rev(i), ATTN_W // HGRN_W)),
            pl.BlockSpec((2, HGRN_W), lambda i: (0, 0)), pl.BlockSpec((1, HGRN_W), lambda i: (0, 0))],
        out_specs=[row, row, row, row, pl.BlockSpec((1, HGRN_W), lambda i: (0, 0)),
                   pl.BlockSpec((2, HGRN_W), lambda i: (0, 0))],
        out_shape=[_sds((s, HGRN_W), BF16)] * 4 + [_sds((1, HGRN_W), F32), _sds((2, HGRN_W), F32)],
        scratch_shapes=[pltpu.VMEM((tb, HGRN_W), F32), pltpu.VMEM((BLK, HGRN_W), F32), pltpu.VMEM((1, HGRN_W), F32)],
        compiler_params=_cp(("arbitrary",)),
    )(proj, proj, proj, proj, o_pre, states, states, dcat, lb_logits, out_gain)
    return res


def _place():
    return lax.axis_index("x"), lax.axis_index("y"), lax.axis_index("c")


def _flip(x, y, ox, oy):
    return (1 - x if ox else x), (1 - y if oy else y)


def _half(rows, cc):
    return pl.ds(cc * (rows // 2), rows // 2)


def _remote(src, dst, ssem, rsem, to):
    return pltpu.make_async_remote_copy(src_ref=src, dst_ref=dst, send_sem=ssem, recv_sem=rsem,
                                        device_id=to, device_id_type=MESH)


def _ag_chip_copies(ins, outs, ssem, rsem):
    x, y, c = _place()
    j = 2 * x + y
    cps = []
    for k in range(len(ins)):
        rows = ins[k].shape[0]
        for idx, (ox, oy) in enumerate(FLIPS):
            px, py = _flip(x, y, ox, oy)
            cps.append(_remote(ins[k].at[_half(rows, c)], outs[k].at[j, _half(rows, c)],
                               ssem.at[k, idx], rsem.at[k, idx], (px, py, c)))
    return cps


def _ag_start(ins, outs, ssem, rsem):
    for cp in _ag_chip_copies(ins, outs, ssem, rsem):
        cp.start()


def _ag_finish(ins, outs, ssem, rsem):
    x, y, c = _place()
    sib = (x, y, 1 - c)
    passed = []
    for k in range(len(ins)):
        rows = ins[k].shape[0]
        for idx, (ox, oy) in enumerate(FLIPS):
            px, py = _flip(x, y, ox, oy)
            blk = outs[k].at[2 * px + py, _half(rows, c)]
            _remote(blk, blk, ssem.at[k, idx], rsem.at[k, idx], (px, py, c)).wait_recv()
            cp = _remote(blk, blk, ssem.at[k, 3 + idx], rsem.at[k, 3 + idx], sib)
            cp.start()
            passed.append(cp)
    for k in range(len(ins)):
        rows = ins[k].shape[0]
        for idx, (ox, oy) in enumerate(FLIPS):
            px, py = _flip(x, y, ox, oy)
            blk = outs[k].at[2 * px + py, _half(rows, 1 - c)]
            _remote(blk, blk, ssem.at[k, 3 + idx], rsem.at[k, 3 + idx], sib).wait_recv()
    for cp in _ag_chip_copies(ins, outs, ssem, rsem) + passed:
        cp.wait_send()


def _ag_shapes(shards):
    nk = len(shards)
    return ([_sds((N_CHIPS,) + tuple(w.shape), w.dtype) for w in shards],
            [pltpu.SemaphoreType.DMA((nk, 6)), pltpu.SemaphoreType.DMA((nk, 6))])


def _with_own(gathered, shard, j):
    return lax.dynamic_update_index_in_dim(gathered, shard, j, 0)


def _rs_pair_copies(ins, outs, ssem, rsem):
    x, y, c = _place()
    return [_remote(ins[k].at[:, _half(ins[k].shape[1], 1 - c)], outs[k], ssem.at[k], rsem.at[k], (x, y, 1 - c))
            for k in range(len(ins))]


def _rs_pair_start(ins, outs, ssem, rsem):
    for cp in _rs_pair_copies(ins, outs, ssem, rsem):
        cp.start()


def _rs_pair_finish(ins, outs, ssem, rsem):
    for cp in _rs_pair_copies(ins, outs, ssem, rsem):
        cp.wait()


def _rs_pair_exchange(grads):
    nk = len(grads)
    return (grads, [_sds((N_CHIPS, g.shape[1] // 2, g.shape[2]), g.dtype) for g in grads],
            [pltpu.SemaphoreType.DMA((nk,)), pltpu.SemaphoreType.DMA((nk,))], _rs_pair_start, _rs_pair_finish)


def _rs_pair(name, grads):
    nk = len(grads)
    ins, out_shape, sems, start, finish = _rs_pair_exchange(grads)

    def body(*refs):
        start(refs[:nk], refs[nk:2 * nk], *refs[2 * nk:])
        finish(refs[:nk], refs[nk:2 * nk], *refs[2 * nk:])

    return pl.pallas_call(body, name=name, in_specs=[ANY] * nk, out_specs=[ANY] * nk, out_shape=out_shape,
                          scratch_shapes=sems)(*ins)


def _rs_chip_copies(ins, outs, ssem, rsem):
    x, y, c = _place()
    cps = []
    for k in range(len(ins)):
        for idx, (ox, oy) in enumerate(FLIPS):
            px, py = _flip(x, y, ox, oy)
            cps.append(_remote(ins[k].at[2 * px + py], outs[k].at[idx], ssem.at[k, idx], rsem.at[k, idx], (px, py, c)))
    return cps


def _rs_chips_start(ins, outs, ssem, rsem):
    for cp in _rs_chip_copies(ins, outs, ssem, rsem):
        cp.start()


def _rs_chips_finish(ins, outs, ssem, rsem):
    for cp in _rs_chip_copies(ins, outs, ssem, rsem):
        cp.wait()


def _rs_chips_shapes(psums):
    nk = len(psums)
    return ([_sds((3,) + tuple(p.shape[1:]), p.dtype) for p in psums],
            [pltpu.SemaphoreType.DMA((nk, 3)), pltpu.SemaphoreType.DMA((nk, 3))])


def _rs_share(fulls):
    nk = len(fulls)

    def body(*refs):
        ins, outs = refs[:nk], refs[nk:2 * nk]
        ssem, rsem = refs[2 * nk:]
        x, y, c = _place()
        cps = []
        for k in range(nk):
            rows = fulls[k].shape[0]
            cp = _remote(ins[k].at[_half(rows, c)], outs[k].at[_half(rows, c)], ssem.at[k], rsem.at[k], (x, y, 1 - c))
            cp.start()
            cps.append(cp)
        for k, cp in enumerate(cps):
            rows = fulls[k].shape[0]
            cp.wait_send()
            theirs = outs[k].at[_half(rows, 1 - c)]
            _remote(theirs, theirs, ssem.at[k], rsem.at[k], (x, y, 1 - c)).wait_recv()

    return pl.pallas_call(
        body, name="rs_share", in_specs=[ANY] * nk, out_specs=[ANY] * nk,
        out_shape=[_sds(f.shape, f.dtype) for f in fulls], input_output_aliases={k: k for k in range(nk)},
        scratch_shapes=[pltpu.SemaphoreType.DMA((nk,)), pltpu.SemaphoreType.DMA((nk,))],
    )(*fulls)


def _allreduce_small(v):
    ndev = 8

    def body(in_ref, out_ref, buf, ssem, rsem):
        x, y, c = _place()
        me = 4 * x + 2 * y + c
        buf[me] = in_ref[...]
        cps = []
        for k in range(1, ndev):
            ox, oy, oc = (k >> 2) & 1, (k >> 1) & 1, k & 1
            px, py = _flip(x, y, ox, oy)
            pc = 1 - c if oc else c
            cp = pltpu.make_async_remote_copy(src_ref=in_ref, dst_ref=buf.at[me], send_sem=ssem.at[k - 1],
                                              recv_sem=rsem.at[k - 1], device_id=(px, py, pc), device_id_type=MESH)
            cp.start()
            cps.append((cp, 4 * px + 2 * py + pc, (px, py, pc)))
        for k, (cp, src, peer) in enumerate(cps):
            cp.wait_send()
            pltpu.make_async_remote_copy(src_ref=in_ref, dst_ref=buf.at[src], send_sem=ssem.at[k],
                                         recv_sem=rsem.at[k], device_id=peer, device_id_type=MESH).wait_recv()
        acc = buf[0]
        for i in range(1, ndev):
            acc = acc + buf[i]
        out_ref[...] = acc

    return pl.pallas_call(
        body, name="allreduce_small",
        in_specs=[pl.BlockSpec(memory_space=pltpu.VMEM)], out_specs=pl.BlockSpec(memory_space=pltpu.VMEM),
        out_shape=_sds(v.shape, v.dtype),
        scratch_shapes=[pltpu.VMEM((ndev,) + v.shape, v.dtype), pltpu.SemaphoreType.DMA((ndev - 1,)),
                        pltpu.SemaphoreType.DMA((ndev - 1,))],
    )(v)


def _rs_sum1(name, g, recv, c_idx):
    _, r, cdim = g.shape
    hr = r // 2
    tr = min(hr, 256)
    nr = hr // tr

    def body(c_ref, g_ref, r_ref, o32_ref, o16_ref):
        v = g_ref[...] + r_ref[...].astype(F32)
        o32_ref[...] = v
        o16_ref[...] = v.astype(BF16)

    spec = pl.BlockSpec((None, tr, cdim), lambda j, i, c_ref: (j, i, 0))
    return pl.pallas_call(
        body, name=name,
        grid_spec=pltpu.PrefetchScalarGridSpec(
            num_scalar_prefetch=1, grid=(N_CHIPS, nr),
            in_specs=[pl.BlockSpec((None, tr, cdim), lambda j, i, c_ref: (j, c_ref[0] * nr + i, 0)), spec],
            out_specs=[spec, spec]),
        out_shape=[_sds((N_CHIPS, hr, cdim), F32), _sds((N_CHIPS, hr, cdim), BF16)],
        compiler_params=_cp(("parallel", "parallel")),
    )(c_idx, g, recv)


def _rs_sum2(name, p32, recv, jc_idx):
    _, hr, cdim = p32.shape
    tr = min(hr, 256)
    nr = hr // tr

    def body(jc_ref, p_ref, r_ref, o_ref):
        o_ref[...] = ((p_ref[...] + r_ref[0].astype(F32)) + r_ref[1].astype(F32)) + r_ref[2].astype(F32)

    return pl.pallas_call(
        body, name=name,
        grid_spec=pltpu.PrefetchScalarGridSpec(
            num_scalar_prefetch=1, grid=(nr,),
            in_specs=[pl.BlockSpec((None, tr, cdim), lambda i, jc: (jc[0], i, 0)),
                      pl.BlockSpec((3, tr, cdim), lambda i, jc: (0, i, 0))],
            out_specs=pl.BlockSpec((tr, cdim), lambda i, jc: (jc[1] * nr + i, 0))),
        out_shape=_sds((2 * hr, cdim), F32),
        compiler_params=_cp(("parallel",)),
    )(jc_idx, p32, recv)


def _adamw(name, w, g, m, v):
    r, cdim = w.shape
    tr = min(r, 256)
    c1 = 1.0 - ADAM_B1 ** ADAM_STEP
    c2 = 1.0 - ADAM_B2 ** ADAM_STEP

    def body(w_ref, g_ref, m_ref, v_ref, d_ref, nm_ref, nv_ref):
        gv = g_ref[...]
        nm = ADAM_B1 * m_ref[...] + (1.0 - ADAM_B1) * gv
        nv = ADAM_B2 * v_ref[...] + (1.0 - ADAM_B2) * (gv * gv)
        d_ref[...] = -ADAM_LR * ((nm / c1) / (jnp.sqrt(nv / c2) + ADAM_EPS) + ADAM_WD * w_ref[...])
        nm_ref[...] = nm
        nv_ref[...] = nv

    spec = pl.BlockSpec((tr, cdim), lambda i: (i, 0))
    return pl.pallas_call(
        body, name=name, grid=(r // tr,), in_specs=[spec] * 4, out_specs=[spec] * 3,
        out_shape=[_sds((r, cdim), F32)] * 3, compiler_params=_cp(("parallel",)),
    )(w, g, m, v)


def _pack_small(mix_pre, attn_out, lb_logits, hgrn_out, mix_post, mlp_pre, mlp_post, extra=None):
    spare = jnp.zeros((1, D_MODEL), F32)
    rows = [mix_pre, jnp.concatenate([attn_out, hgrn_out], axis=1),
            jnp.concatenate([lb_logits[0:1], lb_logits[1:2]], axis=1), mix_post, mlp_pre, mlp_post,
            spare if extra is None else extra, spare]
    return jnp.concatenate(rows, axis=0)


def _unpack_small(p):
    return (p[0:1], p[1:2, :ATTN_W], jnp.concatenate([p[2:3, :HGRN_W], p[2:3, HGRN_W:]], axis=0),
            p[1:2, ATTN_W:], p[3:4], p[4:5], p[5:6])


def kernel(x, mix_pre_norm, w_in, attn_out_norm, hgrn_lb_logits, hgrn_out_norm, w_out, mix_post_norm, mlp_pre_norm, w_ff1, w_ff2, mlp_post_norm, loss_target, m_mix_pre_norm, m_w_in, m_attn_out_norm, m_hgrn_lb_logits, m_hgrn_out_norm, m_w_out, m_mix_post_norm, m_mlp_pre_norm, m_w_ff1, m_w_ff2, m_mlp_post_norm, v_mix_pre_norm, v_w_in, v_attn_out_norm, v_hgrn_lb_logits, v_hgrn_out_norm, v_w_out, v_mix_post_norm, v_mlp_pre_norm, v_w_ff1, v_w_ff2, v_mlp_post_norm):
    s = x.shape[1]
    xs = x.reshape(s, D_MODEL)
    tgt = loss_target.reshape(s, D_MODEL)
    cx, cy, cc = _place()
    chip = 2 * cx + cy
    c_idx = jnp.reshape(cc, (1,)).astype(jnp.int32)
    jc_idx = jnp.stack([chip, cc]).astype(jnp.int32)

    big_w = [w_in[0], w_out[0], w_ff1[0], w_ff2[0]]
    big_m = [m_w_in[0], m_w_out[0], m_w_ff1[0], m_w_ff2[0]]
    big_v = [v_w_in[0], v_w_out[0], v_w_ff1[0], v_w_ff2[0]]
    shards = [w.astype(BF16) for w in big_w]

    h, wg_in = _rows_call("norm_in", lambda xv, g: ((xv * _rstd(xv) * g),),
                          [(xs, _row(D_MODEL)), (mix_pre_norm, "full")], [(D_MODEL, BF16, "row")], s,
                          exchange=(shards[:1], *_ag_shapes(shards[:1]), _ag_start, _ag_finish))
    wg_in = _with_own(wg_in, shards[0], chip)
    (proj,) = _mm_cols("mm_proj", h, wg_in, NN, [F32])
    hg_o, rec, states = _hgrn_fwd(proj, hgrn_lb_logits, hgrn_out_norm)
    attn_o, attn_lse, wg_out, wg_1, wg_2 = _attn_fwd(proj, shards[1:])
    wg_out, wg_1, wg_2 = (_with_own(g, w, chip) for g, w in zip((wg_out, wg_1, wg_2), shards[1:]))
    (attn_n,) = _rows_call("attn_norm", lambda o, gain: (o * _rstd(o) * gain,),
                           [(attn_o, _row(ATTN_W)), (attn_out_norm, "full")], [(ATTN_W, BF16, "row")], s)
    cat = jnp.concatenate([attn_n, rec], axis=1)

    def post1(mv, xv, g_post, g_pre2):
        x1 = xv + mv * _rstd(mv) * g_post
        return mv, x1, x1 * _rstd(x1) * g_pre2

    mixed, x1, h2 = _rows_call(
        "mm_mixed", post1, [(xs, _row(D_MODEL)), (mix_post_norm, "full"), (mlp_pre_norm, "full")],
        [(D_MODEL, F32, "row"), (D_MODEL, F32, "row"), (D_MODEL, BF16, "row")], s,
        matmul=(cat, wg_out.reshape(D_MODEL, D_MODEL)))

    def sq_relu(u):
        r = jnp.maximum(u, 0.0)
        return (r * r,)

    (act,) = _mm_cols("mm_ff1", h2, wg_1, NN, [BF16], epi=sq_relu)

    def post2(fv, x1v, tv, g):
        y = x1v + fv * _rstd(fv) * g
        dy = (y - tv) * (1.0 / D_MODEL)
        err = y - tv
        loss = 0.5 * jnp.sum(jnp.mean(err * err, axis=-1, keepdims=True), axis=0, keepdims=True)
        dff, dgc = _norm_bwd(fv, g, dy)
        return dy, dff, _colsum(dgc), jnp.broadcast_to(loss, (1, BLK))

    dy, dff, g_mlp_post, loss_part = _rows_call(
        "mm_ff2", post2, [(x1, _row(D_MODEL)), (tgt, _row(D_MODEL)), (mlp_post_norm, "full")],
        [(D_MODEL, F32, "row"), (D_MODEL, BF16, "row"), (D_MODEL, F32, "acc"), (BLK, F32, "acc")], s,
        matmul=(act, wg_2.reshape(D_FF, D_MODEL)))

    (du,) = _mm_cols("mm_du", dff, wg_2, NT, [BF16], epi=lambda acc, a: (acc * (2.0 * jnp.sqrt(a.astype(F32))),),
                     extras=(act,))
    gw_2 = _mm_wgrad("mm_gw2", act, dff, True)
    gw_1 = _mm_wgrad("mm_gw1", h2, du, False)

    def bwd_mid(dh2v, dyv, x1v, mv, g_pre2, g_post):
        d1, gc1 = _norm_bwd(x1v, g_pre2, dh2v)
        dx1 = dyv + d1
        dm, gc2 = _norm_bwd(mv, g_post, dx1)
        return dx1, dm, _colsum(gc1), _colsum(gc2)

    dx1, dmixed, g_mlp_pre, g_mix_post, *from_pair = _rows_call(
        "mm_dh2", bwd_mid, [(dy, _row(D_MODEL)), (x1, _row(D_MODEL)), (mixed, _row(D_MODEL)),
                            (mlp_pre_norm, "full"), (mix_post_norm, "full")],
        [(D_MODEL, F32, "row"), (D_MODEL, BF16, "row"), (D_MODEL, F32, "acc"), (D_MODEL, F32, "acc")], s,
        matmul=(du, wg_1), exchange=_rs_pair_exchange([gw_1[1], gw_2[1]]))

    (dcat,) = _mm_cols("mm_dcat", dmixed, wg_out, NT, [F32])
    gw_out = _mm_wgrad("mm_gwout", cat, dmixed, True)
    names = ["out", "ff1", "ff2", "in"]
    ready = [gw_out, gw_1, gw_2]
    from_pair = list(_rs_pair("rs_pair_out", [gw_out[1]])) + from_pair
    pair = [_rs_sum1(f"rs_sum1_{n}", g[0], r, c_idx) for n, g, r in zip(names, ready, from_pair)]

    def attn_norm_bwd(dc, o, gain):
        do, gc = _norm_bwd(o, gain, dc)
        t = do * o
        lane = lax.broadcasted_iota(jnp.int32, (t.shape[0], BLK), 1) < 64
        parts = []
        for p in range(ATTN_W // BLK):
            tp = t[:, p * BLK:(p + 1) * BLK]
            sa = jnp.sum(jnp.where(lane, tp, 0.0), axis=1, keepdims=True)
            sb = jnp.sum(jnp.where(lane, 0.0, tp), axis=1, keepdims=True)
            parts.append(jnp.where(lane, sa, sb))
        return do, jnp.concatenate(parts, axis=1), _colsum(gc)

    do_attn, delta, g_attn_out = _rows_call(
        "attn_norm_bwd", attn_norm_bwd, [(dcat, _row(ATTN_W, 0)), (attn_o, _row(ATTN_W)), (attn_out_norm, "full")],
        [(ATTN_W, F32, "row"), (ATTN_W, F32, "row"), (ATTN_W, F32, "acc")], s)
    dq, dk, dv, *from_chips = _attn_bwd(proj, do_attn, attn_lse, delta, [p[1] for p in pair])
    dhq, dhf, dhi, dhg, g_hgrn_out, g_lb = _hgrn_bwd(proj, hg_o, states, dcat, hgrn_lb_logits, hgrn_out_norm)

    def dproj_asm(*a):
        return (jnp.concatenate([t.astype(BF16) for t in a], axis=1),)

    (dproj,) = _rows_call("dproj_asm", dproj_asm,
                          [(t, _row(ATTN_W)) for t in (dq, dk, dv)] + [(t, _row(HGRN_W)) for t in (dhq, dhf, dhi, dhg)],
                          [(PROJ_W, BF16, "row")], s)
    gw_in = _mm_wgrad("mm_gwin", h, dproj, False)
    (from_pair_in,) = _rs_pair("rs_pair_in", [gw_in[1]])
    pair.append(_rs_sum1("rs_sum1_in", gw_in[0], from_pair_in, c_idx))
    rs_shape, rs_sems = _rs_chips_shapes([pair[3][1]])

    def bwd_in(dhv, dx1v, xv, g):
        d0, gc = _norm_bwd(xv, g, dhv)
        return dx1v + d0, _colsum(gc)

    grad_x, g_mix_pre, from_chips_in = _rows_call(
        "mm_dh", bwd_in, [(dx1, _row(D_MODEL)), (xs, _row(D_MODEL)), (mix_pre_norm, "full")],
        [(D_MODEL, F32, "row"), (D_MODEL, F32, "acc")], s, matmul=(dproj, wg_in),
        exchange=([pair[3][1]], rs_shape, rs_sems, _rs_chips_start, _rs_chips_finish))
    from_chips.append(from_chips_in)

    loss_row = jnp.pad(loss_part, ((0, 0), (0, D_MODEL - BLK)))
    small_g = _allreduce_small(_pack_small(g_mix_pre, g_attn_out, g_lb, g_hgrn_out, g_mix_post, g_mlp_pre, g_mlp_post,
                                           extra=loss_row))
    loss = small_g[6, 0]

    reduced = [_rs_sum2(f"rs_sum2_{n}", p[0], r, jc_idx) for n, p, r in zip(names, pair, from_chips)]
    g_wout, g_w1, g_w2, g_win = _rs_share(reduced)
    full = [g_win, g_wout, g_w1, g_w2]

    upd = [_adamw(f"adamw_{n}", w, g, m, v) for n, w, g, m, v in zip(("in", "out", "ff1", "ff2"), big_w, full, big_m, big_v)]
    small_w = _pack_small(mix_pre_norm, attn_out_norm, hgrn_lb_logits, hgrn_out_norm, mix_post_norm, mlp_pre_norm,
                          mlp_post_norm)
    small_m = _pack_small(m_mix_pre_norm, m_attn_out_norm, m_hgrn_lb_logits, m_hgrn_out_norm, m_mix_post_norm,
                          m_mlp_pre_norm, m_mlp_post_norm)
    small_v = _pack_small(v_mix_pre_norm, v_attn_out_norm, v_hgrn_lb_logits, v_hgrn_out_norm, v_mix_post_norm,
                          v_mlp_pre_norm, v_mlp_post_norm)
    small_upd = _adamw("adamw_small", small_w, small_g, small_m, small_v)

    def assemble(small, big):
        sm = _unpack_small(small)
        return (sm[0], big[0][None], sm[1], sm[2], sm[3], big[1][None], sm[4], sm[5], big[2][None], big[3][None], sm[6])

    g_out = assemble(small_g, full)
    d_out = assemble(small_upd[0], [u[0] for u in upd])
    m_out = assemble(small_upd[1], [u[1] for u in upd])
    v_out = assemble(small_upd[2], [u[2] for u in upd])
    return (loss, grad_x.reshape(x.shape), *g_out, *d_out, *m_out, *v_out)
```

```python
import numpy as np
import jax
import jax.numpy as jnp
from jax import lax
from jax.experimental import pallas as pl
from jax.experimental.pallas import tpu as pltpu

F32 = jnp.float32
BF16 = jnp.bfloat16
MESH = pl.DeviceIdType.MESH
ANY = pl.BlockSpec(memory_space=pl.ANY)

RMS_EPS = 1e-6
D_MODEL = 1024
ATTN_W = 512
HGRN_W = 512
PROJ_W = 3584
D_FF = 4096
N_CHIPS = 4
BLK = 128
CHUNK = 64
HGRN_TB = 512
ATTN_GROUP = 8
DILATIONS = (1, 4, 16)
ATTN_SCALE = 0.125
ROW_TILE = 512
MM_TILE = 1024
VMEM_LIMIT = 48 * 2 ** 20
FLIPS = ((1, 0), (0, 1), (1, 1))

ADAM_LR, ADAM_B1, ADAM_B2, ADAM_EPS, ADAM_WD, ADAM_STEP = 0.001, 0.9, 0.999, 1e-08, 0.01, 10


def _cp(sem=None):
    return pltpu.CompilerParams(dimension_semantics=sem, vmem_limit_bytes=VMEM_LIMIT)


def _sigmoid(v):
    return 1.0 / (1.0 + jnp.exp(-v))


def _dot(a, b, contract, precision=None):
    return lax.dot_general(a, b, (contract, ((), ())), preferred_element_type=F32, precision=precision)


NN = ((1,), (0,))
NT = ((1,), (1,))
TN = ((0,), (0,))


def _sds(shape, dtype):
    return jax.ShapeDtypeStruct(shape, dtype)


def _resident(shape):
    return pl.BlockSpec(shape, lambda *_: (0,) * len(shape), pipeline_mode=pl.Buffered(1))


def _mm_cols(name, a, w, contract, out_dtypes, epi=None, extras=()):
    m, k = a.shape
    jn = w.shape[0]
    nj = w.shape[2] if contract == NN else w.shape[1]
    tm = min(m, MM_TILE)
    n_ex, parts = len(extras), 2

    def body(a_ref, w_ref, *rest):
        ex, out_refs = rest[:n_ex], rest[n_ex:]
        i = pl.program_id(1)
        part = tm // parts
        for h in range(parts):
            rows = slice(h * part, (h + 1) * part)
            acc = _dot(a_ref[pl.ds(pl.multiple_of(i * tm + h * part, part), part), :], w_ref[...], contract)
            res = epi(acc, *[e[rows, :] for e in ex]) if epi else (acc,)
            for o, r in zip(out_refs, res):
                o[rows, :] = r.astype(o.dtype)

    blk = pl.BlockSpec((tm, nj), lambda j, i: (i, j))
    return pl.pallas_call(
        body, name=name, grid=(jn, m // tm),
        in_specs=[_resident((m, k)), pl.BlockSpec((None,) + w.shape[1:], lambda j, i: (j, 0, 0))] + [blk] * n_ex,
        out_specs=[blk] * len(out_dtypes), out_shape=[_sds((m, jn * nj), dt) for dt in out_dtypes],
        compiler_params=_cp(("parallel", "parallel")),
    )(a, w, *extras)


def _mm_wgrad(name, a, b, a_by_j):
    s = a.shape[0]
    if a_by_j:
        r, c = a.shape[1] // N_CHIPS, b.shape[1]
        in_specs = [pl.BlockSpec((s, r), lambda j: (0, j)), _resident((s, c))]
    else:
        r, c = a.shape[1], b.shape[1] // N_CHIPS
        in_specs = [_resident((s, r)), pl.BlockSpec((s, c), lambda j: (0, j))]
    tr = min(r, 512)

    def body(a_ref, b_ref, o32_ref, o16_ref):
        for h in range(r // tr):
            cols = slice(h * tr, (h + 1) * tr)
            acc = _dot(a_ref[:, cols], b_ref[...], TN)
            o32_ref[cols, :] = acc
            o16_ref[cols, :] = acc.astype(BF16)

    out = pl.BlockSpec((None, r, c), lambda j: (j, 0, 0))
    return pl.pallas_call(
        body, name=name, grid=(N_CHIPS,), in_specs=in_specs, out_specs=[out, out],
        out_shape=[_sds((N_CHIPS, r, c), F32), _sds((N_CHIPS, r, c), BF16)], compiler_params=_cp(("parallel",)),
    )(a, b)


def _rows_call(name, fn, ins, outs, s, tm=ROW_TILE, matmul=None, exchange=None):
    in_specs = []
    if matmul:
        a, w = matmul
        in_specs += [pl.BlockSpec((tm, a.shape[1]), lambda i: (i, 0)), _resident(w.shape)]
    for arr, kind in ins:
        if kind == "full":
            in_specs.append(pl.BlockSpec(arr.shape, lambda i: (0, 0)))
        else:
            _, w_, cb = kind
            in_specs.append(pl.BlockSpec((tm, w_), lambda i, cb=cb: (i, cb)))
    out_specs, out_shape, is_acc = [], [], []
    for w_, dt, kind in outs:
        if kind == "acc":
            out_specs.append(pl.BlockSpec((1, w_), lambda i: (0, 0)))
            out_shape.append(_sds((1, w_), dt))
        else:
            out_specs.append(pl.BlockSpec((tm, w_), lambda i: (i, 0)))
            out_shape.append(_sds((s, w_), dt))
        is_acc.append(kind == "acc")
    n_mm, n_in, n_out = (2 if matmul else 0), len(ins), len(outs)
    x_ins, x_shapes, x_sems, x_start, x_finish = exchange if exchange else ((), [], [], None, None)
    n_x = len(x_ins)
    steps = s // tm

    def body(*refs):
        in_refs, xi = refs[n_mm:n_mm + n_in], refs[n_mm + n_in:n_mm + n_in + n_x]
        out_refs = refs[n_mm + n_in + n_x:n_mm + n_in + n_x + n_out]
        xo, sems = refs[n_mm + n_in + n_x + n_out:n_mm + n_in + 2 * n_x + n_out], refs[n_mm + n_in + 2 * n_x + n_out:]
        i = pl.program_id(0)

        if exchange:
            @pl.when(i == 0)
            def _():
                x_start(xi, xo, *sems)

        for o, acc in zip(out_refs, is_acc):
            if acc:
                @pl.when(i == 0)
                def _(o=o):
                    o[...] = jnp.zeros_like(o)

        parts = 2 if matmul else 1
        for h in range(parts):
            rows = slice(h * (tm // parts), (h + 1) * (tm // parts))
            args = [r[...] if kind == "full" else r[rows, :] for r, (_, kind) in zip(in_refs, ins)]
            if matmul:
                a_ref, w_ref = refs[:2]
                if len(w_ref.shape) == 2:
                    acc = _dot(a_ref[rows, :], w_ref[...], NN)
                else:
                    kj = w_ref.shape[2]
                    acc = _dot(a_ref[rows, 0:kj], w_ref[0], NT)
                    for j in range(1, w_ref.shape[0]):
                        acc = acc + _dot(a_ref[rows, j * kj:(j + 1) * kj], w_ref[j], NT)
                args.insert(0, acc)
            for o, r, acc in zip(out_refs, fn(*args), is_acc):
                if acc:
                    o[...] += r.astype(o.dtype)
                else:
                    o[rows, :] = r.astype(o.dtype)

        if exchange:
            @pl.when(i == steps - 1)
            def _():
                x_finish(xi, xo, *sems)

    sem = ("arbitrary",) if any(is_acc) or exchange else ("parallel",)
    return pl.pallas_call(
        body, name=name, grid=(steps,), in_specs=in_specs + [ANY] * n_x, out_specs=out_specs + [ANY] * n_x,
        out_shape=out_shape + list(x_shapes), scratch_shapes=list(x_sems), compiler_params=_cp(sem),
    )(*(matmul or ()), *[a for a, _ in ins], *x_ins)


def _rstd(v):
    return lax.rsqrt(jnp.mean(v * v, axis=-1, keepdims=True) + RMS_EPS)


def _norm_bwd(v, gain, dy):
    r = _rstd(v)
    n = v * r
    dn = dy * gain
    dv = r * (dn - n * jnp.mean(dn * n, axis=-1, keepdims=True))
    return dv, dy * n


def _colsum(v):
    return jnp.sum(v, axis=0, keepdims=True)


def _row(w, cb=0):
    return ("row", w, cb)


N_PAIRS = ATTN_W // BLK


def _head_col(v, mask):
    return jnp.max(jnp.where(mask, v, -jnp.inf), axis=1, keepdims=True)


def _slopes():
    t = np.zeros((N_PAIRS, 8, 2 * BLK), np.float32)
    for p in range(N_PAIRS):
        for hh in range(2):
            t[p, hh, :] = 2.0 ** -(2 * p + hh + 1)
    return jnp.asarray(t)


def _rows(n, r, d):
    base = pl.multiple_of(n * (BLK * d), BLK)
    return pl.ds(base + r, BLK, stride=d) if d > 1 else pl.ds(base, BLK)


def _attn_bias(sl_ref, bias_scr):
    row = lax.broadcasted_iota(jnp.int32, (BLK, 2 * BLK), 0)
    col = lax.broadcasted_iota(jnp.int32, (BLK, 2 * BLK), 1)
    dist = row + BLK - col
    in_window = (dist >= 0) & (dist <= BLK)
    distf = dist.astype(F32)
    for di, d in enumerate(DILATIONS):
        for hh in range(2):
            bias_scr[2 * di + hh] = jnp.where(in_window, -(sl_ref[hh:hh + 1, :] * float(d)) * distf, -1e30)


def _first_block_penalty(n):
    col = lax.broadcasted_iota(jnp.int32, (1, 2 * BLK), 1)
    return jnp.where(col + n * BLK >= BLK, 0.0, -1e30)


def _attn_groups(s, d):
    nb = s // (BLK * d)
    g = ATTN_GROUP
    if d >= g:
        return [(nb, lambda n, r0=r0: [(n, r0 + u) for u in range(g)]) for r0 in range(0, d, g)]
    per = g // d
    return [(nb // per, lambda t: [(per * t + u, r) for u in range(per) for r in range(d)])]


def _attn_fwd(proj, shards):
    s = proj.shape[0]
    nk = len(shards)

    def body(sl_ref, q_ref, k_ref, v_ref, *rest):
        w_refs, (o_ref, l_ref) = rest[:nk], rest[nk:nk + 2]
        wg_refs, (bias_scr, ssem, rsem) = rest[nk + 2:2 * nk + 2], rest[2 * nk + 2:]
        pair = pl.program_id(0)

        @pl.when(pair == 0)
        def _():
            _ag_start(w_refs, wg_refs, ssem, rsem)

        _attn_bias(sl_ref, bias_scr)
        lane_q = lax.broadcasted_iota(jnp.int32, (BLK, BLK), 1) < 64
        lane_k = lax.broadcasted_iota(jnp.int32, (2 * BLK, BLK), 1) < 64

        def branch(n, r, di):
            d = DILATIONS[di]
            rows = _rows(n, r, d)
            prev = _rows(jnp.maximum(n - 1, 0), r, d)
            pen = _first_block_penalty(n)
            q2 = q_ref[rows, :] * ATTN_SCALE
            kk = jnp.concatenate([k_ref[prev, :], k_ref[rows, :]], axis=0).astype(BF16)
            vv = jnp.concatenate([v_ref[prev, :], v_ref[rows, :]], axis=0)
            o2 = jnp.zeros((BLK, BLK), F32)
            lse2 = jnp.zeros((BLK, BLK), F32)
            for hh in range(2):
                mq = lane_q if hh == 0 else ~lane_q
                mk = lane_k if hh == 0 else ~lane_k
                qm = jnp.where(mq, q2, 0.0).astype(BF16)
                sc = _dot(qm, kk, NT) + bias_scr[2 * di + hh] + pen
                m = jnp.max(sc, axis=1, keepdims=True)
                pr = jnp.exp(sc - m)
                den = jnp.sum(pr, axis=1, keepdims=True)
                vm = jnp.where(mk, vv, 0.0).astype(BF16)
                o2 = o2 + _dot(pr.astype(BF16), vm, NN) / den
                lse2 = jnp.where(mq, m + jnp.log(den), lse2)
            return rows, o2, lse2

        def merge(rows, o2, lse2, first):
            if first:
                o_ref[rows, :] = o2
                l_ref[rows, :] = lse2
            else:
                lo = l_ref[rows, :]
                mx = jnp.maximum(lo, lse2)
                ln = mx + jnp.log(jnp.exp(lo - mx) + jnp.exp(lse2 - mx))
                o_ref[rows, :] = jnp.exp(lo - ln) * o_ref[rows, :] + jnp.exp(lse2 - ln) * o2
                l_ref[rows, :] = ln

        for di, d in enumerate(DILATIONS):
            for trips, blocks in _attn_groups(s, d):
                def trip(t, carry, di=di, blocks=blocks):
                    done = [branch(n, r, di) for n, r in blocks(t)]
                    for rows, o2, lse2 in done:
                        merge(rows, o2, lse2, di == 0)
                    return carry

                lax.fori_loop(0, trips, trip, 0)

        @pl.when(pair == N_PAIRS - 1)
        def _():
            _ag_finish(w_refs, wg_refs, ssem, rsem)

    cb = lambda base: pl.BlockSpec((s, BLK), lambda p, base=base: (0, base + p))
    out = pl.BlockSpec((s, BLK), lambda p: (0, p))
    ag_shape, ag_sems = _ag_shapes(shards)
    return pl.pallas_call(
        body, name="attn_fwd", grid=(N_PAIRS,),
        in_specs=[pl.BlockSpec((None, 8, 2 * BLK), lambda p: (p, 0, 0)), cb(0), cb(N_PAIRS), cb(2 * N_PAIRS)]
        + [ANY] * nk,
        out_specs=[out, out] + [ANY] * nk, out_shape=[_sds((s, ATTN_W), F32)] * 2 + ag_shape,
        scratch_shapes=[pltpu.VMEM((2 * len(DILATIONS), BLK, 2 * BLK), F32)] + ag_sems,
        compiler_params=_cp(("arbitrary",)),
    )(_slopes(), proj, proj, proj, *shards)


def _attn_bwd(proj, do, lse, delta, psums):
    s = proj.shape[0]
    nk = len(psums)

    def body(sl_ref, q_ref, k_ref, v_ref, do_ref, l_ref, e_ref, *rest):
        p_refs, out16 = rest[:nk], rest[nk:nk + 3]
        got_refs, (dq_ref, dk_ref, dv_ref, bias_scr, ssem, rsem) = rest[nk + 3:2 * nk + 3], rest[2 * nk + 3:]
        pair = pl.program_id(0)

        @pl.when(pair == 0)
        def _():
            _rs_chips_start(p_refs, got_refs, ssem, rsem)

        _attn_bias(sl_ref, bias_scr)
        lane_q = lax.broadcasted_iota(jnp.int32, (BLK, BLK), 1) < 64
        lane_k = lax.broadcasted_iota(jnp.int32, (2 * BLK, BLK), 1) < 64
        dk_ref[...] = jnp.zeros_like(dk_ref)
        dv_ref[...] = jnp.zeros_like(dv_ref)

        def branch(n, r, di):
            d = DILATIONS[di]
            rows = _rows(n, r, d)
            prev = _rows(jnp.maximum(n - 1, 0), r, d)
            pen = _first_block_penalty(n)
            q1, d1, l1, e1 = q_ref[rows, :] * ATTN_SCALE, do_ref[rows, :], l_ref[rows, :], e_ref[rows, :]
            kk = jnp.concatenate([k_ref[prev, :], k_ref[rows, :]], axis=0)
            kkb = kk.astype(BF16)
            vvb = jnp.concatenate([v_ref[prev, :], v_ref[rows, :]], axis=0).astype(BF16)
            dq2 = jnp.zeros((BLK, BLK), F32)
            dkk = jnp.zeros((2 * BLK, BLK), F32)
            dvv = jnp.zeros((2 * BLK, BLK), F32)
            for hh in range(2):
                mq = lane_q if hh == 0 else ~lane_q
                mk = lane_k if hh == 0 else ~lane_k
                qm = jnp.where(mq, q1, 0.0).astype(BF16)
                dm = jnp.where(mq, d1, 0.0).astype(BF16)
                sc = _dot(qm, kkb, NT) + bias_scr[2 * di + hh] + pen
                pr = jnp.exp(sc - _head_col(l1, mq))
                ds = (pr * (_dot(dm, vvb, NT) - _head_col(e1, mq))).astype(BF16)
                km = jnp.where(mk, kk, 0.0).astype(BF16)
                dq2 = dq2 + _dot(ds, km, NN)
                dkk = dkk + _dot(ds, qm, TN)
                dvv = dvv + _dot(pr.astype(BF16), dm, TN)
            return rows, prev, dq2 * ATTN_SCALE, dkk, dvv

        for di, d in enumerate(DILATIONS):
            for trips, blocks in _attn_groups(s, d):
                def trip(t, carry, di=di, blocks=blocks, first=(di == 0)):
                    done = [branch(n, r, di) for n, r in blocks(t)]
                    for rows, prev, dq2, dkk, dvv in done:
                        dq_ref[rows, :] = dq2 if first else dq_ref[rows, :] + dq2
                        dk_ref[prev, :] = dk_ref[prev, :] + dkk[:BLK]
                        dk_ref[rows, :] = dk_ref[rows, :] + dkk[BLK:]
                        dv_ref[prev, :] = dv_ref[prev, :] + dvv[:BLK]
                        dv_ref[rows, :] = dv_ref[rows, :] + dvv[BLK:]
                    return carry

                lax.fori_loop(0, trips, trip, 0)

        for o16, acc in zip(out16, (dq_ref, dk_ref, dv_ref)):
            o16[...] = acc[...].astype(BF16)

        @pl.when(pair == N_PAIRS - 1)
        def _():
            _rs_chips_finish(p_refs, got_refs, ssem, rsem)

    cb = lambda base: pl.BlockSpec((s, BLK), lambda p, base=base: (0, base + p))
    out = pl.BlockSpec((s, BLK), lambda p: (0, p))
    rs_shape, rs_sems = _rs_chips_shapes(psums)
    return pl.pallas_call(
        body, name="attn_bwd", grid=(N_PAIRS,),
        in_specs=[pl.BlockSpec((None, 8, 2 * BLK), lambda p: (p, 0, 0)), cb(0), cb(N_PAIRS), cb(2 * N_PAIRS),
                  out, out, out] + [ANY] * nk,
        out_specs=[out] * 3 + [ANY] * nk, out_shape=[_sds((s, ATTN_W), BF16)] * 3 + rs_shape,
        scratch_shapes=[pltpu.VMEM((s, BLK), F32)] * 3 + [pltpu.VMEM((2 * len(DILATIONS), BLK, 2 * BLK), F32)] + rs_sems,
        compiler_params=_cp(("arbitrary",)),
    )(_slopes(), proj, proj, proj, do, lse, delta, *psums)


def _lower_bound(lbl):
    return 1.0 / (1.0 + jnp.exp(lbl[1:2, :] - lbl[0:1, :]))


def _hi(a):
    bits = lax.bitcast_convert_type(a, jnp.uint32) & jnp.uint32(0xFFFF0000)
    return lax.bitcast_convert_type(bits, F32)


def _dot3(a, b, contract):
    ah, bh = _hi(a), _hi(b)
    al, bl = (a - ah).astype(BF16), (b - bh).astype(BF16)
    ah, bh = ah.astype(BF16), bh.astype(BF16)
    return _dot(ah, bh, contract) + (_dot(ah, bl, contract) + _dot(al, bh, contract))


def _cumsum_rows(tri, g):
    g1 = _hi(g)
    r1 = g - g1
    g2 = _hi(r1)
    g3 = r1 - g2
    return _dot(tri, g1.astype(BF16), NN) + (_dot(tri, g2.astype(BF16), NN) + _dot(tri, g3.astype(BF16), NN))


def _heads(fn):
    return jnp.concatenate([fn(slice(h * BLK, (h + 1) * BLK)) for h in range(HGRN_W // BLK)], axis=1)


def _head_mean(t):
    return _heads(lambda hs: jnp.broadcast_to(jnp.mean(t[:, hs], axis=1, keepdims=True), (t.shape[0], BLK)))


def _hgrn_chunk(q_ref, f_ref, i_ref, sl, lb, tri):
    qp = q_ref[sl, :]
    sq = _sigmoid(qp)
    qf = qp * sq
    sg = _sigmoid(f_ref[sl, :])
    f = lb + (1.0 - lb) * sg
    kf = 1.0 - f
    v = i_ref[sl, :]
    b = _cumsum_rows(tri, jnp.log(f))
    bm = b[CHUNK // 2:CHUNK // 2 + 1, :]
    bl = b[CHUNK - 1:CHUNK, :]
    qt = qf * jnp.exp(b - bm)
    kt = kf * jnp.exp(bm - b)
    return qp, sq, qf, sg, f, kf, v, b, bm, bl, qt, kt


def _hgrn_specs(tb, block):
    first = 3 * ATTN_W // HGRN_W
    return [pl.BlockSpec((tb, HGRN_W), lambda i, k=k: (block(i), first + k)) for k in range(4)]


def _hgrn_fwd(proj, lb_logits, out_gain):
    s = proj.shape[0]
    tb = min(HGRN_TB, s)
    nb, cpb, nc = s // tb, tb // CHUNK, s // CHUNK

    def body(q_ref, f_ref, i_ref, g_ref, lbl_ref, gain_ref, o_ref, rec_ref, st_ref, st_scr):
        step = pl.program_id(0)

        @pl.when(step == 0)
        def _():
            st_scr[...] = jnp.zeros_like(st_scr)

        lb = _lower_bound(lbl_ref[...])
        r64 = lax.broadcasted_iota(jnp.int32, (CHUNK, CHUNK), 0)
        c64 = lax.broadcasted_iota(jnp.int32, (CHUNK, CHUNK), 1)
        tril = r64 >= c64
        tri = tril.astype(BF16)
        st = st_scr[...]
        for cc in range(cpb):
            sl = slice(cc * CHUNK, (cc + 1) * CHUNK)
            _, _, qf, _, _, kf, v, b, _, bl, qt, kt = _hgrn_chunk(q_ref, f_ref, i_ref, sl, lb, tri)
            qe = (qf * jnp.exp(b)).astype(BF16)
            kh = (kf * jnp.exp(bl - b)).astype(BF16)
            qtb, ktb, vb, stb = qt.astype(BF16), kt.astype(BF16), v.astype(BF16), st.astype(BF16)

            def out_h(hs):
                a = jnp.where(tril, _dot(qtb[:, hs], ktb[:, hs], NT), 0.0).astype(BF16)
                return _dot(qe[:, hs], stb[:, hs], NT) + _dot(a, vb[:, hs], NN)

            o_ref[sl, :] = _heads(out_h)
            st_ref[cc] = stb
            st = st * jnp.exp(bl) + _heads(lambda hs: _dot(vb[:, hs], kh[:, hs], TN))
        st_scr[...] = st
        o = o_ref[...]
        gate = g_ref[...]
        rec_ref[...] = (o * lax.rsqrt(_head_mean(o * o) + RMS_EPS) * gain_ref[...] * (gate * _sigmoid(gate))).astype(BF16)

    row = pl.BlockSpec((tb, HGRN_W), lambda i: (i, 0))
    return pl.pallas_call(
        body, name="hgrn_fwd", grid=(nb,),
        in_specs=_hgrn_specs(tb, lambda i: i) + [pl.BlockSpec((2, HGRN_W), lambda i: (0, 0)),
                                                 pl.BlockSpec((1, HGRN_W), lambda i: (0, 0))],
        out_specs=[row, row, pl.BlockSpec((cpb, BLK, HGRN_W), lambda i: (i, 0, 0))],
        out_shape=[_sds((s, HGRN_W), F32), _sds((s, HGRN_W), BF16), _sds((nc, BLK, HGRN_W), BF16)],
        scratch_shapes=[pltpu.VMEM((BLK, HGRN_W), F32)],
        compiler_params=_cp(("arbitrary",)),
    )(proj, proj, proj, proj, lb_logits, out_gain)


def _hgrn_bwd(proj, o_pre, states, dcat, lb_logits, out_gain):
    s = proj.shape[0]
    tb = min(HGRN_TB, s)
    nb, cpb, nc = s // tb, tb // CHUNK, s // CHUNK

    def body(q_ref, f_ref, i_ref, g_ref, o_ref, st_ref, stn_ref, dy_ref, lbl_ref, gain_ref,
             dq_ref, df_ref, di_ref, dg_ref, dgain_ref, dlbl_ref, do_scr, dst_scr, dlb_scr):
        step = pl.program_id(0)

        @pl.when(step == 0)
        def _():
            dst_scr[...] = jnp.zeros_like(dst_scr)
            dlb_scr[...] = jnp.zeros_like(dlb_scr)
            dgain_ref[...] = jnp.zeros_like(dgain_ref)

        lb = _lower_bound(lbl_ref[...])
        gain = gain_ref[...]
        o = o_ref[...]
        r = lax.rsqrt(_head_mean(o * o) + RMS_EPS)
        nrm = o * r
        gate = g_ref[...]
        sgt = _sigmoid(gate)
        dy = dy_ref[...]
        dg_ref[...] = (dy * nrm * gain * (sgt * (1.0 + gate * (1.0 - sgt)))).astype(BF16)
        dng = dy * (gate * sgt)
        dgain_ref[...] += _colsum(dng * nrm)
        dn = dng * gain
        do_scr[...] = r * (dn - nrm * _head_mean(dn * nrm))

        r64 = lax.broadcasted_iota(jnp.int32, (CHUNK, CHUNK), 0)
        c64 = lax.broadcasted_iota(jnp.int32, (CHUNK, CHUNK), 1)
        tril = r64 >= c64
        tri = tril.astype(BF16)
        triu = (r64 <= c64).astype(BF16)
        dst = dst_scr[...]
        dlb = dlb_scr[...]
        for cc in reversed(range(cpb)):
            sl = slice(cc * CHUNK, (cc + 1) * CHUNK)
            qp, sq, qf, sg, f, kf, v, b, bm, bl, qt, kt = _hgrn_chunk(q_ref, f_ref, i_ref, sl, lb, tri)
            stf = st_ref[cc].astype(F32)
            st_end = (st_ref[cc + 1] if cc + 1 < cpb else stn_ref[0]).astype(F32)
            csum = jnp.sum(st_end * dst, axis=0, keepdims=True)
            doc = do_scr[sl, :]
            dob, dstb = doc.astype(BF16), dst.astype(BF16)
            eb = jnp.exp(b)
            qe = (qf * eb).astype(BF16)
            kh = (kf * jnp.exp(bl - b)).astype(BF16)
            qtb, ktb = qt.astype(BF16), kt.astype(BF16)
            parts = []
            for h in range(HGRN_W // BLK):
                hs = slice(h * BLK, (h + 1) * BLK)
                da = jnp.where(tril, _dot3(doc[:, hs], v[:, hs], NT), 0.0)
                a = jnp.where(tril, _dot(qtb[:, hs], ktb[:, hs], NT), 0.0).astype(BF16)
                parts.append((
                    _dot3(da, kt[:, hs], NN), _dot3(doc[:, hs], stf[:, hs], NN),
                    _dot3(da, qt[:, hs], TN), _dot3(v[:, hs], dst[:, hs], NN),
                    _dot(a, dob[:, hs], TN) + _dot(kh[:, hs], dstb[:, hs], NT),
                    _dot(dob[:, hs], qe[:, hs], TN)))
            dqt, dqi, dkt, dks, dv, upd = (jnp.concatenate([p[n] for p in parts], axis=1) for n in range(6))
            dqf = dqt * jnp.exp(b - bm) + eb * dqi
            dkf = dkt * jnp.exp(bm - b) + jnp.exp(bl - b) * dks
            gq = qf * dqf - kf * dkf
            dlogf = csum + _cumsum_rows(triu, gq)
            dfv = dlogf / f - dkf
            dq_ref[sl, :] = (dqf * (sq * (1.0 + qp * (1.0 - sq)))).astype(BF16)
            df_ref[sl, :] = (dfv * (1.0 - lb) * sg * (1.0 - sg)).astype(BF16)
            di_ref[sl, :] = dv.astype(BF16)
            dst = dst * jnp.exp(bl) + upd
            dlb = dlb + _colsum(dfv * (1.0 - sg))
        dst_scr[...] = dst
        dlb_scr[...] = dlb

        @pl.when(step == nb - 1)
        def _():
            t = dlb * lb * (1.0 - lb)
            dlbl_ref[...] = jnp.concatenate([t, -t], axis=0)

    rev = lambda i: nb - 1 - i
    row = pl.BlockSpec((tb, HGRN_W), lambda i: (rev(i), 0))
    res = pl.pallas_call(
        body, name="hgrn_bwd", grid=(nb,),
        in_specs=_hgrn_specs(tb, rev) + [
            row, pl.BlockSpec((cpb, BLK, HGRN_W), lambda i: (rev(i), 0, 0)),
            pl.BlockSpec((1, BLK, HGRN_W), lambda i: (jnp.minimum((rev(i) + 1) * cpb, nc - 1), 0, 0)),
            pl.BlockSpec((tb, HGRN_W), lambda i: (rev(i), ATTN_W // HGRN_W)),
            pl.BlockSpec((2, HGRN_W), lambda i: (0, 0)), pl.BlockSpec((1, HGRN_W), lambda i: (0, 0))],
        out_specs=[row, row, row, row, pl.BlockSpec((1, HGRN_W), lambda i: (0, 0)),
                   pl.BlockSpec((2, HGRN_W), lambda i: (0, 0))],
        out_shape=[_sds((s, HGRN_W), BF16)] * 4 + [_sds((1, HGRN_W), F32), _sds((2, HGRN_W), F32)],
        scratch_shapes=[pltpu.VMEM((tb, HGRN_W), F32), pltpu.VMEM((BLK, HGRN_W), F32), pltpu.VMEM((1, HGRN_W), F32)],
        compiler_params=_cp(("arbitrary",)),
    )(proj, proj, proj, proj, o_pre, states, states, dcat, lb_logits, out_gain)
    return res


def _place():
    return lax.axis_index("x"), lax.axis_index("y"), lax.axis_index("c")


def _flip(x, y, ox, oy):
    return (1 - x if ox else x), (1 - y if oy else y)


def _half(rows, cc):
    return pl.ds(cc * (rows // 2), rows // 2)


def _remote(src, dst, ssem, rsem, to):
    return pltpu.make_async_remote_copy(src_ref=src, dst_ref=dst, send_sem=ssem, recv_sem=rsem,
                                        device_id=to, device_id_type=MESH)


def _ag_chip_copies(ins, outs, ssem, rsem):
    x, y, c = _place()
    j = 2 * x + y
    cps = []
    for k in range(len(ins)):
        rows = ins[k].shape[0]
        for idx, (ox, oy) in enumerate(FLIPS):
            px, py = _flip(x, y, ox, oy)
            cps.append(_remote(ins[k].at[_half(rows, c)], outs[k].at[j, _half(rows, c)],
                               ssem.at[k, idx], rsem.at[k, idx], (px, py, c)))
    return cps


def _ag_start(ins, outs, ssem, rsem):
    for cp in _ag_chip_copies(ins, outs, ssem, rsem):
        cp.start()


def _ag_finish(ins, outs, ssem, rsem):
    x, y, c = _place()
    sib = (x, y, 1 - c)
    passed = []
    for k in range(len(ins)):
        rows = ins[k].shape[0]
        for idx, (ox, oy) in enumerate(FLIPS):
            px, py = _flip(x, y, ox, oy)
            blk = outs[k].at[2 * px + py, _half(rows, c)]
            _remote(blk, blk, ssem.at[k, idx], rsem.at[k, idx], (px, py, c)).wait_recv()
            cp = _remote(blk, blk, ssem.at[k, 3 + idx], rsem.at[k, 3 + idx], sib)
            cp.start()
            passed.append(cp)
    for k in range(len(ins)):
        rows = ins[k].shape[0]
        for idx, (ox, oy) in enumerate(FLIPS):
            px, py = _flip(x, y, ox, oy)
            blk = outs[k].at[2 * px + py, _half(rows, 1 - c)]
            _remote(blk, blk, ssem.at[k, 3 + idx], rsem.at[k, 3 + idx], sib).wait_recv()
    for cp in _ag_chip_copies(ins, outs, ssem, rsem) + passed:
        cp.wait_send()


def _ag_shapes(shards):
    nk = len(shards)
    return ([_sds((N_CHIPS,) + tuple(w.shape), w.dtype) for w in shards],
            [pltpu.SemaphoreType.DMA((nk, 6)), pltpu.SemaphoreType.DMA((nk, 6))])


def _with_own(gathered, shard, j):
    return lax.dynamic_update_index_in_dim(gathered, shard, j, 0)


def _rs_pair_copies(ins, outs, ssem, rsem):
    x, y, c = _place()
    return [_remote(ins[k].at[:, _half(ins[k].shape[1], 1 - c)], outs[k], ssem.at[k], rsem.at[k], (x, y, 1 - c))
            for k in range(len(ins))]


def _rs_pair_start(ins, outs, ssem, rsem):
    for cp in _rs_pair_copies(ins, outs, ssem, rsem):
        cp.start()


def _rs_pair_finish(ins, outs, ssem, rsem):
    for cp in _rs_pair_copies(ins, outs, ssem, rsem):
        cp.wait()


def _rs_pair_exchange(grads):
    nk = len(grads)
    return (grads, [_sds((N_CHIPS, g.shape[1] // 2, g.shape[2]), g.dtype) for g in grads],
            [pltpu.SemaphoreType.DMA((nk,)), pltpu.SemaphoreType.DMA((nk,))], _rs_pair_start, _rs_pair_finish)


def _rs_pair(name, grads):
    nk = len(grads)
    ins, out_shape, sems, start, finish = _rs_pair_exchange(grads)

    def body(*refs):
        start(refs[:nk], refs[nk:2 * nk], *refs[2 * nk:])
        finish(refs[:nk], refs[nk:2 * nk], *refs[2 * nk:])

    return pl.pallas_call(body, name=name, in_specs=[ANY] * nk, out_specs=[ANY] * nk, out_shape=out_shape,
                          scratch_shapes=sems)(*ins)


def _rs_chip_copies(ins, outs, ssem, rsem):
    x, y, c = _place()
    cps = []
    for k in range(len(ins)):
        for idx, (ox, oy) in enumerate(FLIPS):
            px, py = _flip(x, y, ox, oy)
            cps.append(_remote(ins[k].at[2 * px + py], outs[k].at[idx], ssem.at[k, idx], rsem.at[k, idx], (px, py, c)))
    return cps


def _rs_chips_start(ins, outs, ssem, rsem):
    for cp in _rs_chip_copies(ins, outs, ssem, rsem):
        cp.start()


def _rs_chips_finish(ins, outs, ssem, rsem):
    for cp in _rs_chip_copies(ins, outs, ssem, rsem):
        cp.wait()


def _rs_chips_shapes(psums):
    nk = len(psums)
    return ([_sds((3,) + tuple(p.shape[1:]), p.dtype) for p in psums],
            [pltpu.SemaphoreType.DMA((nk, 3)), pltpu.SemaphoreType.DMA((nk, 3))])


def _rs_share(fulls):
    nk = len(fulls)

    def body(*refs):
        ins, outs = refs[:nk], refs[nk:2 * nk]
        ssem, rsem = refs[2 * nk:]
        x, y, c = _place()
        cps = []
        for k in range(nk):
            rows = fulls[k].shape[0]
            cp = _remote(ins[k].at[_half(rows, c)], outs[k].at[_half(rows, c)], ssem.at[k], rsem.at[k], (x, y, 1 - c))
            cp.start()
            cps.append(cp)
        for k, cp in enumerate(cps):
            rows = fulls[k].shape[0]
            cp.wait_send()
            theirs = outs[k].at[_half(rows, 1 - c)]
            _remote(theirs, theirs, ssem.at[k], rsem.at[k], (x, y, 1 - c)).wait_recv()

    return pl.pallas_call(
        body, name="rs_share", in_specs=[ANY] * nk, out_specs=[ANY] * nk,
        out_shape=[_sds(f.shape, f.dtype) for f in fulls], input_output_aliases={k: k for k in range(nk)},
        scratch_shapes=[pltpu.SemaphoreType.DMA((nk,)), pltpu.SemaphoreType.DMA((nk,))],
    )(*fulls)


def _allreduce_small(v):
    ndev = 8

    def body(in_ref, out_ref, buf, ssem, rsem):
        x, y, c = _place()
        me = 4 * x + 2 * y + c
        buf[me] = in_ref[...]
        cps = []
        for k in range(1, ndev):
            ox, oy, oc = (k >> 2) & 1, (k >> 1) & 1, k & 1
            px, py = _flip(x, y, ox, oy)
            pc = 1 - c if oc else c
            cp = pltpu.make_async_remote_copy(src_ref=in_ref, dst_ref=buf.at[me], send_sem=ssem.at[k - 1],
                                              recv_sem=rsem.at[k - 1], device_id=(px, py, pc), device_id_type=MESH)
            cp.start()
            cps.append((cp, 4 * px + 2 * py + pc, (px, py, pc)))
        for k, (cp, src, peer) in enumerate(cps):
            cp.wait_send()
            pltpu.make_async_remote_copy(src_ref=in_ref, dst_ref=buf.at[src], send_sem=ssem.at[k],
                                         recv_sem=rsem.at[k], device_id=peer, device_id_type=MESH).wait_recv()
        acc = buf[0]
        for i in range(1, ndev):
            acc = acc + buf[i]
        out_ref[...] = acc

    return pl.pallas_call(
        body, name="allreduce_small",
        in_specs=[pl.BlockSpec(memory_space=pltpu.VMEM)], out_specs=pl.BlockSpec(memory_space=pltpu.VMEM),
        out_shape=_sds(v.shape, v.dtype),
        scratch_shapes=[pltpu.VMEM((ndev,) + v.shape, v.dtype), pltpu.SemaphoreType.DMA((ndev - 1,)),
                        pltpu.SemaphoreType.DMA((ndev - 1,))],
    )(v)


def _rs_sum1(name, g, recv, c_idx):
    _, r, cdim = g.shape
    hr = r // 2
    tr = min(hr, 256)
    nr = hr // tr

    def body(c_ref, g_ref, r_ref, o32_ref, o16_ref):
        v = g_ref[...] + r_ref[...].astype(F32)
        o32_ref[...] = v
        o16_ref[...] = v.astype(BF16)

    spec = pl.BlockSpec((None, tr, cdim), lambda j, i, c_ref: (j, i, 0))
    return pl.pallas_call(
        body, name=name,
        grid_spec=pltpu.PrefetchScalarGridSpec(
            num_scalar_prefetch=1, grid=(N_CHIPS, nr),
            in_specs=[pl.BlockSpec((None, tr, cdim), lambda j, i, c_ref: (j, c_ref[0] * nr + i, 0)), spec],
            out_specs=[spec, spec]),
        out_shape=[_sds((N_CHIPS, hr, cdim), F32), _sds((N_CHIPS, hr, cdim), BF16)],
        compiler_params=_cp(("parallel", "parallel")),
    )(c_idx, g, recv)


def _rs_sum2(name, p32, recv, jc_idx):
    _, hr, cdim = p32.shape
    tr = min(hr, 256)
    nr = hr // tr

    def body(jc_ref, p_ref, r_ref, o_ref):
        o_ref[...] = ((p_ref[...] + r_ref[0].astype(F32)) + r_ref[1].astype(F32)) + r_ref[2].astype(F32)

    return pl.pallas_call(
        body, name=name,
        grid_spec=pltpu.PrefetchScalarGridSpec(
            num_scalar_prefetch=1, grid=(nr,),
            in_specs=[pl.BlockSpec((None, tr, cdim), lambda i, jc: (jc[0], i, 0)),
                      pl.BlockSpec((3, tr, cdim), lambda i, jc: (0, i, 0))],
            out_specs=pl.BlockSpec((tr, cdim), lambda i, jc: (jc[1] * nr + i, 0))),
        out_shape=_sds((2 * hr, cdim), F32),
        compiler_params=_cp(("parallel",)),
    )(jc_idx, p32, recv)


def _adamw(name, w, g, m, v):
    r, cdim = w.shape
    tr = min(r, 256)
    c1 = 1.0 - ADAM_B1 ** ADAM_STEP
    c2 = 1.0 - ADAM_B2 ** ADAM_STEP

    def body(w_ref, g_ref, m_ref, v_ref, d_ref, nm_ref, nv_ref):
        gv = g_ref[...]
        nm = ADAM_B1 * m_ref[...] + (1.0 - ADAM_B1) * gv
        nv = ADAM_B2 * v_ref[...] + (1.0 - ADAM_B2) * (gv * gv)
        d_ref[...] = -ADAM_LR * ((nm / c1) / (jnp.sqrt(nv / c2) + ADAM_EPS) + ADAM_WD * w_ref[...])
        nm_ref[...] = nm
        nv_ref[...] = nv

    spec = pl.BlockSpec((tr, cdim), lambda i: (i, 0))
    return pl.pallas_call(
        body, name=name, grid=(r // tr,), in_specs=[spec] * 4, out_specs=[spec] * 3,
        out_shape=[_sds((r, cdim), F32)] * 3, compiler_params=_cp(("parallel",)),
    )(w, g, m, v)


def _pack_small(mix_pre, attn_out, lb_logits, hgrn_out, mix_post, mlp_pre, mlp_post, extra=None):
    spare = jnp.zeros((1, D_MODEL), F32)
    rows = [mix_pre, jnp.concatenate([attn_out, hgrn_out], axis=1),
            jnp.concatenate([lb_logits[0:1], lb_logits[1:2]], axis=1), mix_post, mlp_pre, mlp_post,
            spare if extra is None else extra, spare]
    return jnp.concatenate(rows, axis=0)


def _unpack_small(p):
    return (p[0:1], p[1:2, :ATTN_W], jnp.concatenate([p[2:3, :HGRN_W], p[2:3, HGRN_W:]], axis=0),
            p[1:2, ATTN_W:], p[3:4], p[4:5], p[5:6])


def kernel(x, mix_pre_norm, w_in, attn_out_norm, hgrn_lb_logits, hgrn_out_norm, w_out, mix_post_norm, mlp_pre_norm, w_ff1, w_ff2, mlp_post_norm, loss_target, m_mix_pre_norm, m_w_in, m_attn_out_norm, m_hgrn_lb_logits, m_hgrn_out_norm, m_w_out, m_mix_post_norm, m_mlp_pre_norm, m_w_ff1, m_w_ff2, m_mlp_post_norm, v_mix_pre_norm, v_w_in, v_attn_out_norm, v_hgrn_lb_logits, v_hgrn_out_norm, v_w_out, v_mix_post_norm, v_mlp_pre_norm, v_w_ff1, v_w_ff2, v_mlp_post_norm):
    s = x.shape[1]
    xs = x.reshape(s, D_MODEL)
    tgt = loss_target.reshape(s, D_MODEL)
    cx, cy, cc = _place()
    chip = 2 * cx + cy
    c_idx = jnp.reshape(cc, (1,)).astype(jnp.int32)
    jc_idx = jnp.stack([chip, cc]).astype(jnp.int32)

    big_w = [w_in[0], w_out[0], w_ff1[0], w_ff2[0]]
    big_m = [m_w_in[0], m_w_out[0], m_w_ff1[0], m_w_ff2[0]]
    big_v = [v_w_in[0], v_w_out[0], v_w_ff1[0], v_w_ff2[0]]
    shards = [w.astype(BF16) for w in big_w]

    h, wg_in = _rows_call("norm_in", lambda xv, g: ((xv * _rstd(xv) * g),),
                          [(xs, _row(D_MODEL)), (mix_pre_norm, "full")], [(D_MODEL, BF16, "row")], s,
                          exchange=(shards[:1], *_ag_shapes(shards[:1]), _ag_start, _ag_finish))
    wg_in = _with_own(wg_in, shards[0], chip)
    (proj,) = _mm_cols("mm_proj", h, wg_in, NN, [F32])
    hg_o, rec, states = _hgrn_fwd(proj, hgrn_lb_logits, hgrn_out_norm)
    attn_o, attn_lse, wg_out, wg_1, wg_2 = _attn_fwd(proj, shards[1:])
    wg_out, wg_1, wg_2 = (_with_own(g, w, chip) for g, w in zip((wg_out, wg_1, wg_2), shards[1:]))
    (attn_n,) = _rows_call("attn_norm", lambda o, gain: (o * _rstd(o) * gain,),
                           [(attn_o, _row(ATTN_W)), (attn_out_norm, "full")], [(ATTN_W, BF16, "row")], s)
    cat = jnp.concatenate([attn_n, rec], axis=1)

    def post1(mv, xv, g_post, g_pre2):
        x1 = xv + mv * _rstd(mv) * g_post
        return mv, x1, x1 * _rstd(x1) * g_pre2

    mixed, x1, h2 = _rows_call(
        "mm_mixed", post1, [(xs, _row(D_MODEL)), (mix_post_norm, "full"), (mlp_pre_norm, "full")],
        [(D_MODEL, F32, "row"), (D_MODEL, F32, "row"), (D_MODEL, BF16, "row")], s,
        matmul=(cat, wg_out.reshape(D_MODEL, D_MODEL)))

    def sq_relu(u):
        r = jnp.maximum(u, 0.0)
        return r * r, r

    act, ru = _mm_cols("mm_ff1", h2, wg_1, NN, [BF16, BF16], epi=sq_relu)

    def post2(fv, x1v, tv, g):
        y = x1v + fv * _rstd(fv) * g
        dy = (y - tv) * (1.0 / D_MODEL)
        err = y - tv
        loss = 0.5 * jnp.sum(jnp.mean(err * err, axis=-1, keepdims=True), axis=0, keepdims=True)
        dff, dgc = _norm_bwd(fv, g, dy)
        return dy, dff, _colsum(dgc), jnp.broadcast_to(loss, (1, BLK))

    dy, dff, g_mlp_post, loss_part = _rows_call(
        "mm_ff2", post2, [(x1, _row(D_MODEL)), (tgt, _row(D_MODEL)), (mlp_post_norm, "full")],
        [(D_MODEL, F32, "row"), (D_MODEL, BF16, "row"), (D_MODEL, F32, "acc"), (BLK, F32, "acc")], s,
        matmul=(act, wg_2.reshape(D_FF, D_MODEL)))

    (du,) = _mm_cols("mm_du", dff, wg_2, NT, [BF16], epi=lambda acc, r: (acc * (2.0 * r.astype(F32)),),
                     extras=(ru,))
    gw_2 = _mm_wgrad("mm_gw2", act, dff, True)
    gw_1 = _mm_wgrad("mm_gw1", h2, du, False)

    def bwd_mid(dh2v, dyv, x1v, mv, g_pre2, g_post):
        d1, gc1 = _norm_bwd(x1v, g_pre2, dh2v)
        dx1 = dyv + d1
        dm, gc2 = _norm_bwd(mv, g_post, dx1)
        return dx1, dm, _colsum(gc1), _colsum(gc2)

    dx1, dmixed, g_mlp_pre, g_mix_post, *from_pair = _rows_call(
        "mm_dh2", bwd_mid, [(dy, _row(D_MODEL)), (x1, _row(D_MODEL)), (mixed, _row(D_MODEL)),
                            (mlp_pre_norm, "full"), (mix_post_norm, "full")],
        [(D_MODEL, F32, "row"), (D_MODEL, BF16, "row"), (D_MODEL, F32, "acc"), (D_MODEL, F32, "acc")], s,
        matmul=(du, wg_1), exchange=_rs_pair_exchange([gw_1[1], gw_2[1]]))

    (dcat,) = _mm_cols("mm_dcat", dmixed, wg_out, NT, [F32])
    gw_out = _mm_wgrad("mm_gwout", cat, dmixed, True)
    names = ["out", "ff1", "ff2", "in"]
    ready = [gw_out, gw_1, gw_2]
    from_pair = list(_rs_pair("rs_pair_out", [gw_out[1]])) + from_pair
    pair = [_rs_sum1(f"rs_sum1_{n}", g[0], r, c_idx) for n, g, r in zip(names, ready, from_pair)]

    def attn_norm_bwd(dc, o, gain):
        do, gc = _norm_bwd(o, gain, dc)
        t = do * o
        lane = lax.broadcasted_iota(jnp.int32, (t.shape[0], BLK), 1) < 64
        parts = []
        for p in range(ATTN_W // BLK):
            tp = t[:, p * BLK:(p + 1) * BLK]
            sa = jnp.sum(jnp.where(lane, tp, 0.0), axis=1, keepdims=True)
            sb = jnp.sum(jnp.where(lane, 0.0, tp), axis=1, keepdims=True)
            parts.append(jnp.where(lane, sa, sb))
        return do, jnp.concatenate(parts, axis=1), _colsum(gc)

    do_attn, delta, g_attn_out = _rows_call(
        "attn_norm_bwd", attn_norm_bwd, [(dcat, _row(ATTN_W, 0)), (attn_o, _row(ATTN_W)), (attn_out_norm, "full")],
        [(ATTN_W, F32, "row"), (ATTN_W, F32, "row"), (ATTN_W, F32, "acc")], s)
    dq, dk, dv, *from_chips = _attn_bwd(proj, do_attn, attn_lse, delta, [p[1] for p in pair])
    dhq, dhf, dhi, dhg, g_hgrn_out, g_lb = _hgrn_bwd(proj, hg_o, states, dcat, hgrn_lb_logits, hgrn_out_norm)

    dproj = jnp.concatenate([dq, dk, dv, dhq, dhf, dhi, dhg], axis=1)
    gw_in = _mm_wgrad("mm_gwin", h, dproj, False)
    (from_pair_in,) = _rs_pair("rs_pair_in", [gw_in[1]])
    pair.append(_rs_sum1("rs_sum1_in", gw_in[0], from_pair_in, c_idx))
    rs_shape, rs_sems = _rs_chips_shapes([pair[3][1]])

    def bwd_in(dhv, dx1v, xv, g):
        d0, gc = _norm_bwd(xv, g, dhv)
        return dx1v + d0, _colsum(gc)

    grad_x, g_mix_pre, from_chips_in = _rows_call(
        "mm_dh", bwd_in, [(dx1, _row(D_MODEL)), (xs, _row(D_MODEL)), (mix_pre_norm, "full")],
        [(D_MODEL, F32, "row"), (D_MODEL, F32, "acc")], s, matmul=(dproj, wg_in),
        exchange=([pair[3][1]], rs_shape, rs_sems, _rs_chips_start, _rs_chips_finish))
    from_chips.append(from_chips_in)

    loss_row = jnp.pad(loss_part, ((0, 0), (0, D_MODEL - BLK)))
    small_g = _allreduce_small(_pack_small(g_mix_pre, g_attn_out, g_lb, g_hgrn_out, g_mix_post, g_mlp_pre, g_mlp_post,
                                           extra=loss_row))
    loss = small_g[6, 0]

    reduced = [_rs_sum2(f"rs_sum2_{n}", p[0], r, jc_idx) for n, p, r in zip(names, pair, from_chips)]
    g_wout, g_w1, g_w2, g_win = _rs_share(reduced)
    full = [g_win, g_wout, g_w1, g_w2]

    upd = [_adamw(f"adamw_{n}", w, g, m, v) for n, w, g, m, v in zip(("in", "out", "ff1", "ff2"), big_w, full, big_m, big_v)]
    small_w = _pack_small(mix_pre_norm, attn_out_norm, hgrn_lb_logits, hgrn_out_norm, mix_post_norm, mlp_pre_norm,
                          mlp_post_norm)
    small_m = _pack_small(m_mix_pre_norm, m_attn_out_norm, m_hgrn_lb_logits, m_hgrn_out_norm, m_mix_post_norm,
                          m_mlp_pre_norm, m_mlp_post_norm)
    small_v = _pack_small(v_mix_pre_norm, v_attn_out_norm, v_hgrn_lb_logits, v_hgrn_out_norm, v_mix_post_norm,
                          v_mlp_pre_norm, v_mlp_post_norm)
    small_upd = _adamw("adamw_small", small_w, small_g, small_m, small_v)

    def assemble(small, big):
        sm = _unpack_small(small)
        return (sm[0], big[0][None], sm[1], sm[2], sm[3], big[1][None], sm[4], sm[5], big[2][None], big[3][None], sm[6])

    g_out = assemble(small_g, full)
    d_out = assemble(small_upd[0], [u[0] for u in upd])
    m_out = assemble(small_upd[1], [u[1] for u in upd])
    v_out = assemble(small_upd[2], [u[2] for u in upd])
    return (loss, grad_x.reshape(x.shape), *g_out, *d_out, *m_out, *v_out)
```

```python
import numpy as np
import jax
import jax.numpy as jnp
from jax import lax
from jax.experimental import pallas as pl
from jax.experimental.pallas import tpu as pltpu

F32 = jnp.float32
BF16 = jnp.bfloat16
MESH = pl.DeviceIdType.MESH
ANY = pl.BlockSpec(memory_space=pl.ANY)

RMS_EPS = 1e-6
D_MODEL = 1024
ATTN_W = 512
HGRN_W = 512
PROJ_W = 3584
D_FF = 4096
N_CHIPS = 4
BLK = 128
CHUNK = 64
HGRN_TB = 512
ATTN_GROUP = 8
DILATIONS = (1, 4, 16)
ATTN_SCALE = 0.125
ROW_TILE = 512
MM_TILE = 1024
VMEM_LIMIT = 48 * 2 ** 20
FLIPS = ((1, 0), (0, 1), (1, 1))

ADAM_LR, ADAM_B1, ADAM_B2, ADAM_EPS, ADAM_WD, ADAM_STEP = 0.001, 0.9, 0.999, 1e-08, 0.01, 10


def _cp(sem=None):
    return pltpu.CompilerParams(dimension_semantics=sem, vmem_limit_bytes=VMEM_LIMIT)


def _sigmoid(v):
    return 1.0 / (1.0 + jnp.exp(-v))


def _dot(a, b, contract, precision=None):
    return lax.dot_general(a, b, (contract, ((), ())), preferred_element_type=F32, precision=precision)


NN = ((1,), (0,))
NT = ((1,), (1,))
TN = ((0,), (0,))


def _sds(shape, dtype):
    return jax.ShapeDtypeStruct(shape, dtype)


def _resident(shape):
    return pl.BlockSpec(shape, lambda *_: (0,) * len(shape), pipeline_mode=pl.Buffered(1))


def _mm_cols(name, a, w, contract, out_dtypes, epi=None, extras=()):
    m, k = a.shape
    jn = w.shape[0]
    nj = w.shape[2] if contract == NN else w.shape[1]
    tm = min(m, MM_TILE)
    n_ex, parts = len(extras), 2

    def body(a_ref, w_ref, *rest):
        ex, out_refs = rest[:n_ex], rest[n_ex:]
        i = pl.program_id(1)
        part = tm // parts
        for h in range(parts):
            rows = slice(h * part, (h + 1) * part)
            acc = _dot(a_ref[pl.ds(pl.multiple_of(i * tm + h * part, part), part), :], w_ref[...], contract)
            res = epi(acc, *[e[rows, :] for e in ex]) if epi else (acc,)
            for o, r in zip(out_refs, res):
                o[rows, :] = r.astype(o.dtype)

    blk = pl.BlockSpec((tm, nj), lambda j, i: (i, j))
    return pl.pallas_call(
        body, name=name, grid=(jn, m // tm),
        in_specs=[_resident((m, k)), pl.BlockSpec((None,) + w.shape[1:], lambda j, i: (j, 0, 0))] + [blk] * n_ex,
        out_specs=[blk] * len(out_dtypes), out_shape=[_sds((m, jn * nj), dt) for dt in out_dtypes],
        compiler_params=_cp(("parallel", "parallel")),
    )(a, w, *extras)


def _mm_wgrad(name, a, b, a_by_j):
    s = a.shape[0]
    if a_by_j:
        r, c = a.shape[1] // N_CHIPS, b.shape[1]
        in_specs = [pl.BlockSpec((s, r), lambda j: (0, j)), _resident((s, c))]
    else:
        r, c = a.shape[1], b.shape[1] // N_CHIPS
        in_specs = [_resident((s, r)), pl.BlockSpec((s, c), lambda j: (0, j))]
    tr = min(r, 512)

    def body(a_ref, b_ref, o32_ref, o16_ref):
        for h in range(r // tr):
            cols = slice(h * tr, (h + 1) * tr)
            acc = _dot(a_ref[:, cols], b_ref[...], TN)
            o32_ref[cols, :] = acc
            o16_ref[cols, :] = acc.astype(BF16)

    out = pl.BlockSpec((None, r, c), lambda j: (j, 0, 0))
    return pl.pallas_call(
        body, name=name, grid=(N_CHIPS,), in_specs=in_specs, out_specs=[out, out],
        out_shape=[_sds((N_CHIPS, r, c), F32), _sds((N_CHIPS, r, c), BF16)], compiler_params=_cp(("parallel",)),
    )(a, b)


def _rows_call(name, fn, ins, outs, s, tm=ROW_TILE, matmul=None, exchange=None):
    in_specs = []
    if matmul:
        a, w = matmul
        in_specs += [pl.BlockSpec((tm, a.shape[1]), lambda i: (i, 0)), _resident(w.shape)]
    for arr, kind in ins:
        if kind == "full":
            in_specs.append(pl.BlockSpec(arr.shape, lambda i: (0, 0)))
        else:
            _, w_, cb = kind
            in_specs.append(pl.BlockSpec((tm, w_), lambda i, cb=cb: (i, cb)))
    out_specs, out_shape, is_acc = [], [], []
    for w_, dt, kind in outs:
        if kind == "acc":
            out_specs.append(pl.BlockSpec((1, w_), lambda i: (0, 0)))
            out_shape.append(_sds((1, w_), dt))
        else:
            out_specs.append(pl.BlockSpec((tm, w_), lambda i: (i, 0)))
            out_shape.append(_sds((s, w_), dt))
        is_acc.append(kind == "acc")
    n_mm, n_in, n_out = (2 if matmul else 0), len(ins), len(outs)
    x_ins, x_shapes, x_sems, x_start, x_finish = exchange if exchange else ((), [], [], None, None)
    n_x = len(x_ins)
    steps = s // tm

    def body(*refs):
        in_refs, xi = refs[n_mm:n_mm + n_in], refs[n_mm + n_in:n_mm + n_in + n_x]
        out_refs = refs[n_mm + n_in + n_x:n_mm + n_in + n_x + n_out]
        xo, sems = refs[n_mm + n_in + n_x + n_out:n_mm + n_in + 2 * n_x + n_out], refs[n_mm + n_in + 2 * n_x + n_out:]
        i = pl.program_id(0)

        if exchange:
            @pl.when(i == 0)
            def _():
                x_start(xi, xo, *sems)

        for o, acc in zip(out_refs, is_acc):
            if acc:
                @pl.when(i == 0)
                def _(o=o):
                    o[...] = jnp.zeros_like(o)

        parts = 2 if matmul else 1
        for h in range(parts):
            rows = slice(h * (tm // parts), (h + 1) * (tm // parts))
            args = [r[...] if kind == "full" else r[rows, :] for r, (_, kind) in zip(in_refs, ins)]
            if matmul:
                a_ref, w_ref = refs[:2]
                if len(w_ref.shape) == 2:
                    acc = _dot(a_ref[rows, :], w_ref[...], NN)
                else:
                    kj = w_ref.shape[2]
                    acc = _dot(a_ref[rows, 0:kj], w_ref[0], NT)
                    for j in range(1, w_ref.shape[0]):
                        acc = acc + _dot(a_ref[rows, j * kj:(j + 1) * kj], w_ref[j], NT)
                args.insert(0, acc)
            for o, r, acc in zip(out_refs, fn(*args), is_acc):
                if acc:
                    o[...] += r.astype(o.dtype)
                else:
                    o[rows, :] = r.astype(o.dtype)

        if exchange:
            @pl.when(i == steps - 1)
            def _():
                x_finish(xi, xo, *sems)

    sem = ("arbitrary",) if any(is_acc) or exchange else ("parallel",)
    return pl.pallas_call(
        body, name=name, grid=(steps,), in_specs=in_specs + [ANY] * n_x, out_specs=out_specs + [ANY] * n_x,
        out_shape=out_shape + list(x_shapes), scratch_shapes=list(x_sems), compiler_params=_cp(sem),
    )(*(matmul or ()), *[a for a, _ in ins], *x_ins)


def _rstd(v):
    return lax.rsqrt(jnp.mean(v * v, axis=-1, keepdims=True) + RMS_EPS)


def _norm_bwd(v, gain, dy):
    r = _rstd(v)
    n = v * r
    dn = dy * gain
    dv = r * (dn - n * jnp.mean(dn * n, axis=-1, keepdims=True))
    return dv, dy * n


def _colsum(v):
    return jnp.sum(v, axis=0, keepdims=True)


def _row(w, cb=0):
    return ("row", w, cb)


N_PAIRS = ATTN_W // BLK


def _head_col(v, mask):
    return jnp.max(jnp.where(mask, v, -jnp.inf), axis=1, keepdims=True)


def _slopes():
    t = np.zeros((N_PAIRS, 8, 2 * BLK), np.float32)
    for p in range(N_PAIRS):
        for hh in range(2):
            t[p, hh, :] = 2.0 ** -(2 * p + hh + 1)
    return jnp.asarray(t)


def _rows(n, r, d):
    base = pl.multiple_of(n * (BLK * d), BLK)
    return pl.ds(base + r, BLK, stride=d) if d > 1 else pl.ds(base, BLK)


def _attn_bias(sl_ref, bias_scr):
    row = lax.broadcasted_iota(jnp.int32, (BLK, 2 * BLK), 0)
    col = lax.broadcasted_iota(jnp.int32, (BLK, 2 * BLK), 1)
    dist = row + BLK - col
    in_window = (dist >= 0) & (dist <= BLK)
    distf = dist.astype(F32)
    for di, d in enumerate(DILATIONS):
        for hh in range(2):
            bias_scr[2 * di + hh] = jnp.where(in_window, -(sl_ref[hh:hh + 1, :] * float(d)) * distf, -1e30)


def _first_block_penalty(n):
    col = lax.broadcasted_iota(jnp.int32, (1, 2 * BLK), 1)
    return jnp.where(col + n * BLK >= BLK, 0.0, -1e30)


def _attn_groups(s, d):
    nb = s // (BLK * d)
    g = ATTN_GROUP
    if d >= g:
        return [(nb, lambda n, r0=r0: [(n, r0 + u) for u in range(g)]) for r0 in range(0, d, g)]
    per = g // d
    return [(nb // per, lambda t: [(per * t + u, r) for u in range(per) for r in range(d)])]


def _attn_fwd(proj, shards):
    s = proj.shape[0]
    nk = len(shards)

    def body(sl_ref, q_ref, k_ref, v_ref, *rest):
        w_refs, (o_ref, l_ref) = rest[:nk], rest[nk:nk + 2]
        wg_refs, (bias_scr, ssem, rsem) = rest[nk + 2:2 * nk + 2], rest[2 * nk + 2:]
        pair = pl.program_id(0)

        @pl.when(pair == 0)
        def _():
            _ag_start(w_refs, wg_refs, ssem, rsem)

        _attn_bias(sl_ref, bias_scr)
        lane_q = lax.broadcasted_iota(jnp.int32, (BLK, BLK), 1) < 64
        lane_k = lax.broadcasted_iota(jnp.int32, (2 * BLK, BLK), 1) < 64

        def branch(n, r, di):
            d = DILATIONS[di]
            rows = _rows(n, r, d)
            prev = _rows(jnp.maximum(n - 1, 0), r, d)
            pen = _first_block_penalty(n)
            q2 = q_ref[rows, :] * ATTN_SCALE
            kk = jnp.concatenate([k_ref[prev, :], k_ref[rows, :]], axis=0).astype(BF16)
            vv = jnp.concatenate([v_ref[prev, :], v_ref[rows, :]], axis=0)
            o2 = jnp.zeros((BLK, BLK), F32)
            lse2 = jnp.zeros((BLK, BLK), F32)
            for hh in range(2):
                mq = lane_q if hh == 0 else ~lane_q
                mk = lane_k if hh == 0 else ~lane_k
                qm = jnp.where(mq, q2, 0.0).astype(BF16)
                sc = _dot(qm, kk, NT) + bias_scr[2 * di + hh] + pen
                m = jnp.max(sc, axis=1, keepdims=True)
                pr = jnp.exp(sc - m)
                den = jnp.sum(pr, axis=1, keepdims=True)
                vm = jnp.where(mk, vv, 0.0).astype(BF16)
                o2 = o2 + _dot(pr.astype(BF16), vm, NN) / den
                lse2 = jnp.where(mq, m + jnp.log(den), lse2)
            return rows, o2, lse2

        def merge(rows, o2, lse2, first):
            if first:
                o_ref[rows, :] = o2
                l_ref[rows, :] = lse2
            else:
                lo = l_ref[rows, :]
                mx = jnp.maximum(lo, lse2)
                ln = mx + jnp.log(jnp.exp(lo - mx) + jnp.exp(lse2 - mx))
                o_ref[rows, :] = jnp.exp(lo - ln) * o_ref[rows, :] + jnp.exp(lse2 - ln) * o2
                l_ref[rows, :] = ln

        for di, d in enumerate(DILATIONS):
            for trips, blocks in _attn_groups(s, d):
                def trip(t, carry, di=di, blocks=blocks):
                    done = [branch(n, r, di) for n, r in blocks(t)]
                    for rows, o2, lse2 in done:
                        merge(rows, o2, lse2, di == 0)
                    return carry

                lax.fori_loop(0, trips, trip, 0)

        @pl.when(pair == N_PAIRS - 1)
        def _():
            _ag_finish(w_refs, wg_refs, ssem, rsem)

    cb = lambda base: pl.BlockSpec((s, BLK), lambda p, base=base: (0, base + p))
    out = pl.BlockSpec((s, BLK), lambda p: (0, p))
    ag_shape, ag_sems = _ag_shapes(shards)
    return pl.pallas_call(
        body, name="attn_fwd", grid=(N_PAIRS,),
        in_specs=[pl.BlockSpec((None, 8, 2 * BLK), lambda p: (p, 0, 0)), cb(0), cb(N_PAIRS), cb(2 * N_PAIRS)]
        + [ANY] * nk,
        out_specs=[out, out] + [ANY] * nk, out_shape=[_sds((s, ATTN_W), F32)] * 2 + ag_shape,
        scratch_shapes=[pltpu.VMEM((2 * len(DILATIONS), BLK, 2 * BLK), F32)] + ag_sems,
        compiler_params=_cp(("arbitrary",)),
    )(_slopes(), proj, proj, proj, *shards)


def _attn_bwd(proj, do, lse, delta, psums):
    s = proj.shape[0]
    nk = len(psums)

    def body(sl_ref, q_ref, k_ref, v_ref, do_ref, l_ref, e_ref, *rest):
        p_refs, out16 = rest[:nk], rest[nk:nk + 3]
        got_refs, (dq_ref, dk_ref, dv_ref, bias_scr, ssem, rsem) = rest[nk + 3:2 * nk + 3], rest[2 * nk + 3:]
        pair = pl.program_id(0)

        @pl.when(pair == 0)
        def _():
            _rs_chips_start(p_refs, got_refs, ssem, rsem)

        _attn_bias(sl_ref, bias_scr)
        lane_q = lax.broadcasted_iota(jnp.int32, (BLK, BLK), 1) < 64
        lane_k = lax.broadcasted_iota(jnp.int32, (2 * BLK, BLK), 1) < 64
        dk_ref[...] = jnp.zeros_like(dk_ref)
        dv_ref[...] = jnp.zeros_like(dv_ref)

        def branch(n, r, di):
            d = DILATIONS[di]
            rows = _rows(n, r, d)
            prev = _rows(jnp.maximum(n - 1, 0), r, d)
            pen = _first_block_penalty(n)
            q1, d1, l1, e1 = q_ref[rows, :] * ATTN_SCALE, do_ref[rows, :], l_ref[rows, :], e_ref[rows, :]
            kk = jnp.concatenate([k_ref[prev, :], k_ref[rows, :]], axis=0)
            kkb = kk.astype(BF16)
            vvb = jnp.concatenate([v_ref[prev, :], v_ref[rows, :]], axis=0).astype(BF16)
            dq2 = jnp.zeros((BLK, BLK), F32)
            dkk = jnp.zeros((2 * BLK, BLK), F32)
            dvv = jnp.zeros((2 * BLK, BLK), F32)
            for hh in range(2):
                mq = lane_q if hh == 0 else ~lane_q
                mk = lane_k if hh == 0 else ~lane_k
                qm = jnp.where(mq, q1, 0.0).astype(BF16)
                dm = jnp.where(mq, d1, 0.0).astype(BF16)
                sc = _dot(qm, kkb, NT) + bias_scr[2 * di + hh] + pen
                pr = jnp.exp(sc - _head_col(l1, mq))
                ds = (pr * (_dot(dm, vvb, NT) - _head_col(e1, mq))).astype(BF16)
                km = jnp.where(mk, kk, 0.0).astype(BF16)
                dq2 = dq2 + _dot(ds, km, NN)
                dkk = dkk + _dot(ds, qm, TN)
                dvv = dvv + _dot(pr.astype(BF16), dm, TN)
            return rows, prev, dq2 * ATTN_SCALE, dkk, dvv

        for di, d in enumerate(DILATIONS):
            for trips, blocks in _attn_groups(s, d):
                def trip(t, carry, di=di, blocks=blocks, first=(di == 0)):
                    done = [branch(n, r, di) for n, r in blocks(t)]
                    for rows, prev, dq2, dkk, dvv in done:
                        dq_ref[rows, :] = dq2 if first else dq_ref[rows, :] + dq2
                        dk_ref[prev, :] = dk_ref[prev, :] + dkk[:BLK]
                        dk_ref[rows, :] = dk_ref[rows, :] + dkk[BLK:]
                        dv_ref[prev, :] = dv_ref[prev, :] + dvv[:BLK]
                        dv_ref[rows, :] = dv_ref[rows, :] + dvv[BLK:]
                    return carry

                lax.fori_loop(0, trips, trip, 0)

        for o16, acc in zip(out16, (dq_ref, dk_ref, dv_ref)):
            o16[...] = acc[...].astype(BF16)

        @pl.when(pair == N_PAIRS - 1)
        def _():
            _rs_chips_finish(p_refs, got_refs, ssem, rsem)

    cb = lambda base: pl.BlockSpec((s, BLK), lambda p, base=base: (0, base + p))
    out = pl.BlockSpec((s, BLK), lambda p: (0, p))
    rs_shape, rs_sems = _rs_chips_shapes(psums)
    return pl.pallas_call(
        body, name="attn_bwd", grid=(N_PAIRS,),
        in_specs=[pl.BlockSpec((None, 8, 2 * BLK), lambda p: (p, 0, 0)), cb(0), cb(N_PAIRS), cb(2 * N_PAIRS),
                  out, out, out] + [ANY] * nk,
        out_specs=[out] * 3 + [ANY] * nk, out_shape=[_sds((s, ATTN_W), BF16)] * 3 + rs_shape,
        scratch_shapes=[pltpu.VMEM((s, BLK), F32)] * 3 + [pltpu.VMEM((2 * len(DILATIONS), BLK, 2 * BLK), F32)] + rs_sems,
        compiler_params=_cp(("arbitrary",)),
    )(_slopes(), proj, proj, proj, do, lse, delta, *psums)


def _lower_bound(lbl):
    return 1.0 / (1.0 + jnp.exp(lbl[1:2, :] - lbl[0:1, :]))


def _hi(a):
    bits = lax.bitcast_convert_type(a, jnp.uint32) & jnp.uint32(0xFFFF0000)
    return lax.bitcast_convert_type(bits, F32)


def _dot3(a, b, contract):
    ah, bh = _hi(a), _hi(b)
    al, bl = (a - ah).astype(BF16), (b - bh).astype(BF16)
    ah, bh = ah.astype(BF16), bh.astype(BF16)
    return _dot(ah, bh, contract) + (_dot(ah, bl, contract) + _dot(al, bh, contract))


def _cumsum_rows(tri, g):
    g1 = _hi(g)
    r1 = g - g1
    g2 = _hi(r1)
    g3 = r1 - g2
    return _dot(tri, g1.astype(BF16), NN) + (_dot(tri, g2.astype(BF16), NN) + _dot(tri, g3.astype(BF16), NN))


def _heads(fn):
    return jnp.concatenate([fn(slice(h * BLK, (h + 1) * BLK)) for h in range(HGRN_W // BLK)], axis=1)


def _head_mean(t):
    return _heads(lambda hs: jnp.broadcast_to(jnp.mean(t[:, hs], axis=1, keepdims=True), (t.shape[0], BLK)))


def _hgrn_chunk(q_ref, f_ref, i_ref, sl, lb, tri):
    qp = q_ref[sl, :]
    sq = _sigmoid(qp)
    qf = qp * sq
    sg = _sigmoid(f_ref[sl, :])
    f = lb + (1.0 - lb) * sg
    kf = 1.0 - f
    v = i_ref[sl, :]
    b = _cumsum_rows(tri, jnp.log(f))
    bm = b[CHUNK // 2:CHUNK // 2 + 1, :]
    bl = b[CHUNK - 1:CHUNK, :]
    qt = qf * jnp.exp(b - bm)
    kt = kf * jnp.exp(bm - b)
    return qp, sq, qf, sg, f, kf, v, b, bm, bl, qt, kt


def _hgrn_specs(tb, block):
    first = 3 * ATTN_W // HGRN_W
    return [pl.BlockSpec((tb, HGRN_W), lambda i, k=k: (block(i), first + k)) for k in range(4)]


def _hgrn_fwd(proj, lb_logits, out_gain):
    s = proj.shape[0]
    tb = min(HGRN_TB, s)
    nb, cpb, nc = s // tb, tb // CHUNK, s // CHUNK

    def body(q_ref, f_ref, i_ref, g_ref, lbl_ref, gain_ref, o_ref, rec_ref, st_ref, st_scr):
        step = pl.program_id(0)

        @pl.when(step == 0)
        def _():
            st_scr[...] = jnp.zeros_like(st_scr)

        lb = _lower_bound(lbl_ref[...])
        r64 = lax.broadcasted_iota(jnp.int32, (CHUNK, CHUNK), 0)
        c64 = lax.broadcasted_iota(jnp.int32, (CHUNK, CHUNK), 1)
        tril = r64 >= c64
        tri = tril.astype(BF16)
        st = st_scr[...]
        for cc in range(cpb):
            sl = slice(cc * CHUNK, (cc + 1) * CHUNK)
            _, _, qf, _, _, kf, v, b, _, bl, qt, kt = _hgrn_chunk(q_ref, f_ref, i_ref, sl, lb, tri)
            qe = (qf * jnp.exp(b)).astype(BF16)
            kh = (kf * jnp.exp(bl - b)).astype(BF16)
            qtb, ktb, vb, stb = qt.astype(BF16), kt.astype(BF16), v.astype(BF16), st.astype(BF16)

            def out_h(hs):
                a = jnp.where(tril, _dot(qtb[:, hs], ktb[:, hs], NT), 0.0).astype(BF16)
                return _dot(qe[:, hs], stb[:, hs], NT) + _dot(a, vb[:, hs], NN)

            o_ref[sl, :] = _heads(out_h)
            st_ref[cc] = stb
            st = st * jnp.exp(bl) + _heads(lambda hs: _dot(vb[:, hs], kh[:, hs], TN))
        st_scr[...] = st
        o = o_ref[...]
        gate = g_ref[...]
        rec_ref[...] = (o * lax.rsqrt(_head_mean(o * o) + RMS_EPS) * gain_ref[...] * (gate * _sigmoid(gate))).astype(BF16)

    row = pl.BlockSpec((tb, HGRN_W), lambda i: (i, 0))
    return pl.pallas_call(
        body, name="hgrn_fwd", grid=(nb,),
        in_specs=_hgrn_specs(tb, lambda i: i) + [pl.BlockSpec((2, HGRN_W), lambda i: (0, 0)),
                                                 pl.BlockSpec((1, HGRN_W), lambda i: (0, 0))],
        out_specs=[row, row, pl.BlockSpec((cpb, BLK, HGRN_W), lambda i: (i, 0, 0))],
        out_shape=[_sds((s, HGRN_W), F32), _sds((s, HGRN_W), BF16), _sds((nc, BLK, HGRN_W), BF16)],
        scratch_shapes=[pltpu.VMEM((BLK, HGRN_W), F32)],
        compiler_params=_cp(("arbitrary",)),
    )(proj, proj, proj, proj, lb_logits, out_gain)


def _hgrn_bwd(proj, o_pre, states, drec, lb_logits, out_gain):
    s = proj.shape[0]
    tb = min(HGRN_TB, s)
    nb, cpb, nc = s // tb, tb // CHUNK, s // CHUNK

    def body(q_ref, f_ref, i_ref, g_ref, o_ref, st_ref, stn_ref, dy_ref, lbl_ref, gain_ref,
             dq_ref, df_ref, di_ref, dg_ref, dgain_ref, dlbl_ref, do_scr, dst_scr, dlb_scr):
        step = pl.program_id(0)

        @pl.when(step == 0)
        def _():
            dst_scr[...] = jnp.zeros_like(dst_scr)
            dlb_scr[...] = jnp.zeros_like(dlb_scr)
            dgain_ref[...] = jnp.zeros_like(dgain_ref)

        lb = _lower_bound(lbl_ref[...])
        gain = gain_ref[...]
        o = o_ref[...]
        r = lax.rsqrt(_head_mean(o * o) + RMS_EPS)
        nrm = o * r
        gate = g_ref[...]
        sgt = _sigmoid(gate)
        dy = dy_ref[...]
        dg_ref[...] = (dy * nrm * gain * (sgt * (1.0 + gate * (1.0 - sgt)))).astype(BF16)
        dng = dy * (gate * sgt)
        dgain_ref[...] += _colsum(dng * nrm)
        dn = dng * gain
        do_scr[...] = r * (dn - nrm * _head_mean(dn * nrm))

        r64 = lax.broadcasted_iota(jnp.int32, (CHUNK, CHUNK), 0)
        c64 = lax.broadcasted_iota(jnp.int32, (CHUNK, CHUNK), 1)
        tril = r64 >= c64
        tri = tril.astype(BF16)
        triu = (r64 <= c64).astype(BF16)
        dst = dst_scr[...]
        dlb = dlb_scr[...]
        for cc in reversed(range(cpb)):
            sl = slice(cc * CHUNK, (cc + 1) * CHUNK)
            qp, sq, qf, sg, f, kf, v, b, bm, bl, qt, kt = _hgrn_chunk(q_ref, f_ref, i_ref, sl, lb, tri)
            stf = st_ref[cc].astype(F32)
            st_end = (st_ref[cc + 1] if cc + 1 < cpb else stn_ref[0]).astype(F32)
            csum = jnp.sum(st_end * dst, axis=0, keepdims=True)
            doc = do_scr[sl, :]
            dob, dstb = doc.astype(BF16), dst.astype(BF16)
            eb = jnp.exp(b)
            qe = (qf * eb).astype(BF16)
            kh = (kf * jnp.exp(bl - b)).astype(BF16)
            qtb, ktb = qt.astype(BF16), kt.astype(BF16)
            parts = []
            for h in range(HGRN_W // BLK):
                hs = slice(h * BLK, (h + 1) * BLK)
                da = jnp.where(tril, _dot3(doc[:, hs], v[:, hs], NT), 0.0)
                a = jnp.where(tril, _dot(qtb[:, hs], ktb[:, hs], NT), 0.0).astype(BF16)
                parts.append((
                    _dot3(da, kt[:, hs], NN), _dot3(doc[:, hs], stf[:, hs], NN),
                    _dot3(da, qt[:, hs], TN), _dot3(v[:, hs], dst[:, hs], NN),
                    _dot(a, dob[:, hs], TN) + _dot(kh[:, hs], dstb[:, hs], NT),
                    _dot(dob[:, hs], qe[:, hs], TN)))
            dqt, dqi, dkt, dks, dv, upd = (jnp.concatenate([p[n] for p in parts], axis=1) for n in range(6))
            dqf = dqt * jnp.exp(b - bm) + eb * dqi
            dkf = dkt * jnp.exp(bm - b) + jnp.exp(bl - b) * dks
            gq = qf * dqf - kf * dkf
            dlogf = csum + _cumsum_rows(triu, gq)
            dfv = dlogf / f - dkf
            dq_ref[sl, :] = (dqf * (sq * (1.0 + qp * (1.0 - sq)))).astype(BF16)
            df_ref[sl, :] = (dfv * (1.0 - lb) * sg * (1.0 - sg)).astype(BF16)
            di_ref[sl, :] = dv.astype(BF16)
            dst = dst * jnp.exp(bl) + upd
            dlb = dlb + _colsum(dfv * (1.0 - sg))
        dst_scr[...] = dst
        dlb_scr[...] = dlb

        @pl.when(step == nb - 1)
        def _():
            t = dlb * lb * (1.0 - lb)
            dlbl_ref[...] = jnp.concatenate([t, -t], axis=0)

    rev = lambda i: nb - 1 - i
    row = pl.BlockSpec((tb, HGRN_W), lambda i: (rev(i), 0))
    res = pl.pallas_call(
        body, name="hgrn_bwd", grid=(nb,),
        in_specs=_hgrn_specs(tb, rev) + [
            row, pl.BlockSpec((cpb, BLK, HGRN_W), lambda i: (rev(i), 0, 0)),
            pl.BlockSpec((1, BLK, HGRN_W), lambda i: (jnp.minimum((rev(i) + 1) * cpb, nc - 1), 0, 0)),
            row, pl.BlockSpec((2, HGRN_W), lambda i: (0, 0)), pl.BlockSpec((1, HGRN_W), lambda i: (0, 0))],
        out_specs=[row, row, row, row, pl.BlockSpec((1, HGRN_W), lambda i: (0, 0)),
                   pl.BlockSpec((2, HGRN_W), lambda i: (0, 0))],
        out_shape=[_sds((s, HGRN_W), BF16)] * 4 + [_sds((1, HGRN_W), F32), _sds((2, HGRN_W), F32)],
        scratch_shapes=[pltpu.VMEM((tb, HGRN_W), F32), pltpu.VMEM((BLK, HGRN_W), F32), pltpu.VMEM((1, HGRN_W), F32)],
        compiler_params=_cp(("arbitrary",)),
    )(proj, proj, proj, proj, o_pre, states, states, drec, lb_logits, out_gain)
    return res


def _place():
    return lax.axis_index("x"), lax.axis_index("y"), lax.axis_index("c")


def _flip(x, y, ox, oy):
    return (1 - x if ox else x), (1 - y if oy else y)


def _half(rows, cc):
    return pl.ds(cc * (rows // 2), rows // 2)


def _remote(src, dst, ssem, rsem, to):
    return pltpu.make_async_remote_copy(src_ref=src, dst_ref=dst, send_sem=ssem, recv_sem=rsem,
                                        device_id=to, device_id_type=MESH)


def _ag_chip_copies(ins, outs, ssem, rsem):
    x, y, c = _place()
    j = 2 * x + y
    cps = []
    for k in range(len(ins)):
        rows = ins[k].shape[0]
        for idx, (ox, oy) in enumerate(FLIPS):
            px, py = _flip(x, y, ox, oy)
            cps.append(_remote(ins[k].at[_half(rows, c)], outs[k].at[j, _half(rows, c)],
                               ssem.at[k, idx], rsem.at[k, idx], (px, py, c)))
    return cps


def _ag_start(ins, outs, ssem, rsem):
    for cp in _ag_chip_copies(ins, outs, ssem, rsem):
        cp.start()


def _ag_finish(ins, outs, ssem, rsem):
    x, y, c = _place()
    sib = (x, y, 1 - c)
    passed = []
    for k in range(len(ins)):
        rows = ins[k].shape[0]
        for idx, (ox, oy) in enumerate(FLIPS):
            px, py = _flip(x, y, ox, oy)
            blk = outs[k].at[2 * px + py, _half(rows, c)]
            _remote(blk, blk, ssem.at[k, idx], rsem.at[k, idx], (px, py, c)).wait_recv()
            cp = _remote(blk, blk, ssem.at[k, 3 + idx], rsem.at[k, 3 + idx], sib)
            cp.start()
            passed.append(cp)
    for k in range(len(ins)):
        rows = ins[k].shape[0]
        for idx, (ox, oy) in enumerate(FLIPS):
            px, py = _flip(x, y, ox, oy)
            blk = outs[k].at[2 * px + py, _half(rows, 1 - c)]
            _remote(blk, blk, ssem.at[k, 3 + idx], rsem.at[k, 3 + idx], sib).wait_recv()
    for cp in _ag_chip_copies(ins, outs, ssem, rsem) + passed:
        cp.wait_send()


def _ag_shapes(shards):
    nk = len(shards)
    return ([_sds((N_CHIPS,) + tuple(w.shape), w.dtype) for w in shards],
            [pltpu.SemaphoreType.DMA((nk, 6)), pltpu.SemaphoreType.DMA((nk, 6))])


def _with_own(gathered, shard, j):
    return lax.dynamic_update_index_in_dim(gathered, shard, j, 0)


def _rs_pair_copies(ins, outs, ssem, rsem):
    x, y, c = _place()
    return [_remote(ins[k].at[:, _half(ins[k].shape[1], 1 - c)], outs[k], ssem.at[k], rsem.at[k], (x, y, 1 - c))
            for k in range(len(ins))]


def _rs_pair_start(ins, outs, ssem, rsem):
    for cp in _rs_pair_copies(ins, outs, ssem, rsem):
        cp.start()


def _rs_pair_finish(ins, outs, ssem, rsem):
    for cp in _rs_pair_copies(ins, outs, ssem, rsem):
        cp.wait()


def _rs_pair_exchange(grads):
    nk = len(grads)
    return (grads, [_sds((N_CHIPS, g.shape[1] // 2, g.shape[2]), g.dtype) for g in grads],
            [pltpu.SemaphoreType.DMA((nk,)), pltpu.SemaphoreType.DMA((nk,))], _rs_pair_start, _rs_pair_finish)


def _rs_pair(name, grads):
    nk = len(grads)
    ins, out_shape, sems, start, finish = _rs_pair_exchange(grads)

    def body(*refs):
        start(refs[:nk], refs[nk:2 * nk], *refs[2 * nk:])
        finish(refs[:nk], refs[nk:2 * nk], *refs[2 * nk:])

    return pl.pallas_call(body, name=name, in_specs=[ANY] * nk, out_specs=[ANY] * nk, out_shape=out_shape,
                          scratch_shapes=sems)(*ins)


def _rs_chip_copies(ins, outs, ssem, rsem):
    x, y, c = _place()
    cps = []
    for k in range(len(ins)):
        for idx, (ox, oy) in enumerate(FLIPS):
            px, py = _flip(x, y, ox, oy)
            cps.append(_remote(ins[k].at[2 * px + py], outs[k].at[idx], ssem.at[k, idx], rsem.at[k, idx], (px, py, c)))
    return cps


def _rs_chips_start(ins, outs, ssem, rsem):
    for cp in _rs_chip_copies(ins, outs, ssem, rsem):
        cp.start()


def _rs_chips_finish(ins, outs, ssem, rsem):
    for cp in _rs_chip_copies(ins, outs, ssem, rsem):
        cp.wait()


def _rs_chips_shapes(psums):
    nk = len(psums)
    return ([_sds((3,) + tuple(p.shape[1:]), p.dtype) for p in psums],
            [pltpu.SemaphoreType.DMA((nk, 3)), pltpu.SemaphoreType.DMA((nk, 3))])


def _rs_share(fulls):
    nk = len(fulls)

    def body(*refs):
        ins, outs = refs[:nk], refs[nk:2 * nk]
        ssem, rsem = refs[2 * nk:]
        x, y, c = _place()
        cps = []
        for k in range(nk):
            rows = fulls[k].shape[0]
            cp = _remote(ins[k].at[_half(rows, c)], outs[k].at[_half(rows, c)], ssem.at[k], rsem.at[k], (x, y, 1 - c))
            cp.start()
            cps.append(cp)
        for k, cp in enumerate(cps):
            rows = fulls[k].shape[0]
            cp.wait_send()
            theirs = outs[k].at[_half(rows, 1 - c)]
            _remote(theirs, theirs, ssem.at[k], rsem.at[k], (x, y, 1 - c)).wait_recv()

    return pl.pallas_call(
        body, name="rs_share", in_specs=[ANY] * nk, out_specs=[ANY] * nk,
        out_shape=[_sds(f.shape, f.dtype) for f in fulls], input_output_aliases={k: k for k in range(nk)},
        scratch_shapes=[pltpu.SemaphoreType.DMA((nk,)), pltpu.SemaphoreType.DMA((nk,))],
    )(*fulls)


def _allreduce_small(v):
    ndev = 8

    def body(in_ref, out_ref, buf, ssem, rsem):
        x, y, c = _place()
        me = 4 * x + 2 * y + c
        buf[me] = in_ref[...]
        cps = []
        for k in range(1, ndev):
            ox, oy, oc = (k >> 2) & 1, (k >> 1) & 1, k & 1
            px, py = _flip(x, y, ox, oy)
            pc = 1 - c if oc else c
            cp = pltpu.make_async_remote_copy(src_ref=in_ref, dst_ref=buf.at[me], send_sem=ssem.at[k - 1],
                                              recv_sem=rsem.at[k - 1], device_id=(px, py, pc), device_id_type=MESH)
            cp.start()
            cps.append((cp, 4 * px + 2 * py + pc, (px, py, pc)))
        for k, (cp, src, peer) in enumerate(cps):
            cp.wait_send()
            pltpu.make_async_remote_copy(src_ref=in_ref, dst_ref=buf.at[src], send_sem=ssem.at[k],
                                         recv_sem=rsem.at[k], device_id=peer, device_id_type=MESH).wait_recv()
        acc = buf[0]
        for i in range(1, ndev):
            acc = acc + buf[i]
        out_ref[...] = acc

    return pl.pallas_call(
        body, name="allreduce_small",
        in_specs=[pl.BlockSpec(memory_space=pltpu.VMEM)], out_specs=pl.BlockSpec(memory_space=pltpu.VMEM),
        out_shape=_sds(v.shape, v.dtype),
        scratch_shapes=[pltpu.VMEM((ndev,) + v.shape, v.dtype), pltpu.SemaphoreType.DMA((ndev - 1,)),
                        pltpu.SemaphoreType.DMA((ndev - 1,))],
    )(v)


def _rs_sum1(name, g, recv, c_idx):
    _, r, cdim = g.shape
    hr = r // 2
    tr = min(hr, 256)
    nr = hr // tr

    def body(c_ref, g_ref, r_ref, o32_ref, o16_ref):
        v = g_ref[...] + r_ref[...].astype(F32)
        o32_ref[...] = v
        o16_ref[...] = v.astype(BF16)

    spec = pl.BlockSpec((None, tr, cdim), lambda j, i, c_ref: (j, i, 0))
    return pl.pallas_call(
        body, name=name,
        grid_spec=pltpu.PrefetchScalarGridSpec(
            num_scalar_prefetch=1, grid=(N_CHIPS, nr),
            in_specs=[pl.BlockSpec((None, tr, cdim), lambda j, i, c_ref: (j, c_ref[0] * nr + i, 0)), spec],
            out_specs=[spec, spec]),
        out_shape=[_sds((N_CHIPS, hr, cdim), F32), _sds((N_CHIPS, hr, cdim), BF16)],
        compiler_params=_cp(("parallel", "parallel")),
    )(c_idx, g, recv)


def _rs_sum2(name, p32, recv, jc_idx):
    _, hr, cdim = p32.shape
    tr = min(hr, 256)
    nr = hr // tr

    def body(jc_ref, p_ref, r_ref, o_ref):
        o_ref[...] = ((p_ref[...] + r_ref[0].astype(F32)) + r_ref[1].astype(F32)) + r_ref[2].astype(F32)

    return pl.pallas_call(
        body, name=name,
        grid_spec=pltpu.PrefetchScalarGridSpec(
            num_scalar_prefetch=1, grid=(nr,),
            in_specs=[pl.BlockSpec((None, tr, cdim), lambda i, jc: (jc[0], i, 0)),
                      pl.BlockSpec((3, tr, cdim), lambda i, jc: (0, i, 0))],
            out_specs=pl.BlockSpec((tr, cdim), lambda i, jc: (jc[1] * nr + i, 0))),
        out_shape=_sds((2 * hr, cdim), F32),
        compiler_params=_cp(("parallel",)),
    )(jc_idx, p32, recv)


def _adamw(name, w, g, m, v):
    r, cdim = w.shape
    tr = min(r, 256)
    c1 = 1.0 - ADAM_B1 ** ADAM_STEP
    c2 = 1.0 - ADAM_B2 ** ADAM_STEP

    def body(w_ref, g_ref, m_ref, v_ref, d_ref, nm_ref, nv_ref):
        gv = g_ref[...]
        nm = ADAM_B1 * m_ref[...] + (1.0 - ADAM_B1) * gv
        nv = ADAM_B2 * v_ref[...] + (1.0 - ADAM_B2) * (gv * gv)
        d_ref[...] = -ADAM_LR * ((nm / c1) / (jnp.sqrt(nv / c2) + ADAM_EPS) + ADAM_WD * w_ref[...])
        nm_ref[...] = nm
        nv_ref[...] = nv

    spec = pl.BlockSpec((tr, cdim), lambda i: (i, 0))
    return pl.pallas_call(
        body, name=name, grid=(r // tr,), in_specs=[spec] * 4, out_specs=[spec] * 3,
        out_shape=[_sds((r, cdim), F32)] * 3, compiler_params=_cp(("parallel",)),
    )(w, g, m, v)


def _pack_small(mix_pre, attn_out, lb_logits, hgrn_out, mix_post, mlp_pre, mlp_post, extra=None):
    spare = jnp.zeros((1, D_MODEL), F32)
    rows = [mix_pre, jnp.concatenate([attn_out, hgrn_out], axis=1),
            jnp.concatenate([lb_logits[0:1], lb_logits[1:2]], axis=1), mix_post, mlp_pre, mlp_post,
            spare if extra is None else extra, spare]
    return jnp.concatenate(rows, axis=0)


def _unpack_small(p):
    return (p[0:1], p[1:2, :ATTN_W], jnp.concatenate([p[2:3, :HGRN_W], p[2:3, HGRN_W:]], axis=0),
            p[1:2, ATTN_W:], p[3:4], p[4:5], p[5:6])


def kernel(x, mix_pre_norm, w_in, attn_out_norm, hgrn_lb_logits, hgrn_out_norm, w_out, mix_post_norm, mlp_pre_norm, w_ff1, w_ff2, mlp_post_norm, loss_target, m_mix_pre_norm, m_w_in, m_attn_out_norm, m_hgrn_lb_logits, m_hgrn_out_norm, m_w_out, m_mix_post_norm, m_mlp_pre_norm, m_w_ff1, m_w_ff2, m_mlp_post_norm, v_mix_pre_norm, v_w_in, v_attn_out_norm, v_hgrn_lb_logits, v_hgrn_out_norm, v_w_out, v_mix_post_norm, v_mlp_pre_norm, v_w_ff1, v_w_ff2, v_mlp_post_norm):
    s = x.shape[1]
    xs = x.reshape(s, D_MODEL)
    tgt = loss_target.reshape(s, D_MODEL)
    cx, cy, cc = _place()
    chip = 2 * cx + cy
    c_idx = jnp.reshape(cc, (1,)).astype(jnp.int32)
    jc_idx = jnp.stack([chip, cc]).astype(jnp.int32)

    big_w = [w_in[0], w_out[0], w_ff1[0], w_ff2[0]]
    big_m = [m_w_in[0], m_w_out[0], m_w_ff1[0], m_w_ff2[0]]
    big_v = [v_w_in[0], v_w_out[0], v_w_ff1[0], v_w_ff2[0]]
    shards = [w.astype(BF16) for w in big_w]

    h, wg_in = _rows_call("norm_in", lambda xv, g: ((xv * _rstd(xv) * g),),
                          [(xs, _row(D_MODEL)), (mix_pre_norm, "full")], [(D_MODEL, BF16, "row")], s,
                          exchange=(shards[:1], *_ag_shapes(shards[:1]), _ag_start, _ag_finish))
    wg_in = _with_own(wg_in, shards[0], chip)
    (proj,) = _mm_cols("mm_proj", h, wg_in, NN, [F32])
    hg_o, rec, states = _hgrn_fwd(proj, hgrn_lb_logits, hgrn_out_norm)
    attn_o, attn_lse, wg_out, wg_1, wg_2 = _attn_fwd(proj, shards[1:])
    wg_out, wg_1, wg_2 = (_with_own(g, w, chip) for g, w in zip((wg_out, wg_1, wg_2), shards[1:]))
    (attn_n,) = _rows_call("attn_norm", lambda o, gain: (o * _rstd(o) * gain,),
                           [(attn_o, _row(ATTN_W)), (attn_out_norm, "full")], [(ATTN_W, BF16, "row")], s)
    cat = jnp.concatenate([attn_n, rec], axis=1)

    def post1(mv, xv, g_post, g_pre2):
        x1 = xv + mv * _rstd(mv) * g_post
        return mv, x1, x1 * _rstd(x1) * g_pre2

    mixed, x1, h2 = _rows_call(
        "mm_mixed", post1, [(xs, _row(D_MODEL)), (mix_post_norm, "full"), (mlp_pre_norm, "full")],
        [(D_MODEL, F32, "row"), (D_MODEL, F32, "row"), (D_MODEL, BF16, "row")], s,
        matmul=(cat, wg_out.reshape(D_MODEL, D_MODEL)))

    def sq_relu(u):
        r = jnp.maximum(u, 0.0)
        return r * r, r

    act, ru = _mm_cols("mm_ff1", h2, wg_1, NN, [BF16, BF16], epi=sq_relu)

    def post2(fv, x1v, tv, g):
        y = x1v + fv * _rstd(fv) * g
        dy = (y - tv) * (1.0 / D_MODEL)
        err = y - tv
        loss = 0.5 * jnp.sum(jnp.mean(err * err, axis=-1, keepdims=True), axis=0, keepdims=True)
        dff, dgc = _norm_bwd(fv, g, dy)
        return dy, dff, _colsum(dgc), jnp.broadcast_to(loss, (1, BLK))

    dy, dff, g_mlp_post, loss_part = _rows_call(
        "mm_ff2", post2, [(x1, _row(D_MODEL)), (tgt, _row(D_MODEL)), (mlp_post_norm, "full")],
        [(D_MODEL, F32, "row"), (D_MODEL, BF16, "row"), (D_MODEL, F32, "acc"), (BLK, F32, "acc")], s,
        matmul=(act, wg_2.reshape(D_FF, D_MODEL)))

    (du,) = _mm_cols("mm_du", dff, wg_2, NT, [BF16], epi=lambda acc, r: (acc * (2.0 * r.astype(F32)),),
                     extras=(ru,))
    gw_2 = _mm_wgrad("mm_gw2", act, dff, True)
    gw_1 = _mm_wgrad("mm_gw1", h2, du, False)

    def bwd_mid(dh2v, dyv, x1v, mv, g_pre2, g_post):
        d1, gc1 = _norm_bwd(x1v, g_pre2, dh2v)
        dx1 = dyv + d1
        dm, gc2 = _norm_bwd(mv, g_post, dx1)
        return dx1, dm, _colsum(gc1), _colsum(gc2)

    dx1, dmixed, g_mlp_pre, g_mix_post, *from_pair = _rows_call(
        "mm_dh2", bwd_mid, [(dy, _row(D_MODEL)), (x1, _row(D_MODEL)), (mixed, _row(D_MODEL)),
                            (mlp_pre_norm, "full"), (mix_post_norm, "full")],
        [(D_MODEL, F32, "row"), (D_MODEL, BF16, "row"), (D_MODEL, F32, "acc"), (D_MODEL, F32, "acc")], s,
        matmul=(du, wg_1), exchange=_rs_pair_exchange([gw_1[1], gw_2[1]]))

    def attn_norm_bwd(dc, o, gain):
        do, gc = _norm_bwd(o, gain, dc[:, :ATTN_W])
        t = do * o
        lane = lax.broadcasted_iota(jnp.int32, (t.shape[0], BLK), 1) < 64
        parts = []
        for p in range(ATTN_W // BLK):
            tp = t[:, p * BLK:(p + 1) * BLK]
            sa = jnp.sum(jnp.where(lane, tp, 0.0), axis=1, keepdims=True)
            sb = jnp.sum(jnp.where(lane, 0.0, tp), axis=1, keepdims=True)
            parts.append(jnp.where(lane, sa, sb))
        return do, jnp.concatenate(parts, axis=1), dc[:, ATTN_W:], _colsum(gc)

    do_attn, delta, drec, g_attn_out = _rows_call(
        "mm_dcat", attn_norm_bwd, [(attn_o, _row(ATTN_W)), (attn_out_norm, "full")],
        [(ATTN_W, F32, "row"), (ATTN_W, F32, "row"), (HGRN_W, F32, "row"), (ATTN_W, F32, "acc")], s,
        matmul=(dmixed, wg_out.reshape(1, D_MODEL, D_MODEL)))
    gw_out = _mm_wgrad("mm_gwout", cat, dmixed, True)
    names = ["out", "ff1", "ff2", "in"]
    ready = [gw_out, gw_1, gw_2]
    from_pair = list(_rs_pair("rs_pair_out", [gw_out[1]])) + from_pair
    pair = [_rs_sum1(f"rs_sum1_{n}", g[0], r, c_idx) for n, g, r in zip(names, ready, from_pair)]

    dq, dk, dv, *from_chips = _attn_bwd(proj, do_attn, attn_lse, delta, [p[1] for p in pair])
    dhq, dhf, dhi, dhg, g_hgrn_out, g_lb = _hgrn_bwd(proj, hg_o, states, drec, hgrn_lb_logits, hgrn_out_norm)

    dproj = jnp.concatenate([dq, dk, dv, dhq, dhf, dhi, dhg], axis=1)
    gw_in = _mm_wgrad("mm_gwin", h, dproj, False)
    (from_pair_in,) = _rs_pair("rs_pair_in", [gw_in[1]])
    pair.append(_rs_sum1("rs_sum1_in", gw_in[0], from_pair_in, c_idx))
    rs_shape, rs_sems = _rs_chips_shapes([pair[3][1]])

    def bwd_in(dhv, dx1v, xv, g):
        d0, gc = _norm_bwd(xv, g, dhv)
        return dx1v + d0, _colsum(gc)

    grad_x, g_mix_pre, from_chips_in = _rows_call(
        "mm_dh", bwd_in, [(dx1, _row(D_MODEL)), (xs, _row(D_MODEL)), (mix_pre_norm, "full")],
        [(D_MODEL, F32, "row"), (D_MODEL, F32, "acc")], s, matmul=(dproj, wg_in),
        exchange=([pair[3][1]], rs_shape, rs_sems, _rs_chips_start, _rs_chips_finish))
    from_chips.append(from_chips_in)

    loss_row = jnp.pad(loss_part, ((0, 0), (0, D_MODEL - BLK)))
    small_g = _allreduce_small(_pack_small(g_mix_pre, g_attn_out, g_lb, g_hgrn_out, g_mix_post, g_mlp_pre, g_mlp_post,
                                           extra=loss_row))
    loss = small_g[6, 0]

    reduced = [_rs_sum2(f"rs_sum2_{n}", p[0], r, jc_idx) for n, p, r in zip(names, pair, from_chips)]
    g_wout, g_w1, g_w2, g_win = _rs_share(reduced)
    full = [g_win, g_wout, g_w1, g_w2]

    upd = [_adamw(f"adamw_{n}", w, g, m, v) for n, w, g, m, v in zip(("in", "out", "ff1", "ff2"), big_w, full, big_m, big_v)]
    small_w = _pack_small(mix_pre_norm, attn_out_norm, hgrn_lb_logits, hgrn_out_norm, mix_post_norm, mlp_pre_norm,
                          mlp_post_norm)
    small_m = _pack_small(m_mix_pre_norm, m_attn_out_norm, m_hgrn_lb_logits, m_hgrn_out_norm, m_mix_post_norm,
                          m_mlp_pre_norm, m_mlp_post_norm)
    small_v = _pack_small(v_mix_pre_norm, v_attn_out_norm, v_hgrn_lb_logits, v_hgrn_out_norm, v_mix_post_norm,
                          v_mlp_pre_norm, v_mlp_post_norm)
    small_upd = _adamw("adamw_small", small_w, small_g, small_m, small_v)

    def assemble(small, big):
        sm = _unpack_small(small)
        return (sm[0], big[0][None], sm[1], sm[2], sm[3], big[1][None], sm[4], sm[5], big[2][None], big[3][None], sm[6])

    g_out = assemble(small_g, full)
    d_out = assemble(small_upd[0], [u[0] for u in upd])
    m_out = assemble(small_upd[1], [u[1] for u in upd])
    v_out = assemble(small_upd[2], [u[2] for u in upd])
    return (loss, grad_x.reshape(x.shape), *g_out, *d_out, *m_out, *v_out)
```

```python
import numpy as np
import jax
import jax.numpy as jnp
from jax import lax
from jax.experimental import pallas as pl
from jax.experimental.pallas import tpu as pltpu

F32 = jnp.float32
BF16 = jnp.bfloat16
MESH = pl.DeviceIdType.MESH
ANY = pl.BlockSpec(memory_space=pl.ANY)

RMS_EPS = 1e-6
D_MODEL = 1024
ATTN_W = 512
HGRN_W = 512
PROJ_W = 3584
D_FF = 4096
N_CHIPS = 4
BLK = 128
CHUNK = 64
HGRN_TB = 512
ATTN_GROUP = 8
DILATIONS = (1, 4, 16)
ATTN_SCALE = 0.125
ROW_TILE = 512
MM_TILE = 1024
VMEM_LIMIT = 48 * 2 ** 20
FLIPS = ((1, 0), (0, 1), (1, 1))

ADAM_LR, ADAM_B1, ADAM_B2, ADAM_EPS, ADAM_WD, ADAM_STEP = 0.001, 0.9, 0.999, 1e-08, 0.01, 10


def _cp(sem=None):
    return pltpu.CompilerParams(dimension_semantics=sem, vmem_limit_bytes=VMEM_LIMIT)


def _sigmoid(v):
    return 1.0 / (1.0 + jnp.exp(-v))


def _dot(a, b, contract, precision=None):
    return lax.dot_general(a, b, (contract, ((), ())), preferred_element_type=F32, precision=precision)


NN = ((1,), (0,))
NT = ((1,), (1,))
TN = ((0,), (0,))


def _sds(shape, dtype):
    return jax.ShapeDtypeStruct(shape, dtype)


def _resident(shape):
    return pl.BlockSpec(shape, lambda *_: (0,) * len(shape), pipeline_mode=pl.Buffered(1))


def _mm_cols(name, a, w, contract, out_dtypes, epi=None, extras=()):
    m, k = a.shape
    jn = w.shape[0]
    nj = w.shape[2] if contract == NN else w.shape[1]
    tm = min(m, MM_TILE)
    n_ex, parts = len(extras), 2

    def body(a_ref, w_ref, *rest):
        ex, out_refs = rest[:n_ex], rest[n_ex:]
        i = pl.program_id(1)
        part = tm // parts
        for h in range(parts):
            rows = slice(h * part, (h + 1) * part)
            acc = _dot(a_ref[pl.ds(pl.multiple_of(i * tm + h * part, part), part), :], w_ref[...], contract)
            res = epi(acc, *[e[rows, :] for e in ex]) if epi else (acc,)
            for o, r in zip(out_refs, res):
                o[rows, :] = r.astype(o.dtype)

    blk = pl.BlockSpec((tm, nj), lambda j, i: (i, j))
    return pl.pallas_call(
        body, name=name, grid=(jn, m // tm),
        in_specs=[_resident((m, k)), pl.BlockSpec((None,) + w.shape[1:], lambda j, i: (j, 0, 0))] + [blk] * n_ex,
        out_specs=[blk] * len(out_dtypes), out_shape=[_sds((m, jn * nj), dt) for dt in out_dtypes],
        compiler_params=_cp(("parallel", "parallel")),
    )(a, w, *extras)


def _mm_wgrad(name, a, b, a_by_j):
    s = a.shape[0]
    if a_by_j:
        r, c = a.shape[1] // N_CHIPS, b.shape[1]
        in_specs = [pl.BlockSpec((s, r), lambda j: (0, j)), _resident((s, c))]
    else:
        r, c = a.shape[1], b.shape[1] // N_CHIPS
        in_specs = [_resident((s, r)), pl.BlockSpec((s, c), lambda j: (0, j))]
    tr = min(r, 512)

    def body(a_ref, b_ref, o32_ref, o16_ref):
        for h in range(r // tr):
            cols = slice(h * tr, (h + 1) * tr)
            acc = _dot(a_ref[:, cols], b_ref[...], TN)
            o32_ref[cols, :] = acc
            o16_ref[cols, :] = acc.astype(BF16)

    out = pl.BlockSpec((None, r, c), lambda j: (j, 0, 0))
    return pl.pallas_call(
        body, name=name, grid=(N_CHIPS,), in_specs=in_specs, out_specs=[out, out],
        out_shape=[_sds((N_CHIPS, r, c), F32), _sds((N_CHIPS, r, c), BF16)], compiler_params=_cp(("parallel",)),
    )(a, b)


def _rows_call(name, fn, ins, outs, s, tm=ROW_TILE, matmul=None, exchange=None):
    in_specs = []
    if matmul:
        a, w = matmul
        in_specs += [pl.BlockSpec((tm, a.shape[1]), lambda i: (i, 0)), _resident(w.shape)]
    for arr, kind in ins:
        if kind == "full":
            in_specs.append(pl.BlockSpec(arr.shape, lambda i: (0, 0)))
        else:
            _, w_, cb = kind
            in_specs.append(pl.BlockSpec((tm, w_), lambda i, cb=cb: (i, cb)))
    out_specs, out_shape, is_acc = [], [], []
    for w_, dt, kind in outs:
        if kind == "acc":
            out_specs.append(pl.BlockSpec((1, w_), lambda i: (0, 0)))
            out_shape.append(_sds((1, w_), dt))
        else:
            out_specs.append(pl.BlockSpec((tm, w_), lambda i: (i, 0)))
            out_shape.append(_sds((s, w_), dt))
        is_acc.append(kind == "acc")
    n_mm, n_in, n_out = (2 if matmul else 0), len(ins), len(outs)
    x_ins, x_shapes, x_sems, x_start, x_finish = exchange if exchange else ((), [], [], None, None)
    n_x = len(x_ins)
    steps = s // tm

    def body(*refs):
        in_refs, xi = refs[n_mm:n_mm + n_in], refs[n_mm + n_in:n_mm + n_in + n_x]
        out_refs = refs[n_mm + n_in + n_x:n_mm + n_in + n_x + n_out]
        xo, sems = refs[n_mm + n_in + n_x + n_out:n_mm + n_in + 2 * n_x + n_out], refs[n_mm + n_in + 2 * n_x + n_out:]
        i = pl.program_id(0)

        if exchange:
            @pl.when(i == 0)
            def _():
                x_start(xi, xo, *sems)

        for o, acc in zip(out_refs, is_acc):
            if acc:
                @pl.when(i == 0)
                def _(o=o):
                    o[...] = jnp.zeros_like(o)

        parts = 2 if matmul else 1
        for h in range(parts):
            rows = slice(h * (tm // parts), (h + 1) * (tm // parts))
            args = [r[...] if kind == "full" else r[rows, :] for r, (_, kind) in zip(in_refs, ins)]
            if matmul:
                a_ref, w_ref = refs[:2]
                if len(w_ref.shape) == 2:
                    acc = _dot(a_ref[rows, :], w_ref[...], NN)
                else:
                    kj = w_ref.shape[2]
                    acc = _dot(a_ref[rows, 0:kj], w_ref[0], NT)
                    for j in range(1, w_ref.shape[0]):
                        acc = acc + _dot(a_ref[rows, j * kj:(j + 1) * kj], w_ref[j], NT)
                args.insert(0, acc)
            for o, r, acc in zip(out_refs, fn(*args), is_acc):
                if acc:
                    o[...] += r.astype(o.dtype)
                else:
                    o[rows, :] = r.astype(o.dtype)

        if exchange:
            @pl.when(i == steps - 1)
            def _():
                x_finish(xi, xo, *sems)

    sem = ("arbitrary",) if any(is_acc) or exchange else ("parallel",)
    return pl.pallas_call(
        body, name=name, grid=(steps,), in_specs=in_specs + [ANY] * n_x, out_specs=out_specs + [ANY] * n_x,
        out_shape=out_shape + list(x_shapes), scratch_shapes=list(x_sems), compiler_params=_cp(sem),
    )(*(matmul or ()), *[a for a, _ in ins], *x_ins)


def _rstd(v):
    return lax.rsqrt(jnp.mean(v * v, axis=-1, keepdims=True) + RMS_EPS)


def _norm_bwd(v, gain, dy):
    r = _rstd(v)
    n = v * r
    dn = dy * gain
    dv = r * (dn - n * jnp.mean(dn * n, axis=-1, keepdims=True))
    return dv, dy * n


def _colsum(v):
    return jnp.sum(v, axis=0, keepdims=True)


def _row(w, cb=0):
    return ("row", w, cb)


N_PAIRS = ATTN_W // BLK


def _head_col(v, mask):
    return jnp.max(jnp.where(mask, v, -jnp.inf), axis=1, keepdims=True)


def _slopes():
    t = np.zeros((N_PAIRS, 8, 2 * BLK), np.float32)
    for p in range(N_PAIRS):
        for hh in range(2):
            t[p, hh, :] = 2.0 ** -(2 * p + hh + 1)
    return jnp.asarray(t)


def _rows(n, r, d):
    base = pl.multiple_of(n * (BLK * d), BLK)
    return pl.ds(base + r, BLK, stride=d) if d > 1 else pl.ds(base, BLK)


def _attn_bias(sl_ref, bias_scr):
    row = lax.broadcasted_iota(jnp.int32, (BLK, 2 * BLK), 0)
    col = lax.broadcasted_iota(jnp.int32, (BLK, 2 * BLK), 1)
    dist = row + BLK - col
    in_window = (dist >= 0) & (dist <= BLK)
    distf = dist.astype(F32)
    for di, d in enumerate(DILATIONS):
        for hh in range(2):
            bias_scr[2 * di + hh] = jnp.where(in_window, -(sl_ref[hh:hh + 1, :] * float(d)) * distf, -1e30)


def _first_block_penalty(n):
    col = lax.broadcasted_iota(jnp.int32, (1, 2 * BLK), 1)
    return jnp.where(col + n * BLK >= BLK, 0.0, -1e30)


def _attn_groups(s, d):
    nb = s // (BLK * d)
    g = max(ATTN_GROUP, d)
    if d >= g:
        return [(nb, lambda n, r0=r0: [(n, r0 + u) for u in range(g)]) for r0 in range(0, d, g)]
    per = g // d
    return [(nb // per, lambda t: [(per * t + u, r) for u in range(per) for r in range(d)])]


def _attn_fwd(proj, shards):
    s = proj.shape[0]
    nk = len(shards)

    def body(sl_ref, q_ref, k_ref, v_ref, *rest):
        w_refs, (o_ref, l_ref) = rest[:nk], rest[nk:nk + 2]
        wg_refs, (bias_scr, ssem, rsem) = rest[nk + 2:2 * nk + 2], rest[2 * nk + 2:]
        pair = pl.program_id(0)

        @pl.when(pair == 0)
        def _():
            _ag_start(w_refs, wg_refs, ssem, rsem)

        _attn_bias(sl_ref, bias_scr)
        lane_q = lax.broadcasted_iota(jnp.int32, (BLK, BLK), 1) < 64
        lane_k = lax.broadcasted_iota(jnp.int32, (2 * BLK, BLK), 1) < 64

        def branch(n, r, di):
            d = DILATIONS[di]
            rows = _rows(n, r, d)
            prev = _rows(jnp.maximum(n - 1, 0), r, d)
            pen = _first_block_penalty(n)
            q2 = q_ref[rows, :] * ATTN_SCALE
            kk = jnp.concatenate([k_ref[prev, :], k_ref[rows, :]], axis=0).astype(BF16)
            vv = jnp.concatenate([v_ref[prev, :], v_ref[rows, :]], axis=0)
            o2 = jnp.zeros((BLK, BLK), F32)
            lse2 = jnp.zeros((BLK, BLK), F32)
            for hh in range(2):
                mq = lane_q if hh == 0 else ~lane_q
                mk = lane_k if hh == 0 else ~lane_k
                qm = jnp.where(mq, q2, 0.0).astype(BF16)
                sc = _dot(qm, kk, NT) + bias_scr[2 * di + hh] + pen
                m = jnp.max(sc, axis=1, keepdims=True)
                pr = jnp.exp(sc - m)
                den = jnp.sum(pr, axis=1, keepdims=True)
                vm = jnp.where(mk, vv, 0.0).astype(BF16)
                o2 = o2 + _dot(pr.astype(BF16), vm, NN) / den
                lse2 = jnp.where(mq, m + jnp.log(den), lse2)
            return rows, o2, lse2

        def merge(rows, o2, lse2, first):
            if first:
                o_ref[rows, :] = o2
                l_ref[rows, :] = lse2
            else:
                lo = l_ref[rows, :]
                mx = jnp.maximum(lo, lse2)
                ln = mx + jnp.log(jnp.exp(lo - mx) + jnp.exp(lse2 - mx))
                o_ref[rows, :] = jnp.exp(lo - ln) * o_ref[rows, :] + jnp.exp(lse2 - ln) * o2
                l_ref[rows, :] = ln

        for di, d in enumerate(DILATIONS):
            for trips, blocks in _attn_groups(s, d):
                def trip(t, carry, di=di, blocks=blocks):
                    done = [branch(n, r, di) for n, r in blocks(t)]
                    for rows, o2, lse2 in done:
                        merge(rows, o2, lse2, di == 0)
                    return carry

                lax.fori_loop(0, trips, trip, 0)

        @pl.when(pair == N_PAIRS - 1)
        def _():
            _ag_finish(w_refs, wg_refs, ssem, rsem)

    cb = lambda base: pl.BlockSpec((s, BLK), lambda p, base=base: (0, base + p))
    out = pl.BlockSpec((s, BLK), lambda p: (0, p))
    ag_shape, ag_sems = _ag_shapes(shards)
    return pl.pallas_call(
        body, name="attn_fwd", grid=(N_PAIRS,),
        in_specs=[pl.BlockSpec((None, 8, 2 * BLK), lambda p: (p, 0, 0)), cb(0), cb(N_PAIRS), cb(2 * N_PAIRS)]
        + [ANY] * nk,
        out_specs=[out, out] + [ANY] * nk, out_shape=[_sds((s, ATTN_W), F32)] * 2 + ag_shape,
        scratch_shapes=[pltpu.VMEM((2 * len(DILATIONS), BLK, 2 * BLK), F32)] + ag_sems,
        compiler_params=_cp(("arbitrary",)),
    )(_slopes(), proj, proj, proj, *shards)


def _attn_bwd(proj, do, lse, delta, psums):
    s = proj.shape[0]
    nk = len(psums)

    def body(sl_ref, q_ref, k_ref, v_ref, do_ref, l_ref, e_ref, *rest):
        p_refs, out16 = rest[:nk], rest[nk:nk + 3]
        got_refs, (dq_ref, dk_ref, dv_ref, bias_scr, ssem, rsem) = rest[nk + 3:2 * nk + 3], rest[2 * nk + 3:]
        pair = pl.program_id(0)

        @pl.when(pair == 0)
        def _():
            _rs_chips_start(p_refs, got_refs, ssem, rsem)

        _attn_bias(sl_ref, bias_scr)
        lane_q = lax.broadcasted_iota(jnp.int32, (BLK, BLK), 1) < 64
        lane_k = lax.broadcasted_iota(jnp.int32, (2 * BLK, BLK), 1) < 64
        dk_ref[...] = jnp.zeros_like(dk_ref)
        dv_ref[...] = jnp.zeros_like(dv_ref)

        def branch(n, r, di):
            d = DILATIONS[di]
            rows = _rows(n, r, d)
            prev = _rows(jnp.maximum(n - 1, 0), r, d)
            pen = _first_block_penalty(n)
            q1, d1, l1, e1 = q_ref[rows, :] * ATTN_SCALE, do_ref[rows, :], l_ref[rows, :], e_ref[rows, :]
            kk = jnp.concatenate([k_ref[prev, :], k_ref[rows, :]], axis=0)
            kkb = kk.astype(BF16)
            vvb = jnp.concatenate([v_ref[prev, :], v_ref[rows, :]], axis=0).astype(BF16)
            dq2 = jnp.zeros((BLK, BLK), F32)
            dkk = jnp.zeros((2 * BLK, BLK), F32)
            dvv = jnp.zeros((2 * BLK, BLK), F32)
            for hh in range(2):
                mq = lane_q if hh == 0 else ~lane_q
                mk = lane_k if hh == 0 else ~lane_k
                qm = jnp.where(mq, q1, 0.0).astype(BF16)
                dm = jnp.where(mq, d1, 0.0).astype(BF16)
                sc = _dot(qm, kkb, NT) + bias_scr[2 * di + hh] + pen
                pr = jnp.exp(sc - _head_col(l1, mq))
                ds = (pr * (_dot(dm, vvb, NT) - _head_col(e1, mq))).astype(BF16)
                km = jnp.where(mk, kk, 0.0).astype(BF16)
                dq2 = dq2 + _dot(ds, km, NN)
                dkk = dkk + _dot(ds, qm, TN)
                dvv = dvv + _dot(pr.astype(BF16), dm, TN)
            return rows, prev, dq2 * ATTN_SCALE, dkk, dvv

        for di, d in enumerate(DILATIONS):
            for trips, blocks in _attn_groups(s, d):
                def trip(t, carry, di=di, blocks=blocks, first=(di == 0)):
                    done = [branch(n, r, di) for n, r in blocks(t)]
                    for rows, prev, dq2, dkk, dvv in done:
                        dq_ref[rows, :] = dq2 if first else dq_ref[rows, :] + dq2
                        dk_ref[prev, :] = dk_ref[prev, :] + dkk[:BLK]
                        dk_ref[rows, :] = dk_ref[rows, :] + dkk[BLK:]
                        dv_ref[prev, :] = dv_ref[prev, :] + dvv[:BLK]
                        dv_ref[rows, :] = dv_ref[rows, :] + dvv[BLK:]
                    return carry

                lax.fori_loop(0, trips, trip, 0)

        for o16, acc in zip(out16, (dq_ref, dk_ref, dv_ref)):
            o16[...] = acc[...].astype(BF16)

        @pl.when(pair == N_PAIRS - 1)
        def _():
            _rs_chips_finish(p_refs, got_refs, ssem, rsem)

    cb = lambda base: pl.BlockSpec((s, BLK), lambda p, base=base: (0, base + p))
    out = pl.BlockSpec((s, BLK), lambda p: (0, p))
    rs_shape, rs_sems = _rs_chips_shapes(psums)
    return pl.pallas_call(
        body, name="attn_bwd", grid=(N_PAIRS,),
        in_specs=[pl.BlockSpec((None, 8, 2 * BLK), lambda p: (p, 0, 0)), cb(0), cb(N_PAIRS), cb(2 * N_PAIRS),
                  out, out, out] + [ANY] * nk,
        out_specs=[out] * 3 + [ANY] * nk, out_shape=[_sds((s, ATTN_W), BF16)] * 3 + rs_shape,
        scratch_shapes=[pltpu.VMEM((s, BLK), F32)] * 3 + [pltpu.VMEM((2 * len(DILATIONS), BLK, 2 * BLK), F32)] + rs_sems,
        compiler_params=_cp(("arbitrary",)),
    )(_slopes(), proj, proj, proj, do, lse, delta, *psums)


def _lower_bound(lbl):
    return 1.0 / (1.0 + jnp.exp(lbl[1:2, :] - lbl[0:1, :]))


def _hi(a):
    bits = lax.bitcast_convert_type(a, jnp.uint32) & jnp.uint32(0xFFFF0000)
    return lax.bitcast_convert_type(bits, F32)


def _dot3(a, b, contract):
    ah, bh = _hi(a), _hi(b)
    al, bl = (a - ah).astype(BF16), (b - bh).astype(BF16)
    ah, bh = ah.astype(BF16), bh.astype(BF16)
    return _dot(ah, bh, contract) + (_dot(ah, bl, contract) + _dot(al, bh, contract))


def _cumsum_rows(tri, g):
    g1 = _hi(g)
    r1 = g - g1
    g2 = _hi(r1)
    g3 = r1 - g2
    return _dot(tri, g1.astype(BF16), NN) + (_dot(tri, g2.astype(BF16), NN) + _dot(tri, g3.astype(BF16), NN))


def _heads(fn):
    return jnp.concatenate([fn(slice(h * BLK, (h + 1) * BLK)) for h in range(HGRN_W // BLK)], axis=1)


def _head_mean(t):
    return _heads(lambda hs: jnp.broadcast_to(jnp.mean(t[:, hs], axis=1, keepdims=True), (t.shape[0], BLK)))


def _hgrn_chunk(q_ref, f_ref, i_ref, sl, lb, tri):
    qp = q_ref[sl, :]
    sq = _sigmoid(qp)
    qf = qp * sq
    sg = _sigmoid(f_ref[sl, :])
    f = lb + (1.0 - lb) * sg
    kf = 1.0 - f
    v = i_ref[sl, :]
    b = _cumsum_rows(tri, jnp.log(f))
    bm = b[CHUNK // 2:CHUNK // 2 + 1, :]
    bl = b[CHUNK - 1:CHUNK, :]
    qt = qf * jnp.exp(b - bm)
    kt = kf * jnp.exp(bm - b)
    return qp, sq, qf, sg, f, kf, v, b, bm, bl, qt, kt


def _hgrn_specs(tb, block):
    first = 3 * ATTN_W // HGRN_W
    return [pl.BlockSpec((tb, HGRN_W), lambda i, k=k: (block(i), first + k)) for k in range(4)]


def _hgrn_fwd(proj, lb_logits, out_gain):
    s = proj.shape[0]
    tb = min(HGRN_TB, s)
    nb, cpb, nc = s // tb, tb // CHUNK, s // CHUNK

    def body(q_ref, f_ref, i_ref, g_ref, lbl_ref, gain_ref, o_ref, rec_ref, st_ref, st_scr):
        step = pl.program_id(0)

        @pl.when(step == 0)
        def _():
            st_scr[...] = jnp.zeros_like(st_scr)

        lb = _lower_bound(lbl_ref[...])
        r64 = lax.broadcasted_iota(jnp.int32, (CHUNK, CHUNK), 0)
        c64 = lax.broadcasted_iota(jnp.int32, (CHUNK, CHUNK), 1)
        tril = r64 >= c64
        tri = tril.astype(BF16)
        st = st_scr[...]
        for cc in range(cpb):
            sl = slice(cc * CHUNK, (cc + 1) * CHUNK)
            _, _, qf, _, _, kf, v, b, _, bl, qt, kt = _hgrn_chunk(q_ref, f_ref, i_ref, sl, lb, tri)
            qe = (qf * jnp.exp(b)).astype(BF16)
            kh = (kf * jnp.exp(bl - b)).astype(BF16)
            qtb, ktb, vb, stb = qt.astype(BF16), kt.astype(BF16), v.astype(BF16), st.astype(BF16)

            def out_h(hs):
                a = jnp.where(tril, _dot(qtb[:, hs], ktb[:, hs], NT), 0.0).astype(BF16)
                return _dot(qe[:, hs], stb[:, hs], NT) + _dot(a, vb[:, hs], NN)

            o_ref[sl, :] = _heads(out_h)
            st_ref[cc] = stb
            st = st * jnp.exp(bl) + _heads(lambda hs: _dot(vb[:, hs], kh[:, hs], TN))
        st_scr[...] = st
        o = o_ref[...]
        gate = g_ref[...]
        rec_ref[...] = (o * lax.rsqrt(_head_mean(o * o) + RMS_EPS) * gain_ref[...] * (gate * _sigmoid(gate))).astype(BF16)

    row = pl.BlockSpec((tb, HGRN_W), lambda i: (i, 0))
    return pl.pallas_call(
        body, name="hgrn_fwd", grid=(nb,),
        in_specs=_hgrn_specs(tb, lambda i: i) + [pl.BlockSpec((2, HGRN_W), lambda i: (0, 0)),
                                                 pl.BlockSpec((1, HGRN_W), lambda i: (0, 0))],
        out_specs=[row, row, pl.BlockSpec((cpb, BLK, HGRN_W), lambda i: (i, 0, 0))],
        out_shape=[_sds((s, HGRN_W), F32), _sds((s, HGRN_W), BF16), _sds((nc, BLK, HGRN_W), BF16)],
        scratch_shapes=[pltpu.VMEM((BLK, HGRN_W), F32)],
        compiler_params=_cp(("arbitrary",)),
    )(proj, proj, proj, proj, lb_logits, out_gain)


def _hgrn_bwd(proj, o_pre, states, drec, lb_logits, out_gain):
    s = proj.shape[0]
    tb = min(HGRN_TB, s)
    nb, cpb, nc = s // tb, tb // CHUNK, s // CHUNK

    def body(q_ref, f_ref, i_ref, g_ref, o_ref, st_ref, stn_ref, dy_ref, lbl_ref, gain_ref,
             dq_ref, df_ref, di_ref, dg_ref, dgain_ref, dlbl_ref, do_scr, dst_scr, dlb_scr):
        step = pl.program_id(0)

        @pl.when(step == 0)
        def _():
            dst_scr[...] = jnp.zeros_like(dst_scr)
            dlb_scr[...] = jnp.zeros_like(dlb_scr)
            dgain_ref[...] = jnp.zeros_like(dgain_ref)

        lb = _lower_bound(lbl_ref[...])
        gain = gain_ref[...]
        o = o_ref[...]
        r = lax.rsqrt(_head_mean(o * o) + RMS_EPS)
        nrm = o * r
        gate = g_ref[...]
        sgt = _sigmoid(gate)
        dy = dy_ref[...]
        dg_ref[...] = (dy * nrm * gain * (sgt * (1.0 + gate * (1.0 - sgt)))).astype(BF16)
        dng = dy * (gate * sgt)
        dgain_ref[...] += _colsum(dng * nrm)
        dn = dng * gain
        do_scr[...] = r * (dn - nrm * _head_mean(dn * nrm))

        r64 = lax.broadcasted_iota(jnp.int32, (CHUNK, CHUNK), 0)
        c64 = lax.broadcasted_iota(jnp.int32, (CHUNK, CHUNK), 1)
        tril = r64 >= c64
        tri = tril.astype(BF16)
        triu = (r64 <= c64).astype(BF16)
        dst = dst_scr[...]
        dlb = dlb_scr[...]
        for cc in reversed(range(cpb)):
            sl = slice(cc * CHUNK, (cc + 1) * CHUNK)
            qp, sq, qf, sg, f, kf, v, b, bm, bl, qt, kt = _hgrn_chunk(q_ref, f_ref, i_ref, sl, lb, tri)
            stf = st_ref[cc].astype(F32)
            st_end = (st_ref[cc + 1] if cc + 1 < cpb else stn_ref[0]).astype(F32)
            csum = jnp.sum(st_end * dst, axis=0, keepdims=True)
            doc = do_scr[sl, :]
            dob, dstb = doc.astype(BF16), dst.astype(BF16)
            eb = jnp.exp(b)
            qe = (qf * eb).astype(BF16)
            kh = (kf * jnp.exp(bl - b)).astype(BF16)
            qtb, ktb = qt.astype(BF16), kt.astype(BF16)
            parts = []
            for h in range(HGRN_W // BLK):
                hs = slice(h * BLK, (h + 1) * BLK)
                da = jnp.where(tril, _dot3(doc[:, hs], v[:, hs], NT), 0.0)
                a = jnp.where(tril, _dot(qtb[:, hs], ktb[:, hs], NT), 0.0).astype(BF16)
                parts.append((
                    _dot3(da, kt[:, hs], NN), _dot3(doc[:, hs], stf[:, hs], NN),
                    _dot3(da, qt[:, hs], TN), _dot3(v[:, hs], dst[:, hs], NN),
                    _dot(a, dob[:, hs], TN) + _dot(kh[:, hs], dstb[:, hs], NT),
                    _dot(dob[:, hs], qe[:, hs], TN)))
            dqt, dqi, dkt, dks, dv, upd = (jnp.concatenate([p[n] for p in parts], axis=1) for n in range(6))
            dqf = dqt * jnp.exp(b - bm) + eb * dqi
            dkf = dkt * jnp.exp(bm - b) + jnp.exp(bl - b) * dks
            gq = qf * dqf - kf * dkf
            dlogf = csum + _cumsum_rows(triu, gq)
            dfv = dlogf / f - dkf
            dq_ref[sl, :] = (dqf * (sq * (1.0 + qp * (1.0 - sq)))).astype(BF16)
            df_ref[sl, :] = (dfv * (1.0 - lb) * sg * (1.0 - sg)).astype(BF16)
            di_ref[sl, :] = dv.astype(BF16)
            dst = dst * jnp.exp(bl) + upd
            dlb = dlb + _colsum(dfv * (1.0 - sg))
        dst_scr[...] = dst
        dlb_scr[...] = dlb

        @pl.when(step == nb - 1)
        def _():
            t = dlb * lb * (1.0 - lb)
            dlbl_ref[...] = jnp.concatenate([t, -t], axis=0)

    rev = lambda i: nb - 1 - i
    row = pl.BlockSpec((tb, HGRN_W), lambda i: (rev(i), 0))
    res = pl.pallas_call(
        body, name="hgrn_bwd", grid=(nb,),
        in_specs=_hgrn_specs(tb, rev) + [
            row, pl.BlockSpec((cpb, BLK, HGRN_W), lambda i: (rev(i), 0, 0)),
            pl.BlockSpec((1, BLK, HGRN_W), lambda i: (jnp.minimum((rev(i) + 1) * cpb, nc - 1), 0, 0)),
            row, pl.BlockSpec((2, HGRN_W), lambda i: (0, 0)), pl.BlockSpec((1, HGRN_W), lambda i: (0, 0))],
        out_specs=[row, row, row, row, pl.BlockSpec((1, HGRN_W), lambda i: (0, 0)),
                   pl.BlockSpec((2, HGRN_W), lambda i: (0, 0))],
        out_shape=[_sds((s, HGRN_W), BF16)] * 4 + [_sds((1, HGRN_W), F32), _sds((2, HGRN_W), F32)],
        scratch_shapes=[pltpu.VMEM((tb, HGRN_W), F32), pltpu.VMEM((BLK, HGRN_W), F32), pltpu.VMEM((1, HGRN_W), F32)],
        compiler_params=_cp(("arbitrary",)),
    )(proj, proj, proj, proj, o_pre, states, states, drec, lb_logits, out_gain)
    return res


def _place():
    return lax.axis_index("x"), lax.axis_index("y"), lax.axis_index("c")


def _flip(x, y, ox, oy):
    return (1 - x if ox else x), (1 - y if oy else y)


def _half(rows, cc):
    return pl.ds(cc * (rows // 2), rows // 2)


def _remote(src, dst, ssem, rsem, to):
    return pltpu.make_async_remote_copy(src_ref=src, dst_ref=dst, send_sem=ssem, recv_sem=rsem,
                                        device_id=to, device_id_type=MESH)


def _ag_chip_copies(ins, outs, ssem, rsem):
    x, y, c = _place()
    j = 2 * x + y
    cps = []
    for k in range(len(ins)):
        rows = ins[k].shape[0]
        for idx, (ox, oy) in enumerate(FLIPS):
            px, py = _flip(x, y, ox, oy)
            cps.append(_remote(ins[k].at[_half(rows, c)], outs[k].at[j, _half(rows, c)],
                               ssem.at[k, idx], rsem.at[k, idx], (px, py, c)))
    return cps


def _ag_start(ins, outs, ssem, rsem):
    for cp in _ag_chip_copies(ins, outs, ssem, rsem):
        cp.start()


def _ag_finish(ins, outs, ssem, rsem):
    x, y, c = _place()
    sib = (x, y, 1 - c)
    passed = []
    for k in range(len(ins)):
        rows = ins[k].shape[0]
        for idx, (ox, oy) in enumerate(FLIPS):
            px, py = _flip(x, y, ox, oy)
            blk = outs[k].at[2 * px + py, _half(rows, c)]
            _remote(blk, blk, ssem.at[k, idx], rsem.at[k, idx], (px, py, c)).wait_recv()
            cp = _remote(blk, blk, ssem.at[k, 3 + idx], rsem.at[k, 3 + idx], sib)
            cp.start()
            passed.append(cp)
    for k in range(len(ins)):
        rows = ins[k].shape[0]
        for idx, (ox, oy) in enumerate(FLIPS):
            px, py = _flip(x, y, ox, oy)
            blk = outs[k].at[2 * px + py, _half(rows, 1 - c)]
            _remote(blk, blk, ssem.at[k, 3 + idx], rsem.at[k, 3 + idx], sib).wait_recv()
    for cp in _ag_chip_copies(ins, outs, ssem, rsem) + passed:
        cp.wait_send()


def _ag_shapes(shards):
    nk = len(shards)
    return ([_sds((N_CHIPS,) + tuple(w.shape), w.dtype) for w in shards],
            [pltpu.SemaphoreType.DMA((nk, 6)), pltpu.SemaphoreType.DMA((nk, 6))])


def _with_own(gathered, shard, j):
    return lax.dynamic_update_index_in_dim(gathered, shard, j, 0)


def _rs_pair_copies(ins, outs, ssem, rsem):
    x, y, c = _place()
    return [_remote(ins[k].at[:, _half(ins[k].shape[1], 1 - c)], outs[k], ssem.at[k], rsem.at[k], (x, y, 1 - c))
            for k in range(len(ins))]


def _rs_pair_start(ins, outs, ssem, rsem):
    for cp in _rs_pair_copies(ins, outs, ssem, rsem):
        cp.start()


def _rs_pair_finish(ins, outs, ssem, rsem):
    for cp in _rs_pair_copies(ins, outs, ssem, rsem):
        cp.wait()


def _rs_pair_exchange(grads):
    nk = len(grads)
    return (grads, [_sds((N_CHIPS, g.shape[1] // 2, g.shape[2]), g.dtype) for g in grads],
            [pltpu.SemaphoreType.DMA((nk,)), pltpu.SemaphoreType.DMA((nk,))], _rs_pair_start, _rs_pair_finish)


def _rs_pair(name, grads):
    nk = len(grads)
    ins, out_shape, sems, start, finish = _rs_pair_exchange(grads)

    def body(*refs):
        start(refs[:nk], refs[nk:2 * nk], *refs[2 * nk:])
        finish(refs[:nk], refs[nk:2 * nk], *refs[2 * nk:])

    return pl.pallas_call(body, name=name, in_specs=[ANY] * nk, out_specs=[ANY] * nk, out_shape=out_shape,
                          scratch_shapes=sems)(*ins)


def _rs_chip_copies(ins, outs, ssem, rsem):
    x, y, c = _place()
    cps = []
    for k in range(len(ins)):
        for idx, (ox, oy) in enumerate(FLIPS):
            px, py = _flip(x, y, ox, oy)
            cps.append(_remote(ins[k].at[2 * px + py], outs[k].at[idx], ssem.at[k, idx], rsem.at[k, idx], (px, py, c)))
    return cps


def _rs_chips_start(ins, outs, ssem, rsem):
    for cp in _rs_chip_copies(ins, outs, ssem, rsem):
        cp.start()


def _rs_chips_finish(ins, outs, ssem, rsem):
    for cp in _rs_chip_copies(ins, outs, ssem, rsem):
        cp.wait()


def _rs_chips_shapes(psums):
    nk = len(psums)
    return ([_sds((3,) + tuple(p.shape[1:]), p.dtype) for p in psums],
            [pltpu.SemaphoreType.DMA((nk, 3)), pltpu.SemaphoreType.DMA((nk, 3))])


def _rs_share(fulls):
    nk = len(fulls)

    def body(*refs):
        ins, outs = refs[:nk], refs[nk:2 * nk]
        ssem, rsem = refs[2 * nk:]
        x, y, c = _place()
        cps = []
        for k in range(nk):
            rows = fulls[k].shape[0]
            cp = _remote(ins[k].at[_half(rows, c)], outs[k].at[_half(rows, c)], ssem.at[k], rsem.at[k], (x, y, 1 - c))
            cp.start()
            cps.append(cp)
        for k, cp in enumerate(cps):
            rows = fulls[k].shape[0]
            cp.wait_send()
            theirs = outs[k].at[_half(rows, 1 - c)]
            _remote(theirs, theirs, ssem.at[k], rsem.at[k], (x, y, 1 - c)).wait_recv()

    return pl.pallas_call(
        body, name="rs_share", in_specs=[ANY] * nk, out_specs=[ANY] * nk,
        out_shape=[_sds(f.shape, f.dtype) for f in fulls], input_output_aliases={k: k for k in range(nk)},
        scratch_shapes=[pltpu.SemaphoreType.DMA((nk,)), pltpu.SemaphoreType.DMA((nk,))],
    )(*fulls)


def _allreduce_small(v):
    ndev = 8

    def body(in_ref, out_ref, buf, ssem, rsem):
        x, y, c = _place()
        me = 4 * x + 2 * y + c
        buf[me] = in_ref[...]
        cps = []
        for k in range(1, ndev):
            ox, oy, oc = (k >> 2) & 1, (k >> 1) & 1, k & 1
            px, py = _flip(x, y, ox, oy)
            pc = 1 - c if oc else c
            cp = pltpu.make_async_remote_copy(src_ref=in_ref, dst_ref=buf.at[me], send_sem=ssem.at[k - 1],
                                              recv_sem=rsem.at[k - 1], device_id=(px, py, pc), device_id_type=MESH)
            cp.start()
            cps.append((cp, 4 * px + 2 * py + pc, (px, py, pc)))
        for k, (cp, src, peer) in enumerate(cps):
            cp.wait_send()
            pltpu.make_async_remote_copy(src_ref=in_ref, dst_ref=buf.at[src], send_sem=ssem.at[k],
                                         recv_sem=rsem.at[k], device_id=peer, device_id_type=MESH).wait_recv()
        acc = buf[0]
        for i in range(1, ndev):
            acc = acc + buf[i]
        out_ref[...] = acc

    return pl.pallas_call(
        body, name="allreduce_small",
        in_specs=[pl.BlockSpec(memory_space=pltpu.VMEM)], out_specs=pl.BlockSpec(memory_space=pltpu.VMEM),
        out_shape=_sds(v.shape, v.dtype),
        scratch_shapes=[pltpu.VMEM((ndev,) + v.shape, v.dtype), pltpu.SemaphoreType.DMA((ndev - 1,)),
                        pltpu.SemaphoreType.DMA((ndev - 1,))],
    )(v)


def _rs_sum1(name, g, recv, jc_idx):
    _, r, cdim = g.shape
    hr = r // 2
    tr = min(hr, 256)
    nr = hr // tr

    def body(jc_ref, g_ref, r_ref, o32_ref, o16_ref):
        v = g_ref[...] + r_ref[...].astype(F32)
        o16_ref[...] = v.astype(BF16)

        @pl.when(pl.program_id(1) == jc_ref[0])
        def _():
            o32_ref[...] = v

    spec = pl.BlockSpec((None, tr, cdim), lambda i, j, jc: (j, i, 0))
    return pl.pallas_call(
        body, name=name,
        grid_spec=pltpu.PrefetchScalarGridSpec(
            num_scalar_prefetch=1, grid=(nr, N_CHIPS),
            in_specs=[pl.BlockSpec((None, tr, cdim), lambda i, j, jc: (j, jc[1] * nr + i, 0)), spec],
            out_specs=[pl.BlockSpec((tr, cdim), lambda i, j, jc: (i, 0)), spec]),
        out_shape=[_sds((hr, cdim), F32), _sds((N_CHIPS, hr, cdim), BF16)],
        compiler_params=_cp(("parallel", "arbitrary")),
    )(jc_idx, g, recv)


def _rs_sum2(name, p32, recv, jc_idx):
    hr, cdim = p32.shape
    tr = min(hr, 256)
    nr = hr // tr

    def body(jc_ref, p_ref, r_ref, o_ref):
        o_ref[...] = ((p_ref[...] + r_ref[0].astype(F32)) + r_ref[1].astype(F32)) + r_ref[2].astype(F32)

    return pl.pallas_call(
        body, name=name,
        grid_spec=pltpu.PrefetchScalarGridSpec(
            num_scalar_prefetch=1, grid=(nr,),
            in_specs=[pl.BlockSpec((tr, cdim), lambda i, jc: (i, 0)),
                      pl.BlockSpec((3, tr, cdim), lambda i, jc: (0, i, 0))],
            out_specs=pl.BlockSpec((tr, cdim), lambda i, jc: (jc[1] * nr + i, 0))),
        out_shape=_sds((2 * hr, cdim), F32),
        compiler_params=_cp(("parallel",)),
    )(jc_idx, p32, recv)


def _adamw(name, w, g, m, v):
    r, cdim = w.shape
    tr = min(r, 256)
    c1 = 1.0 - ADAM_B1 ** ADAM_STEP
    c2 = 1.0 - ADAM_B2 ** ADAM_STEP

    def body(w_ref, g_ref, m_ref, v_ref, d_ref, nm_ref, nv_ref):
        gv = g_ref[...]
        nm = ADAM_B1 * m_ref[...] + (1.0 - ADAM_B1) * gv
        nv = ADAM_B2 * v_ref[...] + (1.0 - ADAM_B2) * (gv * gv)
        d_ref[...] = -ADAM_LR * ((nm / c1) / (jnp.sqrt(nv / c2) + ADAM_EPS) + ADAM_WD * w_ref[...])
        nm_ref[...] = nm
        nv_ref[...] = nv

    spec = pl.BlockSpec((tr, cdim), lambda i: (i, 0))
    return pl.pallas_call(
        body, name=name, grid=(r // tr,), in_specs=[spec] * 4, out_specs=[spec] * 3,
        out_shape=[_sds((r, cdim), F32)] * 3, compiler_params=_cp(("parallel",)),
    )(w, g, m, v)


def _pack_small(mix_pre, attn_out, lb_logits, hgrn_out, mix_post, mlp_pre, mlp_post, extra=None):
    spare = jnp.zeros((1, D_MODEL), F32)
    rows = [mix_pre, jnp.concatenate([attn_out, hgrn_out], axis=1),
            jnp.concatenate([lb_logits[0:1], lb_logits[1:2]], axis=1), mix_post, mlp_pre, mlp_post,
            spare if extra is None else extra, spare]
    return jnp.concatenate(rows, axis=0)


def _unpack_small(p):
    return (p[0:1], p[1:2, :ATTN_W], jnp.concatenate([p[2:3, :HGRN_W], p[2:3, HGRN_W:]], axis=0),
            p[1:2, ATTN_W:], p[3:4], p[4:5], p[5:6])


def kernel(x, mix_pre_norm, w_in, attn_out_norm, hgrn_lb_logits, hgrn_out_norm, w_out, mix_post_norm, mlp_pre_norm, w_ff1, w_ff2, mlp_post_norm, loss_target, m_mix_pre_norm, m_w_in, m_attn_out_norm, m_hgrn_lb_logits, m_hgrn_out_norm, m_w_out, m_mix_post_norm, m_mlp_pre_norm, m_w_ff1, m_w_ff2, m_mlp_post_norm, v_mix_pre_norm, v_w_in, v_attn_out_norm, v_hgrn_lb_logits, v_hgrn_out_norm, v_w_out, v_mix_post_norm, v_mlp_pre_norm, v_w_ff1, v_w_ff2, v_mlp_post_norm):
    s = x.shape[1]
    xs = x.reshape(s, D_MODEL)
    tgt = loss_target.reshape(s, D_MODEL)
    cx, cy, cc = _place()
    chip = 2 * cx + cy
    jc_idx = jnp.stack([chip, cc]).astype(jnp.int32)

    big_w = [w_in[0], w_out[0], w_ff1[0], w_ff2[0]]
    big_m = [m_w_in[0], m_w_out[0], m_w_ff1[0], m_w_ff2[0]]
    big_v = [v_w_in[0], v_w_out[0], v_w_ff1[0], v_w_ff2[0]]
    shards = [w.astype(BF16) for w in big_w]

    h, wg_in = _rows_call("norm_in", lambda xv, g: ((xv * _rstd(xv) * g),),
                          [(xs, _row(D_MODEL)), (mix_pre_norm, "full")], [(D_MODEL, BF16, "row")], s,
                          exchange=(shards[:1], *_ag_shapes(shards[:1]), _ag_start, _ag_finish))
    wg_in = _with_own(wg_in, shards[0], chip)
    (proj,) = _mm_cols("mm_proj", h, wg_in, NN, [F32])
    hg_o, rec, states = _hgrn_fwd(proj, hgrn_lb_logits, hgrn_out_norm)
    attn_o, attn_lse, wg_out, wg_1, wg_2 = _attn_fwd(proj, shards[1:])
    wg_out, wg_1, wg_2 = (_with_own(g, w, chip) for g, w in zip((wg_out, wg_1, wg_2), shards[1:]))
    (attn_n,) = _rows_call("attn_norm", lambda o, gain: (o * _rstd(o) * gain,),
                           [(attn_o, _row(ATTN_W)), (attn_out_norm, "full")], [(ATTN_W, BF16, "row")], s)
    cat = jnp.concatenate([attn_n, rec], axis=1)

    def post1(mv, xv, g_post, g_pre2):
        x1 = xv + mv * _rstd(mv) * g_post
        return mv, x1, x1 * _rstd(x1) * g_pre2

    mixed, x1, h2 = _rows_call(
        "mm_mixed", post1, [(xs, _row(D_MODEL)), (mix_post_norm, "full"), (mlp_pre_norm, "full")],
        [(D_MODEL, F32, "row"), (D_MODEL, F32, "row"), (D_MODEL, BF16, "row")], s,
        matmul=(cat, wg_out.reshape(D_MODEL, D_MODEL)))

    def sq_relu(u):
        r = jnp.maximum(u, 0.0)
        return r * r, r

    act, ru = _mm_cols("mm_ff1", h2, wg_1, NN, [BF16, BF16], epi=sq_relu)

    def post2(fv, x1v, tv, g):
        y = x1v + fv * _rstd(fv) * g
        dy = (y - tv) * (1.0 / D_MODEL)
        err = y - tv
        loss = 0.5 * jnp.sum(jnp.mean(err * err, axis=-1, keepdims=True), axis=0, keepdims=True)
        dff, dgc = _norm_bwd(fv, g, dy)
        return dy, dff, _colsum(dgc), jnp.broadcast_to(loss, (1, BLK))

    dy, dff, g_mlp_post, loss_part = _rows_call(
        "mm_ff2", post2, [(x1, _row(D_MODEL)), (tgt, _row(D_MODEL)), (mlp_post_norm, "full")],
        [(D_MODEL, F32, "row"), (D_MODEL, BF16, "row"), (D_MODEL, F32, "acc"), (BLK, F32, "acc")], s,
        matmul=(act, wg_2.reshape(D_FF, D_MODEL)))

    (du,) = _mm_cols("mm_du", dff, wg_2, NT, [BF16], epi=lambda acc, r: (acc * (2.0 * r.astype(F32)),),
                     extras=(ru,))
    gw_2 = _mm_wgrad("mm_gw2", act, dff, True)
    gw_1 = _mm_wgrad("mm_gw1", h2, du, False)

    def bwd_mid(dh2v, dyv, x1v, mv, g_pre2, g_post):
        d1, gc1 = _norm_bwd(x1v, g_pre2, dh2v)
        dx1 = dyv + d1
        dm, gc2 = _norm_bwd(mv, g_post, dx1)
        return dx1, dm, _colsum(gc1), _colsum(gc2)

    dx1, dmixed, g_mlp_pre, g_mix_post, *from_pair = _rows_call(
        "mm_dh2", bwd_mid, [(dy, _row(D_MODEL)), (x1, _row(D_MODEL)), (mixed, _row(D_MODEL)),
                            (mlp_pre_norm, "full"), (mix_post_norm, "full")],
        [(D_MODEL, F32, "row"), (D_MODEL, BF16, "row"), (D_MODEL, F32, "acc"), (D_MODEL, F32, "acc")], s,
        matmul=(du, wg_1), exchange=_rs_pair_exchange([gw_1[1], gw_2[1]]))

    def attn_norm_bwd(dc, o, gain):
        do, gc = _norm_bwd(o, gain, dc[:, :ATTN_W])
        t = do * o
        lane = lax.broadcasted_iota(jnp.int32, (t.shape[0], BLK), 1) < 64
        parts = []
        for p in range(ATTN_W // BLK):
            tp = t[:, p * BLK:(p + 1) * BLK]
            sa = jnp.sum(jnp.where(lane, tp, 0.0), axis=1, keepdims=True)
            sb = jnp.sum(jnp.where(lane, 0.0, tp), axis=1, keepdims=True)
            parts.append(jnp.where(lane, sa, sb))
        return do, jnp.concatenate(parts, axis=1), dc[:, ATTN_W:], _colsum(gc)

    do_attn, delta, drec, g_attn_out = _rows_call(
        "mm_dcat", attn_norm_bwd, [(attn_o, _row(ATTN_W)), (attn_out_norm, "full")],
        [(ATTN_W, F32, "row"), (ATTN_W, F32, "row"), (HGRN_W, F32, "row"), (ATTN_W, F32, "acc")], s,
        matmul=(dmixed, wg_out.reshape(1, D_MODEL, D_MODEL)))
    gw_out = _mm_wgrad("mm_gwout", cat, dmixed, True)
    names = ["out", "ff1", "ff2", "in"]
    ready = [gw_out, gw_1, gw_2]
    from_pair = list(_rs_pair("rs_pair_out", [gw_out[1]])) + from_pair
    pair = [_rs_sum1(f"rs_sum1_{n}", g[0], r, jc_idx) for n, g, r in zip(names, ready, from_pair)]

    dq, dk, dv, *from_chips = _attn_bwd(proj, do_attn, attn_lse, delta, [p[1] for p in pair])
    dhq, dhf, dhi, dhg, g_hgrn_out, g_lb = _hgrn_bwd(proj, hg_o, states, drec, hgrn_lb_logits, hgrn_out_norm)

    dproj = jnp.concatenate([dq, dk, dv, dhq, dhf, dhi, dhg], axis=1)
    gw_in = _mm_wgrad("mm_gwin", h, dproj, False)
    (from_pair_in,) = _rs_pair("rs_pair_in", [gw_in[1]])
    pair.append(_rs_sum1("rs_sum1_in", gw_in[0], from_pair_in, jc_idx))
    rs_shape, rs_sems = _rs_chips_shapes([pair[3][1]])

    def bwd_in(dhv, dx1v, xv, g):
        d0, gc = _norm_bwd(xv, g, dhv)
        return dx1v + d0, _colsum(gc)

    grad_x, g_mix_pre, from_chips_in = _rows_call(
        "mm_dh", bwd_in, [(dx1, _row(D_MODEL)), (xs, _row(D_MODEL)), (mix_pre_norm, "full")],
        [(D_MODEL, F32, "row"), (D_MODEL, F32, "acc")], s, matmul=(dproj, wg_in),
        exchange=([pair[3][1]], rs_shape, rs_sems, _rs_chips_start, _rs_chips_finish))
    from_chips.append(from_chips_in)

    loss_row = jnp.pad(loss_part, ((0, 0), (0, D_MODEL - BLK)))
    small_g = _allreduce_small(_pack_small(g_mix_pre, g_attn_out, g_lb, g_hgrn_out, g_mix_post, g_mlp_pre, g_mlp_post,
                                           extra=loss_row))
    loss = small_g[6, 0]

    reduced = [_rs_sum2(f"rs_sum2_{n}", p[0], r, jc_idx) for n, p, r in zip(names, pair, from_chips)]
    g_wout, g_w1, g_w2, g_win = _rs_share(reduced)
    full = [g_win, g_wout, g_w1, g_w2]

    upd = [_adamw(f"adamw_{n}", w, g, m, v) for n, w, g, m, v in zip(("in", "out", "ff1", "ff2"), big_w, full, big_m, big_v)]
    small_w = _pack_small(mix_pre_norm, attn_out_norm, hgrn_lb_logits, hgrn_out_norm, mix_post_norm, mlp_pre_norm,
                          mlp_post_norm)
    small_m = _pack_small(m_mix_pre_norm, m_attn_out_norm, m_hgrn_lb_logits, m_hgrn_out_norm, m_mix_post_norm,
                          m_mlp_pre_norm, m_mlp_post_norm)
    small_v = _pack_small(v_mix_pre_norm, v_attn_out_norm, v_hgrn_lb_logits, v_hgrn_out_norm, v_mix_post_norm,
                          v_mlp_pre_norm, v_mlp_post_norm)
    small_upd = _adamw("adamw_small", small_w, small_g, small_m, small_v)

    def assemble(small, big):
        sm = _unpack_small(small)
        return (sm[0], big[0][None], sm[1], sm[2], sm[3], big[1][None], sm[4], sm[5], big[2][None], big[3][None], sm[6])

    g_out = assemble(small_g, full)
    d_out = assemble(small_upd[0], [u[0] for u in upd])
    m_out = assemble(small_upd[1], [u[1] for u in upd])
    v_out = assemble(small_upd[2], [u[2] for u in upd])
    return (loss, grad_x.reshape(x.shape), *g_out, *d_out, *m_out, *v_out)
```

```python
import numpy as np
import jax
import jax.numpy as jnp
from jax import lax
from jax.experimental import pallas as pl
from jax.experimental.pallas import tpu as pltpu

F32 = jnp.float32
BF16 = jnp.bfloat16
MESH = pl.DeviceIdType.MESH
ANY = pl.BlockSpec(memory_space=pl.ANY)

RMS_EPS = 1e-6
D_MODEL = 1024
ATTN_W = 512
HGRN_W = 512
PROJ_W = 3584
D_FF = 4096
N_CHIPS = 4
BLK = 128
CHUNK = 64
HGRN_TB = 512
ATTN_GROUP = 8
DILATIONS = (1, 4, 16)
ATTN_SCALE = 0.125
ROW_TILE = 512
MM_TILE = 1024
VMEM_LIMIT = 48 * 2 ** 20
FLIPS = ((1, 0), (0, 1), (1, 1))

ADAM_LR, ADAM_B1, ADAM_B2, ADAM_EPS, ADAM_WD, ADAM_STEP = 0.001, 0.9, 0.999, 1e-08, 0.01, 10


def _cp(sem=None):
    return pltpu.CompilerParams(dimension_semantics=sem, vmem_limit_bytes=VMEM_LIMIT)


def _sigmoid(v):
    return 1.0 / (1.0 + jnp.exp(-v))


def _dot(a, b, contract, precision=None):
    return lax.dot_general(a, b, (contract, ((), ())), preferred_element_type=F32, precision=precision)


NN = ((1,), (0,))
NT = ((1,), (1,))
TN = ((0,), (0,))


def _sds(shape, dtype):
    return jax.ShapeDtypeStruct(shape, dtype)


def _resident(shape):
    return pl.BlockSpec(shape, lambda *_: (0,) * len(shape), pipeline_mode=pl.Buffered(1))


def _mm_cols(name, a, w, contract, out_dtypes, epi=None, extras=()):
    m, k = a.shape
    jn = w.shape[0]
    nj = w.shape[2] if contract == NN else w.shape[1]
    tm = min(m, MM_TILE)
    n_ex, parts = len(extras), 2

    def body(a_ref, w_ref, *rest):
        ex, out_refs = rest[:n_ex], rest[n_ex:]
        i = pl.program_id(1)
        part = tm // parts
        for h in range(parts):
            rows = slice(h * part, (h + 1) * part)
            acc = _dot(a_ref[pl.ds(pl.multiple_of(i * tm + h * part, part), part), :], w_ref[...], contract)
            res = epi(acc, *[e[rows, :] for e in ex]) if epi else (acc,)
            for o, r in zip(out_refs, res):
                o[rows, :] = r.astype(o.dtype)

    blk = pl.BlockSpec((tm, nj), lambda j, i: (i, j))
    return pl.pallas_call(
        body, name=name, grid=(jn, m // tm),
        in_specs=[_resident((m, k)), pl.BlockSpec((None,) + w.shape[1:], lambda j, i: (j, 0, 0))] + [blk] * n_ex,
        out_specs=[blk] * len(out_dtypes), out_shape=[_sds((m, jn * nj), dt) for dt in out_dtypes],
        compiler_params=_cp(("parallel", "parallel")),
    )(a, w, *extras)


def _mm_wgrad(name, a, b, a_by_j):
    s = a.shape[0]
    if a_by_j:
        r, c = a.shape[1] // N_CHIPS, b.shape[1]
        in_specs = [pl.BlockSpec((s, r), lambda j: (0, j)), _resident((s, c))]
    else:
        r, c = a.shape[1], b.shape[1] // N_CHIPS
        in_specs = [_resident((s, r)), pl.BlockSpec((s, c), lambda j: (0, j))]
    tr = min(r, 512)

    def body(a_ref, b_ref, o32_ref, o16_ref):
        for h in range(r // tr):
            cols = slice(h * tr, (h + 1) * tr)
            acc = _dot(a_ref[:, cols], b_ref[...], TN)
            o32_ref[cols, :] = acc
            o16_ref[cols, :] = acc.astype(BF16)

    out = pl.BlockSpec((None, r, c), lambda j: (j, 0, 0))
    return pl.pallas_call(
        body, name=name, grid=(N_CHIPS,), in_specs=in_specs, out_specs=[out, out],
        out_shape=[_sds((N_CHIPS, r, c), F32), _sds((N_CHIPS, r, c), BF16)], compiler_params=_cp(("parallel",)),
    )(a, b)


def _rows_call(name, fn, ins, outs, s, tm=ROW_TILE, matmul=None, exchange=None):
    in_specs = []
    if matmul:
        a, w = matmul
        in_specs += [pl.BlockSpec((tm, a.shape[1]), lambda i: (i, 0)), _resident(w.shape)]
    for arr, kind in ins:
        if kind == "full":
            in_specs.append(pl.BlockSpec(arr.shape, lambda i: (0, 0)))
        else:
            _, w_, cb = kind
            in_specs.append(pl.BlockSpec((tm, w_), lambda i, cb=cb: (i, cb)))
    out_specs, out_shape, is_acc = [], [], []
    for w_, dt, kind in outs:
        if kind == "acc":
            out_specs.append(pl.BlockSpec((1, w_), lambda i: (0, 0)))
            out_shape.append(_sds((1, w_), dt))
        else:
            out_specs.append(pl.BlockSpec((tm, w_), lambda i: (i, 0)))
            out_shape.append(_sds((s, w_), dt))
        is_acc.append(kind == "acc")
    n_mm, n_in, n_out = (2 if matmul else 0), len(ins), len(outs)
    x_ins, x_shapes, x_sems, x_start, x_finish = exchange if exchange else ((), [], [], None, None)
    n_x = len(x_ins)
    steps = s // tm

    def body(*refs):
        in_refs, xi = refs[n_mm:n_mm + n_in], refs[n_mm + n_in:n_mm + n_in + n_x]
        out_refs = refs[n_mm + n_in + n_x:n_mm + n_in + n_x + n_out]
        xo, sems = refs[n_mm + n_in + n_x + n_out:n_mm + n_in + 2 * n_x + n_out], refs[n_mm + n_in + 2 * n_x + n_out:]
        i = pl.program_id(0)

        if exchange:
            @pl.when(i == 0)
            def _():
                x_start(xi, xo, *sems)

        for o, acc in zip(out_refs, is_acc):
            if acc:
                @pl.when(i == 0)
                def _(o=o):
                    o[...] = jnp.zeros_like(o)

        parts = 2 if matmul else 1
        for h in range(parts):
            rows = slice(h * (tm // parts), (h + 1) * (tm // parts))
            args = [r[...] if kind == "full" else r[rows, :] for r, (_, kind) in zip(in_refs, ins)]
            if matmul:
                a_ref, w_ref = refs[:2]
                if len(w_ref.shape) == 2:
                    acc = _dot(a_ref[rows, :], w_ref[...], NN)
                else:
                    kj = w_ref.shape[2]
                    acc = _dot(a_ref[rows, 0:kj], w_ref[0], NT)
                    for j in range(1, w_ref.shape[0]):
                        acc = acc + _dot(a_ref[rows, j * kj:(j + 1) * kj], w_ref[j], NT)
                args.insert(0, acc)
            for o, r, acc in zip(out_refs, fn(*args), is_acc):
                if acc:
                    o[...] += r.astype(o.dtype)
                else:
                    o[rows, :] = r.astype(o.dtype)

        if exchange:
            @pl.when(i == steps - 1)
            def _():
                x_finish(xi, xo, *sems)

    sem = ("arbitrary",) if any(is_acc) or exchange else ("parallel",)
    return pl.pallas_call(
        body, name=name, grid=(steps,), in_specs=in_specs + [ANY] * n_x, out_specs=out_specs + [ANY] * n_x,
        out_shape=out_shape + list(x_shapes), scratch_shapes=list(x_sems), compiler_params=_cp(sem),
    )(*(matmul or ()), *[a for a, _ in ins], *x_ins)


def _rstd(v):
    return lax.rsqrt(jnp.mean(v * v, axis=-1, keepdims=True) + RMS_EPS)


def _norm_bwd(v, gain, dy):
    r = _rstd(v)
    n = v * r
    dn = dy * gain
    dv = r * (dn - n * jnp.mean(dn * n, axis=-1, keepdims=True))
    return dv, dy * n


def _colsum(v):
    return jnp.sum(v, axis=0, keepdims=True)


def _row(w, cb=0):
    return ("row", w, cb)


N_PAIRS = ATTN_W // BLK


def _head_col(v, mask):
    return jnp.max(jnp.where(mask, v, -jnp.inf), axis=1, keepdims=True)


def _slopes():
    t = np.zeros((N_PAIRS, 8, 2 * BLK), np.float32)
    for p in range(N_PAIRS):
        for hh in range(2):
            t[p, hh, :] = 2.0 ** -(2 * p + hh + 1)
    return jnp.asarray(t)


def _rows(n, r, d):
    base = pl.multiple_of(n * (BLK * d), BLK)
    return pl.ds(base + r, BLK, stride=d) if d > 1 else pl.ds(base, BLK)


def _attn_bias(sl_ref, bias_scr):
    row = lax.broadcasted_iota(jnp.int32, (BLK, 2 * BLK), 0)
    col = lax.broadcasted_iota(jnp.int32, (BLK, 2 * BLK), 1)
    dist = row + BLK - col
    in_window = (dist >= 0) & (dist <= BLK)
    distf = dist.astype(F32)
    for di, d in enumerate(DILATIONS):
        for hh in range(2):
            bias_scr[2 * di + hh] = jnp.where(in_window, -(sl_ref[hh:hh + 1, :] * float(d)) * distf, -1e30)


def _first_block_penalty(n):
    col = lax.broadcasted_iota(jnp.int32, (1, 2 * BLK), 1)
    return jnp.where(col + n * BLK >= BLK, 0.0, -1e30)


def _attn_groups(s, d):
    nb = s // (BLK * d)
    g = ATTN_GROUP
    if d >= g:
        return [(nb, lambda n, r0=r0: [(n, r0 + u) for u in range(g)]) for r0 in range(0, d, g)]
    per = g // d
    return [(nb // per, lambda t: [(per * t + u, r) for u in range(per) for r in range(d)])]


def _attn_fwd(proj, shards):
    s = proj.shape[0]
    nk = len(shards)

    def body(sl_ref, q_ref, k_ref, v_ref, *rest):
        w_refs, (o_ref, l_ref) = rest[:nk], rest[nk:nk + 2]
        wg_refs, (bias_scr, ssem, rsem) = rest[nk + 2:2 * nk + 2], rest[2 * nk + 2:]
        pair = pl.program_id(0)

        @pl.when(pair == 0)
        def _():
            _ag_start(w_refs, wg_refs, ssem, rsem)

        _attn_bias(sl_ref, bias_scr)
        lane_q = lax.broadcasted_iota(jnp.int32, (BLK, BLK), 1) < 64
        lane_k = lax.broadcasted_iota(jnp.int32, (2 * BLK, BLK), 1) < 64

        def branch(n, r, di):
            d = DILATIONS[di]
            rows = _rows(n, r, d)
            prev = _rows(jnp.maximum(n - 1, 0), r, d)
            pen = _first_block_penalty(n)
            q2 = q_ref[rows, :] * ATTN_SCALE
            kk = jnp.concatenate([k_ref[prev, :], k_ref[rows, :]], axis=0).astype(BF16)
            vv = jnp.concatenate([v_ref[prev, :], v_ref[rows, :]], axis=0)
            o2 = jnp.zeros((BLK, BLK), F32)
            lse2 = jnp.zeros((BLK, BLK), F32)
            for hh in range(2):
                mq = lane_q if hh == 0 else ~lane_q
                mk = lane_k if hh == 0 else ~lane_k
                qm = jnp.where(mq, q2, 0.0).astype(BF16)
                sc = _dot(qm, kk, NT) + bias_scr[2 * di + hh] + pen
                m = jnp.max(sc, axis=1, keepdims=True)
                pr = jnp.exp(sc - m)
                den = jnp.sum(pr, axis=1, keepdims=True)
                vm = jnp.where(mk, vv, 0.0).astype(BF16)
                o2 = o2 + _dot(pr.astype(BF16), vm, NN) / den
                lse2 = jnp.where(mq, m + jnp.log(den), lse2)
            return rows, o2, lse2

        def merge(rows, o2, lse2, first):
            if first:
                o_ref[rows, :] = o2
                l_ref[rows, :] = lse2
            else:
                lo = l_ref[rows, :]
                mx = jnp.maximum(lo, lse2)
                ln = mx + jnp.log(jnp.exp(lo - mx) + jnp.exp(lse2 - mx))
                o_ref[rows, :] = jnp.exp(lo - ln) * o_ref[rows, :] + jnp.exp(lse2 - ln) * o2
                l_ref[rows, :] = ln

        for di, d in enumerate(DILATIONS):
            for trips, blocks in _attn_groups(s, d):
                def trip(t, carry, di=di, blocks=blocks):
                    done = [branch(n, r, di) for n, r in blocks(t)]
                    for rows, o2, lse2 in done:
                        merge(rows, o2, lse2, di == 0)
                    return carry

                lax.fori_loop(0, trips, trip, 0)

        @pl.when(pair == N_PAIRS - 1)
        def _():
            _ag_finish(w_refs, wg_refs, ssem, rsem)

    cb = lambda base: pl.BlockSpec((s, BLK), lambda p, base=base: (0, base + p))
    out = pl.BlockSpec((s, BLK), lambda p: (0, p))
    ag_shape, ag_sems = _ag_shapes(shards)
    return pl.pallas_call(
        body, name="attn_fwd", grid=(N_PAIRS,),
        in_specs=[pl.BlockSpec((None, 8, 2 * BLK), lambda p: (p, 0, 0)), cb(0), cb(N_PAIRS), cb(2 * N_PAIRS)]
        + [ANY] * nk,
        out_specs=[out, out] + [ANY] * nk, out_shape=[_sds((s, ATTN_W), F32)] * 2 + ag_shape,
        scratch_shapes=[pltpu.VMEM((2 * len(DILATIONS), BLK, 2 * BLK), F32)] + ag_sems,
        compiler_params=_cp(("arbitrary",)),
    )(_slopes(), proj, proj, proj, *shards)


def _attn_bwd(proj, do, lse, delta, psums):
    s = proj.shape[0]
    nk = len(psums)

    def body(sl_ref, q_ref, k_ref, v_ref, do_ref, l_ref, e_ref, *rest):
        p_refs, out16 = rest[:nk], rest[nk:nk + 3]
        got_refs, (dq_ref, dk_ref, dv_ref, bias_scr, ssem, rsem) = rest[nk + 3:2 * nk + 3], rest[2 * nk + 3:]
        pair = pl.program_id(0)

        @pl.when(pair == 0)
        def _():
            _rs_chips_start(p_refs, got_refs, ssem, rsem)

        _attn_bias(sl_ref, bias_scr)
        lane_q = lax.broadcasted_iota(jnp.int32, (BLK, BLK), 1) < 64
        lane_k = lax.broadcasted_iota(jnp.int32, (2 * BLK, BLK), 1) < 64
        dk_ref[...] = jnp.zeros_like(dk_ref)
        dv_ref[...] = jnp.zeros_like(dv_ref)

        def branch(n, r, di):
            d = DILATIONS[di]
            rows = _rows(n, r, d)
            prev = _rows(jnp.maximum(n - 1, 0), r, d)
            pen = _first_block_penalty(n)
            q1, d1, l1, e1 = q_ref[rows, :] * ATTN_SCALE, do_ref[rows, :], l_ref[rows, :], e_ref[rows, :]
            kk = jnp.concatenate([k_ref[prev, :], k_ref[rows, :]], axis=0)
            kkb = kk.astype(BF16)
            vvb = jnp.concatenate([v_ref[prev, :], v_ref[rows, :]], axis=0).astype(BF16)
            dq2 = jnp.zeros((BLK, BLK), F32)
            dkk = jnp.zeros((2 * BLK, BLK), F32)
            dvv = jnp.zeros((2 * BLK, BLK), F32)
            for hh in range(2):
                mq = lane_q if hh == 0 else ~lane_q
                mk = lane_k if hh == 0 else ~lane_k
                qm = jnp.where(mq, q1, 0.0).astype(BF16)
                dm = jnp.where(mq, d1, 0.0).astype(BF16)
                sc = _dot(qm, kkb, NT) + bias_scr[2 * di + hh] + pen
                pr = jnp.exp(sc - _head_col(l1, mq))
                ds = (pr * (_dot(dm, vvb, NT) - _head_col(e1, mq))).astype(BF16)
                km = jnp.where(mk, kk, 0.0).astype(BF16)
                dq2 = dq2 + _dot(ds, km, NN)
                dkk = dkk + _dot(ds, qm, TN)
                dvv = dvv + _dot(pr.astype(BF16), dm, TN)
            return rows, prev, dq2 * ATTN_SCALE, dkk, dvv

        for di, d in enumerate(DILATIONS):
            for trips, blocks in _attn_groups(s, d):
                def trip(t, carry, di=di, blocks=blocks, first=(di == 0)):
                    done = [branch(n, r, di) for n, r in blocks(t)]
                    for rows, prev, dq2, dkk, dvv in done:
                        dq_ref[rows, :] = dq2 if first else dq_ref[rows, :] + dq2
                        dk_ref[prev, :] = dk_ref[prev, :] + dkk[:BLK]
                        dk_ref[rows, :] = dk_ref[rows, :] + dkk[BLK:]
                        dv_ref[prev, :] = dv_ref[prev, :] + dvv[:BLK]
                        dv_ref[rows, :] = dv_ref[rows, :] + dvv[BLK:]
                    return carry

                lax.fori_loop(0, trips, trip, 0)

        for o16, acc in zip(out16, (dq_ref, dk_ref, dv_ref)):
            o16[...] = acc[...].astype(BF16)

        @pl.when(pair == N_PAIRS - 1)
        def _():
            _rs_chips_finish(p_refs, got_refs, ssem, rsem)

    cb = lambda base: pl.BlockSpec((s, BLK), lambda p, base=base: (0, base + p))
    out = pl.BlockSpec((s, BLK), lambda p: (0, p))
    rs_shape, rs_sems = _rs_chips_shapes(psums)
    return pl.pallas_call(
        body, name="attn_bwd", grid=(N_PAIRS,),
        in_specs=[pl.BlockSpec((None, 8, 2 * BLK), lambda p: (p, 0, 0)), cb(0), cb(N_PAIRS), cb(2 * N_PAIRS),
                  out, out, out] + [ANY] * nk,
        out_specs=[out] * 3 + [ANY] * nk, out_shape=[_sds((s, ATTN_W), BF16)] * 3 + rs_shape,
        scratch_shapes=[pltpu.VMEM((s, BLK), F32)] * 3 + [pltpu.VMEM((2 * len(DILATIONS), BLK, 2 * BLK), F32)] + rs_sems,
        compiler_params=_cp(("arbitrary",)),
    )(_slopes(), proj, proj, proj, do, lse, delta, *psums)


def _lower_bound(lbl):
    return 1.0 / (1.0 + jnp.exp(lbl[1:2, :] - lbl[0:1, :]))


def _hi(a):
    bits = lax.bitcast_convert_type(a, jnp.uint32) & jnp.uint32(0xFFFF0000)
    return lax.bitcast_convert_type(bits, F32)


def _dot3(a, b, contract):
    ah, bh = _hi(a), _hi(b)
    al, bl = (a - ah).astype(BF16), (b - bh).astype(BF16)
    ah, bh = ah.astype(BF16), bh.astype(BF16)
    return _dot(ah, bh, contract) + (_dot(ah, bl, contract) + _dot(al, bh, contract))


def _cumsum_rows(tri, g):
    g1 = _hi(g)
    r1 = g - g1
    g2 = _hi(r1)
    g3 = r1 - g2
    return _dot(tri, g1.astype(BF16), NN) + (_dot(tri, g2.astype(BF16), NN) + _dot(tri, g3.astype(BF16), NN))


def _heads(fn):
    return jnp.concatenate([fn(slice(h * BLK, (h + 1) * BLK)) for h in range(HGRN_W // BLK)], axis=1)


def _head_mean(t):
    return _heads(lambda hs: jnp.broadcast_to(jnp.mean(t[:, hs], axis=1, keepdims=True), (t.shape[0], BLK)))


def _hgrn_chunk(q_ref, f_ref, i_ref, sl, lb, tri):
    qp = q_ref[sl, :]
    sq = _sigmoid(qp)
    qf = qp * sq
    sg = _sigmoid(f_ref[sl, :])
    f = lb + (1.0 - lb) * sg
    kf = 1.0 - f
    v = i_ref[sl, :]
    b = _cumsum_rows(tri, jnp.log(f))
    bm = b[CHUNK // 2:CHUNK // 2 + 1, :]
    bl = b[CHUNK - 1:CHUNK, :]
    qt = qf * jnp.exp(b - bm)
    kt = kf * jnp.exp(bm - b)
    return qp, sq, qf, sg, f, kf, v, b, bm, bl, qt, kt


def _hgrn_specs(tb, block):
    first = 3 * ATTN_W // HGRN_W
    return [pl.BlockSpec((tb, HGRN_W), lambda i, k=k: (block(i), first + k)) for k in range(4)]


def _hgrn_fwd(proj, lb_logits, out_gain):
    s = proj.shape[0]
    tb = min(HGRN_TB, s)
    nb, cpb, nc = s // tb, tb // CHUNK, s // CHUNK

    def body(q_ref, f_ref, i_ref, g_ref, lbl_ref, gain_ref, o_ref, rec_ref, st_ref, st_scr):
        step = pl.program_id(0)

        @pl.when(step == 0)
        def _():
            st_scr[...] = jnp.zeros_like(st_scr)

        lb = _lower_bound(lbl_ref[...])
        r64 = lax.broadcasted_iota(jnp.int32, (CHUNK, CHUNK), 0)
        c64 = lax.broadcasted_iota(jnp.int32, (CHUNK, CHUNK), 1)
        tril = r64 >= c64
        tri = tril.astype(BF16)
        st = st_scr[...]
        for cc in range(cpb):
            sl = slice(cc * CHUNK, (cc + 1) * CHUNK)
            _, _, qf, _, _, kf, v, b, _, bl, qt, kt = _hgrn_chunk(q_ref, f_ref, i_ref, sl, lb, tri)
            qe = (qf * jnp.exp(b)).astype(BF16)
            kh = (kf * jnp.exp(bl - b)).astype(BF16)
            qtb, ktb, vb, stb = qt.astype(BF16), kt.astype(BF16), v.astype(BF16), st.astype(BF16)

            def out_h(hs):
                a = jnp.where(tril, _dot(qtb[:, hs], ktb[:, hs], NT), 0.0).astype(BF16)
                return _dot(qe[:, hs], stb[:, hs], NT) + _dot(a, vb[:, hs], NN)

            o_ref[sl, :] = _heads(out_h)
            st_ref[cc] = stb
            st = st * jnp.exp(bl) + _heads(lambda hs: _dot(vb[:, hs], kh[:, hs], TN))
        st_scr[...] = st
        o = o_ref[...]
        gate = g_ref[...]
        rec_ref[...] = (o * lax.rsqrt(_head_mean(o * o) + RMS_EPS) * gain_ref[...] * (gate * _sigmoid(gate))).astype(BF16)

    row = pl.BlockSpec((tb, HGRN_W), lambda i: (i, 0))
    return pl.pallas_call(
        body, name="hgrn_fwd", grid=(nb,),
        in_specs=_hgrn_specs(tb, lambda i: i) + [pl.BlockSpec((2, HGRN_W), lambda i: (0, 0)),
                                                 pl.BlockSpec((1, HGRN_W), lambda i: (0, 0))],
        out_specs=[row, row, pl.BlockSpec((cpb, BLK, HGRN_W), lambda i: (i, 0, 0))],
        out_shape=[_sds((s, HGRN_W), F32), _sds((s, HGRN_W), BF16), _sds((nc, BLK, HGRN_W), BF16)],
        scratch_shapes=[pltpu.VMEM((BLK, HGRN_W), F32)],
        compiler_params=_cp(("arbitrary",)),
    )(proj, proj, proj, proj, lb_logits, out_gain)


def _hgrn_bwd(proj, o_pre, states, drec, lb_logits, out_gain):
    s = proj.shape[0]
    tb = min(HGRN_TB, s)
    nb, cpb, nc = s // tb, tb // CHUNK, s // CHUNK

    def body(q_ref, f_ref, i_ref, g_ref, o_ref, st_ref, stn_ref, dy_ref, lbl_ref, gain_ref,
             dq_ref, df_ref, di_ref, dg_ref, dgain_ref, dlbl_ref, do_scr, dst_scr, dlb_scr):
        step = pl.program_id(0)

        @pl.when(step == 0)
        def _():
            dst_scr[...] = jnp.zeros_like(dst_scr)
            dlb_scr[...] = jnp.zeros_like(dlb_scr)
            dgain_ref[...] = jnp.zeros_like(dgain_ref)

        lb = _lower_bound(lbl_ref[...])
        gain = gain_ref[...]
        o = o_ref[...]
        r = lax.rsqrt(_head_mean(o * o) + RMS_EPS)
        nrm = o * r
        gate = g_ref[...]
        sgt = _sigmoid(gate)
        dy = dy_ref[...]
        dg_ref[...] = (dy * nrm * gain * (sgt * (1.0 + gate * (1.0 - sgt)))).astype(BF16)
        dng = dy * (gate * sgt)
        dgain_ref[...] += _colsum(dng * nrm)
        dn = dng * gain
        do_scr[...] = r * (dn - nrm * _head_mean(dn * nrm))

        r64 = lax.broadcasted_iota(jnp.int32, (CHUNK, CHUNK), 0)
        c64 = lax.broadcasted_iota(jnp.int32, (CHUNK, CHUNK), 1)
        tril = r64 >= c64
        tri = tril.astype(BF16)
        triu = (r64 <= c64).astype(BF16)
        dst = dst_scr[...]
        dlb = dlb_scr[...]
        for cc in reversed(range(cpb)):
            sl = slice(cc * CHUNK, (cc + 1) * CHUNK)
            qp, sq, qf, sg, f, kf, v, b, bm, bl, qt, kt = _hgrn_chunk(q_ref, f_ref, i_ref, sl, lb, tri)
            stf = st_ref[cc].astype(F32)
            st_end = (st_ref[cc + 1] if cc + 1 < cpb else stn_ref[0]).astype(F32)
            csum = jnp.sum(st_end * dst, axis=0, keepdims=True)
            doc = do_scr[sl, :]
            dob, dstb = doc.astype(BF16), dst.astype(BF16)
            eb = jnp.exp(b)
            qe = (qf * eb).astype(BF16)
            kh = (kf * jnp.exp(bl - b)).astype(BF16)
            qtb, ktb = qt.astype(BF16), kt.astype(BF16)
            parts = []
            for h in range(HGRN_W // BLK):
                hs = slice(h * BLK, (h + 1) * BLK)
                da = jnp.where(tril, _dot3(doc[:, hs], v[:, hs], NT), 0.0)
                a = jnp.where(tril, _dot(qtb[:, hs], ktb[:, hs], NT), 0.0).astype(BF16)
                parts.append((
                    _dot3(da, kt[:, hs], NN), _dot3(doc[:, hs], stf[:, hs], NN),
                    _dot3(da, qt[:, hs], TN), _dot3(v[:, hs], dst[:, hs], NN),
                    _dot(a, dob[:, hs], TN) + _dot(kh[:, hs], dstb[:, hs], NT),
                    _dot(dob[:, hs], qe[:, hs], TN)))
            dqt, dqi, dkt, dks, dv, upd = (jnp.concatenate([p[n] for p in parts], axis=1) for n in range(6))
            dqf = dqt * jnp.exp(b - bm) + eb * dqi
            dkf = dkt * jnp.exp(bm - b) + jnp.exp(bl - b) * dks
            gq = qf * dqf - kf * dkf
            dlogf = csum + _cumsum_rows(triu, gq)
            dfv = dlogf / f - dkf
            dq_ref[sl, :] = (dqf * (sq * (1.0 + qp * (1.0 - sq)))).astype(BF16)
            df_ref[sl, :] = (dfv * (1.0 - lb) * sg * (1.0 - sg)).astype(BF16)
            di_ref[sl, :] = dv.astype(BF16)
            dst = dst * jnp.exp(bl) + upd
            dlb = dlb + _colsum(dfv * (1.0 - sg))
        dst_scr[...] = dst
        dlb_scr[...] = dlb

        @pl.when(step == nb - 1)
        def _():
            t = dlb * lb * (1.0 - lb)
            dlbl_ref[...] = jnp.concatenate([t, -t], axis=0)

    rev = lambda i: nb - 1 - i
    row = pl.BlockSpec((tb, HGRN_W), lambda i: (rev(i), 0))
    res = pl.pallas_call(
        body, name="hgrn_bwd", grid=(nb,),
        in_specs=_hgrn_specs(tb, rev) + [
            row, pl.BlockSpec((cpb, BLK, HGRN_W), lambda i: (rev(i), 0, 0)),
            pl.BlockSpec((1, BLK, HGRN_W), lambda i: (jnp.minimum((rev(i) + 1) * cpb, nc - 1), 0, 0)),
            row, pl.BlockSpec((2, HGRN_W), lambda i: (0, 0)), pl.BlockSpec((1, HGRN_W), lambda i: (0, 0))],
        out_specs=[row, row, row, row, pl.BlockSpec((1, HGRN_W), lambda i: (0, 0)),
                   pl.BlockSpec((2, HGRN_W), lambda i: (0, 0))],
        out_shape=[_sds((s, HGRN_W), BF16)] * 4 + [_sds((1, HGRN_W), F32), _sds((2, HGRN_W), F32)],
        scratch_shapes=[pltpu.VMEM((tb, HGRN_W), F32), pltpu.VMEM((BLK, HGRN_W), F32), pltpu.VMEM((1, HGRN_W), F32)],
        compiler_params=_cp(("arbitrary",)),
    )(proj, proj, proj, proj, o_pre, states, states, drec, lb_logits, out_gain)
    return res


def _place():
    return lax.axis_index("x"), lax.axis_index("y"), lax.axis_index("c")


def _flip(x, y, ox, oy):
    return (1 - x if ox else x), (1 - y if oy else y)


def _half(rows, cc):
    return pl.ds(cc * (rows // 2), rows // 2)


def _remote(src, dst, ssem, rsem, to):
    return pltpu.make_async_remote_copy(src_ref=src, dst_ref=dst, send_sem=ssem, recv_sem=rsem,
                                        device_id=to, device_id_type=MESH)


def _ag_chip_copies(ins, outs, ssem, rsem):
    x, y, c = _place()
    j = 2 * x + y
    cps = []
    for k in range(len(ins)):
        rows = ins[k].shape[0]
        for idx, (ox, oy) in enumerate(FLIPS):
            px, py = _flip(x, y, ox, oy)
            cps.append(_remote(ins[k].at[_half(rows, c)], outs[k].at[j, _half(rows, c)],
                               ssem.at[k, idx], rsem.at[k, idx], (px, py, c)))
    return cps


def _ag_start(ins, outs, ssem, rsem):
    for cp in _ag_chip_copies(ins, outs, ssem, rsem):
        cp.start()


def _ag_finish(ins, outs, ssem, rsem):
    x, y, c = _place()
    sib = (x, y, 1 - c)
    passed = []
    for k in range(len(ins)):
        rows = ins[k].shape[0]
        for idx, (ox, oy) in enumerate(FLIPS):
            px, py = _flip(x, y, ox, oy)
            blk = outs[k].at[2 * px + py, _half(rows, c)]
            _remote(blk, blk, ssem.at[k, idx], rsem.at[k, idx], (px, py, c)).wait_recv()
            cp = _remote(blk, blk, ssem.at[k, 3 + idx], rsem.at[k, 3 + idx], sib)
            cp.start()
            passed.append(cp)
    for k in range(len(ins)):
        rows = ins[k].shape[0]
        for idx, (ox, oy) in enumerate(FLIPS):
            px, py = _flip(x, y, ox, oy)
            blk = outs[k].at[2 * px + py, _half(rows, 1 - c)]
            _remote(blk, blk, ssem.at[k, 3 + idx], rsem.at[k, 3 + idx], sib).wait_recv()
    for cp in _ag_chip_copies(ins, outs, ssem, rsem) + passed:
        cp.wait_send()


def _ag_shapes(shards):
    nk = len(shards)
    return ([_sds((N_CHIPS,) + tuple(w.shape), w.dtype) for w in shards],
            [pltpu.SemaphoreType.DMA((nk, 6)), pltpu.SemaphoreType.DMA((nk, 6))])


def _with_own(gathered, shard, j):
    return lax.dynamic_update_index_in_dim(gathered, shard, j, 0)


def _rs_pair_copies(ins, outs, ssem, rsem):
    x, y, c = _place()
    return [_remote(ins[k].at[:, _half(ins[k].shape[1], 1 - c)], outs[k], ssem.at[k], rsem.at[k], (x, y, 1 - c))
            for k in range(len(ins))]


def _rs_pair_start(ins, outs, ssem, rsem):
    for cp in _rs_pair_copies(ins, outs, ssem, rsem):
        cp.start()


def _rs_pair_finish(ins, outs, ssem, rsem):
    for cp in _rs_pair_copies(ins, outs, ssem, rsem):
        cp.wait()


def _rs_pair_exchange(grads):
    nk = len(grads)
    return (grads, [_sds((N_CHIPS, g.shape[1] // 2, g.shape[2]), g.dtype) for g in grads],
            [pltpu.SemaphoreType.DMA((nk,)), pltpu.SemaphoreType.DMA((nk,))], _rs_pair_start, _rs_pair_finish)


def _rs_pair(name, grads):
    nk = len(grads)
    ins, out_shape, sems, start, finish = _rs_pair_exchange(grads)

    def body(*refs):
        start(refs[:nk], refs[nk:2 * nk], *refs[2 * nk:])
        finish(refs[:nk], refs[nk:2 * nk], *refs[2 * nk:])

    return pl.pallas_call(body, name=name, in_specs=[ANY] * nk, out_specs=[ANY] * nk, out_shape=out_shape,
                          scratch_shapes=sems)(*ins)


def _rs_chip_copies(ins, outs, ssem, rsem):
    x, y, c = _place()
    cps = []
    for k in range(len(ins)):
        for idx, (ox, oy) in enumerate(FLIPS):
            px, py = _flip(x, y, ox, oy)
            cps.append(_remote(ins[k].at[2 * px + py], outs[k].at[idx], ssem.at[k, idx], rsem.at[k, idx], (px, py, c)))
    return cps


def _rs_chips_start(ins, outs, ssem, rsem):
    for cp in _rs_chip_copies(ins, outs, ssem, rsem):
        cp.start()


def _rs_chips_finish(ins, outs, ssem, rsem):
    for cp in _rs_chip_copies(ins, outs, ssem, rsem):
        cp.wait()


def _rs_chips_shapes(psums):
    nk = len(psums)
    return ([_sds((3,) + tuple(p.shape[1:]), p.dtype) for p in psums],
            [pltpu.SemaphoreType.DMA((nk, 3)), pltpu.SemaphoreType.DMA((nk, 3))])


def _rs_share(fulls):
    nk = len(fulls)

    def body(*refs):
        ins, outs = refs[:nk], refs[nk:2 * nk]
        ssem, rsem = refs[2 * nk:]
        x, y, c = _place()
        cps = []
        for k in range(nk):
            rows = fulls[k].shape[0]
            cp = _remote(ins[k].at[_half(rows, c)], outs[k].at[_half(rows, c)], ssem.at[k], rsem.at[k], (x, y, 1 - c))
            cp.start()
            cps.append(cp)
        for k, cp in enumerate(cps):
            rows = fulls[k].shape[0]
            cp.wait_send()
            theirs = outs[k].at[_half(rows, 1 - c)]
            _remote(theirs, theirs, ssem.at[k], rsem.at[k], (x, y, 1 - c)).wait_recv()

    return pl.pallas_call(
        body, name="rs_share", in_specs=[ANY] * nk, out_specs=[ANY] * nk,
        out_shape=[_sds(f.shape, f.dtype) for f in fulls], input_output_aliases={k: k for k in range(nk)},
        scratch_shapes=[pltpu.SemaphoreType.DMA((nk,)), pltpu.SemaphoreType.DMA((nk,))],
    )(*fulls)


def _allreduce_small(v):
    ndev = 8

    def body(in_ref, out_ref, buf, ssem, rsem):
        x, y, c = _place()
        me = 4 * x + 2 * y + c
        buf[me] = in_ref[...]
        cps = []
        for k in range(1, ndev):
            ox, oy, oc = (k >> 2) & 1, (k >> 1) & 1, k & 1
            px, py = _flip(x, y, ox, oy)
            pc = 1 - c if oc else c
            cp = pltpu.make_async_remote_copy(src_ref=in_ref, dst_ref=buf.at[me], send_sem=ssem.at[k - 1],
                                              recv_sem=rsem.at[k - 1], device_id=(px, py, pc), device_id_type=MESH)
            cp.start()
            cps.append((cp, 4 * px + 2 * py + pc, (px, py, pc)))
        for k, (cp, src, peer) in enumerate(cps):
            cp.wait_send()
            pltpu.make_async_remote_copy(src_ref=in_ref, dst_ref=buf.at[src], send_sem=ssem.at[k],
                                         recv_sem=rsem.at[k], device_id=peer, device_id_type=MESH).wait_recv()
        acc = buf[0]
        for i in range(1, ndev):
            acc = acc + buf[i]
        out_ref[...] = acc

    return pl.pallas_call(
        body, name="allreduce_small",
        in_specs=[pl.BlockSpec(memory_space=pltpu.VMEM)], out_specs=pl.BlockSpec(memory_space=pltpu.VMEM),
        out_shape=_sds(v.shape, v.dtype),
        scratch_shapes=[pltpu.VMEM((ndev,) + v.shape, v.dtype), pltpu.SemaphoreType.DMA((ndev - 1,)),
                        pltpu.SemaphoreType.DMA((ndev - 1,))],
    )(v)


def _rs_sum1(name, g, recv, jc_idx):
    _, r, cdim = g.shape
    hr = r // 2
    tr = min(hr, 256)
    nr = hr // tr

    def body(jc_ref, g_ref, r_ref, o32_ref, o16_ref):
        v = g_ref[...] + r_ref[...].astype(F32)
        o16_ref[...] = v.astype(BF16)

        @pl.when(pl.program_id(1) == jc_ref[0])
        def _():
            o32_ref[...] = v

    spec = pl.BlockSpec((None, tr, cdim), lambda i, j, jc: (j, i, 0))
    return pl.pallas_call(
        body, name=name,
        grid_spec=pltpu.PrefetchScalarGridSpec(
            num_scalar_prefetch=1, grid=(nr, N_CHIPS),
            in_specs=[pl.BlockSpec((None, tr, cdim), lambda i, j, jc: (j, jc[1] * nr + i, 0)), spec],
            out_specs=[pl.BlockSpec((tr, cdim), lambda i, j, jc: (i, 0)), spec]),
        out_shape=[_sds((hr, cdim), F32), _sds((N_CHIPS, hr, cdim), BF16)],
        compiler_params=_cp(("parallel", "arbitrary")),
    )(jc_idx, g, recv)


def _rs_sum2(name, p32, recv, jc_idx):
    hr, cdim = p32.shape
    tr = min(hr, 256)
    nr = hr // tr

    def body(jc_ref, p_ref, r_ref, o_ref):
        o_ref[...] = ((p_ref[...] + r_ref[0].astype(F32)) + r_ref[1].astype(F32)) + r_ref[2].astype(F32)

    return pl.pallas_call(
        body, name=name,
        grid_spec=pltpu.PrefetchScalarGridSpec(
            num_scalar_prefetch=1, grid=(nr,),
            in_specs=[pl.BlockSpec((tr, cdim), lambda i, jc: (i, 0)),
                      pl.BlockSpec((3, tr, cdim), lambda i, jc: (0, i, 0))],
            out_specs=pl.BlockSpec((tr, cdim), lambda i, jc: (jc[1] * nr + i, 0))),
        out_shape=_sds((2 * hr, cdim), F32),
        compiler_params=_cp(("parallel",)),
    )(jc_idx, p32, recv)


def _adamw(name, w, g, m, v):
    r, cdim = w.shape
    tr = min(r, 256)
    c1 = 1.0 - ADAM_B1 ** ADAM_STEP
    c2 = 1.0 - ADAM_B2 ** ADAM_STEP

    def body(w_ref, g_ref, m_ref, v_ref, d_ref, nm_ref, nv_ref):
        gv = g_ref[...]
        nm = ADAM_B1 * m_ref[...] + (1.0 - ADAM_B1) * gv
        nv = ADAM_B2 * v_ref[...] + (1.0 - ADAM_B2) * (gv * gv)
        d_ref[...] = -ADAM_LR * ((nm / c1) / (jnp.sqrt(nv / c2) + ADAM_EPS) + ADAM_WD * w_ref[...])
        nm_ref[...] = nm
        nv_ref[...] = nv

    spec = pl.BlockSpec((tr, cdim), lambda i: (i, 0))
    return pl.pallas_call(
        body, name=name, grid=(r // tr,), in_specs=[spec] * 4, out_specs=[spec] * 3,
        out_shape=[_sds((r, cdim), F32)] * 3, compiler_params=_cp(("parallel",)),
    )(w, g, m, v)


def _pack_small(mix_pre, attn_out, lb_logits, hgrn_out, mix_post, mlp_pre, mlp_post, extra=None):
    spare = jnp.zeros((1, D_MODEL), F32)
    rows = [mix_pre, jnp.concatenate([attn_out, hgrn_out], axis=1),
            jnp.concatenate([lb_logits[0:1], lb_logits[1:2]], axis=1), mix_post, mlp_pre, mlp_post,
            spare if extra is None else extra, spare]
    return jnp.concatenate(rows, axis=0)


def _unpack_small(p):
    return (p[0:1], p[1:2, :ATTN_W], jnp.concatenate([p[2:3, :HGRN_W], p[2:3, HGRN_W:]], axis=0),
            p[1:2, ATTN_W:], p[3:4], p[4:5], p[5:6])


def kernel(x, mix_pre_norm, w_in, attn_out_norm, hgrn_lb_logits, hgrn_out_norm, w_out, mix_post_norm, mlp_pre_norm, w_ff1, w_ff2, mlp_post_norm, loss_target, m_mix_pre_norm, m_w_in, m_attn_out_norm, m_hgrn_lb_logits, m_hgrn_out_norm, m_w_out, m_mix_post_norm, m_mlp_pre_norm, m_w_ff1, m_w_ff2, m_mlp_post_norm, v_mix_pre_norm, v_w_in, v_attn_out_norm, v_hgrn_lb_logits, v_hgrn_out_norm, v_w_out, v_mix_post_norm, v_mlp_pre_norm, v_w_ff1, v_w_ff2, v_mlp_post_norm):
    s = x.shape[1]
    xs = x.reshape(s, D_MODEL)
    tgt = loss_target.reshape(s, D_MODEL)
    cx, cy, cc = _place()
    chip = 2 * cx + cy
    jc_idx = jnp.stack([chip, cc]).astype(jnp.int32)

    big_w = [w_in[0], w_out[0], w_ff1[0], w_ff2[0]]
    big_m = [m_w_in[0], m_w_out[0], m_w_ff1[0], m_w_ff2[0]]
    big_v = [v_w_in[0], v_w_out[0], v_w_ff1[0], v_w_ff2[0]]
    shards = [w.astype(BF16) for w in big_w]

    h, wg_in = _rows_call("norm_in", lambda xv, g: ((xv * _rstd(xv) * g),),
                          [(xs, _row(D_MODEL)), (mix_pre_norm, "full")], [(D_MODEL, BF16, "row")], s,
                          exchange=(shards[:1], *_ag_shapes(shards[:1]), _ag_start, _ag_finish))
    wg_in = _with_own(wg_in, shards[0], chip)
    (proj,) = _mm_cols("mm_proj", h, wg_in, NN, [F32])
    hg_o, rec, states = _hgrn_fwd(proj, hgrn_lb_logits, hgrn_out_norm)
    attn_o, attn_lse, wg_out, wg_1, wg_2 = _attn_fwd(proj, shards[1:])
    wg_out, wg_1, wg_2 = (_with_own(g, w, chip) for g, w in zip((wg_out, wg_1, wg_2), shards[1:]))
    (attn_n,) = _rows_call("attn_norm", lambda o, gain: (o * _rstd(o) * gain,),
                           [(attn_o, _row(ATTN_W)), (attn_out_norm, "full")], [(ATTN_W, BF16, "row")], s)
    cat = jnp.concatenate([attn_n, rec], axis=1)

    def post1(mv, xv, g_post, g_pre2):
        x1 = xv + mv * _rstd(mv) * g_post
        return mv, x1, x1 * _rstd(x1) * g_pre2

    mixed, x1, h2 = _rows_call(
        "mm_mixed", post1, [(xs, _row(D_MODEL)), (mix_post_norm, "full"), (mlp_pre_norm, "full")],
        [(D_MODEL, F32, "row"), (D_MODEL, F32, "row"), (D_MODEL, BF16, "row")], s,
        matmul=(cat, wg_out.reshape(D_MODEL, D_MODEL)))

    def sq_relu(u):
        r = jnp.maximum(u, 0.0)
        return r * r, r

    act, ru = _mm_cols("mm_ff1", h2, wg_1, NN, [BF16, BF16], epi=sq_relu)

    def post2(fv, x1v, tv, g):
        y = x1v + fv * _rstd(fv) * g
        dy = (y - tv) * (1.0 / D_MODEL)
        err = y - tv
        loss = 0.5 * jnp.sum(jnp.mean(err * err, axis=-1, keepdims=True), axis=0, keepdims=True)
        dff, dgc = _norm_bwd(fv, g, dy)
        return dy, dff, _colsum(dgc), jnp.broadcast_to(loss, (1, BLK))

    dy, dff, g_mlp_post, loss_part = _rows_call(
        "mm_ff2", post2, [(x1, _row(D_MODEL)), (tgt, _row(D_MODEL)), (mlp_post_norm, "full")],
        [(D_MODEL, F32, "row"), (D_MODEL, BF16, "row"), (D_MODEL, F32, "acc"), (BLK, F32, "acc")], s,
        matmul=(act, wg_2.reshape(D_FF, D_MODEL)))

    (du,) = _mm_cols("mm_du", dff, wg_2, NT, [BF16], epi=lambda acc, r: (acc * (2.0 * r.astype(F32)),),
                     extras=(ru,))
    gw_2 = _mm_wgrad("mm_gw2", act, dff, True)
    gw_1 = _mm_wgrad("mm_gw1", h2, du, False)

    def bwd_mid(dh2v, dyv, x1v, mv, g_pre2, g_post):
        d1, gc1 = _norm_bwd(x1v, g_pre2, dh2v)
        dx1 = dyv + d1
        dm, gc2 = _norm_bwd(mv, g_post, dx1)
        return dx1, dm, _colsum(gc1), _colsum(gc2)

    dx1, dmixed, g_mlp_pre, g_mix_post, *from_pair = _rows_call(
        "mm_dh2", bwd_mid, [(dy, _row(D_MODEL)), (x1, _row(D_MODEL)), (mixed, _row(D_MODEL)),
                            (mlp_pre_norm, "full"), (mix_post_norm, "full")],
        [(D_MODEL, F32, "row"), (D_MODEL, BF16, "row"), (D_MODEL, F32, "acc"), (D_MODEL, F32, "acc")], s,
        matmul=(du, wg_1), exchange=_rs_pair_exchange([gw_1[1], gw_2[1]]))

    def attn_norm_bwd(dc, o, gain):
        do, gc = _norm_bwd(o, gain, dc[:, :ATTN_W])
        t = do * o
        lane = lax.broadcasted_iota(jnp.int32, (t.shape[0], BLK), 1) < 64
        parts = []
        for p in range(ATTN_W // BLK):
            tp = t[:, p * BLK:(p + 1) * BLK]
            sa = jnp.sum(jnp.where(lane, tp, 0.0), axis=1, keepdims=True)
            sb = jnp.sum(jnp.where(lane, 0.0, tp), axis=1, keepdims=True)
            parts.append(jnp.where(lane, sa, sb))
        return do, jnp.concatenate(parts, axis=1), dc[:, ATTN_W:], _colsum(gc)

    do_attn, delta, drec, g_attn_out = _rows_call(
        "mm_dcat", attn_norm_bwd, [(attn_o, _row(ATTN_W)), (attn_out_norm, "full")],
        [(ATTN_W, F32, "row"), (ATTN_W, F32, "row"), (HGRN_W, F32, "row"), (ATTN_W, F32, "acc")], s,
        matmul=(dmixed, wg_out.reshape(1, D_MODEL, D_MODEL)))
    gw_out = _mm_wgrad("mm_gwout", cat, dmixed, True)
    names = ["out", "ff1", "ff2", "in"]
    ready = [gw_out, gw_1, gw_2]
    from_pair = list(_rs_pair("rs_pair_out", [gw_out[1]])) + from_pair
    pair = [_rs_sum1(f"rs_sum1_{n}", g[0], r, jc_idx) for n, g, r in zip(names, ready, from_pair)]

    dq, dk, dv, *from_chips = _attn_bwd(proj, do_attn, attn_lse, delta, [p[1] for p in pair])
    dhq, dhf, dhi, dhg, g_hgrn_out, g_lb = _hgrn_bwd(proj, hg_o, states, drec, hgrn_lb_logits, hgrn_out_norm)

    dproj = jnp.concatenate([dq, dk, dv, dhq, dhf, dhi, dhg], axis=1)
    gw_in = _mm_wgrad("mm_gwin", h, dproj, False)
    (from_pair_in,) = _rs_pair("rs_pair_in", [gw_in[1]])
    pair.append(_rs_sum1("rs_sum1_in", gw_in[0], from_pair_in, jc_idx))
    rs_shape, rs_sems = _rs_chips_shapes([pair[3][1]])

    def bwd_in(dhv, dx1v, xv, g):
        d0, gc = _norm_bwd(xv, g, dhv)
        return dx1v + d0, _colsum(gc)

    grad_x, g_mix_pre, from_chips_in = _rows_call(
        "mm_dh", bwd_in, [(dx1, _row(D_MODEL)), (xs, _row(D_MODEL)), (mix_pre_norm, "full")],
        [(D_MODEL, F32, "row"), (D_MODEL, F32, "acc")], s, matmul=(dproj, wg_in),
        exchange=([pair[3][1]], rs_shape, rs_sems, _rs_chips_start, _rs_chips_finish))
    from_chips.append(from_chips_in)

    loss_row = jnp.pad(loss_part, ((0, 0), (0, D_MODEL - BLK)))
    small_g = _allreduce_small(_pack_small(g_mix_pre, g_attn_out, g_lb, g_hgrn_out, g_mix_post, g_mlp_pre, g_mlp_post,
                                           extra=loss_row))
    loss = small_g[6, 0]

    reduced = [_rs_sum2(f"rs_sum2_{n}", p[0], r, jc_idx) for n, p, r in zip(names, pair, from_chips)]
    g_wout, g_w1, g_w2, g_win = _rs_share(reduced)
    full = [g_win, g_wout, g_w1, g_w2]

    upd = [_adamw(f"adamw_{n}", w, g, m, v) for n, w, g, m, v in zip(("in", "out", "ff1", "ff2"), big_w, full, big_m, big_v)]
    small_w = _pack_small(mix_pre_norm, attn_out_norm, hgrn_lb_logits, hgrn_out_norm, mix_post_norm, mlp_pre_norm,
                          mlp_post_norm)
    small_m = _pack_small(m_mix_pre_norm, m_attn_out_norm, m_hgrn_lb_logits, m_hgrn_out_norm, m_mix_post_norm,
                          m_mlp_pre_norm, m_mlp_post_norm)
    small_v = _pack_small(v_mix_pre_norm, v_attn_out_norm, v_hgrn_lb_logits, v_hgrn_out_norm, v_mix_post_norm,
                          v_mlp_pre_norm, v_mlp_post_norm)
    small_upd = _adamw("adamw_small", small_w, small_g, small_m, small_v)

    def assemble(small, big):
        sm = _unpack_small(small)
        return (sm[0], big[0][None], sm[1], sm[2], sm[3], big[1][None], sm[4], sm[5], big[2][None], big[3][None], sm[6])

    g_out = assemble(small_g, full)
    d_out = assemble(small_upd[0], [u[0] for u in upd])
    m_out = assemble(small_upd[1], [u[1] for u in upd])
    v_out = assemble(small_upd[2], [u[2] for u in upd])
    return (loss, grad_x.reshape(x.shape), *g_out, *d_out, *m_out, *v_out)
```

```python
import numpy as np
import jax
import jax.numpy as jnp
from jax import lax
from jax.experimental import pallas as pl
from jax.experimental.pallas import tpu as pltpu

F32 = jnp.float32
BF16 = jnp.bfloat16
MESH = pl.DeviceIdType.MESH
ANY = pl.BlockSpec(memory_space=pl.ANY)

RMS_EPS = 1e-6
D_MODEL = 1024
ATTN_W = 512
HGRN_W = 512
D_FF = 4096
N_CHIPS = 4
BLK = 128
CHUNK = 64
HGRN_TB = 512
ATTN_GROUP = 8
DILATIONS = (1, 4, 16)
ATTN_SCALE = 0.125
ROW_TILE = 512
MM_TILE = 1024
VMEM_LIMIT = 48 * 2 ** 20
FLIPS = ((1, 0), (0, 1), (1, 1))

ADAM_LR, ADAM_B1, ADAM_B2, ADAM_EPS, ADAM_WD, ADAM_STEP = 0.001, 0.9, 0.999, 1e-08, 0.01, 10


def _cp(sem=None):
    return pltpu.CompilerParams(dimension_semantics=sem, vmem_limit_bytes=VMEM_LIMIT)


def _sigmoid(v):
    return 1.0 / (1.0 + jnp.exp(-v))


def _dot(a, b, contract, precision=None):
    return lax.dot_general(a, b, (contract, ((), ())), preferred_element_type=F32, precision=precision)


NN = ((1,), (0,))
NT = ((1,), (1,))
TN = ((0,), (0,))


def _sds(shape, dtype):
    return jax.ShapeDtypeStruct(shape, dtype)


def _resident(shape):
    return pl.BlockSpec(shape, lambda *_: (0,) * len(shape), pipeline_mode=pl.Buffered(1))


def _mm_cols(name, a, w, contract, out_dtypes, epi=None, extras=()):
    m, k = a.shape
    jn = w.shape[0]
    nj = w.shape[2] if contract == NN else w.shape[1]
    tm = min(m, MM_TILE)
    n_ex, parts = len(extras), 2

    def body(a_ref, w_ref, *rest):
        ex, out_refs = rest[:n_ex], rest[n_ex:]
        i = pl.program_id(1)
        part = tm // parts
        for h in range(parts):
            rows = slice(h * part, (h + 1) * part)
            acc = _dot(a_ref[pl.ds(pl.multiple_of(i * tm + h * part, part), part), :], w_ref[...], contract)
            res = epi(acc, *[e[rows, :] for e in ex]) if epi else (acc,)
            for o, r in zip(out_refs, res):
                o[rows, :] = r.astype(o.dtype)

    blk = pl.BlockSpec((tm, nj), lambda j, i: (i, j))
    return pl.pallas_call(
        body, name=name, grid=(jn, m // tm),
        in_specs=[_resident((m, k)), pl.BlockSpec((None,) + w.shape[1:], lambda j, i: (j, 0, 0))] + [blk] * n_ex,
        out_specs=[blk] * len(out_dtypes), out_shape=[_sds((m, jn * nj), dt) for dt in out_dtypes],
        compiler_params=_cp(("parallel", "parallel")),
    )(a, w, *extras)


def _mm_wgrad(name, a, b, a_by_j):
    s = a.shape[0]
    if a_by_j:
        r, c = a.shape[1] // N_CHIPS, b.shape[1]
        in_specs = [pl.BlockSpec((s, r), lambda j: (0, j)), _resident((s, c))]
    else:
        r, c = a.shape[1], b.shape[1] // N_CHIPS
        in_specs = [_resident((s, r)), pl.BlockSpec((s, c), lambda j: (0, j))]
    tr = min(r, 512)

    def body(a_ref, b_ref, o32_ref, o16_ref):
        for h in range(r // tr):
            cols = slice(h * tr, (h + 1) * tr)
            acc = _dot(a_ref[:, cols], b_ref[...], TN)
            o32_ref[cols, :] = acc
            o16_ref[cols, :] = acc.astype(BF16)

    out = pl.BlockSpec((None, r, c), lambda j: (j, 0, 0))
    return pl.pallas_call(
        body, name=name, grid=(N_CHIPS,), in_specs=in_specs, out_specs=[out, out],
        out_shape=[_sds((N_CHIPS, r, c), F32), _sds((N_CHIPS, r, c), BF16)], compiler_params=_cp(("parallel",)),
    )(a, b)


def _rows_call(name, fn, ins, outs, s, tm=ROW_TILE, matmul=None, exchange=None):
    in_specs = []
    if matmul:
        a, w = matmul
        in_specs += [pl.BlockSpec((tm, a.shape[1]), lambda i: (i, 0)), _resident(w.shape)]
    for arr, kind in ins:
        if kind == "full":
            in_specs.append(pl.BlockSpec(arr.shape, lambda i: (0, 0)))
        else:
            _, w_, cb = kind
            in_specs.append(pl.BlockSpec((tm, w_), lambda i, cb=cb: (i, cb)))
    out_specs, out_shape, is_acc = [], [], []
    for w_, dt, kind in outs:
        if kind == "acc":
            out_specs.append(pl.BlockSpec((1, w_), lambda i: (0, 0)))
            out_shape.append(_sds((1, w_), dt))
        else:
            out_specs.append(pl.BlockSpec((tm, w_), lambda i: (i, 0)))
            out_shape.append(_sds((s, w_), dt))
        is_acc.append(kind == "acc")
    n_mm, n_in, n_out = (2 if matmul else 0), len(ins), len(outs)
    x_ins, x_shapes, x_sems, x_start, x_finish = exchange if exchange else ((), [], [], None, None)
    n_x = len(x_ins)
    steps = s // tm

    def body(*refs):
        in_refs, xi = refs[n_mm:n_mm + n_in], refs[n_mm + n_in:n_mm + n_in + n_x]
        out_refs = refs[n_mm + n_in + n_x:n_mm + n_in + n_x + n_out]
        xo, sems = refs[n_mm + n_in + n_x + n_out:n_mm + n_in + 2 * n_x + n_out], refs[n_mm + n_in + 2 * n_x + n_out:]
        i = pl.program_id(0)

        if exchange:
            @pl.when(i == 0)
            def _():
                x_start(xi, xo, *sems)

        for o, acc in zip(out_refs, is_acc):
            if acc:
                @pl.when(i == 0)
                def _(o=o):
                    o[...] = jnp.zeros_like(o)

        parts = 2 if matmul else 1
        for h in range(parts):
            rows = slice(h * (tm // parts), (h + 1) * (tm // parts))
            args = [r[...] if kind == "full" else r[rows, :] for r, (_, kind) in zip(in_refs, ins)]
            if matmul:
                a_ref, w_ref = refs[:2]
                if len(w_ref.shape) == 2:
                    acc = _dot(a_ref[rows, :], w_ref[...], NN)
                else:
                    kj = w_ref.shape[2]
                    acc = _dot(a_ref[rows, 0:kj], w_ref[0], NT)
                    for j in range(1, w_ref.shape[0]):
                        acc = acc + _dot(a_ref[rows, j * kj:(j + 1) * kj], w_ref[j], NT)
                args.insert(0, acc)
            for o, r, acc in zip(out_refs, fn(*args), is_acc):
                if acc:
                    o[...] += r.astype(o.dtype)
                else:
                    o[rows, :] = r.astype(o.dtype)

        if exchange:
            @pl.when(i == steps - 1)
            def _():
                x_finish(xi, xo, *sems)

    sem = ("arbitrary",) if any(is_acc) or exchange else ("parallel",)
    return pl.pallas_call(
        body, name=name, grid=(steps,), in_specs=in_specs + [ANY] * n_x, out_specs=out_specs + [ANY] * n_x,
        out_shape=out_shape + list(x_shapes), scratch_shapes=list(x_sems), compiler_params=_cp(sem),
    )(*(matmul or ()), *[a for a, _ in ins], *x_ins)


def _rstd(v):
    return lax.rsqrt(jnp.mean(v * v, axis=-1, keepdims=True) + RMS_EPS)


def _norm_bwd(v, gain, dy):
    r = _rstd(v)
    n = v * r
    dn = dy * gain
    dv = r * (dn - n * jnp.mean(dn * n, axis=-1, keepdims=True))
    return dv, dy * n


def _colsum(v):
    return jnp.sum(v, axis=0, keepdims=True)


def _row(w, cb=0):
    return ("row", w, cb)


N_PAIRS = ATTN_W // BLK


def _head_col(v, mask):
    return jnp.max(jnp.where(mask, v, -jnp.inf), axis=1, keepdims=True)


def _slopes():
    t = np.zeros((N_PAIRS, 8, 2 * BLK), np.float32)
    for p in range(N_PAIRS):
        for hh in range(2):
            t[p, hh, :] = 2.0 ** -(2 * p + hh + 1)
    return jnp.asarray(t)


def _rows(n, r, d):
    base = pl.multiple_of(n * (BLK * d), BLK)
    return pl.ds(base + r, BLK, stride=d) if d > 1 else pl.ds(base, BLK)


def _attn_bias(sl_ref, bias_scr):
    row = lax.broadcasted_iota(jnp.int32, (BLK, 2 * BLK), 0)
    col = lax.broadcasted_iota(jnp.int32, (BLK, 2 * BLK), 1)
    dist = row + BLK - col
    in_window = (dist >= 0) & (dist <= BLK)
    distf = dist.astype(F32)
    for di, d in enumerate(DILATIONS):
        for hh in range(2):
            bias_scr[2 * di + hh] = jnp.where(in_window, -(sl_ref[hh:hh + 1, :] * float(d)) * distf, -1e30)


def _first_block_penalty(n):
    col = lax.broadcasted_iota(jnp.int32, (1, 2 * BLK), 1)
    return jnp.where(col + n * BLK >= BLK, 0.0, -1e30)


def _attn_groups(s, d):
    nb = s // (BLK * d)
    g = ATTN_GROUP
    if d >= g:
        return [(nb, lambda n, r0=r0: [(n, r0 + u) for u in range(g)]) for r0 in range(0, d, g)]
    per = g // d
    return [(nb // per, lambda t: [(per * t + u, r) for u in range(per) for r in range(d)])]


def _attn_fwd(proj, shards):
    s = proj.shape[0]
    nk = len(shards)

    def body(sl_ref, q_ref, k_ref, v_ref, *rest):
        w_refs, (o_ref, l_ref) = rest[:nk], rest[nk:nk + 2]
        wg_refs, (bias_scr, ssem, rsem) = rest[nk + 2:2 * nk + 2], rest[2 * nk + 2:]
        pair = pl.program_id(0)

        @pl.when(pair == 0)
        def _():
            _ag_start(w_refs, wg_refs, ssem, rsem)

        _attn_bias(sl_ref, bias_scr)
        lane_q = lax.broadcasted_iota(jnp.int32, (BLK, BLK), 1) < 64
        lane_k = lax.broadcasted_iota(jnp.int32, (2 * BLK, BLK), 1) < 64

        def branch(n, r, di):
            d = DILATIONS[di]
            rows = _rows(n, r, d)
            prev = _rows(jnp.maximum(n - 1, 0), r, d)
            pen = _first_block_penalty(n)
            q2 = q_ref[rows, :] * ATTN_SCALE
            kk = jnp.concatenate([k_ref[prev, :], k_ref[rows, :]], axis=0).astype(BF16)
            vv = jnp.concatenate([v_ref[prev, :], v_ref[rows, :]], axis=0)
            o2 = jnp.zeros((BLK, BLK), F32)
            lse2 = jnp.zeros((BLK, BLK), F32)
            for hh in range(2):
                mq = lane_q if hh == 0 else ~lane_q
                mk = lane_k if hh == 0 else ~lane_k
                qm = jnp.where(mq, q2, 0.0).astype(BF16)
                sc = _dot(qm, kk, NT) + bias_scr[2 * di + hh] + pen
                m = jnp.max(sc, axis=1, keepdims=True)
                pr = jnp.exp(sc - m)
                den = jnp.sum(pr, axis=1, keepdims=True)
                vm = jnp.where(mk, vv, 0.0).astype(BF16)
                o2 = o2 + _dot(pr.astype(BF16), vm, NN) / den
                lse2 = jnp.where(mq, m + jnp.log(den), lse2)
            return rows, o2, lse2

        def merge(rows, o2, lse2, first):
            if first:
                o_ref[rows, :] = o2
                l_ref[rows, :] = lse2
            else:
                lo = l_ref[rows, :]
                mx = jnp.maximum(lo, lse2)
                ln = mx + jnp.log(jnp.exp(lo - mx) + jnp.exp(lse2 - mx))
                o_ref[rows, :] = jnp.exp(lo - ln) * o_ref[rows, :] + jnp.exp(lse2 - ln) * o2
                l_ref[rows, :] = ln

        for di, d in enumerate(DILATIONS):
            for trips, blocks in _attn_groups(s, d):
                def trip(t, carry, di=di, blocks=blocks):
                    done = [branch(n, r, di) for n, r in blocks(t)]
                    for rows, o2, lse2 in done:
                        merge(rows, o2, lse2, di == 0)
                    return carry

                lax.fori_loop(0, trips, trip, 0)

        @pl.when(pair == N_PAIRS - 1)
        def _():
            _ag_finish(w_refs, wg_refs, ssem, rsem)

    cb = lambda base: pl.BlockSpec((s, BLK), lambda p, base=base: (0, base + p))
    out = pl.BlockSpec((s, BLK), lambda p: (0, p))
    ag_shape, ag_sems = _ag_shapes(shards)
    return pl.pallas_call(
        body, name="attn_fwd", grid=(N_PAIRS,),
        in_specs=[pl.BlockSpec((None, 8, 2 * BLK), lambda p: (p, 0, 0)), cb(0), cb(N_PAIRS), cb(2 * N_PAIRS)]
        + [ANY] * nk,
        out_specs=[out, out] + [ANY] * nk, out_shape=[_sds((s, ATTN_W), F32)] * 2 + ag_shape,
        scratch_shapes=[pltpu.VMEM((2 * len(DILATIONS), BLK, 2 * BLK), F32)] + ag_sems,
        compiler_params=_cp(("arbitrary",)),
    )(_slopes(), proj, proj, proj, *shards)


def _attn_bwd(proj, do, lse, delta, psums):
    s = proj.shape[0]
    nk = len(psums)

    def body(sl_ref, q_ref, k_ref, v_ref, do_ref, l_ref, e_ref, *rest):
        p_refs, out16 = rest[:nk], rest[nk:nk + 3]
        got_refs, (dq_ref, dk_ref, dv_ref, bias_scr, ssem, rsem) = rest[nk + 3:2 * nk + 3], rest[2 * nk + 3:]
        pair = pl.program_id(0)

        @pl.when(pair == 0)
        def _():
            _rs_chips_start(p_refs, got_refs, ssem, rsem)

        _attn_bias(sl_ref, bias_scr)
        lane_q = lax.broadcasted_iota(jnp.int32, (BLK, BLK), 1) < 64
        lane_k = lax.broadcasted_iota(jnp.int32, (2 * BLK, BLK), 1) < 64
        dk_ref[...] = jnp.zeros_like(dk_ref)
        dv_ref[...] = jnp.zeros_like(dv_ref)

        def branch(n, r, di):
            d = DILATIONS[di]
            rows = _rows(n, r, d)
            prev = _rows(jnp.maximum(n - 1, 0), r, d)
            pen = _first_block_penalty(n)
            q1, d1, l1, e1 = q_ref[rows, :] * ATTN_SCALE, do_ref[rows, :], l_ref[rows, :], e_ref[rows, :]
            kk = jnp.concatenate([k_ref[prev, :], k_ref[rows, :]], axis=0)
            kkb = kk.astype(BF16)
            vvb = jnp.concatenate([v_ref[prev, :], v_ref[rows, :]], axis=0).astype(BF16)
            dq2 = jnp.zeros((BLK, BLK), F32)
            dkk = jnp.zeros((2 * BLK, BLK), F32)
            dvv = jnp.zeros((2 * BLK, BLK), F32)
            for hh in range(2):
                mq = lane_q if hh == 0 else ~lane_q
                mk = lane_k if hh == 0 else ~lane_k
                qm = jnp.where(mq, q1, 0.0).astype(BF16)
                dm = jnp.where(mq, d1, 0.0).astype(BF16)
                sc = _dot(qm, kkb, NT) + bias_scr[2 * di + hh] + pen
                pr = jnp.exp(sc - _head_col(l1, mq))
                ds = (pr * (_dot(dm, vvb, NT) - _head_col(e1, mq))).astype(BF16)
                km = jnp.where(mk, kk, 0.0).astype(BF16)
                dq2 = dq2 + _dot(ds, km, NN)
                dkk = dkk + _dot(ds, qm, TN)
                dvv = dvv + _dot(pr.astype(BF16), dm, TN)
            return rows, prev, dq2 * ATTN_SCALE, dkk, dvv

        for di, d in enumerate(DILATIONS):
            for trips, blocks in _attn_groups(s, d):
                def trip(t, carry, di=di, blocks=blocks, first=(di == 0)):
                    done = [branch(n, r, di) for n, r in blocks(t)]
                    for rows, prev, dq2, dkk, dvv in done:
                        dq_ref[rows, :] = dq2 if first else dq_ref[rows, :] + dq2
                        dk_ref[prev, :] = dk_ref[prev, :] + dkk[:BLK]
                        dk_ref[rows, :] = dk_ref[rows, :] + dkk[BLK:]
                        dv_ref[prev, :] = dv_ref[prev, :] + dvv[:BLK]
                        dv_ref[rows, :] = dv_ref[rows, :] + dvv[BLK:]
                    return carry

                lax.fori_loop(0, trips, trip, 0)

        for o16, acc in zip(out16, (dq_ref, dk_ref, dv_ref)):
            o16[...] = acc[...].astype(BF16)

        @pl.when(pair == N_PAIRS - 1)
        def _():
            _rs_chips_finish(p_refs, got_refs, ssem, rsem)

    cb = lambda base: pl.BlockSpec((s, BLK), lambda p, base=base: (0, base + p))
    out = pl.BlockSpec((s, BLK), lambda p: (0, p))
    rs_shape, rs_sems = _rs_chips_shapes(psums)
    return pl.pallas_call(
        body, name="attn_bwd", grid=(N_PAIRS,),
        in_specs=[pl.BlockSpec((None, 8, 2 * BLK), lambda p: (p, 0, 0)), cb(0), cb(N_PAIRS), cb(2 * N_PAIRS),
                  out, out, out] + [ANY] * nk,
        out_specs=[out] * 3 + [ANY] * nk, out_shape=[_sds((s, ATTN_W), BF16)] * 3 + rs_shape,
        scratch_shapes=[pltpu.VMEM((s, BLK), F32)] * 3 + [pltpu.VMEM((2 * len(DILATIONS), BLK, 2 * BLK), F32)] + rs_sems,
        compiler_params=_cp(("arbitrary",)),
    )(_slopes(), proj, proj, proj, do, lse, delta, *psums)


def _lower_bound(lbl):
    return 1.0 / (1.0 + jnp.exp(lbl[1:2, :] - lbl[0:1, :]))


def _hi(a):
    bits = lax.bitcast_convert_type(a, jnp.uint32) & jnp.uint32(0xFFFF0000)
    return lax.bitcast_convert_type(bits, F32)


def _dot3(a, b, contract):
    ah, bh = _hi(a), _hi(b)
    al, bl = (a - ah).astype(BF16), (b - bh).astype(BF16)
    ah, bh = ah.astype(BF16), bh.astype(BF16)
    return _dot(ah, bh, contract) + (_dot(ah, bl, contract) + _dot(al, bh, contract))


def _cumsum_rows(tri, g):
    g1 = _hi(g)
    r1 = g - g1
    g2 = _hi(r1)
    g3 = r1 - g2
    return _dot(tri, g1.astype(BF16), NN) + (_dot(tri, g2.astype(BF16), NN) + _dot(tri, g3.astype(BF16), NN))


def _heads(fn):
    return jnp.concatenate([fn(slice(h * BLK, (h + 1) * BLK)) for h in range(HGRN_W // BLK)], axis=1)


def _head_mean(t):
    return _heads(lambda hs: jnp.broadcast_to(jnp.mean(t[:, hs], axis=1, keepdims=True), (t.shape[0], BLK)))


def _hgrn_chunk(q_ref, f_ref, i_ref, sl, lb, tri):
    qp = q_ref[sl, :]
    sq = _sigmoid(qp)
    qf = qp * sq
    sg = _sigmoid(f_ref[sl, :])
    f = lb + (1.0 - lb) * sg
    kf = 1.0 - f
    v = i_ref[sl, :]
    b = _cumsum_rows(tri, jnp.log(f))
    bm = b[CHUNK // 2:CHUNK // 2 + 1, :]
    bl = b[CHUNK - 1:CHUNK, :]
    qt = qf * jnp.exp(b - bm)
    kt = kf * jnp.exp(bm - b)
    return qp, sq, qf, sg, f, kf, v, b, bm, bl, qt, kt


def _hgrn_specs(tb, block):
    first = 3 * ATTN_W // HGRN_W
    return [pl.BlockSpec((tb, HGRN_W), lambda i, k=k: (block(i), first + k)) for k in range(4)]


def _hgrn_fwd(proj, lb_logits, out_gain):
    s = proj.shape[0]
    tb = min(HGRN_TB, s)
    nb, cpb, nc = s // tb, tb // CHUNK, s // CHUNK

    def body(q_ref, f_ref, i_ref, g_ref, lbl_ref, gain_ref, o_ref, rec_ref, st_ref, st_scr):
        step = pl.program_id(0)

        @pl.when(step == 0)
        def _():
            st_scr[...] = jnp.zeros_like(st_scr)

        lb = _lower_bound(lbl_ref[...])
        r64 = lax.broadcasted_iota(jnp.int32, (CHUNK, CHUNK), 0)
        c64 = lax.broadcasted_iota(jnp.int32, (CHUNK, CHUNK), 1)
        tril = r64 >= c64
        tri = tril.astype(BF16)
        st = st_scr[...]
        for cc in range(cpb):
            sl = slice(cc * CHUNK, (cc + 1) * CHUNK)
            _, _, qf, _, _, kf, v, b, _, bl, qt, kt = _hgrn_chunk(q_ref, f_ref, i_ref, sl, lb, tri)
            qe = (qf * jnp.exp(b)).astype(BF16)
            kh = (kf * jnp.exp(bl - b)).astype(BF16)
            qtb, ktb, vb, stb = qt.astype(BF16), kt.astype(BF16), v.astype(BF16), st.astype(BF16)

            def out_h(hs):
                a = jnp.where(tril, _dot(qtb[:, hs], ktb[:, hs], NT), 0.0).astype(BF16)
                return _dot(qe[:, hs], stb[:, hs], NT) + _dot(a, vb[:, hs], NN)

            o_ref[sl, :] = _heads(out_h)
            st_ref[cc] = stb
            st = st * jnp.exp(bl) + _heads(lambda hs: _dot(vb[:, hs], kh[:, hs], TN))
        st_scr[...] = st
        o = o_ref[...]
        gate = g_ref[...]
        rec_ref[...] = (o * lax.rsqrt(_head_mean(o * o) + RMS_EPS) * gain_ref[...] * (gate * _sigmoid(gate))).astype(BF16)

    row = pl.BlockSpec((tb, HGRN_W), lambda i: (i, 0))
    return pl.pallas_call(
        body, name="hgrn_fwd", grid=(nb,),
        in_specs=_hgrn_specs(tb, lambda i: i) + [pl.BlockSpec((2, HGRN_W), lambda i: (0, 0)),
                                                 pl.BlockSpec((1, HGRN_W), lambda i: (0, 0))],
        out_specs=[row, row, pl.BlockSpec((cpb, BLK, HGRN_W), lambda i: (i, 0, 0))],
        out_shape=[_sds((s, HGRN_W), F32), _sds((s, HGRN_W), BF16), _sds((nc, BLK, HGRN_W), BF16)],
        scratch_shapes=[pltpu.VMEM((BLK, HGRN_W), F32)],
        compiler_params=_cp(("arbitrary",)),
    )(proj, proj, proj, proj, lb_logits, out_gain)


def _hgrn_bwd(proj, o_pre, states, drec, lb_logits, out_gain):
    s = proj.shape[0]
    tb = min(HGRN_TB, s)
    nb, cpb, nc = s // tb, tb // CHUNK, s // CHUNK

    def body(q_ref, f_ref, i_ref, g_ref, o_ref, st_ref, stn_ref, dy_ref, lbl_ref, gain_ref,
             dq_ref, df_ref, di_ref, dg_ref, dgain_ref, dlbl_ref, do_scr, dst_scr, dlb_scr):
        step = pl.program_id(0)

        @pl.when(step == 0)
        def _():
            dst_scr[...] = jnp.zeros_like(dst_scr)
            dlb_scr[...] = jnp.zeros_like(dlb_scr)
            dgain_ref[...] = jnp.zeros_like(dgain_ref)

        lb = _lower_bound(lbl_ref[...])
        gain = gain_ref[...]
        o = o_ref[...]
        r = lax.rsqrt(_head_mean(o * o) + RMS_EPS)
        nrm = o * r
        gate = g_ref[...]
        sgt = _sigmoid(gate)
        dy = dy_ref[...]
        dg_ref[...] = (dy * nrm * gain * (sgt * (1.0 + gate * (1.0 - sgt)))).astype(BF16)
        dng = dy * (gate * sgt)
        dgain_ref[...] += _colsum(dng * nrm)
        dn = dng * gain
        do_scr[...] = r * (dn - nrm * _head_mean(dn * nrm))

        r64 = lax.broadcasted_iota(jnp.int32, (CHUNK, CHUNK), 0)
        c64 = lax.broadcasted_iota(jnp.int32, (CHUNK, CHUNK), 1)
        tril = r64 >= c64
        tri = tril.astype(BF16)
        triu = (r64 <= c64).astype(BF16)
        dst = dst_scr[...]
        dlb = dlb_scr[...]
        for cc in reversed(range(cpb)):
            sl = slice(cc * CHUNK, (cc + 1) * CHUNK)
            qp, sq, qf, sg, f, kf, v, b, bm, bl, qt, kt = _hgrn_chunk(q_ref, f_ref, i_ref, sl, lb, tri)
            stf = st_ref[cc].astype(F32)
            st_end = (st_ref[cc + 1] if cc + 1 < cpb else stn_ref[0]).astype(F32)
            csum = jnp.sum(st_end * dst, axis=0, keepdims=True)
            doc = do_scr[sl, :]
            dob, dstb = doc.astype(BF16), dst.astype(BF16)
            eb = jnp.exp(b)
            qe = (qf * eb).astype(BF16)
            kh = (kf * jnp.exp(bl - b)).astype(BF16)
            qtb, ktb = qt.astype(BF16), kt.astype(BF16)
            parts = []
            for h in range(HGRN_W // BLK):
                hs = slice(h * BLK, (h + 1) * BLK)
                da = jnp.where(tril, _dot3(doc[:, hs], v[:, hs], NT), 0.0)
                a = jnp.where(tril, _dot(qtb[:, hs], ktb[:, hs], NT), 0.0).astype(BF16)
                parts.append((
                    _dot3(da, kt[:, hs], NN), _dot3(doc[:, hs], stf[:, hs], NN),
                    _dot3(da, qt[:, hs], TN), _dot3(v[:, hs], dst[:, hs], NN),
                    _dot(a, dob[:, hs], TN) + _dot(kh[:, hs], dstb[:, hs], NT),
                    _dot(dob[:, hs], qe[:, hs], TN)))
            dqt, dqi, dkt, dks, dv, upd = (jnp.concatenate([p[n] for p in parts], axis=1) for n in range(6))
            dqf = dqt * jnp.exp(b - bm) + eb * dqi
            dkf = dkt * jnp.exp(bm - b) + jnp.exp(bl - b) * dks
            gq = qf * dqf - kf * dkf
            dlogf = csum + _cumsum_rows(triu, gq)
            dfv = dlogf / f - dkf
            dq_ref[sl, :] = (dqf * (sq * (1.0 + qp * (1.0 - sq)))).astype(BF16)
            df_ref[sl, :] = (dfv * (1.0 - lb) * sg * (1.0 - sg)).astype(BF16)
            di_ref[sl, :] = dv.astype(BF16)
            dst = dst * jnp.exp(bl) + upd
            dlb = dlb + _colsum(dfv * (1.0 - sg))
        dst_scr[...] = dst
        dlb_scr[...] = dlb

        @pl.when(step == nb - 1)
        def _():
            t = dlb * lb * (1.0 - lb)
            dlbl_ref[...] = jnp.concatenate([t, -t], axis=0)

    rev = lambda i: nb - 1 - i
    row = pl.BlockSpec((tb, HGRN_W), lambda i: (rev(i), 0))
    res = pl.pallas_call(
        body, name="hgrn_bwd", grid=(nb,),
        in_specs=_hgrn_specs(tb, rev) + [
            row, pl.BlockSpec((cpb, BLK, HGRN_W), lambda i: (rev(i), 0, 0)),
            pl.BlockSpec((1, BLK, HGRN_W), lambda i: (jnp.minimum((rev(i) + 1) * cpb, nc - 1), 0, 0)),
            row, pl.BlockSpec((2, HGRN_W), lambda i: (0, 0)), pl.BlockSpec((1, HGRN_W), lambda i: (0, 0))],
        out_specs=[row, row, row, row, pl.BlockSpec((1, HGRN_W), lambda i: (0, 0)),
                   pl.BlockSpec((2, HGRN_W), lambda i: (0, 0))],
        out_shape=[_sds((s, HGRN_W), BF16)] * 4 + [_sds((1, HGRN_W), F32), _sds((2, HGRN_W), F32)],
        scratch_shapes=[pltpu.VMEM((tb, HGRN_W), F32), pltpu.VMEM((BLK, HGRN_W), F32), pltpu.VMEM((1, HGRN_W), F32)],
        compiler_params=_cp(("arbitrary",)),
    )(proj, proj, proj, proj, o_pre, states, states, drec, lb_logits, out_gain)
    return res


def _place():
    return lax.axis_index("x"), lax.axis_index("y"), lax.axis_index("c")


def _flip(x, y, ox, oy):
    return (1 - x if ox else x), (1 - y if oy else y)


def _half(rows, cc):
    return pl.ds(cc * (rows // 2), rows // 2)


def _remote(src, dst, ssem, rsem, to):
    return pltpu.make_async_remote_copy(src_ref=src, dst_ref=dst, send_sem=ssem, recv_sem=rsem,
                                        device_id=to, device_id_type=MESH)


def _ag_chip_copies(ins, outs, ssem, rsem):
    x, y, c = _place()
    j = 2 * x + y
    cps = []
    for k in range(len(ins)):
        rows = ins[k].shape[0]
        for idx, (ox, oy) in enumerate(FLIPS):
            px, py = _flip(x, y, ox, oy)
            cps.append(_remote(ins[k].at[_half(rows, c)], outs[k].at[j, _half(rows, c)],
                               ssem.at[k, idx], rsem.at[k, idx], (px, py, c)))
    return cps


def _ag_start(ins, outs, ssem, rsem):
    for cp in _ag_chip_copies(ins, outs, ssem, rsem):
        cp.start()


def _ag_finish(ins, outs, ssem, rsem):
    x, y, c = _place()
    sib = (x, y, 1 - c)
    passed = []
    for k in range(len(ins)):
        rows = ins[k].shape[0]
        for idx, (ox, oy) in enumerate(FLIPS):
            px, py = _flip(x, y, ox, oy)
            blk = outs[k].at[2 * px + py, _half(rows, c)]
            _remote(blk, blk, ssem.at[k, idx], rsem.at[k, idx], (px, py, c)).wait_recv()
            cp = _remote(blk, blk, ssem.at[k, 3 + idx], rsem.at[k, 3 + idx], sib)
            cp.start()
            passed.append(cp)
    for k in range(len(ins)):
        rows = ins[k].shape[0]
        for idx, (ox, oy) in enumerate(FLIPS):
            px, py = _flip(x, y, ox, oy)
            blk = outs[k].at[2 * px + py, _half(rows, 1 - c)]
            _remote(blk, blk, ssem.at[k, 3 + idx], rsem.at[k, 3 + idx], sib).wait_recv()
    for cp in _ag_chip_copies(ins, outs, ssem, rsem) + passed:
        cp.wait_send()


def _ag_shapes(shards):
    nk = len(shards)
    return ([_sds((N_CHIPS,) + tuple(w.shape), w.dtype) for w in shards],
            [pltpu.SemaphoreType.DMA((nk, 6)), pltpu.SemaphoreType.DMA((nk, 6))])


def _with_own(gathered, shard, j):
    return lax.dynamic_update_index_in_dim(gathered, shard, j, 0)


def _rs_pair_copies(ins, outs, ssem, rsem):
    x, y, c = _place()
    return [_remote(ins[k].at[:, _half(ins[k].shape[1], 1 - c)], outs[k], ssem.at[k], rsem.at[k], (x, y, 1 - c))
            for k in range(len(ins))]


def _rs_pair_start(ins, outs, ssem, rsem):
    for cp in _rs_pair_copies(ins, outs, ssem, rsem):
        cp.start()


def _rs_pair_finish(ins, outs, ssem, rsem):
    for cp in _rs_pair_copies(ins, outs, ssem, rsem):
        cp.wait()


def _rs_pair_exchange(grads):
    nk = len(grads)
    return (grads, [_sds((N_CHIPS, g.shape[1] // 2, g.shape[2]), g.dtype) for g in grads],
            [pltpu.SemaphoreType.DMA((nk,)), pltpu.SemaphoreType.DMA((nk,))], _rs_pair_start, _rs_pair_finish)


def _rs_pair(name, grads):
    nk = len(grads)
    ins, out_shape, sems, start, finish = _rs_pair_exchange(grads)

    def body(*refs):
        start(refs[:nk], refs[nk:2 * nk], *refs[2 * nk:])
        finish(refs[:nk], refs[nk:2 * nk], *refs[2 * nk:])

    return pl.pallas_call(body, name=name, in_specs=[ANY] * nk, out_specs=[ANY] * nk, out_shape=out_shape,
                          scratch_shapes=sems)(*ins)


def _rs_chip_copies(ins, outs, ssem, rsem):
    x, y, c = _place()
    cps = []
    for k in range(len(ins)):
        for idx, (ox, oy) in enumerate(FLIPS):
            px, py = _flip(x, y, ox, oy)
            cps.append(_remote(ins[k].at[2 * px + py], outs[k].at[idx], ssem.at[k, idx], rsem.at[k, idx], (px, py, c)))
    return cps


def _rs_chips_start(ins, outs, ssem, rsem):
    for cp in _rs_chip_copies(ins, outs, ssem, rsem):
        cp.start()


def _rs_chips_finish(ins, outs, ssem, rsem):
    for cp in _rs_chip_copies(ins, outs, ssem, rsem):
        cp.wait()


def _rs_chips_shapes(psums):
    nk = len(psums)
    return ([_sds((3,) + tuple(p.shape[1:]), p.dtype) for p in psums],
            [pltpu.SemaphoreType.DMA((nk, 3)), pltpu.SemaphoreType.DMA((nk, 3))])


def _rs_share(fulls):
    nk = len(fulls)

    def body(*refs):
        ins, outs = refs[:nk], refs[nk:2 * nk]
        ssem, rsem = refs[2 * nk:]
        x, y, c = _place()
        cps = []
        for k in range(nk):
            rows = fulls[k].shape[0]
            cp = _remote(ins[k].at[_half(rows, c)], outs[k].at[_half(rows, c)], ssem.at[k], rsem.at[k], (x, y, 1 - c))
            cp.start()
            cps.append(cp)
        for k, cp in enumerate(cps):
            rows = fulls[k].shape[0]
            cp.wait_send()
            theirs = outs[k].at[_half(rows, 1 - c)]
            _remote(theirs, theirs, ssem.at[k], rsem.at[k], (x, y, 1 - c)).wait_recv()

    return pl.pallas_call(
        body, name="rs_share", in_specs=[ANY] * nk, out_specs=[ANY] * nk,
        out_shape=[_sds(f.shape, f.dtype) for f in fulls], input_output_aliases={k: k for k in range(nk)},
        scratch_shapes=[pltpu.SemaphoreType.DMA((nk,)), pltpu.SemaphoreType.DMA((nk,))],
    )(*fulls)


def _allreduce_small(v):
    ndev = 8

    def body(in_ref, out_ref, buf, ssem, rsem):
        x, y, c = _place()
        me = 4 * x + 2 * y + c
        buf[me] = in_ref[...]
        cps = []
        for k in range(1, ndev):
            ox, oy, oc = (k >> 2) & 1, (k >> 1) & 1, k & 1
            px, py = _flip(x, y, ox, oy)
            pc = 1 - c if oc else c
            cp = pltpu.make_async_remote_copy(src_ref=in_ref, dst_ref=buf.at[me], send_sem=ssem.at[k - 1],
                                              recv_sem=rsem.at[k - 1], device_id=(px, py, pc), device_id_type=MESH)
            cp.start()
            cps.append((cp, 4 * px + 2 * py + pc, (px, py, pc)))
        for k, (cp, src, peer) in enumerate(cps):
            cp.wait_send()
            pltpu.make_async_remote_copy(src_ref=in_ref, dst_ref=buf.at[src], send_sem=ssem.at[k],
                                         recv_sem=rsem.at[k], device_id=peer, device_id_type=MESH).wait_recv()
        acc = buf[0]
        for i in range(1, ndev):
            acc = acc + buf[i]
        out_ref[...] = acc

    return pl.pallas_call(
        body, name="allreduce_small",
        in_specs=[pl.BlockSpec(memory_space=pltpu.VMEM)], out_specs=pl.BlockSpec(memory_space=pltpu.VMEM),
        out_shape=_sds(v.shape, v.dtype),
        scratch_shapes=[pltpu.VMEM((ndev,) + v.shape, v.dtype), pltpu.SemaphoreType.DMA((ndev - 1,)),
                        pltpu.SemaphoreType.DMA((ndev - 1,))],
    )(v)


def _rs_sum1(name, g, recv, jc_idx):
    _, r, cdim = g.shape
    hr = r // 2
    tr = min(hr, 256)
    nr = hr // tr

    def body(jc_ref, g_ref, r_ref, o32_ref, o16_ref):
        v = g_ref[...] + r_ref[...].astype(F32)
        o16_ref[...] = v.astype(BF16)

        @pl.when(pl.program_id(1) == jc_ref[0])
        def _():
            o32_ref[...] = v

    spec = pl.BlockSpec((None, tr, cdim), lambda i, j, jc: (j, i, 0))
    return pl.pallas_call(
        body, name=name,
        grid_spec=pltpu.PrefetchScalarGridSpec(
            num_scalar_prefetch=1, grid=(nr, N_CHIPS),
            in_specs=[pl.BlockSpec((None, tr, cdim), lambda i, j, jc: (j, jc[1] * nr + i, 0)), spec],
            out_specs=[pl.BlockSpec((tr, cdim), lambda i, j, jc: (i, 0)), spec]),
        out_shape=[_sds((hr, cdim), F32), _sds((N_CHIPS, hr, cdim), BF16)],
        compiler_params=_cp(("parallel", "arbitrary")),
    )(jc_idx, g, recv)


def _rs_sum2(name, p32, recv, jc_idx):
    hr, cdim = p32.shape
    tr = min(hr, 256)
    nr = hr // tr

    def body(jc_ref, p_ref, r_ref, o_ref):
        o_ref[...] = ((p_ref[...] + r_ref[0].astype(F32)) + r_ref[1].astype(F32)) + r_ref[2].astype(F32)

    return pl.pallas_call(
        body, name=name,
        grid_spec=pltpu.PrefetchScalarGridSpec(
            num_scalar_prefetch=1, grid=(nr,),
            in_specs=[pl.BlockSpec((tr, cdim), lambda i, jc: (i, 0)),
                      pl.BlockSpec((3, tr, cdim), lambda i, jc: (0, i, 0))],
            out_specs=pl.BlockSpec((tr, cdim), lambda i, jc: (jc[1] * nr + i, 0))),
        out_shape=_sds((2 * hr, cdim), F32),
        compiler_params=_cp(("parallel",)),
    )(jc_idx, p32, recv)


def _adamw(name, w, g, m, v):
    r, cdim = w.shape
    tr = min(r, 256)
    c1 = 1.0 - ADAM_B1 ** ADAM_STEP
    c2 = 1.0 - ADAM_B2 ** ADAM_STEP

    def body(w_ref, g_ref, m_ref, v_ref, d_ref, nm_ref, nv_ref):
        gv = g_ref[...]
        nm = ADAM_B1 * m_ref[...] + (1.0 - ADAM_B1) * gv
        nv = ADAM_B2 * v_ref[...] + (1.0 - ADAM_B2) * (gv * gv)
        d_ref[...] = -ADAM_LR * ((nm / c1) / (jnp.sqrt(nv / c2) + ADAM_EPS) + ADAM_WD * w_ref[...])
        nm_ref[...] = nm
        nv_ref[...] = nv

    spec = pl.BlockSpec((tr, cdim), lambda i: (i, 0))
    return pl.pallas_call(
        body, name=name, grid=(r // tr,), in_specs=[spec] * 4, out_specs=[spec] * 3,
        out_shape=[_sds((r, cdim), F32)] * 3, compiler_params=_cp(("parallel",)),
    )(w, g, m, v)


def _pack_small(mix_pre, attn_out, lb_logits, hgrn_out, mix_post, mlp_pre, mlp_post, extra=None):
    spare = jnp.zeros((1, D_MODEL), F32)
    rows = [mix_pre, jnp.concatenate([attn_out, hgrn_out], axis=1),
            jnp.concatenate([lb_logits[0:1], lb_logits[1:2]], axis=1), mix_post, mlp_pre, mlp_post,
            spare if extra is None else extra, spare]
    return jnp.concatenate(rows, axis=0)


def _unpack_small(p):
    return (p[0:1], p[1:2, :ATTN_W], jnp.concatenate([p[2:3, :HGRN_W], p[2:3, HGRN_W:]], axis=0),
            p[1:2, ATTN_W:], p[3:4], p[4:5], p[5:6])


def kernel(x, mix_pre_norm, w_in, attn_out_norm, hgrn_lb_logits, hgrn_out_norm, w_out, mix_post_norm, mlp_pre_norm, w_ff1, w_ff2, mlp_post_norm, loss_target, m_mix_pre_norm, m_w_in, m_attn_out_norm, m_hgrn_lb_logits, m_hgrn_out_norm, m_w_out, m_mix_post_norm, m_mlp_pre_norm, m_w_ff1, m_w_ff2, m_mlp_post_norm, v_mix_pre_norm, v_w_in, v_attn_out_norm, v_hgrn_lb_logits, v_hgrn_out_norm, v_w_out, v_mix_post_norm, v_mlp_pre_norm, v_w_ff1, v_w_ff2, v_mlp_post_norm):
    s = x.shape[1]
    xs = x.reshape(s, D_MODEL)
    tgt = loss_target.reshape(s, D_MODEL)
    cx, cy, cc = _place()
    chip = 2 * cx + cy
    jc_idx = jnp.stack([chip, cc]).astype(jnp.int32)

    big_w = [w_in[0], w_out[0], w_ff1[0], w_ff2[0]]
    big_m = [m_w_in[0], m_w_out[0], m_w_ff1[0], m_w_ff2[0]]
    big_v = [v_w_in[0], v_w_out[0], v_w_ff1[0], v_w_ff2[0]]
    shards = [w.astype(BF16) for w in big_w]

    h, wg_in = _rows_call("norm_in", lambda xv, g: ((xv * _rstd(xv) * g),),
                          [(xs, _row(D_MODEL)), (mix_pre_norm, "full")], [(D_MODEL, BF16, "row")], s,
                          exchange=(shards[:1], *_ag_shapes(shards[:1]), _ag_start, _ag_finish))
    wg_in = _with_own(wg_in, shards[0], chip)
    (proj,) = _mm_cols("mm_proj", h, wg_in, NN, [F32])
    hg_o, rec, states = _hgrn_fwd(proj, hgrn_lb_logits, hgrn_out_norm)
    attn_o, attn_lse, wg_out, wg_1, wg_2 = _attn_fwd(proj, shards[1:])
    wg_out, wg_1, wg_2 = (_with_own(g, w, chip) for g, w in zip((wg_out, wg_1, wg_2), shards[1:]))
    (attn_n,) = _rows_call("attn_norm", lambda o, gain: (o * _rstd(o) * gain,),
                           [(attn_o, _row(ATTN_W)), (attn_out_norm, "full")], [(ATTN_W, BF16, "row")], s)
    cat = jnp.concatenate([attn_n, rec], axis=1)

    def post1(mv, xv, g_post, g_pre2):
        x1 = xv + mv * _rstd(mv) * g_post
        return mv, x1, x1 * _rstd(x1) * g_pre2

    mixed, x1, h2 = _rows_call(
        "mm_mixed", post1, [(xs, _row(D_MODEL)), (mix_post_norm, "full"), (mlp_pre_norm, "full")],
        [(D_MODEL, F32, "row"), (D_MODEL, F32, "row"), (D_MODEL, BF16, "row")], s,
        matmul=(cat, wg_out.reshape(D_MODEL, D_MODEL)))

    def sq_relu(u):
        r = jnp.maximum(u, 0.0)
        return r * r, r

    act, ru = _mm_cols("mm_ff1", h2, wg_1, NN, [BF16, BF16], epi=sq_relu)

    def post2(fv, x1v, tv, g):
        y = x1v + fv * _rstd(fv) * g
        dy = (y - tv) * (1.0 / D_MODEL)
        err = y - tv
        loss = 0.5 * jnp.sum(jnp.mean(err * err, axis=-1, keepdims=True), axis=0, keepdims=True)
        dff, dgc = _norm_bwd(fv, g, dy)
        return dy, dff, _colsum(dgc), jnp.broadcast_to(loss, (1, BLK))

    dy, dff, g_mlp_post, loss_part = _rows_call(
        "mm_ff2", post2, [(x1, _row(D_MODEL)), (tgt, _row(D_MODEL)), (mlp_post_norm, "full")],
        [(D_MODEL, F32, "row"), (D_MODEL, BF16, "row"), (D_MODEL, F32, "acc"), (BLK, F32, "acc")], s,
        matmul=(act, wg_2.reshape(D_FF, D_MODEL)))

    (du,) = _mm_cols("mm_du", dff, wg_2, NT, [BF16], epi=lambda acc, r: (acc * (2.0 * r.astype(F32)),),
                     extras=(ru,))
    gw_2 = _mm_wgrad("mm_gw2", act, dff, True)
    gw_1 = _mm_wgrad("mm_gw1", h2, du, False)

    def bwd_mid(dh2v, dyv, x1v, mv, g_pre2, g_post):
        d1, gc1 = _norm_bwd(x1v, g_pre2, dh2v)
        dx1 = dyv + d1
        dm, gc2 = _norm_bwd(mv, g_post, dx1)
        return dx1, dm, _colsum(gc1), _colsum(gc2)

    dx1, dmixed, g_mlp_pre, g_mix_post, *from_pair = _rows_call(
        "mm_dh2", bwd_mid, [(dy, _row(D_MODEL)), (x1, _row(D_MODEL)), (mixed, _row(D_MODEL)),
                            (mlp_pre_norm, "full"), (mix_post_norm, "full")],
        [(D_MODEL, F32, "row"), (D_MODEL, BF16, "row"), (D_MODEL, F32, "acc"), (D_MODEL, F32, "acc")], s,
        matmul=(du, wg_1), exchange=_rs_pair_exchange([gw_1[1], gw_2[1]]))

    def attn_norm_bwd(dc, o, gain):
        do, gc = _norm_bwd(o, gain, dc[:, :ATTN_W])
        t = do * o
        lane = lax.broadcasted_iota(jnp.int32, (t.shape[0], BLK), 1) < 64
        parts = []
        for p in range(ATTN_W // BLK):
            tp = t[:, p * BLK:(p + 1) * BLK]
            sa = jnp.sum(jnp.where(lane, tp, 0.0), axis=1, keepdims=True)
            sb = jnp.sum(jnp.where(lane, 0.0, tp), axis=1, keepdims=True)
            parts.append(jnp.where(lane, sa, sb))
        return do, jnp.concatenate(parts, axis=1), dc[:, ATTN_W:], _colsum(gc)

    gw_out = _mm_wgrad("mm_gwout", cat, dmixed, True)
    do_attn, delta, drec, g_attn_out, from_pair_out = _rows_call(
        "mm_dcat", attn_norm_bwd, [(attn_o, _row(ATTN_W)), (attn_out_norm, "full")],
        [(ATTN_W, F32, "row"), (ATTN_W, F32, "row"), (HGRN_W, F32, "row"), (ATTN_W, F32, "acc")], s,
        matmul=(dmixed, wg_out.reshape(1, D_MODEL, D_MODEL)), exchange=_rs_pair_exchange([gw_out[1]]))
    names = ["out", "ff1", "ff2", "in"]
    ready = [gw_out, gw_1, gw_2]
    from_pair = [from_pair_out] + from_pair
    pair =[_rs_sum1(f"rs_sum1_{n}", g[0], r, jc_idx) for n, g, r in zip(names, ready, from_pair)]

    dq, dk, dv, *from_chips = _attn_bwd(proj, do_attn, attn_lse, delta, [p[1] for p in pair])
    dhq, dhf, dhi, dhg, g_hgrn_out, g_lb = _hgrn_bwd(proj, hg_o, states, drec, hgrn_lb_logits, hgrn_out_norm)

    dproj = jnp.concatenate([dq, dk, dv, dhq, dhf, dhi, dhg], axis=1)
    gw_in = _mm_wgrad("mm_gwin", h, dproj, False)
    (from_pair_in,) = _rs_pair("rs_pair_in", [gw_in[1]])
    pair.append(_rs_sum1("rs_sum1_in", gw_in[0], from_pair_in, jc_idx))
    rs_shape, rs_sems = _rs_chips_shapes([pair[3][1]])

    def bwd_in(dhv, dx1v, xv, g):
        d0, gc = _norm_bwd(xv, g, dhv)
        return dx1v + d0, _colsum(gc)

    grad_x, g_mix_pre, from_chips_in = _rows_call(
        "mm_dh", bwd_in, [(dx1, _row(D_MODEL)), (xs, _row(D_MODEL)), (mix_pre_norm, "full")],
        [(D_MODEL, F32, "row"), (D_MODEL, F32, "acc")], s, matmul=(dproj, wg_in),
        exchange=([pair[3][1]], rs_shape, rs_sems, _rs_chips_start, _rs_chips_finish))
    from_chips.append(from_chips_in)

    loss_row = jnp.pad(loss_part, ((0, 0), (0, D_MODEL - BLK)))
    small_g = _allreduce_small(_pack_small(g_mix_pre, g_attn_out, g_lb, g_hgrn_out, g_mix_post, g_mlp_pre, g_mlp_post,
                                           extra=loss_row))
    loss = small_g[6, 0]

    reduced = [_rs_sum2(f"rs_sum2_{n}", p[0], r, jc_idx) for n, p, r in zip(names, pair, from_chips)]
    g_wout, g_w1, g_w2, g_win = _rs_share(reduced)
    full = [g_win, g_wout, g_w1, g_w2]

    upd = [_adamw(f"adamw_{n}", w, g, m, v) for n, w, g, m, v in zip(("in", "out", "ff1", "ff2"), big_w, full, big_m, big_v)]
    small_w = _pack_small(mix_pre_norm, attn_out_norm, hgrn_lb_logits, hgrn_out_norm, mix_post_norm, mlp_pre_norm,
                          mlp_post_norm)
    small_m = _pack_small(m_mix_pre_norm, m_attn_out_norm, m_hgrn_lb_logits, m_hgrn_out_norm, m_mix_post_norm,
                          m_mlp_pre_norm, m_mlp_post_norm)
    small_v = _pack_small(v_mix_pre_norm, v_attn_out_norm, v_hgrn_lb_logits, v_hgrn_out_norm, v_mix_post_norm,
                          v_mlp_pre_norm, v_mlp_post_norm)
    small_upd = _adamw("adamw_small", small_w, small_g, small_m, small_v)

    def assemble(small, big):
        sm = _unpack_small(small)
        return (sm[0], big[0][None], sm[1], sm[2], sm[3], big[1][None], sm[4], sm[5], big[2][None], big[3][None], sm[6])

    g_out = assemble(small_g, full)
    d_out = assemble(small_upd[0], [u[0] for u in upd])
    m_out = assemble(small_upd[1], [u[1] for u in upd])
    v_out = assemble(small_upd[2], [u[2] for u in upd])
    return (loss, grad_x.reshape(x.shape), *g_out, *d_out, *m_out, *v_out)
```

```python
import numpy as np
import jax
import jax.numpy as jnp
from jax import lax
from jax.experimental import pallas as pl
from jax.experimental.pallas import tpu as pltpu

F32 = jnp.float32
BF16 = jnp.bfloat16
MESH = pl.DeviceIdType.MESH
ANY = pl.BlockSpec(memory_space=pl.ANY)

RMS_EPS = 1e-6
D_MODEL = 1024
ATTN_W = 512
HGRN_W = 512
D_FF = 4096
N_CHIPS = 4
BLK = 128
CHUNK = 64
HGRN_TB = 512
ATTN_GROUP = 8
DILATIONS = (1, 4, 16)
ATTN_SCALE = 0.125
ROW_TILE = 512
MM_TILE = 1024
VMEM_LIMIT = 48 * 2 ** 20
FLIPS = ((1, 0), (0, 1), (1, 1))

ADAM_LR, ADAM_B1, ADAM_B2, ADAM_EPS, ADAM_WD, ADAM_STEP = 0.001, 0.9, 0.999, 1e-08, 0.01, 10


def _cp(sem=None):
    return pltpu.CompilerParams(dimension_semantics=sem, vmem_limit_bytes=VMEM_LIMIT)


def _sigmoid(v):
    return 1.0 / (1.0 + jnp.exp(-v))


def _dot(a, b, contract, precision=None):
    return lax.dot_general(a, b, (contract, ((), ())), preferred_element_type=F32, precision=precision)


NN = ((1,), (0,))
NT = ((1,), (1,))
TN = ((0,), (0,))


def _sds(shape, dtype):
    return jax.ShapeDtypeStruct(shape, dtype)


def _resident(shape):
    return pl.BlockSpec(shape, lambda *_: (0,) * len(shape), pipeline_mode=pl.Buffered(1))


def _mm_cols(name, a, w, contract, out_dtypes, epi=None, extras=()):
    m, k = a.shape
    jn = w.shape[0]
    nj = w.shape[2] if contract == NN else w.shape[1]
    tm = min(m, MM_TILE)
    n_ex, parts = len(extras), 2

    def body(a_ref, w_ref, *rest):
        ex, out_refs = rest[:n_ex], rest[n_ex:]
        i = pl.program_id(1)
        part = tm // parts
        for h in range(parts):
            rows = slice(h * part, (h + 1) * part)
            acc = _dot(a_ref[pl.ds(pl.multiple_of(i * tm + h * part, part), part), :], w_ref[...], contract)
            res = epi(acc, *[e[rows, :] for e in ex]) if epi else (acc,)
            for o, r in zip(out_refs, res):
                o[rows, :] = r.astype(o.dtype)

    blk = pl.BlockSpec((tm, nj), lambda j, i: (i, j))
    return pl.pallas_call(
        body, name=name, grid=(jn, m // tm),
        in_specs=[_resident((m, k)), pl.BlockSpec((None,) + w.shape[1:], lambda j, i: (j, 0, 0))] + [blk] * n_ex,
        out_specs=[blk] * len(out_dtypes), out_shape=[_sds((m, jn * nj), dt) for dt in out_dtypes],
        compiler_params=_cp(("parallel", "parallel")),
    )(a, w, *extras)


def _mm_wgrad(name, a, b, a_by_j):
    s = a.shape[0]
    if a_by_j:
        r, c = a.shape[1] // N_CHIPS, b.shape[1]
        in_specs = [pl.BlockSpec((s, r), lambda j: (0, j)), _resident((s, c))]
    else:
        r, c = a.shape[1], b.shape[1] // N_CHIPS
        in_specs = [_resident((s, r)), pl.BlockSpec((s, c), lambda j: (0, j))]
    tr = min(r, 512)

    def body(a_ref, b_ref, o32_ref, o16_ref):
        for h in range(r // tr):
            cols = slice(h * tr, (h + 1) * tr)
            acc = _dot(a_ref[:, cols], b_ref[...], TN)
            o32_ref[cols, :] = acc
            o16_ref[cols, :] = acc.astype(BF16)

    out = pl.BlockSpec((None, r, c), lambda j: (j, 0, 0))
    return pl.pallas_call(
        body, name=name, grid=(N_CHIPS,), in_specs=in_specs, out_specs=[out, out],
        out_shape=[_sds((N_CHIPS, r, c), F32), _sds((N_CHIPS, r, c), BF16)], compiler_params=_cp(("parallel",)),
    )(a, b)


def _rows_call(name, fn, ins, outs, s, tm=ROW_TILE, matmul=None, exchange=None):
    steps = s // tm
    in_specs = []
    if matmul:
        a, w = matmul
        in_specs += [pl.BlockSpec((tm, a.shape[1]), lambda i: (i, 0)), _resident(w.shape)]
    for arr, kind in ins:
        if kind == "full":
            in_specs.append(pl.BlockSpec(arr.shape, lambda i: (0, 0)))
        else:
            _, w_, cb = kind
            in_specs.append(pl.BlockSpec((arr.shape[0] // steps, w_), lambda i, cb=cb: (i, cb)))
    out_specs, out_shape, is_acc = [], [], []
    for w_, dt, kind, *rows_all in outs:
        if kind == "acc":
            out_specs.append(pl.BlockSpec((1, w_), lambda i: (0, 0)))
            out_shape.append(_sds((1, w_), dt))
        else:
            rows_all = rows_all[0] if rows_all else s
            out_specs.append(pl.BlockSpec((rows_all // steps, w_), lambda i: (i, 0)))
            out_shape.append(_sds((rows_all, w_), dt))
        is_acc.append(kind == "acc")
    n_mm, n_in, n_out = (2 if matmul else 0), len(ins), len(outs)
    x_ins, x_shapes, x_sems, x_start, x_finish = exchange if exchange else ((), [], [], None, None)
    n_x = len(x_ins)

    def body(*refs):
        in_refs, xi = refs[n_mm:n_mm + n_in], refs[n_mm + n_in:n_mm + n_in + n_x]
        out_refs = refs[n_mm + n_in + n_x:n_mm + n_in + n_x + n_out]
        xo, sems = refs[n_mm + n_in + n_x + n_out:n_mm + n_in + 2 * n_x + n_out], refs[n_mm + n_in + 2 * n_x + n_out:]
        i = pl.program_id(0)

        if exchange:
            @pl.when(i == 0)
            def _():
                x_start(xi, xo, *sems)

        for o, acc in zip(out_refs, is_acc):
            if acc:
                @pl.when(i == 0)
                def _(o=o):
                    o[...] = jnp.zeros_like(o)

        parts = 2 if matmul else 1
        for h in range(parts):
            rows = slice(h * (tm // parts), (h + 1) * (tm // parts)) if matmul else slice(None)
            args = [r[...] if kind == "full" else r[rows, :] for r, (_, kind) in zip(in_refs, ins)]
            if matmul:
                a_ref, w_ref = refs[:2]
                if len(w_ref.shape) == 2:
                    acc = _dot(a_ref[rows, :], w_ref[...], NN)
                else:
                    kj = w_ref.shape[2]
                    acc = _dot(a_ref[rows, 0:kj], w_ref[0], NT)
                    for j in range(1, w_ref.shape[0]):
                        acc = acc + _dot(a_ref[rows, j * kj:(j + 1) * kj], w_ref[j], NT)
                args.insert(0, acc)
            for o, r, acc in zip(out_refs, fn(*args), is_acc):
                if acc:
                    o[...] += r.astype(o.dtype)
                else:
                    o[rows, :] = r.astype(o.dtype)

        if exchange:
            @pl.when(i == steps - 1)
            def _():
                x_finish(xi, xo, *sems)

    sem = ("arbitrary",) if any(is_acc) or exchange else ("parallel",)
    return pl.pallas_call(
        body, name=name, grid=(steps,), in_specs=in_specs + [ANY] * n_x, out_specs=out_specs + [ANY] * n_x,
        out_shape=out_shape + list(x_shapes), scratch_shapes=list(x_sems), compiler_params=_cp(sem),
    )(*(matmul or ()), *[a for a, _ in ins], *x_ins)


def _rstd(v):
    return lax.rsqrt(jnp.mean(v * v, axis=-1, keepdims=True) + RMS_EPS)


def _norm_bwd(v, gain, dy):
    r = _rstd(v)
    n = v * r
    dn = dy * gain
    dv = r * (dn - n * jnp.mean(dn * n, axis=-1, keepdims=True))
    return dv, dy * n


def _colsum(v):
    return jnp.sum(v, axis=0, keepdims=True)


def _row(w, cb=0):
    return ("row", w, cb)


N_PAIRS = ATTN_W // BLK


def _head_col(v, mask):
    return jnp.max(jnp.where(mask, v, -jnp.inf), axis=1, keepdims=True)


def _slopes():
    t = np.zeros((N_PAIRS, 8, 2 * BLK), np.float32)
    for p in range(N_PAIRS):
        for hh in range(2):
            t[p, hh, :] = 2.0 ** -(2 * p + hh + 1)
    return jnp.asarray(t)


def _rows(n, r, d):
    base = pl.multiple_of(n * (BLK * d), BLK)
    return pl.ds(base + r, BLK, stride=d) if d > 1 else pl.ds(base, BLK)


def _attn_bias(sl_ref, bias_scr):
    row = lax.broadcasted_iota(jnp.int32, (BLK, 2 * BLK), 0)
    col = lax.broadcasted_iota(jnp.int32, (BLK, 2 * BLK), 1)
    dist = row + BLK - col
    in_window = (dist >= 0) & (dist <= BLK)
    distf = dist.astype(F32)
    for di, d in enumerate(DILATIONS):
        for hh in range(2):
            bias_scr[2 * di + hh] = jnp.where(in_window, -(sl_ref[hh:hh + 1, :] * float(d)) * distf, -1e30)


def _first_block_penalty(n):
    col = lax.broadcasted_iota(jnp.int32, (1, 2 * BLK), 1)
    return jnp.where(col + n * BLK >= BLK, 0.0, -1e30)


def _attn_groups(s, d):
    nb = s // (BLK * d)
    g = ATTN_GROUP
    if d >= g:
        return [(nb, lambda n, r0=r0: [(n, r0 + u) for u in range(g)]) for r0 in range(0, d, g)]
    per = g // d
    return [(nb // per, lambda t: [(per * t + u, r) for u in range(per) for r in range(d)])]


def _attn_fwd(proj, shards):
    s = proj.shape[0]
    nk = len(shards)

    def body(sl_ref, q_ref, k_ref, v_ref, *rest):
        w_refs, (o_ref, l_ref) = rest[:nk], rest[nk:nk + 2]
        wg_refs, (bias_scr, ssem, rsem) = rest[nk + 2:2 * nk + 2], rest[2 * nk + 2:]
        pair = pl.program_id(0)

        @pl.when(pair == 0)
        def _():
            _ag_start(w_refs, wg_refs, ssem, rsem)

        _attn_bias(sl_ref, bias_scr)
        lane_q = lax.broadcasted_iota(jnp.int32, (BLK, BLK), 1) < 64
        lane_k = lax.broadcasted_iota(jnp.int32, (2 * BLK, BLK), 1) < 64

        def branch(n, r, di):
            d = DILATIONS[di]
            rows = _rows(n, r, d)
            prev = _rows(jnp.maximum(n - 1, 0), r, d)
            pen = _first_block_penalty(n)
            q2 = q_ref[rows, :] * ATTN_SCALE
            kk = jnp.concatenate([k_ref[prev, :], k_ref[rows, :]], axis=0).astype(BF16)
            vv = jnp.concatenate([v_ref[prev, :], v_ref[rows, :]], axis=0)
            o2 = jnp.zeros((BLK, BLK), F32)
            lse2 = jnp.zeros((BLK, BLK), F32)
            for hh in range(2):
                mq = lane_q if hh == 0 else ~lane_q
                mk = lane_k if hh == 0 else ~lane_k
                qm = jnp.where(mq, q2, 0.0).astype(BF16)
                sc = _dot(qm, kk, NT) + bias_scr[2 * di + hh] + pen
                m = jnp.max(sc, axis=1, keepdims=True)
                pr = jnp.exp(sc - m)
                den = jnp.sum(pr, axis=1, keepdims=True)
                vm = jnp.where(mk, vv, 0.0).astype(BF16)
                o2 = o2 + _dot(pr.astype(BF16), vm, NN) / den
                lse2 = jnp.where(mq, m + jnp.log(den), lse2)
            return rows, o2, lse2

        def merge(rows, o2, lse2, first):
            if first:
                o_ref[rows, :] = o2
                l_ref[rows, :] = lse2
            else:
                lo = l_ref[rows, :]
                mx = jnp.maximum(lo, lse2)
                ln = mx + jnp.log(jnp.exp(lo - mx) + jnp.exp(lse2 - mx))
                o_ref[rows, :] = jnp.exp(lo - ln) * o_ref[rows, :] + jnp.exp(lse2 - ln) * o2
                l_ref[rows, :] = ln

        for di, d in enumerate(DILATIONS):
            for trips, blocks in _attn_groups(s, d):
                def trip(t, carry, di=di, blocks=blocks):
                    done = [branch(n, r, di) for n, r in blocks(t)]
                    for rows, o2, lse2 in done:
                        merge(rows, o2, lse2, di == 0)
                    return carry

                lax.fori_loop(0, trips, trip, 0)

        @pl.when(pair == N_PAIRS - 1)
        def _():
            _ag_finish(w_refs, wg_refs, ssem, rsem)

    cb = lambda base: pl.BlockSpec((s, BLK), lambda p, base=base: (0, base + p))
    out = pl.BlockSpec((s, BLK), lambda p: (0, p))
    ag_shape, ag_sems = _ag_shapes(shards)
    return pl.pallas_call(
        body, name="attn_fwd", grid=(N_PAIRS,),
        in_specs=[pl.BlockSpec((None, 8, 2 * BLK), lambda p: (p, 0, 0)), cb(0), cb(N_PAIRS), cb(2 * N_PAIRS)]
        + [ANY] * nk,
        out_specs=[out, out] + [ANY] * nk, out_shape=[_sds((s, ATTN_W), F32)] * 2 + ag_shape,
        scratch_shapes=[pltpu.VMEM((2 * len(DILATIONS), BLK, 2 * BLK), F32)] + ag_sems,
        compiler_params=_cp(("arbitrary",)),
    )(_slopes(), proj, proj, proj, *shards)


def _attn_bwd(proj, do, lse, delta, psums):
    s = proj.shape[0]
    nk = len(psums)

    def body(sl_ref, q_ref, k_ref, v_ref, do_ref, l_ref, e_ref, *rest):
        p_refs, out16 = rest[:nk], rest[nk:nk + 3]
        got_refs, (dq_ref, dk_ref, dv_ref, bias_scr, ssem, rsem) = rest[nk + 3:2 * nk + 3], rest[2 * nk + 3:]
        pair = pl.program_id(0)

        @pl.when(pair == 0)
        def _():
            _rs_chips_start(p_refs, got_refs, ssem, rsem)

        _attn_bias(sl_ref, bias_scr)
        lane_q = lax.broadcasted_iota(jnp.int32, (BLK, BLK), 1) < 64
        lane_k = lax.broadcasted_iota(jnp.int32, (2 * BLK, BLK), 1) < 64
        dk_ref[...] = jnp.zeros_like(dk_ref)
        dv_ref[...] = jnp.zeros_like(dv_ref)

        def branch(n, r, di):
            d = DILATIONS[di]
            rows = _rows(n, r, d)
            prev = _rows(jnp.maximum(n - 1, 0), r, d)
            pen = _first_block_penalty(n)
            q1, d1, l1, e1 = q_ref[rows, :] * ATTN_SCALE, do_ref[rows, :], l_ref[rows, :], e_ref[rows, :]
            kk = jnp.concatenate([k_ref[prev, :], k_ref[rows, :]], axis=0)
            kkb = kk.astype(BF16)
            vvb = jnp.concatenate([v_ref[prev, :], v_ref[rows, :]], axis=0).astype(BF16)
            dq2 = jnp.zeros((BLK, BLK), F32)
            dkk = jnp.zeros((2 * BLK, BLK), F32)
            dvv = jnp.zeros((2 * BLK, BLK), F32)
            for hh in range(2):
                mq = lane_q if hh == 0 else ~lane_q
                mk = lane_k if hh == 0 else ~lane_k
                qm = jnp.where(mq, q1, 0.0).astype(BF16)
                dm = jnp.where(mq, d1, 0.0).astype(BF16)
                sc = _dot(qm, kkb, NT) + bias_scr[2 * di + hh] + pen
                pr = jnp.exp(sc - _head_col(l1, mq))
                ds = (pr * (_dot(dm, vvb, NT) - _head_col(e1, mq))).astype(BF16)
                km = jnp.where(mk, kk, 0.0).astype(BF16)
                dq2 = dq2 + _dot(ds, km, NN)
                dkk = dkk + _dot(ds, qm, TN)
                dvv = dvv + _dot(pr.astype(BF16), dm, TN)
            return rows, prev, dq2 * ATTN_SCALE, dkk, dvv

        for di, d in enumerate(DILATIONS):
            for trips, blocks in _attn_groups(s, d):
                def trip(t, carry, di=di, blocks=blocks, first=(di == 0)):
                    done = [branch(n, r, di) for n, r in blocks(t)]
                    for rows, prev, dq2, dkk, dvv in done:
                        dq_ref[rows, :] = dq2 if first else dq_ref[rows, :] + dq2
                        dk_ref[prev, :] = dk_ref[prev, :] + dkk[:BLK]
                        dk_ref[rows, :] = dk_ref[rows, :] + dkk[BLK:]
                        dv_ref[prev, :] = dv_ref[prev, :] + dvv[:BLK]
                        dv_ref[rows, :] = dv_ref[rows, :] + dvv[BLK:]
                    return carry

                lax.fori_loop(0, trips, trip, 0)

        for o16, acc in zip(out16, (dq_ref, dk_ref, dv_ref)):
            o16[...] = acc[...].astype(BF16)

        @pl.when(pair == N_PAIRS - 1)
        def _():
            _rs_chips_finish(p_refs, got_refs, ssem, rsem)

    cb = lambda base: pl.BlockSpec((s, BLK), lambda p, base=base: (0, base + p))
    out = pl.BlockSpec((s, BLK), lambda p: (0, p))
    rs_shape, rs_sems = _rs_chips_shapes(psums)
    return pl.pallas_call(
        body, name="attn_bwd", grid=(N_PAIRS,),
        in_specs=[pl.BlockSpec((None, 8, 2 * BLK), lambda p: (p, 0, 0)), cb(0), cb(N_PAIRS), cb(2 * N_PAIRS),
                  out, out, out] + [ANY] * nk,
        out_specs=[out] * 3 + [ANY] * nk, out_shape=[_sds((s, ATTN_W), BF16)] * 3 + rs_shape,
        scratch_shapes=[pltpu.VMEM((s, BLK), F32)] * 3 + [pltpu.VMEM((2 * len(DILATIONS), BLK, 2 * BLK), F32)] + rs_sems,
        compiler_params=_cp(("arbitrary",)),
    )(_slopes(), proj, proj, proj, do, lse, delta, *psums)


def _lower_bound(lbl):
    return 1.0 / (1.0 + jnp.exp(lbl[1:2, :] - lbl[0:1, :]))


def _hi(a):
    bits = lax.bitcast_convert_type(a, jnp.uint32) & jnp.uint32(0xFFFF0000)
    return lax.bitcast_convert_type(bits, F32)


def _dot3(a, b, contract):
    ah, bh = _hi(a), _hi(b)
    al, bl = (a - ah).astype(BF16), (b - bh).astype(BF16)
    ah, bh = ah.astype(BF16), bh.astype(BF16)
    return _dot(ah, bh, contract) + (_dot(ah, bl, contract) + _dot(al, bh, contract))


def _cumsum_rows(tri, g):
    g1 = _hi(g)
    r1 = g - g1
    g2 = _hi(r1)
    g3 = r1 - g2
    return _dot(tri, g1.astype(BF16), NN) + (_dot(tri, g2.astype(BF16), NN) + _dot(tri, g3.astype(BF16), NN))


def _heads(fn):
    return jnp.concatenate([fn(slice(h * BLK, (h + 1) * BLK)) for h in range(HGRN_W // BLK)], axis=1)


def _head_mean(t):
    return _heads(lambda hs: jnp.broadcast_to(jnp.mean(t[:, hs], axis=1, keepdims=True), (t.shape[0], BLK)))


def _hgrn_chunk(q_ref, f_ref, i_ref, sl, lb, tri):
    qp = q_ref[sl, :]
    sq = _sigmoid(qp)
    qf = qp * sq
    sg = _sigmoid(f_ref[sl, :])
    f = lb + (1.0 - lb) * sg
    kf = 1.0 - f
    v = i_ref[sl, :]
    b = _cumsum_rows(tri, jnp.log(f))
    bm = b[CHUNK // 2:CHUNK // 2 + 1, :]
    bl = b[CHUNK - 1:CHUNK, :]
    qt = qf * jnp.exp(b - bm)
    kt = kf * jnp.exp(bm - b)
    return qp, sq, qf, sg, f, kf, v, b, bm, bl, qt, kt


def _hgrn_specs(tb, block):
    first = 3 * ATTN_W // HGRN_W
    return [pl.BlockSpec((tb, HGRN_W), lambda i, k=k: (block(i), first + k)) for k in range(4)]


def _hgrn_fwd(proj, lb_logits, out_gain):
    s = proj.shape[0]
    tb = min(HGRN_TB, s)
    nb, cpb, nc = s // tb, tb // CHUNK, s // CHUNK

    def body(q_ref, f_ref, i_ref, g_ref, lbl_ref, gain_ref, o_ref, rec_ref, st_ref, st_scr):
        step = pl.program_id(0)

        @pl.when(step == 0)
        def _():
            st_scr[...] = jnp.zeros_like(st_scr)

        lb = _lower_bound(lbl_ref[...])
        r64 = lax.broadcasted_iota(jnp.int32, (CHUNK, CHUNK), 0)
        c64 = lax.broadcasted_iota(jnp.int32, (CHUNK, CHUNK), 1)
        tril = r64 >= c64
        tri = tril.astype(BF16)
        st = st_scr[...]
        for cc in range(cpb):
            sl = slice(cc * CHUNK, (cc + 1) * CHUNK)
            _, _, qf, _, _, kf, v, b, _, bl, qt, kt = _hgrn_chunk(q_ref, f_ref, i_ref, sl, lb, tri)
            qe = (qf * jnp.exp(b)).astype(BF16)
            kh = (kf * jnp.exp(bl - b)).astype(BF16)
            qtb, ktb, vb, stb = qt.astype(BF16), kt.astype(BF16), v.astype(BF16), st.astype(BF16)

            def out_h(hs):
                a = jnp.where(tril, _dot(qtb[:, hs], ktb[:, hs], NT), 0.0).astype(BF16)
                return _dot(qe[:, hs], stb[:, hs], NT) + _dot(a, vb[:, hs], NN)

            o_ref[sl, :] = _heads(out_h)
            st_ref[cc] = stb
            st = st * jnp.exp(bl) + _heads(lambda hs: _dot(vb[:, hs], kh[:, hs], TN))
        st_scr[...] = st
        o = o_ref[...]
        gate = g_ref[...]
        rec_ref[...] = (o * lax.rsqrt(_head_mean(o * o) + RMS_EPS) * gain_ref[...] * (gate * _sigmoid(gate))).astype(BF16)

    row = pl.BlockSpec((tb, HGRN_W), lambda i: (i, 0))
    return pl.pallas_call(
        body, name="hgrn_fwd", grid=(nb,),
        in_specs=_hgrn_specs(tb, lambda i: i) + [pl.BlockSpec((2, HGRN_W), lambda i: (0, 0)),
                                                 pl.BlockSpec((1, HGRN_W), lambda i: (0, 0))],
        out_specs=[row, row, pl.BlockSpec((cpb, BLK, HGRN_W), lambda i: (i, 0, 0))],
        out_shape=[_sds((s, HGRN_W), F32), _sds((s, HGRN_W), BF16), _sds((nc, BLK, HGRN_W), BF16)],
        scratch_shapes=[pltpu.VMEM((BLK, HGRN_W), F32)],
        compiler_params=_cp(("arbitrary",)),
    )(proj, proj, proj, proj, lb_logits, out_gain)


def _hgrn_bwd(proj, o_pre, states, drec, lb_logits, out_gain):
    s = proj.shape[0]
    tb = min(HGRN_TB, s)
    nb, cpb, nc = s // tb, tb // CHUNK, s // CHUNK

    def body(q_ref, f_ref, i_ref, g_ref, o_ref, st_ref, stn_ref, dy_ref, lbl_ref, gain_ref,
             dq_ref, df_ref, di_ref, dg_ref, dgain_ref, dlbl_ref, do_scr, dst_scr, dlb_scr):
        step = pl.program_id(0)

        @pl.when(step == 0)
        def _():
            dst_scr[...] = jnp.zeros_like(dst_scr)
            dlb_scr[...] = jnp.zeros_like(dlb_scr)
            dgain_ref[...] = jnp.zeros_like(dgain_ref)

        lb = _lower_bound(lbl_ref[...])
        gain = gain_ref[...]
        o = o_ref[...]
        r = lax.rsqrt(_head_mean(o * o) + RMS_EPS)
        nrm = o * r
        gate = g_ref[...]
        sgt = _sigmoid(gate)
        dy = dy_ref[...]
        dg_ref[...] = (dy * nrm * gain * (sgt * (1.0 + gate * (1.0 - sgt)))).astype(BF16)
        dng = dy * (gate * sgt)
        dgain_ref[...] += _colsum(dng * nrm)
        dn = dng * gain
        do_scr[...] = r * (dn - nrm * _head_mean(dn * nrm))

        r64 = lax.broadcasted_iota(jnp.int32, (CHUNK, CHUNK), 0)
        c64 = lax.broadcasted_iota(jnp.int32, (CHUNK, CHUNK), 1)
        tril = r64 >= c64
        tri = tril.astype(BF16)
        triu = (r64 <= c64).astype(BF16)
        dst = dst_scr[...]
        dlb = dlb_scr[...]
        for cc in reversed(range(cpb)):
            sl = slice(cc * CHUNK, (cc + 1) * CHUNK)
            qp, sq, qf, sg, f, kf, v, b, bm, bl, qt, kt = _hgrn_chunk(q_ref, f_ref, i_ref, sl, lb, tri)
            stf = st_ref[cc].astype(F32)
            st_end = (st_ref[cc + 1] if cc + 1 < cpb else stn_ref[0]).astype(F32)
            csum = jnp.sum(st_end * dst, axis=0, keepdims=True)
            doc = do_scr[sl, :]
            dob, dstb = doc.astype(BF16), dst.astype(BF16)
            eb = jnp.exp(b)
            qe = (qf * eb).astype(BF16)
            kh = (kf * jnp.exp(bl - b)).astype(BF16)
            qtb, ktb = qt.astype(BF16), kt.astype(BF16)
            parts = []
            for h in range(HGRN_W // BLK):
                hs = slice(h * BLK, (h + 1) * BLK)
                da = jnp.where(tril, _dot3(doc[:, hs], v[:, hs], NT), 0.0)
                a = jnp.where(tril, _dot(qtb[:, hs], ktb[:, hs], NT), 0.0).astype(BF16)
                parts.append((
                    _dot3(da, kt[:, hs], NN), _dot3(doc[:, hs], stf[:, hs], NN),
                    _dot3(da, qt[:, hs], TN), _dot3(v[:, hs], dst[:, hs], NN),
                    _dot(a, dob[:, hs], TN) + _dot(kh[:, hs], dstb[:, hs], NT),
                    _dot(dob[:, hs], qe[:, hs], TN)))
            dqt, dqi, dkt, dks, dv, upd = (jnp.concatenate([p[n] for p in parts], axis=1) for n in range(6))
            dqf = dqt * jnp.exp(b - bm) + eb * dqi
            dkf = dkt * jnp.exp(bm - b) + jnp.exp(bl - b) * dks
            gq = qf * dqf - kf * dkf
            dlogf = csum + _cumsum_rows(triu, gq)
            dfv = dlogf / f - dkf
            dq_ref[sl, :] = (dqf * (sq * (1.0 + qp * (1.0 - sq)))).astype(BF16)
            df_ref[sl, :] = (dfv * (1.0 - lb) * sg * (1.0 - sg)).astype(BF16)
            di_ref[sl, :] = dv.astype(BF16)
            dst = dst * jnp.exp(bl) + upd
            dlb = dlb + _colsum(dfv * (1.0 - sg))
        dst_scr[...] = dst
        dlb_scr[...] = dlb

        @pl.when(step == nb - 1)
        def _():
            t = dlb * lb * (1.0 - lb)
            dlbl_ref[...] = jnp.concatenate([t, -t], axis=0)

    rev = lambda i: nb - 1 - i
    row = pl.BlockSpec((tb, HGRN_W), lambda i: (rev(i), 0))
    res = pl.pallas_call(
        body, name="hgrn_bwd", grid=(nb,),
        in_specs=_hgrn_specs(tb, rev) + [
            row, pl.BlockSpec((cpb, BLK, HGRN_W), lambda i: (rev(i), 0, 0)),
            pl.BlockSpec((1, BLK, HGRN_W), lambda i: (jnp.minimum((rev(i) + 1) * cpb, nc - 1), 0, 0)),
            row, pl.BlockSpec((2, HGRN_W), lambda i: (0, 0)), pl.BlockSpec((1, HGRN_W), lambda i: (0, 0))],
        out_specs=[row, row, row, row, pl.BlockSpec((1, HGRN_W), lambda i: (0, 0)),
                   pl.BlockSpec((2, HGRN_W), lambda i: (0, 0))],
        out_shape=[_sds((s, HGRN_W), BF16)] * 4 + [_sds((1, HGRN_W), F32), _sds((2, HGRN_W), F32)],
        scratch_shapes=[pltpu.VMEM((tb, HGRN_W), F32), pltpu.VMEM((BLK, HGRN_W), F32), pltpu.VMEM((1, HGRN_W), F32)],
        compiler_params=_cp(("arbitrary",)),
    )(proj, proj, proj, proj, o_pre, states, states, drec, lb_logits, out_gain)
    return res


def _place():
    return lax.axis_index("x"), lax.axis_index("y"), lax.axis_index("c")


def _flip(x, y, ox, oy):
    return (1 - x if ox else x), (1 - y if oy else y)


def _half(rows, cc):
    return pl.ds(cc * (rows // 2), rows // 2)


def _remote(src, dst, ssem, rsem, to):
    return pltpu.make_async_remote_copy(src_ref=src, dst_ref=dst, send_sem=ssem, recv_sem=rsem,
                                        device_id=to, device_id_type=MESH)


def _ag_chip_copies(ins, outs, ssem, rsem):
    x, y, c = _place()
    j = 2 * x + y
    cps = []
    for k in range(len(ins)):
        rows = ins[k].shape[0]
        for idx, (ox, oy) in enumerate(FLIPS):
            px, py = _flip(x, y, ox, oy)
            cps.append(_remote(ins[k].at[_half(rows, c)], outs[k].at[j, _half(rows, c)],
                               ssem.at[k, idx], rsem.at[k, idx], (px, py, c)))
    return cps


def _ag_start(ins, outs, ssem, rsem):
    for cp in _ag_chip_copies(ins, outs, ssem, rsem):
        cp.start()


def _ag_finish(ins, outs, ssem, rsem):
    x, y, c = _place()
    sib = (x, y, 1 - c)
    passed = []
    for k in range(len(ins)):
        rows = ins[k].shape[0]
        for idx, (ox, oy) in enumerate(FLIPS):
            px, py = _flip(x, y, ox, oy)
            blk = outs[k].at[2 * px + py, _half(rows, c)]
            _remote(blk, blk, ssem.at[k, idx], rsem.at[k, idx], (px, py, c)).wait_recv()
            cp = _remote(blk, blk, ssem.at[k, 3 + idx], rsem.at[k, 3 + idx], sib)
            cp.start()
            passed.append(cp)
    for k in range(len(ins)):
        rows = ins[k].shape[0]
        for idx, (ox, oy) in enumerate(FLIPS):
            px, py = _flip(x, y, ox, oy)
            blk = outs[k].at[2 * px + py, _half(rows, 1 - c)]
            _remote(blk, blk, ssem.at[k, 3 + idx], rsem.at[k, 3 + idx], sib).wait_recv()
    for cp in _ag_chip_copies(ins, outs, ssem, rsem) + passed:
        cp.wait_send()


def _ag_shapes(shards):
    nk = len(shards)
    return ([_sds((N_CHIPS,) + tuple(w.shape), w.dtype) for w in shards],
            [pltpu.SemaphoreType.DMA((nk, 6)), pltpu.SemaphoreType.DMA((nk, 6))])


def _with_own(gathered, shard, j):
    return lax.dynamic_update_index_in_dim(gathered, shard, j, 0)


def _rs_pair_copies(ins, outs, ssem, rsem):
    x, y, c = _place()
    return [_remote(ins[k].at[:, _half(ins[k].shape[1], 1 - c)], outs[k], ssem.at[k], rsem.at[k], (x, y, 1 - c))
            for k in range(len(ins))]


def _rs_pair_start(ins, outs, ssem, rsem):
    for cp in _rs_pair_copies(ins, outs, ssem, rsem):
        cp.start()


def _rs_pair_finish(ins, outs, ssem, rsem):
    for cp in _rs_pair_copies(ins, outs, ssem, rsem):
        cp.wait()


def _rs_pair_exchange(grads):
    nk = len(grads)
    return (grads, [_sds((N_CHIPS, g.shape[1] // 2, g.shape[2]), g.dtype) for g in grads],
            [pltpu.SemaphoreType.DMA((nk,)), pltpu.SemaphoreType.DMA((nk,))], _rs_pair_start, _rs_pair_finish)


def _rs_pair(name, grads):
    nk = len(grads)
    ins, out_shape, sems, start, finish = _rs_pair_exchange(grads)

    def body(*refs):
        start(refs[:nk], refs[nk:2 * nk], *refs[2 * nk:])
        finish(refs[:nk], refs[nk:2 * nk], *refs[2 * nk:])

    return pl.pallas_call(body, name=name, in_specs=[ANY] * nk, out_specs=[ANY] * nk, out_shape=out_shape,
                          scratch_shapes=sems)(*ins)


def _rs_chip_copies(ins, outs, ssem, rsem):
    x, y, c = _place()
    cps = []
    for k in range(len(ins)):
        for idx, (ox, oy) in enumerate(FLIPS):
            px, py = _flip(x, y, ox, oy)
            cps.append(_remote(ins[k].at[2 * px + py], outs[k].at[idx], ssem.at[k, idx], rsem.at[k, idx], (px, py, c)))
    return cps


def _rs_chips_start(ins, outs, ssem, rsem):
    for cp in _rs_chip_copies(ins, outs, ssem, rsem):
        cp.start()


def _rs_chips_finish(ins, outs, ssem, rsem):
    for cp in _rs_chip_copies(ins, outs, ssem, rsem):
        cp.wait()


def _rs_chips_shapes(psums):
    nk = len(psums)
    return ([_sds((3,) + tuple(p.shape[1:]), p.dtype) for p in psums],
            [pltpu.SemaphoreType.DMA((nk, 3)), pltpu.SemaphoreType.DMA((nk, 3))])


def _rs_share(fulls):
    nk = len(fulls)

    def body(*refs):
        ins, outs = refs[:nk], refs[nk:2 * nk]
        ssem, rsem = refs[2 * nk:]
        x, y, c = _place()
        cps = []
        for k in range(nk):
            rows = fulls[k].shape[0]
            cp = _remote(ins[k].at[_half(rows, c)], outs[k].at[_half(rows, c)], ssem.at[k], rsem.at[k], (x, y, 1 - c))
            cp.start()
            cps.append(cp)
        for k, cp in enumerate(cps):
            rows = fulls[k].shape[0]
            cp.wait_send()
            theirs = outs[k].at[_half(rows, 1 - c)]
            _remote(theirs, theirs, ssem.at[k], rsem.at[k], (x, y, 1 - c)).wait_recv()

    return pl.pallas_call(
        body, name="rs_share", in_specs=[ANY] * nk, out_specs=[ANY] * nk,
        out_shape=[_sds(f.shape, f.dtype) for f in fulls], input_output_aliases={k: k for k in range(nk)},
        scratch_shapes=[pltpu.SemaphoreType.DMA((nk,)), pltpu.SemaphoreType.DMA((nk,))],
    )(*fulls)


def _allreduce_small(v):
    ndev = 8

    def body(in_ref, out_ref, buf, ssem, rsem):
        x, y, c = _place()
        me = 4 * x + 2 * y + c
        buf[me] = in_ref[...]
        cps = []
        for k in range(1, ndev):
            ox, oy, oc = (k >> 2) & 1, (k >> 1) & 1, k & 1
            px, py = _flip(x, y, ox, oy)
            pc = 1 - c if oc else c
            cp = pltpu.make_async_remote_copy(src_ref=in_ref, dst_ref=buf.at[me], send_sem=ssem.at[k - 1],
                                              recv_sem=rsem.at[k - 1], device_id=(px, py, pc), device_id_type=MESH)
            cp.start()
            cps.append((cp, 4 * px + 2 * py + pc, (px, py, pc)))
        for k, (cp, src, peer) in enumerate(cps):
            cp.wait_send()
            pltpu.make_async_remote_copy(src_ref=in_ref, dst_ref=buf.at[src], send_sem=ssem.at[k],
                                         recv_sem=rsem.at[k], device_id=peer, device_id_type=MESH).wait_recv()
        acc = buf[0]
        for i in range(1, ndev):
            acc = acc + buf[i]
        out_ref[...] = acc

    return pl.pallas_call(
        body, name="allreduce_small",
        in_specs=[pl.BlockSpec(memory_space=pltpu.VMEM)], out_specs=pl.BlockSpec(memory_space=pltpu.VMEM),
        out_shape=_sds(v.shape, v.dtype),
        scratch_shapes=[pltpu.VMEM((ndev,) + v.shape, v.dtype), pltpu.SemaphoreType.DMA((ndev - 1,)),
                        pltpu.SemaphoreType.DMA((ndev - 1,))],
    )(v)


def _rs_sum1(name, g, recv, jc_idx):
    _, r, cdim = g.shape
    hr = r // 2
    tr = min(hr, 256)
    nr = hr // tr

    def body(jc_ref, g_ref, r_ref, o32_ref, o16_ref):
        v = g_ref[...] + r_ref[...].astype(F32)
        o16_ref[...] = v.astype(BF16)

        @pl.when(pl.program_id(1) == jc_ref[0])
        def _():
            o32_ref[...] = v

    spec = pl.BlockSpec((None, tr, cdim), lambda i, j, jc: (j, i, 0))
    return pl.pallas_call(
        body, name=name,
        grid_spec=pltpu.PrefetchScalarGridSpec(
            num_scalar_prefetch=1, grid=(nr, N_CHIPS),
            in_specs=[pl.BlockSpec((None, tr, cdim), lambda i, j, jc: (j, jc[1] * nr + i, 0)), spec],
            out_specs=[pl.BlockSpec((tr, cdim), lambda i, j, jc: (i, 0)), spec]),
        out_shape=[_sds((hr, cdim), F32), _sds((N_CHIPS, hr, cdim), BF16)],
        compiler_params=_cp(("parallel", "arbitrary")),
    )(jc_idx, g, recv)


def _rs_sum2(name, p32, recv, jc_idx):
    hr, cdim = p32.shape
    tr = min(hr, 256)
    nr = hr // tr

    def body(jc_ref, p_ref, r_ref, o_ref):
        o_ref[...] = ((p_ref[...] + r_ref[0].astype(F32)) + r_ref[1].astype(F32)) + r_ref[2].astype(F32)

    return pl.pallas_call(
        body, name=name,
        grid_spec=pltpu.PrefetchScalarGridSpec(
            num_scalar_prefetch=1, grid=(nr,),
            in_specs=[pl.BlockSpec((tr, cdim), lambda i, jc: (i, 0)),
                      pl.BlockSpec((3, tr, cdim), lambda i, jc: (0, i, 0))],
            out_specs=pl.BlockSpec((tr, cdim), lambda i, jc: (jc[1] * nr + i, 0))),
        out_shape=_sds((2 * hr, cdim), F32),
        compiler_params=_cp(("parallel",)),
    )(jc_idx, p32, recv)


def _adamw(name, w, g, m, v):
    r, cdim = w.shape
    tr = min(r, 256)
    c1 = 1.0 - ADAM_B1 ** ADAM_STEP
    c2 = 1.0 - ADAM_B2 ** ADAM_STEP

    def body(w_ref, g_ref, m_ref, v_ref, d_ref, nm_ref, nv_ref):
        gv = g_ref[...]
        nm = ADAM_B1 * m_ref[...] + (1.0 - ADAM_B1) * gv
        nv = ADAM_B2 * v_ref[...] + (1.0 - ADAM_B2) * (gv * gv)
        d_ref[...] = -ADAM_LR * ((nm / c1) / (jnp.sqrt(nv / c2) + ADAM_EPS) + ADAM_WD * w_ref[...])
        nm_ref[...] = nm
        nv_ref[...] = nv

    spec = pl.BlockSpec((tr, cdim), lambda i: (i, 0))
    return pl.pallas_call(
        body, name=name, grid=(r // tr,), in_specs=[spec] * 4, out_specs=[spec] * 3,
        out_shape=[_sds((r, cdim), F32)] * 3, compiler_params=_cp(("parallel",)),
    )(w, g, m, v)


def _pack_small(mix_pre, attn_out, lb_logits, hgrn_out, mix_post, mlp_pre, mlp_post, extra=None):
    spare = jnp.zeros((1, D_MODEL), F32)
    rows = [mix_pre, jnp.concatenate([attn_out, hgrn_out], axis=1),
            jnp.concatenate([lb_logits[0:1], lb_logits[1:2]], axis=1), mix_post, mlp_pre, mlp_post,
            spare if extra is None else extra, spare]
    return jnp.concatenate(rows, axis=0)


def _unpack_small(p):
    return (p[0:1], p[1:2, :ATTN_W], jnp.concatenate([p[2:3, :HGRN_W], p[2:3, HGRN_W:]], axis=0),
            p[1:2, ATTN_W:], p[3:4], p[4:5], p[5:6])


def kernel(x, mix_pre_norm, w_in, attn_out_norm, hgrn_lb_logits, hgrn_out_norm, w_out, mix_post_norm, mlp_pre_norm, w_ff1, w_ff2, mlp_post_norm, loss_target, m_mix_pre_norm, m_w_in, m_attn_out_norm, m_hgrn_lb_logits, m_hgrn_out_norm, m_w_out, m_mix_post_norm, m_mlp_pre_norm, m_w_ff1, m_w_ff2, m_mlp_post_norm, v_mix_pre_norm, v_w_in, v_attn_out_norm, v_hgrn_lb_logits, v_hgrn_out_norm, v_w_out, v_mix_post_norm, v_mlp_pre_norm, v_w_ff1, v_w_ff2, v_mlp_post_norm):
    s = x.shape[1]
    xs = x.reshape(s, D_MODEL)
    tgt = loss_target.reshape(s, D_MODEL)
    cx, cy, cc = _place()
    chip = 2 * cx + cy
    jc_idx = jnp.stack([chip, cc]).astype(jnp.int32)

    big_w = [w_in[0], w_out[0], w_ff1[0], w_ff2[0]]
    big_m = [m_w_in[0], m_w_out[0], m_w_ff1[0], m_w_ff2[0]]
    big_v = [v_w_in[0], v_w_out[0], v_w_ff1[0], v_w_ff2[0]]
    w_in_bf = big_w[0].astype(BF16)

    h, *rest_bf, wg_in = _rows_call(
        "norm_in", lambda xv, g, *ws: (xv * _rstd(xv) * g, *ws),
        [(xs, _row(D_MODEL)), (mix_pre_norm, "full")] + [(w, _row(D_MODEL)) for w in big_w[1:]],
        [(D_MODEL, BF16, "row")] + [(D_MODEL, BF16, "row", w.shape[0]) for w in big_w[1:]], s,
        exchange=([w_in_bf], *_ag_shapes([w_in_bf]), _ag_start, _ag_finish))
    shards = [w_in_bf, *rest_bf]
    wg_in = _with_own(wg_in, w_in_bf, chip)
    (proj,) = _mm_cols("mm_proj", h, wg_in, NN, [F32])
    hg_o, rec, states = _hgrn_fwd(proj, hgrn_lb_logits, hgrn_out_norm)
    attn_o, attn_lse, wg_out, wg_1, wg_2 = _attn_fwd(proj, shards[1:])
    wg_out, wg_1, wg_2 = (_with_own(g, w, chip) for g, w in zip((wg_out, wg_1, wg_2), shards[1:]))
    (attn_n,) = _rows_call("attn_norm", lambda o, gain: (o * _rstd(o) * gain,),
                           [(attn_o, _row(ATTN_W)), (attn_out_norm, "full")], [(ATTN_W, BF16, "row")], s)
    cat = jnp.concatenate([attn_n, rec], axis=1)

    def post1(mv, xv, g_post, g_pre2):
        x1 = xv + mv * _rstd(mv) * g_post
        return mv, x1, x1 * _rstd(x1) * g_pre2

    mixed, x1, h2 = _rows_call(
        "mm_mixed", post1, [(xs, _row(D_MODEL)), (mix_post_norm, "full"), (mlp_pre_norm, "full")],
        [(D_MODEL, F32, "row"), (D_MODEL, F32, "row"), (D_MODEL, BF16, "row")], s,
        matmul=(cat, wg_out.reshape(D_MODEL, D_MODEL)))

    def sq_relu(u):
        r = jnp.maximum(u, 0.0)
        return r * r, r

    act, ru = _mm_cols("mm_ff1", h2, wg_1, NN, [BF16, BF16], epi=sq_relu)

    def post2(fv, x1v, tv, g):
        y = x1v + fv * _rstd(fv) * g
        dy = (y - tv) * (1.0 / D_MODEL)
        err = y - tv
        loss = 0.5 * jnp.sum(jnp.mean(err * err, axis=-1, keepdims=True), axis=0, keepdims=True)
        dff, dgc = _norm_bwd(fv, g, dy)
        return dy, dff, _colsum(dgc), jnp.broadcast_to(loss, (1, BLK))

    dy, dff, g_mlp_post, loss_part = _rows_call(
        "mm_ff2", post2, [(x1, _row(D_MODEL)), (tgt, _row(D_MODEL)), (mlp_post_norm, "full")],
        [(D_MODEL, F32, "row"), (D_MODEL, BF16, "row"), (D_MODEL, F32, "acc"), (BLK, F32, "acc")], s,
        matmul=(act, wg_2.reshape(D_FF, D_MODEL)))

    (du,) = _mm_cols("mm_du", dff, wg_2, NT, [BF16], epi=lambda acc, r: (acc * (2.0 * r.astype(F32)),),
                     extras=(ru,))
    gw_2 = _mm_wgrad("mm_gw2", act, dff, True)
    gw_1 = _mm_wgrad("mm_gw1", h2, du, False)

    def bwd_mid(dh2v, dyv, x1v, mv, g_pre2, g_post):
        d1, gc1 = _norm_bwd(x1v, g_pre2, dh2v)
        dx1 = dyv + d1
        dm, gc2 = _norm_bwd(mv, g_post, dx1)
        return dx1, dm, _colsum(gc1), _colsum(gc2)

    dx1, dmixed, g_mlp_pre, g_mix_post, *from_pair = _rows_call(
        "mm_dh2", bwd_mid, [(dy, _row(D_MODEL)), (x1, _row(D_MODEL)), (mixed, _row(D_MODEL)),
                            (mlp_pre_norm, "full"), (mix_post_norm, "full")],
        [(D_MODEL, F32, "row"), (D_MODEL, BF16, "row"), (D_MODEL, F32, "acc"), (D_MODEL, F32, "acc")], s,
        matmul=(du, wg_1), exchange=_rs_pair_exchange([gw_1[1], gw_2[1]]))

    def attn_norm_bwd(dc, o, gain):
        do, gc = _norm_bwd(o, gain, dc[:, :ATTN_W])
        t = do * o
        lane = lax.broadcasted_iota(jnp.int32, (t.shape[0], BLK), 1) < 64
        parts = []
        for p in range(ATTN_W // BLK):
            tp = t[:, p * BLK:(p + 1) * BLK]
            sa = jnp.sum(jnp.where(lane, tp, 0.0), axis=1, keepdims=True)
            sb = jnp.sum(jnp.where(lane, 0.0, tp), axis=1, keepdims=True)
            parts.append(jnp.where(lane, sa, sb))
        return do, jnp.concatenate(parts, axis=1), dc[:, ATTN_W:], _colsum(gc)

    gw_out = _mm_wgrad("mm_gwout", cat, dmixed, True)
    do_attn, delta, drec, g_attn_out, from_pair_out = _rows_call(
        "mm_dcat", attn_norm_bwd, [(attn_o, _row(ATTN_W)), (attn_out_norm, "full")],
        [(ATTN_W, F32, "row"), (ATTN_W, F32, "row"), (HGRN_W, F32, "row"), (ATTN_W, F32, "acc")], s,
        matmul=(dmixed, wg_out.reshape(1, D_MODEL, D_MODEL)), exchange=_rs_pair_exchange([gw_out[1]]))
    names = ["out", "ff1", "ff2", "in"]
    ready = [gw_out, gw_1, gw_2]
    from_pair = [from_pair_out] + from_pair
    pair =[_rs_sum1(f"rs_sum1_{n}", g[0], r, jc_idx) for n, g, r in zip(names, ready, from_pair)]

    dq, dk, dv, *from_chips = _attn_bwd(proj, do_attn, attn_lse, delta, [p[1] for p in pair])
    dhq, dhf, dhi, dhg, g_hgrn_out, g_lb = _hgrn_bwd(proj, hg_o, states, drec, hgrn_lb_logits, hgrn_out_norm)

    dproj = jnp.concatenate([dq, dk, dv, dhq, dhf, dhi, dhg], axis=1)
    gw_in = _mm_wgrad("mm_gwin", h, dproj, False)
    (from_pair_in,) = _rs_pair("rs_pair_in", [gw_in[1]])
    pair.append(_rs_sum1("rs_sum1_in", gw_in[0], from_pair_in, jc_idx))
    rs_shape, rs_sems = _rs_chips_shapes([pair[3][1]])

    def bwd_in(dhv, dx1v, xv, g):
        d0, gc = _norm_bwd(xv, g, dhv)
        return dx1v + d0, _colsum(gc)

    grad_x, g_mix_pre, from_chips_in = _rows_call(
        "mm_dh", bwd_in, [(dx1, _row(D_MODEL)), (xs, _row(D_MODEL)), (mix_pre_norm, "full")],
        [(D_MODEL, F32, "row"), (D_MODEL, F32, "acc")], s, matmul=(dproj, wg_in),
        exchange=([pair[3][1]], rs_shape, rs_sems, _rs_chips_start, _rs_chips_finish))
    from_chips.append(from_chips_in)

    loss_row = jnp.pad(loss_part, ((0, 0), (0, D_MODEL - BLK)))
    small_g = _allreduce_small(_pack_small(g_mix_pre, g_attn_out, g_lb, g_hgrn_out, g_mix_post, g_mlp_pre, g_mlp_post,
                                           extra=loss_row))
    loss = small_g[6, 0]

    reduced = [_rs_sum2(f"rs_sum2_{n}", p[0], r, jc_idx) for n, p, r in zip(names, pair, from_chips)]
    g_wout, g_w1, g_w2, g_win = _rs_share(reduced)
    full = [g_win, g_wout, g_w1, g_w2]

    upd = [_adamw(f"adamw_{n}", w, g, m, v) for n, w, g, m, v in zip(("in", "out", "ff1", "ff2"), big_w, full, big_m, big_v)]
    small_w = _pack_small(mix_pre_norm, attn_out_norm, hgrn_lb_logits, hgrn_out_norm, mix_post_norm, mlp_pre_norm,
                          mlp_post_norm)
    small_m = _pack_small(m_mix_pre_norm, m_attn_out_norm, m_hgrn_lb_logits, m_hgrn_out_norm, m_mix_post_norm,
                          m_mlp_pre_norm, m_mlp_post_norm)
    small_v = _pack_small(v_mix_pre_norm, v_attn_out_norm, v_hgrn_lb_logits, v_hgrn_out_norm, v_mix_post_norm,
                          v_mlp_pre_norm, v_mlp_post_norm)
    small_upd = _adamw("adamw_small", small_w, small_g, small_m, small_v)

    def assemble(small, big):
        sm = _unpack_small(small)
        return (sm[0], big[0][None], sm[1], sm[2], sm[3], big[1][None], sm[4], sm[5], big[2][None], big[3][None], sm[6])

    g_out = assemble(small_g, full)
    d_out = assemble(small_upd[0], [u[0] for u in upd])
    m_out = assemble(small_upd[1], [u[1] for u in upd])
    v_out = assemble(small_upd[2], [u[2] for u in upd])
    return (loss, grad_x.reshape(x.shape), *g_out, *d_out, *m_out, *v_out)
```

```python
import numpy as np
import jax
import jax.numpy as jnp
from jax import lax
from jax.experimental import pallas as pl
from jax.experimental.pallas import tpu as pltpu

F32 = jnp.float32
BF16 = jnp.bfloat16
MESH = pl.DeviceIdType.MESH
ANY = pl.BlockSpec(memory_space=pl.ANY)

RMS_EPS = 1e-6
D_MODEL = 1024
ATTN_W = 512
HGRN_W = 512
D_FF = 4096
N_CHIPS = 4
BLK = 128
CHUNK = 64
HGRN_TB = 512
HGRN_LANES = 256
ATTN_GROUP = 8
DILATIONS = (1, 4, 16)
ATTN_SCALE = 0.125
ROW_TILE = 512
MM_TILE = 1024
VMEM_LIMIT = 48 * 2 ** 20
FLIPS = ((1, 0), (0, 1), (1, 1))

ADAM_LR, ADAM_B1, ADAM_B2, ADAM_EPS, ADAM_WD, ADAM_STEP = 0.001, 0.9, 0.999, 1e-08, 0.01, 10


def _cp(sem=None):
    return pltpu.CompilerParams(dimension_semantics=sem, vmem_limit_bytes=VMEM_LIMIT)


def _sigmoid(v):
    return 1.0 / (1.0 + jnp.exp(-v))


def _dot(a, b, contract, precision=None):
    return lax.dot_general(a, b, (contract, ((), ())), preferred_element_type=F32, precision=precision)


NN = ((1,), (0,))
NT = ((1,), (1,))
TN = ((0,), (0,))


def _sds(shape, dtype):
    return jax.ShapeDtypeStruct(shape, dtype)


def _resident(shape):
    return pl.BlockSpec(shape, lambda *_: (0,) * len(shape), pipeline_mode=pl.Buffered(1))


def _mm_cols(name, a, w, contract, out_dtypes, epi=None, extras=()):
    m, k = a.shape
    jn = w.shape[0]
    nj = w.shape[2] if contract == NN else w.shape[1]
    tm = min(m, MM_TILE)
    n_ex, parts = len(extras), 2

    def body(a_ref, w_ref, *rest):
        ex, out_refs = rest[:n_ex], rest[n_ex:]
        i = pl.program_id(1)
        part = tm // parts
        for h in range(parts):
            rows = slice(h * part, (h + 1) * part)
            acc = _dot(a_ref[pl.ds(pl.multiple_of(i * tm + h * part, part), part), :], w_ref[...], contract)
            res = epi(acc, *[e[rows, :] for e in ex]) if epi else (acc,)
            for o, r in zip(out_refs, res):
                o[rows, :] = r.astype(o.dtype)

    blk = pl.BlockSpec((tm, nj), lambda j, i: (i, j))
    return pl.pallas_call(
        body, name=name, grid=(jn, m // tm),
        in_specs=[_resident((m, k)), pl.BlockSpec((None,) + w.shape[1:], lambda j, i: (j, 0, 0))] + [blk] * n_ex,
        out_specs=[blk] * len(out_dtypes), out_shape=[_sds((m, jn * nj), dt) for dt in out_dtypes],
        compiler_params=_cp(("parallel", "parallel")),
    )(a, w, *extras)


def _mm_wgrad(name, a, b, a_by_j):
    s = a.shape[0]
    if a_by_j:
        r, c = a.shape[1] // N_CHIPS, b.shape[1]
        in_specs = [pl.BlockSpec((s, r), lambda j: (0, j)), _resident((s, c))]
    else:
        r, c = a.shape[1], b.shape[1] // N_CHIPS
        in_specs = [_resident((s, r)), pl.BlockSpec((s, c), lambda j: (0, j))]
    tr = min(r, 512)

    def body(a_ref, b_ref, o32_ref, o16_ref):
        for h in range(r // tr):
            cols = slice(h * tr, (h + 1) * tr)
            acc = _dot(a_ref[:, cols], b_ref[...], TN)
            o32_ref[cols, :] = acc
            o16_ref[cols, :] = acc.astype(BF16)

    out = pl.BlockSpec((None, r, c), lambda j: (j, 0, 0))
    return pl.pallas_call(
        body, name=name, grid=(N_CHIPS,), in_specs=in_specs, out_specs=[out, out],
        out_shape=[_sds((N_CHIPS, r, c), F32), _sds((N_CHIPS, r, c), BF16)], compiler_params=_cp(("parallel",)),
    )(a, b)


def _rows_call(name, fn, ins, outs, s, tm=ROW_TILE, matmul=None, exchange=None):
    steps = s // tm
    in_specs = []
    if matmul:
        a, w = matmul
        in_specs += [pl.BlockSpec((tm, a.shape[1]), lambda i: (i, 0)), _resident(w.shape)]
    for arr, kind in ins:
        if kind == "full":
            in_specs.append(pl.BlockSpec(arr.shape, lambda i: (0, 0)))
        else:
            _, w_, cb = kind
            in_specs.append(pl.BlockSpec((arr.shape[0] // steps, w_), lambda i, cb=cb: (i, cb)))
    out_specs, out_shape, is_acc = [], [], []
    for w_, dt, kind, *rows_all in outs:
        if kind == "acc":
            out_specs.append(pl.BlockSpec((1, w_), lambda i: (0, 0)))
            out_shape.append(_sds((1, w_), dt))
        else:
            rows_all = rows_all[0] if rows_all else s
            out_specs.append(pl.BlockSpec((rows_all // steps, w_), lambda i: (i, 0)))
            out_shape.append(_sds((rows_all, w_), dt))
        is_acc.append(kind == "acc")
    n_mm, n_in, n_out = (2 if matmul else 0), len(ins), len(outs)
    x_ins, x_shapes, x_sems, x_start, x_finish = exchange if exchange else ((), [], [], None, None)
    n_x = len(x_ins)

    def body(*refs):
        in_refs, xi = refs[n_mm:n_mm + n_in], refs[n_mm + n_in:n_mm + n_in + n_x]
        out_refs = refs[n_mm + n_in + n_x:n_mm + n_in + n_x + n_out]
        xo, sems = refs[n_mm + n_in + n_x + n_out:n_mm + n_in + 2 * n_x + n_out], refs[n_mm + n_in + 2 * n_x + n_out:]
        i = pl.program_id(0)

        if exchange:
            @pl.when(i == 0)
            def _():
                x_start(xi, xo, *sems)

        for o, acc in zip(out_refs, is_acc):
            if acc:
                @pl.when(i == 0)
                def _(o=o):
                    o[...] = jnp.zeros_like(o)

        parts = 2 if matmul else 1
        for h in range(parts):
            rows = slice(h * (tm // parts), (h + 1) * (tm // parts)) if matmul else slice(None)
            args = [r[...] if kind == "full" else r[rows, :] for r, (_, kind) in zip(in_refs, ins)]
            if matmul:
                a_ref, w_ref = refs[:2]
                if len(w_ref.shape) == 2:
                    acc = _dot(a_ref[rows, :], w_ref[...], NN)
                else:
                    kj = w_ref.shape[2]
                    acc = _dot(a_ref[rows, 0:kj], w_ref[0], NT)
                    for j in range(1, w_ref.shape[0]):
                        acc = acc + _dot(a_ref[rows, j * kj:(j + 1) * kj], w_ref[j], NT)
                args.insert(0, acc)
            for o, r, acc in zip(out_refs, fn(*args), is_acc):
                if acc:
                    o[...] += r.astype(o.dtype)
                else:
                    o[rows, :] = r.astype(o.dtype)

        if exchange:
            @pl.when(i == steps - 1)
            def _():
                x_finish(xi, xo, *sems)

    sem = ("arbitrary",) if any(is_acc) or exchange else ("parallel",)
    return pl.pallas_call(
        body, name=name, grid=(steps,), in_specs=in_specs + [ANY] * n_x, out_specs=out_specs + [ANY] * n_x,
        out_shape=out_shape + list(x_shapes), scratch_shapes=list(x_sems), compiler_params=_cp(sem),
    )(*(matmul or ()), *[a for a, _ in ins], *x_ins)


def _rstd(v):
    return lax.rsqrt(jnp.mean(v * v, axis=-1, keepdims=True) + RMS_EPS)


def _norm_bwd(v, gain, dy):
    r = _rstd(v)
    n = v * r
    dn = dy * gain
    dv = r * (dn - n * jnp.mean(dn * n, axis=-1, keepdims=True))
    return dv, dy * n


def _colsum(v):
    return jnp.sum(v, axis=0, keepdims=True)


def _row(w, cb=0):
    return ("row", w, cb)


N_PAIRS = ATTN_W // BLK


def _head_col(v, mask):
    return jnp.max(jnp.where(mask, v, -jnp.inf), axis=1, keepdims=True)


def _slopes():
    t = np.zeros((N_PAIRS, 8, 2 * BLK), np.float32)
    for p in range(N_PAIRS):
        for hh in range(2):
            t[p, hh, :] = 2.0 ** -(2 * p + hh + 1)
    return jnp.asarray(t)


def _rows(n, r, d):
    base = pl.multiple_of(n * (BLK * d), BLK)
    return pl.ds(base + r, BLK, stride=d) if d > 1 else pl.ds(base, BLK)


def _attn_bias(sl_ref, bias_scr):
    row = lax.broadcasted_iota(jnp.int32, (BLK, 2 * BLK), 0)
    col = lax.broadcasted_iota(jnp.int32, (BLK, 2 * BLK), 1)
    dist = row + BLK - col
    in_window = (dist >= 0) & (dist <= BLK)
    distf = dist.astype(F32)
    for di, d in enumerate(DILATIONS):
        for hh in range(2):
            bias_scr[2 * di + hh] = jnp.where(in_window, -(sl_ref[hh:hh + 1, :] * float(d)) * distf, -1e30)


def _first_block_penalty(n):
    col = lax.broadcasted_iota(jnp.int32, (1, 2 * BLK), 1)
    return jnp.where(col + n * BLK >= BLK, 0.0, -1e30)


def _attn_groups(s, d):
    nb = s // (BLK * d)
    g = ATTN_GROUP
    if d >= g:
        return [(nb, lambda n, r0=r0: [(n, r0 + u) for u in range(g)]) for r0 in range(0, d, g)]
    per = g // d
    return [(nb // per, lambda t: [(per * t + u, r) for u in range(per) for r in range(d)])]


def _attn_fwd(proj, shards):
    s = proj.shape[0]
    nk = len(shards)

    def body(sl_ref, q_ref, k_ref, v_ref, *rest):
        w_refs, (o_ref, l_ref) = rest[:nk], rest[nk:nk + 2]
        wg_refs, (bias_scr, ssem, rsem) = rest[nk + 2:2 * nk + 2], rest[2 * nk + 2:]
        pair = pl.program_id(0)

        @pl.when(pair == 0)
        def _():
            _ag_start(w_refs, wg_refs, ssem, rsem)

        _attn_bias(sl_ref, bias_scr)
        lane_q = lax.broadcasted_iota(jnp.int32, (BLK, BLK), 1) < 64
        lane_k = lax.broadcasted_iota(jnp.int32, (2 * BLK, BLK), 1) < 64

        def branch(n, r, di):
            d = DILATIONS[di]
            rows = _rows(n, r, d)
            prev = _rows(jnp.maximum(n - 1, 0), r, d)
            pen = _first_block_penalty(n)
            q2 = q_ref[rows, :] * ATTN_SCALE
            kk = jnp.concatenate([k_ref[prev, :], k_ref[rows, :]], axis=0).astype(BF16)
            vv = jnp.concatenate([v_ref[prev, :], v_ref[rows, :]], axis=0)
            o2 = jnp.zeros((BLK, BLK), F32)
            lse2 = jnp.zeros((BLK, BLK), F32)
            for hh in range(2):
                mq = lane_q if hh == 0 else ~lane_q
                mk = lane_k if hh == 0 else ~lane_k
                qm = jnp.where(mq, q2, 0.0).astype(BF16)
                sc = _dot(qm, kk, NT) + bias_scr[2 * di + hh] + pen
                m = jnp.max(sc, axis=1, keepdims=True)
                pr = jnp.exp(sc - m)
                den = jnp.sum(pr, axis=1, keepdims=True)
                vm = jnp.where(mk, vv, 0.0).astype(BF16)
                o2 = o2 + _dot(pr.astype(BF16), vm, NN) / den
                lse2 = jnp.where(mq, m + jnp.log(den), lse2)
            return rows, o2, lse2

        def merge(rows, o2, lse2, first):
            if first:
                o_ref[rows, :] = o2
                l_ref[rows, :] = lse2
            else:
                lo = l_ref[rows, :]
                mx = jnp.maximum(lo, lse2)
                ln = mx + jnp.log(jnp.exp(lo - mx) + jnp.exp(lse2 - mx))
                o_ref[rows, :] = jnp.exp(lo - ln) * o_ref[rows, :] + jnp.exp(lse2 - ln) * o2
                l_ref[rows, :] = ln

        for di, d in enumerate(DILATIONS):
            for trips, blocks in _attn_groups(s, d):
                def trip(t, carry, di=di, blocks=blocks):
                    done = [branch(n, r, di) for n, r in blocks(t)]
                    for rows, o2, lse2 in done:
                        merge(rows, o2, lse2, di == 0)
                    return carry

                lax.fori_loop(0, trips, trip, 0)

        @pl.when(pair == N_PAIRS - 1)
        def _():
            _ag_finish(w_refs, wg_refs, ssem, rsem)

    cb = lambda base: pl.BlockSpec((s, BLK), lambda p, base=base: (0, base + p))
    out = pl.BlockSpec((s, BLK), lambda p: (0, p))
    ag_shape, ag_sems = _ag_shapes(shards)
    return pl.pallas_call(
        body, name="attn_fwd", grid=(N_PAIRS,),
        in_specs=[pl.BlockSpec((None, 8, 2 * BLK), lambda p: (p, 0, 0)), cb(0), cb(N_PAIRS), cb(2 * N_PAIRS)]
        + [ANY] * nk,
        out_specs=[out, out] + [ANY] * nk, out_shape=[_sds((s, ATTN_W), F32)] * 2 + ag_shape,
        scratch_shapes=[pltpu.VMEM((2 * len(DILATIONS), BLK, 2 * BLK), F32)] + ag_sems,
        compiler_params=_cp(("arbitrary",)),
    )(_slopes(), proj, proj, proj, *shards)


def _attn_bwd(proj, do, lse, delta, psums):
    s = proj.shape[0]
    nk = len(psums)

    def body(sl_ref, q_ref, k_ref, v_ref, do_ref, l_ref, e_ref, *rest):
        p_refs, out16 = rest[:nk], rest[nk:nk + 3]
        got_refs, (dq_ref, dk_ref, dv_ref, bias_scr, ssem, rsem) = rest[nk + 3:2 * nk + 3], rest[2 * nk + 3:]
        pair = pl.program_id(0)

        @pl.when(pair == 0)
        def _():
            _rs_chips_start(p_refs, got_refs, ssem, rsem)

        _attn_bias(sl_ref, bias_scr)
        lane_q = lax.broadcasted_iota(jnp.int32, (BLK, BLK), 1) < 64
        lane_k = lax.broadcasted_iota(jnp.int32, (2 * BLK, BLK), 1) < 64
        dk_ref[...] = jnp.zeros_like(dk_ref)
        dv_ref[...] = jnp.zeros_like(dv_ref)

        def branch(n, r, di):
            d = DILATIONS[di]
            rows = _rows(n, r, d)
            prev = _rows(jnp.maximum(n - 1, 0), r, d)
            pen = _first_block_penalty(n)
            q1, d1, l1, e1 = q_ref[rows, :] * ATTN_SCALE, do_ref[rows, :], l_ref[rows, :], e_ref[rows, :]
            kk = jnp.concatenate([k_ref[prev, :], k_ref[rows, :]], axis=0)
            kkb = kk.astype(BF16)
            vvb = jnp.concatenate([v_ref[prev, :], v_ref[rows, :]], axis=0).astype(BF16)
            dq2 = jnp.zeros((BLK, BLK), F32)
            dkk = jnp.zeros((2 * BLK, BLK), F32)
            dvv = jnp.zeros((2 * BLK, BLK), F32)
            for hh in range(2):
                mq = lane_q if hh == 0 else ~lane_q
                mk = lane_k if hh == 0 else ~lane_k
                qm = jnp.where(mq, q1, 0.0).astype(BF16)
                dm = jnp.where(mq, d1, 0.0).astype(BF16)
                sc = _dot(qm, kkb, NT) + bias_scr[2 * di + hh] + pen
                pr = jnp.exp(sc - _head_col(l1, mq))
                ds = (pr * (_dot(dm, vvb, NT) - _head_col(e1, mq))).astype(BF16)
                km = jnp.where(mk, kk, 0.0).astype(BF16)
                dq2 = dq2 + _dot(ds, km, NN)
                dkk = dkk + _dot(ds, qm, TN)
                dvv = dvv + _dot(pr.astype(BF16), dm, TN)
            return rows, prev, dq2 * ATTN_SCALE, dkk, dvv

        for di, d in enumerate(DILATIONS):
            for trips, blocks in _attn_groups(s, d):
                def trip(t, carry, di=di, blocks=blocks, first=(di == 0)):
                    done = [branch(n, r, di) for n, r in blocks(t)]
                    for rows, prev, dq2, dkk, dvv in done:
                        dq_ref[rows, :] = dq2 if first else dq_ref[rows, :] + dq2
                        dk_ref[prev, :] = dk_ref[prev, :] + dkk[:BLK]
                        dk_ref[rows, :] = dk_ref[rows, :] + dkk[BLK:]
                        dv_ref[prev, :] = dv_ref[prev, :] + dvv[:BLK]
                        dv_ref[rows, :] = dv_ref[rows, :] + dvv[BLK:]
                    return carry

                lax.fori_loop(0, trips, trip, 0)

        for o16, acc in zip(out16, (dq_ref, dk_ref, dv_ref)):
            o16[...] = acc[...].astype(BF16)

        @pl.when(pair == N_PAIRS - 1)
        def _():
            _rs_chips_finish(p_refs, got_refs, ssem, rsem)

    cb = lambda base: pl.BlockSpec((s, BLK), lambda p, base=base: (0, base + p))
    out = pl.BlockSpec((s, BLK), lambda p: (0, p))
    rs_shape, rs_sems = _rs_chips_shapes(psums)
    return pl.pallas_call(
        body, name="attn_bwd", grid=(N_PAIRS,),
        in_specs=[pl.BlockSpec((None, 8, 2 * BLK), lambda p: (p, 0, 0)), cb(0), cb(N_PAIRS), cb(2 * N_PAIRS),
                  out, out, out] + [ANY] * nk,
        out_specs=[out] * 3 + [ANY] * nk, out_shape=[_sds((s, ATTN_W), BF16)] * 3 + rs_shape,
        scratch_shapes=[pltpu.VMEM((s, BLK), F32)] * 3 + [pltpu.VMEM((2 * len(DILATIONS), BLK, 2 * BLK), F32)] + rs_sems,
        compiler_params=_cp(("arbitrary",)),
    )(_slopes(), proj, proj, proj, do, lse, delta, *psums)


def _lower_bound(lbl):
    return 1.0 / (1.0 + jnp.exp(lbl[1:2, :] - lbl[0:1, :]))


def _hi(a):
    bits = lax.bitcast_convert_type(a, jnp.uint32) & jnp.uint32(0xFFFF0000)
    return lax.bitcast_convert_type(bits, F32)


def _dot3(a, b, contract):
    ah, bh = _hi(a), _hi(b)
    al, bl = (a - ah).astype(BF16), (b - bh).astype(BF16)
    ah, bh = ah.astype(BF16), bh.astype(BF16)
    return _dot(ah, bh, contract) + (_dot(ah, bl, contract) + _dot(al, bh, contract))


def _cumsum_rows(tri, g):
    g1 = _hi(g)
    r1 = g - g1
    g2 = _hi(r1)
    g3 = r1 - g2
    return _dot(tri, g1.astype(BF16), NN) + (_dot(tri, g2.astype(BF16), NN) + _dot(tri, g3.astype(BF16), NN))


def _heads(fn):
    return jnp.concatenate([fn(slice(h * BLK, (h + 1) * BLK)) for h in range(HGRN_W // BLK)], axis=1)


def _head_mean(t):
    return _heads(lambda hs: jnp.broadcast_to(jnp.mean(t[:, hs], axis=1, keepdims=True), (t.shape[0], BLK)))


def _hgrn_chunk(q_ref, f_ref, i_ref, sl, lb, tri, ls=slice(None)):
    qp = q_ref[sl, ls]
    sq = _sigmoid(qp)
    qf = qp * sq
    sg = _sigmoid(f_ref[sl, ls])
    f = lb + (1.0 - lb) * sg
    kf = 1.0 - f
    v = i_ref[sl, ls]
    b = _cumsum_rows(tri, jnp.log(f))
    bm = b[CHUNK // 2:CHUNK // 2 + 1, :]
    bl = b[CHUNK - 1:CHUNK, :]
    qt = qf * jnp.exp(b - bm)
    kt = kf * jnp.exp(bm - b)
    return qp, sq, qf, sg, f, kf, v, b, bm, bl, qt, kt


def _hgrn_specs(tb, block):
    first = 3 * ATTN_W // HGRN_W
    return [pl.BlockSpec((tb, HGRN_W), lambda i, k=k: (block(i), first + k)) for k in range(4)]


def _hgrn_fwd(proj, lb_logits, out_gain):
    s = proj.shape[0]
    tb = min(HGRN_TB, s)
    nb, cpb, nc = s // tb, tb // CHUNK, s // CHUNK

    def body(q_ref, f_ref, i_ref, g_ref, lbl_ref, gain_ref, o_ref, rec_ref, st_ref, st_scr):
        step = pl.program_id(0)

        @pl.when(step == 0)
        def _():
            st_scr[...] = jnp.zeros_like(st_scr)

        lb = _lower_bound(lbl_ref[...])
        r64 = lax.broadcasted_iota(jnp.int32, (CHUNK, CHUNK), 0)
        c64 = lax.broadcasted_iota(jnp.int32, (CHUNK, CHUNK), 1)
        tril = r64 >= c64
        tri = tril.astype(BF16)
        st = st_scr[...]
        for cc in range(cpb):
            sl = slice(cc * CHUNK, (cc + 1) * CHUNK)
            _, _, qf, _, _, kf, v, b, _, bl, qt, kt = _hgrn_chunk(q_ref, f_ref, i_ref, sl, lb, tri)
            qe = (qf * jnp.exp(b)).astype(BF16)
            kh = (kf * jnp.exp(bl - b)).astype(BF16)
            qtb, ktb, vb, stb = qt.astype(BF16), kt.astype(BF16), v.astype(BF16), st.astype(BF16)

            def out_h(hs):
                a = jnp.where(tril, _dot(qtb[:, hs], ktb[:, hs], NT), 0.0).astype(BF16)
                return _dot(qe[:, hs], stb[:, hs], NT) + _dot(a, vb[:, hs], NN)

            o_ref[sl, :] = _heads(out_h)
            st_ref[cc] = stb
            st = st * jnp.exp(bl) + _heads(lambda hs: _dot(vb[:, hs], kh[:, hs], TN))
        st_scr[...] = st
        o = o_ref[...]
        gate = g_ref[...]
        rec_ref[...] = (o * lax.rsqrt(_head_mean(o * o) + RMS_EPS) * gain_ref[...] * (gate * _sigmoid(gate))).astype(BF16)

    row = pl.BlockSpec((tb, HGRN_W), lambda i: (i, 0))
    return pl.pallas_call(
        body, name="hgrn_fwd", grid=(nb,),
        in_specs=_hgrn_specs(tb, lambda i: i) + [pl.BlockSpec((2, HGRN_W), lambda i: (0, 0)),
                                                 pl.BlockSpec((1, HGRN_W), lambda i: (0, 0))],
        out_specs=[row, row, pl.BlockSpec((cpb, BLK, HGRN_W), lambda i: (i, 0, 0))],
        out_shape=[_sds((s, HGRN_W), F32), _sds((s, HGRN_W), BF16), _sds((nc, BLK, HGRN_W), BF16)],
        scratch_shapes=[pltpu.VMEM((BLK, HGRN_W), F32)],
        compiler_params=_cp(("arbitrary",)),
    )(proj, proj, proj, proj, lb_logits, out_gain)


def _hgrn_bwd(proj, o_pre, states, drec, lb_logits, out_gain):
    s = proj.shape[0]
    tb = min(HGRN_TB, s)
    nb, cpb, nc = s // tb, tb // CHUNK, s // CHUNK

    def body(q_ref, f_ref, i_ref, g_ref, o_ref, st_ref, stn_ref, dy_ref, lbl_ref, gain_ref,
             dq_ref, df_ref, di_ref, dg_ref, dgain_ref, dlbl_ref, do_scr, dst_scr, dlb_scr):
        step = pl.program_id(0)

        @pl.when(step == 0)
        def _():
            dst_scr[...] = jnp.zeros_like(dst_scr)
            dlb_scr[...] = jnp.zeros_like(dlb_scr)
            dgain_ref[...] = jnp.zeros_like(dgain_ref)

        lb = _lower_bound(lbl_ref[...])
        gain = gain_ref[...]
        o = o_ref[...]
        r = lax.rsqrt(_head_mean(o * o) + RMS_EPS)
        nrm = o * r
        gate = g_ref[...]
        sgt = _sigmoid(gate)
        dy = dy_ref[...]
        dg_ref[...] = (dy * nrm * gain * (sgt * (1.0 + gate * (1.0 - sgt)))).astype(BF16)
        dng = dy * (gate * sgt)
        dgain_ref[...] += _colsum(dng * nrm)
        dn = dng * gain
        do_scr[...] = r * (dn - nrm * _head_mean(dn * nrm))

        r64 = lax.broadcasted_iota(jnp.int32, (CHUNK, CHUNK), 0)
        c64 = lax.broadcasted_iota(jnp.int32, (CHUNK, CHUNK), 1)
        tril = r64 >= c64
        tri = tril.astype(BF16)
        triu = (r64 <= c64).astype(BF16)
        groups = [slice(g * HGRN_LANES, (g + 1) * HGRN_LANES) for g in range(HGRN_W // HGRN_LANES)]
        dst = [dst_scr[:, ls] for ls in groups]
        dlb = [dlb_scr[:, ls] for ls in groups]
        for cc in reversed(range(cpb)):
            sl = slice(cc * CHUNK, (cc + 1) * CHUNK)
            for gi, ls in enumerate(groups):
                lbg = lb[:, ls]
                qp, sq, qf, sg, f, kf, v, b, bm, bl, qt, kt = _hgrn_chunk(q_ref, f_ref, i_ref, sl, lbg, tri, ls)
                stf = st_ref[cc, :, ls].astype(F32)
                st_end = (st_ref[cc + 1, :, ls] if cc + 1 < cpb else stn_ref[0, :, ls]).astype(F32)
                csum = jnp.sum(st_end * dst[gi], axis=0, keepdims=True)
                doc = do_scr[sl, ls]
                dob, dstb = doc.astype(BF16), dst[gi].astype(BF16)
                eb = jnp.exp(b)
                qe = (qf * eb).astype(BF16)
                kh = (kf * jnp.exp(bl - b)).astype(BF16)
                qtb, ktb = qt.astype(BF16), kt.astype(BF16)
                parts = []
                for h in range(HGRN_LANES // BLK):
                    hs = slice(h * BLK, (h + 1) * BLK)
                    da = jnp.where(tril, _dot3(doc[:, hs], v[:, hs], NT), 0.0)
                    a = jnp.where(tril, _dot(qtb[:, hs], ktb[:, hs], NT), 0.0).astype(BF16)
                    parts.append((
                        _dot3(da, kt[:, hs], NN), _dot3(doc[:, hs], stf[:, hs], NN),
                        _dot3(da, qt[:, hs], TN), _dot3(v[:, hs], dst[gi][:, hs], NN),
                        _dot(a, dob[:, hs], TN) + _dot(kh[:, hs], dstb[:, hs], NT),
                        _dot(dob[:, hs], qe[:, hs], TN)))
                dqt, dqi, dkt, dks, dv, upd = (jnp.concatenate([p[n] for p in parts], axis=1) for n in range(6))
                dqf = dqt * jnp.exp(b - bm) + eb * dqi
                dkf = dkt * jnp.exp(bm - b) + jnp.exp(bl - b) * dks
                gq = qf * dqf - kf * dkf
                dlogf = csum + _cumsum_rows(triu, gq)
                dfv = dlogf / f - dkf
                dq_ref[sl, ls] = (dqf * (sq * (1.0 + qp * (1.0 - sq)))).astype(BF16)
                df_ref[sl, ls] = (dfv * (1.0 - lbg) * sg * (1.0 - sg)).astype(BF16)
                di_ref[sl, ls] = dv.astype(BF16)
                dst[gi] = dst[gi] * jnp.exp(bl) + upd
                dlb[gi] = dlb[gi] + _colsum(dfv * (1.0 - sg))
        for gi, ls in enumerate(groups):
            dst_scr[:, ls] = dst[gi]
            dlb_scr[:, ls] = dlb[gi]

        @pl.when(step == nb - 1)
        def _():
            t = jnp.concatenate(dlb, axis=1) * lb * (1.0 - lb)
            dlbl_ref[...] = jnp.concatenate([t, -t], axis=0)

    rev = lambda i: nb - 1 - i
    row = pl.BlockSpec((tb, HGRN_W), lambda i: (rev(i), 0))
    res = pl.pallas_call(
        body, name="hgrn_bwd", grid=(nb,),
        in_specs=_hgrn_specs(tb, rev) + [
            row, pl.BlockSpec((cpb, BLK, HGRN_W), lambda i: (rev(i), 0, 0)),
            pl.BlockSpec((1, BLK, HGRN_W), lambda i: (jnp.minimum((rev(i) + 1) * cpb, nc - 1), 0, 0)),
            row, pl.BlockSpec((2, HGRN_W), lambda i: (0, 0)), pl.BlockSpec((1, HGRN_W), lambda i: (0, 0))],
        out_specs=[row, row, row, row, pl.BlockSpec((1, HGRN_W), lambda i: (0, 0)),
                   pl.BlockSpec((2, HGRN_W), lambda i: (0, 0))],
        out_shape=[_sds((s, HGRN_W), BF16)] * 4 + [_sds((1, HGRN_W), F32), _sds((2, HGRN_W), F32)],
        scratch_shapes=[pltpu.VMEM((tb, HGRN_W), F32), pltpu.VMEM((BLK, HGRN_W), F32), pltpu.VMEM((1, HGRN_W), F32)],
        compiler_params=_cp(("arbitrary",)),
    )(proj, proj, proj, proj, o_pre, states, states, drec, lb_logits, out_gain)
    return res


def _place():
    return lax.axis_index("x"), lax.axis_index("y"), lax.axis_index("c")


def _flip(x, y, ox, oy):
    return (1 - x if ox else x), (1 - y if oy else y)


def _half(rows, cc):
    return pl.ds(cc * (rows // 2), rows // 2)


def _remote(src, dst, ssem, rsem, to):
    return pltpu.make_async_remote_copy(src_ref=src, dst_ref=dst, send_sem=ssem, recv_sem=rsem,
                                        device_id=to, device_id_type=MESH)


def _ag_chip_copies(ins, outs, ssem, rsem):
    x, y, c = _place()
    j = 2 * x + y
    cps = []
    for k in range(len(ins)):
        rows = ins[k].shape[0]
        for idx, (ox, oy) in enumerate(FLIPS):
            px, py = _flip(x, y, ox, oy)
            cps.append(_remote(ins[k].at[_half(rows, c)], outs[k].at[j, _half(rows, c)],
                               ssem.at[k, idx], rsem.at[k, idx], (px, py, c)))
    return cps


def _ag_start(ins, outs, ssem, rsem):
    for cp in _ag_chip_copies(ins, outs, ssem, rsem):
        cp.start()


def _ag_finish(ins, outs, ssem, rsem):
    x, y, c = _place()
    sib = (x, y, 1 - c)
    passed = []
    for k in range(len(ins)):
        rows = ins[k].shape[0]
        for idx, (ox, oy) in enumerate(FLIPS):
            px, py = _flip(x, y, ox, oy)
            blk = outs[k].at[2 * px + py, _half(rows, c)]
            _remote(blk, blk, ssem.at[k, idx], rsem.at[k, idx], (px, py, c)).wait_recv()
            cp = _remote(blk, blk, ssem.at[k, 3 + idx], rsem.at[k, 3 + idx], sib)
            cp.start()
            passed.append(cp)
    for k in range(len(ins)):
        rows = ins[k].shape[0]
        for idx, (ox, oy) in enumerate(FLIPS):
            px, py = _flip(x, y, ox, oy)
            blk = outs[k].at[2 * px + py, _half(rows, 1 - c)]
            _remote(blk, blk, ssem.at[k, 3 + idx], rsem.at[k, 3 + idx], sib).wait_recv()
    for cp in _ag_chip_copies(ins, outs, ssem, rsem) + passed:
        cp.wait_send()


def _ag_shapes(shards):
    nk = len(shards)
    return ([_sds((N_CHIPS,) + tuple(w.shape), w.dtype) for w in shards],
            [pltpu.SemaphoreType.DMA((nk, 6)), pltpu.SemaphoreType.DMA((nk, 6))])


def _with_own(gathered, shard, j):
    return lax.dynamic_update_index_in_dim(gathered, shard, j, 0)


def _rs_pair_copies(ins, outs, ssem, rsem):
    x, y, c = _place()
    return [_remote(ins[k].at[:, _half(ins[k].shape[1], 1 - c)], outs[k], ssem.at[k], rsem.at[k], (x, y, 1 - c))
            for k in range(len(ins))]


def _rs_pair_start(ins, outs, ssem, rsem):
    for cp in _rs_pair_copies(ins, outs, ssem, rsem):
        cp.start()


def _rs_pair_finish(ins, outs, ssem, rsem):
    for cp in _rs_pair_copies(ins, outs, ssem, rsem):
        cp.wait()


def _rs_pair_exchange(grads):
    nk = len(grads)
    return (grads, [_sds((N_CHIPS, g.shape[1] // 2, g.shape[2]), g.dtype) for g in grads],
            [pltpu.SemaphoreType.DMA((nk,)), pltpu.SemaphoreType.DMA((nk,))], _rs_pair_start, _rs_pair_finish)


def _rs_pair(name, grads):
    nk = len(grads)
    ins, out_shape, sems, start, finish = _rs_pair_exchange(grads)

    def body(*refs):
        start(refs[:nk], refs[nk:2 * nk], *refs[2 * nk:])
        finish(refs[:nk], refs[nk:2 * nk], *refs[2 * nk:])

    return pl.pallas_call(body, name=name, in_specs=[ANY] * nk, out_specs=[ANY] * nk, out_shape=out_shape,
                          scratch_shapes=sems)(*ins)


def _rs_chip_copies(ins, outs, ssem, rsem):
    x, y, c = _place()
    cps = []
    for k in range(len(ins)):
        for idx, (ox, oy) in enumerate(FLIPS):
            px, py = _flip(x, y, ox, oy)
            cps.append(_remote(ins[k].at[2 * px + py], outs[k].at[idx], ssem.at[k, idx], rsem.at[k, idx], (px, py, c)))
    return cps


def _rs_chips_start(ins, outs, ssem, rsem):
    for cp in _rs_chip_copies(ins, outs, ssem, rsem):
        cp.start()


def _rs_chips_finish(ins, outs, ssem, rsem):
    for cp in _rs_chip_copies(ins, outs, ssem, rsem):
        cp.wait()


def _rs_chips_shapes(psums):
    nk = len(psums)
    return ([_sds((3,) + tuple(p.shape[1:]), p.dtype) for p in psums],
            [pltpu.SemaphoreType.DMA((nk, 3)), pltpu.SemaphoreType.DMA((nk, 3))])


def _rs_share(fulls):
    nk = len(fulls)

    def body(*refs):
        ins, outs = refs[:nk], refs[nk:2 * nk]
        ssem, rsem = refs[2 * nk:]
        x, y, c = _place()
        cps = []
        for k in range(nk):
            rows = fulls[k].shape[0]
            cp = _remote(ins[k].at[_half(rows, c)], outs[k].at[_half(rows, c)], ssem.at[k], rsem.at[k], (x, y, 1 - c))
            cp.start()
            cps.append(cp)
        for k, cp in enumerate(cps):
            rows = fulls[k].shape[0]
            cp.wait_send()
            theirs = outs[k].at[_half(rows, 1 - c)]
            _remote(theirs, theirs, ssem.at[k], rsem.at[k], (x, y, 1 - c)).wait_recv()

    return pl.pallas_call(
        body, name="rs_share", in_specs=[ANY] * nk, out_specs=[ANY] * nk,
        out_shape=[_sds(f.shape, f.dtype) for f in fulls], input_output_aliases={k: k for k in range(nk)},
        scratch_shapes=[pltpu.SemaphoreType.DMA((nk,)), pltpu.SemaphoreType.DMA((nk,))],
    )(*fulls)


def _allreduce_small(v):
    ndev = 8

    def body(in_ref, out_ref, buf, ssem, rsem):
        x, y, c = _place()
        me = 4 * x + 2 * y + c
        buf[me] = in_ref[...]
        cps = []
        for k in range(1, ndev):
            ox, oy, oc = (k >> 2) & 1, (k >> 1) & 1, k & 1
            px, py = _flip(x, y, ox, oy)
            pc = 1 - c if oc else c
            cp = pltpu.make_async_remote_copy(src_ref=in_ref, dst_ref=buf.at[me], send_sem=ssem.at[k - 1],
                                              recv_sem=rsem.at[k - 1], device_id=(px, py, pc), device_id_type=MESH)
            cp.start()
            cps.append((cp, 4 * px + 2 * py + pc, (px, py, pc)))
        for k, (cp, src, peer) in enumerate(cps):
            cp.wait_send()
            pltpu.make_async_remote_copy(src_ref=in_ref, dst_ref=buf.at[src], send_sem=ssem.at[k],
                                         recv_sem=rsem.at[k], device_id=peer, device_id_type=MESH).wait_recv()
        acc = buf[0]
        for i in range(1, ndev):
            acc = acc + buf[i]
        out_ref[...] = acc

    return pl.pallas_call(
        body, name="allreduce_small",
        in_specs=[pl.BlockSpec(memory_space=pltpu.VMEM)], out_specs=pl.BlockSpec(memory_space=pltpu.VMEM),
        out_shape=_sds(v.shape, v.dtype),
        scratch_shapes=[pltpu.VMEM((ndev,) + v.shape, v.dtype), pltpu.SemaphoreType.DMA((ndev - 1,)),
                        pltpu.SemaphoreType.DMA((ndev - 1,))],
    )(v)


def _rs_sum1(name, g, recv, jc_idx):
    _, r, cdim = g.shape
    hr = r // 2
    tr = min(hr, 256)
    nr = hr // tr

    def body(jc_ref, g_ref, r_ref, o32_ref, o16_ref):
        v = g_ref[...] + r_ref[...].astype(F32)
        o16_ref[...] = v.astype(BF16)

        @pl.when(pl.program_id(1) == jc_ref[0])
        def _():
            o32_ref[...] = v

    spec = pl.BlockSpec((None, tr, cdim), lambda i, j, jc: (j, i, 0))
    return pl.pallas_call(
        body, name=name,
        grid_spec=pltpu.PrefetchScalarGridSpec(
            num_scalar_prefetch=1, grid=(nr, N_CHIPS),
            in_specs=[pl.BlockSpec((None, tr, cdim), lambda i, j, jc: (j, jc[1] * nr + i, 0)), spec],
            out_specs=[pl.BlockSpec((tr, cdim), lambda i, j, jc: (i, 0)), spec]),
        out_shape=[_sds((hr, cdim), F32), _sds((N_CHIPS, hr, cdim), BF16)],
        compiler_params=_cp(("parallel", "arbitrary")),
    )(jc_idx, g, recv)


def _rs_sum2(name, p32, recv, jc_idx):
    hr, cdim = p32.shape
    tr = min(hr, 256)
    nr = hr // tr

    def body(jc_ref, p_ref, r_ref, o_ref):
        o_ref[...] = ((p_ref[...] + r_ref[0].astype(F32)) + r_ref[1].astype(F32)) + r_ref[2].astype(F32)

    return pl.pallas_call(
        body, name=name,
        grid_spec=pltpu.PrefetchScalarGridSpec(
            num_scalar_prefetch=1, grid=(nr,),
            in_specs=[pl.BlockSpec((tr, cdim), lambda i, jc: (i, 0)),
                      pl.BlockSpec((3, tr, cdim), lambda i, jc: (0, i, 0))],
            out_specs=pl.BlockSpec((tr, cdim), lambda i, jc: (jc[1] * nr + i, 0))),
        out_shape=_sds((2 * hr, cdim), F32),
        compiler_params=_cp(("parallel",)),
    )(jc_idx, p32, recv)


def _adamw(name, w, g, m, v):
    r, cdim = w.shape
    tr = min(r, 256)
    c1 = 1.0 - ADAM_B1 ** ADAM_STEP
    c2 = 1.0 - ADAM_B2 ** ADAM_STEP

    def body(w_ref, g_ref, m_ref, v_ref, d_ref, nm_ref, nv_ref):
        gv = g_ref[...]
        nm = ADAM_B1 * m_ref[...] + (1.0 - ADAM_B1) * gv
        nv = ADAM_B2 * v_ref[...] + (1.0 - ADAM_B2) * (gv * gv)
        d_ref[...] = -ADAM_LR * ((nm / c1) / (jnp.sqrt(nv / c2) + ADAM_EPS) + ADAM_WD * w_ref[...])
        nm_ref[...] = nm
        nv_ref[...] = nv

    spec = pl.BlockSpec((tr, cdim), lambda i: (i, 0))
    return pl.pallas_call(
        body, name=name, grid=(r // tr,), in_specs=[spec] * 4, out_specs=[spec] * 3,
        out_shape=[_sds((r, cdim), F32)] * 3, compiler_params=_cp(("parallel",)),
    )(w, g, m, v)


def _pack_small(mix_pre, attn_out, lb_logits, hgrn_out, mix_post, mlp_pre, mlp_post, extra=None):
    spare = jnp.zeros((1, D_MODEL), F32)
    rows = [mix_pre, jnp.concatenate([attn_out, hgrn_out], axis=1),
            jnp.concatenate([lb_logits[0:1], lb_logits[1:2]], axis=1), mix_post, mlp_pre, mlp_post,
            spare if extra is None else extra, spare]
    return jnp.concatenate(rows, axis=0)


def _unpack_small(p):
    return (p[0:1], p[1:2, :ATTN_W], jnp.concatenate([p[2:3, :HGRN_W], p[2:3, HGRN_W:]], axis=0),
            p[1:2, ATTN_W:], p[3:4], p[4:5], p[5:6])


def kernel(x, mix_pre_norm, w_in, attn_out_norm, hgrn_lb_logits, hgrn_out_norm, w_out, mix_post_norm, mlp_pre_norm, w_ff1, w_ff2, mlp_post_norm, loss_target, m_mix_pre_norm, m_w_in, m_attn_out_norm, m_hgrn_lb_logits, m_hgrn_out_norm, m_w_out, m_mix_post_norm, m_mlp_pre_norm, m_w_ff1, m_w_ff2, m_mlp_post_norm, v_mix_pre_norm, v_w_in, v_attn_out_norm, v_hgrn_lb_logits, v_hgrn_out_norm, v_w_out, v_mix_post_norm, v_mlp_pre_norm, v_w_ff1, v_w_ff2, v_mlp_post_norm):
    s = x.shape[1]
    xs = x.reshape(s, D_MODEL)
    tgt = loss_target.reshape(s, D_MODEL)
    cx, cy, cc = _place()
    chip = 2 * cx + cy
    jc_idx = jnp.stack([chip, cc]).astype(jnp.int32)

    big_w = [w_in[0], w_out[0], w_ff1[0], w_ff2[0]]
    big_m = [m_w_in[0], m_w_out[0], m_w_ff1[0], m_w_ff2[0]]
    big_v = [v_w_in[0], v_w_out[0], v_w_ff1[0], v_w_ff2[0]]
    w_in_bf = big_w[0].astype(BF16)

    h, *rest_bf, wg_in = _rows_call(
        "norm_in", lambda xv, g, *ws: (xv * _rstd(xv) * g, *ws),
        [(xs, _row(D_MODEL)), (mix_pre_norm, "full")] + [(w, _row(D_MODEL)) for w in big_w[1:]],
        [(D_MODEL, BF16, "row")] + [(D_MODEL, BF16, "row", w.shape[0]) for w in big_w[1:]], s,
        exchange=([w_in_bf], *_ag_shapes([w_in_bf]), _ag_start, _ag_finish))
    shards = [w_in_bf, *rest_bf]
    wg_in = _with_own(wg_in, w_in_bf, chip)
    (proj,) = _mm_cols("mm_proj", h, wg_in, NN, [F32])
    hg_o, rec, states = _hgrn_fwd(proj, hgrn_lb_logits, hgrn_out_norm)
    attn_o, attn_lse, wg_out, wg_1, wg_2 = _attn_fwd(proj, shards[1:])
    wg_out, wg_1, wg_2 = (_with_own(g, w, chip) for g, w in zip((wg_out, wg_1, wg_2), shards[1:]))
    (attn_n,) = _rows_call("attn_norm", lambda o, gain: (o * _rstd(o) * gain,),
                           [(attn_o, _row(ATTN_W)), (attn_out_norm, "full")], [(ATTN_W, BF16, "row")], s)
    cat = jnp.concatenate([attn_n, rec], axis=1)

    def post1(mv, xv, g_post, g_pre2):
        x1 = xv + mv * _rstd(mv) * g_post
        return mv, x1, x1 * _rstd(x1) * g_pre2

    mixed, x1, h2 = _rows_call(
        "mm_mixed", post1, [(xs, _row(D_MODEL)), (mix_post_norm, "full"), (mlp_pre_norm, "full")],
        [(D_MODEL, F32, "row"), (D_MODEL, F32, "row"), (D_MODEL, BF16, "row")], s,
        matmul=(cat, wg_out.reshape(D_MODEL, D_MODEL)))

    def sq_relu(u):
        r = jnp.maximum(u, 0.0)
        return r * r, r

    act, ru = _mm_cols("mm_ff1", h2, wg_1, NN, [BF16, BF16], epi=sq_relu)

    def post2(fv, x1v, tv, g):
        y = x1v + fv * _rstd(fv) * g
        dy = (y - tv) * (1.0 / D_MODEL)
        err = y - tv
        loss = 0.5 * jnp.sum(jnp.mean(err * err, axis=-1, keepdims=True), axis=0, keepdims=True)
        dff, dgc = _norm_bwd(fv, g, dy)
        return dy, dff, _colsum(dgc), jnp.broadcast_to(loss, (1, BLK))

    dy, dff, g_mlp_post, loss_part = _rows_call(
        "mm_ff2", post2, [(x1, _row(D_MODEL)), (tgt, _row(D_MODEL)), (mlp_post_norm, "full")],
        [(D_MODEL, F32, "row"), (D_MODEL, BF16, "row"), (D_MODEL, F32, "acc"), (BLK, F32, "acc")], s,
        matmul=(act, wg_2.reshape(D_FF, D_MODEL)))

    (du,) = _mm_cols("mm_du", dff, wg_2, NT, [BF16], epi=lambda acc, r: (acc * (2.0 * r.astype(F32)),),
                     extras=(ru,))
    gw_2 = _mm_wgrad("mm_gw2", act, dff, True)
    gw_1 = _mm_wgrad("mm_gw1", h2, du, False)

    def bwd_mid(dh2v, dyv, x1v, mv, g_pre2, g_post):
        d1, gc1 = _norm_bwd(x1v, g_pre2, dh2v)
        dx1 = dyv + d1
        dm, gc2 = _norm_bwd(mv, g_post, dx1)
        return dx1, dm, _colsum(gc1), _colsum(gc2)

    dx1, dmixed, g_mlp_pre, g_mix_post, *from_pair = _rows_call(
        "mm_dh2", bwd_mid, [(dy, _row(D_MODEL)), (x1, _row(D_MODEL)), (mixed, _row(D_MODEL)),
                            (mlp_pre_norm, "full"), (mix_post_norm, "full")],
        [(D_MODEL, F32, "row"), (D_MODEL, BF16, "row"), (D_MODEL, F32, "acc"), (D_MODEL, F32, "acc")], s,
        matmul=(du, wg_1), exchange=_rs_pair_exchange([gw_1[1], gw_2[1]]))

    def attn_norm_bwd(dc, o, gain):
        do, gc = _norm_bwd(o, gain, dc[:, :ATTN_W])
        t = do * o
        lane = lax.broadcasted_iota(jnp.int32, (t.shape[0], BLK), 1) < 64
        parts = []
        for p in range(ATTN_W // BLK):
            tp = t[:, p * BLK:(p + 1) * BLK]
            sa = jnp.sum(jnp.where(lane, tp, 0.0), axis=1, keepdims=True)
            sb = jnp.sum(jnp.where(lane, 0.0, tp), axis=1, keepdims=True)
            parts.append(jnp.where(lane, sa, sb))
        return do, jnp.concatenate(parts, axis=1), dc[:, ATTN_W:], _colsum(gc)

    gw_out = _mm_wgrad("mm_gwout", cat, dmixed, True)
    do_attn, delta, drec, g_attn_out, from_pair_out = _rows_call(
        "mm_dcat", attn_norm_bwd, [(attn_o, _row(ATTN_W)), (attn_out_norm, "full")],
        [(ATTN_W, F32, "row"), (ATTN_W, F32, "row"), (HGRN_W, F32, "row"), (ATTN_W, F32, "acc")], s,
        matmul=(dmixed, wg_out.reshape(1, D_MODEL, D_MODEL)), exchange=_rs_pair_exchange([gw_out[1]]))
    names = ["out", "ff1", "ff2", "in"]
    ready = [gw_out, gw_1, gw_2]
    from_pair = [from_pair_out] + from_pair
    pair =[_rs_sum1(f"rs_sum1_{n}", g[0], r, jc_idx) for n, g, r in zip(names, ready, from_pair)]

    dq, dk, dv, *from_chips = _attn_bwd(proj, do_attn, attn_lse, delta, [p[1] for p in pair])
    dhq, dhf, dhi, dhg, g_hgrn_out, g_lb = _hgrn_bwd(proj, hg_o, states, drec, hgrn_lb_logits, hgrn_out_norm)

    dproj = jnp.concatenate([dq, dk, dv, dhq, dhf, dhi, dhg], axis=1)
    gw_in = _mm_wgrad("mm_gwin", h, dproj, False)
    (from_pair_in,) = _rs_pair("rs_pair_in", [gw_in[1]])
    pair.append(_rs_sum1("rs_sum1_in", gw_in[0], from_pair_in, jc_idx))
    rs_shape, rs_sems = _rs_chips_shapes([pair[3][1]])

    def bwd_in(dhv, dx1v, xv, g):
        d0, gc = _norm_bwd(xv, g, dhv)
        return dx1v + d0, _colsum(gc)

    grad_x, g_mix_pre, from_chips_in = _rows_call(
        "mm_dh", bwd_in, [(dx1, _row(D_MODEL)), (xs, _row(D_MODEL)), (mix_pre_norm, "full")],
        [(D_MODEL, F32, "row"), (D_MODEL, F32, "acc")], s, matmul=(dproj, wg_in),
        exchange=([pair[3][1]], rs_shape, rs_sems, _rs_chips_start, _rs_chips_finish))
    from_chips.append(from_chips_in)

    loss_row = jnp.pad(loss_part, ((0, 0), (0, D_MODEL - BLK)))
    small_g = _allreduce_small(_pack_small(g_mix_pre, g_attn_out, g_lb, g_hgrn_out, g_mix_post, g_mlp_pre, g_mlp_post,
                                           extra=loss_row))
    loss = small_g[6, 0]

    reduced = [_rs_sum2(f"rs_sum2_{n}", p[0], r, jc_idx) for n, p, r in zip(names, pair, from_chips)]
    g_wout, g_w1, g_w2, g_win = _rs_share(reduced)
    full = [g_win, g_wout, g_w1, g_w2]

    upd = [_adamw(f"adamw_{n}", w, g, m, v) for n, w, g, m, v in zip(("in", "out", "ff1", "ff2"), big_w, full, big_m, big_v)]
    small_w = _pack_small(mix_pre_norm, attn_out_norm, hgrn_lb_logits, hgrn_out_norm, mix_post_norm, mlp_pre_norm,
                          mlp_post_norm)
    small_m = _pack_small(m_mix_pre_norm, m_attn_out_norm, m_hgrn_lb_logits, m_hgrn_out_norm, m_mix_post_norm,
                          m_mlp_pre_norm, m_mlp_post_norm)
    small_v = _pack_small(v_mix_pre_norm, v_attn_out_norm, v_hgrn_lb_logits, v_hgrn_out_norm, v_mix_post_norm,
                          v_mlp_pre_norm, v_mlp_post_norm)
    small_upd = _adamw("adamw_small", small_w, small_g, small_m, small_v)

    def assemble(small, big):
        sm = _unpack_small(small)
        return (sm[0], big[0][None], sm[1], sm[2], sm[3], big[1][None], sm[4], sm[5], big[2][None], big[3][None], sm[6])

    g_out = assemble(small_g, full)
    d_out = assemble(small_upd[0], [u[0] for u in upd])
    m_out = assemble(small_upd[1], [u[1] for u in upd])
    v_out = assemble(small_upd[2], [u[2] for u in upd])
    return (loss, grad_x.reshape(x.shape), *g_out, *d_out, *m_out, *v_out)
```
